```python
import jax
import jax.numpy as jnp
from jax import lax
import numpy as np

D_MODEL = 1024
BATCH = 8
SEQ = 2048
DEPTH = 1
DEC_BATCH = 128
DEC_SEQ = 8
PAST_LEN = 16384
PAGE_SIZE = 128

N_MEM = 256
MEM_HEADS = 4
MEM_DH = D_MODEL // MEM_HEADS
RET_HEADS = 4
RET_W = D_MODEL // 2
RET_DV = RET_W // RET_HEADS
RET_DK = RET_DV // 2
RET_QK = RET_HEADS * RET_DK
RET_CHUNK = 128
ROPE_BASE = 10000.0
RWKV_N = 64
RWKV_W = D_MODEL - RET_W
RWKV_HEADS = RWKV_W // RWKV_N
LORA_W = 64
LORA_A = 64
LORA_G = 128
LNX_EPS = 64e-5
N_RET_COLS = 2 * RET_QK + 2 * RET_W
N_RWKV_COLS = 3 * RWKV_W + LORA_W + LORA_A + LORA_G
N_IN_COLS = N_RET_COLS + N_RWKV_COLS
N_GROUPS = 4
EXP_PER_GROUP = 8
N_EXPERTS = N_GROUPS * EXP_PER_GROUP
TOP_K = 2
D_EXPERT = D_MODEL // 2
MOE_BLOCK = 128
EPS = 1e-6

kernel_name = 'hymba_retnet_rwkv7_hmoe_step'


def rmsnorm(x, g):
    xf = x.astype(jnp.float32)
    y = xf * lax.rsqrt(jnp.mean(xf * xf, axis=-1, keepdims=True) + EPS)
    return (y * g.astype(jnp.float32)).astype(x.dtype)


def rotary(x, pos):
    half = x.shape[-1] // 2
    inv_freq = ROPE_BASE ** (-jnp.arange(half, dtype=jnp.float32) / half)
    ang = pos.astype(jnp.float32)[:, None] * inv_freq[None, :]
    cos = jnp.cos(ang)[None, :, None, :]
    sin = jnp.sin(ang)[None, :, None, :]
    xf = x.astype(jnp.float32)
    x1, x2 = xf[..., :half], xf[..., half:]
    return jnp.concatenate([x1 * cos - x2 * sin, x2 * cos + x1 * sin], axis=-1)


def retention(q, k, v, s0):
    B, T, H, _ = q.shape
    C = RET_CHUNK if T % RET_CHUNK == 0 else T
    n = T // C
    lg = jnp.log1p(-jnp.exp2(-5.0 - jnp.arange(H, dtype=jnp.float32)))
    idx = jnp.arange(C, dtype=jnp.float32)
    diff = idx[:, None] - idx[None, :]
    mask = jnp.where(diff[None] >= 0, jnp.exp(jnp.maximum(diff, 0.0)[None] * lg[:, None, None]), 0.0)
    q_dec = jnp.exp((idx[:, None] + 1.0) * lg[None, :])
    k_dec = jnp.exp((C - 1.0 - idx)[:, None] * lg[None, :])
    c_dec = jnp.exp(C * lg)

    def to_chunks(t):
        return t.astype(jnp.float32).reshape(B, n, C, H, t.shape[-1]).transpose(1, 0, 2, 3, 4)

    def step(S, inp):
        qc, kc, vc = inp
        att = jnp.einsum('bihd,bjhd->bhij', qc, kc) * mask[None]
        o = (jnp.einsum('bhij,bjhe->bihe', att, vc)
             + jnp.einsum('bihd,bhde->bihe', qc, S) * q_dec[None, :, :, None])
        S = S * c_dec[None, :, None, None] + jnp.einsum('bjhd,bjhe->bhde', kc * k_dec[None, :, :, None], vc)
        return S, o

    S, o = lax.scan(step, s0.astype(jnp.float32), (to_chunks(q), to_chunks(k), to_chunks(v)))
    return o.transpose(1, 0, 2, 3, 4).reshape(B, T, H, -1), S


def rwkv7_scan(r, w, k, v, a, b, s0):
    def step(S, inp):
        rt, wt, kt, vt, at, bt = inp
        sa = jnp.einsum('bhvk,bhk->bhv', S, at)
        S = S * wt[:, :, None, :] + sa[..., None] * bt[:, :, None, :] + vt[..., None] * kt[:, :, None, :]
        y = jnp.einsum('bhvk,bhk->bhv', S, rt)
        return S, y

    xs = tuple(t.transpose(1, 0, 2, 3) for t in (r, w, k, v, a, b))
    S, y = lax.scan(step, s0, xs)
    return y.transpose(1, 0, 2, 3), S


def token_mixers(xn, pos, s_ret, s_rwkv, s_shift, w_in, ret_gn, rwkv_mu, rwkv_w0, rwkv_w2, rwkv_a0,
                 rwkv_a2, rwkv_g2, rwkv_k_k, rwkv_k_a, rwkv_r_k, rwkv_lnx_w, rwkv_lnx_b, w_out):
    B, T, _ = xn.shape
    f32 = jnp.float32
    proj = xn @ w_in
    ret_p, rw_p = proj[..., :N_RET_COLS], proj[..., N_RET_COLS:]
    q, k, v, gate = jnp.split(ret_p, [RET_QK, 2 * RET_QK, 2 * RET_QK + RET_W], axis=-1)
    q = rotary(q.reshape(B, T, RET_HEADS, RET_DK), pos)
    k = rotary(k.reshape(B, T, RET_HEADS, RET_DK), pos) * (RET_DK ** -0.5)
    v = v.reshape(B, T, RET_HEADS, RET_DV)
    o_ret, s_ret_new = retention(q, k, v, s_ret)
    o_ret = o_ret * lax.rsqrt(jnp.mean(o_ret * o_ret, axis=-1, keepdims=True) + EPS)
    o_ret = o_ret.reshape(B, T, RET_W) * ret_gn.astype(f32) * jax.nn.silu(gate.astype(f32))
    prev = jnp.concatenate([s_shift[:, None, :].astype(rw_p.dtype), rw_p[:, :-1]], axis=1)
    rw_s = (rw_p + (prev - rw_p) * rwkv_mu).astype(f32)
    s_shift_new = rw_p[:, -1]
    r, kr, vr, hw, ha, hg = jnp.split(
        rw_s, [RWKV_W, 2 * RWKV_W, 3 * RWKV_W, 3 * RWKV_W + LORA_W, 3 * RWKV_W + LORA_W + LORA_A], axis=-1)
    wlog = -jax.nn.softplus(-(rwkv_w0 + jnp.tanh(hw) @ rwkv_w2)) - 0.5
    decay = jnp.exp(-jnp.exp(wlog))
    a = jax.nn.sigmoid(rwkv_a0 + ha @ rwkv_a2)
    g = jax.nn.sigmoid(hg) @ rwkv_g2

    def heads(t):
        return t.reshape(B, T, RWKV_HEADS, RWKV_N)

    kk = heads(kr * rwkv_k_k)
    kk = kk / jnp.maximum(jnp.sqrt(jnp.sum(kk * kk, axis=-1, keepdims=True)), 1e-12)
    kr = kr * (1.0 + (a - 1.0) * rwkv_k_a)
    r4, k4, v4, a4 = heads(r), heads(kr), heads(vr), heads(a)
    y, s_rwkv_new = rwkv7_scan(r4, heads(decay), k4, v4, -kk, kk * a4, s_rwkv.astype(f32))
    mean = jnp.mean(y, axis=-1, keepdims=True)
    var = jnp.mean((y - mean) ** 2, axis=-1, keepdims=True)
    y = ((y - mean) * lax.rsqrt(var + LNX_EPS)).reshape(B, T, RWKV_W) * rwkv_lnx_w + rwkv_lnx_b
    bonus = jnp.sum(r4 * k4 * rwkv_r_k, axis=-1, keepdims=True) * v4
    y = (y + bonus.reshape(B, T, RWKV_W)) * g
    o = jnp.concatenate([o_ret, y], axis=-1).astype(xn.dtype) @ w_out
    return (o.astype(xn.dtype), s_ret_new.astype(s_ret.dtype), s_rwkv_new.astype(s_rwkv.dtype),
            s_shift_new.astype(s_shift.dtype))


def memory_kv(mem, g_mem_kv, w_mk, w_mv):
    B, M, _ = mem.shape
    mn = rmsnorm(mem, g_mem_kv)
    return (mn @ w_mk).reshape(B, M, MEM_HEADS, MEM_DH), (mn @ w_mv).reshape(B, M, MEM_HEADS, MEM_DH)


def memory_attn(hn, mem_k, mem_v, w_mq, w_mo):
    B, T, _ = hn.shape
    f32 = jnp.float32
    q = (hn @ w_mq).reshape(B, T, MEM_HEADS, MEM_DH)
    s = jnp.einsum('bthd,bmhd->bhtm', q.astype(f32), mem_k.astype(f32)) * (MEM_DH ** -0.5)
    p = jax.nn.softmax(s, axis=-1)
    o = jnp.einsum('bhtm,bmhd->bthd', p, mem_v.astype(f32)).reshape(B, T, MEM_HEADS * MEM_DH)
    return o.astype(hn.dtype) @ w_mo


def hier_route(xf, w_gr, b_gr, w_er, b_er):
    f32 = jnp.float32
    x32 = xf.astype(f32)
    gl = x32 @ w_gr.astype(f32) + b_gr.astype(f32)
    pg = jax.nn.softmax(gl, axis=-1)
    gsel = jnp.argmax(gl, axis=-1)
    el = (x32 @ w_er.astype(f32) + b_er.astype(f32)).reshape(-1, N_GROUPS, EXP_PER_GROUP)
    el = jnp.take_along_axis(el, gsel[:, None, None], axis=1)[:, 0]
    top_p, top_i = lax.top_k(jax.nn.softmax(el, axis=-1), TOP_K)
    pg_sel = jnp.take_along_axis(pg, gsel[:, None], axis=1)
    comb = pg_sel * top_p / jnp.sum(top_p, axis=-1, keepdims=True)
    eid = (gsel[:, None] * EXP_PER_GROUP + top_i).astype(jnp.int32)
    return eid, comb


def routed_experts(xf, eid, comb, w_e_gate, w_e_up, w_e_down):
    T, D = xf.shape
    A = T * TOP_K
    blk = MOE_BLOCK
    n_blocks = -(-(A + N_EXPERTS * (blk - 1)) // blk)
    P = n_blocks * blk
    flat_e = eid.reshape(A)
    flat_t = jnp.arange(A, dtype=jnp.int32) // TOP_K
    order = jnp.argsort(flat_e)
    se, st = flat_e[order], flat_t[order]
    counts = jnp.bincount(flat_e, length=N_EXPERTS)
    starts = jnp.cumsum(counts) - counts
    pcounts = (counts + blk - 1) // blk * blk
    pends = jnp.cumsum(pcounts)
    pstarts = pends - pcounts
    dest = pstarts[se] + jnp.arange(A, dtype=jnp.int32) - starts[se]
    buf_tok = jnp.zeros((P,), jnp.int32).at[dest].set(st)
    blk_e = jnp.minimum(jnp.searchsorted(pends, jnp.arange(n_blocks) * blk, side='right'), N_EXPERTS - 1)
    xb = xf[buf_tok].reshape(n_blocks, blk, D)

    def expert_block(args):
        xblk, e = args
        h = jax.nn.silu(xblk @ w_e_gate[e]) * (xblk @ w_e_up[e])
        return h @ w_e_down[e]

    yb = lax.map(expert_block, (xb, blk_e)).reshape(P, D)
    contrib = yb[dest] * comb.reshape(A)[order][:, None].astype(yb.dtype)
    return jax.ops.segment_sum(contrib, st, num_segments=T)


def decoder_layer(x, pos, mem_k, mem_v, s_ret, s_rwkv, s_shift, g_mix, w_in, ret_gn, rwkv_mu, rwkv_w0,
                  rwkv_w2, rwkv_a0, rwkv_a2, rwkv_g2, rwkv_k_k, rwkv_k_a, rwkv_r_k, rwkv_lnx_w, rwkv_lnx_b,
                  w_out, g_mem_q, w_mq, w_mo, g_ffn, w_gr, b_gr, w_er, b_er, w_e_gate, w_e_up, w_e_down):
    B, T, D = x.shape
    mix, s_ret_new, s_rwkv_new, s_shift_new = token_mixers(
        rmsnorm(x, g_mix), pos, s_ret, s_rwkv, s_shift, w_in, ret_gn, rwkv_mu, rwkv_w0, rwkv_w2, rwkv_a0,
        rwkv_a2, rwkv_g2, rwkv_k_k, rwkv_k_a, rwkv_r_k, rwkv_lnx_w, rwkv_lnx_b, w_out)
    h = x + mix
    h = h + memory_attn(rmsnorm(h, g_mem_q), mem_k, mem_v, w_mq, w_mo).astype(h.dtype)
    hn = rmsnorm(h, g_ffn).reshape(B * T, D)
    eid, comb = hier_route(hn, w_gr, b_gr, w_er, b_er)
    h = h + routed_experts(hn, eid, comb, w_e_gate, w_e_up, w_e_down).reshape(B, T, D).astype(h.dtype)
    return h, s_ret_new, s_rwkv_new, s_shift_new


def setup_inputs(seed: int = 0) -> dict:
    key = jax.random.key(seed)
    ks = iter(jax.random.split(key, 64))
    L, D = DEPTH, D_MODEL

    def nrm(shape, scale):
        return scale * jax.random.normal(next(ks), shape, jnp.float32)

    def gain(shape):
        return 1.0 + 0.02 * jax.random.normal(next(ks), shape, jnp.float32)

    return {
        'x_prompt': nrm((BATCH, SEQ, D), 1.0),
        'x_sample': nrm((DEC_BATCH, DEC_SEQ, D), 1.0),
        'mem_prompt': nrm((BATCH, N_MEM, D), 1.0),
        'state_ret': nrm((L, DEC_BATCH, RET_HEADS, RET_DK, RET_DV), 0.1),
        'state_rwkv': nrm((L, DEC_BATCH, RWKV_HEADS, RWKV_N, RWKV_N), 0.1),
        'state_shift': nrm((L, DEC_BATCH, N_RWKV_COLS), 1.0),
        'cache_mem_k': nrm((L, DEC_BATCH, N_MEM, MEM_HEADS, MEM_DH), 1.0),
        'cache_mem_v': nrm((L, DEC_BATCH, N_MEM, MEM_HEADS, MEM_DH), 1.0),
        'g_mix': gain((L, D)),
        'w_in': nrm((L, D, N_IN_COLS), D ** -0.5),
        'ret_gn': gain((L, RET_W)),
        'rwkv_mu': jax.random.uniform(next(ks), (L, N_RWKV_COLS), jnp.float32),
        'rwkv_w0': -1.0 + nrm((L, RWKV_W), 0.5),
        'rwkv_w2': nrm((L, LORA_W, RWKV_W), 0.5 * LORA_W ** -0.5),
        'rwkv_a0': nrm((L, RWKV_W), 0.1),
        'rwkv_a2': nrm((L, LORA_A, RWKV_W), 0.5 * LORA_A ** -0.5),
        'rwkv_g2': nrm((L, LORA_G, RWKV_W), LORA_G ** -0.5),
        'rwkv_k_k': 0.85 + nrm((L, RWKV_W), 0.02),
        'rwkv_k_a': gain((L, RWKV_W)),
        'rwkv_r_k': nrm((L, RWKV_HEADS, RWKV_N), 0.1),
        'rwkv_lnx_w': gain((L, RWKV_W)),
        'rwkv_lnx_b': nrm((L, RWKV_W), 0.02),
        'w_out': nrm((L, D, D), D ** -0.5),
        'g_mem_q': gain((L, D)),
        'g_mem_kv': gain((L, D)),
        'w_mq': nrm((L, D, MEM_HEADS * MEM_DH), D ** -0.5),
        'w_mk': nrm((L, D, MEM_HEADS * MEM_DH), D ** -0.5),
        'w_mv': nrm((L, D, MEM_HEADS * MEM_DH), D ** -0.5),
        'w_mo': nrm((L, MEM_HEADS * MEM_DH, D), D ** -0.5),
        'g_ffn': gain((L, D)),
        'w_group_router': nrm((L, D, N_GROUPS), D ** -0.5),
        'b_group_router': nrm((L, N_GROUPS), 0.01),
        'w_expert_router': nrm((L, D, N_EXPERTS), D ** -0.5),
        'b_expert_router': nrm((L, N_EXPERTS), 0.01),
        'w_e_gate': nrm((L, N_EXPERTS, D, D_EXPERT), D ** -0.5),
        'w_e_up': nrm((L, N_EXPERTS, D, D_EXPERT), D ** -0.5),
        'w_e_down': nrm((L, N_EXPERTS, D_EXPERT, D), D_EXPERT ** -0.5),
        'g_final': gain((D,)),
    }


def reference(x_prompt, x_sample, mem_prompt, state_ret, state_rwkv, state_shift, cache_mem_k, cache_mem_v,
              g_mix, w_in, ret_gn, rwkv_mu, rwkv_w0, rwkv_w2, rwkv_a0, rwkv_a2, rwkv_g2, rwkv_k_k, rwkv_k_a,
              rwkv_r_k, rwkv_lnx_w, rwkv_lnx_b, w_out, g_mem_q, g_mem_kv, w_mq, w_mk, w_mv, w_mo, g_ffn,
              w_group_router, b_group_router, w_expert_router, b_expert_router, w_e_gate, w_e_up, w_e_down,
              g_final):
    Bp, Tp, _ = x_prompt.shape
    Ts = x_sample.shape[1]
    pos_p = jnp.arange(Tp, dtype=jnp.int32)
    pos_s = PAST_LEN + jnp.arange(Ts, dtype=jnp.int32)
    hp, hs = x_prompt, x_sample
    sr_p, sw_p, ss_p, mk_p, mv_p = [], [], [], [], []
    sr_s, sw_s, ss_s = [], [], []
    for l in range(DEPTH):
        lw = (g_mix[l], w_in[l], ret_gn[l], rwkv_mu[l], rwkv_w0[l], rwkv_w2[l], rwkv_a0[l], rwkv_a2[l],
              rwkv_g2[l], rwkv_k_k[l], rwkv_k_a[l], rwkv_r_k[l], rwkv_lnx_w[l], rwkv_lnx_b[l], w_out[l],
              g_mem_q[l], w_mq[l], w_mo[l], g_ffn[l], w_group_router[l], b_group_router[l],
              w_expert_router[l], b_expert_router[l], w_e_gate[l], w_e_up[l], w_e_down[l])
        mk, mv = memory_kv(mem_prompt, g_mem_kv[l], w_mk[l], w_mv[l])
        z_ret = jnp.zeros((Bp, RET_HEADS, RET_DK, RET_DV), jnp.float32)
        z_rwkv = jnp.zeros((Bp, RWKV_HEADS, RWKV_N, RWKV_N), jnp.float32)
        z_shift = jnp.zeros((Bp, N_RWKV_COLS), x_prompt.dtype)
        hp, a1, a2, a3 = decoder_layer(hp, pos_p, mk, mv, z_ret, z_rwkv, z_shift, *lw)
        sr_p.append(a1); sw_p.append(a2); ss_p.append(a3); mk_p.append(mk); mv_p.append(mv)
        hs, b1, b2, b3 = decoder_layer(hs, pos_s, cache_mem_k[l], cache_mem_v[l], state_ret[l],
                                       state_rwkv[l], state_shift[l], *lw)
        sr_s.append(b1); sw_s.append(b2); ss_s.append(b3)
    y_prompt = rmsnorm(hp, g_final)
    y_sample = rmsnorm(hs, g_final)
    return (y_prompt, y_sample, jnp.stack(sr_p), jnp.stack(sw_p), jnp.stack(ss_p), jnp.stack(mk_p),
            jnp.stack(mv_p), jnp.stack(sr_s), jnp.stack(sw_s), jnp.stack(ss_s))
```

```python
import functools

import numpy as np
import jax
import jax.numpy as jnp
from jax import lax
from jax.experimental import pallas as pl
from jax.experimental.pallas import tpu as pltpu

F32 = jnp.float32
BF16 = jnp.bfloat16

D_MODEL = 1024
PAST_LEN = 16384
N_MEM = 256
MEM_HEADS = 4
MEM_DH = D_MODEL // MEM_HEADS
RET_HEADS = 4
RET_W = D_MODEL // 2
RET_DV = RET_W // RET_HEADS
RET_DK = RET_DV // 2
RET_QK = RET_HEADS * RET_DK
RET_CHUNK = 128
ROPE_BASE = 10000.0
RWKV_N = 64
RWKV_W = D_MODEL - RET_W
RWKV_HEADS = RWKV_W // RWKV_N
LORA_W = 64
LORA_A = 64
LORA_G = 128
LNX_EPS = 64e-5
N_RET_COLS = 2 * RET_QK + 2 * RET_W
N_RWKV_COLS = 3 * RWKV_W + LORA_W + LORA_A + LORA_G
N_IN_COLS = N_RET_COLS + N_RWKV_COLS
N_GROUPS = 4
EXP_PER_GROUP = 8
N_EXPERTS = N_GROUPS * EXP_PER_GROUP
TOP_K = 2
D_EXPERT = D_MODEL // 2
EPS = 1e-6

LANES = 128
MOE_ROWS = 128
TM = 512


def _params(*sem):
    return pltpu.CompilerParams(dimension_semantics=sem)


def _mm_body(*refs, n_w, has_gain, has_res, n_chunk):
    it = iter(refs)
    x_ref = next(it)
    g_ref = next(it) if has_gain else None
    w_refs = [next(it) for _ in range(n_w)]
    r_ref = next(it) if has_res else None
    o_refs = [next(it) for _ in range(n_w)]
    x = x_ref[...].astype(F32)
    if has_gain:
        x = x * lax.rsqrt(jnp.mean(x * x, axis=-1, keepdims=True) + EPS) * g_ref[...]
    xb = x.astype(BF16)
    for w_ref, o_ref in zip(w_refs, o_refs):
        for j in range(0, w_ref.shape[1], n_chunk):
            acc = jnp.dot(xb, w_ref[:, j:j + n_chunk], preferred_element_type=F32)
            if has_res:
                acc = acc + r_ref[:, j:j + n_chunk]
            o_ref[:, j:j + n_chunk] = acc.astype(o_ref.dtype)


def _matmul(x, ws, *, gain=None, residual=None, tm=TM):
    m, k = x.shape
    n_chunk = 256
    assert m % tm == 0 and all(w.shape[1] % n_chunk == 0 for w in ws)
    assert residual is None or len(ws) == 1
    in_specs = [pl.BlockSpec((tm, k), lambda i: (i, 0))]
    args = [x]
    if gain is not None:
        in_specs.append(pl.BlockSpec((1, k), lambda i: (0, 0)))
        args.append(gain.reshape(1, k).astype(F32))
    for w in ws:
        in_specs.append(pl.BlockSpec(w.shape, lambda i: (0, 0)))
        args.append(w)
    if residual is not None:
        in_specs.append(pl.BlockSpec((tm, ws[0].shape[1]), lambda i: (i, 0)))
        args.append(residual)
    return pl.pallas_call(
        functools.partial(_mm_body, n_w=len(ws), has_gain=gain is not None,
                          has_res=residual is not None, n_chunk=n_chunk),
        grid=(m // tm,),
        in_specs=in_specs,
        out_specs=[pl.BlockSpec((tm, w.shape[1]), lambda i: (i, 0)) for w in ws],
        out_shape=[jax.ShapeDtypeStruct((m, w.shape[1]), F32) for w in ws],
        compiler_params=_params("parallel"),
    )(*args)


def _rot_tables(pos):
    half = RET_DK // 2
    inv_freq = ROPE_BASE ** (-(np.arange(half, dtype=np.float64) / half))
    ang = pos.astype(np.float64)[:, None] * inv_freq[None, :]
    cos, sin = np.cos(ang), np.sin(ang)
    zero = np.zeros_like(sin)
    c = np.tile(np.concatenate([cos, cos], axis=1), (1, RET_HEADS))
    s_lo = np.tile(np.concatenate([-sin, zero], axis=1), (1, RET_HEADS))
    s_hi = np.tile(np.concatenate([zero, sin], axis=1), (1, RET_HEADS))
    return [jnp.asarray(t, F32) for t in (c, s_lo, s_hi)]


def _ret_decay_tables(c):
    lg = np.log1p(-np.exp2(-5.0 - np.arange(RET_HEADS, dtype=np.float64)))
    idx = np.arange(c, dtype=np.float64)
    diff = idx[:, None] - idx[None, :]
    mask = np.where(diff[None] >= 0, np.exp(np.maximum(diff, 0.0)[None] * lg[:, None, None]), 0.0)
    q_dec = np.repeat(np.exp((idx[:, None] + 1.0) * lg[None, :]), RET_DV, axis=1)
    k_dec = np.repeat(np.exp((c - 1.0 - idx)[:, None] * lg[None, :]), RET_DK, axis=1)
    c_dec = [float(v) for v in np.exp(c * lg)]
    return jnp.asarray(mask, F32), jnp.asarray(q_dec, F32), jnp.asarray(k_dec, F32), c_dec


def _ret_body(q_ref, k_ref, v_ref, gate_ref, c_ref, slo_ref, shi_ref, mask_ref, qdec_ref, kdec_ref,
              gn_ref, s0_ref, o_ref, sout_ref, s_scr, *, n_seq, c, c_dec):
    ci = pl.program_id(1)

    @pl.when(ci == 0)
    def _():
        s_scr[...] = s0_ref[...].astype(F32)

    cos, s_lo, s_hi = c_ref[...], slo_ref[...], shi_ref[...]
    half = RET_DK // 2

    def rope(x):
        return x * cos + pltpu.roll(x, RET_QK - half, 1) * s_lo + pltpu.roll(x, half, 1) * s_hi

    nt = (((1,), (1,)), ((), ()))
    tn = (((0,), (0,)), ((), ()))
    for g in range(n_seq):
        rows = slice(g * c, (g + 1) * c)
        q = rope(q_ref[rows, :].astype(F32))
        k = rope(k_ref[rows, :].astype(F32)) * (RET_DK ** -0.5)
        k_st = k * kdec_ref[...]
        for h in range(RET_HEADS):
            kc = slice(h * RET_DK, (h + 1) * RET_DK)
            vc = slice(h * RET_DV, (h + 1) * RET_DV)
            qh = q[:, kc].astype(BF16)
            vh = v_ref[rows, vc].astype(BF16)
            s_h = s_scr[g, h]
            att = lax.dot_general(qh, k[:, kc].astype(BF16), nt, preferred_element_type=F32) * mask_ref[h]
            o = jnp.dot(att.astype(BF16), vh, preferred_element_type=F32)
            o = o + jnp.dot(qh, s_h.astype(BF16), preferred_element_type=F32) * qdec_ref[:, vc]
            s_scr[g, h] = s_h * c_dec[h] + lax.dot_general(
                k_st[:, kc].astype(BF16), vh, tn, preferred_element_type=F32)
            o = o * lax.rsqrt(jnp.mean(o * o, axis=-1, keepdims=True) + EPS)
            gate = gate_ref[rows, vc].astype(F32)
            o_ref[rows, vc] = o * gn_ref[:, vc] * (gate * jax.nn.sigmoid(gate))

    @pl.when(ci == pl.num_programs(1) - 1)
    def _():
        sout_ref[...] = s_scr[...]


def _retention(proj, prev_out, s0, ret_gn, pos, *, row0, n_batch, t, n_seq):
    c = RET_CHUNK if t % RET_CHUNK == 0 else t
    n_chunks = t // c
    rows = n_seq * c
    assert n_batch % n_seq == 0 and row0 % rows == 0 and (n_seq == 1 or n_chunks == 1)
    blk0 = row0 // rows
    mask, q_dec, k_dec, c_dec = _ret_decay_tables(c)
    cos, s_lo, s_hi = _rot_tables(pos)

    def row_map(col):
        return lambda b, ci: (blk0 + b * n_chunks + ci, col)

    def const2(b, ci):
        return (0, 0)

    in_specs = [
        pl.BlockSpec((rows, RET_QK), row_map(0)),
        pl.BlockSpec((rows, RET_QK), row_map(1)),
        pl.BlockSpec((rows, RET_W), row_map(1)),
        pl.BlockSpec((rows, RET_W), row_map(2)),
        pl.BlockSpec((c, RET_QK), lambda b, ci: (ci, 0)),
        pl.BlockSpec((c, RET_QK), lambda b, ci: (ci, 0)),
        pl.BlockSpec((c, RET_QK), lambda b, ci: (ci, 0)),
        pl.BlockSpec((RET_HEADS, c, c), lambda b, ci: (0, 0, 0)),
        pl.BlockSpec((c, RET_W), const2),
        pl.BlockSpec((c, RET_QK), const2),
        pl.BlockSpec((1, RET_W), const2),
        pl.BlockSpec((n_seq, RET_HEADS, RET_DK, RET_DV), lambda b, ci: (b, 0, 0, 0)),
    ]
    args = [proj, proj, proj, proj, cos, s_lo, s_hi, mask, q_dec, k_dec,
            ret_gn.reshape(1, RET_W).astype(F32), s0]
    aliases = {}
    if prev_out is not None:
        in_specs.append(pl.BlockSpec(memory_space=pl.ANY))
        args.append(prev_out)
        aliases = {len(args) - 1: 0}

    def body(*refs):
        if prev_out is not None:
            refs = refs[:12] + refs[13:]
        _ret_body(*refs, n_seq=n_seq, c=c, c_dec=c_dec)

    return pl.pallas_call(
        body,
        grid=(n_batch // n_seq, n_chunks),
        in_specs=in_specs,
        out_specs=[pl.BlockSpec((rows, RET_W), row_map(0)),
                   pl.BlockSpec((n_seq, RET_HEADS, RET_DK, RET_DV), lambda b, ci: (b, 0, 0, 0))],
        out_shape=[jax.ShapeDtypeStruct((proj.shape[0], RET_W), F32),
                   jax.ShapeDtypeStruct((n_batch, RET_HEADS, RET_DK, RET_DV), F32)],
        scratch_shapes=[pltpu.VMEM((n_seq, RET_HEADS, RET_DK, RET_DV), F32)],
        input_output_aliases=aliases,
        compiler_params=_params("parallel", "arbitrary"),
    )(*args)


def _rwkv_pre_body(r_ref, k_ref, v_ref, lo_ref, shift_ref, mu_ref, w0_ref, w2_ref, a0_ref, a2_ref, g2_ref,
                   ro_ref, ko_ref, vo_ref, wo_ref, ao_ref, go_ref, prev_scr, *, n_seq, c):
    ci = pl.program_id(1)

    @pl.when(ci == 0)
    def _():
        for g in range(n_seq):
            prev_scr[g] = shift_ref[g].astype(F32)

    first_row = lax.broadcasted_iota(jnp.int32, (c, 1), 0) == 0

    def shifted(x_ref, g, col0):
        w = x_ref.shape[1]
        x = x_ref[g * c:(g + 1) * c, :].astype(F32)
        prev_row = prev_scr[g, :, col0:col0 + w]
        prev = jnp.where(first_row, prev_row, pltpu.roll(x, 1, 0))
        prev_scr[g, :, col0:col0 + w] = x[c - 1:c, :]
        return x + (prev - x) * mu_ref[:, col0:col0 + w]

    for g in range(n_seq):
        rows = slice(g * c, (g + 1) * c)
        ro_ref[rows, :] = shifted(r_ref, g, 0)
        ko_ref[rows, :] = shifted(k_ref, g, RWKV_W)
        vo_ref[rows, :] = shifted(v_ref, g, 2 * RWKV_W)
        lo = shifted(lo_ref, g, 3 * RWKV_W)
        hw = lo[:, :LORA_W]
        ha = lo[:, LORA_W:LORA_W + LORA_A]
        hg = lo[:, LORA_W + LORA_A:]
        u = w0_ref[...] + jnp.dot(jnp.tanh(hw).astype(BF16), w2_ref[...], preferred_element_type=F32)
        wo_ref[rows, :] = jnp.exp(-float(np.exp(-0.5)) * jax.nn.sigmoid(u))
        ao_ref[rows, :] = jax.nn.sigmoid(
            a0_ref[...] + jnp.dot(ha.astype(BF16), a2_ref[...], preferred_element_type=F32))
        go_ref[rows, :] = jnp.dot(jax.nn.sigmoid(hg).astype(BF16), g2_ref[...], preferred_element_type=F32)


def _rwkv_pre(proj, prev_g, s_shift, mu, w0, w2, a0, a2, g2, *, row0, n_batch, t, n_seq, c):
    n_chunks = t // c
    rows = n_seq * c
    assert t % c == 0 and n_batch % n_seq == 0 and row0 % rows == 0 and (n_seq == 1 or n_chunks == 1)
    blk0 = row0 // rows
    col_r = N_RET_COLS // RWKV_W
    lo_w = LORA_W + LORA_A + LORA_G
    col_lo = (N_RET_COLS + 3 * RWKV_W) // lo_w
    assert col_r * RWKV_W == N_RET_COLS and col_lo * lo_w == N_RET_COLS + 3 * RWKV_W

    def row_map(col):
        return lambda b, ci: (blk0 + b * n_chunks + ci, col)

    def own_map(b, ci):
        return (b * n_chunks + ci, 0)

    def const2(b, ci):
        return (0, 0)

    in_specs = [
        pl.BlockSpec((rows, RWKV_W), row_map(col_r)),
        pl.BlockSpec((rows, RWKV_W), row_map(col_r + 1)),
        pl.BlockSpec((rows, RWKV_W), row_map(col_r + 2)),
        pl.BlockSpec((rows, lo_w), row_map(col_lo)),
        pl.BlockSpec((n_seq, 1, N_RWKV_COLS), lambda b, ci: (b, 0, 0)),
        pl.BlockSpec((1, N_RWKV_COLS), const2),
        pl.BlockSpec((1, RWKV_W), const2),
        pl.BlockSpec((LORA_W, RWKV_W), const2),
        pl.BlockSpec((1, RWKV_W), const2),
        pl.BlockSpec((LORA_A, RWKV_W), const2),
        pl.BlockSpec((LORA_G, RWKV_W), const2),
    ]
    args = [proj, proj, proj, proj, s_shift.reshape(n_batch, 1, N_RWKV_COLS),
            mu.reshape(1, -1), w0.reshape(1, -1), w2.astype(BF16), a0.reshape(1, -1),
            a2.astype(BF16), g2.astype(BF16)]
    aliases = {}
    if prev_g is not None:
        in_specs.append(pl.BlockSpec(memory_space=pl.ANY))
        args.append(prev_g)
        aliases = {len(args) - 1: 5}

    def body(*refs):
        if prev_g is not None:
            refs = refs[:11] + refs[12:]
        _rwkv_pre_body(*refs, n_seq=n_seq, c=c)

    own = jax.ShapeDtypeStruct((n_batch * t, RWKV_W), F32)
    return pl.pallas_call(
        body,
        grid=(n_batch // n_seq, n_chunks),
        in_specs=in_specs,
        out_specs=[pl.BlockSpec((rows, RWKV_W), own_map)] * 5 + [pl.BlockSpec((rows, RWKV_W), row_map(0))],
        out_shape=[own] * 5 + [jax.ShapeDtypeStruct((proj.shape[0], RWKV_W), F32)],
        scratch_shapes=[pltpu.VMEM((n_seq, 1, N_RWKV_COLS), F32)],
        input_output_aliases=aliases,
        compiler_params=_params("parallel", "arbitrary"),
    )(*args)


def _scan_body(r_ref, k_ref, w_ref, a_ref, v_ref, kk_ref, ka_ref, rk_ref, lw_ref, lb_ref, s0_ref,
               y_ref, sout_ref, s_scr, *, tc, vr, halves):
    ci = pl.program_id(1)

    @pl.when(ci == 0)
    def _():
        s_scr[...] = s0_ref[...].astype(F32)

    def ksum(x):
        return jnp.sum(x, axis=0, keepdims=True)

    def vsum(x):
        if halves == 2:
            x = x + pltpu.roll(x, LANES // 2, 1)
        return jnp.sum(x, axis=0, keepdims=True)

    def token(t, carry):
        r, kr, w, a = r_ref[t], k_ref[t], w_ref[t], a_ref[t]
        kk = kr * kk_ref[...]
        kk = kk / jnp.maximum(jnp.sqrt(ksum(kk * kk)), 1e-12)
        kmod = kr * (1.0 + (a - 1.0) * ka_ref[...])
        avec = -kk
        bvec = kk * a
        bonus = ksum(r * kmod * rk_ref[...])

        def value_row(i, c2):
            s = s_scr[i]
            sa = ksum(s * avec)
            s = s * w + sa * bvec + v_ref[t, pl.ds(i, 1), :] * kmod
            s_scr[i] = s
            y_ref[t, pl.ds(i, 1), :] = ksum(s * r)
            return c2

        lax.fori_loop(0, vr, value_row, 0, unroll=4)
        y = y_ref[t]
        d = y - vsum(y) * (1.0 / RWKV_N)
        var = vsum(d * d) * (1.0 / RWKV_N)
        y_ref[t] = d * lax.rsqrt(var + LNX_EPS) * lw_ref[...] + lb_ref[...] + bonus * v_ref[t]
        return carry

    lax.fori_loop(0, tc, token, 0)

    @pl.when(ci == pl.num_programs(1) - 1)
    def _():
        sout_ref[...] = s_scr[...]


def _rwkv_scan(r, k, w, a, v, k_k, k_a, r_k, lnx_w, lnx_b, s0, *, tc, halves):
    n_grp, t, _, lanes = r.shape
    vr = v.shape[2]
    assert lanes == LANES and t % tc == 0 and vr * halves == RWKV_N

    def tok_spec(rows):
        return pl.BlockSpec((None, tc, rows, LANES), lambda g, ci: (g, ci, 0, 0))

    def par_spec(rows):
        return pl.BlockSpec((None, rows, LANES), lambda g, ci: (g, 0, 0))

    st_spec = pl.BlockSpec((None, vr, RWKV_N, LANES), lambda g, ci: (g, 0, 0, 0))
    return pl.pallas_call(
        functools.partial(_scan_body, tc=tc, vr=vr, halves=halves),
        grid=(n_grp, t // tc),
        in_specs=[tok_spec(RWKV_N)] * 4 + [tok_spec(vr)] + [par_spec(RWKV_N)] * 3 + [par_spec(vr)] * 2 + [st_spec],
        out_specs=[tok_spec(vr), st_spec],
        out_shape=[jax.ShapeDtypeStruct((n_grp, t, vr, LANES), F32),
                   jax.ShapeDtypeStruct((n_grp, vr, RWKV_N, LANES), F32)],
        scratch_shapes=[pltpu.VMEM((vr, RWKV_N, LANES), F32)],
        compiler_params=_params("parallel", "arbitrary"),
    )(r, k, w, a, v, k_k, k_a, r_k, lnx_w, lnx_b, s0)


def _merge_body(x_ref, oret_ref, y_ref, g_ref, wt_ref, wb_ref, o_ref):
    yb = (y_ref[...] * g_ref[...]).astype(BF16)
    ob = oret_ref[...].astype(BF16)
    n_chunk = 512
    for j in range(0, D_MODEL, n_chunk):
        acc = jnp.dot(ob, wt_ref[:, j:j + n_chunk], preferred_element_type=F32)
        acc = acc + jnp.dot(yb, wb_ref[:, j:j + n_chunk], preferred_element_type=F32)
        o_ref[:, j:j + n_chunk] = x_ref[:, j:j + n_chunk] + acc


def _merge(x, o_ret, y, g, w_out):
    m = x.shape[0]
    row = lambda w: pl.BlockSpec((TM, w), lambda i: (i, 0))
    wspec = pl.BlockSpec((RET_W, D_MODEL), lambda i: (0, 0))
    return pl.pallas_call(
        _merge_body,
        grid=(m // TM,),
        in_specs=[row(D_MODEL), row(RET_W), row(RWKV_W), row(RWKV_W), wspec, wspec],
        out_specs=row(D_MODEL),
        out_shape=jax.ShapeDtypeStruct((m, D_MODEL), F32),
        compiler_params=_params("parallel"),
    )(x, o_ret, y, g, w_out[:RET_W].astype(BF16), w_out[RET_W:].astype(BF16))


def _attn_body(q_ref, k_ref, v_ref, o_ref, *, n_seq, tq):
    nt = (((1,), (1,)), ((), ()))
    for g in range(n_seq):
        rows = slice(g * tq, (g + 1) * tq)
        for h in range(MEM_HEADS):
            cols = slice(h * MEM_DH, (h + 1) * MEM_DH)
            q = q_ref[rows, cols].astype(BF16)
            s = lax.dot_general(q, k_ref[g, :, cols].astype(BF16), nt, preferred_element_type=F32)
            s = s * (MEM_DH ** -0.5)
            p = jnp.exp(s - jnp.max(s, axis=-1, keepdims=True))
            l = jnp.sum(p, axis=-1, keepdims=True)
            o = jnp.dot(p.astype(BF16), v_ref[g, :, cols].astype(BF16), preferred_element_type=F32)
            o_ref[rows, cols] = o / l


def _attention(q, prev_out, mem_k, mem_v, *, row0, n_batch, t, n_seq, tq):
    q_tiles = t // tq
    rows = n_seq * tq
    assert t % tq == 0 and n_batch % n_seq == 0 and row0 % rows == 0 and (n_seq == 1 or q_tiles == 1)
    blk0 = row0 // rows
    row_spec = pl.BlockSpec((rows, D_MODEL), lambda b, qi: (blk0 + b * q_tiles + qi, 0))
    kv_spec = pl.BlockSpec((n_seq, N_MEM, D_MODEL), lambda b, qi: (b, 0, 0))
    in_specs = [row_spec, kv_spec, kv_spec]
    args = [q, mem_k, mem_v]
    aliases = {}
    if prev_out is not None:
        in_specs.append(pl.BlockSpec(memory_space=pl.ANY))
        args.append(prev_out)
        aliases = {3: 0}

    def body(*refs):
        if prev_out is not None:
            refs = refs[:3] + refs[4:]
        _attn_body(*refs, n_seq=n_seq, tq=tq)

    return pl.pallas_call(
        body,
        grid=(n_batch // n_seq, q_tiles),
        in_specs=in_specs,
        out_specs=row_spec,
        out_shape=jax.ShapeDtypeStruct(q.shape, F32),
        input_output_aliases=aliases,
        compiler_params=_params("parallel", "parallel"),
    )(*args)


def _router_body(h_ref, g_ref, w_ref, b_ref, hn_ref, ids_ref, comb_ref):
    x = h_ref[...]
    hn = x * lax.rsqrt(jnp.mean(x * x, axis=-1, keepdims=True) + EPS) * g_ref[...]
    hn_ref[...] = hn
    logits = jnp.dot(hn, w_ref[...], precision=lax.Precision.HIGHEST, preferred_element_type=F32) + b_ref[...]
    lane = lax.broadcasted_iota(jnp.int32, logits.shape, 1).astype(F32)
    neg = -jnp.inf

    def first_argmax(vals):
        m = jnp.max(vals, axis=-1, keepdims=True)
        return m, jnp.min(jnp.where(vals == m, lane, float(LANES)), axis=-1, keepdims=True)

    gl = jnp.where(lane < N_GROUPS, logits, neg)
    gmax, gsel = first_argmax(gl)
    pg_sel = 1.0 / jnp.sum(jnp.exp(gl - gmax), axis=-1, keepdims=True)
    e0 = N_GROUPS + gsel * EXP_PER_GROUP
    el = jnp.where((lane >= e0) & (lane < e0 + EXP_PER_GROUP), logits, neg)
    m1, i1 = first_argmax(el)
    m2, i2 = first_argmax(jnp.where(lane == i1, neg, el))
    e21 = jnp.exp(m2 - m1)
    c1 = pg_sel / (1.0 + e21)
    c2 = c1 * e21
    ids = jnp.where(lane == 0, i1 - N_GROUPS, jnp.where(lane == 1, i2 - N_GROUPS, 0.0))
    ids_ref[...] = ids.astype(jnp.int32)
    comb_ref[...] = jnp.where(lane == 0, c1, jnp.where(lane == 1, c2, 0.0))


def _router(h, g_ffn, w_gr, b_gr, w_er, b_er):
    m = h.shape[0]
    pad = LANES - N_GROUPS - N_EXPERTS
    w = jnp.concatenate([w_gr, w_er, jnp.zeros((D_MODEL, pad), F32)], axis=1)
    b = jnp.concatenate([b_gr, b_er, jnp.zeros((pad,), F32)]).reshape(1, LANES)
    row = lambda n: pl.BlockSpec((TM, n), lambda i: (i, 0))
    return pl.pallas_call(
        _router_body,
        grid=(m // TM,),
        in_specs=[row(D_MODEL), pl.BlockSpec((1, D_MODEL), lambda i: (0, 0)),
                  pl.BlockSpec((D_MODEL, LANES), lambda i: (0, 0)), pl.BlockSpec((1, LANES), lambda i: (0, 0))],
        out_specs=[row(D_MODEL), row(LANES), row(LANES)],
        out_shape=[jax.ShapeDtypeStruct((m, D_MODEL), F32), jax.ShapeDtypeStruct((m, LANES), jnp.int32),
                   jax.ShapeDtypeStruct((m, LANES), F32)],
        compiler_params=_params("parallel"),
    )(h, g_ffn.reshape(1, D_MODEL), w, b)


def _expert_body(blk_e_ref, idx_ref, idx_next_ref, x_hbm, wg_ref, wu_ref, wd_ref, out_hbm,
                 xbuf, ybuf, gsem, ssem):
    del blk_e_ref
    i = pl.program_id(0)
    n = pl.num_programs(0)
    slot = i % 2

    def gather_copy(tok, r, sl):
        return pltpu.make_async_copy(x_hbm.at[pl.ds(tok, 1), :], xbuf.at[sl, pl.ds(r, 1), :], gsem.at[sl])

    def scatter_copy(dst, r, sl):
        return pltpu.make_async_copy(ybuf.at[sl, pl.ds(r, 1), :], out_hbm.at[pl.ds(dst, 1), :], ssem.at[sl])

    def start_gather(ref, sl):
        def body(r, c):
            gather_copy(ref[0, 0, r], r, sl).start()
            return c
        lax.fori_loop(0, MOE_ROWS, body, 0, unroll=8)

    def wait_rows(copy_of_row):
        def body(r, c):
            copy_of_row(r).wait()
            return c
        lax.fori_loop(0, MOE_ROWS, body, 0, unroll=8)

    @pl.when(i == 0)
    def _():
        start_gather(idx_ref, 0)

    @pl.when(i + 1 < n)
    def _():
        start_gather(idx_next_ref, 1 - slot)

    wait_rows(lambda r: gather_copy(0, r, slot))

    @pl.when(i >= 2)
    def _():
        wait_rows(lambda r: scatter_copy(0, r, slot))

    x = xbuf[slot].astype(BF16)
    hg = jnp.dot(x, wg_ref[0].astype(BF16), preferred_element_type=F32)
    hu = jnp.dot(x, wu_ref[0].astype(BF16), preferred_element_type=F32)
    act = (hg * jax.nn.sigmoid(hg) * hu).astype(BF16)
    ybuf[slot] = jnp.dot(act, wd_ref[0].astype(BF16), preferred_element_type=F32)

    def start_scatter(r, c):
        scatter_copy(idx_ref[0, 1, r], r, slot).start()
        return c
    lax.fori_loop(0, MOE_ROWS, start_scatter, 0, unroll=8)

    @pl.when(i == n - 1)
    def _():
        wait_rows(lambda r: scatter_copy(0, r, slot))

        @pl.when(n >= 2)
        def _():
            wait_rows(lambda r: scatter_copy(0, r, 1 - slot))


def _experts(hn, blk_e, idx, w_gate, w_up, w_down):
    n_blocks = idx.shape[0]
    p = n_blocks * MOE_ROWS
    idx_spec = lambda f: pl.BlockSpec((1, 2, MOE_ROWS), f, memory_space=pltpu.SMEM)
    grid_spec = pltpu.PrefetchScalarGridSpec(
        num_scalar_prefetch=1,
        grid=(n_blocks,),
        in_specs=[
            idx_spec(lambda i, be: (i, 0, 0)),
            idx_spec(lambda i, be: (jnp.minimum(i + 1, n_blocks - 1), 0, 0)),
            pl.BlockSpec(memory_space=pl.ANY),
            pl.BlockSpec((1, D_MODEL, D_EXPERT), lambda i, be: (be[i], 0, 0)),
            pl.BlockSpec((1, D_MODEL, D_EXPERT), lambda i, be: (be[i], 0, 0)),
            pl.BlockSpec((1, D_EXPERT, D_MODEL), lambda i, be: (be[i], 0, 0)),
        ],
        out_specs=pl.BlockSpec(memory_space=pl.ANY),
        scratch_shapes=[pltpu.VMEM((2, MOE_ROWS, D_MODEL), F32), pltpu.VMEM((2, MOE_ROWS, D_MODEL), F32),
                        pltpu.SemaphoreType.DMA((2,)), pltpu.SemaphoreType.DMA((2,))],
    )
    return pl.pallas_call(
        _expert_body,
        grid_spec=grid_spec,
        out_shape=jax.ShapeDtypeStruct((p, D_MODEL), F32),
        compiler_params=_params("arbitrary"),
    )(blk_e, idx, idx, hn, w_gate, w_up, w_down)


def _route_plan(ids):
    n_pairs = ids.size
    n_blocks = -(-(n_pairs + N_EXPERTS * (MOE_ROWS - 1)) // MOE_ROWS)
    p = n_blocks * MOE_ROWS
    flat_e = ids.reshape(n_pairs)
    onehot = (flat_e[:, None] == jnp.arange(N_EXPERTS, dtype=jnp.int32)[None, :]).astype(jnp.int32)
    csum = jnp.cumsum(onehot, axis=0)
    rank = jnp.sum(onehot * csum, axis=1) - 1
    counts = csum[-1]
    pcounts = (counts + MOE_ROWS - 1) // MOE_ROWS * MOE_ROWS
    pends = jnp.cumsum(pcounts)
    pstarts = pends - pcounts
    dest = pstarts[flat_e] + rank
    pair = jnp.arange(n_pairs, dtype=jnp.int32)
    row_pair = jnp.full((p,), -1, jnp.int32).at[dest].set(pair)
    is_pad = row_pair < 0
    pad_rank = jnp.cumsum(is_pad.astype(jnp.int32)) - 1
    row_tok = jnp.where(is_pad, 0, row_pair // TOP_K)
    row_dst = jnp.where(is_pad, n_pairs + pad_rank, row_pair)
    blk_e = jnp.minimum(jnp.searchsorted(pends, jnp.arange(n_blocks, dtype=jnp.int32) * MOE_ROWS, side='right'),
                        N_EXPERTS - 1).astype(jnp.int32)
    idx = jnp.stack([row_tok.reshape(n_blocks, MOE_ROWS), row_dst.reshape(n_blocks, MOE_ROWS)], axis=1)
    return blk_e, idx.astype(jnp.int32)


def _final_body(h_ref, pair_ref, comb_ref, g_ref, o_ref):
    c1 = comb_ref[:, 0:1]
    c2 = comb_ref[:, 1:2]
    x = h_ref[...] + (pair_ref[:, :D_MODEL] * c1 + pair_ref[:, D_MODEL:] * c2)
    o_ref[...] = x * lax.rsqrt(jnp.mean(x * x, axis=-1, keepdims=True) + EPS) * g_ref[...]


def _final(h, pair_out, comb, g_final):
    m = h.shape[0]
    row = lambda n: pl.BlockSpec((TM, n), lambda i: (i, 0))
    return pl.pallas_call(
        _final_body,
        grid=(m // TM,),
        in_specs=[row(D_MODEL), row(TOP_K * D_MODEL), row(LANES), pl.BlockSpec((1, D_MODEL), lambda i: (0, 0))],
        out_specs=row(D_MODEL),
        out_shape=jax.ShapeDtypeStruct((m, D_MODEL), F32),
        compiler_params=_params("parallel"),
    )(h, pair_out, comb, g_final.reshape(1, D_MODEL))


def _prompt_key_layout(x, b, t):
    xc = x.reshape(b, t, RWKV_HEADS, RWKV_N).transpose(1, 3, 0, 2).reshape(t, RWKV_N, b * RWKV_HEADS)
    return jnp.concatenate([xc, xc], axis=-1)[None]


def _prompt_value_layout(x, b, t):
    xc = x.reshape(b, t, RWKV_HEADS, 2, RWKV_N // 2).transpose(1, 4, 3, 0, 2)
    return xc.reshape(t, RWKV_N // 2, 2 * b * RWKV_HEADS)[None]


def _prompt_value_unlayout(y, b, t):
    yc = y[0].reshape(t, RWKV_N // 2, 2, b, RWKV_HEADS).transpose(3, 0, 4, 2, 1)
    return yc.reshape(b * t, RWKV_W)


def _sample_layout(x, b, t):
    return x.reshape(b, t, RWKV_HEADS, RWKV_N).transpose(2, 1, 3, 0)


def _sample_unlayout(y, b, t):
    return y.transpose(3, 1, 0, 2).reshape(b * t, RWKV_W)


def kernel(x_prompt, x_sample, mem_prompt, state_ret, state_rwkv, state_shift, cache_mem_k, cache_mem_v,
           g_mix, w_in, ret_gn, rwkv_mu, rwkv_w0, rwkv_w2, rwkv_a0, rwkv_a2, rwkv_g2, rwkv_k_k, rwkv_k_a,
           rwkv_r_k, rwkv_lnx_w, rwkv_lnx_b, w_out, g_mem_q, g_mem_kv, w_mq, w_mk, w_mv, w_mo, g_ffn,
           w_group_router, b_group_router, w_expert_router, b_expert_router, w_e_gate, w_e_up, w_e_down,
           g_final):
    assert w_in.shape[0] == 1, "single-layer decoder"
    bp, tp, d = x_prompt.shape
    bs, ts, _ = x_sample.shape
    np_tok, ns_tok = bp * tp, bs * ts
    assert d == D_MODEL and bp * RWKV_HEADS * 2 == LANES and bs == LANES
    l = 0
    x = jnp.concatenate([x_prompt.reshape(np_tok, d), x_sample.reshape(ns_tok, d)], axis=0)

    (proj,) = _matmul(x, [w_in[l].astype(BF16)], gain=g_mix[l])

    pos_p = np.arange(tp)
    pos_s = PAST_LEN + np.arange(ts)
    zero_ret = jnp.zeros((bp, RET_HEADS, RET_DK, RET_DV), F32)
    n_tok = np_tok + ns_tok
    o_ret, sret_p = _retention(proj, jnp.zeros((n_tok, RET_W), F32), zero_ret, ret_gn[l], pos_p, row0=0, n_batch=bp, t=tp, n_seq=1)
    o_ret, sret_s = _retention(proj, o_ret, state_ret[l], ret_gn[l], pos_s, row0=np_tok, n_batch=bs, t=ts,
                               n_seq=16)

    pre_w = (rwkv_mu[l], rwkv_w0[l], rwkv_w2[l], rwkv_a0[l], rwkv_a2[l], rwkv_g2[l])
    zero_shift = jnp.zeros((bp, N_RWKV_COLS), F32)
    *pre_p, gate = _rwkv_pre(proj, jnp.zeros((n_tok, RWKV_W), F32), zero_shift, *pre_w, row0=0, n_batch=bp, t=tp, n_seq=1, c=256)
    *pre_s, gate = _rwkv_pre(proj, gate, state_shift[l], *pre_w, row0=np_tok, n_batch=bs, t=ts, n_seq=16, c=ts)

    kvec = lambda v: v.reshape(RWKV_HEADS, RWKV_N)
    r_p, k_p, v_p, w_p, a_p = pre_p
    key_par = lambda v: jnp.broadcast_to(kvec(v).T[:, None, None, :], (RWKV_N, 2, bp, RWKV_HEADS)).reshape(
        1, RWKV_N, LANES)
    val_par = lambda v: jnp.broadcast_to(
        v.reshape(RWKV_HEADS, 2, RWKV_N // 2).transpose(2, 1, 0)[:, :, None, :],
        (RWKV_N // 2, 2, bp, RWKV_HEADS)).reshape(1, RWKV_N // 2, LANES)
    y_p, srw_p = _rwkv_scan(
        _prompt_key_layout(r_p, bp, tp), _prompt_key_layout(k_p, bp, tp), _prompt_key_layout(w_p, bp, tp),
        _prompt_key_layout(a_p, bp, tp), _prompt_value_layout(v_p, bp, tp),
        key_par(rwkv_k_k[l]), key_par(rwkv_k_a[l]), key_par(rwkv_r_k[l]),
        val_par(rwkv_lnx_w[l]), val_par(rwkv_lnx_b[l]),
        jnp.zeros((1, RWKV_N // 2, RWKV_N, LANES), F32), tc=32, halves=2)
    y_p = _prompt_value_unlayout(y_p, bp, tp)
    srw_p = srw_p[0].reshape(RWKV_N // 2, RWKV_N, 2, bp, RWKV_HEADS).transpose(3, 4, 2, 0, 1).reshape(
        bp, RWKV_HEADS, RWKV_N, RWKV_N)
    r_s, k_s, v_s, w_s, a_s = pre_s
    head_par = lambda v: jnp.broadcast_to(kvec(v)[:, :, None], (RWKV_HEADS, RWKV_N, LANES))
    y_s, srw_s = _rwkv_scan(
        _sample_layout(r_s, bs, ts), _sample_layout(k_s, bs, ts), _sample_layout(w_s, bs, ts),
        _sample_layout(a_s, bs, ts), _sample_layout(v_s, bs, ts),
        head_par(rwkv_k_k[l]), head_par(rwkv_k_a[l]), head_par(rwkv_r_k[l]),
        head_par(rwkv_lnx_w[l]), head_par(rwkv_lnx_b[l]),
        state_rwkv[l].astype(F32).transpose(1, 2, 3, 0), tc=ts, halves=1)
    y_s = _sample_unlayout(y_s, bs, ts)
    srw_s = srw_s.transpose(3, 0, 1, 2)
    y_rwkv = jnp.concatenate([y_p, y_s], axis=0)

    h = _merge(x, o_ret, y_rwkv, gate, w_out[l])

    mk, mv = _matmul(mem_prompt.reshape(bp * N_MEM, d), [w_mk[l].astype(BF16), w_mv[l].astype(BF16)],
                     gain=g_mem_kv[l])
    (q,) = _matmul(h, [w_mq[l].astype(BF16)], gain=g_mem_q[l])
    att = _attention(q, jnp.zeros((n_tok, d), F32), mk.reshape(bp, N_MEM, d), mv.reshape(bp, N_MEM, d),
                     row0=0, n_batch=bp, t=tp, n_seq=1, tq=TM)
    att = _attention(q, att, cache_mem_k[l].reshape(bs, N_MEM, d), cache_mem_v[l].reshape(bs, N_MEM, d),
                     row0=np_tok, n_batch=bs, t=ts, n_seq=4, tq=ts)
    (h,) = _matmul(att, [w_mo[l].astype(BF16)], residual=h)

    hn, ids, comb = _router(h, g_ffn[l], w_group_router[l], b_group_router[l], w_expert_router[l],
                            b_expert_router[l])
    blk_e, idx = _route_plan(ids[:, :TOP_K])
    pair_out = _experts(hn, blk_e, idx, w_e_gate[l], w_e_up[l], w_e_down[l])
    y = _final(h, pair_out.reshape(-1, TOP_K * D_MODEL), comb, g_final)

    y_prompt = y[:np_tok].reshape(bp, tp, d)
    y_sample = y[np_tok:].reshape(bs, ts, d)
    shift_p = proj[:np_tok].reshape(bp, tp, N_IN_COLS)[:, -1, N_RET_COLS:]
    shift_s = proj[np_tok:].reshape(bs, ts, N_IN_COLS)[:, -1, N_RET_COLS:]
    return (y_prompt, y_sample, sret_p[None], srw_p[None], shift_p[None],
            mk.reshape(1, bp, N_MEM, MEM_HEADS, MEM_DH), mv.reshape(1, bp, N_MEM, MEM_HEADS, MEM_DH),
            sret_s[None], srw_s[None], shift_s[None])
```

```python
import functools

import numpy as np
import jax
import jax.numpy as jnp
from jax import lax
from jax.experimental import pallas as pl
from jax.experimental.pallas import tpu as pltpu

F32 = jnp.float32
BF16 = jnp.bfloat16

D_MODEL = 1024
PAST_LEN = 16384
N_MEM = 256
MEM_HEADS = 4
MEM_DH = D_MODEL // MEM_HEADS
RET_HEADS = 4
RET_W = D_MODEL // 2
RET_DV = RET_W // RET_HEADS
RET_DK = RET_DV // 2
RET_QK = RET_HEADS * RET_DK
RET_CHUNK = 128
ROPE_BASE = 10000.0
RWKV_N = 64
RWKV_W = D_MODEL - RET_W
RWKV_HEADS = RWKV_W // RWKV_N
LORA_W = 64
LORA_A = 64
LORA_G = 128
LNX_EPS = 64e-5
N_RET_COLS = 2 * RET_QK + 2 * RET_W
N_RWKV_COLS = 3 * RWKV_W + LORA_W + LORA_A + LORA_G
N_IN_COLS = N_RET_COLS + N_RWKV_COLS
N_GROUPS = 4
EXP_PER_GROUP = 8
N_EXPERTS = N_GROUPS * EXP_PER_GROUP
TOP_K = 2
D_EXPERT = D_MODEL // 2
EPS = 1e-6

LANES = 128
MOE_ROWS = 128
TM = 512


def _params(*sem):
    return pltpu.CompilerParams(dimension_semantics=sem)


def _row_part_specs(parts, tm):
    specs, counts, start = [], [], 0
    for part in parts:
        nb = part.shape[0] // tm
        assert nb * tm == part.shape[0]
        specs.append(pl.BlockSpec((tm, part.shape[1]), lambda i, s=start, n=nb: (jnp.clip(i - s, 0, n - 1), 0)))
        counts.append(nb)
        start += nb
    return specs, counts


def _read_row_parts(refs, counts):
    i = pl.program_id(0)
    x = refs[0][...]
    start = counts[0]
    for ref, nb in zip(refs[1:], counts[1:]):
        x = jnp.where(i >= start, ref[...], x)
        start += nb
    return x


def _mm_body(*refs, part_counts, n_w, has_gain, has_res, n_chunk, head_dim):
    it = iter(refs)
    x_refs = [next(it) for _ in part_counts]
    g_ref = next(it) if has_gain else None
    w_refs = [next(it) for _ in range(n_w)]
    r_ref = next(it) if has_res else None
    o_refs = [next(it) for _ in range(n_w)]
    x = _read_row_parts(x_refs, part_counts).astype(F32)
    if has_gain:
        x = x * lax.rsqrt(jnp.mean(x * x, axis=-1, keepdims=True) + EPS) * g_ref[...]
    xb = x.astype(BF16)
    for w_ref, o_ref in zip(w_refs, o_refs):
        for j in range(0, w_ref.shape[1], n_chunk):
            acc = jnp.dot(xb, w_ref[:, j:j + n_chunk], preferred_element_type=F32)
            if has_res:
                acc = acc + r_ref[:, j:j + n_chunk]
            if head_dim is None:
                o_ref[:, j:j + n_chunk] = acc
            else:
                o_ref[:, j // head_dim, :] = acc


def _matmul(x_parts, ws, *, gain=None, residual=None, tm=TM, head_dim=None):
    k = x_parts[0].shape[1]
    m = sum(part.shape[0] for part in x_parts)
    n_chunk = 256 if head_dim is None else head_dim
    assert all(w.shape[1] % n_chunk == 0 for w in ws)
    assert residual is None or len(ws) == 1
    in_specs, part_counts = _row_part_specs(x_parts, tm)
    args = list(x_parts)
    if gain is not None:
        in_specs.append(pl.BlockSpec((1, k), lambda i: (0, 0)))
        args.append(gain.reshape(1, k).astype(F32))
    for w in ws:
        in_specs.append(pl.BlockSpec(w.shape, lambda i: (0, 0)))
        args.append(w)
    if residual is not None:
        in_specs.append(pl.BlockSpec((tm, ws[0].shape[1]), lambda i: (i, 0)))
        args.append(residual)
    if head_dim is None:
        out_specs = [pl.BlockSpec((tm, w.shape[1]), lambda i: (i, 0)) for w in ws]
        out_shape = [jax.ShapeDtypeStruct((m, w.shape[1]), F32) for w in ws]
    else:
        out_specs = [pl.BlockSpec((tm, w.shape[1] // head_dim, head_dim), lambda i: (i, 0, 0)) for w in ws]
        out_shape = [jax.ShapeDtypeStruct((m, w.shape[1] // head_dim, head_dim), F32) for w in ws]
    return pl.pallas_call(
        functools.partial(_mm_body, part_counts=part_counts, n_w=len(ws), has_gain=gain is not None,
                          has_res=residual is not None, n_chunk=n_chunk, head_dim=head_dim),
        grid=(m // tm,),
        in_specs=in_specs,
        out_specs=out_specs,
        out_shape=out_shape,
        compiler_params=_params("parallel"),
    )(*args)


def _rot_tables(pos):
    half = RET_DK // 2
    inv_freq = ROPE_BASE ** (-(np.arange(half, dtype=np.float64) / half))
    ang = pos.astype(np.float64)[:, None] * inv_freq[None, :]
    cos, sin = np.cos(ang), np.sin(ang)
    zero = np.zeros_like(sin)
    c = np.tile(np.concatenate([cos, cos], axis=1), (1, RET_HEADS))
    s_lo = np.tile(np.concatenate([-sin, zero], axis=1), (1, RET_HEADS))
    s_hi = np.tile(np.concatenate([zero, sin], axis=1), (1, RET_HEADS))
    return [jnp.asarray(t, F32) for t in (c, s_lo, s_hi)]


def _ret_decay_tables(c):
    lg = np.log1p(-np.exp2(-5.0 - np.arange(RET_HEADS, dtype=np.float64)))
    idx = np.arange(c, dtype=np.float64)
    diff = idx[:, None] - idx[None, :]
    mask = np.where(diff[None] >= 0, np.exp(np.maximum(diff, 0.0)[None] * lg[:, None, None]), 0.0)
    q_dec = np.repeat(np.exp((idx[:, None] + 1.0) * lg[None, :]), RET_DV, axis=1)
    k_dec = np.repeat(np.exp((c - 1.0 - idx)[:, None] * lg[None, :]), RET_DK, axis=1)
    c_dec = [float(v) for v in np.exp(c * lg)]
    return jnp.asarray(mask, F32), jnp.asarray(q_dec, F32), jnp.asarray(k_dec, F32), c_dec


def _ret_body(q_ref, k_ref, v_ref, gate_ref, c_ref, slo_ref, shi_ref, mask_ref, qdec_ref, kdec_ref,
              gn_ref, s0_ref, o_ref, sout_ref, s_scr, *, n_seq, c, c_dec):
    ci = pl.program_id(1)

    @pl.when(ci == 0)
    def _():
        s_scr[...] = s0_ref[...].astype(F32)

    cos, s_lo, s_hi = c_ref[...], slo_ref[...], shi_ref[...]
    half = RET_DK // 2

    def rope(x):
        return x * cos + pltpu.roll(x, RET_QK - half, 1) * s_lo + pltpu.roll(x, half, 1) * s_hi

    nt = (((1,), (1,)), ((), ()))
    tn = (((0,), (0,)), ((), ()))
    for g in range(n_seq):
        rows = slice(g * c, (g + 1) * c)
        q = rope(q_ref[rows, :].astype(F32))
        k = rope(k_ref[rows, :].astype(F32)) * (RET_DK ** -0.5)
        k_st = k * kdec_ref[...]
        for h in range(RET_HEADS):
            kc = slice(h * RET_DK, (h + 1) * RET_DK)
            vc = slice(h * RET_DV, (h + 1) * RET_DV)
            qh = q[:, kc].astype(BF16)
            vh = v_ref[rows, vc].astype(BF16)
            s_h = s_scr[g, h]
            att = lax.dot_general(qh, k[:, kc].astype(BF16), nt, preferred_element_type=F32) * mask_ref[h]
            o = jnp.dot(att.astype(BF16), vh, preferred_element_type=F32)
            o = o + jnp.dot(qh, s_h.astype(BF16), preferred_element_type=F32) * qdec_ref[:, vc]
            s_scr[g, h] = s_h * c_dec[h] + lax.dot_general(
                k_st[:, kc].astype(BF16), vh, tn, preferred_element_type=F32)
            o = o * lax.rsqrt(jnp.mean(o * o, axis=-1, keepdims=True) + EPS)
            gate = gate_ref[rows, vc].astype(F32)
            o_ref[rows, vc] = o * gn_ref[:, vc] * (gate * jax.nn.sigmoid(gate))

    @pl.when(ci == pl.num_programs(1) - 1)
    def _():
        sout_ref[...] = s_scr[...]


def _retention(proj, s0, ret_gn, pos, *, row0, n_batch, t, n_seq):
    c = RET_CHUNK if t % RET_CHUNK == 0 else t
    n_chunks = t // c
    rows = n_seq * c
    assert n_batch % n_seq == 0 and row0 % rows == 0 and (n_seq == 1 or n_chunks == 1)
    blk0 = row0 // rows
    mask, q_dec, k_dec, c_dec = _ret_decay_tables(c)
    cos, s_lo, s_hi = _rot_tables(pos)

    def row_map(col):
        return lambda b, ci: (blk0 + b * n_chunks + ci, col)

    def const2(b, ci):
        return (0, 0)

    state_spec = pl.BlockSpec((n_seq, RET_HEADS, RET_DK, RET_DV), lambda b, ci: (b, 0, 0, 0))
    in_specs = [
        pl.BlockSpec((rows, RET_QK), row_map(0)),
        pl.BlockSpec((rows, RET_QK), row_map(1)),
        pl.BlockSpec((rows, RET_W), row_map(1)),
        pl.BlockSpec((rows, RET_W), row_map(2)),
        pl.BlockSpec((c, RET_QK), lambda b, ci: (ci, 0)),
        pl.BlockSpec((c, RET_QK), lambda b, ci: (ci, 0)),
        pl.BlockSpec((c, RET_QK), lambda b, ci: (ci, 0)),
        pl.BlockSpec((RET_HEADS, c, c), lambda b, ci: (0, 0, 0)),
        pl.BlockSpec((c, RET_W), const2),
        pl.BlockSpec((c, RET_QK), const2),
        pl.BlockSpec((1, RET_W), const2),
        state_spec,
    ]
    return pl.pallas_call(
        functools.partial(_ret_body, n_seq=n_seq, c=c, c_dec=c_dec),
        grid=(n_batch // n_seq, n_chunks),
        in_specs=in_specs,
        out_specs=[pl.BlockSpec((rows, RET_W), lambda b, ci: (b * n_chunks + ci, 0)), state_spec],
        out_shape=[jax.ShapeDtypeStruct((n_batch * t, RET_W), F32),
                   jax.ShapeDtypeStruct((n_batch, RET_HEADS, RET_DK, RET_DV), F32)],
        scratch_shapes=[pltpu.VMEM((n_seq, RET_HEADS, RET_DK, RET_DV), F32)],
        compiler_params=_params("parallel", "arbitrary"),
    )(proj, proj, proj, proj, cos, s_lo, s_hi, mask, q_dec, k_dec, ret_gn.reshape(1, RET_W).astype(F32), s0)


def _rwkv_pre_body(r_ref, k_ref, v_ref, lo_ref, shift_ref, mu_ref, w0_ref, w2_ref, a0_ref, a2_ref, g2_ref,
                   ro_ref, ko_ref, vo_ref, wo_ref, ao_ref, go_ref, prev_scr, *, n_seq, c):
    ci = pl.program_id(1)

    @pl.when(ci == 0)
    def _():
        for g in range(n_seq):
            prev_scr[g] = shift_ref[g].astype(F32)

    first_row = lax.broadcasted_iota(jnp.int32, (c, 1), 0) == 0

    def shifted(x_ref, g, col0):
        w = x_ref.shape[1]
        x = x_ref[g * c:(g + 1) * c, :].astype(F32)
        prev_row = prev_scr[g, :, col0:col0 + w]
        prev = jnp.where(first_row, prev_row, pltpu.roll(x, 1, 0))
        prev_scr[g, :, col0:col0 + w] = x[c - 1:c, :]
        return x + (prev - x) * mu_ref[:, col0:col0 + w]

    for g in range(n_seq):
        rows = slice(g * c, (g + 1) * c)
        ro_ref[rows, :] = shifted(r_ref, g, 0)
        ko_ref[rows, :] = shifted(k_ref, g, RWKV_W)
        vo_ref[rows, :] = shifted(v_ref, g, 2 * RWKV_W)
        lo = shifted(lo_ref, g, 3 * RWKV_W)
        hw = lo[:, :LORA_W]
        ha = lo[:, LORA_W:LORA_W + LORA_A]
        hg = lo[:, LORA_W + LORA_A:]
        u = w0_ref[...] + jnp.dot(jnp.tanh(hw).astype(BF16), w2_ref[...], preferred_element_type=F32)
        wo_ref[rows, :] = jnp.exp(-float(np.exp(-0.5)) * jax.nn.sigmoid(u))
        ao_ref[rows, :] = jax.nn.sigmoid(
            a0_ref[...] + jnp.dot(ha.astype(BF16), a2_ref[...], preferred_element_type=F32))
        go_ref[rows, :] = jnp.dot(jax.nn.sigmoid(hg).astype(BF16), g2_ref[...], preferred_element_type=F32)


def _rwkv_pre(proj, s_shift, mu, w0, w2, a0, a2, g2, *, row0, n_batch, t, n_seq, c):
    n_chunks = t // c
    rows = n_seq * c
    assert t % c == 0 and n_batch % n_seq == 0 and row0 % rows == 0 and (n_seq == 1 or n_chunks == 1)
    blk0 = row0 // rows
    col_r = N_RET_COLS // RWKV_W
    lo_w = LORA_W + LORA_A + LORA_G
    col_lo = (N_RET_COLS + 3 * RWKV_W) // lo_w
    assert col_r * RWKV_W == N_RET_COLS and col_lo * lo_w == N_RET_COLS + 3 * RWKV_W

    def row_map(col):
        return lambda b, ci: (blk0 + b * n_chunks + ci, col)

    def const2(b, ci):
        return (0, 0)

    in_specs = [
        pl.BlockSpec((rows, RWKV_W), row_map(col_r)),
        pl.BlockSpec((rows, RWKV_W), row_map(col_r + 1)),
        pl.BlockSpec((rows, RWKV_W), row_map(col_r + 2)),
        pl.BlockSpec((rows, lo_w), row_map(col_lo)),
        pl.BlockSpec((n_seq, 1, N_RWKV_COLS), lambda b, ci: (b, 0, 0)),
        pl.BlockSpec((1, N_RWKV_COLS), const2),
        pl.BlockSpec((1, RWKV_W), const2),
        pl.BlockSpec((LORA_W, RWKV_W), const2),
        pl.BlockSpec((1, RWKV_W), const2),
        pl.BlockSpec((LORA_A, RWKV_W), const2),
        pl.BlockSpec((LORA_G, RWKV_W), const2),
    ]
    own = jax.ShapeDtypeStruct((n_batch * t, RWKV_W), F32)
    return pl.pallas_call(
        functools.partial(_rwkv_pre_body, n_seq=n_seq, c=c),
        grid=(n_batch // n_seq, n_chunks),
        in_specs=in_specs,
        out_specs=[pl.BlockSpec((rows, RWKV_W), lambda b, ci: (b * n_chunks + ci, 0))] * 6,
        out_shape=[own] * 6,
        scratch_shapes=[pltpu.VMEM((n_seq, 1, N_RWKV_COLS), F32)],
        compiler_params=_params("parallel", "arbitrary"),
    )(proj, proj, proj, proj, s_shift.reshape(n_batch, 1, N_RWKV_COLS),
      mu.reshape(1, -1), w0.reshape(1, -1), w2.astype(BF16), a0.reshape(1, -1),
      a2.astype(BF16), g2.astype(BF16))


def _scan_body(r_ref, k_ref, w_ref, a_ref, v_ref, kk_ref, ka_ref, rk_ref, lw_ref, lb_ref, s0_ref,
               y_ref, sout_ref, s_scr, a_scr, b_scr, km_scr, *, tc, vr, halves):
    ci = pl.program_id(1)

    @pl.when(ci == 0)
    def _():
        s_scr[...] = s0_ref[...].astype(F32)

    def ksum(x):
        return jnp.sum(x, axis=-2, keepdims=True)

    def vsum(x):
        if halves == 2:
            x2 = x.reshape(tc * vr, LANES)
            x = (x2 + pltpu.roll(x2, LANES // 2, 1)).reshape(tc, vr, LANES)
        return jnp.sum(x, axis=1, keepdims=True)

    kr = k_ref[...]
    a = a_ref[...]
    kk = kr * kk_ref[...]
    kk = kk / jnp.maximum(jnp.sqrt(ksum(kk * kk)), 1e-12)
    a_scr[...] = -kk
    b_scr[...] = kk * a
    km_scr[...] = kr * (1.0 + (a - 1.0) * ka_ref[...])

    def token(t, carry):
        r, w, avec, bvec, kmod = r_ref[t], w_ref[t], a_scr[t], b_scr[t], km_scr[t]

        def value_row(i, c2):
            s = s_scr[i]
            sa = ksum(s * avec)
            s = s * w + sa * bvec + v_ref[t, pl.ds(i, 1), :] * kmod
            s_scr[i] = s
            y_ref[t, pl.ds(i, 1), :] = ksum(s * r)
            return c2

        lax.fori_loop(0, vr, value_row, 0, unroll=4)
        return carry

    lax.fori_loop(0, tc, token, 0)

    y = y_ref[...]
    d = y - vsum(y) * (1.0 / RWKV_N)
    var = vsum(d * d) * (1.0 / RWKV_N)
    bonus = ksum(r_ref[...] * km_scr[...] * rk_ref[...])
    y_ref[...] = d * lax.rsqrt(var + LNX_EPS) * lw_ref[...] + lb_ref[...] + bonus * v_ref[...]

    @pl.when(ci == pl.num_programs(1) - 1)
    def _():
        sout_ref[...] = s_scr[...]


def _rwkv_scan(r, k, w, a, v, k_k, k_a, r_k, lnx_w, lnx_b, s0, *, tc, halves):
    n_grp, t, _, lanes = r.shape
    vr = v.shape[2]
    assert lanes == LANES and t % tc == 0 and vr * halves == RWKV_N

    def tok_spec(rows):
        return pl.BlockSpec((None, tc, rows, LANES), lambda g, ci: (g, ci, 0, 0))

    def par_spec(rows):
        return pl.BlockSpec((None, rows, LANES), lambda g, ci: (g, 0, 0))

    st_spec = pl.BlockSpec((None, vr, RWKV_N, LANES), lambda g, ci: (g, 0, 0, 0))
    key_scratch = pltpu.VMEM((tc, RWKV_N, LANES), F32)
    return pl.pallas_call(
        functools.partial(_scan_body, tc=tc, vr=vr, halves=halves),
        grid=(n_grp, t // tc),
        in_specs=[tok_spec(RWKV_N)] * 4 + [tok_spec(vr)] + [par_spec(RWKV_N)] * 3 + [par_spec(vr)] * 2 + [st_spec],
        out_specs=[tok_spec(vr), st_spec],
        out_shape=[jax.ShapeDtypeStruct((n_grp, t, vr, LANES), F32),
                   jax.ShapeDtypeStruct((n_grp, vr, RWKV_N, LANES), F32)],
        scratch_shapes=[pltpu.VMEM((vr, RWKV_N, LANES), F32), key_scratch, key_scratch, key_scratch],
        compiler_params=_params("parallel", "arbitrary"),
    )(r, k, w, a, v, k_k, k_a, r_k, lnx_w, lnx_b, s0)


def _merge_body(*refs, part_counts):
    n = len(part_counts)
    x_refs, oret_refs, y_refs, g_refs = refs[:n], refs[n:2 * n], refs[2 * n:3 * n], refs[3 * n:4 * n]
    wt_ref, wb_ref, o_ref = refs[4 * n:]
    x = _read_row_parts(x_refs, part_counts)
    yb = (_read_row_parts(y_refs, part_counts) * _read_row_parts(g_refs, part_counts)).astype(BF16)
    ob = _read_row_parts(oret_refs, part_counts).astype(BF16)
    n_chunk = 256
    for j in range(0, D_MODEL, n_chunk):
        acc = jnp.dot(ob, wt_ref[:, j:j + n_chunk], preferred_element_type=F32)
        acc = acc + jnp.dot(yb, wb_ref[:, j:j + n_chunk], preferred_element_type=F32)
        o_ref[:, j:j + n_chunk] = x[:, j:j + n_chunk] + acc


def _merge(x_parts, oret_parts, y_parts, g_parts, w_out):
    m = sum(part.shape[0] for part in x_parts)
    in_specs, part_counts = [], None
    for parts in (x_parts, oret_parts, y_parts, g_parts):
        specs, part_counts = _row_part_specs(parts, TM)
        in_specs += specs
    wspec = pl.BlockSpec((RET_W, D_MODEL), lambda i: (0, 0))
    return pl.pallas_call(
        functools.partial(_merge_body, part_counts=part_counts),
        grid=(m // TM,),
        in_specs=in_specs + [wspec, wspec],
        out_specs=pl.BlockSpec((TM, D_MODEL), lambda i: (i, 0)),
        out_shape=jax.ShapeDtypeStruct((m, D_MODEL), F32),
        compiler_params=_params("parallel"),
    )(*x_parts, *oret_parts, *y_parts, *g_parts, w_out[:RET_W].astype(BF16), w_out[RET_W:].astype(BF16))


def _attn_body(q_ref, k_ref, v_ref, o_ref, *, n_seq, tq):
    nt = (((1,), (1,)), ((), ()))
    for g in range(n_seq):
        rows = slice(g * tq, (g + 1) * tq)
        for h in range(MEM_HEADS):
            cols = slice(h * MEM_DH, (h + 1) * MEM_DH)
            q = q_ref[rows, cols].astype(BF16)
            s = lax.dot_general(q, k_ref[g, :, h, :].astype(BF16), nt, preferred_element_type=F32)
            s = s * (MEM_DH ** -0.5)
            p = jnp.exp(s - jnp.max(s, axis=-1, keepdims=True))
            l = jnp.sum(p, axis=-1, keepdims=True)
            o = jnp.dot(p.astype(BF16), v_ref[g, :, h, :].astype(BF16), preferred_element_type=F32)
            o_ref[rows, cols] = o / l


def _attention(q, mem_k, mem_v, *, row0, n_batch, t, n_seq, tq):
    q_tiles = t // tq
    rows = n_seq * tq
    assert t % tq == 0 and n_batch % n_seq == 0 and row0 % rows == 0 and (n_seq == 1 or q_tiles == 1)
    blk0 = row0 // rows
    kv_spec = pl.BlockSpec((n_seq, N_MEM, MEM_HEADS, MEM_DH), lambda b, qi: (b, 0, 0, 0))
    return pl.pallas_call(
        functools.partial(_attn_body, n_seq=n_seq, tq=tq),
        grid=(n_batch // n_seq, q_tiles),
        in_specs=[pl.BlockSpec((rows, D_MODEL), lambda b, qi: (blk0 + b * q_tiles + qi, 0)), kv_spec, kv_spec],
        out_specs=pl.BlockSpec((rows, D_MODEL), lambda b, qi: (b * q_tiles + qi, 0)),
        out_shape=jax.ShapeDtypeStruct((n_batch * t, D_MODEL), F32),
        compiler_params=_params("parallel", "parallel"),
    )(q, mem_k, mem_v)


def _router_body(h_ref, g_ref, w_ref, b_ref, hn_ref, ids_ref, comb_ref):
    x = h_ref[...]
    hn = x * lax.rsqrt(jnp.mean(x * x, axis=-1, keepdims=True) + EPS) * g_ref[...]
    hn_ref[...] = hn
    logits = jnp.dot(hn, w_ref[...], precision=lax.Precision.HIGHEST, preferred_element_type=F32) + b_ref[...]
    lane = lax.broadcasted_iota(jnp.int32, logits.shape, 1).astype(F32)
    neg = -jnp.inf

    def first_argmax(vals):
        m = jnp.max(vals, axis=-1, keepdims=True)
        return m, jnp.min(jnp.where(vals == m, lane, float(LANES)), axis=-1, keepdims=True)

    gl = jnp.where(lane < N_GROUPS, logits, neg)
    gmax, gsel = first_argmax(gl)
    pg_sel = 1.0 / jnp.sum(jnp.exp(gl - gmax), axis=-1, keepdims=True)
    e0 = N_GROUPS + gsel * EXP_PER_GROUP
    el = jnp.where((lane >= e0) & (lane < e0 + EXP_PER_GROUP), logits, neg)
    m1, i1 = first_argmax(el)
    m2, i2 = first_argmax(jnp.where(lane == i1, neg, el))
    e21 = jnp.exp(m2 - m1)
    c1 = pg_sel / (1.0 + e21)
    c2 = c1 * e21
    ids = jnp.where(lane == 0, i1 - N_GROUPS, jnp.where(lane == 1, i2 - N_GROUPS, 0.0))
    ids_ref[...] = ids.astype(jnp.int32)
    comb_ref[...] = jnp.where(lane == 0, c1, jnp.where(lane == 1, c2, 0.0))


def _router(h, g_ffn, w_gr, b_gr, w_er, b_er):
    m = h.shape[0]
    pad = LANES - N_GROUPS - N_EXPERTS
    w = jnp.concatenate([w_gr, w_er, jnp.zeros((D_MODEL, pad), F32)], axis=1)
    b = jnp.concatenate([b_gr, b_er, jnp.zeros((pad,), F32)]).reshape(1, LANES)
    row = lambda n: pl.BlockSpec((TM, n), lambda i: (i, 0))
    return pl.pallas_call(
        _router_body,
        grid=(m // TM,),
        in_specs=[row(D_MODEL), pl.BlockSpec((1, D_MODEL), lambda i: (0, 0)),
                  pl.BlockSpec((D_MODEL, LANES), lambda i: (0, 0)), pl.BlockSpec((1, LANES), lambda i: (0, 0))],
        out_specs=[row(D_MODEL), row(LANES), row(LANES)],
        out_shape=[jax.ShapeDtypeStruct((m, D_MODEL), F32), jax.ShapeDtypeStruct((m, LANES), jnp.int32),
                   jax.ShapeDtypeStruct((m, LANES), F32)],
        compiler_params=_params("parallel"),
    )(h, g_ffn.reshape(1, D_MODEL), w, b)


def _expert_body(blk_e_ref, idx_ref, idx_next_ref, x_hbm, wg_ref, wu_ref, wd_ref, out_hbm,
                 xbuf, ybuf, gsem, ssem):
    del blk_e_ref
    i = pl.program_id(0)
    n = pl.num_programs(0)
    slot = i % 2

    def start_gather(ref, sl):
        def body(r, c):
            pltpu.make_async_copy(x_hbm.at[pl.ds(ref[0, 0, r], 1), :], xbuf.at[sl, pl.ds(r, 1), :],
                                  gsem.at[sl]).start()
            return c
        lax.fori_loop(0, MOE_ROWS, body, 0, unroll=8)

    def wait_gather(sl):
        pltpu.make_async_copy(x_hbm.at[pl.ds(0, MOE_ROWS), :], xbuf.at[sl], gsem.at[sl]).wait()

    def wait_scatter(sl):
        pltpu.make_async_copy(ybuf.at[sl], out_hbm.at[pl.ds(0, MOE_ROWS), :], ssem.at[sl]).wait()

    @pl.when(i == 0)
    def _():
        start_gather(idx_ref, 0)

    @pl.when(i + 1 < n)
    def _():
        start_gather(idx_next_ref, 1 - slot)

    wait_gather(slot)

    @pl.when(i >= 2)
    def _():
        wait_scatter(slot)

    x = xbuf[slot].astype(BF16)
    hg = jnp.dot(x, wg_ref[0].astype(BF16), preferred_element_type=F32)
    hu = jnp.dot(x, wu_ref[0].astype(BF16), preferred_element_type=F32)
    act = (hg * jax.nn.sigmoid(hg) * hu).astype(BF16)
    ybuf[slot] = jnp.dot(act, wd_ref[0].astype(BF16), preferred_element_type=F32)

    def start_scatter(r, c):
        pltpu.make_async_copy(ybuf.at[slot, pl.ds(r, 1), :], out_hbm.at[pl.ds(idx_ref[0, 1, r], 1), :],
                              ssem.at[slot]).start()
        return c
    lax.fori_loop(0, MOE_ROWS, start_scatter, 0, unroll=8)

    @pl.when(i == n - 1)
    def _():
        wait_scatter(slot)

        @pl.when(n >= 2)
        def _():
            wait_scatter(1 - slot)


def _experts(hn, blk_e, idx, w_gate, w_up, w_down):
    n_blocks = idx.shape[0]
    p = n_blocks * MOE_ROWS
    idx_spec = lambda f: pl.BlockSpec((1, 2, MOE_ROWS), f, memory_space=pltpu.SMEM)
    grid_spec = pltpu.PrefetchScalarGridSpec(
        num_scalar_prefetch=1,
        grid=(n_blocks,),
        in_specs=[
            idx_spec(lambda i, be: (i, 0, 0)),
            idx_spec(lambda i, be: (jnp.minimum(i + 1, n_blocks - 1), 0, 0)),
            pl.BlockSpec(memory_space=pl.ANY),
            pl.BlockSpec((1, D_MODEL, D_EXPERT), lambda i, be: (be[i], 0, 0)),
            pl.BlockSpec((1, D_MODEL, D_EXPERT), lambda i, be: (be[i], 0, 0)),
            pl.BlockSpec((1, D_EXPERT, D_MODEL), lambda i, be: (be[i], 0, 0)),
        ],
        out_specs=pl.BlockSpec(memory_space=pl.ANY),
        scratch_shapes=[pltpu.VMEM((2, MOE_ROWS, D_MODEL), F32), pltpu.VMEM((2, MOE_ROWS, D_MODEL), F32),
                        pltpu.SemaphoreType.DMA((2,)), pltpu.SemaphoreType.DMA((2,))],
    )
    return pl.pallas_call(
        _expert_body,
        grid_spec=grid_spec,
        out_shape=jax.ShapeDtypeStruct((p, D_MODEL), F32),
        compiler_params=_params("arbitrary"),
    )(blk_e, idx, idx, hn, w_gate, w_up, w_down)


def _route_plan(ids):
    n_tok = ids.shape[0]
    n_pairs = ids.size
    n_blocks = -(-(n_pairs + N_EXPERTS * (MOE_ROWS - 1)) // MOE_ROWS)
    p = n_blocks * MOE_ROWS
    flat_e = ids.reshape(n_pairs)
    onehot = (flat_e[:, None] == jnp.arange(N_EXPERTS, dtype=jnp.int32)[None, :]).astype(jnp.int32)
    csum = jnp.cumsum(onehot, axis=0)
    rank = jnp.sum(onehot * csum, axis=1) - 1
    counts = csum[-1]
    pcounts = (counts + MOE_ROWS - 1) // MOE_ROWS * MOE_ROWS
    pends = jnp.cumsum(pcounts)
    pstarts = pends - pcounts
    dest = jnp.sum(onehot * pstarts[None, :], axis=1) + rank
    pair = jnp.arange(n_pairs, dtype=jnp.int32)
    row_pair = jnp.full((p,), -1, jnp.int32).at[dest].set(pair)
    is_pad = row_pair < 0
    pad_rank = jnp.cumsum(is_pad.astype(jnp.int32)) - 1
    row_tok = jnp.where(is_pad, 0, row_pair // TOP_K)
    row_dst = jnp.where(is_pad, n_pairs + pad_rank, (row_pair % TOP_K) * n_tok + row_tok)
    block_start = jnp.arange(n_blocks, dtype=jnp.int32) * MOE_ROWS
    blk_e = jnp.minimum(jnp.sum((block_start[:, None] >= pends[None, :]).astype(jnp.int32), axis=1),
                        N_EXPERTS - 1).astype(jnp.int32)
    idx = jnp.stack([row_tok.reshape(n_blocks, MOE_ROWS), row_dst.reshape(n_blocks, MOE_ROWS)], axis=1)
    return blk_e, idx.astype(jnp.int32)


def _final_body(h_ref, first_ref, second_ref, comb_ref, g_ref, o_ref):
    x = h_ref[...] + (first_ref[...] * comb_ref[:, 0:1] + second_ref[...] * comb_ref[:, 1:2])
    o_ref[...] = x * lax.rsqrt(jnp.mean(x * x, axis=-1, keepdims=True) + EPS) * g_ref[...]


def _final(h, pair_out, comb, g_final, *, row0, n_rows):
    n_tok = h.shape[0]
    assert row0 % TM == 0 and n_rows % TM == 0 and n_tok % TM == 0
    blk0, plane = row0 // TM, n_tok // TM
    row = lambda n, off: pl.BlockSpec((TM, n), lambda i: (off + i, 0))
    return pl.pallas_call(
        _final_body,
        grid=(n_rows // TM,),
        in_specs=[row(D_MODEL, blk0), row(D_MODEL, blk0), row(D_MODEL, plane + blk0), row(LANES, blk0),
                  pl.BlockSpec((1, D_MODEL), lambda i: (0, 0))],
        out_specs=pl.BlockSpec((TM, D_MODEL), lambda i: (i, 0)),
        out_shape=jax.ShapeDtypeStruct((n_rows, D_MODEL), F32),
        compiler_params=_params("parallel"),
    )(h, pair_out, pair_out, comb, g_final.reshape(1, D_MODEL))


def _prompt_key_layout(x, b, t):
    xc = x.reshape(b, t, RWKV_HEADS, RWKV_N).transpose(1, 3, 0, 2).reshape(t, RWKV_N, b * RWKV_HEADS)
    return jnp.concatenate([xc, xc], axis=-1)[None]


def _prompt_value_layout(x, b, t):
    xc = x.reshape(b, t, RWKV_HEADS, 2, RWKV_N // 2).transpose(1, 4, 3, 0, 2)
    return xc.reshape(t, RWKV_N // 2, 2 * b * RWKV_HEADS)[None]


def _prompt_value_unlayout(y, b, t):
    yc = y[0].reshape(t, RWKV_N // 2, 2, b, RWKV_HEADS).transpose(3, 0, 4, 2, 1)
    return yc.reshape(b * t, RWKV_W)


def _sample_layout(x, b, t):
    return x.reshape(b, t, RWKV_HEADS, RWKV_N).transpose(2, 1, 3, 0)


def _sample_unlayout(y, b, t):
    return y.transpose(3, 1, 0, 2).reshape(b * t, RWKV_W)


def kernel(x_prompt, x_sample, mem_prompt, state_ret, state_rwkv, state_shift, cache_mem_k, cache_mem_v,
           g_mix, w_in, ret_gn, rwkv_mu, rwkv_w0, rwkv_w2, rwkv_a0, rwkv_a2, rwkv_g2, rwkv_k_k, rwkv_k_a,
           rwkv_r_k, rwkv_lnx_w, rwkv_lnx_b, w_out, g_mem_q, g_mem_kv, w_mq, w_mk, w_mv, w_mo, g_ffn,
           w_group_router, b_group_router, w_expert_router, b_expert_router, w_e_gate, w_e_up, w_e_down,
           g_final):
    assert w_in.shape[0] == 1, "single-layer decoder"
    bp, tp, d = x_prompt.shape
    bs, ts, _ = x_sample.shape
    np_tok, ns_tok = bp * tp, bs * ts
    assert d == D_MODEL and bp * RWKV_HEADS * 2 == LANES and bs == LANES
    l = 0
    x_parts = [x_prompt.reshape(np_tok, d), x_sample.reshape(ns_tok, d)]

    (proj,) = _matmul(x_parts, [w_in[l].astype(BF16)], gain=g_mix[l])

    pos_p = np.arange(tp)
    pos_s = PAST_LEN + np.arange(ts)
    zero_ret = jnp.zeros((bp, RET_HEADS, RET_DK, RET_DV), F32)
    oret_p, sret_p = _retention(proj, zero_ret, ret_gn[l], pos_p, row0=0, n_batch=bp, t=tp, n_seq=1)
    oret_s, sret_s = _retention(proj, state_ret[l], ret_gn[l], pos_s, row0=np_tok, n_batch=bs, t=ts, n_seq=16)

    pre_w = (rwkv_mu[l], rwkv_w0[l], rwkv_w2[l], rwkv_a0[l], rwkv_a2[l], rwkv_g2[l])
    zero_shift = jnp.zeros((bp, N_RWKV_COLS), F32)
    r_p, k_p, v_p, w_p, a_p, gate_p = _rwkv_pre(proj, zero_shift, *pre_w, row0=0, n_batch=bp, t=tp,
                                                 n_seq=1, c=256)
    r_s, k_s, v_s, w_s, a_s, gate_s = _rwkv_pre(proj, state_shift[l], *pre_w, row0=np_tok, n_batch=bs, t=ts,
                                                 n_seq=16, c=ts)

    kvec = lambda v: v.reshape(RWKV_HEADS, RWKV_N)
    key_par = lambda v: jnp.broadcast_to(kvec(v).T[:, None, None, :], (RWKV_N, 2, bp, RWKV_HEADS)).reshape(
        1, RWKV_N, LANES)
    val_par = lambda v: jnp.broadcast_to(
        v.reshape(RWKV_HEADS, 2, RWKV_N // 2).transpose(2, 1, 0)[:, :, None, :],
        (RWKV_N // 2, 2, bp, RWKV_HEADS)).reshape(1, RWKV_N // 2, LANES)
    y_p, srw_p = _rwkv_scan(
        _prompt_key_layout(r_p, bp, tp), _prompt_key_layout(k_p, bp, tp), _prompt_key_layout(w_p, bp, tp),
        _prompt_key_layout(a_p, bp, tp), _prompt_value_layout(v_p, bp, tp),
        key_par(rwkv_k_k[l]), key_par(rwkv_k_a[l]), key_par(rwkv_r_k[l]),
        val_par(rwkv_lnx_w[l]), val_par(rwkv_lnx_b[l]),
        jnp.zeros((1, RWKV_N // 2, RWKV_N, LANES), F32), tc=32, halves=2)
    y_p = _prompt_value_unlayout(y_p, bp, tp)
    srw_p = srw_p[0].reshape(RWKV_N // 2, RWKV_N, 2, bp, RWKV_HEADS).transpose(3, 4, 2, 0, 1).reshape(
        bp, RWKV_HEADS, RWKV_N, RWKV_N)
    head_par = lambda v: jnp.broadcast_to(kvec(v)[:, :, None], (RWKV_HEADS, RWKV_N, LANES))
    y_s, srw_s = _rwkv_scan(
        _sample_layout(r_s, bs, ts), _sample_layout(k_s, bs, ts), _sample_layout(w_s, bs, ts),
        _sample_layout(a_s, bs, ts), _sample_layout(v_s, bs, ts),
        head_par(rwkv_k_k[l]), head_par(rwkv_k_a[l]), head_par(rwkv_r_k[l]),
        head_par(rwkv_lnx_w[l]), head_par(rwkv_lnx_b[l]),
        state_rwkv[l].astype(F32).transpose(1, 2, 3, 0), tc=ts, halves=1)
    y_s = _sample_unlayout(y_s, bs, ts)
    srw_s = srw_s.transpose(3, 0, 1, 2)

    h = _merge(x_parts, [oret_p, oret_s], [y_p, y_s], [gate_p, gate_s], w_out[l])

    mk, mv = _matmul([mem_prompt.reshape(bp * N_MEM, d)], [w_mk[l].astype(BF16), w_mv[l].astype(BF16)],
                     gain=g_mem_kv[l], head_dim=MEM_DH)
    mem_shape = (N_MEM, MEM_HEADS, MEM_DH)
    (q,) = _matmul([h], [w_mq[l].astype(BF16)], gain=g_mem_q[l])
    att_p = _attention(q, mk.reshape(bp, *mem_shape), mv.reshape(bp, *mem_shape),
                       row0=0, n_batch=bp, t=tp, n_seq=1, tq=TM)
    att_s = _attention(q, cache_mem_k.reshape(bs, *mem_shape), cache_mem_v.reshape(bs, *mem_shape),
                       row0=np_tok, n_batch=bs, t=ts, n_seq=2, tq=ts)
    (h,) = _matmul([att_p, att_s], [w_mo[l].astype(BF16)], residual=h)

    hn, ids, comb = _router(h, g_ffn[l], w_group_router[l], b_group_router[l], w_expert_router[l],
                            b_expert_router[l])
    blk_e, idx = _route_plan(ids[:, :TOP_K])
    pair_out = _experts(hn, blk_e, idx, w_e_gate[l], w_e_up[l], w_e_down[l])
    y_prompt = _final(h, pair_out, comb, g_final, row0=0, n_rows=np_tok).reshape(bp, tp, d)
    y_sample = _final(h, pair_out, comb, g_final, row0=np_tok, n_rows=ns_tok).reshape(bs, ts, d)

    shift_p = lax.slice(proj, (tp - 1, N_RET_COLS), (np_tok, N_IN_COLS), (tp, 1))
    shift_s = lax.slice(proj, (np_tok + ts - 1, N_RET_COLS), (np_tok + ns_tok, N_IN_COLS), (ts, 1))
    return (y_prompt, y_sample, sret_p[None], srw_p[None], shift_p[None],
            mk.reshape(1, bp, *mem_shape), mv.reshape(1, bp, *mem_shape),
            sret_s[None], srw_s[None], shift_s[None])
```

```python
import functools

import numpy as np
import jax
import jax.numpy as jnp
from jax import lax
from jax.experimental import pallas as pl
from jax.experimental.pallas import tpu as pltpu

F32 = jnp.float32
BF16 = jnp.bfloat16

D_MODEL = 1024
PAST_LEN = 16384
N_MEM = 256
MEM_HEADS = 4
MEM_DH = D_MODEL // MEM_HEADS
RET_HEADS = 4
RET_W = D_MODEL // 2
RET_DV = RET_W // RET_HEADS
RET_DK = RET_DV // 2
RET_QK = RET_HEADS * RET_DK
RET_CHUNK = 128
ROPE_BASE = 10000.0
RWKV_N = 64
RWKV_W = D_MODEL - RET_W
RWKV_HEADS = RWKV_W // RWKV_N
LORA_W = 64
LORA_A = 64
LORA_G = 128
LNX_EPS = 64e-5
N_RET_COLS = 2 * RET_QK + 2 * RET_W
N_RWKV_COLS = 3 * RWKV_W + LORA_W + LORA_A + LORA_G
N_IN_COLS = N_RET_COLS + N_RWKV_COLS
N_GROUPS = 4
EXP_PER_GROUP = 8
N_EXPERTS = N_GROUPS * EXP_PER_GROUP
TOP_K = 2
D_EXPERT = D_MODEL // 2
EPS = 1e-6

LANES = 128
MOE_ROWS = 128
TM = 512


def _params(*sem):
    return pltpu.CompilerParams(dimension_semantics=sem)


def _row_part_specs(parts, tm):
    specs, counts, start = [], [], 0
    for part in parts:
        nb = part.shape[0] // tm
        assert nb * tm == part.shape[0]
        specs.append(pl.BlockSpec((tm, part.shape[1]), lambda i, s=start, n=nb: (jnp.clip(i - s, 0, n - 1), 0)))
        counts.append(nb)
        start += nb
    return specs, counts


def _read_row_parts(refs, counts):
    i = pl.program_id(0)
    x = refs[0][...]
    start = counts[0]
    for ref, nb in zip(refs[1:], counts[1:]):
        x = jnp.where(i >= start, ref[...], x)
        start += nb
    return x


def _mm_body(*refs, part_counts, n_w, has_gain, has_res, n_chunk, head_dim):
    it = iter(refs)
    x_refs = [next(it) for _ in part_counts]
    g_ref = next(it) if has_gain else None
    w_refs = [next(it) for _ in range(n_w)]
    r_ref = next(it) if has_res else None
    o_refs = [next(it) for _ in range(n_w)]
    x = _read_row_parts(x_refs, part_counts).astype(F32)
    if has_gain:
        x = x * lax.rsqrt(jnp.mean(x * x, axis=-1, keepdims=True) + EPS) * g_ref[...]
    xb = x.astype(BF16)
    for w_ref, o_ref in zip(w_refs, o_refs):
        for j in range(0, w_ref.shape[1], n_chunk):
            acc = jnp.dot(xb, w_ref[:, j:j + n_chunk], preferred_element_type=F32)
            if has_res:
                acc = acc + r_ref[:, j:j + n_chunk]
            if head_dim is None:
                o_ref[:, j:j + n_chunk] = acc
            else:
                o_ref[:, j // head_dim, :] = acc


def _matmul(x_parts, ws, *, gain=None, residual=None, tm=TM, head_dim=None):
    k = x_parts[0].shape[1]
    m = sum(part.shape[0] for part in x_parts)
    n_chunk = 256 if head_dim is None else head_dim
    assert all(w.shape[1] % n_chunk == 0 for w in ws)
    assert residual is None or len(ws) == 1
    in_specs, part_counts = _row_part_specs(x_parts, tm)
    args = list(x_parts)
    if gain is not None:
        in_specs.append(pl.BlockSpec((1, k), lambda i: (0, 0)))
        args.append(gain.reshape(1, k).astype(F32))
    for w in ws:
        in_specs.append(pl.BlockSpec(w.shape, lambda i: (0, 0)))
        args.append(w)
    if residual is not None:
        in_specs.append(pl.BlockSpec((tm, ws[0].shape[1]), lambda i: (i, 0)))
        args.append(residual)
    if head_dim is None:
        out_specs = [pl.BlockSpec((tm, w.shape[1]), lambda i: (i, 0)) for w in ws]
        out_shape = [jax.ShapeDtypeStruct((m, w.shape[1]), F32) for w in ws]
    else:
        out_specs = [pl.BlockSpec((tm, w.shape[1] // head_dim, head_dim), lambda i: (i, 0, 0)) for w in ws]
        out_shape = [jax.ShapeDtypeStruct((m, w.shape[1] // head_dim, head_dim), F32) for w in ws]
    return pl.pallas_call(
        functools.partial(_mm_body, part_counts=part_counts, n_w=len(ws), has_gain=gain is not None,
                          has_res=residual is not None, n_chunk=n_chunk, head_dim=head_dim),
        grid=(m // tm,),
        in_specs=in_specs,
        out_specs=out_specs,
        out_shape=out_shape,
        compiler_params=_params("parallel"),
    )(*args)


def _rot_tables(pos):
    half = RET_DK // 2
    inv_freq = ROPE_BASE ** (-(np.arange(half, dtype=np.float64) / half))
    ang = pos.astype(np.float64)[:, None] * inv_freq[None, :]
    cos, sin = np.cos(ang), np.sin(ang)
    zero = np.zeros_like(sin)
    c = np.tile(np.concatenate([cos, cos], axis=1), (1, RET_HEADS))
    s_lo = np.tile(np.concatenate([-sin, zero], axis=1), (1, RET_HEADS))
    s_hi = np.tile(np.concatenate([zero, sin], axis=1), (1, RET_HEADS))
    return [jnp.asarray(t, F32) for t in (c, s_lo, s_hi)]


def _ret_decay_tables(c):
    lg = np.log1p(-np.exp2(-5.0 - np.arange(RET_HEADS, dtype=np.float64)))
    idx = np.arange(c, dtype=np.float64)
    diff = idx[:, None] - idx[None, :]
    mask = np.where(diff[None] >= 0, np.exp(np.maximum(diff, 0.0)[None] * lg[:, None, None]), 0.0)
    q_dec = np.repeat(np.exp((idx[:, None] + 1.0) * lg[None, :]), RET_DV, axis=1)
    k_dec = np.repeat(np.exp((c - 1.0 - idx)[:, None] * lg[None, :]), RET_DK, axis=1)
    c_dec = [float(v) for v in np.exp(c * lg)]
    return jnp.asarray(mask, F32), jnp.asarray(q_dec, F32), jnp.asarray(k_dec, F32), c_dec


def _ret_body(q_ref, k_ref, v_ref, gate_ref, c_ref, slo_ref, shi_ref, mask_ref, qdec_ref, kdec_ref,
              gn_ref, s0_ref, o_ref, sout_ref, s_scr, *, n_seq, c, c_dec):
    ci = pl.program_id(1)

    @pl.when(ci == 0)
    def _():
        s_scr[...] = s0_ref[...].astype(F32)

    cos, s_lo, s_hi = c_ref[...], slo_ref[...], shi_ref[...]
    half = RET_DK // 2

    def rope(x):
        return x * cos + pltpu.roll(x, RET_QK - half, 1) * s_lo + pltpu.roll(x, half, 1) * s_hi

    nt = (((1,), (1,)), ((), ()))
    tn = (((0,), (0,)), ((), ()))
    for g in range(n_seq):
        rows = slice(g * c, (g + 1) * c)
        q = rope(q_ref[rows, :].astype(F32))
        k = rope(k_ref[rows, :].astype(F32)) * (RET_DK ** -0.5)
        k_st = k * kdec_ref[...]
        for h in range(RET_HEADS):
            kc = slice(h * RET_DK, (h + 1) * RET_DK)
            vc = slice(h * RET_DV, (h + 1) * RET_DV)
            qh = q[:, kc].astype(BF16)
            vh = v_ref[rows, vc].astype(BF16)
            s_h = s_scr[g, h]
            att = lax.dot_general(qh, k[:, kc].astype(BF16), nt, preferred_element_type=F32) * mask_ref[h]
            o = jnp.dot(att.astype(BF16), vh, preferred_element_type=F32)
            o = o + jnp.dot(qh, s_h.astype(BF16), preferred_element_type=F32) * qdec_ref[:, vc]
            s_scr[g, h] = s_h * c_dec[h] + lax.dot_general(
                k_st[:, kc].astype(BF16), vh, tn, preferred_element_type=F32)
            o = o * lax.rsqrt(jnp.mean(o * o, axis=-1, keepdims=True) + EPS)
            gate = gate_ref[rows, vc].astype(F32)
            o_ref[rows, vc] = o * gn_ref[:, vc] * (gate * jax.nn.sigmoid(gate))

    @pl.when(ci == pl.num_programs(1) - 1)
    def _():
        sout_ref[...] = s_scr[...]


def _retention(proj, s0, ret_gn, pos, *, row0, n_batch, t, n_seq):
    c = RET_CHUNK if t % RET_CHUNK == 0 else t
    n_chunks = t // c
    rows = n_seq * c
    assert n_batch % n_seq == 0 and row0 % rows == 0 and (n_seq == 1 or n_chunks == 1)
    blk0 = row0 // rows
    mask, q_dec, k_dec, c_dec = _ret_decay_tables(c)
    cos, s_lo, s_hi = _rot_tables(pos)

    def row_map(col):
        return lambda b, ci: (blk0 + b * n_chunks + ci, col)

    def const2(b, ci):
        return (0, 0)

    state_spec = pl.BlockSpec((n_seq, RET_HEADS, RET_DK, RET_DV), lambda b, ci: (b, 0, 0, 0))
    in_specs = [
        pl.BlockSpec((rows, RET_QK), row_map(0)),
        pl.BlockSpec((rows, RET_QK), row_map(1)),
        pl.BlockSpec((rows, RET_W), row_map(1)),
        pl.BlockSpec((rows, RET_W), row_map(2)),
        pl.BlockSpec((c, RET_QK), lambda b, ci: (ci, 0)),
        pl.BlockSpec((c, RET_QK), lambda b, ci: (ci, 0)),
        pl.BlockSpec((c, RET_QK), lambda b, ci: (ci, 0)),
        pl.BlockSpec((RET_HEADS, c, c), lambda b, ci: (0, 0, 0)),
        pl.BlockSpec((c, RET_W), const2),
        pl.BlockSpec((c, RET_QK), const2),
        pl.BlockSpec((1, RET_W), const2),
        state_spec,
    ]
    return pl.pallas_call(
        functools.partial(_ret_body, n_seq=n_seq, c=c, c_dec=c_dec),
        grid=(n_batch // n_seq, n_chunks),
        in_specs=in_specs,
        out_specs=[pl.BlockSpec((rows, RET_W), lambda b, ci: (b * n_chunks + ci, 0)), state_spec],
        out_shape=[jax.ShapeDtypeStruct((n_batch * t, RET_W), F32),
                   jax.ShapeDtypeStruct((n_batch, RET_HEADS, RET_DK, RET_DV), F32)],
        scratch_shapes=[pltpu.VMEM((n_seq, RET_HEADS, RET_DK, RET_DV), F32)],
        compiler_params=_params("parallel", "arbitrary"),
    )(proj, proj, proj, proj, cos, s_lo, s_hi, mask, q_dec, k_dec, ret_gn.reshape(1, RET_W).astype(F32), s0)


def _rwkv_pre_body(r_ref, k_ref, v_ref, lo_ref, shift_ref, mu_ref, w0_ref, w2_ref, a0_ref, a2_ref, g2_ref,
                   ro_ref, ko_ref, vo_ref, wo_ref, ao_ref, go_ref, prev_scr, *, n_seq, c):
    ci = pl.program_id(1)

    @pl.when(ci == 0)
    def _():
        for g in range(n_seq):
            prev_scr[g] = shift_ref[g].astype(F32)

    first_row = lax.broadcasted_iota(jnp.int32, (c, 1), 0) == 0

    def shifted(x_ref, g, col0):
        w = x_ref.shape[1]
        x = x_ref[g * c:(g + 1) * c, :].astype(F32)
        prev_row = prev_scr[g, :, col0:col0 + w]
        prev = jnp.where(first_row, prev_row, pltpu.roll(x, 1, 0))
        prev_scr[g, :, col0:col0 + w] = x[c - 1:c, :]
        return x + (prev - x) * mu_ref[:, col0:col0 + w]

    for g in range(n_seq):
        rows = slice(g * c, (g + 1) * c)
        ro_ref[rows, :] = shifted(r_ref, g, 0)
        ko_ref[rows, :] = shifted(k_ref, g, RWKV_W)
        vo_ref[rows, :] = shifted(v_ref, g, 2 * RWKV_W)
        lo = shifted(lo_ref, g, 3 * RWKV_W)
        hw = lo[:, :LORA_W]
        ha = lo[:, LORA_W:LORA_W + LORA_A]
        hg = lo[:, LORA_W + LORA_A:]
        u = w0_ref[...] + jnp.dot(jnp.tanh(hw).astype(BF16), w2_ref[...], preferred_element_type=F32)
        wo_ref[rows, :] = jnp.exp(-float(np.exp(-0.5)) * jax.nn.sigmoid(u))
        ao_ref[rows, :] = jax.nn.sigmoid(
            a0_ref[...] + jnp.dot(ha.astype(BF16), a2_ref[...], preferred_element_type=F32))
        go_ref[rows, :] = jnp.dot(jax.nn.sigmoid(hg).astype(BF16), g2_ref[...], preferred_element_type=F32)


def _rwkv_pre(proj, s_shift, mu, w0, w2, a0, a2, g2, *, row0, n_batch, t, n_seq, c):
    n_chunks = t // c
    rows = n_seq * c
    assert t % c == 0 and n_batch % n_seq == 0 and row0 % rows == 0 and (n_seq == 1 or n_chunks == 1)
    blk0 = row0 // rows
    col_r = N_RET_COLS // RWKV_W
    lo_w = LORA_W + LORA_A + LORA_G
    col_lo = (N_RET_COLS + 3 * RWKV_W) // lo_w
    assert col_r * RWKV_W == N_RET_COLS and col_lo * lo_w == N_RET_COLS + 3 * RWKV_W

    def row_map(col):
        return lambda b, ci: (blk0 + b * n_chunks + ci, col)

    def const2(b, ci):
        return (0, 0)

    in_specs = [
        pl.BlockSpec((rows, RWKV_W), row_map(col_r)),
        pl.BlockSpec((rows, RWKV_W), row_map(col_r + 1)),
        pl.BlockSpec((rows, RWKV_W), row_map(col_r + 2)),
        pl.BlockSpec((rows, lo_w), row_map(col_lo)),
        pl.BlockSpec((n_seq, 1, N_RWKV_COLS), lambda b, ci: (b, 0, 0)),
        pl.BlockSpec((1, N_RWKV_COLS), const2),
        pl.BlockSpec((1, RWKV_W), const2),
        pl.BlockSpec((LORA_W, RWKV_W), const2),
        pl.BlockSpec((1, RWKV_W), const2),
        pl.BlockSpec((LORA_A, RWKV_W), const2),
        pl.BlockSpec((LORA_G, RWKV_W), const2),
    ]
    own = jax.ShapeDtypeStruct((n_batch * t, RWKV_W), F32)
    return pl.pallas_call(
        functools.partial(_rwkv_pre_body, n_seq=n_seq, c=c),
        grid=(n_batch // n_seq, n_chunks),
        in_specs=in_specs,
        out_specs=[pl.BlockSpec((rows, RWKV_W), lambda b, ci: (b * n_chunks + ci, 0))] * 6,
        out_shape=[own] * 6,
        scratch_shapes=[pltpu.VMEM((n_seq, 1, N_RWKV_COLS), F32)],
        compiler_params=_params("parallel", "arbitrary"),
    )(proj, proj, proj, proj, s_shift.reshape(n_batch, 1, N_RWKV_COLS),
      mu.reshape(1, -1), w0.reshape(1, -1), w2.astype(BF16), a0.reshape(1, -1),
      a2.astype(BF16), g2.astype(BF16))


def _scan_body(r_ref, k_ref, w_ref, a_ref, v_ref, kk_ref, ka_ref, rk_ref, lw_ref, lb_ref, s0_ref,
               y_ref, sout_ref, s_scr, a_scr, b_scr, km_scr, *, tc, vr, halves):
    ci = pl.program_id(1)

    @pl.when(ci == 0)
    def _():
        s_scr[...] = s0_ref[...].astype(F32)

    def ksum(x):
        return jnp.sum(x, axis=-2, keepdims=True)

    def vsum(x):
        if halves == 2:
            x2 = x.reshape(tc * vr, LANES)
            x = (x2 + pltpu.roll(x2, LANES // 2, 1)).reshape(tc, vr, LANES)
        return jnp.sum(x, axis=1, keepdims=True)

    kr = k_ref[...]
    a = a_ref[...]
    kk = kr * kk_ref[...]
    kk = kk / jnp.maximum(jnp.sqrt(ksum(kk * kk)), 1e-12)
    a_scr[...] = -kk
    b_scr[...] = kk * a
    km_scr[...] = kr * (1.0 + (a - 1.0) * ka_ref[...])

    def token(t, carry):
        r, w, avec, bvec, kmod = r_ref[t], w_ref[t], a_scr[t], b_scr[t], km_scr[t]

        def value_row(i, c2):
            s = s_scr[i]
            sa = ksum(s * avec)
            s = s * w + sa * bvec + v_ref[t, pl.ds(i, 1), :] * kmod
            s_scr[i] = s
            y_ref[t, pl.ds(i, 1), :] = ksum(s * r)
            return c2

        lax.fori_loop(0, vr, value_row, 0, unroll=4)
        return carry

    lax.fori_loop(0, tc, token, 0)

    y = y_ref[...]
    d = y - vsum(y) * (1.0 / RWKV_N)
    var = vsum(d * d) * (1.0 / RWKV_N)
    bonus = ksum(r_ref[...] * km_scr[...] * rk_ref[...])
    y_ref[...] = d * lax.rsqrt(var + LNX_EPS) * lw_ref[...] + lb_ref[...] + bonus * v_ref[...]

    @pl.when(ci == pl.num_programs(1) - 1)
    def _():
        sout_ref[...] = s_scr[...]


def _rwkv_scan(r, k, w, a, v, k_k, k_a, r_k, lnx_w, lnx_b, s0, *, tc, halves):
    n_grp, t, _, lanes = r.shape
    vr = v.shape[2]
    assert lanes == LANES and t % tc == 0 and vr * halves == RWKV_N

    def tok_spec(rows):
        return pl.BlockSpec((None, tc, rows, LANES), lambda g, ci: (g, ci, 0, 0))

    def par_spec(rows):
        return pl.BlockSpec((None, rows, LANES), lambda g, ci: (g, 0, 0))

    st_spec = pl.BlockSpec((None, vr, RWKV_N, LANES), lambda g, ci: (g, 0, 0, 0))
    key_scratch = pltpu.VMEM((tc, RWKV_N, LANES), F32)
    return pl.pallas_call(
        functools.partial(_scan_body, tc=tc, vr=vr, halves=halves),
        grid=(n_grp, t // tc),
        in_specs=[tok_spec(RWKV_N)] * 4 + [tok_spec(vr)] + [par_spec(RWKV_N)] * 3 + [par_spec(vr)] * 2 + [st_spec],
        out_specs=[tok_spec(vr), st_spec],
        out_shape=[jax.ShapeDtypeStruct((n_grp, t, vr, LANES), F32),
                   jax.ShapeDtypeStruct((n_grp, vr, RWKV_N, LANES), F32)],
        scratch_shapes=[pltpu.VMEM((vr, RWKV_N, LANES), F32), key_scratch, key_scratch, key_scratch],
        compiler_params=_params("parallel", "arbitrary"),
    )(r, k, w, a, v, k_k, k_a, r_k, lnx_w, lnx_b, s0)


def _merge_body(*refs, part_counts):
    n = len(part_counts)
    x_refs, oret_refs, y_refs, g_refs = refs[:n], refs[n:2 * n], refs[2 * n:3 * n], refs[3 * n:4 * n]
    wt_ref, wb_ref, o_ref = refs[4 * n:]
    x = _read_row_parts(x_refs, part_counts)
    yb = (_read_row_parts(y_refs, part_counts) * _read_row_parts(g_refs, part_counts)).astype(BF16)
    ob = _read_row_parts(oret_refs, part_counts).astype(BF16)
    n_chunk = 256
    for j in range(0, D_MODEL, n_chunk):
        acc = jnp.dot(ob, wt_ref[:, j:j + n_chunk], preferred_element_type=F32)
        acc = acc + jnp.dot(yb, wb_ref[:, j:j + n_chunk], preferred_element_type=F32)
        o_ref[:, j:j + n_chunk] = x[:, j:j + n_chunk] + acc


def _merge(x_parts, oret_parts, y_parts, g_parts, w_out):
    m = sum(part.shape[0] for part in x_parts)
    in_specs, part_counts = [], None
    for parts in (x_parts, oret_parts, y_parts, g_parts):
        specs, part_counts = _row_part_specs(parts, TM)
        in_specs += specs
    wspec = pl.BlockSpec((RET_W, D_MODEL), lambda i: (0, 0))
    return pl.pallas_call(
        functools.partial(_merge_body, part_counts=part_counts),
        grid=(m // TM,),
        in_specs=in_specs + [wspec, wspec],
        out_specs=pl.BlockSpec((TM, D_MODEL), lambda i: (i, 0)),
        out_shape=jax.ShapeDtypeStruct((m, D_MODEL), F32),
        compiler_params=_params("parallel"),
    )(*x_parts, *oret_parts, *y_parts, *g_parts, w_out[:RET_W].astype(BF16), w_out[RET_W:].astype(BF16))


def _attn_body(q_ref, k_hbm, v_hbm, o_ref, kbuf, vbuf, sem, *, n_seq, tq):
    b, h, qi = pl.program_id(0), pl.program_id(1), pl.program_id(2)

    def copies(g):
        seq = b * n_seq + g
        return (pltpu.make_async_copy(k_hbm.at[seq, :, h, :], kbuf.at[g], sem.at[0]),
                pltpu.make_async_copy(v_hbm.at[seq, :, h, :], vbuf.at[g], sem.at[1]))

    @pl.when(qi == 0)
    def _():
        for g in range(n_seq):
            for cp in copies(g):
                cp.start()
        for g in range(n_seq):
            for cp in copies(g):
                cp.wait()

    nt = (((1,), (1,)), ((), ()))
    for g in range(n_seq):
        rows = slice(g * tq, (g + 1) * tq)
        q = q_ref[rows, :].astype(BF16)
        s = lax.dot_general(q, kbuf[g].astype(BF16), nt, preferred_element_type=F32) * (MEM_DH ** -0.5)
        p = jnp.exp(s - jnp.max(s, axis=-1, keepdims=True))
        l = jnp.sum(p, axis=-1, keepdims=True)
        o = jnp.dot(p.astype(BF16), vbuf[g].astype(BF16), preferred_element_type=F32)
        o_ref[rows, :] = o / l


def _attention(q, mem_k, mem_v, *, row0, n_batch, t, n_seq, tq):
    q_tiles = t // tq
    rows = n_seq * tq
    assert t % tq == 0 and n_batch % n_seq == 0 and row0 % rows == 0 and (n_seq == 1 or q_tiles == 1)
    blk0 = row0 // rows
    kv_buf = pltpu.VMEM((n_seq, N_MEM, MEM_DH), F32)
    return pl.pallas_call(
        functools.partial(_attn_body, n_seq=n_seq, tq=tq),
        grid=(n_batch // n_seq, MEM_HEADS, q_tiles),
        in_specs=[pl.BlockSpec((rows, MEM_DH), lambda b, h, qi: (blk0 + b * q_tiles + qi, h)),
                  pl.BlockSpec(memory_space=pl.ANY), pl.BlockSpec(memory_space=pl.ANY)],
        out_specs=pl.BlockSpec((rows, MEM_DH), lambda b, h, qi: (b * q_tiles + qi, h)),
        out_shape=jax.ShapeDtypeStruct((n_batch * t, D_MODEL), F32),
        scratch_shapes=[kv_buf, kv_buf, pltpu.SemaphoreType.DMA((2,))],
        compiler_params=_params("arbitrary", "arbitrary", "arbitrary"),
    )(q, mem_k, mem_v)


def _router_body(h_ref, g_ref, w_ref, b_ref, hn_ref, ids_ref, comb_ref):
    x = h_ref[...]
    hn = x * lax.rsqrt(jnp.mean(x * x, axis=-1, keepdims=True) + EPS) * g_ref[...]
    hn_ref[...] = hn
    logits = jnp.dot(hn, w_ref[...], precision=lax.Precision.HIGHEST, preferred_element_type=F32) + b_ref[...]
    lane = lax.broadcasted_iota(jnp.int32, logits.shape, 1).astype(F32)
    neg = -jnp.inf

    def first_argmax(vals):
        m = jnp.max(vals, axis=-1, keepdims=True)
        return m, jnp.min(jnp.where(vals == m, lane, float(LANES)), axis=-1, keepdims=True)

    gl = jnp.where(lane < N_GROUPS, logits, neg)
    gmax, gsel = first_argmax(gl)
    pg_sel = 1.0 / jnp.sum(jnp.exp(gl - gmax), axis=-1, keepdims=True)
    e0 = N_GROUPS + gsel * EXP_PER_GROUP
    el = jnp.where((lane >= e0) & (lane < e0 + EXP_PER_GROUP), logits, neg)
    m1, i1 = first_argmax(el)
    m2, i2 = first_argmax(jnp.where(lane == i1, neg, el))
    e21 = jnp.exp(m2 - m1)
    c1 = pg_sel / (1.0 + e21)
    c2 = c1 * e21
    ids = jnp.where(lane == 0, i1 - N_GROUPS, jnp.where(lane == 1, i2 - N_GROUPS, 0.0))
    ids_ref[...] = ids.astype(jnp.int32)
    comb_ref[...] = jnp.where(lane == 0, c1, jnp.where(lane == 1, c2, 0.0))


def _router(h, g_ffn, w_gr, b_gr, w_er, b_er):
    m = h.shape[0]
    pad = LANES - N_GROUPS - N_EXPERTS
    w = jnp.concatenate([w_gr, w_er, jnp.zeros((D_MODEL, pad), F32)], axis=1)
    b = jnp.concatenate([b_gr, b_er, jnp.zeros((pad,), F32)]).reshape(1, LANES)
    row = lambda n: pl.BlockSpec((TM, n), lambda i: (i, 0))
    return pl.pallas_call(
        _router_body,
        grid=(m // TM,),
        in_specs=[row(D_MODEL), pl.BlockSpec((1, D_MODEL), lambda i: (0, 0)),
                  pl.BlockSpec((D_MODEL, LANES), lambda i: (0, 0)), pl.BlockSpec((1, LANES), lambda i: (0, 0))],
        out_specs=[row(D_MODEL), row(LANES), row(LANES)],
        out_shape=[jax.ShapeDtypeStruct((m, D_MODEL), F32), jax.ShapeDtypeStruct((m, LANES), jnp.int32),
                   jax.ShapeDtypeStruct((m, LANES), F32)],
        compiler_params=_params("parallel"),
    )(h, g_ffn.reshape(1, D_MODEL), w, b)


def _expert_body(blk_e_ref, idx_ref, idx_next_ref, x_hbm, wg_ref, wu_ref, wd_ref, out_hbm,
                 xbuf, ybuf, gsem, ssem):
    del blk_e_ref
    i = pl.program_id(0)
    n = pl.num_programs(0)
    slot = i % 2

    def start_gather(ref, sl):
        def body(r, c):
            pltpu.make_async_copy(x_hbm.at[pl.ds(ref[0, 0, r], 1), :], xbuf.at[sl, pl.ds(r, 1), :],
                                  gsem.at[sl]).start()
            return c
        lax.fori_loop(0, MOE_ROWS, body, 0, unroll=8)

    def wait_gather(sl):
        pltpu.make_async_copy(x_hbm.at[pl.ds(0, MOE_ROWS), :], xbuf.at[sl], gsem.at[sl]).wait()

    def wait_scatter(sl):
        pltpu.make_async_copy(ybuf.at[sl], out_hbm.at[pl.ds(0, MOE_ROWS), :], ssem.at[sl]).wait()

    @pl.when(i == 0)
    def _():
        start_gather(idx_ref, 0)

    @pl.when(i + 1 < n)
    def _():
        start_gather(idx_next_ref, 1 - slot)

    wait_gather(slot)

    @pl.when(i >= 2)
    def _():
        wait_scatter(slot)

    x = xbuf[slot].astype(BF16)
    hg = jnp.dot(x, wg_ref[0].astype(BF16), preferred_element_type=F32)
    hu = jnp.dot(x, wu_ref[0].astype(BF16), preferred_element_type=F32)
    act = (hg * jax.nn.sigmoid(hg) * hu).astype(BF16)
    ybuf[slot] = jnp.dot(act, wd_ref[0].astype(BF16), preferred_element_type=F32)

    def start_scatter(r, c):
        pltpu.make_async_copy(ybuf.at[slot, pl.ds(r, 1), :], out_hbm.at[pl.ds(idx_ref[0, 1, r], 1), :],
                              ssem.at[slot]).start()
        return c
    lax.fori_loop(0, MOE_ROWS, start_scatter, 0, unroll=8)

    @pl.when(i == n - 1)
    def _():
        wait_scatter(slot)

        @pl.when(n >= 2)
        def _():
            wait_scatter(1 - slot)


def _experts(hn, blk_e, idx, w_gate, w_up, w_down):
    n_blocks = idx.shape[0]
    p = n_blocks * MOE_ROWS
    idx_spec = lambda f: pl.BlockSpec((1, 2, MOE_ROWS), f, memory_space=pltpu.SMEM)
    grid_spec = pltpu.PrefetchScalarGridSpec(
        num_scalar_prefetch=1,
        grid=(n_blocks,),
        in_specs=[
            idx_spec(lambda i, be: (i, 0, 0)),
            idx_spec(lambda i, be: (jnp.minimum(i + 1, n_blocks - 1), 0, 0)),
            pl.BlockSpec(memory_space=pl.ANY),
            pl.BlockSpec((1, D_MODEL, D_EXPERT), lambda i, be: (be[i], 0, 0)),
            pl.BlockSpec((1, D_MODEL, D_EXPERT), lambda i, be: (be[i], 0, 0)),
            pl.BlockSpec((1, D_EXPERT, D_MODEL), lambda i, be: (be[i], 0, 0)),
        ],
        out_specs=pl.BlockSpec(memory_space=pl.ANY),
        scratch_shapes=[pltpu.VMEM((2, MOE_ROWS, D_MODEL), F32), pltpu.VMEM((2, MOE_ROWS, D_MODEL), F32),
                        pltpu.SemaphoreType.DMA((2,)), pltpu.SemaphoreType.DMA((2,))],
    )
    return pl.pallas_call(
        _expert_body,
        grid_spec=grid_spec,
        out_shape=jax.ShapeDtypeStruct((p, D_MODEL), F32),
        compiler_params=_params("arbitrary"),
    )(blk_e, idx, idx, hn, w_gate, w_up, w_down)


def _route_plan(ids):
    n_tok = ids.shape[0]
    n_pairs = ids.size
    n_blocks = -(-(n_pairs + N_EXPERTS * (MOE_ROWS - 1)) // MOE_ROWS)
    p = n_blocks * MOE_ROWS
    flat_e = ids.reshape(n_pairs)
    onehot = (flat_e[:, None] == jnp.arange(N_EXPERTS, dtype=jnp.int32)[None, :]).astype(jnp.int32)
    csum = jnp.cumsum(onehot, axis=0)
    rank = jnp.sum(onehot * csum, axis=1) - 1
    counts = csum[-1]
    pcounts = (counts + MOE_ROWS - 1) // MOE_ROWS * MOE_ROWS
    pends = jnp.cumsum(pcounts)
    pstarts = pends - pcounts
    dest = jnp.sum(onehot * pstarts[None, :], axis=1) + rank
    pair = jnp.arange(n_pairs, dtype=jnp.int32)
    row_pair = jnp.full((p,), -1, jnp.int32).at[dest].set(pair)
    is_pad = row_pair < 0
    pad_rank = jnp.cumsum(is_pad.astype(jnp.int32)) - 1
    row_tok = jnp.where(is_pad, 0, row_pair // TOP_K)
    row_dst = jnp.where(is_pad, n_pairs + pad_rank, (row_pair % TOP_K) * n_tok + row_tok)
    block_start = jnp.arange(n_blocks, dtype=jnp.int32) * MOE_ROWS
    blk_e = jnp.minimum(jnp.sum((block_start[:, None] >= pends[None, :]).astype(jnp.int32), axis=1),
                        N_EXPERTS - 1).astype(jnp.int32)
    idx = jnp.stack([row_tok.reshape(n_blocks, MOE_ROWS), row_dst.reshape(n_blocks, MOE_ROWS)], axis=1)
    return blk_e, idx.astype(jnp.int32)


def _final_body(h_ref, first_ref, second_ref, comb_ref, g_ref, o_ref):
    x = h_ref[...] + (first_ref[...] * comb_ref[:, 0:1] + second_ref[...] * comb_ref[:, 1:2])
    o_ref[...] = x * lax.rsqrt(jnp.mean(x * x, axis=-1, keepdims=True) + EPS) * g_ref[...]


def _final(h, pair_out, comb, g_final, *, row0, n_rows):
    n_tok = h.shape[0]
    assert row0 % TM == 0 and n_rows % TM == 0 and n_tok % TM == 0
    blk0, plane = row0 // TM, n_tok // TM
    row = lambda n, off: pl.BlockSpec((TM, n), lambda i: (off + i, 0))
    return pl.pallas_call(
        _final_body,
        grid=(n_rows // TM,),
        in_specs=[row(D_MODEL, blk0), row(D_MODEL, blk0), row(D_MODEL, plane + blk0), row(LANES, blk0),
                  pl.BlockSpec((1, D_MODEL), lambda i: (0, 0))],
        out_specs=pl.BlockSpec((TM, D_MODEL), lambda i: (i, 0)),
        out_shape=jax.ShapeDtypeStruct((n_rows, D_MODEL), F32),
        compiler_params=_params("parallel"),
    )(h, pair_out, pair_out, comb, g_final.reshape(1, D_MODEL))


def _prompt_key_layout(x, b, t):
    xc = x.reshape(b, t, RWKV_HEADS, RWKV_N).transpose(1, 3, 0, 2).reshape(t, RWKV_N, b * RWKV_HEADS)
    return jnp.concatenate([xc, xc], axis=-1)[None]


def _prompt_value_layout(x, b, t):
    xc = x.reshape(b, t, RWKV_HEADS, 2, RWKV_N // 2).transpose(1, 4, 3, 0, 2)
    return xc.reshape(t, RWKV_N // 2, 2 * b * RWKV_HEADS)[None]


def _prompt_value_unlayout(y, b, t):
    yc = y[0].reshape(t, RWKV_N // 2, 2, b, RWKV_HEADS).transpose(3, 0, 4, 2, 1)
    return yc.reshape(b * t, RWKV_W)


def _sample_layout(x, b, t):
    return x.reshape(b, t, RWKV_HEADS, RWKV_N).transpose(2, 1, 3, 0)


def _sample_unlayout(y, b, t):
    return y.transpose(3, 1, 0, 2).reshape(b * t, RWKV_W)


def kernel(x_prompt, x_sample, mem_prompt, state_ret, state_rwkv, state_shift, cache_mem_k, cache_mem_v,
           g_mix, w_in, ret_gn, rwkv_mu, rwkv_w0, rwkv_w2, rwkv_a0, rwkv_a2, rwkv_g2, rwkv_k_k, rwkv_k_a,
           rwkv_r_k, rwkv_lnx_w, rwkv_lnx_b, w_out, g_mem_q, g_mem_kv, w_mq, w_mk, w_mv, w_mo, g_ffn,
           w_group_router, b_group_router, w_expert_router, b_expert_router, w_e_gate, w_e_up, w_e_down,
           g_final):
    assert w_in.shape[0] == 1, "single-layer decoder"
    bp, tp, d = x_prompt.shape
    bs, ts, _ = x_sample.shape
    np_tok, ns_tok = bp * tp, bs * ts
    assert d == D_MODEL and bp * RWKV_HEADS * 2 == LANES and bs == LANES
    l = 0
    x_parts = [x_prompt.reshape(np_tok, d), x_sample.reshape(ns_tok, d)]

    (proj,) = _matmul(x_parts, [w_in[l].astype(BF16)], gain=g_mix[l])

    pos_p = np.arange(tp)
    pos_s = PAST_LEN + np.arange(ts)
    zero_ret = jnp.zeros((bp, RET_HEADS, RET_DK, RET_DV), F32)
    oret_p, sret_p = _retention(proj, zero_ret, ret_gn[l], pos_p, row0=0, n_batch=bp, t=tp, n_seq=1)
    oret_s, sret_s = _retention(proj, state_ret[l], ret_gn[l], pos_s, row0=np_tok, n_batch=bs, t=ts, n_seq=16)

    pre_w = (rwkv_mu[l], rwkv_w0[l], rwkv_w2[l], rwkv_a0[l], rwkv_a2[l], rwkv_g2[l])
    zero_shift = jnp.zeros((bp, N_RWKV_COLS), F32)
    r_p, k_p, v_p, w_p, a_p, gate_p = _rwkv_pre(proj, zero_shift, *pre_w, row0=0, n_batch=bp, t=tp,
                                                 n_seq=1, c=256)
    r_s, k_s, v_s, w_s, a_s, gate_s = _rwkv_pre(proj, state_shift[l], *pre_w, row0=np_tok, n_batch=bs, t=ts,
                                                 n_seq=16, c=ts)

    kvec = lambda v: v.reshape(RWKV_HEADS, RWKV_N)
    key_par = lambda v: jnp.broadcast_to(kvec(v).T[:, None, None, :], (RWKV_N, 2, bp, RWKV_HEADS)).reshape(
        1, RWKV_N, LANES)
    val_par = lambda v: jnp.broadcast_to(
        v.reshape(RWKV_HEADS, 2, RWKV_N // 2).transpose(2, 1, 0)[:, :, None, :],
        (RWKV_N // 2, 2, bp, RWKV_HEADS)).reshape(1, RWKV_N // 2, LANES)
    y_p, srw_p = _rwkv_scan(
        _prompt_key_layout(r_p, bp, tp), _prompt_key_layout(k_p, bp, tp), _prompt_key_layout(w_p, bp, tp),
        _prompt_key_layout(a_p, bp, tp), _prompt_value_layout(v_p, bp, tp),
        key_par(rwkv_k_k[l]), key_par(rwkv_k_a[l]), key_par(rwkv_r_k[l]),
        val_par(rwkv_lnx_w[l]), val_par(rwkv_lnx_b[l]),
        jnp.zeros((1, RWKV_N // 2, RWKV_N, LANES), F32), tc=32, halves=2)
    y_p = _prompt_value_unlayout(y_p, bp, tp)
    srw_p = srw_p[0].reshape(RWKV_N // 2, RWKV_N, 2, bp, RWKV_HEADS).transpose(3, 4, 2, 0, 1).reshape(
        bp, RWKV_HEADS, RWKV_N, RWKV_N)
    head_par = lambda v: jnp.broadcast_to(kvec(v)[:, :, None], (RWKV_HEADS, RWKV_N, LANES))
    y_s, srw_s = _rwkv_scan(
        _sample_layout(r_s, bs, ts), _sample_layout(k_s, bs, ts), _sample_layout(w_s, bs, ts),
        _sample_layout(a_s, bs, ts), _sample_layout(v_s, bs, ts),
        head_par(rwkv_k_k[l]), head_par(rwkv_k_a[l]), head_par(rwkv_r_k[l]),
        head_par(rwkv_lnx_w[l]), head_par(rwkv_lnx_b[l]),
        state_rwkv[l].astype(F32).transpose(1, 2, 3, 0), tc=ts, halves=1)
    y_s = _sample_unlayout(y_s, bs, ts)
    srw_s = srw_s.transpose(3, 0, 1, 2)

    h = _merge(x_parts, [oret_p, oret_s], [y_p, y_s], [gate_p, gate_s], w_out[l])

    mk, mv = _matmul([mem_prompt.reshape(bp * N_MEM, d)], [w_mk[l].astype(BF16), w_mv[l].astype(BF16)],
                     gain=g_mem_kv[l], head_dim=MEM_DH)
    mem_shape = (N_MEM, MEM_HEADS, MEM_DH)
    (q,) = _matmul([h], [w_mq[l].astype(BF16)], gain=g_mem_q[l])
    att_p = _attention(q, mk.reshape(bp, *mem_shape), mv.reshape(bp, *mem_shape),
                       row0=0, n_batch=bp, t=tp, n_seq=1, tq=TM)
    att_s = _attention(q, cache_mem_k.reshape(bs, *mem_shape), cache_mem_v.reshape(bs, *mem_shape),
                       row0=np_tok, n_batch=bs, t=ts, n_seq=16, tq=ts)
    (h,) = _matmul([att_p, att_s], [w_mo[l].astype(BF16)], residual=h)

    hn, ids, comb = _router(h, g_ffn[l], w_group_router[l], b_group_router[l], w_expert_router[l],
                            b_expert_router[l])
    blk_e, idx = _route_plan(ids[:, :TOP_K])
    pair_out = _experts(hn, blk_e, idx, w_e_gate[l], w_e_up[l], w_e_down[l])
    y_prompt = _final(h, pair_out, comb, g_final, row0=0, n_rows=np_tok).reshape(bp, tp, d)
    y_sample = _final(h, pair_out, comb, g_final, row0=np_tok, n_rows=ns_tok).reshape(bs, ts, d)

    shift_p = lax.slice(proj, (tp - 1, N_RET_COLS), (np_tok, N_IN_COLS), (tp, 1))
    shift_s = lax.slice(proj, (np_tok + ts - 1, N_RET_COLS), (np_tok + ns_tok, N_IN_COLS), (ts, 1))
    return (y_prompt, y_sample, sret_p[None], srw_p[None], shift_p[None],
            mk.reshape(1, bp, *mem_shape), mv.reshape(1, bp, *mem_shape),
            sret_s[None], srw_s[None], shift_s[None])
```

```python
import functools

import numpy as np
import jax
import jax.numpy as jnp
from jax import lax
from jax.experimental import pallas as pl
from jax.experimental.pallas import tpu as pltpu

F32 = jnp.float32
BF16 = jnp.bfloat16

D_MODEL = 1024
PAST_LEN = 16384
N_MEM = 256
MEM_HEADS = 4
MEM_DH = D_MODEL // MEM_HEADS
RET_HEADS = 4
RET_W = D_MODEL // 2
RET_DV = RET_W // RET_HEADS
RET_DK = RET_DV // 2
RET_QK = RET_HEADS * RET_DK
RET_CHUNK = 128
ROPE_BASE = 10000.0
RWKV_N = 64
RWKV_W = D_MODEL - RET_W
RWKV_HEADS = RWKV_W // RWKV_N
LORA_W = 64
LORA_A = 64
LORA_G = 128
LNX_EPS = 64e-5
N_RET_COLS = 2 * RET_QK + 2 * RET_W
N_RWKV_COLS = 3 * RWKV_W + LORA_W + LORA_A + LORA_G
N_IN_COLS = N_RET_COLS + N_RWKV_COLS
N_GROUPS = 4
EXP_PER_GROUP = 8
N_EXPERTS = N_GROUPS * EXP_PER_GROUP
TOP_K = 2
D_EXPERT = D_MODEL // 2
EPS = 1e-6

LANES = 128
MOE_ROWS = 128
TM = 512


def _params(*sem):
    return pltpu.CompilerParams(dimension_semantics=sem)


def _row_part_specs(parts, tm):
    specs, counts, start = [], [], 0
    for part in parts:
        nb = part.shape[0] // tm
        assert nb * tm == part.shape[0]
        specs.append(pl.BlockSpec((tm, part.shape[1]), lambda i, s=start, n=nb: (jnp.clip(i - s, 0, n - 1), 0)))
        counts.append(nb)
        start += nb
    return specs, counts


def _read_row_parts(refs, counts):
    i = pl.program_id(0)
    x = refs[0][...]
    start = counts[0]
    for ref, nb in zip(refs[1:], counts[1:]):
        x = jnp.where(i >= start, ref[...], x)
        start += nb
    return x


def _mm_body(*refs, part_counts, n_w, has_gain, has_res, n_chunk, head_dim):
    it = iter(refs)
    x_refs = [next(it) for _ in part_counts]
    g_ref = next(it) if has_gain else None
    w_refs = [next(it) for _ in range(n_w)]
    r_ref = next(it) if has_res else None
    o_refs = [next(it) for _ in range(n_w)]
    x = _read_row_parts(x_refs, part_counts).astype(F32)
    if has_gain:
        x = x * lax.rsqrt(jnp.mean(x * x, axis=-1, keepdims=True) + EPS) * g_ref[...]
    xb = x.astype(BF16)
    for w_ref, o_ref in zip(w_refs, o_refs):
        for j in range(0, w_ref.shape[1], n_chunk):
            acc = jnp.dot(xb, w_ref[:, j:j + n_chunk], preferred_element_type=F32)
            if has_res:
                acc = acc + r_ref[:, j:j + n_chunk]
            if head_dim is None:
                o_ref[:, j:j + n_chunk] = acc
            else:
                o_ref[:, j // head_dim, :] = acc


def _matmul(x_parts, ws, *, gain=None, residual=None, tm=TM, head_dim=None):
    k = x_parts[0].shape[1]
    m = sum(part.shape[0] for part in x_parts)
    n_chunk = 256 if head_dim is None else head_dim
    assert all(w.shape[1] % n_chunk == 0 for w in ws)
    assert residual is None or len(ws) == 1
    in_specs, part_counts = _row_part_specs(x_parts, tm)
    args = list(x_parts)
    if gain is not None:
        in_specs.append(pl.BlockSpec((1, k), lambda i: (0, 0)))
        args.append(gain.reshape(1, k).astype(F32))
    for w in ws:
        in_specs.append(pl.BlockSpec(w.shape, lambda i: (0, 0)))
        args.append(w)
    if residual is not None:
        in_specs.append(pl.BlockSpec((tm, ws[0].shape[1]), lambda i: (i, 0)))
        args.append(residual)
    if head_dim is None:
        out_specs = [pl.BlockSpec((tm, w.shape[1]), lambda i: (i, 0)) for w in ws]
        out_shape = [jax.ShapeDtypeStruct((m, w.shape[1]), F32) for w in ws]
    else:
        out_specs = [pl.BlockSpec((tm, w.shape[1] // head_dim, head_dim), lambda i: (i, 0, 0)) for w in ws]
        out_shape = [jax.ShapeDtypeStruct((m, w.shape[1] // head_dim, head_dim), F32) for w in ws]
    return pl.pallas_call(
        functools.partial(_mm_body, part_counts=part_counts, n_w=len(ws), has_gain=gain is not None,
                          has_res=residual is not None, n_chunk=n_chunk, head_dim=head_dim),
        grid=(m // tm,),
        in_specs=in_specs,
        out_specs=out_specs,
        out_shape=out_shape,
        compiler_params=_params("parallel"),
    )(*args)


def _rot_tables(pos):
    half = RET_DK // 2
    inv_freq = ROPE_BASE ** (-(np.arange(half, dtype=np.float64) / half))
    ang = pos.astype(np.float64)[:, None] * inv_freq[None, :]
    cos, sin = np.cos(ang), np.sin(ang)
    zero = np.zeros_like(sin)
    c = np.tile(np.concatenate([cos, cos], axis=1), (1, RET_HEADS))
    s_lo = np.tile(np.concatenate([-sin, zero], axis=1), (1, RET_HEADS))
    s_hi = np.tile(np.concatenate([zero, sin], axis=1), (1, RET_HEADS))
    return [jnp.asarray(t, F32) for t in (c, s_lo, s_hi)]


def _ret_decay_tables(c):
    lg = np.log1p(-np.exp2(-5.0 - np.arange(RET_HEADS, dtype=np.float64)))
    idx = np.arange(c, dtype=np.float64)
    diff = idx[:, None] - idx[None, :]
    mask = np.where(diff[None] >= 0, np.exp(np.maximum(diff, 0.0)[None] * lg[:, None, None]), 0.0)
    q_dec = np.repeat(np.exp((idx[:, None] + 1.0) * lg[None, :]), RET_DV, axis=1)
    k_dec = np.repeat(np.exp((c - 1.0 - idx)[:, None] * lg[None, :]), RET_DK, axis=1)
    c_dec = [float(v) for v in np.exp(c * lg)]
    return jnp.asarray(mask, F32), jnp.asarray(q_dec, F32), jnp.asarray(k_dec, F32), c_dec


def _ret_body(q_ref, k_ref, v_ref, gate_ref, c_ref, slo_ref, shi_ref, mask_ref, qdec_ref, kdec_ref,
              gn_ref, s0_ref, o_ref, sout_ref, s_scr, *, n_seq, c, c_dec):
    ci = pl.program_id(1)

    @pl.when(ci == 0)
    def _():
        s_scr[...] = s0_ref[...].astype(F32)

    cos, s_lo, s_hi = c_ref[...], slo_ref[...], shi_ref[...]
    half = RET_DK // 2

    def rope(x):
        return x * cos + pltpu.roll(x, RET_QK - half, 1) * s_lo + pltpu.roll(x, half, 1) * s_hi

    nt = (((1,), (1,)), ((), ()))
    tn = (((0,), (0,)), ((), ()))
    for g in range(n_seq):
        rows = slice(g * c, (g + 1) * c)
        q = rope(q_ref[rows, :].astype(F32))
        k = rope(k_ref[rows, :].astype(F32)) * (RET_DK ** -0.5)
        k_st = k * kdec_ref[...]
        for h in range(RET_HEADS):
            kc = slice(h * RET_DK, (h + 1) * RET_DK)
            vc = slice(h * RET_DV, (h + 1) * RET_DV)
            qh = q[:, kc].astype(BF16)
            vh = v_ref[rows, vc].astype(BF16)
            s_h = s_scr[g, h]
            att = lax.dot_general(qh, k[:, kc].astype(BF16), nt, preferred_element_type=F32) * mask_ref[h]
            o = jnp.dot(att.astype(BF16), vh, preferred_element_type=F32)
            o = o + jnp.dot(qh, s_h.astype(BF16), preferred_element_type=F32) * qdec_ref[:, vc]
            s_scr[g, h] = s_h * c_dec[h] + lax.dot_general(
                k_st[:, kc].astype(BF16), vh, tn, preferred_element_type=F32)
            o = o * lax.rsqrt(jnp.mean(o * o, axis=-1, keepdims=True) + EPS)
            gate = gate_ref[rows, vc].astype(F32)
            o_ref[rows, vc] = o * gn_ref[:, vc] * (gate * jax.nn.sigmoid(gate))

    @pl.when(ci == pl.num_programs(1) - 1)
    def _():
        sout_ref[...] = s_scr[...]


def _retention(proj, s0, ret_gn, pos, *, row0, n_batch, t, n_seq):
    c = RET_CHUNK if t % RET_CHUNK == 0 else t
    n_chunks = t // c
    rows = n_seq * c
    assert n_batch % n_seq == 0 and row0 % rows == 0 and (n_seq == 1 or n_chunks == 1)
    blk0 = row0 // rows
    mask, q_dec, k_dec, c_dec = _ret_decay_tables(c)
    cos, s_lo, s_hi = _rot_tables(pos)

    def row_map(col):
        return lambda b, ci: (blk0 + b * n_chunks + ci, col)

    def const2(b, ci):
        return (0, 0)

    state_spec = pl.BlockSpec((n_seq, RET_HEADS, RET_DK, RET_DV), lambda b, ci: (b, 0, 0, 0))
    in_specs = [
        pl.BlockSpec((rows, RET_QK), row_map(0)),
        pl.BlockSpec((rows, RET_QK), row_map(1)),
        pl.BlockSpec((rows, RET_W), row_map(1)),
        pl.BlockSpec((rows, RET_W), row_map(2)),
        pl.BlockSpec((c, RET_QK), lambda b, ci: (ci, 0)),
        pl.BlockSpec((c, RET_QK), lambda b, ci: (ci, 0)),
        pl.BlockSpec((c, RET_QK), lambda b, ci: (ci, 0)),
        pl.BlockSpec((RET_HEADS, c, c), lambda b, ci: (0, 0, 0)),
        pl.BlockSpec((c, RET_W), const2),
        pl.BlockSpec((c, RET_QK), const2),
        pl.BlockSpec((1, RET_W), const2),
        state_spec,
    ]
    return pl.pallas_call(
        functools.partial(_ret_body, n_seq=n_seq, c=c, c_dec=c_dec),
        grid=(n_batch // n_seq, n_chunks),
        in_specs=in_specs,
        out_specs=[pl.BlockSpec((rows, RET_W), lambda b, ci: (b * n_chunks + ci, 0)), state_spec],
        out_shape=[jax.ShapeDtypeStruct((n_batch * t, RET_W), F32),
                   jax.ShapeDtypeStruct((n_batch, RET_HEADS, RET_DK, RET_DV), F32)],
        scratch_shapes=[pltpu.VMEM((n_seq, RET_HEADS, RET_DK, RET_DV), F32)],
        compiler_params=_params("parallel", "arbitrary"),
    )(proj, proj, proj, proj, cos, s_lo, s_hi, mask, q_dec, k_dec, ret_gn.reshape(1, RET_W).astype(F32), s0)


def _rwkv_pre_body(r_ref, k_ref, v_ref, lo_ref, shift_ref, mu_ref, w0_ref, w2_ref, a0_ref, a2_ref, g2_ref,
                   ro_ref, ko_ref, vo_ref, wo_ref, ao_ref, go_ref, prev_scr, *, n_seq, c, transposed):
    ci = pl.program_id(1)

    @pl.when(ci == 0)
    def _():
        for g in range(n_seq):
            prev_scr[g] = shift_ref[g].astype(F32)

    first_row = lax.broadcasted_iota(jnp.int32, (c, 1), 0) == 0

    def shifted(x_ref, g, col0):
        w = x_ref.shape[1]
        x = x_ref[g * c:(g + 1) * c, :].astype(F32)
        prev_row = prev_scr[g, :, col0:col0 + w]
        prev = jnp.where(first_row, prev_row, pltpu.roll(x, 1, 0))
        prev_scr[g, :, col0:col0 + w] = x[c - 1:c, :]
        return x + (prev - x) * mu_ref[:, col0:col0 + w]

    def put(o_ref, g, val):
        if transposed:
            o_ref[...] = val.T
        else:
            o_ref[g * c:(g + 1) * c, :] = val

    for g in range(n_seq):
        put(ro_ref, g, shifted(r_ref, g, 0))
        put(ko_ref, g, shifted(k_ref, g, RWKV_W))
        put(vo_ref, g, shifted(v_ref, g, 2 * RWKV_W))
        lo = shifted(lo_ref, g, 3 * RWKV_W)
        hw = lo[:, :LORA_W]
        ha = lo[:, LORA_W:LORA_W + LORA_A]
        hg = lo[:, LORA_W + LORA_A:]
        u = w0_ref[...] + jnp.dot(jnp.tanh(hw).astype(BF16), w2_ref[...], preferred_element_type=F32)
        put(wo_ref, g, jnp.exp(-float(np.exp(-0.5)) * jax.nn.sigmoid(u)))
        put(ao_ref, g, jax.nn.sigmoid(
            a0_ref[...] + jnp.dot(ha.astype(BF16), a2_ref[...], preferred_element_type=F32)))
        go_ref[g * c:(g + 1) * c, :] = jnp.dot(jax.nn.sigmoid(hg).astype(BF16), g2_ref[...],
                                                preferred_element_type=F32)


def _rwkv_pre(proj, s_shift, mu, w0, w2, a0, a2, g2, *, row0, n_batch, t, n_seq, c, transposed):
    n_chunks = t // c
    rows = n_seq * c
    assert t % c == 0 and n_batch % n_seq == 0 and row0 % rows == 0 and (n_seq == 1 or n_chunks == 1)
    assert not transposed or n_seq == 1
    blk0 = row0 // rows
    col_r = N_RET_COLS // RWKV_W
    lo_w = LORA_W + LORA_A + LORA_G
    col_lo = (N_RET_COLS + 3 * RWKV_W) // lo_w
    assert col_r * RWKV_W == N_RET_COLS and col_lo * lo_w == N_RET_COLS + 3 * RWKV_W

    def row_map(col):
        return lambda b, ci: (blk0 + b * n_chunks + ci, col)

    def const2(b, ci):
        return (0, 0)

    in_specs = [
        pl.BlockSpec((rows, RWKV_W), row_map(col_r)),
        pl.BlockSpec((rows, RWKV_W), row_map(col_r + 1)),
        pl.BlockSpec((rows, RWKV_W), row_map(col_r + 2)),
        pl.BlockSpec((rows, lo_w), row_map(col_lo)),
        pl.BlockSpec((n_seq, 1, N_RWKV_COLS), lambda b, ci: (b, 0, 0)),
        pl.BlockSpec((1, N_RWKV_COLS), const2),
        pl.BlockSpec((1, RWKV_W), const2),
        pl.BlockSpec((LORA_W, RWKV_W), const2),
        pl.BlockSpec((1, RWKV_W), const2),
        pl.BlockSpec((LORA_A, RWKV_W), const2),
        pl.BlockSpec((LORA_G, RWKV_W), const2),
    ]
    nat_spec = pl.BlockSpec((rows, RWKV_W), lambda b, ci: (b * n_chunks + ci, 0))
    nat_shape = jax.ShapeDtypeStruct((n_batch * t, RWKV_W), F32)
    if transposed:
        vec_spec = pl.BlockSpec((None, RWKV_W, c), lambda b, ci: (b, 0, ci))
        vec_shape = jax.ShapeDtypeStruct((n_batch, RWKV_W, t), F32)
    else:
        vec_spec, vec_shape = nat_spec, nat_shape
    return pl.pallas_call(
        functools.partial(_rwkv_pre_body, n_seq=n_seq, c=c, transposed=transposed),
        grid=(n_batch // n_seq, n_chunks),
        in_specs=in_specs,
        out_specs=[vec_spec] * 5 + [nat_spec],
        out_shape=[vec_shape] * 5 + [nat_shape],
        scratch_shapes=[pltpu.VMEM((n_seq, 1, N_RWKV_COLS), F32)],
        compiler_params=_params("parallel", "arbitrary"),
    )(proj, proj, proj, proj, s_shift.reshape(n_batch, 1, N_RWKV_COLS),
      mu.reshape(1, -1), w0.reshape(1, -1), w2.astype(BF16), a0.reshape(1, -1),
      a2.astype(BF16), g2.astype(BF16))


def _scan_body(r_ref, k_ref, w_ref, a_ref, v_ref, kk_ref, ka_ref, rk_ref, lw_ref, lb_ref, s0_ref,
               y_ref, sout_ref, s_scr, a_scr, b_scr, km_scr, *, tc, vr, halves):
    ci = pl.program_id(1)

    @pl.when(ci == 0)
    def _():
        s_scr[...] = s0_ref[...].astype(F32)

    def ksum(x):
        return jnp.sum(x, axis=-2, keepdims=True)

    def vsum(x):
        if halves == 2:
            x2 = x.reshape(tc * vr, LANES)
            x = (x2 + pltpu.roll(x2, LANES // 2, 1)).reshape(tc, vr, LANES)
        return jnp.sum(x, axis=1, keepdims=True)

    kr = k_ref[...]
    a = a_ref[...]
    kk = kr * kk_ref[...]
    kk = kk / jnp.maximum(jnp.sqrt(ksum(kk * kk)), 1e-12)
    a_scr[...] = -kk
    b_scr[...] = kk * a
    km_scr[...] = kr * (1.0 + (a - 1.0) * ka_ref[...])

    def token(t, carry):
        r, w, avec, bvec, kmod = r_ref[t], w_ref[t], a_scr[t], b_scr[t], km_scr[t]

        def value_row(i, c2):
            s = s_scr[i]
            sa = ksum(s * avec)
            s = s * w + sa * bvec + v_ref[t, pl.ds(i, 1), :] * kmod
            s_scr[i] = s
            y_ref[t, pl.ds(i, 1), :] = ksum(s * r)
            return c2

        lax.fori_loop(0, vr, value_row, 0, unroll=4)
        return carry

    lax.fori_loop(0, tc, token, 0)

    y = y_ref[...]
    d = y - vsum(y) * (1.0 / RWKV_N)
    var = vsum(d * d) * (1.0 / RWKV_N)
    bonus = ksum(r_ref[...] * km_scr[...] * rk_ref[...])
    y_ref[...] = d * lax.rsqrt(var + LNX_EPS) * lw_ref[...] + lb_ref[...] + bonus * v_ref[...]

    @pl.when(ci == pl.num_programs(1) - 1)
    def _():
        sout_ref[...] = s_scr[...]


def _rwkv_scan(r, k, w, a, v, k_k, k_a, r_k, lnx_w, lnx_b, s0, *, tc, halves):
    n_grp, t, _, lanes = r.shape
    vr = v.shape[2]
    assert lanes == LANES and t % tc == 0 and vr * halves == RWKV_N

    def tok_spec(rows):
        return pl.BlockSpec((None, tc, rows, LANES), lambda g, ci: (g, ci, 0, 0))

    def par_spec(rows):
        return pl.BlockSpec((None, rows, LANES), lambda g, ci: (g, 0, 0))

    st_spec = pl.BlockSpec((None, vr, RWKV_N, LANES), lambda g, ci: (g, 0, 0, 0))
    key_scratch = pltpu.VMEM((tc, RWKV_N, LANES), F32)
    return pl.pallas_call(
        functools.partial(_scan_body, tc=tc, vr=vr, halves=halves),
        grid=(n_grp, t // tc),
        in_specs=[tok_spec(RWKV_N)] * 4 + [tok_spec(vr)] + [par_spec(RWKV_N)] * 3 + [par_spec(vr)] * 2 + [st_spec],
        out_specs=[tok_spec(vr), st_spec],
        out_shape=[jax.ShapeDtypeStruct((n_grp, t, vr, LANES), F32),
                   jax.ShapeDtypeStruct((n_grp, vr, RWKV_N, LANES), F32)],
        scratch_shapes=[pltpu.VMEM((vr, RWKV_N, LANES), F32), key_scratch, key_scratch, key_scratch],
        compiler_params=_params("parallel", "arbitrary"),
    )(r, k, w, a, v, k_k, k_a, r_k, lnx_w, lnx_b, s0)


SCAN_TC = 128
SCAN_SUB = 64


def _scan_prompt_body(r_ref, k_ref, w_ref, a_ref, v_ref, kk_ref, ka_ref, rk_ref, lw_ref, lb_ref, s0_ref,
                      y_ref, sout_ref, s_scr, r_c, w_c, a_c, b_c, km_c, v_c, y_c, *, n_b):
    ci = pl.program_id(0)
    vr = RWKV_N // 2
    ts = SCAN_SUB
    tile = RWKV_HEADS
    half_lanes = LANES // 2

    @pl.when(ci == 0)
    def _():
        s_scr[...] = s0_ref[...].astype(F32)

    low = lax.broadcasted_iota(jnp.int32, (ts, LANES), 1) < half_lanes

    def feature_pair_rows(x_ref, base):
        tiles = [x_ref[b, pl.ds(base + f * tile, tile), :] for f in range(2) for b in range(n_b)]
        return jnp.concatenate(tiles, axis=0).T

    def key_to_chain(x_ref, dst, t0):
        def group(g, c):
            rows = []
            for j in range(4):
                mt = feature_pair_rows(x_ref, pl.multiple_of((g * 4 + j) * 2 * tile, 2 * tile))[t0:t0 + ts]
                sw = pltpu.roll(mt, half_lanes, 1)
                rows += [jnp.where(low, mt, sw), jnp.where(low, sw, mt)]
            dst[:, pl.ds(pl.multiple_of(g * 8, 8), 8), :] = jnp.swapaxes(jnp.stack(rows, axis=0), 0, 1)
            return c
        lax.fori_loop(0, RWKV_N // 8, group, 0)

    def value_to_chain(g, c):
        rows = [feature_pair_rows(v_ref, pl.multiple_of((g * 8 + j) * 2 * tile, 2 * tile)) for j in range(8)]
        v_c[:, pl.ds(pl.multiple_of(g * 8, 8), 8), :] = jnp.swapaxes(jnp.stack(rows, axis=0), 0, 1)
        return c
    lax.fori_loop(0, vr // 8, value_to_chain, 0)

    def ksum(x):
        return jnp.sum(x, axis=-2, keepdims=True)

    for t0 in range(0, SCAN_TC, ts):
        key_to_chain(r_ref, r_c, t0)
        key_to_chain(w_ref, w_c, t0)
        key_to_chain(k_ref, km_c, t0)
        key_to_chain(a_ref, b_c, t0)

        def prep(g8, c):
            toks = pl.ds(pl.multiple_of(g8 * 8, 8), 8)
            kr = km_c[toks]
            a = b_c[toks]
            kk = kr * kk_ref[...]
            kk = kk / jnp.maximum(jnp.sqrt(ksum(kk * kk)), 1e-12)
            a_c[toks] = -kk
            b_c[toks] = kk * a
            km_c[toks] = kr * (1.0 + (a - 1.0) * ka_ref[...])
            return c
        lax.fori_loop(0, ts // 8, prep, 0)

        def token(t, carry):
            r, w, avec, bvec, kmod = r_c[t], w_c[t], a_c[t], b_c[t], km_c[t]

            def value_row(i, c2):
                s = s_scr[i]
                sa = ksum(s * avec)
                s = s * w + sa * bvec + v_c[t0 + t, pl.ds(i, 1), :] * kmod
                s_scr[i] = s
                y_c[t0 + t, pl.ds(i, 1), :] = ksum(s * r)
                return c2

            lax.fori_loop(0, vr, value_row, 0, unroll=4)
            return carry

        lax.fori_loop(0, ts, token, 0)

        def post(g8, c):
            ktoks = pl.ds(pl.multiple_of(g8 * 8, 8), 8)
            vtoks = pl.ds(pl.multiple_of(t0 + g8 * 8, 8), 8)

            def vsum(x):
                x2 = x.reshape(8 * vr, LANES)
                x2 = x2 + pltpu.roll(x2, half_lanes, 1)
                return jnp.sum(x2.reshape(8, vr, LANES), axis=1, keepdims=True)

            y = y_c[vtoks]
            d = y - vsum(y) * (1.0 / RWKV_N)
            var = vsum(d * d) * (1.0 / RWKV_N)
            bonus = ksum(r_c[ktoks] * km_c[ktoks] * rk_ref[...])
            y_c[vtoks] = d * lax.rsqrt(var + LNX_EPS) * lw_ref[...] + lb_ref[...] + bonus * v_c[vtoks]
            return c
        lax.fori_loop(0, ts // 8, post, 0)

    def value_from_chain(g, c):
        blk = jnp.swapaxes(y_c[:, pl.ds(pl.multiple_of(g * 8, 8), 8), :], 0, 1)
        for j in range(8):
            mt = blk[j].T
            base = pl.multiple_of((g * 8 + j) * 2 * tile, 2 * tile)
            for hf in range(2):
                for b in range(n_b):
                    row0 = (hf * n_b + b) * tile
                    y_ref[b, pl.ds(base + hf * tile, tile), :] = mt[row0:row0 + tile, :]
        return c
    lax.fori_loop(0, vr // 8, value_from_chain, 0)

    @pl.when(ci == pl.num_programs(0) - 1)
    def _():
        sout_ref[...] = s_scr[...]


def _rwkv_scan_prompt(r, k, w, a, v, k_k, k_a, r_k, lnx_w, lnx_b, s0):
    n_b, _, t = r.shape
    vr = RWKV_N // 2
    assert t % SCAN_TC == 0 and 2 * n_b * RWKV_HEADS == LANES
    tok_spec = pl.BlockSpec((n_b, RWKV_W, SCAN_TC), lambda ci: (0, 0, ci))
    key_par = pl.BlockSpec((RWKV_N, LANES), lambda ci: (0, 0))
    val_par = pl.BlockSpec((vr, LANES), lambda ci: (0, 0))
    st_spec = pl.BlockSpec((vr, RWKV_N, LANES), lambda ci: (0, 0, 0))
    key_chain = pltpu.VMEM((SCAN_SUB, RWKV_N, LANES), F32)
    val_chain = pltpu.VMEM((SCAN_TC, vr, LANES), F32)
    return pl.pallas_call(
        functools.partial(_scan_prompt_body, n_b=n_b),
        grid=(t // SCAN_TC,),
        in_specs=[tok_spec] * 5 + [key_par] * 3 + [val_par] * 2 + [st_spec],
        out_specs=[tok_spec, st_spec],
        out_shape=[jax.ShapeDtypeStruct((n_b, RWKV_W, t), F32),
                   jax.ShapeDtypeStruct((vr, RWKV_N, LANES), F32)],
        scratch_shapes=[pltpu.VMEM((vr, RWKV_N, LANES), F32)] + [key_chain] * 5 + [val_chain] * 2,
        compiler_params=_params("arbitrary"),
    )(r, k, w, a, v, k_k, k_a, r_k, lnx_w, lnx_b, s0)


def _merge_body(*refs, part_counts):
    n = len(part_counts)
    x_refs, oret_refs, g_refs = refs[:n], refs[n:2 * n], refs[2 * n:3 * n]
    yt_ref, ys_ref, wt_ref, wb_ref, o_ref = refs[3 * n:]
    x = _read_row_parts(x_refs, part_counts)
    y = jnp.where(pl.program_id(0) >= part_counts[0], ys_ref[...], yt_ref[...].T)
    yb = (y * _read_row_parts(g_refs, part_counts)).astype(BF16)
    ob = _read_row_parts(oret_refs, part_counts).astype(BF16)
    n_chunk = 256
    for j in range(0, D_MODEL, n_chunk):
        acc = jnp.dot(ob, wt_ref[:, j:j + n_chunk], preferred_element_type=F32)
        acc = acc + jnp.dot(yb, wb_ref[:, j:j + n_chunk], preferred_element_type=F32)
        o_ref[:, j:j + n_chunk] = x[:, j:j + n_chunk] + acc


def _merge(x_parts, oret_parts, g_parts, y_first_t, y_second, w_ret, w_rwkv):
    m = sum(part.shape[0] for part in x_parts)
    in_specs, part_counts = [], None
    for parts in (x_parts, oret_parts, g_parts):
        specs, part_counts = _row_part_specs(parts, TM)
        in_specs += specs
    assert len(part_counts) == 2
    n_first = part_counts[0]
    tiles = y_first_t.shape[2] // TM
    assert y_first_t.shape[0] * tiles == n_first
    yt_spec = pl.BlockSpec((None, RWKV_W, TM),
                           lambda i: (jnp.minimum(i, n_first - 1) // tiles, 0, jnp.minimum(i, n_first - 1) % tiles))
    ys_spec = pl.BlockSpec((TM, RWKV_W), lambda i: (jnp.clip(i - n_first, 0, part_counts[1] - 1), 0))
    wspec = pl.BlockSpec((RET_W, D_MODEL), lambda i: (0, 0))
    return pl.pallas_call(
        functools.partial(_merge_body, part_counts=part_counts),
        grid=(m // TM,),
        in_specs=in_specs + [yt_spec, ys_spec, wspec, wspec],
        out_specs=pl.BlockSpec((TM, D_MODEL), lambda i: (i, 0)),
        out_shape=jax.ShapeDtypeStruct((m, D_MODEL), F32),
        compiler_params=_params("parallel"),
    )(*x_parts, *oret_parts, *g_parts, y_first_t, y_second, w_ret.astype(BF16), w_rwkv.astype(BF16))


def _attn_body(q_ref, k_hbm, v_hbm, o_ref, kbuf, vbuf, sem, *, n_seq, tq):
    b, h, qi = pl.program_id(0), pl.program_id(1), pl.program_id(2)

    def copies(g):
        seq = b * n_seq + g
        return (pltpu.make_async_copy(k_hbm.at[seq, :, h, :], kbuf.at[g], sem.at[0]),
                pltpu.make_async_copy(v_hbm.at[seq, :, h, :], vbuf.at[g], sem.at[1]))

    @pl.when(qi == 0)
    def _():
        for g in range(n_seq):
            for cp in copies(g):
                cp.start()
        for g in range(n_seq):
            for cp in copies(g):
                cp.wait()

    nt = (((1,), (1,)), ((), ()))
    for g in range(n_seq):
        rows = slice(g * tq, (g + 1) * tq)
        q = q_ref[rows, :].astype(BF16)
        s = lax.dot_general(q, kbuf[g].astype(BF16), nt, preferred_element_type=F32) * (MEM_DH ** -0.5)
        p = jnp.exp(s - jnp.max(s, axis=-1, keepdims=True))
        l = jnp.sum(p, axis=-1, keepdims=True)
        o = jnp.dot(p.astype(BF16), vbuf[g].astype(BF16), preferred_element_type=F32)
        o_ref[rows, :] = o / l


def _attention(q, mem_k, mem_v, *, row0, n_batch, t, n_seq, tq):
    q_tiles = t // tq
    rows = n_seq * tq
    assert t % tq == 0 and n_batch % n_seq == 0 and row0 % rows == 0 and (n_seq == 1 or q_tiles == 1)
    blk0 = row0 // rows
    kv_buf = pltpu.VMEM((n_seq, N_MEM, MEM_DH), F32)
    return pl.pallas_call(
        functools.partial(_attn_body, n_seq=n_seq, tq=tq),
        grid=(n_batch // n_seq, MEM_HEADS, q_tiles),
        in_specs=[pl.BlockSpec((rows, MEM_DH), lambda b, h, qi: (blk0 + b * q_tiles + qi, h)),
                  pl.BlockSpec(memory_space=pl.ANY), pl.BlockSpec(memory_space=pl.ANY)],
        out_specs=pl.BlockSpec((rows, MEM_DH), lambda b, h, qi: (b * q_tiles + qi, h)),
        out_shape=jax.ShapeDtypeStruct((n_batch * t, D_MODEL), F32),
        scratch_shapes=[kv_buf, kv_buf, pltpu.SemaphoreType.DMA((2,))],
        compiler_params=_params("arbitrary", "arbitrary", "arbitrary"),
    )(q, mem_k, mem_v)


def _router_body(h_ref, g_ref, w_ref, b_ref, hn_ref, ids_ref, comb_ref):
    x = h_ref[...]
    hn = x * lax.rsqrt(jnp.mean(x * x, axis=-1, keepdims=True) + EPS) * g_ref[...]
    hn_ref[...] = hn
    logits = jnp.dot(hn, w_ref[...], precision=lax.Precision.HIGHEST, preferred_element_type=F32) + b_ref[...]
    lane = lax.broadcasted_iota(jnp.int32, logits.shape, 1).astype(F32)
    neg = -jnp.inf

    def first_argmax(vals):
        m = jnp.max(vals, axis=-1, keepdims=True)
        return m, jnp.min(jnp.where(vals == m, lane, float(LANES)), axis=-1, keepdims=True)

    gl = jnp.where(lane < N_GROUPS, logits, neg)
    gmax, gsel = first_argmax(gl)
    pg_sel = 1.0 / jnp.sum(jnp.exp(gl - gmax), axis=-1, keepdims=True)
    e0 = N_GROUPS + gsel * EXP_PER_GROUP
    el = jnp.where((lane >= e0) & (lane < e0 + EXP_PER_GROUP), logits, neg)
    m1, i1 = first_argmax(el)
    m2, i2 = first_argmax(jnp.where(lane == i1, neg, el))
    e21 = jnp.exp(m2 - m1)
    c1 = pg_sel / (1.0 + e21)
    c2 = c1 * e21
    ids = jnp.where(lane == 0, i1 - N_GROUPS, jnp.where(lane == 1, i2 - N_GROUPS, 0.0))
    ids_ref[...] = ids.astype(jnp.int32)
    comb_ref[...] = jnp.where(lane == 0, c1, jnp.where(lane == 1, c2, 0.0))


def _router(h, g_ffn, w_gr, b_gr, w_er, b_er):
    m = h.shape[0]
    pad = LANES - N_GROUPS - N_EXPERTS
    w = jnp.concatenate([w_gr, w_er, jnp.zeros((D_MODEL, pad), F32)], axis=1)
    b = jnp.concatenate([b_gr, b_er, jnp.zeros((pad,), F32)]).reshape(1, LANES)
    row = lambda n: pl.BlockSpec((TM, n), lambda i: (i, 0))
    return pl.pallas_call(
        _router_body,
        grid=(m // TM,),
        in_specs=[row(D_MODEL), pl.BlockSpec((1, D_MODEL), lambda i: (0, 0)),
                  pl.BlockSpec((D_MODEL, LANES), lambda i: (0, 0)), pl.BlockSpec((1, LANES), lambda i: (0, 0))],
        out_specs=[row(D_MODEL), row(LANES), row(LANES)],
        out_shape=[jax.ShapeDtypeStruct((m, D_MODEL), F32), jax.ShapeDtypeStruct((m, LANES), jnp.int32),
                   jax.ShapeDtypeStruct((m, LANES), F32)],
        compiler_params=_params("parallel"),
    )(h, g_ffn.reshape(1, D_MODEL), w, b)


def _expert_body(blk_e_ref, idx_ref, idx_next_ref, x_hbm, wg_ref, wu_ref, wd_ref, out_hbm,
                 xbuf, ybuf, gsem, ssem):
    del blk_e_ref
    i = pl.program_id(0)
    n = pl.num_programs(0)
    slot = i % 2

    def start_gather(ref, sl):
        def body(r, c):
            pltpu.make_async_copy(x_hbm.at[pl.ds(ref[0, 0, r], 1), :], xbuf.at[sl, pl.ds(r, 1), :],
                                  gsem.at[sl]).start()
            return c
        lax.fori_loop(0, MOE_ROWS, body, 0, unroll=8)

    def wait_gather(sl):
        pltpu.make_async_copy(x_hbm.at[pl.ds(0, MOE_ROWS), :], xbuf.at[sl], gsem.at[sl]).wait()

    def wait_scatter(sl):
        pltpu.make_async_copy(ybuf.at[sl], out_hbm.at[pl.ds(0, MOE_ROWS), :], ssem.at[sl]).wait()

    @pl.when(i == 0)
    def _():
        start_gather(idx_ref, 0)

    @pl.when(i + 1 < n)
    def _():
        start_gather(idx_next_ref, 1 - slot)

    wait_gather(slot)

    @pl.when(i >= 2)
    def _():
        wait_scatter(slot)

    x = xbuf[slot].astype(BF16)
    hg = jnp.dot(x, wg_ref[0].astype(BF16), preferred_element_type=F32)
    hu = jnp.dot(x, wu_ref[0].astype(BF16), preferred_element_type=F32)
    act = (hg * jax.nn.sigmoid(hg) * hu).astype(BF16)
    ybuf[slot] = jnp.dot(act, wd_ref[0].astype(BF16), preferred_element_type=F32)

    def start_scatter(r, c):
        pltpu.make_async_copy(ybuf.at[slot, pl.ds(r, 1), :], out_hbm.at[pl.ds(idx_ref[0, 1, r], 1), :],
                              ssem.at[slot]).start()
        return c
    lax.fori_loop(0, MOE_ROWS, start_scatter, 0, unroll=8)

    @pl.when(i == n - 1)
    def _():
        wait_scatter(slot)

        @pl.when(n >= 2)
        def _():
            wait_scatter(1 - slot)


def _experts(hn, blk_e, idx, w_gate, w_up, w_down):
    n_blocks = idx.shape[0]
    p = n_blocks * MOE_ROWS
    idx_spec = lambda f: pl.BlockSpec((1, 2, MOE_ROWS), f, memory_space=pltpu.SMEM)
    grid_spec = pltpu.PrefetchScalarGridSpec(
        num_scalar_prefetch=1,
        grid=(n_blocks,),
        in_specs=[
            idx_spec(lambda i, be: (i, 0, 0)),
            idx_spec(lambda i, be: (jnp.minimum(i + 1, n_blocks - 1), 0, 0)),
            pl.BlockSpec(memory_space=pl.ANY),
            pl.BlockSpec((1, D_MODEL, D_EXPERT), lambda i, be: (be[i], 0, 0)),
            pl.BlockSpec((1, D_MODEL, D_EXPERT), lambda i, be: (be[i], 0, 0)),
            pl.BlockSpec((1, D_EXPERT, D_MODEL), lambda i, be: (be[i], 0, 0)),
        ],
        out_specs=pl.BlockSpec(memory_space=pl.ANY),
        scratch_shapes=[pltpu.VMEM((2, MOE_ROWS, D_MODEL), F32), pltpu.VMEM((2, MOE_ROWS, D_MODEL), F32),
                        pltpu.SemaphoreType.DMA((2,)), pltpu.SemaphoreType.DMA((2,))],
    )
    return pl.pallas_call(
        _expert_body,
        grid_spec=grid_spec,
        out_shape=jax.ShapeDtypeStruct((p, D_MODEL), F32),
        compiler_params=_params("arbitrary"),
    )(blk_e, idx, idx, hn, w_gate, w_up, w_down)


def _route_plan(ids):
    n_tok = ids.shape[0]
    n_pairs = ids.size
    n_blocks = -(-(n_pairs + N_EXPERTS * (MOE_ROWS - 1)) // MOE_ROWS)
    p = n_blocks * MOE_ROWS
    flat_e = ids.reshape(n_pairs)
    onehot = (flat_e[:, None] == jnp.arange(N_EXPERTS, dtype=jnp.int32)[None, :]).astype(jnp.int32)
    csum = jnp.cumsum(onehot, axis=0)
    rank = jnp.sum(onehot * csum, axis=1) - 1
    counts = csum[-1]
    pcounts = (counts + MOE_ROWS - 1) // MOE_ROWS * MOE_ROWS
    pends = jnp.cumsum(pcounts)
    pstarts = pends - pcounts
    dest = jnp.sum(onehot * pstarts[None, :], axis=1) + rank
    pair = jnp.arange(n_pairs, dtype=jnp.int32)
    row_pair = jnp.full((p,), -1, jnp.int32).at[dest].set(pair)
    is_pad = row_pair < 0
    pad_rank = jnp.cumsum(is_pad.astype(jnp.int32)) - 1
    row_tok = jnp.where(is_pad, 0, row_pair // TOP_K)
    row_dst = jnp.where(is_pad, n_pairs + pad_rank, (row_pair % TOP_K) * n_tok + row_tok)
    block_start = jnp.arange(n_blocks, dtype=jnp.int32) * MOE_ROWS
    blk_e = jnp.minimum(jnp.sum((block_start[:, None] >= pends[None, :]).astype(jnp.int32), axis=1),
                        N_EXPERTS - 1).astype(jnp.int32)
    idx = jnp.stack([row_tok.reshape(n_blocks, MOE_ROWS), row_dst.reshape(n_blocks, MOE_ROWS)], axis=1)
    return blk_e, idx.astype(jnp.int32)


def _final_body(h_ref, first_ref, second_ref, comb_ref, g_ref, o_ref):
    x = h_ref[...] + (first_ref[...] * comb_ref[:, 0:1] + second_ref[...] * comb_ref[:, 1:2])
    o_ref[...] = x * lax.rsqrt(jnp.mean(x * x, axis=-1, keepdims=True) + EPS) * g_ref[...]


def _final(h, pair_out, comb, g_final, *, row0, n_rows):
    n_tok = h.shape[0]
    assert row0 % TM == 0 and n_rows % TM == 0 and n_tok % TM == 0
    blk0, plane = row0 // TM, n_tok // TM
    row = lambda n, off: pl.BlockSpec((TM, n), lambda i: (off + i, 0))
    return pl.pallas_call(
        _final_body,
        grid=(n_rows // TM,),
        in_specs=[row(D_MODEL, blk0), row(D_MODEL, blk0), row(D_MODEL, plane + blk0), row(LANES, blk0),
                  pl.BlockSpec((1, D_MODEL), lambda i: (0, 0))],
        out_specs=pl.BlockSpec((TM, D_MODEL), lambda i: (i, 0)),
        out_shape=jax.ShapeDtypeStruct((n_rows, D_MODEL), F32),
        compiler_params=_params("parallel"),
    )(h, pair_out, pair_out, comb, g_final.reshape(1, D_MODEL))


def _reorder_last(x, shape, order):
    lead = x.shape[:-1]
    n = len(lead)
    y = x.reshape(lead + shape).transpose(tuple(range(n)) + tuple(n + o for o in order))
    return y.reshape(lead + (x.shape[-1],))


HALF_N = RWKV_N // 2


def _key_major(x):
    return _reorder_last(x, (RWKV_HEADS, RWKV_N), (1, 0))


def _key_major_inv(x):
    return _reorder_last(x, (RWKV_N, RWKV_HEADS), (1, 0))


def _value_major(x):
    return _reorder_last(x, (RWKV_HEADS, 2, HALF_N), (2, 1, 0))


def _value_major_inv(x):
    return _reorder_last(x, (HALF_N, 2, RWKV_HEADS), (2, 1, 0))


def _rwkv_cols(x, key_fn, value_fn):
    return jnp.concatenate([key_fn(x[..., :RWKV_W]), key_fn(x[..., RWKV_W:2 * RWKV_W]),
                            value_fn(x[..., 2 * RWKV_W:3 * RWKV_W]), x[..., 3 * RWKV_W:]], axis=-1)


def _sample_key_layout(x, b, t):
    return x.reshape(b, t, RWKV_N, RWKV_HEADS).transpose(3, 1, 2, 0)


def _sample_value_layout(x, b, t):
    return x.reshape(b, t, HALF_N, 2, RWKV_HEADS).transpose(4, 1, 3, 2, 0).reshape(RWKV_HEADS, t, RWKV_N, b)


def _sample_value_unlayout(y, b, t):
    return y.reshape(RWKV_HEADS, t, 2, HALF_N, b).transpose(4, 1, 3, 2, 0).reshape(b * t, RWKV_W)


def kernel(x_prompt, x_sample, mem_prompt, state_ret, state_rwkv, state_shift, cache_mem_k, cache_mem_v,
           g_mix, w_in, ret_gn, rwkv_mu, rwkv_w0, rwkv_w2, rwkv_a0, rwkv_a2, rwkv_g2, rwkv_k_k, rwkv_k_a,
           rwkv_r_k, rwkv_lnx_w, rwkv_lnx_b, w_out, g_mem_q, g_mem_kv, w_mq, w_mk, w_mv, w_mo, g_ffn,
           w_group_router, b_group_router, w_expert_router, b_expert_router, w_e_gate, w_e_up, w_e_down,
           g_final):
    assert w_in.shape[0] == 1, "single-layer decoder"
    bp, tp, d = x_prompt.shape
    bs, ts, _ = x_sample.shape
    np_tok, ns_tok = bp * tp, bs * ts
    assert d == D_MODEL and bp * RWKV_HEADS * 2 == LANES and bs == LANES
    l = 0
    x_parts = [x_prompt.reshape(np_tok, d), x_sample.reshape(ns_tok, d)]

    w_in_l = jnp.concatenate([w_in[l][:, :N_RET_COLS], _rwkv_cols(w_in[l][:, N_RET_COLS:], _key_major, _value_major)],
                             axis=1)
    (proj,) = _matmul(x_parts, [w_in_l.astype(BF16)], gain=g_mix[l])

    pos_p = np.arange(tp)
    pos_s = PAST_LEN + np.arange(ts)
    zero_ret = jnp.zeros((bp, RET_HEADS, RET_DK, RET_DV), F32)
    oret_p, sret_p = _retention(proj, zero_ret, ret_gn[l], pos_p, row0=0, n_batch=bp, t=tp, n_seq=1)
    oret_s, sret_s = _retention(proj, state_ret[l], ret_gn[l], pos_s, row0=np_tok, n_batch=bs, t=ts, n_seq=16)

    pre_w = (_rwkv_cols(rwkv_mu[l], _key_major, _value_major), _key_major(rwkv_w0[l]), _key_major(rwkv_w2[l]),
             _key_major(rwkv_a0[l]), _key_major(rwkv_a2[l]), _value_major(rwkv_g2[l]))
    zero_shift = jnp.zeros((bp, N_RWKV_COLS), F32)
    shift_in = _rwkv_cols(state_shift[l], _key_major, _value_major)
    r_p, k_p, v_p, w_p, a_p, gate_p = _rwkv_pre(proj, zero_shift, *pre_w, row0=0, n_batch=bp, t=tp,
                                                 n_seq=1, c=256, transposed=True)
    r_s, k_s, v_s, w_s, a_s, gate_s = _rwkv_pre(proj, shift_in, *pre_w, row0=np_tok, n_batch=bs, t=ts,
                                                 n_seq=16, c=ts, transposed=False)

    kvec = lambda v: v.reshape(RWKV_HEADS, RWKV_N)
    key_par = lambda v: jnp.broadcast_to(kvec(v).T[:, None, None, :], (RWKV_N, 2, bp, RWKV_HEADS)).reshape(
        RWKV_N, LANES)

    val_par = lambda v: jnp.broadcast_to(
        v.reshape(RWKV_HEADS, 2, HALF_N).transpose(2, 1, 0)[:, :, None, :],
        (HALF_N, 2, bp, RWKV_HEADS)).reshape(HALF_N, LANES)
    y_p, srw_p = _rwkv_scan_prompt(
        r_p, k_p, w_p, a_p, v_p, key_par(rwkv_k_k[l]), key_par(rwkv_k_a[l]), key_par(rwkv_r_k[l]),
        val_par(rwkv_lnx_w[l]), val_par(rwkv_lnx_b[l]), jnp.zeros((HALF_N, RWKV_N, LANES), F32))
    srw_p = srw_p.reshape(HALF_N, RWKV_N, 2, bp, RWKV_HEADS).transpose(3, 4, 2, 0, 1).reshape(
        bp, RWKV_HEADS, RWKV_N, RWKV_N)
    head_par = lambda v: jnp.broadcast_to(kvec(v)[:, :, None], (RWKV_HEADS, RWKV_N, LANES))
    y_s, srw_s = _rwkv_scan(
        _sample_key_layout(r_s, bs, ts), _sample_key_layout(k_s, bs, ts), _sample_key_layout(w_s, bs, ts),
        _sample_key_layout(a_s, bs, ts), _sample_value_layout(v_s, bs, ts),
        head_par(rwkv_k_k[l]), head_par(rwkv_k_a[l]), head_par(rwkv_r_k[l]),
        head_par(rwkv_lnx_w[l]), head_par(rwkv_lnx_b[l]),
        state_rwkv[l].astype(F32).transpose(1, 2, 3, 0), tc=ts, halves=1)
    y_s = _sample_value_unlayout(y_s, bs, ts)
    srw_s = srw_s.transpose(3, 0, 1, 2)

    w_rwkv_out = _value_major(w_out[l][RET_W:].T).T
    h = _merge(x_parts, [oret_p, oret_s], [gate_p, gate_s], y_p, y_s, w_out[l][:RET_W], w_rwkv_out)

    mk, mv = _matmul([mem_prompt.reshape(bp * N_MEM, d)], [w_mk[l].astype(BF16), w_mv[l].astype(BF16)],
                     gain=g_mem_kv[l], head_dim=MEM_DH)
    mem_shape = (N_MEM, MEM_HEADS, MEM_DH)
    (q,) = _matmul([h], [w_mq[l].astype(BF16)], gain=g_mem_q[l])
    att_p = _attention(q, mk.reshape(bp, *mem_shape), mv.reshape(bp, *mem_shape),
                       row0=0, n_batch=bp, t=tp, n_seq=1, tq=TM)
    att_s = _attention(q, cache_mem_k.reshape(bs, *mem_shape), cache_mem_v.reshape(bs, *mem_shape),
                       row0=np_tok, n_batch=bs, t=ts, n_seq=16, tq=ts)
    (h,) = _matmul([att_p, att_s], [w_mo[l].astype(BF16)], residual=h)

    hn, ids, comb = _router(h, g_ffn[l], w_group_router[l], b_group_router[l], w_expert_router[l],
                            b_expert_router[l])
    blk_e, idx = _route_plan(ids[:, :TOP_K])
    pair_out = _experts(hn, blk_e, idx, w_e_gate[l], w_e_up[l], w_e_down[l])
    y_prompt = _final(h, pair_out, comb, g_final, row0=0, n_rows=np_tok).reshape(bp, tp, d)
    y_sample = _final(h, pair_out, comb, g_final, row0=np_tok, n_rows=ns_tok).reshape(bs, ts, d)

    shift_p = lax.slice(proj, (tp - 1, N_RET_COLS), (np_tok, N_IN_COLS), (tp, 1))
    shift_s = lax.slice(proj, (np_tok + ts - 1, N_RET_COLS), (np_tok + ns_tok, N_IN_COLS), (ts, 1))
    shift_p = _rwkv_cols(shift_p, _key_major_inv, _value_major_inv)
    shift_s = _rwkv_cols(shift_s, _key_major_inv, _value_major_inv)
    return (y_prompt, y_sample, sret_p[None], srw_p[None], shift_p[None],
            mk.reshape(1, bp, *mem_shape), mv.reshape(1, bp, *mem_shape),
            sret_s[None], srw_s[None], shift_s[None])
```

```python
import functools

import numpy as np
import jax
import jax.numpy as jnp
from jax import lax
from jax.experimental import pallas as pl
from jax.experimental.pallas import tpu as pltpu

F32 = jnp.float32
BF16 = jnp.bfloat16

D_MODEL = 1024
PAST_LEN = 16384
N_MEM = 256
MEM_HEADS = 4
MEM_DH = D_MODEL // MEM_HEADS
RET_HEADS = 4
RET_W = D_MODEL // 2
RET_DV = RET_W // RET_HEADS
RET_DK = RET_DV // 2
RET_QK = RET_HEADS * RET_DK
RET_CHUNK = 128
ROPE_BASE = 10000.0
RWKV_N = 64
RWKV_W = D_MODEL - RET_W
RWKV_HEADS = RWKV_W // RWKV_N
LORA_W = 64
LORA_A = 64
LORA_G = 128
LNX_EPS = 64e-5
N_RET_COLS = 2 * RET_QK + 2 * RET_W
N_RWKV_COLS = 3 * RWKV_W + LORA_W + LORA_A + LORA_G
N_IN_COLS = N_RET_COLS + N_RWKV_COLS
N_GROUPS = 4
EXP_PER_GROUP = 8
N_EXPERTS = N_GROUPS * EXP_PER_GROUP
TOP_K = 2
D_EXPERT = D_MODEL // 2
EPS = 1e-6

LANES = 128
MOE_ROWS = 128
TM = 512


def _params(*sem):
    return pltpu.CompilerParams(dimension_semantics=sem)


def _row_part_specs(parts, tm):
    specs, counts, start = [], [], 0
    for part in parts:
        nb = part.shape[0] // tm
        assert nb * tm == part.shape[0]
        specs.append(pl.BlockSpec((tm, part.shape[1]), lambda i, s=start, n=nb: (jnp.clip(i - s, 0, n - 1), 0)))
        counts.append(nb)
        start += nb
    return specs, counts


def _read_row_parts(refs, counts):
    i = pl.program_id(0)
    x = refs[0][...]
    start = counts[0]
    for ref, nb in zip(refs[1:], counts[1:]):
        x = jnp.where(i >= start, ref[...], x)
        start += nb
    return x


def _mm_body(*refs, part_counts, n_w, has_gain, has_res, n_chunk, head_dim):
    it = iter(refs)
    x_refs = [next(it) for _ in part_counts]
    g_ref = next(it) if has_gain else None
    w_refs = [next(it) for _ in range(n_w)]
    r_ref = next(it) if has_res else None
    o_refs = [next(it) for _ in range(n_w)]
    x = _read_row_parts(x_refs, part_counts).astype(F32)
    if has_gain:
        x = x * lax.rsqrt(jnp.mean(x * x, axis=-1, keepdims=True) + EPS) * g_ref[...]
    xb = x.astype(BF16)
    for w_ref, o_ref in zip(w_refs, o_refs):
        for j in range(0, w_ref.shape[1], n_chunk):
            acc = jnp.dot(xb, w_ref[:, j:j + n_chunk], preferred_element_type=F32)
            if has_res:
                acc = acc + r_ref[:, j:j + n_chunk]
            if head_dim is None:
                o_ref[:, j:j + n_chunk] = acc
            else:
                o_ref[:, j // head_dim, :] = acc


def _matmul(x_parts, ws, *, gain=None, residual=None, tm=TM, head_dim=None):
    k = x_parts[0].shape[1]
    m = sum(part.shape[0] for part in x_parts)
    n_chunk = 256 if head_dim is None else head_dim
    assert all(w.shape[1] % n_chunk == 0 for w in ws)
    assert residual is None or len(ws) == 1
    in_specs, part_counts = _row_part_specs(x_parts, tm)
    args = list(x_parts)
    if gain is not None:
        in_specs.append(pl.BlockSpec((1, k), lambda i: (0, 0)))
        args.append(gain.reshape(1, k).astype(F32))
    for w in ws:
        in_specs.append(pl.BlockSpec(w.shape, lambda i: (0, 0)))
        args.append(w)
    if residual is not None:
        in_specs.append(pl.BlockSpec((tm, ws[0].shape[1]), lambda i: (i, 0)))
        args.append(residual)
    if head_dim is None:
        out_specs = [pl.BlockSpec((tm, w.shape[1]), lambda i: (i, 0)) for w in ws]
        out_shape = [jax.ShapeDtypeStruct((m, w.shape[1]), F32) for w in ws]
    else:
        out_specs = [pl.BlockSpec((tm, w.shape[1] // head_dim, head_dim), lambda i: (i, 0, 0)) for w in ws]
        out_shape = [jax.ShapeDtypeStruct((m, w.shape[1] // head_dim, head_dim), F32) for w in ws]
    return pl.pallas_call(
        functools.partial(_mm_body, part_counts=part_counts, n_w=len(ws), has_gain=gain is not None,
                          has_res=residual is not None, n_chunk=n_chunk, head_dim=head_dim),
        grid=(m // tm,),
        in_specs=in_specs,
        out_specs=out_specs,
        out_shape=out_shape,
        compiler_params=_params("parallel"),
    )(*args)


def _rot_tables(pos):
    half = RET_DK // 2
    inv_freq = ROPE_BASE ** (-(np.arange(half, dtype=np.float64) / half))
    ang = pos.astype(np.float64)[:, None] * inv_freq[None, :]
    cos, sin = np.cos(ang), np.sin(ang)
    zero = np.zeros_like(sin)
    c = np.tile(np.concatenate([cos, cos], axis=1), (1, RET_HEADS))
    s_lo = np.tile(np.concatenate([-sin, zero], axis=1), (1, RET_HEADS))
    s_hi = np.tile(np.concatenate([zero, sin], axis=1), (1, RET_HEADS))
    return [jnp.asarray(t, F32) for t in (c, s_lo, s_hi)]


def _ret_decay_tables(c):
    lg = np.log1p(-np.exp2(-5.0 - np.arange(RET_HEADS, dtype=np.float64)))
    idx = np.arange(c, dtype=np.float64)
    diff = idx[:, None] - idx[None, :]
    mask = np.where(diff[None] >= 0, np.exp(np.maximum(diff, 0.0)[None] * lg[:, None, None]), 0.0)
    q_dec = np.repeat(np.exp((idx[:, None] + 1.0) * lg[None, :]), RET_DV, axis=1)
    k_dec = np.repeat(np.exp((c - 1.0 - idx)[:, None] * lg[None, :]), RET_DK, axis=1)
    c_dec = [float(v) for v in np.exp(c * lg)]
    return jnp.asarray(mask, F32), jnp.asarray(q_dec, F32), jnp.asarray(k_dec, F32), c_dec


def _ret_body(q_ref, k_ref, v_ref, gate_ref, c_ref, slo_ref, shi_ref, mask_ref, qdec_ref, kdec_ref,
              gn_ref, s0_ref, o_ref, sout_ref, s_scr, *, n_seq, c, c_dec):
    ci = pl.program_id(1)

    @pl.when(ci == 0)
    def _():
        s_scr[...] = s0_ref[...].astype(F32)

    cos, s_lo, s_hi = c_ref[...], slo_ref[...], shi_ref[...]
    half = RET_DK // 2

    def rope(x):
        return x * cos + pltpu.roll(x, RET_QK - half, 1) * s_lo + pltpu.roll(x, half, 1) * s_hi

    nt = (((1,), (1,)), ((), ()))
    tn = (((0,), (0,)), ((), ()))
    for g in range(n_seq):
        rows = slice(g * c, (g + 1) * c)
        q = rope(q_ref[rows, :].astype(F32))
        k = rope(k_ref[rows, :].astype(F32)) * (RET_DK ** -0.5)
        k_st = k * kdec_ref[...]
        for h in range(RET_HEADS):
            kc = slice(h * RET_DK, (h + 1) * RET_DK)
            vc = slice(h * RET_DV, (h + 1) * RET_DV)
            qh = q[:, kc].astype(BF16)
            vh = v_ref[rows, vc].astype(BF16)
            s_h = s_scr[g, h]
            att = lax.dot_general(qh, k[:, kc].astype(BF16), nt, preferred_element_type=F32) * mask_ref[h]
            o = jnp.dot(att.astype(BF16), vh, preferred_element_type=F32)
            o = o + jnp.dot(qh, s_h.astype(BF16), preferred_element_type=F32) * qdec_ref[:, vc]
            s_scr[g, h] = s_h * c_dec[h] + lax.dot_general(
                k_st[:, kc].astype(BF16), vh, tn, preferred_element_type=F32)
            o = o * lax.rsqrt(jnp.mean(o * o, axis=-1, keepdims=True) + EPS)
            gate = gate_ref[rows, vc].astype(F32)
            o_ref[rows, vc] = o * gn_ref[:, vc] * (gate * jax.nn.sigmoid(gate))

    @pl.when(ci == pl.num_programs(1) - 1)
    def _():
        sout_ref[...] = s_scr[...]


def _retention(proj, s0, ret_gn, pos, *, row0, n_batch, t, n_seq):
    c = RET_CHUNK if t % RET_CHUNK == 0 else t
    n_chunks = t // c
    rows = n_seq * c
    assert n_batch % n_seq == 0 and row0 % rows == 0 and (n_seq == 1 or n_chunks == 1)
    blk0 = row0 // rows
    mask, q_dec, k_dec, c_dec = _ret_decay_tables(c)
    cos, s_lo, s_hi = _rot_tables(pos)

    def row_map(col):
        return lambda b, ci: (blk0 + b * n_chunks + ci, col)

    def const2(b, ci):
        return (0, 0)

    state_spec = pl.BlockSpec((n_seq, RET_HEADS, RET_DK, RET_DV), lambda b, ci: (b, 0, 0, 0))
    in_specs = [
        pl.BlockSpec((rows, RET_QK), row_map(0)),
        pl.BlockSpec((rows, RET_QK), row_map(1)),
        pl.BlockSpec((rows, RET_W), row_map(1)),
        pl.BlockSpec((rows, RET_W), row_map(2)),
        pl.BlockSpec((c, RET_QK), lambda b, ci: (ci, 0)),
        pl.BlockSpec((c, RET_QK), lambda b, ci: (ci, 0)),
        pl.BlockSpec((c, RET_QK), lambda b, ci: (ci, 0)),
        pl.BlockSpec((RET_HEADS, c, c), lambda b, ci: (0, 0, 0)),
        pl.BlockSpec((c, RET_W), const2),
        pl.BlockSpec((c, RET_QK), const2),
        pl.BlockSpec((1, RET_W), const2),
        state_spec,
    ]
    return pl.pallas_call(
        functools.partial(_ret_body, n_seq=n_seq, c=c, c_dec=c_dec),
        grid=(n_batch // n_seq, n_chunks),
        in_specs=in_specs,
        out_specs=[pl.BlockSpec((rows, RET_W), lambda b, ci: (b * n_chunks + ci, 0)), state_spec],
        out_shape=[jax.ShapeDtypeStruct((n_batch * t, RET_W), F32),
                   jax.ShapeDtypeStruct((n_batch, RET_HEADS, RET_DK, RET_DV), F32)],
        scratch_shapes=[pltpu.VMEM((n_seq, RET_HEADS, RET_DK, RET_DV), F32)],
        compiler_params=_params("parallel", "arbitrary"),
    )(proj, proj, proj, proj, cos, s_lo, s_hi, mask, q_dec, k_dec, ret_gn.reshape(1, RET_W).astype(F32), s0)


def _rwkv_pre_body(r_ref, k_ref, v_ref, lo_ref, shift_ref, mu_ref, w0_ref, w2_ref, a0_ref, a2_ref, g2_ref,
                   ro_ref, ko_ref, vo_ref, wo_ref, ao_ref, go_ref, prev_scr, *, n_seq, c, transposed):
    ci = pl.program_id(1)

    @pl.when(ci == 0)
    def _():
        for g in range(n_seq):
            prev_scr[g] = shift_ref[g].astype(F32)

    first_row = lax.broadcasted_iota(jnp.int32, (c, 1), 0) == 0

    def shifted(x_ref, g, col0):
        w = x_ref.shape[1]
        x = x_ref[g * c:(g + 1) * c, :].astype(F32)
        prev_row = prev_scr[g, :, col0:col0 + w]
        prev = jnp.where(first_row, prev_row, pltpu.roll(x, 1, 0))
        prev_scr[g, :, col0:col0 + w] = x[c - 1:c, :]
        return x + (prev - x) * mu_ref[:, col0:col0 + w]

    def put(o_ref, g, val):
        if transposed:
            o_ref[...] = val.T
        else:
            o_ref[g * c:(g + 1) * c, :] = val

    for g in range(n_seq):
        put(ro_ref, g, shifted(r_ref, g, 0))
        put(ko_ref, g, shifted(k_ref, g, RWKV_W))
        put(vo_ref, g, shifted(v_ref, g, 2 * RWKV_W))
        lo = shifted(lo_ref, g, 3 * RWKV_W)
        hw = lo[:, :LORA_W]
        ha = lo[:, LORA_W:LORA_W + LORA_A]
        hg = lo[:, LORA_W + LORA_A:]
        u = w0_ref[...] + jnp.dot(jnp.tanh(hw).astype(BF16), w2_ref[...], preferred_element_type=F32)
        put(wo_ref, g, jnp.exp(-float(np.exp(-0.5)) * jax.nn.sigmoid(u)))
        put(ao_ref, g, jax.nn.sigmoid(
            a0_ref[...] + jnp.dot(ha.astype(BF16), a2_ref[...], preferred_element_type=F32)))
        go_ref[g * c:(g + 1) * c, :] = jnp.dot(jax.nn.sigmoid(hg).astype(BF16), g2_ref[...],
                                                preferred_element_type=F32)


def _rwkv_pre(proj, s_shift, mu, w0, w2, a0, a2, g2, *, row0, n_batch, t, n_seq, c, transposed):
    n_chunks = t // c
    rows = n_seq * c
    assert t % c == 0 and n_batch % n_seq == 0 and row0 % rows == 0 and (n_seq == 1 or n_chunks == 1)
    assert not transposed or n_seq == 1
    blk0 = row0 // rows
    col_r = N_RET_COLS // RWKV_W
    lo_w = LORA_W + LORA_A + LORA_G
    col_lo = (N_RET_COLS + 3 * RWKV_W) // lo_w
    assert col_r * RWKV_W == N_RET_COLS and col_lo * lo_w == N_RET_COLS + 3 * RWKV_W

    def row_map(col):
        return lambda b, ci: (blk0 + b * n_chunks + ci, col)

    def const2(b, ci):
        return (0, 0)

    in_specs = [
        pl.BlockSpec((rows, RWKV_W), row_map(col_r)),
        pl.BlockSpec((rows, RWKV_W), row_map(col_r + 1)),
        pl.BlockSpec((rows, RWKV_W), row_map(col_r + 2)),
        pl.BlockSpec((rows, lo_w), row_map(col_lo)),
        pl.BlockSpec((n_seq, 1, N_RWKV_COLS), lambda b, ci: (b, 0, 0)),
        pl.BlockSpec((1, N_RWKV_COLS), const2),
        pl.BlockSpec((1, RWKV_W), const2),
        pl.BlockSpec((LORA_W, RWKV_W), const2),
        pl.BlockSpec((1, RWKV_W), const2),
        pl.BlockSpec((LORA_A, RWKV_W), const2),
        pl.BlockSpec((LORA_G, RWKV_W), const2),
    ]
    nat_spec = pl.BlockSpec((rows, RWKV_W), lambda b, ci: (b * n_chunks + ci, 0))
    nat_shape = jax.ShapeDtypeStruct((n_batch * t, RWKV_W), F32)
    if transposed:
        vec_spec = pl.BlockSpec((None, RWKV_W, c), lambda b, ci: (b, 0, ci))
        vec_shape = jax.ShapeDtypeStruct((n_batch, RWKV_W, t), F32)
    else:
        vec_spec, vec_shape = nat_spec, nat_shape
    return pl.pallas_call(
        functools.partial(_rwkv_pre_body, n_seq=n_seq, c=c, transposed=transposed),
        grid=(n_batch // n_seq, n_chunks),
        in_specs=in_specs,
        out_specs=[vec_spec] * 5 + [nat_spec],
        out_shape=[vec_shape] * 5 + [nat_shape],
        scratch_shapes=[pltpu.VMEM((n_seq, 1, N_RWKV_COLS), F32)],
        compiler_params=_params("parallel", "arbitrary"),
    )(proj, proj, proj, proj, s_shift.reshape(n_batch, 1, N_RWKV_COLS),
      mu.reshape(1, -1), w0.reshape(1, -1), w2.astype(BF16), a0.reshape(1, -1),
      a2.astype(BF16), g2.astype(BF16))


def _scan_body(r_ref, k_ref, w_ref, a_ref, v_ref, kk_ref, ka_ref, rk_ref, lw_ref, lb_ref, s0_ref,
               y_ref, sout_ref, s_scr, a_scr, b_scr, km_scr, *, tc, vr, halves):
    ci = pl.program_id(1)

    @pl.when(ci == 0)
    def _():
        s_scr[...] = s0_ref[...].astype(F32)

    def ksum(x):
        return jnp.sum(x, axis=-2, keepdims=True)

    def vsum(x):
        if halves == 2:
            x2 = x.reshape(tc * vr, LANES)
            x = (x2 + pltpu.roll(x2, LANES // 2, 1)).reshape(tc, vr, LANES)
        return jnp.sum(x, axis=1, keepdims=True)

    kr = k_ref[...]
    a = a_ref[...]
    kk = kr * kk_ref[...]
    kk = kk / jnp.maximum(jnp.sqrt(ksum(kk * kk)), 1e-12)
    a_scr[...] = -kk
    b_scr[...] = kk * a
    km_scr[...] = kr * (1.0 + (a - 1.0) * ka_ref[...])

    def token(t, carry):
        r, w, avec, bvec, kmod = r_ref[t], w_ref[t], a_scr[t], b_scr[t], km_scr[t]

        def value_row(i, c2):
            s = s_scr[i]
            sa = ksum(s * avec)
            s = s * w + sa * bvec + v_ref[t, pl.ds(i, 1), :] * kmod
            s_scr[i] = s
            y_ref[t, pl.ds(i, 1), :] = ksum(s * r)
            return c2

        lax.fori_loop(0, vr, value_row, 0, unroll=16)
        return carry

    lax.fori_loop(0, tc, token, 0)

    y = y_ref[...]
    d = y - vsum(y) * (1.0 / RWKV_N)
    var = vsum(d * d) * (1.0 / RWKV_N)
    bonus = ksum(r_ref[...] * km_scr[...] * rk_ref[...])
    y_ref[...] = d * lax.rsqrt(var + LNX_EPS) * lw_ref[...] + lb_ref[...] + bonus * v_ref[...]

    @pl.when(ci == pl.num_programs(1) - 1)
    def _():
        sout_ref[...] = s_scr[...]


def _rwkv_scan(r, k, w, a, v, k_k, k_a, r_k, lnx_w, lnx_b, s0, *, tc, halves):
    n_grp, t, _, lanes = r.shape
    vr = v.shape[2]
    assert lanes == LANES and t % tc == 0 and vr * halves == RWKV_N

    def tok_spec(rows):
        return pl.BlockSpec((None, tc, rows, LANES), lambda g, ci: (g, ci, 0, 0))

    def par_spec(rows):
        return pl.BlockSpec((None, rows, LANES), lambda g, ci: (g, 0, 0))

    st_spec = pl.BlockSpec((None, vr, RWKV_N, LANES), lambda g, ci: (g, 0, 0, 0))
    key_scratch = pltpu.VMEM((tc, RWKV_N, LANES), F32)
    return pl.pallas_call(
        functools.partial(_scan_body, tc=tc, vr=vr, halves=halves),
        grid=(n_grp, t // tc),
        in_specs=[tok_spec(RWKV_N)] * 4 + [tok_spec(vr)] + [par_spec(RWKV_N)] * 3 + [par_spec(vr)] * 2 + [st_spec],
        out_specs=[tok_spec(vr), st_spec],
        out_shape=[jax.ShapeDtypeStruct((n_grp, t, vr, LANES), F32),
                   jax.ShapeDtypeStruct((n_grp, vr, RWKV_N, LANES), F32)],
        scratch_shapes=[pltpu.VMEM((vr, RWKV_N, LANES), F32), key_scratch, key_scratch, key_scratch],
        compiler_params=_params("parallel", "arbitrary"),
    )(r, k, w, a, v, k_k, k_a, r_k, lnx_w, lnx_b, s0)


SCAN_TC = 128
SCAN_SUB = 64


def _scan_prompt_body(r_ref, k_ref, w_ref, a_ref, v_ref, kk_ref, ka_ref, rk_ref, lw_ref, lb_ref, s0_ref,
                      y_ref, sout_ref, s_scr, r_c, w_c, a_c, b_c, km_c, v_c, y_c, *, n_b):
    ci = pl.program_id(0)
    vr = RWKV_N // 2
    ts = SCAN_SUB
    tile = RWKV_HEADS
    half_lanes = LANES // 2

    @pl.when(ci == 0)
    def _():
        s_scr[...] = s0_ref[...].astype(F32)

    low = lax.broadcasted_iota(jnp.int32, (ts, LANES), 1) < half_lanes

    def feature_pair_rows(x_ref, base):
        tiles = [x_ref[b, pl.ds(base + f * tile, tile), :] for f in range(2) for b in range(n_b)]
        return jnp.concatenate(tiles, axis=0).T

    def key_to_chain(x_ref, dst, t0):
        def group(g, c):
            rows = []
            for j in range(4):
                mt = feature_pair_rows(x_ref, pl.multiple_of((g * 4 + j) * 2 * tile, 2 * tile))[t0:t0 + ts]
                sw = pltpu.roll(mt, half_lanes, 1)
                rows += [jnp.where(low, mt, sw), jnp.where(low, sw, mt)]
            dst[:, pl.ds(pl.multiple_of(g * 8, 8), 8), :] = jnp.swapaxes(jnp.stack(rows, axis=0), 0, 1)
            return c
        lax.fori_loop(0, RWKV_N // 8, group, 0)

    def value_to_chain(g, c):
        rows = [feature_pair_rows(v_ref, pl.multiple_of((g * 8 + j) * 2 * tile, 2 * tile)) for j in range(8)]
        v_c[:, pl.ds(pl.multiple_of(g * 8, 8), 8), :] = jnp.swapaxes(jnp.stack(rows, axis=0), 0, 1)
        return c
    lax.fori_loop(0, vr // 8, value_to_chain, 0)

    def ksum(x):
        return jnp.sum(x, axis=-2, keepdims=True)

    for t0 in range(0, SCAN_TC, ts):
        key_to_chain(r_ref, r_c, t0)
        key_to_chain(w_ref, w_c, t0)
        key_to_chain(k_ref, km_c, t0)
        key_to_chain(a_ref, b_c, t0)

        def prep(g8, c):
            toks = pl.ds(pl.multiple_of(g8 * 8, 8), 8)
            kr = km_c[toks]
            a = b_c[toks]
            kk = kr * kk_ref[...]
            kk = kk / jnp.maximum(jnp.sqrt(ksum(kk * kk)), 1e-12)
            a_c[toks] = -kk
            b_c[toks] = kk * a
            km_c[toks] = kr * (1.0 + (a - 1.0) * ka_ref[...])
            return c
        lax.fori_loop(0, ts // 8, prep, 0)

        def token(t, carry):
            r, w, avec, bvec, kmod = r_c[t], w_c[t], a_c[t], b_c[t], km_c[t]

            def value_row(i, c2):
                s = s_scr[i]
                sa = ksum(s * avec)
                s = s * w + sa * bvec + v_c[t0 + t, pl.ds(i, 1), :] * kmod
                s_scr[i] = s
                y_c[t0 + t, pl.ds(i, 1), :] = ksum(s * r)
                return c2

            lax.fori_loop(0, vr, value_row, 0, unroll=True)
            return carry

        lax.fori_loop(0, ts, token, 0)

        def post(g8, c):
            ktoks = pl.ds(pl.multiple_of(g8 * 8, 8), 8)
            vtoks = pl.ds(pl.multiple_of(t0 + g8 * 8, 8), 8)

            def vsum(x):
                x2 = x.reshape(8 * vr, LANES)
                x2 = x2 + pltpu.roll(x2, half_lanes, 1)
                return jnp.sum(x2.reshape(8, vr, LANES), axis=1, keepdims=True)

            y = y_c[vtoks]
            d = y - vsum(y) * (1.0 / RWKV_N)
            var = vsum(d * d) * (1.0 / RWKV_N)
            bonus = ksum(r_c[ktoks] * km_c[ktoks] * rk_ref[...])
            y_c[vtoks] = d * lax.rsqrt(var + LNX_EPS) * lw_ref[...] + lb_ref[...] + bonus * v_c[vtoks]
            return c
        lax.fori_loop(0, ts // 8, post, 0)

    def value_from_chain(g, c):
        blk = jnp.swapaxes(y_c[:, pl.ds(pl.multiple_of(g * 8, 8), 8), :], 0, 1)
        for j in range(8):
            mt = blk[j].T
            base = pl.multiple_of((g * 8 + j) * 2 * tile, 2 * tile)
            for hf in range(2):
                for b in range(n_b):
                    row0 = (hf * n_b + b) * tile
                    y_ref[b, pl.ds(base + hf * tile, tile), :] = mt[row0:row0 + tile, :]
        return c
    lax.fori_loop(0, vr // 8, value_from_chain, 0)

    @pl.when(ci == pl.num_programs(0) - 1)
    def _():
        sout_ref[...] = s_scr[...]


def _rwkv_scan_prompt(r, k, w, a, v, k_k, k_a, r_k, lnx_w, lnx_b, s0):
    n_b, _, t = r.shape
    vr = RWKV_N // 2
    assert t % SCAN_TC == 0 and 2 * n_b * RWKV_HEADS == LANES
    tok_spec = pl.BlockSpec((n_b, RWKV_W, SCAN_TC), lambda ci: (0, 0, ci))
    key_par = pl.BlockSpec((RWKV_N, LANES), lambda ci: (0, 0))
    val_par = pl.BlockSpec((vr, LANES), lambda ci: (0, 0))
    st_spec = pl.BlockSpec((vr, RWKV_N, LANES), lambda ci: (0, 0, 0))
    key_chain = pltpu.VMEM((SCAN_SUB, RWKV_N, LANES), F32)
    val_chain = pltpu.VMEM((SCAN_TC, vr, LANES), F32)
    return pl.pallas_call(
        functools.partial(_scan_prompt_body, n_b=n_b),
        grid=(t // SCAN_TC,),
        in_specs=[tok_spec] * 5 + [key_par] * 3 + [val_par] * 2 + [st_spec],
        out_specs=[tok_spec, st_spec],
        out_shape=[jax.ShapeDtypeStruct((n_b, RWKV_W, t), F32),
                   jax.ShapeDtypeStruct((vr, RWKV_N, LANES), F32)],
        scratch_shapes=[pltpu.VMEM((vr, RWKV_N, LANES), F32)] + [key_chain] * 5 + [val_chain] * 2,
        compiler_params=_params("arbitrary"),
    )(r, k, w, a, v, k_k, k_a, r_k, lnx_w, lnx_b, s0)


def _merge_body(*refs, part_counts):
    n = len(part_counts)
    x_refs, oret_refs, g_refs = refs[:n], refs[n:2 * n], refs[2 * n:3 * n]
    yt_ref, ys_ref, wt_ref, wb_ref, o_ref = refs[3 * n:]
    x = _read_row_parts(x_refs, part_counts)
    y = jnp.where(pl.program_id(0) >= part_counts[0], ys_ref[...], yt_ref[...].T)
    yb = (y * _read_row_parts(g_refs, part_counts)).astype(BF16)
    ob = _read_row_parts(oret_refs, part_counts).astype(BF16)
    n_chunk = 256
    for j in range(0, D_MODEL, n_chunk):
        acc = jnp.dot(ob, wt_ref[:, j:j + n_chunk], preferred_element_type=F32)
        acc = acc + jnp.dot(yb, wb_ref[:, j:j + n_chunk], preferred_element_type=F32)
        o_ref[:, j:j + n_chunk] = x[:, j:j + n_chunk] + acc


def _merge(x_parts, oret_parts, g_parts, y_first_t, y_second, w_ret, w_rwkv):
    m = sum(part.shape[0] for part in x_parts)
    in_specs, part_counts = [], None
    for parts in (x_parts, oret_parts, g_parts):
        specs, part_counts = _row_part_specs(parts, TM)
        in_specs += specs
    assert len(part_counts) == 2
    n_first = part_counts[0]
    tiles = y_first_t.shape[2] // TM
    assert y_first_t.shape[0] * tiles == n_first
    yt_spec = pl.BlockSpec((None, RWKV_W, TM),
                           lambda i: (jnp.minimum(i, n_first - 1) // tiles, 0, jnp.minimum(i, n_first - 1) % tiles))
    ys_spec = pl.BlockSpec((TM, RWKV_W), lambda i: (jnp.clip(i - n_first, 0, part_counts[1] - 1), 0))
    wspec = pl.BlockSpec((RET_W, D_MODEL), lambda i: (0, 0))
    return pl.pallas_call(
        functools.partial(_merge_body, part_counts=part_counts),
        grid=(m // TM,),
        in_specs=in_specs + [yt_spec, ys_spec, wspec, wspec],
        out_specs=pl.BlockSpec((TM, D_MODEL), lambda i: (i, 0)),
        out_shape=jax.ShapeDtypeStruct((m, D_MODEL), F32),
        compiler_params=_params("parallel"),
    )(*x_parts, *oret_parts, *g_parts, y_first_t, y_second, w_ret.astype(BF16), w_rwkv.astype(BF16))


def _attn_body(q_ref, k_hbm, v_hbm, o_ref, kbuf, vbuf, sem, *, n_seq, tq):
    b, h, qi = pl.program_id(0), pl.program_id(1), pl.program_id(2)

    def copies(g):
        seq = b * n_seq + g
        return (pltpu.make_async_copy(k_hbm.at[seq, :, h, :], kbuf.at[g], sem.at[0]),
                pltpu.make_async_copy(v_hbm.at[seq, :, h, :], vbuf.at[g], sem.at[1]))

    @pl.when(qi == 0)
    def _():
        for g in range(n_seq):
            for queue, cp in enumerate(copies(g)):
                cp.start(priority=queue)
        for g in range(n_seq):
            for cp in copies(g):
                cp.wait()

    nt = (((1,), (1,)), ((), ()))
    for g in range(n_seq):
        rows = slice(g * tq, (g + 1) * tq)
        q = q_ref[rows, :].astype(BF16)
        s = lax.dot_general(q, kbuf[g].astype(BF16), nt, preferred_element_type=F32) * (MEM_DH ** -0.5)
        p = jnp.exp(s - jnp.max(s, axis=-1, keepdims=True))
        l = jnp.sum(p, axis=-1, keepdims=True)
        o = jnp.dot(p.astype(BF16), vbuf[g].astype(BF16), preferred_element_type=F32)
        o_ref[rows, :] = o / l


def _attention(q, mem_k, mem_v, *, row0, n_batch, t, n_seq, tq):
    q_tiles = t // tq
    rows = n_seq * tq
    assert t % tq == 0 and n_batch % n_seq == 0 and row0 % rows == 0 and (n_seq == 1 or q_tiles == 1)
    blk0 = row0 // rows
    kv_buf = pltpu.VMEM((n_seq, N_MEM, MEM_DH), F32)
    return pl.pallas_call(
        functools.partial(_attn_body, n_seq=n_seq, tq=tq),
        grid=(n_batch // n_seq, MEM_HEADS, q_tiles),
        in_specs=[pl.BlockSpec((rows, MEM_DH), lambda b, h, qi: (blk0 + b * q_tiles + qi, h)),
                  pl.BlockSpec(memory_space=pl.ANY), pl.BlockSpec(memory_space=pl.ANY)],
        out_specs=pl.BlockSpec((rows, MEM_DH), lambda b, h, qi: (b * q_tiles + qi, h)),
        out_shape=jax.ShapeDtypeStruct((n_batch * t, D_MODEL), F32),
        scratch_shapes=[kv_buf, kv_buf, pltpu.SemaphoreType.DMA((2,))],
        compiler_params=_params("arbitrary", "arbitrary", "arbitrary"),
    )(q, mem_k, mem_v)


def _router_body(h_ref, g_ref, w_ref, b_ref, hn_ref, ids_ref, comb_ref):
    x = h_ref[...]
    hn = x * lax.rsqrt(jnp.mean(x * x, axis=-1, keepdims=True) + EPS) * g_ref[...]
    hn_ref[...] = hn
    logits = jnp.dot(hn, w_ref[...], precision=lax.Precision.HIGHEST, preferred_element_type=F32) + b_ref[...]
    lane = lax.broadcasted_iota(jnp.int32, logits.shape, 1).astype(F32)
    neg = -jnp.inf

    def first_argmax(vals):
        m = jnp.max(vals, axis=-1, keepdims=True)
        return m, jnp.min(jnp.where(vals == m, lane, float(LANES)), axis=-1, keepdims=True)

    gl = jnp.where(lane < N_GROUPS, logits, neg)
    gmax, gsel = first_argmax(gl)
    pg_sel = 1.0 / jnp.sum(jnp.exp(gl - gmax), axis=-1, keepdims=True)
    e0 = N_GROUPS + gsel * EXP_PER_GROUP
    el = jnp.where((lane >= e0) & (lane < e0 + EXP_PER_GROUP), logits, neg)
    m1, i1 = first_argmax(el)
    m2, i2 = first_argmax(jnp.where(lane == i1, neg, el))
    e21 = jnp.exp(m2 - m1)
    c1 = pg_sel / (1.0 + e21)
    c2 = c1 * e21
    ids = jnp.where(lane == 0, i1 - N_GROUPS, jnp.where(lane == 1, i2 - N_GROUPS, 0.0))
    ids_ref[...] = ids.astype(jnp.int32)
    comb_ref[...] = jnp.where(lane == 0, c1, jnp.where(lane == 1, c2, 0.0))


def _router(h, g_ffn, w_gr, b_gr, w_er, b_er):
    m = h.shape[0]
    pad = LANES - N_GROUPS - N_EXPERTS
    w = jnp.concatenate([w_gr, w_er, jnp.zeros((D_MODEL, pad), F32)], axis=1)
    b = jnp.concatenate([b_gr, b_er, jnp.zeros((pad,), F32)]).reshape(1, LANES)
    row = lambda n: pl.BlockSpec((TM, n), lambda i: (i, 0))
    return pl.pallas_call(
        _router_body,
        grid=(m // TM,),
        in_specs=[row(D_MODEL), pl.BlockSpec((1, D_MODEL), lambda i: (0, 0)),
                  pl.BlockSpec((D_MODEL, LANES), lambda i: (0, 0)), pl.BlockSpec((1, LANES), lambda i: (0, 0))],
        out_specs=[row(D_MODEL), row(LANES), row(LANES)],
        out_shape=[jax.ShapeDtypeStruct((m, D_MODEL), F32), jax.ShapeDtypeStruct((m, LANES), jnp.int32),
                   jax.ShapeDtypeStruct((m, LANES), F32)],
        compiler_params=_params("parallel"),
    )(h, g_ffn.reshape(1, D_MODEL), w, b)


def _expert_body(blk_e_ref, idx_ref, idx_next_ref, x_hbm, wg_ref, wu_ref, wd_ref, out_hbm,
                 xbuf, ybuf, gsem, ssem):
    del blk_e_ref
    i = pl.program_id(0)
    n = pl.num_programs(0)
    slot = i % 2

    def start_gather(ref, sl):
        def body(r, c):
            pltpu.make_async_copy(x_hbm.at[pl.ds(ref[0, 0, r], 1), :], xbuf.at[sl, pl.ds(r, 1), :],
                                  gsem.at[sl]).start()
            return c
        lax.fori_loop(0, MOE_ROWS, body, 0, unroll=8)

    def wait_gather(sl):
        pltpu.make_async_copy(x_hbm.at[pl.ds(0, MOE_ROWS), :], xbuf.at[sl], gsem.at[sl]).wait()

    def wait_scatter(sl):
        pltpu.make_async_copy(ybuf.at[sl], out_hbm.at[pl.ds(0, MOE_ROWS), :], ssem.at[sl]).wait()

    @pl.when(i == 0)
    def _():
        start_gather(idx_ref, 0)

    @pl.when(i + 1 < n)
    def _():
        start_gather(idx_next_ref, 1 - slot)

    wait_gather(slot)

    @pl.when(i >= 2)
    def _():
        wait_scatter(slot)

    x = xbuf[slot].astype(BF16)
    hg = jnp.dot(x, wg_ref[0].astype(BF16), preferred_element_type=F32)
    hu = jnp.dot(x, wu_ref[0].astype(BF16), preferred_element_type=F32)
    act = (hg * jax.nn.sigmoid(hg) * hu).astype(BF16)
    ybuf[slot] = jnp.dot(act, wd_ref[0].astype(BF16), preferred_element_type=F32)

    def start_scatter(r, c):
        pltpu.make_async_copy(ybuf.at[slot, pl.ds(r, 1), :], out_hbm.at[pl.ds(idx_ref[0, 1, r], 1), :],
                              ssem.at[slot]).start()
        return c
    lax.fori_loop(0, MOE_ROWS, start_scatter, 0, unroll=8)

    @pl.when(i == n - 1)
    def _():
        wait_scatter(slot)

        @pl.when(n >= 2)
        def _():
            wait_scatter(1 - slot)


def _experts(hn, blk_e, idx, w_gate, w_up, w_down):
    n_blocks = idx.shape[0]
    p = n_blocks * MOE_ROWS
    idx_spec = lambda f: pl.BlockSpec((1, 2, MOE_ROWS), f, memory_space=pltpu.SMEM)
    grid_spec = pltpu.PrefetchScalarGridSpec(
        num_scalar_prefetch=1,
        grid=(n_blocks,),
        in_specs=[
            idx_spec(lambda i, be: (i, 0, 0)),
            idx_spec(lambda i, be: (jnp.minimum(i + 1, n_blocks - 1), 0, 0)),
            pl.BlockSpec(memory_space=pl.ANY),
            pl.BlockSpec((1, D_MODEL, D_EXPERT), lambda i, be: (be[i], 0, 0)),
            pl.BlockSpec((1, D_MODEL, D_EXPERT), lambda i, be: (be[i], 0, 0)),
            pl.BlockSpec((1, D_EXPERT, D_MODEL), lambda i, be: (be[i], 0, 0)),
        ],
        out_specs=pl.BlockSpec(memory_space=pl.ANY),
        scratch_shapes=[pltpu.VMEM((2, MOE_ROWS, D_MODEL), F32), pltpu.VMEM((2, MOE_ROWS, D_MODEL), F32),
                        pltpu.SemaphoreType.DMA((2,)), pltpu.SemaphoreType.DMA((2,))],
    )
    return pl.pallas_call(
        _expert_body,
        grid_spec=grid_spec,
        out_shape=jax.ShapeDtypeStruct((p, D_MODEL), F32),
        compiler_params=_params("arbitrary"),
    )(blk_e, idx, idx, hn, w_gate, w_up, w_down)


def _route_plan(ids):
    n_tok = ids.shape[0]
    n_pairs = ids.size
    n_blocks = -(-(n_pairs + N_EXPERTS * (MOE_ROWS - 1)) // MOE_ROWS)
    p = n_blocks * MOE_ROWS
    flat_e = ids.reshape(n_pairs)
    onehot = (flat_e[:, None] == jnp.arange(N_EXPERTS, dtype=jnp.int32)[None, :]).astype(jnp.int32)
    csum = jnp.cumsum(onehot, axis=0)
    rank = jnp.sum(onehot * csum, axis=1) - 1
    counts = csum[-1]
    pcounts = (counts + MOE_ROWS - 1) // MOE_ROWS * MOE_ROWS
    pends = jnp.cumsum(pcounts)
    pstarts = pends - pcounts
    dest = jnp.sum(onehot * pstarts[None, :], axis=1) + rank
    pair = jnp.arange(n_pairs, dtype=jnp.int32)
    row_pair = jnp.full((p,), -1, jnp.int32).at[dest].set(pair)
    is_pad = row_pair < 0
    pad_rank = jnp.cumsum(is_pad.astype(jnp.int32)) - 1
    row_tok = jnp.where(is_pad, 0, row_pair // TOP_K)
    row_dst = jnp.where(is_pad, n_pairs + pad_rank, (row_pair % TOP_K) * n_tok + row_tok)
    block_start = jnp.arange(n_blocks, dtype=jnp.int32) * MOE_ROWS
    blk_e = jnp.minimum(jnp.sum((block_start[:, None] >= pends[None, :]).astype(jnp.int32), axis=1),
                        N_EXPERTS - 1).astype(jnp.int32)
    idx = jnp.stack([row_tok.reshape(n_blocks, MOE_ROWS), row_dst.reshape(n_blocks, MOE_ROWS)], axis=1)
    return blk_e, idx.astype(jnp.int32)


def _final_body(h_ref, first_ref, second_ref, comb_ref, g_ref, o_ref):
    x = h_ref[...] + (first_ref[...] * comb_ref[:, 0:1] + second_ref[...] * comb_ref[:, 1:2])
    o_ref[...] = x * lax.rsqrt(jnp.mean(x * x, axis=-1, keepdims=True) + EPS) * g_ref[...]


def _final(h, pair_out, comb, g_final, *, row0, n_rows):
    n_tok = h.shape[0]
    assert row0 % TM == 0 and n_rows % TM == 0 and n_tok % TM == 0
    blk0, plane = row0 // TM, n_tok // TM
    row = lambda n, off: pl.BlockSpec((TM, n), lambda i: (off + i, 0))
    return pl.pallas_call(
        _final_body,
        grid=(n_rows // TM,),
        in_specs=[row(D_MODEL, blk0), row(D_MODEL, blk0), row(D_MODEL, plane + blk0), row(LANES, blk0),
                  pl.BlockSpec((1, D_MODEL), lambda i: (0, 0))],
        out_specs=pl.BlockSpec((TM, D_MODEL), lambda i: (i, 0)),
        out_shape=jax.ShapeDtypeStruct((n_rows, D_MODEL), F32),
        compiler_params=_params("parallel"),
    )(h, pair_out, pair_out, comb, g_final.reshape(1, D_MODEL))


def _reorder_last(x, shape, order):
    lead = x.shape[:-1]
    n = len(lead)
    y = x.reshape(lead + shape).transpose(tuple(range(n)) + tuple(n + o for o in order))
    return y.reshape(lead + (x.shape[-1],))


HALF_N = RWKV_N // 2


def _key_major(x):
    return _reorder_last(x, (RWKV_HEADS, RWKV_N), (1, 0))


def _key_major_inv(x):
    return _reorder_last(x, (RWKV_N, RWKV_HEADS), (1, 0))


def _value_major(x):
    return _reorder_last(x, (RWKV_HEADS, 2, HALF_N), (2, 1, 0))


def _value_major_inv(x):
    return _reorder_last(x, (HALF_N, 2, RWKV_HEADS), (2, 1, 0))


def _rwkv_cols(x, key_fn, value_fn):
    return jnp.concatenate([key_fn(x[..., :RWKV_W]), key_fn(x[..., RWKV_W:2 * RWKV_W]),
                            value_fn(x[..., 2 * RWKV_W:3 * RWKV_W]), x[..., 3 * RWKV_W:]], axis=-1)


def _sample_key_layout(x, b, t):
    return x.reshape(b, t, RWKV_N, RWKV_HEADS).transpose(3, 1, 2, 0)


def _sample_value_layout(x, b, t):
    return x.reshape(b, t, HALF_N, 2, RWKV_HEADS).transpose(4, 1, 3, 2, 0).reshape(RWKV_HEADS, t, RWKV_N, b)


def _sample_value_unlayout(y, b, t):
    return y.reshape(RWKV_HEADS, t, 2, HALF_N, b).transpose(4, 1, 3, 2, 0).reshape(b * t, RWKV_W)


def kernel(x_prompt, x_sample, mem_prompt, state_ret, state_rwkv, state_shift, cache_mem_k, cache_mem_v,
           g_mix, w_in, ret_gn, rwkv_mu, rwkv_w0, rwkv_w2, rwkv_a0, rwkv_a2, rwkv_g2, rwkv_k_k, rwkv_k_a,
           rwkv_r_k, rwkv_lnx_w, rwkv_lnx_b, w_out, g_mem_q, g_mem_kv, w_mq, w_mk, w_mv, w_mo, g_ffn,
           w_group_router, b_group_router, w_expert_router, b_expert_router, w_e_gate, w_e_up, w_e_down,
           g_final):
    assert w_in.shape[0] == 1, "single-layer decoder"
    bp, tp, d = x_prompt.shape
    bs, ts, _ = x_sample.shape
    np_tok, ns_tok = bp * tp, bs * ts
    assert d == D_MODEL and bp * RWKV_HEADS * 2 == LANES and bs == LANES
    l = 0
    x_parts = [x_prompt.reshape(np_tok, d), x_sample.reshape(ns_tok, d)]

    w_in_l = jnp.concatenate([w_in[l][:, :N_RET_COLS], _rwkv_cols(w_in[l][:, N_RET_COLS:], _key_major, _value_major)],
                             axis=1)
    (proj,) = _matmul(x_parts, [w_in_l.astype(BF16)], gain=g_mix[l])

    pos_p = np.arange(tp)
    pos_s = PAST_LEN + np.arange(ts)
    zero_ret = jnp.zeros((bp, RET_HEADS, RET_DK, RET_DV), F32)
    oret_p, sret_p = _retention(proj, zero_ret, ret_gn[l], pos_p, row0=0, n_batch=bp, t=tp, n_seq=1)
    oret_s, sret_s = _retention(proj, state_ret[l], ret_gn[l], pos_s, row0=np_tok, n_batch=bs, t=ts, n_seq=16)

    pre_w = (_rwkv_cols(rwkv_mu[l], _key_major, _value_major), _key_major(rwkv_w0[l]), _key_major(rwkv_w2[l]),
             _key_major(rwkv_a0[l]), _key_major(rwkv_a2[l]), _value_major(rwkv_g2[l]))
    zero_shift = jnp.zeros((bp, N_RWKV_COLS), F32)
    shift_in = _rwkv_cols(state_shift[l], _key_major, _value_major)
    r_p, k_p, v_p, w_p, a_p, gate_p = _rwkv_pre(proj, zero_shift, *pre_w, row0=0, n_batch=bp, t=tp,
                                                 n_seq=1, c=256, transposed=True)
    r_s, k_s, v_s, w_s, a_s, gate_s = _rwkv_pre(proj, shift_in, *pre_w, row0=np_tok, n_batch=bs, t=ts,
                                                 n_seq=16, c=ts, transposed=False)

    kvec = lambda v: v.reshape(RWKV_HEADS, RWKV_N)
    key_par = lambda v: jnp.broadcast_to(kvec(v).T[:, None, None, :], (RWKV_N, 2, bp, RWKV_HEADS)).reshape(
        RWKV_N, LANES)

    val_par = lambda v: jnp.broadcast_to(
        v.reshape(RWKV_HEADS, 2, HALF_N).transpose(2, 1, 0)[:, :, None, :],
        (HALF_N, 2, bp, RWKV_HEADS)).reshape(HALF_N, LANES)
    y_p, srw_p = _rwkv_scan_prompt(
        r_p, k_p, w_p, a_p, v_p, key_par(rwkv_k_k[l]), key_par(rwkv_k_a[l]), key_par(rwkv_r_k[l]),
        val_par(rwkv_lnx_w[l]), val_par(rwkv_lnx_b[l]), jnp.zeros((HALF_N, RWKV_N, LANES), F32))
    srw_p = srw_p.reshape(HALF_N, RWKV_N, 2, bp, RWKV_HEADS).transpose(3, 4, 2, 0, 1).reshape(
        bp, RWKV_HEADS, RWKV_N, RWKV_N)
    head_par = lambda v: jnp.broadcast_to(kvec(v)[:, :, None], (RWKV_HEADS, RWKV_N, LANES))
    y_s, srw_s = _rwkv_scan(
        _sample_key_layout(r_s, bs, ts), _sample_key_layout(k_s, bs, ts), _sample_key_layout(w_s, bs, ts),
        _sample_key_layout(a_s, bs, ts), _sample_value_layout(v_s, bs, ts),
        head_par(rwkv_k_k[l]), head_par(rwkv_k_a[l]), head_par(rwkv_r_k[l]),
        head_par(rwkv_lnx_w[l]), head_par(rwkv_lnx_b[l]),
        state_rwkv[l].astype(F32).transpose(1, 2, 3, 0), tc=ts, halves=1)
    y_s = _sample_value_unlayout(y_s, bs, ts)
    srw_s = srw_s.transpose(3, 0, 1, 2)

    w_rwkv_out = _value_major(w_out[l][RET_W:].T).T
    h = _merge(x_parts, [oret_p, oret_s], [gate_p, gate_s], y_p, y_s, w_out[l][:RET_W], w_rwkv_out)

    mk, mv = _matmul([mem_prompt.reshape(bp * N_MEM, d)], [w_mk[l].astype(BF16), w_mv[l].astype(BF16)],
                     gain=g_mem_kv[l], head_dim=MEM_DH)
    mem_shape = (N_MEM, MEM_HEADS, MEM_DH)
    (q,) = _matmul([h], [w_mq[l].astype(BF16)], gain=g_mem_q[l])
    att_p = _attention(q, mk.reshape(bp, *mem_shape), mv.reshape(bp, *mem_shape),
                       row0=0, n_batch=bp, t=tp, n_seq=1, tq=TM)
    att_s = _attention(q, cache_mem_k.reshape(bs, *mem_shape), cache_mem_v.reshape(bs, *mem_shape),
                       row0=np_tok, n_batch=bs, t=ts, n_seq=16, tq=ts)
    (h,) = _matmul([att_p, att_s], [w_mo[l].astype(BF16)], residual=h)

    hn, ids, comb = _router(h, g_ffn[l], w_group_router[l], b_group_router[l], w_expert_router[l],
                            b_expert_router[l])
    blk_e, idx = _route_plan(ids[:, :TOP_K])
    pair_out = _experts(hn, blk_e, idx, w_e_gate[l], w_e_up[l], w_e_down[l])
    y_prompt = _final(h, pair_out, comb, g_final, row0=0, n_rows=np_tok).reshape(bp, tp, d)
    y_sample = _final(h, pair_out, comb, g_final, row0=np_tok, n_rows=ns_tok).reshape(bs, ts, d)

    shift_p = lax.slice(proj, (tp - 1, N_RET_COLS), (np_tok, N_IN_COLS), (tp, 1))
    shift_s = lax.slice(proj, (np_tok + ts - 1, N_RET_COLS), (np_tok + ns_tok, N_IN_COLS), (ts, 1))
    shift_p = _rwkv_cols(shift_p, _key_major_inv, _value_major_inv)
    shift_s = _rwkv_cols(shift_s, _key_major_inv, _value_major_inv)
    return (y_prompt, y_sample, sret_p[None], srw_p[None], shift_p[None],
            mk.reshape(1, bp, *mem_shape), mv.reshape(1, bp, *mem_shape),
            sret_s[None], srw_s[None], shift_s[None])
```

```python
import functools

import numpy as np
import jax
import jax.numpy as jnp
from jax import lax
from jax.experimental import pallas as pl
from jax.experimental.pallas import tpu as pltpu

F32 = jnp.float32
BF16 = jnp.bfloat16

D_MODEL = 1024
PAST_LEN = 16384
N_MEM = 256
MEM_HEADS = 4
MEM_DH = D_MODEL // MEM_HEADS
RET_HEADS = 4
RET_W = D_MODEL // 2
RET_DV = RET_W // RET_HEADS
RET_DK = RET_DV // 2
RET_QK = RET_HEADS * RET_DK
RET_CHUNK = 128
ROPE_BASE = 10000.0
RWKV_N = 64
RWKV_W = D_MODEL - RET_W
RWKV_HEADS = RWKV_W // RWKV_N
LORA_W = 64
LORA_A = 64
LORA_G = 128
LNX_EPS = 64e-5
N_RET_COLS = 2 * RET_QK + 2 * RET_W
N_RWKV_COLS = 3 * RWKV_W + LORA_W + LORA_A + LORA_G
N_IN_COLS = N_RET_COLS + N_RWKV_COLS
N_GROUPS = 4
EXP_PER_GROUP = 8
N_EXPERTS = N_GROUPS * EXP_PER_GROUP
TOP_K = 2
D_EXPERT = D_MODEL // 2
EPS = 1e-6

LANES = 128
MOE_ROWS = 128
TM = 512


def _params(*sem):
    return pltpu.CompilerParams(dimension_semantics=sem)


def _row_part_specs(parts, tm):
    specs, counts, start = [], [], 0
    for part in parts:
        nb = part.shape[0] // tm
        assert nb * tm == part.shape[0]
        specs.append(pl.BlockSpec((tm, part.shape[1]), lambda i, s=start, n=nb: (jnp.clip(i - s, 0, n - 1), 0)))
        counts.append(nb)
        start += nb
    return specs, counts


def _read_row_parts(refs, counts):
    i = pl.program_id(0)
    x = refs[0][...]
    start = counts[0]
    for ref, nb in zip(refs[1:], counts[1:]):
        x = jnp.where(i >= start, ref[...], x)
        start += nb
    return x


def _mm_body(*refs, part_counts, n_w, has_gain, has_res, n_chunk):
    it = iter(refs)
    x_refs = [next(it) for _ in part_counts]
    g_ref = next(it) if has_gain else None
    w_refs = [next(it) for _ in range(n_w)]
    r_ref = next(it) if has_res else None
    o_refs = [next(it) for _ in range(n_w)]
    x = _read_row_parts(x_refs, part_counts).astype(F32)
    if has_gain:
        x = x * lax.rsqrt(jnp.mean(x * x, axis=-1, keepdims=True) + EPS) * g_ref[...]
    xb = x.astype(BF16)
    for w_ref, o_ref in zip(w_refs, o_refs):
        for j in range(0, w_ref.shape[1], n_chunk):
            acc = jnp.dot(xb, w_ref[:, j:j + n_chunk], preferred_element_type=F32)
            if has_res:
                acc = acc + r_ref[:, j:j + n_chunk]
            o_ref[:, j:j + n_chunk] = acc


def _matmul(x_parts, ws, *, gain=None, residual=None, tm=TM):
    k = x_parts[0].shape[1]
    m = sum(part.shape[0] for part in x_parts)
    n_chunk = 256
    assert all(w.shape[1] % n_chunk == 0 for w in ws)
    assert residual is None or len(ws) == 1
    in_specs, part_counts = _row_part_specs(x_parts, tm)
    args = list(x_parts)
    if gain is not None:
        in_specs.append(pl.BlockSpec((1, k), lambda i: (0, 0)))
        args.append(gain.reshape(1, k).astype(F32))
    for w in ws:
        in_specs.append(pl.BlockSpec(w.shape, lambda i: (0, 0)))
        args.append(w)
    if residual is not None:
        in_specs.append(pl.BlockSpec((tm, ws[0].shape[1]), lambda i: (i, 0)))
        args.append(residual)
    return pl.pallas_call(
        functools.partial(_mm_body, part_counts=part_counts, n_w=len(ws), has_gain=gain is not None,
                          has_res=residual is not None, n_chunk=n_chunk),
        grid=(m // tm,),
        in_specs=in_specs,
        out_specs=[pl.BlockSpec((tm, w.shape[1]), lambda i: (i, 0)) for w in ws],
        out_shape=[jax.ShapeDtypeStruct((m, w.shape[1]), F32) for w in ws],
        compiler_params=_params("parallel"),
    )(*args)


def _rot_tables(pos):
    half = RET_DK // 2
    inv_freq = ROPE_BASE ** (-(np.arange(half, dtype=np.float64) / half))
    ang = pos.astype(np.float64)[:, None] * inv_freq[None, :]
    cos, sin = np.cos(ang), np.sin(ang)
    zero = np.zeros_like(sin)
    c = np.tile(np.concatenate([cos, cos], axis=1), (1, RET_HEADS))
    s_lo = np.tile(np.concatenate([-sin, zero], axis=1), (1, RET_HEADS))
    s_hi = np.tile(np.concatenate([zero, sin], axis=1), (1, RET_HEADS))
    return [jnp.asarray(t, F32) for t in (c, s_lo, s_hi)]


def _ret_decay_tables(c):
    lg = np.log1p(-np.exp2(-5.0 - np.arange(RET_HEADS, dtype=np.float64)))
    idx = np.arange(c, dtype=np.float64)
    diff = idx[:, None] - idx[None, :]
    mask = np.where(diff[None] >= 0, np.exp(np.maximum(diff, 0.0)[None] * lg[:, None, None]), 0.0)
    q_dec = np.repeat(np.exp((idx[:, None] + 1.0) * lg[None, :]), RET_DV, axis=1)
    k_dec = np.repeat(np.exp((c - 1.0 - idx)[:, None] * lg[None, :]), RET_DK, axis=1)
    c_dec = [float(v) for v in np.exp(c * lg)]
    return jnp.asarray(mask, F32), jnp.asarray(q_dec, F32), jnp.asarray(k_dec, F32), c_dec


def _ret_body(q_ref, k_ref, v_ref, gate_ref, c_ref, slo_ref, shi_ref, mask_ref, qdec_ref, kdec_ref,
              gn_ref, s0_ref, o_ref, sout_ref, s_scr, *, n_seq, c, c_dec):
    ci = pl.program_id(1)

    @pl.when(ci == 0)
    def _():
        s_scr[...] = s0_ref[...].astype(F32)

    cos, s_lo, s_hi = c_ref[...], slo_ref[...], shi_ref[...]
    half = RET_DK // 2

    def rope(x):
        return x * cos + pltpu.roll(x, RET_QK - half, 1) * s_lo + pltpu.roll(x, half, 1) * s_hi

    nt = (((1,), (1,)), ((), ()))
    tn = (((0,), (0,)), ((), ()))
    for g in range(n_seq):
        rows = slice(g * c, (g + 1) * c)
        q = rope(q_ref[rows, :].astype(F32))
        k = rope(k_ref[rows, :].astype(F32)) * (RET_DK ** -0.5)
        k_st = k * kdec_ref[...]
        for h in range(RET_HEADS):
            kc = slice(h * RET_DK, (h + 1) * RET_DK)
            vc = slice(h * RET_DV, (h + 1) * RET_DV)
            qh = q[:, kc].astype(BF16)
            vh = v_ref[rows, vc].astype(BF16)
            s_h = s_scr[g, h]
            att = lax.dot_general(qh, k[:, kc].astype(BF16), nt, preferred_element_type=F32) * mask_ref[h]
            o = jnp.dot(att.astype(BF16), vh, preferred_element_type=F32)
            o = o + jnp.dot(qh, s_h.astype(BF16), preferred_element_type=F32) * qdec_ref[:, vc]
            s_scr[g, h] = s_h * c_dec[h] + lax.dot_general(
                k_st[:, kc].astype(BF16), vh, tn, preferred_element_type=F32)
            o = o * lax.rsqrt(jnp.mean(o * o, axis=-1, keepdims=True) + EPS)
            gate = gate_ref[rows, vc].astype(F32)
            o_ref[rows, vc] = o * gn_ref[:, vc] * (gate * jax.nn.sigmoid(gate))

    @pl.when(ci == pl.num_programs(1) - 1)
    def _():
        sout_ref[...] = s_scr[...]


def _retention(proj, s0, ret_gn, pos, *, row0, n_batch, t, n_seq):
    c = RET_CHUNK if t % RET_CHUNK == 0 else t
    n_chunks = t // c
    rows = n_seq * c
    assert n_batch % n_seq == 0 and row0 % rows == 0 and (n_seq == 1 or n_chunks == 1)
    blk0 = row0 // rows
    mask, q_dec, k_dec, c_dec = _ret_decay_tables(c)
    cos, s_lo, s_hi = _rot_tables(pos)

    def row_map(col):
        return lambda b, ci: (blk0 + b * n_chunks + ci, col)

    def const2(b, ci):
        return (0, 0)

    state_spec = pl.BlockSpec((n_seq, RET_HEADS, RET_DK, RET_DV), lambda b, ci: (b, 0, 0, 0))
    in_specs = [
        pl.BlockSpec((rows, RET_QK), row_map(0)),
        pl.BlockSpec((rows, RET_QK), row_map(1)),
        pl.BlockSpec((rows, RET_W), row_map(1)),
        pl.BlockSpec((rows, RET_W), row_map(2)),
        pl.BlockSpec((c, RET_QK), lambda b, ci: (ci, 0)),
        pl.BlockSpec((c, RET_QK), lambda b, ci: (ci, 0)),
        pl.BlockSpec((c, RET_QK), lambda b, ci: (ci, 0)),
        pl.BlockSpec((RET_HEADS, c, c), lambda b, ci: (0, 0, 0)),
        pl.BlockSpec((c, RET_W), const2),
        pl.BlockSpec((c, RET_QK), const2),
        pl.BlockSpec((1, RET_W), const2),
        state_spec,
    ]
    return pl.pallas_call(
        functools.partial(_ret_body, n_seq=n_seq, c=c, c_dec=c_dec),
        grid=(n_batch // n_seq, n_chunks),
        in_specs=in_specs,
        out_specs=[pl.BlockSpec((rows, RET_W), lambda b, ci: (b * n_chunks + ci, 0)), state_spec],
        out_shape=[jax.ShapeDtypeStruct((n_batch * t, RET_W), F32),
                   jax.ShapeDtypeStruct((n_batch, RET_HEADS, RET_DK, RET_DV), F32)],
        scratch_shapes=[pltpu.VMEM((n_seq, RET_HEADS, RET_DK, RET_DV), F32)],
        compiler_params=_params("parallel", "arbitrary"),
    )(proj, proj, proj, proj, cos, s_lo, s_hi, mask, q_dec, k_dec, ret_gn.reshape(1, RET_W).astype(F32), s0)


def _rwkv_pre_body(r_ref, k_ref, v_ref, lo_ref, shift_ref, mu_ref, w0_ref, w2_ref, a0_ref, a2_ref, g2_ref,
                   ro_ref, ko_ref, vo_ref, wo_ref, ao_ref, go_ref, prev_scr, *, n_seq, c, transposed):
    ci = pl.program_id(1)

    @pl.when(ci == 0)
    def _():
        for g in range(n_seq):
            prev_scr[g] = shift_ref[g].astype(F32)

    first_row = lax.broadcasted_iota(jnp.int32, (c, 1), 0) == 0

    def shifted(x_ref, g, col0):
        w = x_ref.shape[1]
        x = x_ref[g * c:(g + 1) * c, :].astype(F32)
        prev_row = prev_scr[g, :, col0:col0 + w]
        prev = jnp.where(first_row, prev_row, pltpu.roll(x, 1, 0))
        prev_scr[g, :, col0:col0 + w] = x[c - 1:c, :]
        return x + (prev - x) * mu_ref[:, col0:col0 + w]

    def put(o_ref, g, val):
        if transposed:
            o_ref[...] = val.T
        else:
            o_ref[g * c:(g + 1) * c, :] = val

    for g in range(n_seq):
        put(ro_ref, g, shifted(r_ref, g, 0))
        put(ko_ref, g, shifted(k_ref, g, RWKV_W))
        put(vo_ref, g, shifted(v_ref, g, 2 * RWKV_W))
        lo = shifted(lo_ref, g, 3 * RWKV_W)
        hw = lo[:, :LORA_W]
        ha = lo[:, LORA_W:LORA_W + LORA_A]
        hg = lo[:, LORA_W + LORA_A:]
        u = w0_ref[...] + jnp.dot(jnp.tanh(hw).astype(BF16), w2_ref[...], preferred_element_type=F32)
        put(wo_ref, g, jnp.exp(-float(np.exp(-0.5)) * jax.nn.sigmoid(u)))
        put(ao_ref, g, jax.nn.sigmoid(
            a0_ref[...] + jnp.dot(ha.astype(BF16), a2_ref[...], preferred_element_type=F32)))
        go_ref[g * c:(g + 1) * c, :] = jnp.dot(jax.nn.sigmoid(hg).astype(BF16), g2_ref[...],
                                                preferred_element_type=F32)


def _rwkv_pre(proj, s_shift, mu, w0, w2, a0, a2, g2, *, row0, n_batch, t, n_seq, c, transposed):
    n_chunks = t // c
    rows = n_seq * c
    assert t % c == 0 and n_batch % n_seq == 0 and row0 % rows == 0 and (n_seq == 1 or n_chunks == 1)
    assert not transposed or n_seq == 1
    blk0 = row0 // rows
    col_r = N_RET_COLS // RWKV_W
    lo_w = LORA_W + LORA_A + LORA_G
    col_lo = (N_RET_COLS + 3 * RWKV_W) // lo_w
    assert col_r * RWKV_W == N_RET_COLS and col_lo * lo_w == N_RET_COLS + 3 * RWKV_W

    def row_map(col):
        return lambda b, ci: (blk0 + b * n_chunks + ci, col)

    def const2(b, ci):
        return (0, 0)

    in_specs = [
        pl.BlockSpec((rows, RWKV_W), row_map(col_r)),
        pl.BlockSpec((rows, RWKV_W), row_map(col_r + 1)),
        pl.BlockSpec((rows, RWKV_W), row_map(col_r + 2)),
        pl.BlockSpec((rows, lo_w), row_map(col_lo)),
        pl.BlockSpec((n_seq, 1, N_RWKV_COLS), lambda b, ci: (b, 0, 0)),
        pl.BlockSpec((1, N_RWKV_COLS), const2),
        pl.BlockSpec((1, RWKV_W), const2),
        pl.BlockSpec((LORA_W, RWKV_W), const2),
        pl.BlockSpec((1, RWKV_W), const2),
        pl.BlockSpec((LORA_A, RWKV_W), const2),
        pl.BlockSpec((LORA_G, RWKV_W), const2),
    ]
    nat_spec = pl.BlockSpec((rows, RWKV_W), lambda b, ci: (b * n_chunks + ci, 0))
    nat_shape = jax.ShapeDtypeStruct((n_batch * t, RWKV_W), F32)
    if transposed:
        vec_spec = pl.BlockSpec((None, RWKV_W, c), lambda b, ci: (b, 0, ci))
        vec_shape = jax.ShapeDtypeStruct((n_batch, RWKV_W, t), F32)
    else:
        vec_spec, vec_shape = nat_spec, nat_shape
    return pl.pallas_call(
        functools.partial(_rwkv_pre_body, n_seq=n_seq, c=c, transposed=transposed),
        grid=(n_batch // n_seq, n_chunks),
        in_specs=in_specs,
        out_specs=[vec_spec] * 5 + [nat_spec],
        out_shape=[vec_shape] * 5 + [nat_shape],
        scratch_shapes=[pltpu.VMEM((n_seq, 1, N_RWKV_COLS), F32)],
        compiler_params=_params("parallel", "arbitrary"),
    )(proj, proj, proj, proj, s_shift.reshape(n_batch, 1, N_RWKV_COLS),
      mu.reshape(1, -1), w0.reshape(1, -1), w2.astype(BF16), a0.reshape(1, -1),
      a2.astype(BF16), g2.astype(BF16))


def _scan_body(r_ref, k_ref, w_ref, a_ref, v_ref, kk_ref, ka_ref, rk_ref, lw_ref, lb_ref, s0_ref,
               y_ref, sout_ref, s_scr, a_scr, b_scr, km_scr, *, tc, vr, halves):
    ci = pl.program_id(1)

    @pl.when(ci == 0)
    def _():
        s_scr[...] = s0_ref[...].astype(F32)

    def ksum(x):
        return jnp.sum(x, axis=-2, keepdims=True)

    def vsum(x):
        if halves == 2:
            x2 = x.reshape(tc * vr, LANES)
            x = (x2 + pltpu.roll(x2, LANES // 2, 1)).reshape(tc, vr, LANES)
        return jnp.sum(x, axis=1, keepdims=True)

    kr = k_ref[...]
    a = a_ref[...]
    kk = kr * kk_ref[...]
    kk = kk / jnp.maximum(jnp.sqrt(ksum(kk * kk)), 1e-12)
    a_scr[...] = -kk
    b_scr[...] = kk * a
    km_scr[...] = kr * (1.0 + (a - 1.0) * ka_ref[...])

    def token(t, carry):
        r, w, avec, bvec, kmod = r_ref[t], w_ref[t], a_scr[t], b_scr[t], km_scr[t]

        def value_row(i, c2):
            s = s_scr[i]
            sa = ksum(s * avec)
            s = s * w + sa * bvec + v_ref[t, pl.ds(i, 1), :] * kmod
            s_scr[i] = s
            y_ref[t, pl.ds(i, 1), :] = ksum(s * r)
            return c2

        lax.fori_loop(0, vr, value_row, 0, unroll=16)
        return carry

    lax.fori_loop(0, tc, token, 0)

    y = y_ref[...]
    d = y - vsum(y) * (1.0 / RWKV_N)
    var = vsum(d * d) * (1.0 / RWKV_N)
    bonus = ksum(r_ref[...] * km_scr[...] * rk_ref[...])
    y_ref[...] = d * lax.rsqrt(var + LNX_EPS) * lw_ref[...] + lb_ref[...] + bonus * v_ref[...]

    @pl.when(ci == pl.num_programs(1) - 1)
    def _():
        sout_ref[...] = s_scr[...]


def _rwkv_scan(r, k, w, a, v, k_k, k_a, r_k, lnx_w, lnx_b, s0, *, tc, halves):
    n_grp, t, _, lanes = r.shape
    vr = v.shape[2]
    assert lanes == LANES and t % tc == 0 and vr * halves == RWKV_N

    def tok_spec(rows):
        return pl.BlockSpec((None, tc, rows, LANES), lambda g, ci: (g, ci, 0, 0))

    def par_spec(rows):
        return pl.BlockSpec((None, rows, LANES), lambda g, ci: (g, 0, 0))

    st_spec = pl.BlockSpec((None, vr, RWKV_N, LANES), lambda g, ci: (g, 0, 0, 0))
    key_scratch = pltpu.VMEM((tc, RWKV_N, LANES), F32)
    return pl.pallas_call(
        functools.partial(_scan_body, tc=tc, vr=vr, halves=halves),
        grid=(n_grp, t // tc),
        in_specs=[tok_spec(RWKV_N)] * 4 + [tok_spec(vr)] + [par_spec(RWKV_N)] * 3 + [par_spec(vr)] * 2 + [st_spec],
        out_specs=[tok_spec(vr), st_spec],
        out_shape=[jax.ShapeDtypeStruct((n_grp, t, vr, LANES), F32),
                   jax.ShapeDtypeStruct((n_grp, vr, RWKV_N, LANES), F32)],
        scratch_shapes=[pltpu.VMEM((vr, RWKV_N, LANES), F32), key_scratch, key_scratch, key_scratch],
        compiler_params=_params("parallel", "arbitrary"),
    )(r, k, w, a, v, k_k, k_a, r_k, lnx_w, lnx_b, s0)


SCAN_TC = 128
SCAN_SUB = 64


def _scan_prompt_body(r_ref, k_ref, w_ref, a_ref, v_ref, kk_ref, ka_ref, rk_ref, lw_ref, lb_ref, s0_ref,
                      memk_hbm, memv_hbm, y_ref, sout_ref, memk_out, memv_out,
                      s_scr, r_c, w_c, a_c, b_c, km_c, v_c, y_c, mem_sem, *, n_b, mem_seqs):
    ci = pl.program_id(0)
    vr = RWKV_N // 2
    ts = SCAN_SUB
    tile = RWKV_HEADS
    half_lanes = LANES // 2

    @pl.when(ci == 0)
    def _():
        s_scr[...] = s0_ref[...].astype(F32)

    def mem_copies():
        out = []
        for j in range(mem_seqs):
            seq = ci * mem_seqs + j
            for h in range(MEM_HEADS):
                out.append(pltpu.make_async_copy(memk_hbm.at[seq, :, h, :], memk_out.at[seq, h], mem_sem.at[0]))
                out.append(pltpu.make_async_copy(memv_hbm.at[seq, :, h, :], memv_out.at[seq, h], mem_sem.at[1]))
        return out

    for cp in mem_copies():
        cp.start()

    low = lax.broadcasted_iota(jnp.int32, (ts, LANES), 1) < half_lanes

    def feature_pair_rows(x_ref, base):
        tiles = [x_ref[b, pl.ds(base + f * tile, tile), :] for f in range(2) for b in range(n_b)]
        return jnp.concatenate(tiles, axis=0).T

    def key_to_chain(x_ref, dst, t0):
        def group(g, c):
            rows = []
            for j in range(4):
                mt = feature_pair_rows(x_ref, pl.multiple_of((g * 4 + j) * 2 * tile, 2 * tile))[t0:t0 + ts]
                sw = pltpu.roll(mt, half_lanes, 1)
                rows += [jnp.where(low, mt, sw), jnp.where(low, sw, mt)]
            dst[:, pl.ds(pl.multiple_of(g * 8, 8), 8), :] = jnp.swapaxes(jnp.stack(rows, axis=0), 0, 1)
            return c
        lax.fori_loop(0, RWKV_N // 8, group, 0)

    def value_to_chain(g, c):
        rows = [feature_pair_rows(v_ref, pl.multiple_of((g * 8 + j) * 2 * tile, 2 * tile)) for j in range(8)]
        v_c[:, pl.ds(pl.multiple_of(g * 8, 8), 8), :] = jnp.swapaxes(jnp.stack(rows, axis=0), 0, 1)
        return c
    lax.fori_loop(0, vr // 8, value_to_chain, 0)

    def ksum(x):
        return jnp.sum(x, axis=-2, keepdims=True)

    for t0 in range(0, SCAN_TC, ts):
        key_to_chain(r_ref, r_c, t0)
        key_to_chain(w_ref, w_c, t0)
        key_to_chain(k_ref, km_c, t0)
        key_to_chain(a_ref, b_c, t0)

        def prep(g8, c):
            toks = pl.ds(pl.multiple_of(g8 * 8, 8), 8)
            kr = km_c[toks]
            a = b_c[toks]
            kk = kr * kk_ref[...]
            kk = kk / jnp.maximum(jnp.sqrt(ksum(kk * kk)), 1e-12)
            a_c[toks] = -kk
            b_c[toks] = kk * a
            km_c[toks] = kr * (1.0 + (a - 1.0) * ka_ref[...])
            return c
        lax.fori_loop(0, ts // 8, prep, 0)

        def token(t, carry):
            r, w, avec, bvec, kmod = r_c[t], w_c[t], a_c[t], b_c[t], km_c[t]

            def value_row(i, c2):
                s = s_scr[i]
                sa = ksum(s * avec)
                s = s * w + sa * bvec + v_c[t0 + t, pl.ds(i, 1), :] * kmod
                s_scr[i] = s
                y_c[t0 + t, pl.ds(i, 1), :] = ksum(s * r)
                return c2

            lax.fori_loop(0, vr, value_row, 0, unroll=True)
            return carry

        lax.fori_loop(0, ts, token, 0)

        def post(g8, c):
            ktoks = pl.ds(pl.multiple_of(g8 * 8, 8), 8)
            vtoks = pl.ds(pl.multiple_of(t0 + g8 * 8, 8), 8)

            def vsum(x):
                x2 = x.reshape(8 * vr, LANES)
                x2 = x2 + pltpu.roll(x2, half_lanes, 1)
                return jnp.sum(x2.reshape(8, vr, LANES), axis=1, keepdims=True)

            y = y_c[vtoks]
            d = y - vsum(y) * (1.0 / RWKV_N)
            var = vsum(d * d) * (1.0 / RWKV_N)
            bonus = ksum(r_c[ktoks] * km_c[ktoks] * rk_ref[...])
            y_c[vtoks] = d * lax.rsqrt(var + LNX_EPS) * lw_ref[...] + lb_ref[...] + bonus * v_c[vtoks]
            return c
        lax.fori_loop(0, ts // 8, post, 0)

    def value_from_chain(g, c):
        blk = jnp.swapaxes(y_c[:, pl.ds(pl.multiple_of(g * 8, 8), 8), :], 0, 1)
        for j in range(8):
            mt = blk[j].T
            base = pl.multiple_of((g * 8 + j) * 2 * tile, 2 * tile)
            for hf in range(2):
                for b in range(n_b):
                    row0 = (hf * n_b + b) * tile
                    y_ref[b, pl.ds(base + hf * tile, tile), :] = mt[row0:row0 + tile, :]
        return c
    lax.fori_loop(0, vr // 8, value_from_chain, 0)

    for cp in mem_copies():
        cp.wait()

    @pl.when(ci == pl.num_programs(0) - 1)
    def _():
        sout_ref[...] = s_scr[...]


def _rwkv_scan_prompt(r, k, w, a, v, k_k, k_a, r_k, lnx_w, lnx_b, s0, mem_k, mem_v):
    n_b, _, t = r.shape
    vr = RWKV_N // 2
    n_steps = t // SCAN_TC
    n_mem_seq = mem_k.shape[0]
    assert t % SCAN_TC == 0 and 2 * n_b * RWKV_HEADS == LANES and n_mem_seq % n_steps == 0
    any_spec = pl.BlockSpec(memory_space=pl.ANY)
    mem_shape = jax.ShapeDtypeStruct((n_mem_seq, MEM_HEADS, N_MEM, MEM_DH), F32)
    tok_spec = pl.BlockSpec((n_b, RWKV_W, SCAN_TC), lambda ci: (0, 0, ci))
    key_par = pl.BlockSpec((RWKV_N, LANES), lambda ci: (0, 0))
    val_par = pl.BlockSpec((vr, LANES), lambda ci: (0, 0))
    st_spec = pl.BlockSpec((vr, RWKV_N, LANES), lambda ci: (0, 0, 0))
    key_chain = pltpu.VMEM((SCAN_SUB, RWKV_N, LANES), F32)
    val_chain = pltpu.VMEM((SCAN_TC, vr, LANES), F32)
    return pl.pallas_call(
        functools.partial(_scan_prompt_body, n_b=n_b, mem_seqs=n_mem_seq // n_steps),
        grid=(n_steps,),
        in_specs=[tok_spec] * 5 + [key_par] * 3 + [val_par] * 2 + [st_spec, any_spec, any_spec],
        out_specs=[tok_spec, st_spec, any_spec, any_spec],
        out_shape=[jax.ShapeDtypeStruct((n_b, RWKV_W, t), F32),
                   jax.ShapeDtypeStruct((vr, RWKV_N, LANES), F32), mem_shape, mem_shape],
        scratch_shapes=([pltpu.VMEM((vr, RWKV_N, LANES), F32)] + [key_chain] * 5 + [val_chain] * 2
                        + [pltpu.SemaphoreType.DMA((2,))]),
        compiler_params=_params("arbitrary"),
    )(r, k, w, a, v, k_k, k_a, r_k, lnx_w, lnx_b, s0, mem_k, mem_v)


def _merge_body(*refs, part_counts):
    n = len(part_counts)
    x_refs, oret_refs, g_refs = refs[:n], refs[n:2 * n], refs[2 * n:3 * n]
    yt_ref, ys_ref, wt_ref, wb_ref, o_ref = refs[3 * n:]
    x = _read_row_parts(x_refs, part_counts)
    y = jnp.where(pl.program_id(0) >= part_counts[0], ys_ref[...], yt_ref[...].T)
    yb = (y * _read_row_parts(g_refs, part_counts)).astype(BF16)
    ob = _read_row_parts(oret_refs, part_counts).astype(BF16)
    n_chunk = 256
    for j in range(0, D_MODEL, n_chunk):
        acc = jnp.dot(ob, wt_ref[:, j:j + n_chunk], preferred_element_type=F32)
        acc = acc + jnp.dot(yb, wb_ref[:, j:j + n_chunk], preferred_element_type=F32)
        o_ref[:, j:j + n_chunk] = x[:, j:j + n_chunk] + acc


def _merge(x_parts, oret_parts, g_parts, y_first_t, y_second, w_ret, w_rwkv):
    m = sum(part.shape[0] for part in x_parts)
    in_specs, part_counts = [], None
    for parts in (x_parts, oret_parts, g_parts):
        specs, part_counts = _row_part_specs(parts, TM)
        in_specs += specs
    assert len(part_counts) == 2
    n_first = part_counts[0]
    tiles = y_first_t.shape[2] // TM
    assert y_first_t.shape[0] * tiles == n_first
    yt_spec = pl.BlockSpec((None, RWKV_W, TM),
                           lambda i: (jnp.minimum(i, n_first - 1) // tiles, 0, jnp.minimum(i, n_first - 1) % tiles))
    ys_spec = pl.BlockSpec((TM, RWKV_W), lambda i: (jnp.clip(i - n_first, 0, part_counts[1] - 1), 0))
    wspec = pl.BlockSpec((RET_W, D_MODEL), lambda i: (0, 0))
    return pl.pallas_call(
        functools.partial(_merge_body, part_counts=part_counts),
        grid=(m // TM,),
        in_specs=in_specs + [yt_spec, ys_spec, wspec, wspec],
        out_specs=pl.BlockSpec((TM, D_MODEL), lambda i: (i, 0)),
        out_shape=jax.ShapeDtypeStruct((m, D_MODEL), F32),
        compiler_params=_params("parallel"),
    )(*x_parts, *oret_parts, *g_parts, y_first_t, y_second, w_ret.astype(BF16), w_rwkv.astype(BF16))


def _mem_kv_body(x_ref, g_ref, wk_ref, wv_ref, k_ref, v_ref, kh_ref, vh_ref, *, n_seq):
    x = x_ref[...].astype(F32)
    xb = (x * lax.rsqrt(jnp.mean(x * x, axis=-1, keepdims=True) + EPS) * g_ref[...]).astype(BF16)
    for w_ref, o_ref, oh_ref in ((wk_ref, k_ref, kh_ref), (wv_ref, v_ref, vh_ref)):
        for h in range(MEM_HEADS):
            acc = jnp.dot(xb, w_ref[:, h * MEM_DH:(h + 1) * MEM_DH], preferred_element_type=F32)
            o_ref[:, h, :] = acc
            for s in range(n_seq):
                oh_ref[s, h] = acc[s * N_MEM:(s + 1) * N_MEM]


def _mem_kv(mem, gain, w_k, w_v):
    n_b = mem.shape[0]
    n_seq = TM // N_MEM
    assert n_seq * N_MEM == TM and n_b % n_seq == 0
    wspec = pl.BlockSpec((D_MODEL, D_MODEL), lambda i: (0, 0))
    tok_spec = pl.BlockSpec((TM, MEM_HEADS, MEM_DH), lambda i: (i, 0, 0))
    head_spec = pl.BlockSpec((n_seq, MEM_HEADS, N_MEM, MEM_DH), lambda i: (i, 0, 0, 0))
    tok_shape = jax.ShapeDtypeStruct((n_b * N_MEM, MEM_HEADS, MEM_DH), F32)
    head_shape = jax.ShapeDtypeStruct((n_b, MEM_HEADS, N_MEM, MEM_DH), F32)
    return pl.pallas_call(
        functools.partial(_mem_kv_body, n_seq=n_seq),
        grid=(n_b // n_seq,),
        in_specs=[pl.BlockSpec((TM, D_MODEL), lambda i: (i, 0)), pl.BlockSpec((1, D_MODEL), lambda i: (0, 0)),
                  wspec, wspec],
        out_specs=[tok_spec, tok_spec, head_spec, head_spec],
        out_shape=[tok_shape, tok_shape, head_shape, head_shape],
        compiler_params=_params("parallel"),
    )(mem.reshape(n_b * N_MEM, D_MODEL), gain.reshape(1, D_MODEL), w_k.astype(BF16), w_v.astype(BF16))


def _attn_body(q_ref, k_ref, v_ref, o_ref, *, n_seq, tq):
    nt = (((1,), (1,)), ((), ()))
    for g in range(n_seq):
        rows = slice(g * tq, (g + 1) * tq)
        q = q_ref[rows, :].astype(BF16)
        s = lax.dot_general(q, k_ref[g].astype(BF16), nt, preferred_element_type=F32) * (MEM_DH ** -0.5)
        p = jnp.exp(s - jnp.max(s, axis=-1, keepdims=True))
        l = jnp.sum(p, axis=-1, keepdims=True)
        o = jnp.dot(p.astype(BF16), v_ref[g].astype(BF16), preferred_element_type=F32)
        o_ref[rows, :] = o / l


def _attention(q, mem_k, mem_v, *, row0, n_batch, t, n_seq, tq):
    q_tiles = t // tq
    rows = n_seq * tq
    assert t % tq == 0 and n_batch % n_seq == 0 and row0 % rows == 0 and (n_seq == 1 or q_tiles == 1)
    blk0 = row0 // rows
    kv_spec = pl.BlockSpec((n_seq, None, N_MEM, MEM_DH), lambda b, h, qi: (b, h, 0, 0))
    return pl.pallas_call(
        functools.partial(_attn_body, n_seq=n_seq, tq=tq),
        grid=(n_batch // n_seq, MEM_HEADS, q_tiles),
        in_specs=[pl.BlockSpec((rows, MEM_DH), lambda b, h, qi: (blk0 + b * q_tiles + qi, h)), kv_spec, kv_spec],
        out_specs=pl.BlockSpec((rows, MEM_DH), lambda b, h, qi: (b * q_tiles + qi, h)),
        out_shape=jax.ShapeDtypeStruct((n_batch * t, D_MODEL), F32),
        compiler_params=_params("parallel", "parallel", "parallel"),
    )(q, mem_k, mem_v)


def _router_body(h_ref, g_ref, w_ref, b_ref, hn_ref, ids_ref, comb_ref):
    x = h_ref[...]
    hn = x * lax.rsqrt(jnp.mean(x * x, axis=-1, keepdims=True) + EPS) * g_ref[...]
    hn_ref[...] = hn
    logits = jnp.dot(hn, w_ref[...], precision=lax.Precision.HIGHEST, preferred_element_type=F32) + b_ref[...]
    lane = lax.broadcasted_iota(jnp.int32, logits.shape, 1).astype(F32)
    neg = -jnp.inf

    def first_argmax(vals):
        m = jnp.max(vals, axis=-1, keepdims=True)
        return m, jnp.min(jnp.where(vals == m, lane, float(LANES)), axis=-1, keepdims=True)

    gl = jnp.where(lane < N_GROUPS, logits, neg)
    gmax, gsel = first_argmax(gl)
    pg_sel = 1.0 / jnp.sum(jnp.exp(gl - gmax), axis=-1, keepdims=True)
    e0 = N_GROUPS + gsel * EXP_PER_GROUP
    el = jnp.where((lane >= e0) & (lane < e0 + EXP_PER_GROUP), logits, neg)
    m1, i1 = first_argmax(el)
    m2, i2 = first_argmax(jnp.where(lane == i1, neg, el))
    e21 = jnp.exp(m2 - m1)
    c1 = pg_sel / (1.0 + e21)
    c2 = c1 * e21
    ids = jnp.where(lane == 0, i1 - N_GROUPS, jnp.where(lane == 1, i2 - N_GROUPS, 0.0))
    ids_ref[...] = ids.astype(jnp.int32)
    comb_ref[...] = jnp.where(lane == 0, c1, jnp.where(lane == 1, c2, 0.0))


def _router(h, g_ffn, w_gr, b_gr, w_er, b_er):
    m = h.shape[0]
    pad = LANES - N_GROUPS - N_EXPERTS
    w = jnp.concatenate([w_gr, w_er, jnp.zeros((D_MODEL, pad), F32)], axis=1)
    b = jnp.concatenate([b_gr, b_er, jnp.zeros((pad,), F32)]).reshape(1, LANES)
    row = lambda n: pl.BlockSpec((TM, n), lambda i: (i, 0))
    return pl.pallas_call(
        _router_body,
        grid=(m // TM,),
        in_specs=[row(D_MODEL), pl.BlockSpec((1, D_MODEL), lambda i: (0, 0)),
                  pl.BlockSpec((D_MODEL, LANES), lambda i: (0, 0)), pl.BlockSpec((1, LANES), lambda i: (0, 0))],
        out_specs=[row(D_MODEL), row(LANES), row(LANES)],
        out_shape=[jax.ShapeDtypeStruct((m, D_MODEL), F32), jax.ShapeDtypeStruct((m, LANES), jnp.int32),
                   jax.ShapeDtypeStruct((m, LANES), F32)],
        compiler_params=_params("parallel"),
    )(h, g_ffn.reshape(1, D_MODEL), w, b)


def _expert_body(blk_e_ref, idx_ref, idx_next_ref, x_hbm, wg_ref, wu_ref, wd_ref, out_hbm,
                 xbuf, ybuf, gsem, ssem):
    del blk_e_ref
    i = pl.program_id(0)
    n = pl.num_programs(0)
    slot = i % 2

    def start_gather(ref, sl):
        def body(r, c):
            pltpu.make_async_copy(x_hbm.at[pl.ds(ref[0, 0, r], 1), :], xbuf.at[sl, pl.ds(r, 1), :],
                                  gsem.at[sl]).start()
            return c
        lax.fori_loop(0, MOE_ROWS, body, 0, unroll=8)

    def wait_gather(sl):
        pltpu.make_async_copy(x_hbm.at[pl.ds(0, MOE_ROWS), :], xbuf.at[sl], gsem.at[sl]).wait()

    def wait_scatter(sl):
        pltpu.make_async_copy(ybuf.at[sl], out_hbm.at[pl.ds(0, MOE_ROWS), :], ssem.at[sl]).wait()

    @pl.when(i == 0)
    def _():
        start_gather(idx_ref, 0)

    @pl.when(i + 1 < n)
    def _():
        start_gather(idx_next_ref, 1 - slot)

    wait_gather(slot)

    @pl.when(i >= 2)
    def _():
        wait_scatter(slot)

    x = xbuf[slot].astype(BF16)
    hg = jnp.dot(x, wg_ref[0].astype(BF16), preferred_element_type=F32)
    hu = jnp.dot(x, wu_ref[0].astype(BF16), preferred_element_type=F32)
    act = (hg * jax.nn.sigmoid(hg) * hu).astype(BF16)
    ybuf[slot] = jnp.dot(act, wd_ref[0].astype(BF16), preferred_element_type=F32)

    def start_scatter(r, c):
        pltpu.make_async_copy(ybuf.at[slot, pl.ds(r, 1), :], out_hbm.at[pl.ds(idx_ref[0, 1, r], 1), :],
                              ssem.at[slot]).start()
        return c
    lax.fori_loop(0, MOE_ROWS, start_scatter, 0, unroll=8)

    @pl.when(i == n - 1)
    def _():
        wait_scatter(slot)

        @pl.when(n >= 2)
        def _():
            wait_scatter(1 - slot)


def _experts(hn, blk_e, idx, w_gate, w_up, w_down):
    n_blocks = idx.shape[0]
    p = n_blocks * MOE_ROWS
    idx_spec = lambda f: pl.BlockSpec((1, 2, MOE_ROWS), f, memory_space=pltpu.SMEM)
    grid_spec = pltpu.PrefetchScalarGridSpec(
        num_scalar_prefetch=1,
        grid=(n_blocks,),
        in_specs=[
            idx_spec(lambda i, be: (i, 0, 0)),
            idx_spec(lambda i, be: (jnp.minimum(i + 1, n_blocks - 1), 0, 0)),
            pl.BlockSpec(memory_space=pl.ANY),
            pl.BlockSpec((1, D_MODEL, D_EXPERT), lambda i, be: (be[i], 0, 0)),
            pl.BlockSpec((1, D_MODEL, D_EXPERT), lambda i, be: (be[i], 0, 0)),
            pl.BlockSpec((1, D_EXPERT, D_MODEL), lambda i, be: (be[i], 0, 0)),
        ],
        out_specs=pl.BlockSpec(memory_space=pl.ANY),
        scratch_shapes=[pltpu.VMEM((2, MOE_ROWS, D_MODEL), F32), pltpu.VMEM((2, MOE_ROWS, D_MODEL), F32),
                        pltpu.SemaphoreType.DMA((2,)), pltpu.SemaphoreType.DMA((2,))],
    )
    return pl.pallas_call(
        _expert_body,
        grid_spec=grid_spec,
        out_shape=jax.ShapeDtypeStruct((p, D_MODEL), F32),
        compiler_params=_params("arbitrary"),
    )(blk_e, idx, idx, hn, w_gate, w_up, w_down)


def _route_plan(ids):
    n_tok = ids.shape[0]
    n_pairs = ids.size
    n_blocks = -(-(n_pairs + N_EXPERTS * (MOE_ROWS - 1)) // MOE_ROWS)
    p = n_blocks * MOE_ROWS
    flat_e = ids.reshape(n_pairs)
    onehot = (flat_e[:, None] == jnp.arange(N_EXPERTS, dtype=jnp.int32)[None, :]).astype(jnp.int32)
    csum = jnp.cumsum(onehot, axis=0)
    rank = jnp.sum(onehot * csum, axis=1) - 1
    counts = csum[-1]
    pcounts = (counts + MOE_ROWS - 1) // MOE_ROWS * MOE_ROWS
    pends = jnp.cumsum(pcounts)
    pstarts = pends - pcounts
    dest = jnp.sum(onehot * pstarts[None, :], axis=1) + rank
    pair = jnp.arange(n_pairs, dtype=jnp.int32)
    row_pair = jnp.full((p,), -1, jnp.int32).at[dest].set(pair)
    is_pad = row_pair < 0
    pad_rank = jnp.cumsum(is_pad.astype(jnp.int32)) - 1
    row_tok = jnp.where(is_pad, 0, row_pair // TOP_K)
    row_dst = jnp.where(is_pad, n_pairs + pad_rank, (row_pair % TOP_K) * n_tok + row_tok)
    block_start = jnp.arange(n_blocks, dtype=jnp.int32) * MOE_ROWS
    blk_e = jnp.minimum(jnp.sum((block_start[:, None] >= pends[None, :]).astype(jnp.int32), axis=1),
                        N_EXPERTS - 1).astype(jnp.int32)
    idx = jnp.stack([row_tok.reshape(n_blocks, MOE_ROWS), row_dst.reshape(n_blocks, MOE_ROWS)], axis=1)
    return blk_e, idx.astype(jnp.int32)


def _final_body(h_ref, first_ref, second_ref, comb_ref, g_ref, o_ref):
    x = h_ref[...] + (first_ref[...] * comb_ref[:, 0:1] + second_ref[...] * comb_ref[:, 1:2])
    o_ref[...] = x * lax.rsqrt(jnp.mean(x * x, axis=-1, keepdims=True) + EPS) * g_ref[...]


def _final(h, pair_out, comb, g_final, *, row0, n_rows):
    n_tok = h.shape[0]
    assert row0 % TM == 0 and n_rows % TM == 0 and n_tok % TM == 0
    blk0, plane = row0 // TM, n_tok // TM
    row = lambda n, off: pl.BlockSpec((TM, n), lambda i: (off + i, 0))
    return pl.pallas_call(
        _final_body,
        grid=(n_rows // TM,),
        in_specs=[row(D_MODEL, blk0), row(D_MODEL, blk0), row(D_MODEL, plane + blk0), row(LANES, blk0),
                  pl.BlockSpec((1, D_MODEL), lambda i: (0, 0))],
        out_specs=pl.BlockSpec((TM, D_MODEL), lambda i: (i, 0)),
        out_shape=jax.ShapeDtypeStruct((n_rows, D_MODEL), F32),
        compiler_params=_params("parallel"),
    )(h, pair_out, pair_out, comb, g_final.reshape(1, D_MODEL))


def _reorder_last(x, shape, order):
    lead = x.shape[:-1]
    n = len(lead)
    y = x.reshape(lead + shape).transpose(tuple(range(n)) + tuple(n + o for o in order))
    return y.reshape(lead + (x.shape[-1],))


HALF_N = RWKV_N // 2


def _key_major(x):
    return _reorder_last(x, (RWKV_HEADS, RWKV_N), (1, 0))


def _key_major_inv(x):
    return _reorder_last(x, (RWKV_N, RWKV_HEADS), (1, 0))


def _value_major(x):
    return _reorder_last(x, (RWKV_HEADS, 2, HALF_N), (2, 1, 0))


def _value_major_inv(x):
    return _reorder_last(x, (HALF_N, 2, RWKV_HEADS), (2, 1, 0))


def _rwkv_cols(x, key_fn, value_fn):
    return jnp.concatenate([key_fn(x[..., :RWKV_W]), key_fn(x[..., RWKV_W:2 * RWKV_W]),
                            value_fn(x[..., 2 * RWKV_W:3 * RWKV_W]), x[..., 3 * RWKV_W:]], axis=-1)


def _sample_key_layout(x, b, t):
    return x.reshape(b, t, RWKV_N, RWKV_HEADS).transpose(3, 1, 2, 0)


def _sample_value_layout(x, b, t):
    return x.reshape(b, t, HALF_N, 2, RWKV_HEADS).transpose(4, 1, 3, 2, 0).reshape(RWKV_HEADS, t, RWKV_N, b)


def _sample_value_unlayout(y, b, t):
    return y.reshape(RWKV_HEADS, t, 2, HALF_N, b).transpose(4, 1, 3, 2, 0).reshape(b * t, RWKV_W)


def kernel(x_prompt, x_sample, mem_prompt, state_ret, state_rwkv, state_shift, cache_mem_k, cache_mem_v,
           g_mix, w_in, ret_gn, rwkv_mu, rwkv_w0, rwkv_w2, rwkv_a0, rwkv_a2, rwkv_g2, rwkv_k_k, rwkv_k_a,
           rwkv_r_k, rwkv_lnx_w, rwkv_lnx_b, w_out, g_mem_q, g_mem_kv, w_mq, w_mk, w_mv, w_mo, g_ffn,
           w_group_router, b_group_router, w_expert_router, b_expert_router, w_e_gate, w_e_up, w_e_down,
           g_final):
    assert w_in.shape[0] == 1, "single-layer decoder"
    bp, tp, d = x_prompt.shape
    bs, ts, _ = x_sample.shape
    np_tok, ns_tok = bp * tp, bs * ts
    assert d == D_MODEL and bp * RWKV_HEADS * 2 == LANES and bs == LANES
    l = 0
    x_parts = [x_prompt.reshape(np_tok, d), x_sample.reshape(ns_tok, d)]

    w_in_l = jnp.concatenate([w_in[l][:, :N_RET_COLS], _rwkv_cols(w_in[l][:, N_RET_COLS:], _key_major, _value_major)],
                             axis=1)
    (proj,) = _matmul(x_parts, [w_in_l.astype(BF16)], gain=g_mix[l])

    pos_p = np.arange(tp)
    pos_s = PAST_LEN + np.arange(ts)
    zero_ret = jnp.zeros((bp, RET_HEADS, RET_DK, RET_DV), F32)
    oret_p, sret_p = _retention(proj, zero_ret, ret_gn[l], pos_p, row0=0, n_batch=bp, t=tp, n_seq=1)
    oret_s, sret_s = _retention(proj, state_ret[l], ret_gn[l], pos_s, row0=np_tok, n_batch=bs, t=ts, n_seq=16)

    pre_w = (_rwkv_cols(rwkv_mu[l], _key_major, _value_major), _key_major(rwkv_w0[l]), _key_major(rwkv_w2[l]),
             _key_major(rwkv_a0[l]), _key_major(rwkv_a2[l]), _value_major(rwkv_g2[l]))
    zero_shift = jnp.zeros((bp, N_RWKV_COLS), F32)
    shift_in = _rwkv_cols(state_shift[l], _key_major, _value_major)
    r_p, k_p, v_p, w_p, a_p, gate_p = _rwkv_pre(proj, zero_shift, *pre_w, row0=0, n_batch=bp, t=tp,
                                                 n_seq=1, c=256, transposed=True)
    r_s, k_s, v_s, w_s, a_s, gate_s = _rwkv_pre(proj, shift_in, *pre_w, row0=np_tok, n_batch=bs, t=ts,
                                                 n_seq=16, c=ts, transposed=False)

    kvec = lambda v: v.reshape(RWKV_HEADS, RWKV_N)
    key_par = lambda v: jnp.broadcast_to(kvec(v).T[:, None, None, :], (RWKV_N, 2, bp, RWKV_HEADS)).reshape(
        RWKV_N, LANES)

    val_par = lambda v: jnp.broadcast_to(
        v.reshape(RWKV_HEADS, 2, HALF_N).transpose(2, 1, 0)[:, :, None, :],
        (HALF_N, 2, bp, RWKV_HEADS)).reshape(HALF_N, LANES)
    mem_shape = (N_MEM, MEM_HEADS, MEM_DH)
    y_p, srw_p, cache_k_heads, cache_v_heads = _rwkv_scan_prompt(
        r_p, k_p, w_p, a_p, v_p, key_par(rwkv_k_k[l]), key_par(rwkv_k_a[l]), key_par(rwkv_r_k[l]),
        val_par(rwkv_lnx_w[l]), val_par(rwkv_lnx_b[l]), jnp.zeros((HALF_N, RWKV_N, LANES), F32),
        cache_mem_k.reshape(bs, *mem_shape), cache_mem_v.reshape(bs, *mem_shape))
    srw_p = srw_p.reshape(HALF_N, RWKV_N, 2, bp, RWKV_HEADS).transpose(3, 4, 2, 0, 1).reshape(
        bp, RWKV_HEADS, RWKV_N, RWKV_N)
    head_par = lambda v: jnp.broadcast_to(kvec(v)[:, :, None], (RWKV_HEADS, RWKV_N, LANES))
    y_s, srw_s = _rwkv_scan(
        _sample_key_layout(r_s, bs, ts), _sample_key_layout(k_s, bs, ts), _sample_key_layout(w_s, bs, ts),
        _sample_key_layout(a_s, bs, ts), _sample_value_layout(v_s, bs, ts),
        head_par(rwkv_k_k[l]), head_par(rwkv_k_a[l]), head_par(rwkv_r_k[l]),
        head_par(rwkv_lnx_w[l]), head_par(rwkv_lnx_b[l]),
        state_rwkv[l].astype(F32).transpose(1, 2, 3, 0), tc=ts, halves=1)
    y_s = _sample_value_unlayout(y_s, bs, ts)
    srw_s = srw_s.transpose(3, 0, 1, 2)

    w_rwkv_out = _value_major(w_out[l][RET_W:].T).T
    h = _merge(x_parts, [oret_p, oret_s], [gate_p, gate_s], y_p, y_s, w_out[l][:RET_W], w_rwkv_out)

    mk, mv, mk_heads, mv_heads = _mem_kv(mem_prompt, g_mem_kv[l], w_mk[l], w_mv[l])
    (q,) = _matmul([h], [w_mq[l].astype(BF16)], gain=g_mem_q[l])
    att_p = _attention(q, mk_heads, mv_heads, row0=0, n_batch=bp, t=tp, n_seq=1, tq=TM)
    att_s = _attention(q, cache_k_heads, cache_v_heads, row0=np_tok, n_batch=bs, t=ts, n_seq=16, tq=ts)
    (h,) = _matmul([att_p, att_s], [w_mo[l].astype(BF16)], residual=h)

    hn, ids, comb = _router(h, g_ffn[l], w_group_router[l], b_group_router[l], w_expert_router[l],
                            b_expert_router[l])
    blk_e, idx = _route_plan(ids[:, :TOP_K])
    pair_out = _experts(hn, blk_e, idx, w_e_gate[l], w_e_up[l], w_e_down[l])
    y_prompt = _final(h, pair_out, comb, g_final, row0=0, n_rows=np_tok).reshape(bp, tp, d)
    y_sample = _final(h, pair_out, comb, g_final, row0=np_tok, n_rows=ns_tok).reshape(bs, ts, d)

    shift_p = lax.slice(proj, (tp - 1, N_RET_COLS), (np_tok, N_IN_COLS), (tp, 1))
    shift_s = lax.slice(proj, (np_tok + ts - 1, N_RET_COLS), (np_tok + ns_tok, N_IN_COLS), (ts, 1))
    shift_p = _rwkv_cols(shift_p, _key_major_inv, _value_major_inv)
    shift_s = _rwkv_cols(shift_s, _key_major_inv, _value_major_inv)
    return (y_prompt, y_sample, sret_p[None], srw_p[None], shift_p[None],
            mk.reshape(1, bp, *mem_shape), mv.reshape(1, bp, *mem_shape),
            sret_s[None], srw_s[None], shift_s[None])
```

```python
import functools

import numpy as np
import jax
import jax.numpy as jnp
from jax import lax
from jax.experimental import pallas as pl
from jax.experimental.pallas import tpu as pltpu

F32 = jnp.float32
BF16 = jnp.bfloat16

D_MODEL = 1024
PAST_LEN = 16384
N_MEM = 256
MEM_HEADS = 4
MEM_DH = D_MODEL // MEM_HEADS
RET_HEADS = 4
RET_W = D_MODEL // 2
RET_DV = RET_W // RET_HEADS
RET_DK = RET_DV // 2
RET_QK = RET_HEADS * RET_DK
RET_CHUNK = 128
ROPE_BASE = 10000.0
RWKV_N = 64
RWKV_W = D_MODEL - RET_W
RWKV_HEADS = RWKV_W // RWKV_N
LORA_W = 64
LORA_A = 64
LORA_G = 128
LNX_EPS = 64e-5
N_RET_COLS = 2 * RET_QK + 2 * RET_W
N_RWKV_COLS = 3 * RWKV_W + LORA_W + LORA_A + LORA_G
N_IN_COLS = N_RET_COLS + N_RWKV_COLS
N_GROUPS = 4
EXP_PER_GROUP = 8
N_EXPERTS = N_GROUPS * EXP_PER_GROUP
TOP_K = 2
D_EXPERT = D_MODEL // 2
EPS = 1e-6

LANES = 128
MOE_ROWS = 128
TM = 512


def _params(*sem):
    return pltpu.CompilerParams(dimension_semantics=sem)


def _row_part_specs(parts, tm):
    specs, counts, start = [], [], 0
    for part in parts:
        nb = part.shape[0] // tm
        assert nb * tm == part.shape[0]
        specs.append(pl.BlockSpec((tm, part.shape[1]), lambda i, s=start, n=nb: (jnp.clip(i - s, 0, n - 1), 0)))
        counts.append(nb)
        start += nb
    return specs, counts


def _read_row_parts(refs, counts):
    i = pl.program_id(0)
    x = refs[0][...]
    start = counts[0]
    for ref, nb in zip(refs[1:], counts[1:]):
        x = jnp.where(i >= start, ref[...], x)
        start += nb
    return x


def _mm_body(*refs, part_counts, n_w, has_gain, has_res, n_chunk):
    it = iter(refs)
    x_refs = [next(it) for _ in part_counts]
    g_ref = next(it) if has_gain else None
    w_refs = [next(it) for _ in range(n_w)]
    r_ref = next(it) if has_res else None
    o_refs = [next(it) for _ in range(n_w)]
    x = _read_row_parts(x_refs, part_counts).astype(F32)
    if has_gain:
        x = x * lax.rsqrt(jnp.mean(x * x, axis=-1, keepdims=True) + EPS) * g_ref[...]
    xb = x.astype(BF16)
    for w_ref, o_ref in zip(w_refs, o_refs):
        for j in range(0, w_ref.shape[1], n_chunk):
            acc = jnp.dot(xb, w_ref[:, j:j + n_chunk], preferred_element_type=F32)
            if has_res:
                acc = acc + r_ref[:, j:j + n_chunk]
            o_ref[:, j:j + n_chunk] = acc


def _matmul(x_parts, ws, *, gain=None, residual=None, tm=TM):
    k = x_parts[0].shape[1]
    m = sum(part.shape[0] for part in x_parts)
    n_chunk = 256
    assert all(w.shape[1] % n_chunk == 0 for w in ws)
    assert residual is None or len(ws) == 1
    in_specs, part_counts = _row_part_specs(x_parts, tm)
    args = list(x_parts)
    if gain is not None:
        in_specs.append(pl.BlockSpec((1, k), lambda i: (0, 0)))
        args.append(gain.reshape(1, k).astype(F32))
    for w in ws:
        in_specs.append(pl.BlockSpec(w.shape, lambda i: (0, 0)))
        args.append(w)
    if residual is not None:
        in_specs.append(pl.BlockSpec((tm, ws[0].shape[1]), lambda i: (i, 0)))
        args.append(residual)
    return pl.pallas_call(
        functools.partial(_mm_body, part_counts=part_counts, n_w=len(ws), has_gain=gain is not None,
                          has_res=residual is not None, n_chunk=n_chunk),
        grid=(m // tm,),
        in_specs=in_specs,
        out_specs=[pl.BlockSpec((tm, w.shape[1]), lambda i: (i, 0)) for w in ws],
        out_shape=[jax.ShapeDtypeStruct((m, w.shape[1]), F32) for w in ws],
        compiler_params=_params("parallel"),
    )(*args)


def _rot_tables(pos):
    half = RET_DK // 2
    inv_freq = ROPE_BASE ** (-(np.arange(half, dtype=np.float64) / half))
    ang = pos.astype(np.float64)[:, None] * inv_freq[None, :]
    cos, sin = np.cos(ang), np.sin(ang)
    zero = np.zeros_like(sin)
    c = np.tile(np.concatenate([cos, cos], axis=1), (1, RET_HEADS))
    s_lo = np.tile(np.concatenate([-sin, zero], axis=1), (1, RET_HEADS))
    s_hi = np.tile(np.concatenate([zero, sin], axis=1), (1, RET_HEADS))
    return [jnp.asarray(t, F32) for t in (c, s_lo, s_hi)]


def _ret_decay_tables(c):
    lg = np.log1p(-np.exp2(-5.0 - np.arange(RET_HEADS, dtype=np.float64)))
    idx = np.arange(c, dtype=np.float64)
    diff = idx[:, None] - idx[None, :]
    mask = np.where(diff[None] >= 0, np.exp(np.maximum(diff, 0.0)[None] * lg[:, None, None]), 0.0)
    q_dec = np.repeat(np.exp((idx[:, None] + 1.0) * lg[None, :]), RET_DV, axis=1)
    k_dec = np.repeat(np.exp((c - 1.0 - idx)[:, None] * lg[None, :]), RET_DK, axis=1)
    c_dec = [float(v) for v in np.exp(c * lg)]
    return jnp.asarray(mask, F32), jnp.asarray(q_dec, F32), jnp.asarray(k_dec, F32), c_dec


def _ret_body(q_ref, k_ref, v_ref, gate_ref, c_ref, slo_ref, shi_ref, mask_ref, qdec_ref, kdec_ref,
              gn_ref, s0_ref, o_ref, sout_ref, s_scr, *, n_seq, c, c_dec):
    ci = pl.program_id(1)

    @pl.when(ci == 0)
    def _():
        s_scr[...] = s0_ref[...].astype(F32)

    cos, s_lo, s_hi = c_ref[...], slo_ref[...], shi_ref[...]
    half = RET_DK // 2

    def rope(x):
        return x * cos + pltpu.roll(x, RET_QK - half, 1) * s_lo + pltpu.roll(x, half, 1) * s_hi

    nt = (((1,), (1,)), ((), ()))
    tn = (((0,), (0,)), ((), ()))
    for g in range(n_seq):
        rows = slice(g * c, (g + 1) * c)
        q = rope(q_ref[rows, :].astype(F32))
        k = rope(k_ref[rows, :].astype(F32)) * (RET_DK ** -0.5)
        k_st = k * kdec_ref[...]
        for h in range(RET_HEADS):
            kc = slice(h * RET_DK, (h + 1) * RET_DK)
            vc = slice(h * RET_DV, (h + 1) * RET_DV)
            qh = q[:, kc].astype(BF16)
            vh = v_ref[rows, vc].astype(BF16)
            s_h = s_scr[g, h]
            att = lax.dot_general(qh, k[:, kc].astype(BF16), nt, preferred_element_type=F32) * mask_ref[h]
            o = jnp.dot(att.astype(BF16), vh, preferred_element_type=F32)
            o = o + jnp.dot(qh, s_h.astype(BF16), preferred_element_type=F32) * qdec_ref[:, vc]
            s_scr[g, h] = s_h * c_dec[h] + lax.dot_general(
                k_st[:, kc].astype(BF16), vh, tn, preferred_element_type=F32)
            o = o * lax.rsqrt(jnp.mean(o * o, axis=-1, keepdims=True) + EPS)
            gate = gate_ref[rows, vc].astype(F32)
            o_ref[rows, vc] = o * gn_ref[:, vc] * (gate * jax.nn.sigmoid(gate))

    @pl.when(ci == pl.num_programs(1) - 1)
    def _():
        sout_ref[...] = s_scr[...]


def _retention(proj, s0, ret_gn, pos, *, row0, n_batch, t, n_seq):
    c = RET_CHUNK if t % RET_CHUNK == 0 else t
    n_chunks = t // c
    rows = n_seq * c
    assert n_batch % n_seq == 0 and row0 % rows == 0 and (n_seq == 1 or n_chunks == 1)
    blk0 = row0 // rows
    mask, q_dec, k_dec, c_dec = _ret_decay_tables(c)
    cos, s_lo, s_hi = _rot_tables(pos)

    def row_map(col):
        return lambda b, ci: (blk0 + b * n_chunks + ci, col)

    def const2(b, ci):
        return (0, 0)

    state_spec = pl.BlockSpec((n_seq, RET_HEADS, RET_DK, RET_DV), lambda b, ci: (b, 0, 0, 0))
    in_specs = [
        pl.BlockSpec((rows, RET_QK), row_map(0)),
        pl.BlockSpec((rows, RET_QK), row_map(1)),
        pl.BlockSpec((rows, RET_W), row_map(1)),
        pl.BlockSpec((rows, RET_W), row_map(2)),
        pl.BlockSpec((c, RET_QK), lambda b, ci: (ci, 0)),
        pl.BlockSpec((c, RET_QK), lambda b, ci: (ci, 0)),
        pl.BlockSpec((c, RET_QK), lambda b, ci: (ci, 0)),
        pl.BlockSpec((RET_HEADS, c, c), lambda b, ci: (0, 0, 0)),
        pl.BlockSpec((c, RET_W), const2),
        pl.BlockSpec((c, RET_QK), const2),
        pl.BlockSpec((1, RET_W), const2),
        state_spec,
    ]
    return pl.pallas_call(
        functools.partial(_ret_body, n_seq=n_seq, c=c, c_dec=c_dec),
        grid=(n_batch // n_seq, n_chunks),
        in_specs=in_specs,
        out_specs=[pl.BlockSpec((rows, RET_W), lambda b, ci: (b * n_chunks + ci, 0)), state_spec],
        out_shape=[jax.ShapeDtypeStruct((n_batch * t, RET_W), F32),
                   jax.ShapeDtypeStruct((n_batch, RET_HEADS, RET_DK, RET_DV), F32)],
        scratch_shapes=[pltpu.VMEM((n_seq, RET_HEADS, RET_DK, RET_DV), F32)],
        compiler_params=_params("parallel", "arbitrary"),
    )(proj, proj, proj, proj, cos, s_lo, s_hi, mask, q_dec, k_dec, ret_gn.reshape(1, RET_W).astype(F32), s0)


def _rwkv_pre_body(r_ref, k_ref, v_ref, lo_ref, shift_ref, mu_ref, w0_ref, w2_ref, a0_ref, a2_ref, g2_ref,
                   ro_ref, ko_ref, vo_ref, wo_ref, ao_ref, go_ref, prev_scr, *, n_seq, c, transposed):
    ci = pl.program_id(1)

    @pl.when(ci == 0)
    def _():
        for g in range(n_seq):
            prev_scr[g] = shift_ref[g].astype(F32)

    first_row = lax.broadcasted_iota(jnp.int32, (c, 1), 0) == 0

    def shifted(x_ref, g, col0):
        w = x_ref.shape[1]
        x = x_ref[g * c:(g + 1) * c, :].astype(F32)
        prev_row = prev_scr[g, :, col0:col0 + w]
        prev = jnp.where(first_row, prev_row, pltpu.roll(x, 1, 0))
        prev_scr[g, :, col0:col0 + w] = x[c - 1:c, :]
        return x + (prev - x) * mu_ref[:, col0:col0 + w]

    def put(o_ref, g, val):
        if transposed:
            o_ref[...] = val.T
        else:
            o_ref[g * c:(g + 1) * c, :] = val

    for g in range(n_seq):
        put(ro_ref, g, shifted(r_ref, g, 0))
        put(ko_ref, g, shifted(k_ref, g, RWKV_W))
        put(vo_ref, g, shifted(v_ref, g, 2 * RWKV_W))
        lo = shifted(lo_ref, g, 3 * RWKV_W)
        hw = lo[:, :LORA_W]
        ha = lo[:, LORA_W:LORA_W + LORA_A]
        hg = lo[:, LORA_W + LORA_A:]
        u = w0_ref[...] + jnp.dot(jnp.tanh(hw).astype(BF16), w2_ref[...], preferred_element_type=F32)
        put(wo_ref, g, jnp.exp(-float(np.exp(-0.5)) * jax.nn.sigmoid(u)))
        put(ao_ref, g, jax.nn.sigmoid(
            a0_ref[...] + jnp.dot(ha.astype(BF16), a2_ref[...], preferred_element_type=F32)))
        go_ref[g * c:(g + 1) * c, :] = jnp.dot(jax.nn.sigmoid(hg).astype(BF16), g2_ref[...],
                                                preferred_element_type=F32)


def _rwkv_pre(proj, s_shift, mu, w0, w2, a0, a2, g2, *, row0, n_batch, t, n_seq, c, transposed):
    n_chunks = t // c
    rows = n_seq * c
    assert t % c == 0 and n_batch % n_seq == 0 and row0 % rows == 0 and (n_seq == 1 or n_chunks == 1)
    assert not transposed or n_seq == 1
    blk0 = row0 // rows
    col_r = N_RET_COLS // RWKV_W
    lo_w = LORA_W + LORA_A + LORA_G
    col_lo = (N_RET_COLS + 3 * RWKV_W) // lo_w
    assert col_r * RWKV_W == N_RET_COLS and col_lo * lo_w == N_RET_COLS + 3 * RWKV_W

    def row_map(col):
        return lambda b, ci: (blk0 + b * n_chunks + ci, col)

    def const2(b, ci):
        return (0, 0)

    in_specs = [
        pl.BlockSpec((rows, RWKV_W), row_map(col_r)),
        pl.BlockSpec((rows, RWKV_W), row_map(col_r + 1)),
        pl.BlockSpec((rows, RWKV_W), row_map(col_r + 2)),
        pl.BlockSpec((rows, lo_w), row_map(col_lo)),
        pl.BlockSpec((n_seq, 1, N_RWKV_COLS), lambda b, ci: (b, 0, 0)),
        pl.BlockSpec((1, N_RWKV_COLS), const2),
        pl.BlockSpec((1, RWKV_W), const2),
        pl.BlockSpec((LORA_W, RWKV_W), const2),
        pl.BlockSpec((1, RWKV_W), const2),
        pl.BlockSpec((LORA_A, RWKV_W), const2),
        pl.BlockSpec((LORA_G, RWKV_W), const2),
    ]
    nat_spec = pl.BlockSpec((rows, RWKV_W), lambda b, ci: (b * n_chunks + ci, 0))
    nat_shape = jax.ShapeDtypeStruct((n_batch * t, RWKV_W), F32)
    if transposed:
        vec_spec = pl.BlockSpec((None, RWKV_W, c), lambda b, ci: (b, 0, ci))
        vec_shape = jax.ShapeDtypeStruct((n_batch, RWKV_W, t), F32)
    else:
        vec_spec, vec_shape = nat_spec, nat_shape
    return pl.pallas_call(
        functools.partial(_rwkv_pre_body, n_seq=n_seq, c=c, transposed=transposed),
        grid=(n_batch // n_seq, n_chunks),
        in_specs=in_specs,
        out_specs=[vec_spec] * 5 + [nat_spec],
        out_shape=[vec_shape] * 5 + [nat_shape],
        scratch_shapes=[pltpu.VMEM((n_seq, 1, N_RWKV_COLS), F32)],
        compiler_params=_params("parallel", "arbitrary"),
    )(proj, proj, proj, proj, s_shift.reshape(n_batch, 1, N_RWKV_COLS),
      mu.reshape(1, -1), w0.reshape(1, -1), w2.astype(BF16), a0.reshape(1, -1),
      a2.astype(BF16), g2.astype(BF16))


def _scan_body(r_ref, k_ref, w_ref, a_ref, v_ref, kk_ref, ka_ref, rk_ref, lw_ref, lb_ref, s0_ref,
               y_ref, sout_ref, s_scr, a_scr, b_scr, km_scr, *, tc, vr, halves):
    ci = pl.program_id(1)

    @pl.when(ci == 0)
    def _():
        s_scr[...] = s0_ref[...].astype(F32)

    def ksum(x):
        return jnp.sum(x, axis=-2, keepdims=True)

    def vsum(x):
        if halves == 2:
            x2 = x.reshape(tc * vr, LANES)
            x = (x2 + pltpu.roll(x2, LANES // 2, 1)).reshape(tc, vr, LANES)
        return jnp.sum(x, axis=1, keepdims=True)

    kr = k_ref[...]
    a = a_ref[...]
    kk = kr * kk_ref[...]
    kk = kk / jnp.maximum(jnp.sqrt(ksum(kk * kk)), 1e-12)
    a_scr[...] = -kk
    b_scr[...] = kk * a
    km_scr[...] = kr * (1.0 + (a - 1.0) * ka_ref[...])

    def token(t, carry):
        r, w, avec, bvec, kmod = r_ref[t], w_ref[t], a_scr[t], b_scr[t], km_scr[t]

        def value_row(i, c2):
            s = s_scr[i]
            sa = ksum(s * avec)
            s = s * w + sa * bvec + v_ref[t, pl.ds(i, 1), :] * kmod
            s_scr[i] = s
            y_ref[t, pl.ds(i, 1), :] = ksum(s * r)
            return c2

        lax.fori_loop(0, vr, value_row, 0, unroll=16)
        return carry

    lax.fori_loop(0, tc, token, 0)

    y = y_ref[...]
    d = y - vsum(y) * (1.0 / RWKV_N)
    var = vsum(d * d) * (1.0 / RWKV_N)
    bonus = ksum(r_ref[...] * km_scr[...] * rk_ref[...])
    y_ref[...] = d * lax.rsqrt(var + LNX_EPS) * lw_ref[...] + lb_ref[...] + bonus * v_ref[...]

    @pl.when(ci == pl.num_programs(1) - 1)
    def _():
        sout_ref[...] = s_scr[...]


def _rwkv_scan(r, k, w, a, v, k_k, k_a, r_k, lnx_w, lnx_b, s0, *, tc, halves):
    n_grp, t, _, lanes = r.shape
    vr = v.shape[2]
    assert lanes == LANES and t % tc == 0 and vr * halves == RWKV_N

    def tok_spec(rows):
        return pl.BlockSpec((None, tc, rows, LANES), lambda g, ci: (g, ci, 0, 0))

    def par_spec(rows):
        return pl.BlockSpec((None, rows, LANES), lambda g, ci: (g, 0, 0))

    st_spec = pl.BlockSpec((None, vr, RWKV_N, LANES), lambda g, ci: (g, 0, 0, 0))
    key_scratch = pltpu.VMEM((tc, RWKV_N, LANES), F32)
    return pl.pallas_call(
        functools.partial(_scan_body, tc=tc, vr=vr, halves=halves),
        grid=(n_grp, t // tc),
        in_specs=[tok_spec(RWKV_N)] * 4 + [tok_spec(vr)] + [par_spec(RWKV_N)] * 3 + [par_spec(vr)] * 2 + [st_spec],
        out_specs=[tok_spec(vr), st_spec],
        out_shape=[jax.ShapeDtypeStruct((n_grp, t, vr, LANES), F32),
                   jax.ShapeDtypeStruct((n_grp, vr, RWKV_N, LANES), F32)],
        scratch_shapes=[pltpu.VMEM((vr, RWKV_N, LANES), F32), key_scratch, key_scratch, key_scratch],
        compiler_params=_params("parallel", "arbitrary"),
    )(r, k, w, a, v, k_k, k_a, r_k, lnx_w, lnx_b, s0)


SCAN_TC = 128
SCAN_SUB = 64
MEM_CHUNKS = 4


def _scan_prompt_body(r_ref, k_ref, w_ref, a_ref, v_ref, kk_ref, ka_ref, rk_ref, lw_ref, lb_ref, s0_ref,
                      memk_hbm, memv_hbm, y_ref, sout_ref, memk_out, memv_out,
                      s_scr, r_c, w_c, a_c, b_c, km_c, v_c, y_c, mem_stage, mem_in_sem, mem_out_sem,
                      *, n_b, mem_seqs):
    ci = pl.program_id(0)
    vr = RWKV_N // 2
    ts = SCAN_SUB
    tile = RWKV_HEADS
    half_lanes = LANES // 2

    @pl.when(ci == 0)
    def _():
        s_scr[...] = s0_ref[...].astype(F32)

    chunk_seqs = mem_seqs // MEM_CHUNKS

    def mem_in(chunk, slot):
        copies = []
        for j in range(chunk_seqs):
            seq = (ci * MEM_CHUNKS + chunk) * chunk_seqs + j
            for h in range(MEM_HEADS):
                copies.append(pltpu.make_async_copy(memk_hbm.at[seq, :, h, :], mem_stage.at[slot, 0, j, h],
                                                    mem_in_sem.at[slot]))
                copies.append(pltpu.make_async_copy(memv_hbm.at[seq, :, h, :], mem_stage.at[slot, 1, j, h],
                                                    mem_in_sem.at[slot]))
        return copies

    def mem_out(chunk, slot):
        seqs = pl.ds((ci * MEM_CHUNKS + chunk) * chunk_seqs, chunk_seqs)
        return [pltpu.make_async_copy(mem_stage.at[slot, 0], memk_out.at[seqs], mem_out_sem.at[slot]),
                pltpu.make_async_copy(mem_stage.at[slot, 1], memv_out.at[seqs], mem_out_sem.at[slot])]

    def mem_phase(p):
        if 1 <= p <= MEM_CHUNKS:
            for cp in mem_in(p - 1, (p - 1) % 2):
                cp.wait()
            for cp in mem_out(p - 1, (p - 1) % 2):
                cp.start()
        if 2 <= p <= MEM_CHUNKS + 1:
            for cp in mem_out(p - 2, p % 2):
                cp.wait()
        if p < MEM_CHUNKS:
            for cp in mem_in(p, p % 2):
                cp.start()

    mem_phase(0)

    low = lax.broadcasted_iota(jnp.int32, (ts, LANES), 1) < half_lanes

    def feature_pair_rows(x_ref, base):
        tiles = [x_ref[b, pl.ds(base + f * tile, tile), :] for f in range(2) for b in range(n_b)]
        return jnp.concatenate(tiles, axis=0).T

    def key_to_chain(x_ref, dst, t0):
        def group(g, c):
            rows = []
            for j in range(4):
                mt = feature_pair_rows(x_ref, pl.multiple_of((g * 4 + j) * 2 * tile, 2 * tile))[t0:t0 + ts]
                sw = pltpu.roll(mt, half_lanes, 1)
                rows += [jnp.where(low, mt, sw), jnp.where(low, sw, mt)]
            dst[:, pl.ds(pl.multiple_of(g * 8, 8), 8), :] = jnp.swapaxes(jnp.stack(rows, axis=0), 0, 1)
            return c
        lax.fori_loop(0, RWKV_N // 8, group, 0)

    def value_to_chain(g, c):
        rows = [feature_pair_rows(v_ref, pl.multiple_of((g * 8 + j) * 2 * tile, 2 * tile)) for j in range(8)]
        v_c[:, pl.ds(pl.multiple_of(g * 8, 8), 8), :] = jnp.swapaxes(jnp.stack(rows, axis=0), 0, 1)
        return c
    lax.fori_loop(0, vr // 8, value_to_chain, 0)

    def ksum(x):
        return jnp.sum(x, axis=-2, keepdims=True)

    for t0 in range(0, SCAN_TC, ts):
        key_to_chain(r_ref, r_c, t0)
        key_to_chain(w_ref, w_c, t0)
        key_to_chain(k_ref, km_c, t0)
        key_to_chain(a_ref, b_c, t0)

        def prep(g8, c):
            toks = pl.ds(pl.multiple_of(g8 * 8, 8), 8)
            kr = km_c[toks]
            a = b_c[toks]
            kk = kr * kk_ref[...]
            kk = kk / jnp.maximum(jnp.sqrt(ksum(kk * kk)), 1e-12)
            a_c[toks] = -kk
            b_c[toks] = kk * a
            km_c[toks] = kr * (1.0 + (a - 1.0) * ka_ref[...])
            return c
        lax.fori_loop(0, ts // 8, prep, 0)
        mem_phase(1 + 2 * (t0 // ts))

        def token(t, carry):
            r, w, avec, bvec, kmod = r_c[t], w_c[t], a_c[t], b_c[t], km_c[t]

            def value_row(i, c2):
                s = s_scr[i]
                sa = ksum(s * avec)
                s = s * w + sa * bvec + v_c[t0 + t, pl.ds(i, 1), :] * kmod
                s_scr[i] = s
                y_c[t0 + t, pl.ds(i, 1), :] = ksum(s * r)
                return c2

            lax.fori_loop(0, vr, value_row, 0, unroll=True)
            return carry

        lax.fori_loop(0, ts, token, 0)
        mem_phase(2 + 2 * (t0 // ts))

        def post(g8, c):
            ktoks = pl.ds(pl.multiple_of(g8 * 8, 8), 8)
            vtoks = pl.ds(pl.multiple_of(t0 + g8 * 8, 8), 8)

            def vsum(x):
                x2 = x.reshape(8 * vr, LANES)
                x2 = x2 + pltpu.roll(x2, half_lanes, 1)
                return jnp.sum(x2.reshape(8, vr, LANES), axis=1, keepdims=True)

            y = y_c[vtoks]
            d = y - vsum(y) * (1.0 / RWKV_N)
            var = vsum(d * d) * (1.0 / RWKV_N)
            bonus = ksum(r_c[ktoks] * km_c[ktoks] * rk_ref[...])
            y_c[vtoks] = d * lax.rsqrt(var + LNX_EPS) * lw_ref[...] + lb_ref[...] + bonus * v_c[vtoks]
            return c
        lax.fori_loop(0, ts // 8, post, 0)

    def value_from_chain(g, c):
        blk = jnp.swapaxes(y_c[:, pl.ds(pl.multiple_of(g * 8, 8), 8), :], 0, 1)
        for j in range(8):
            mt = blk[j].T
            base = pl.multiple_of((g * 8 + j) * 2 * tile, 2 * tile)
            for hf in range(2):
                for b in range(n_b):
                    row0 = (hf * n_b + b) * tile
                    y_ref[b, pl.ds(base + hf * tile, tile), :] = mt[row0:row0 + tile, :]
        return c
    lax.fori_loop(0, vr // 8, value_from_chain, 0)

    assert 2 * (SCAN_TC // ts) == MEM_CHUNKS
    mem_phase(MEM_CHUNKS + 1)

    @pl.when(ci == pl.num_programs(0) - 1)
    def _():
        sout_ref[...] = s_scr[...]


def _rwkv_scan_prompt(r, k, w, a, v, k_k, k_a, r_k, lnx_w, lnx_b, s0, mem_k, mem_v):
    n_b, _, t = r.shape
    vr = RWKV_N // 2
    n_steps = t // SCAN_TC
    n_mem_seq = mem_k.shape[0]
    assert t % SCAN_TC == 0 and 2 * n_b * RWKV_HEADS == LANES and n_mem_seq % (n_steps * MEM_CHUNKS) == 0
    chunk_seqs = n_mem_seq // (n_steps * MEM_CHUNKS)
    any_spec = pl.BlockSpec(memory_space=pl.ANY)
    mem_shape = jax.ShapeDtypeStruct((n_mem_seq, MEM_HEADS, N_MEM, MEM_DH), F32)
    tok_spec = pl.BlockSpec((n_b, RWKV_W, SCAN_TC), lambda ci: (0, 0, ci))
    key_par = pl.BlockSpec((RWKV_N, LANES), lambda ci: (0, 0))
    val_par = pl.BlockSpec((vr, LANES), lambda ci: (0, 0))
    st_spec = pl.BlockSpec((vr, RWKV_N, LANES), lambda ci: (0, 0, 0))
    key_chain = pltpu.VMEM((SCAN_SUB, RWKV_N, LANES), F32)
    val_chain = pltpu.VMEM((SCAN_TC, vr, LANES), F32)
    return pl.pallas_call(
        functools.partial(_scan_prompt_body, n_b=n_b, mem_seqs=n_mem_seq // n_steps),
        grid=(n_steps,),
        in_specs=[tok_spec] * 5 + [key_par] * 3 + [val_par] * 2 + [st_spec, any_spec, any_spec],
        out_specs=[tok_spec, st_spec, any_spec, any_spec],
        out_shape=[jax.ShapeDtypeStruct((n_b, RWKV_W, t), F32),
                   jax.ShapeDtypeStruct((vr, RWKV_N, LANES), F32), mem_shape, mem_shape],
        scratch_shapes=([pltpu.VMEM((vr, RWKV_N, LANES), F32)] + [key_chain] * 5 + [val_chain] * 2
                        + [pltpu.VMEM((2, 2, chunk_seqs, MEM_HEADS, N_MEM, MEM_DH), F32),
                           pltpu.SemaphoreType.DMA((2,)), pltpu.SemaphoreType.DMA((2,))]),
        compiler_params=_params("arbitrary"),
    )(r, k, w, a, v, k_k, k_a, r_k, lnx_w, lnx_b, s0, mem_k, mem_v)


def _merge_body(*refs, part_counts):
    n = len(part_counts)
    x_refs, oret_refs, g_refs = refs[:n], refs[n:2 * n], refs[2 * n:3 * n]
    yt_ref, ys_ref, wt_ref, wb_ref, o_ref = refs[3 * n:]
    x = _read_row_parts(x_refs, part_counts)
    y = jnp.where(pl.program_id(0) >= part_counts[0], ys_ref[...], yt_ref[...].T)
    yb = (y * _read_row_parts(g_refs, part_counts)).astype(BF16)
    ob = _read_row_parts(oret_refs, part_counts).astype(BF16)
    n_chunk = 256
    for j in range(0, D_MODEL, n_chunk):
        acc = jnp.dot(ob, wt_ref[:, j:j + n_chunk], preferred_element_type=F32)
        acc = acc + jnp.dot(yb, wb_ref[:, j:j + n_chunk], preferred_element_type=F32)
        o_ref[:, j:j + n_chunk] = x[:, j:j + n_chunk] + acc


def _merge(x_parts, oret_parts, g_parts, y_first_t, y_second, w_ret, w_rwkv):
    m = sum(part.shape[0] for part in x_parts)
    in_specs, part_counts = [], None
    for parts in (x_parts, oret_parts, g_parts):
        specs, part_counts = _row_part_specs(parts, TM)
        in_specs += specs
    assert len(part_counts) == 2
    n_first = part_counts[0]
    tiles = y_first_t.shape[2] // TM
    assert y_first_t.shape[0] * tiles == n_first
    yt_spec = pl.BlockSpec((None, RWKV_W, TM),
                           lambda i: (jnp.minimum(i, n_first - 1) // tiles, 0, jnp.minimum(i, n_first - 1) % tiles))
    ys_spec = pl.BlockSpec((TM, RWKV_W), lambda i: (jnp.clip(i - n_first, 0, part_counts[1] - 1), 0))
    wspec = pl.BlockSpec((RET_W, D_MODEL), lambda i: (0, 0))
    return pl.pallas_call(
        functools.partial(_merge_body, part_counts=part_counts),
        grid=(m // TM,),
        in_specs=in_specs + [yt_spec, ys_spec, wspec, wspec],
        out_specs=pl.BlockSpec((TM, D_MODEL), lambda i: (i, 0)),
        out_shape=jax.ShapeDtypeStruct((m, D_MODEL), F32),
        compiler_params=_params("parallel"),
    )(*x_parts, *oret_parts, *g_parts, y_first_t, y_second, w_ret.astype(BF16), w_rwkv.astype(BF16))


def _mem_kv_body(x_ref, g_ref, wk_ref, wv_ref, k_ref, v_ref, kh_ref, vh_ref, *, n_seq):
    x = x_ref[...].astype(F32)
    xb = (x * lax.rsqrt(jnp.mean(x * x, axis=-1, keepdims=True) + EPS) * g_ref[...]).astype(BF16)
    for w_ref, o_ref, oh_ref in ((wk_ref, k_ref, kh_ref), (wv_ref, v_ref, vh_ref)):
        for h in range(MEM_HEADS):
            acc = jnp.dot(xb, w_ref[:, h * MEM_DH:(h + 1) * MEM_DH], preferred_element_type=F32)
            o_ref[:, h, :] = acc
            for s in range(n_seq):
                oh_ref[s, h] = acc[s * N_MEM:(s + 1) * N_MEM]


def _mem_kv(mem, gain, w_k, w_v):
    n_b = mem.shape[0]
    n_seq = TM // N_MEM
    assert n_seq * N_MEM == TM and n_b % n_seq == 0
    wspec = pl.BlockSpec((D_MODEL, D_MODEL), lambda i: (0, 0))
    tok_spec = pl.BlockSpec((TM, MEM_HEADS, MEM_DH), lambda i: (i, 0, 0))
    head_spec = pl.BlockSpec((n_seq, MEM_HEADS, N_MEM, MEM_DH), lambda i: (i, 0, 0, 0))
    tok_shape = jax.ShapeDtypeStruct((n_b * N_MEM, MEM_HEADS, MEM_DH), F32)
    head_shape = jax.ShapeDtypeStruct((n_b, MEM_HEADS, N_MEM, MEM_DH), F32)
    return pl.pallas_call(
        functools.partial(_mem_kv_body, n_seq=n_seq),
        grid=(n_b // n_seq,),
        in_specs=[pl.BlockSpec((TM, D_MODEL), lambda i: (i, 0)), pl.BlockSpec((1, D_MODEL), lambda i: (0, 0)),
                  wspec, wspec],
        out_specs=[tok_spec, tok_spec, head_spec, head_spec],
        out_shape=[tok_shape, tok_shape, head_shape, head_shape],
        compiler_params=_params("parallel"),
    )(mem.reshape(n_b * N_MEM, D_MODEL), gain.reshape(1, D_MODEL), w_k.astype(BF16), w_v.astype(BF16))


def _attn_body(q_ref, k_ref, v_ref, o_ref, *, n_seq, tq):
    nt = (((1,), (1,)), ((), ()))
    for g in range(n_seq):
        rows = slice(g * tq, (g + 1) * tq)
        q = q_ref[rows, :].astype(BF16)
        s = lax.dot_general(q, k_ref[g].astype(BF16), nt, preferred_element_type=F32) * (MEM_DH ** -0.5)
        p = jnp.exp(s - jnp.max(s, axis=-1, keepdims=True))
        l = jnp.sum(p, axis=-1, keepdims=True)
        o = jnp.dot(p.astype(BF16), v_ref[g].astype(BF16), preferred_element_type=F32)
        o_ref[rows, :] = o / l


def _attention(q, mem_k, mem_v, *, row0, n_batch, t, n_seq, tq):
    q_tiles = t // tq
    rows = n_seq * tq
    assert t % tq == 0 and n_batch % n_seq == 0 and row0 % rows == 0 and (n_seq == 1 or q_tiles == 1)
    blk0 = row0 // rows
    kv_spec = pl.BlockSpec((n_seq, None, N_MEM, MEM_DH), lambda b, h, qi: (b, h, 0, 0))
    return pl.pallas_call(
        functools.partial(_attn_body, n_seq=n_seq, tq=tq),
        grid=(n_batch // n_seq, MEM_HEADS, q_tiles),
        in_specs=[pl.BlockSpec((rows, MEM_DH), lambda b, h, qi: (blk0 + b * q_tiles + qi, h)), kv_spec, kv_spec],
        out_specs=pl.BlockSpec((rows, MEM_DH), lambda b, h, qi: (b * q_tiles + qi, h)),
        out_shape=jax.ShapeDtypeStruct((n_batch * t, D_MODEL), F32),
        compiler_params=_params("parallel", "parallel", "parallel"),
    )(q, mem_k, mem_v)


def _router_body(h_ref, g_ref, w_ref, b_ref, hn_ref, ids_ref, comb_ref):
    x = h_ref[...]
    hn = x * lax.rsqrt(jnp.mean(x * x, axis=-1, keepdims=True) + EPS) * g_ref[...]
    hn_ref[...] = hn
    logits = jnp.dot(hn, w_ref[...], precision=lax.Precision.HIGHEST, preferred_element_type=F32) + b_ref[...]
    lane = lax.broadcasted_iota(jnp.int32, logits.shape, 1).astype(F32)
    neg = -jnp.inf

    def first_argmax(vals):
        m = jnp.max(vals, axis=-1, keepdims=True)
        return m, jnp.min(jnp.where(vals == m, lane, float(LANES)), axis=-1, keepdims=True)

    gl = jnp.where(lane < N_GROUPS, logits, neg)
    gmax, gsel = first_argmax(gl)
    pg_sel = 1.0 / jnp.sum(jnp.exp(gl - gmax), axis=-1, keepdims=True)
    e0 = N_GROUPS + gsel * EXP_PER_GROUP
    el = jnp.where((lane >= e0) & (lane < e0 + EXP_PER_GROUP), logits, neg)
    m1, i1 = first_argmax(el)
    m2, i2 = first_argmax(jnp.where(lane == i1, neg, el))
    e21 = jnp.exp(m2 - m1)
    c1 = pg_sel / (1.0 + e21)
    c2 = c1 * e21
    ids = jnp.where(lane == 0, i1 - N_GROUPS, jnp.where(lane == 1, i2 - N_GROUPS, 0.0))
    ids_ref[...] = ids.astype(jnp.int32)
    comb_ref[...] = jnp.where(lane == 0, c1, jnp.where(lane == 1, c2, 0.0))


def _router(h, g_ffn, w_gr, b_gr, w_er, b_er):
    m = h.shape[0]
    pad = LANES - N_GROUPS - N_EXPERTS
    w = jnp.concatenate([w_gr, w_er, jnp.zeros((D_MODEL, pad), F32)], axis=1)
    b = jnp.concatenate([b_gr, b_er, jnp.zeros((pad,), F32)]).reshape(1, LANES)
    row = lambda n: pl.BlockSpec((TM, n), lambda i: (i, 0))
    return pl.pallas_call(
        _router_body,
        grid=(m // TM,),
        in_specs=[row(D_MODEL), pl.BlockSpec((1, D_MODEL), lambda i: (0, 0)),
                  pl.BlockSpec((D_MODEL, LANES), lambda i: (0, 0)), pl.BlockSpec((1, LANES), lambda i: (0, 0))],
        out_specs=[row(D_MODEL), row(LANES), row(LANES)],
        out_shape=[jax.ShapeDtypeStruct((m, D_MODEL), F32), jax.ShapeDtypeStruct((m, LANES), jnp.int32),
                   jax.ShapeDtypeStruct((m, LANES), F32)],
        compiler_params=_params("parallel"),
    )(h, g_ffn.reshape(1, D_MODEL), w, b)


def _expert_body(blk_e_ref, idx_ref, idx_next_ref, x_hbm, wg_ref, wu_ref, wd_ref, out_hbm,
                 xbuf, ybuf, gsem, ssem):
    del blk_e_ref
    i = pl.program_id(0)
    n = pl.num_programs(0)
    slot = i % 2

    def start_gather(ref, sl):
        def body(r, c):
            pltpu.make_async_copy(x_hbm.at[pl.ds(ref[0, 0, r], 1), :], xbuf.at[sl, pl.ds(r, 1), :],
                                  gsem.at[sl]).start()
            return c
        lax.fori_loop(0, MOE_ROWS, body, 0, unroll=8)

    def wait_gather(sl):
        pltpu.make_async_copy(x_hbm.at[pl.ds(0, MOE_ROWS), :], xbuf.at[sl], gsem.at[sl]).wait()

    def wait_scatter(sl):
        pltpu.make_async_copy(ybuf.at[sl], out_hbm.at[pl.ds(0, MOE_ROWS), :], ssem.at[sl]).wait()

    @pl.when(i == 0)
    def _():
        start_gather(idx_ref, 0)

    @pl.when(i + 1 < n)
    def _():
        start_gather(idx_next_ref, 1 - slot)

    wait_gather(slot)

    @pl.when(i >= 2)
    def _():
        wait_scatter(slot)

    x = xbuf[slot].astype(BF16)
    hg = jnp.dot(x, wg_ref[0].astype(BF16), preferred_element_type=F32)
    hu = jnp.dot(x, wu_ref[0].astype(BF16), preferred_element_type=F32)
    act = (hg * jax.nn.sigmoid(hg) * hu).astype(BF16)
    ybuf[slot] = jnp.dot(act, wd_ref[0].astype(BF16), preferred_element_type=F32)

    def start_scatter(r, c):
        pltpu.make_async_copy(ybuf.at[slot, pl.ds(r, 1), :], out_hbm.at[pl.ds(idx_ref[0, 1, r], 1), :],
                              ssem.at[slot]).start()
        return c
    lax.fori_loop(0, MOE_ROWS, start_scatter, 0, unroll=8)

    @pl.when(i == n - 1)
    def _():
        wait_scatter(slot)

        @pl.when(n >= 2)
        def _():
            wait_scatter(1 - slot)


def _experts(hn, blk_e, idx, w_gate, w_up, w_down):
    n_blocks = idx.shape[0]
    p = n_blocks * MOE_ROWS
    idx_spec = lambda f: pl.BlockSpec((1, 2, MOE_ROWS), f, memory_space=pltpu.SMEM)
    grid_spec = pltpu.PrefetchScalarGridSpec(
        num_scalar_prefetch=1,
        grid=(n_blocks,),
        in_specs=[
            idx_spec(lambda i, be: (i, 0, 0)),
            idx_spec(lambda i, be: (jnp.minimum(i + 1, n_blocks - 1), 0, 0)),
            pl.BlockSpec(memory_space=pl.ANY),
            pl.BlockSpec((1, D_MODEL, D_EXPERT), lambda i, be: (be[i], 0, 0)),
            pl.BlockSpec((1, D_MODEL, D_EXPERT), lambda i, be: (be[i], 0, 0)),
            pl.BlockSpec((1, D_EXPERT, D_MODEL), lambda i, be: (be[i], 0, 0)),
        ],
        out_specs=pl.BlockSpec(memory_space=pl.ANY),
        scratch_shapes=[pltpu.VMEM((2, MOE_ROWS, D_MODEL), F32), pltpu.VMEM((2, MOE_ROWS, D_MODEL), F32),
                        pltpu.SemaphoreType.DMA((2,)), pltpu.SemaphoreType.DMA((2,))],
    )
    return pl.pallas_call(
        _expert_body,
        grid_spec=grid_spec,
        out_shape=jax.ShapeDtypeStruct((p, D_MODEL), F32),
        compiler_params=_params("arbitrary"),
    )(blk_e, idx, idx, hn, w_gate, w_up, w_down)


def _route_plan(ids):
    n_tok = ids.shape[0]
    n_pairs = ids.size
    n_blocks = -(-(n_pairs + N_EXPERTS * (MOE_ROWS - 1)) // MOE_ROWS)
    p = n_blocks * MOE_ROWS
    flat_e = ids.reshape(n_pairs)
    onehot = (flat_e[:, None] == jnp.arange(N_EXPERTS, dtype=jnp.int32)[None, :]).astype(jnp.int32)
    csum = jnp.cumsum(onehot, axis=0)
    rank = jnp.sum(onehot * csum, axis=1) - 1
    counts = csum[-1]
    pcounts = (counts + MOE_ROWS - 1) // MOE_ROWS * MOE_ROWS
    pends = jnp.cumsum(pcounts)
    pstarts = pends - pcounts
    dest = jnp.sum(onehot * pstarts[None, :], axis=1) + rank
    pair = jnp.arange(n_pairs, dtype=jnp.int32)
    row_pair = jnp.full((p,), -1, jnp.int32).at[dest].set(pair)
    is_pad = row_pair < 0
    pad_rank = jnp.cumsum(is_pad.astype(jnp.int32)) - 1
    row_tok = jnp.where(is_pad, 0, row_pair // TOP_K)
    row_dst = jnp.where(is_pad, n_pairs + pad_rank, (row_pair % TOP_K) * n_tok + row_tok)
    block_start = jnp.arange(n_blocks, dtype=jnp.int32) * MOE_ROWS
    blk_e = jnp.minimum(jnp.sum((block_start[:, None] >= pends[None, :]).astype(jnp.int32), axis=1),
                        N_EXPERTS - 1).astype(jnp.int32)
    idx = jnp.stack([row_tok.reshape(n_blocks, MOE_ROWS), row_dst.reshape(n_blocks, MOE_ROWS)], axis=1)
    return blk_e, idx.astype(jnp.int32)


def _final_body(h_ref, first_ref, second_ref, comb_ref, g_ref, o_ref):
    x = h_ref[...] + (first_ref[...] * comb_ref[:, 0:1] + second_ref[...] * comb_ref[:, 1:2])
    o_ref[...] = x * lax.rsqrt(jnp.mean(x * x, axis=-1, keepdims=True) + EPS) * g_ref[...]


def _final(h, pair_out, comb, g_final, *, row0, n_rows):
    n_tok = h.shape[0]
    assert row0 % TM == 0 and n_rows % TM == 0 and n_tok % TM == 0
    blk0, plane = row0 // TM, n_tok // TM
    row = lambda n, off: pl.BlockSpec((TM, n), lambda i: (off + i, 0))
    return pl.pallas_call(
        _final_body,
        grid=(n_rows // TM,),
        in_specs=[row(D_MODEL, blk0), row(D_MODEL, blk0), row(D_MODEL, plane + blk0), row(LANES, blk0),
                  pl.BlockSpec((1, D_MODEL), lambda i: (0, 0))],
        out_specs=pl.BlockSpec((TM, D_MODEL), lambda i: (i, 0)),
        out_shape=jax.ShapeDtypeStruct((n_rows, D_MODEL), F32),
        compiler_params=_params("parallel"),
    )(h, pair_out, pair_out, comb, g_final.reshape(1, D_MODEL))


def _reorder_last(x, shape, order):
    lead = x.shape[:-1]
    n = len(lead)
    y = x.reshape(lead + shape).transpose(tuple(range(n)) + tuple(n + o for o in order))
    return y.reshape(lead + (x.shape[-1],))


HALF_N = RWKV_N // 2


def _key_major(x):
    return _reorder_last(x, (RWKV_HEADS, RWKV_N), (1, 0))


def _key_major_inv(x):
    return _reorder_last(x, (RWKV_N, RWKV_HEADS), (1, 0))


def _value_major(x):
    return _reorder_last(x, (RWKV_HEADS, 2, HALF_N), (2, 1, 0))


def _value_major_inv(x):
    return _reorder_last(x, (HALF_N, 2, RWKV_HEADS), (2, 1, 0))


def _rwkv_cols(x, key_fn, value_fn):
    return jnp.concatenate([key_fn(x[..., :RWKV_W]), key_fn(x[..., RWKV_W:2 * RWKV_W]),
                            value_fn(x[..., 2 * RWKV_W:3 * RWKV_W]), x[..., 3 * RWKV_W:]], axis=-1)


def _sample_key_layout(x, b, t):
    return x.reshape(b, t, RWKV_N, RWKV_HEADS).transpose(3, 1, 2, 0)


def _sample_value_layout(x, b, t):
    return x.reshape(b, t, HALF_N, 2, RWKV_HEADS).transpose(4, 1, 3, 2, 0).reshape(RWKV_HEADS, t, RWKV_N, b)


def _sample_value_unlayout(y, b, t):
    return y.reshape(RWKV_HEADS, t, 2, HALF_N, b).transpose(4, 1, 3, 2, 0).reshape(b * t, RWKV_W)


def kernel(x_prompt, x_sample, mem_prompt, state_ret, state_rwkv, state_shift, cache_mem_k, cache_mem_v,
           g_mix, w_in, ret_gn, rwkv_mu, rwkv_w0, rwkv_w2, rwkv_a0, rwkv_a2, rwkv_g2, rwkv_k_k, rwkv_k_a,
           rwkv_r_k, rwkv_lnx_w, rwkv_lnx_b, w_out, g_mem_q, g_mem_kv, w_mq, w_mk, w_mv, w_mo, g_ffn,
           w_group_router, b_group_router, w_expert_router, b_expert_router, w_e_gate, w_e_up, w_e_down,
           g_final):
    assert w_in.shape[0] == 1, "single-layer decoder"
    bp, tp, d = x_prompt.shape
    bs, ts, _ = x_sample.shape
    np_tok, ns_tok = bp * tp, bs * ts
    assert d == D_MODEL and bp * RWKV_HEADS * 2 == LANES and bs == LANES
    l = 0
    x_parts = [x_prompt.reshape(np_tok, d), x_sample.reshape(ns_tok, d)]

    w_in_l = jnp.concatenate([w_in[l][:, :N_RET_COLS], _rwkv_cols(w_in[l][:, N_RET_COLS:], _key_major, _value_major)],
                             axis=1)
    (proj,) = _matmul(x_parts, [w_in_l.astype(BF16)], gain=g_mix[l])

    pos_p = np.arange(tp)
    pos_s = PAST_LEN + np.arange(ts)
    zero_ret = jnp.zeros((bp, RET_HEADS, RET_DK, RET_DV), F32)
    oret_p, sret_p = _retention(proj, zero_ret, ret_gn[l], pos_p, row0=0, n_batch=bp, t=tp, n_seq=1)
    oret_s, sret_s = _retention(proj, state_ret[l], ret_gn[l], pos_s, row0=np_tok, n_batch=bs, t=ts, n_seq=16)

    pre_w = (_rwkv_cols(rwkv_mu[l], _key_major, _value_major), _key_major(rwkv_w0[l]), _key_major(rwkv_w2[l]),
             _key_major(rwkv_a0[l]), _key_major(rwkv_a2[l]), _value_major(rwkv_g2[l]))
    zero_shift = jnp.zeros((bp, N_RWKV_COLS), F32)
    shift_in = _rwkv_cols(state_shift[l], _key_major, _value_major)
    r_p, k_p, v_p, w_p, a_p, gate_p = _rwkv_pre(proj, zero_shift, *pre_w, row0=0, n_batch=bp, t=tp,
                                                 n_seq=1, c=256, transposed=True)
    r_s, k_s, v_s, w_s, a_s, gate_s = _rwkv_pre(proj, shift_in, *pre_w, row0=np_tok, n_batch=bs, t=ts,
                                                 n_seq=16, c=ts, transposed=False)

    kvec = lambda v: v.reshape(RWKV_HEADS, RWKV_N)
    key_par = lambda v: jnp.broadcast_to(kvec(v).T[:, None, None, :], (RWKV_N, 2, bp, RWKV_HEADS)).reshape(
        RWKV_N, LANES)

    val_par = lambda v: jnp.broadcast_to(
        v.reshape(RWKV_HEADS, 2, HALF_N).transpose(2, 1, 0)[:, :, None, :],
        (HALF_N, 2, bp, RWKV_HEADS)).reshape(HALF_N, LANES)
    mem_shape = (N_MEM, MEM_HEADS, MEM_DH)
    y_p, srw_p, cache_k_heads, cache_v_heads = _rwkv_scan_prompt(
        r_p, k_p, w_p, a_p, v_p, key_par(rwkv_k_k[l]), key_par(rwkv_k_a[l]), key_par(rwkv_r_k[l]),
        val_par(rwkv_lnx_w[l]), val_par(rwkv_lnx_b[l]), jnp.zeros((HALF_N, RWKV_N, LANES), F32),
        cache_mem_k.reshape(bs, *mem_shape), cache_mem_v.reshape(bs, *mem_shape))
    srw_p = srw_p.reshape(HALF_N, RWKV_N, 2, bp, RWKV_HEADS).transpose(3, 4, 2, 0, 1).reshape(
        bp, RWKV_HEADS, RWKV_N, RWKV_N)
    head_par = lambda v: jnp.broadcast_to(kvec(v)[:, :, None], (RWKV_HEADS, RWKV_N, LANES))
    y_s, srw_s = _rwkv_scan(
        _sample_key_layout(r_s, bs, ts), _sample_key_layout(k_s, bs, ts), _sample_key_layout(w_s, bs, ts),
        _sample_key_layout(a_s, bs, ts), _sample_value_layout(v_s, bs, ts),
        head_par(rwkv_k_k[l]), head_par(rwkv_k_a[l]), head_par(rwkv_r_k[l]),
        head_par(rwkv_lnx_w[l]), head_par(rwkv_lnx_b[l]),
        state_rwkv[l].astype(F32).transpose(1, 2, 3, 0), tc=ts, halves=1)
    y_s = _sample_value_unlayout(y_s, bs, ts)
    srw_s = srw_s.transpose(3, 0, 1, 2)

    w_rwkv_out = _value_major(w_out[l][RET_W:].T).T
    h = _merge(x_parts, [oret_p, oret_s], [gate_p, gate_s], y_p, y_s, w_out[l][:RET_W], w_rwkv_out)

    mk, mv, mk_heads, mv_heads = _mem_kv(mem_prompt, g_mem_kv[l], w_mk[l], w_mv[l])
    (q,) = _matmul([h], [w_mq[l].astype(BF16)], gain=g_mem_q[l])
    att_p = _attention(q, mk_heads, mv_heads, row0=0, n_batch=bp, t=tp, n_seq=1, tq=TM)
    att_s = _attention(q, cache_k_heads, cache_v_heads, row0=np_tok, n_batch=bs, t=ts, n_seq=16, tq=ts)
    (h,) = _matmul([att_p, att_s], [w_mo[l].astype(BF16)], residual=h)

    hn, ids, comb = _router(h, g_ffn[l], w_group_router[l], b_group_router[l], w_expert_router[l],
                            b_expert_router[l])
    blk_e, idx = _route_plan(ids[:, :TOP_K])
    pair_out = _experts(hn, blk_e, idx, w_e_gate[l], w_e_up[l], w_e_down[l])
    y_prompt = _final(h, pair_out, comb, g_final, row0=0, n_rows=np_tok).reshape(bp, tp, d)
    y_sample = _final(h, pair_out, comb, g_final, row0=np_tok, n_rows=ns_tok).reshape(bs, ts, d)

    shift_p = lax.slice(proj, (tp - 1, N_RET_COLS), (np_tok, N_IN_COLS), (tp, 1))
    shift_s = lax.slice(proj, (np_tok + ts - 1, N_RET_COLS), (np_tok + ns_tok, N_IN_COLS), (ts, 1))
    shift_p = _rwkv_cols(shift_p, _key_major_inv, _value_major_inv)
    shift_s = _rwkv_cols(shift_s, _key_major_inv, _value_major_inv)
    return (y_prompt, y_sample, sret_p[None], srw_p[None], shift_p[None],
            mk.reshape(1, bp, *mem_shape), mv.reshape(1, bp, *mem_shape),
            sret_s[None], srw_s[None], shift_s[None])
```

```python
import functools

import numpy as np
import jax
import jax.numpy as jnp
from jax import lax
from jax.experimental import pallas as pl
from jax.experimental.pallas import tpu as pltpu

F32 = jnp.float32
BF16 = jnp.bfloat16

D_MODEL = 1024
PAST_LEN = 16384
N_MEM = 256
MEM_HEADS = 4
MEM_DH = D_MODEL // MEM_HEADS
RET_HEADS = 4
RET_W = D_MODEL // 2
RET_DV = RET_W // RET_HEADS
RET_DK = RET_DV // 2
RET_QK = RET_HEADS * RET_DK
RET_CHUNK = 128
ROPE_BASE = 10000.0
RWKV_N = 64
RWKV_W = D_MODEL - RET_W
RWKV_HEADS = RWKV_W // RWKV_N
LORA_W = 64
LORA_A = 64
LORA_G = 128
LNX_EPS = 64e-5
N_RET_COLS = 2 * RET_QK + 2 * RET_W
N_RWKV_COLS = 3 * RWKV_W + LORA_W + LORA_A + LORA_G
N_IN_COLS = N_RET_COLS + N_RWKV_COLS
N_GROUPS = 4
EXP_PER_GROUP = 8
N_EXPERTS = N_GROUPS * EXP_PER_GROUP
TOP_K = 2
D_EXPERT = D_MODEL // 2
EPS = 1e-6

LANES = 128
MOE_ROWS = 128
TM = 512


def _params(*sem):
    return pltpu.CompilerParams(dimension_semantics=sem)


def _row_part_specs(parts, tm):
    specs, counts, start = [], [], 0
    for part in parts:
        nb = part.shape[0] // tm
        assert nb * tm == part.shape[0]
        specs.append(pl.BlockSpec((tm, part.shape[1]), lambda i, s=start, n=nb: (jnp.clip(i - s, 0, n - 1), 0)))
        counts.append(nb)
        start += nb
    return specs, counts


def _read_row_parts(refs, counts):
    i = pl.program_id(0)
    x = refs[0][...]
    start = counts[0]
    for ref, nb in zip(refs[1:], counts[1:]):
        x = jnp.where(i >= start, ref[...], x)
        start += nb
    return x


def _mm_body(*refs, part_counts, n_w, has_gain, has_res, n_chunk):
    it = iter(refs)
    x_refs = [next(it) for _ in part_counts]
    g_ref = next(it) if has_gain else None
    w_refs = [next(it) for _ in range(n_w)]
    r_ref = next(it) if has_res else None
    o_refs = [next(it) for _ in range(n_w)]
    x = _read_row_parts(x_refs, part_counts).astype(F32)
    if has_gain:
        x = x * lax.rsqrt(jnp.mean(x * x, axis=-1, keepdims=True) + EPS) * g_ref[...]
    xb = x.astype(BF16)
    for w_ref, o_ref in zip(w_refs, o_refs):
        for j in range(0, w_ref.shape[1], n_chunk):
            acc = jnp.dot(xb, w_ref[:, j:j + n_chunk], preferred_element_type=F32)
            if has_res:
                acc = acc + r_ref[:, j:j + n_chunk]
            o_ref[:, j:j + n_chunk] = acc


def _matmul(x_parts, ws, *, gain=None, residual=None, tm=TM):
    k = x_parts[0].shape[1]
    m = sum(part.shape[0] for part in x_parts)
    n_chunk = 256
    assert all(w.shape[1] % n_chunk == 0 for w in ws)
    assert residual is None or len(ws) == 1
    in_specs, part_counts = _row_part_specs(x_parts, tm)
    args = list(x_parts)
    if gain is not None:
        in_specs.append(pl.BlockSpec((1, k), lambda i: (0, 0)))
        args.append(gain.reshape(1, k).astype(F32))
    for w in ws:
        in_specs.append(pl.BlockSpec(w.shape, lambda i: (0, 0)))
        args.append(w)
    if residual is not None:
        in_specs.append(pl.BlockSpec((tm, ws[0].shape[1]), lambda i: (i, 0)))
        args.append(residual)
    return pl.pallas_call(
        functools.partial(_mm_body, part_counts=part_counts, n_w=len(ws), has_gain=gain is not None,
                          has_res=residual is not None, n_chunk=n_chunk),
        grid=(m // tm,),
        in_specs=in_specs,
        out_specs=[pl.BlockSpec((tm, w.shape[1]), lambda i: (i, 0)) for w in ws],
        out_shape=[jax.ShapeDtypeStruct((m, w.shape[1]), F32) for w in ws],
        compiler_params=_params("parallel"),
    )(*args)


def _rot_tables(pos):
    half = RET_DK // 2
    inv_freq = ROPE_BASE ** (-(np.arange(half, dtype=np.float64) / half))
    ang = pos.astype(np.float64)[:, None] * inv_freq[None, :]
    cos, sin = np.cos(ang), np.sin(ang)
    zero = np.zeros_like(sin)
    c = np.tile(np.concatenate([cos, cos], axis=1), (1, RET_HEADS))
    s_lo = np.tile(np.concatenate([-sin, zero], axis=1), (1, RET_HEADS))
    s_hi = np.tile(np.concatenate([zero, sin], axis=1), (1, RET_HEADS))
    return [jnp.asarray(t, F32) for t in (c, s_lo, s_hi)]


def _ret_decay_tables(c):
    lg = np.log1p(-np.exp2(-5.0 - np.arange(RET_HEADS, dtype=np.float64)))
    idx = np.arange(c, dtype=np.float64)
    diff = idx[:, None] - idx[None, :]
    mask = np.where(diff[None] >= 0, np.exp(np.maximum(diff, 0.0)[None] * lg[:, None, None]), 0.0)
    q_dec = np.repeat(np.exp((idx[:, None] + 1.0) * lg[None, :]), RET_DV, axis=1)
    k_dec = np.repeat(np.exp((c - 1.0 - idx)[:, None] * lg[None, :]), RET_DK, axis=1)
    c_dec = [float(v) for v in np.exp(c * lg)]
    return jnp.asarray(mask, F32), jnp.asarray(q_dec, F32), jnp.asarray(k_dec, F32), c_dec


def _ret_body(q_ref, k_ref, v_ref, gate_ref, c_ref, slo_ref, shi_ref, mask_ref, qdec_ref, kdec_ref,
              gn_ref, s0_ref, o_ref, sout_ref, s_scr, *, n_seq, c, c_dec):
    ci = pl.program_id(1)

    @pl.when(ci == 0)
    def _():
        s_scr[...] = s0_ref[...].astype(F32)

    cos, s_lo, s_hi = c_ref[...], slo_ref[...], shi_ref[...]
    half = RET_DK // 2

    def rope(x):
        return x * cos + pltpu.roll(x, RET_QK - half, 1) * s_lo + pltpu.roll(x, half, 1) * s_hi

    nt = (((1,), (1,)), ((), ()))
    tn = (((0,), (0,)), ((), ()))
    for g in range(n_seq):
        rows = slice(g * c, (g + 1) * c)
        q = rope(q_ref[rows, :].astype(F32))
        k = rope(k_ref[rows, :].astype(F32)) * (RET_DK ** -0.5)
        k_st = k * kdec_ref[...]
        for h in range(RET_HEADS):
            kc = slice(h * RET_DK, (h + 1) * RET_DK)
            vc = slice(h * RET_DV, (h + 1) * RET_DV)
            qh = q[:, kc].astype(BF16)
            vh = v_ref[rows, vc].astype(BF16)
            s_h = s_scr[g, h]
            att = lax.dot_general(qh, k[:, kc].astype(BF16), nt, preferred_element_type=F32) * mask_ref[h]
            o = jnp.dot(att.astype(BF16), vh, preferred_element_type=F32)
            o = o + jnp.dot(qh, s_h.astype(BF16), preferred_element_type=F32) * qdec_ref[:, vc]
            s_scr[g, h] = s_h * c_dec[h] + lax.dot_general(
                k_st[:, kc].astype(BF16), vh, tn, preferred_element_type=F32)
            o = o * lax.rsqrt(jnp.mean(o * o, axis=-1, keepdims=True) + EPS)
            gate = gate_ref[rows, vc].astype(F32)
            o_ref[rows, vc] = o * gn_ref[:, vc] * (gate * jax.nn.sigmoid(gate))

    @pl.when(ci == pl.num_programs(1) - 1)
    def _():
        sout_ref[...] = s_scr[...]


def _retention(proj, s0, ret_gn, pos, *, row0, n_batch, t, n_seq):
    c = RET_CHUNK if t % RET_CHUNK == 0 else t
    n_chunks = t // c
    rows = n_seq * c
    assert n_batch % n_seq == 0 and row0 % rows == 0 and (n_seq == 1 or n_chunks == 1)
    blk0 = row0 // rows
    mask, q_dec, k_dec, c_dec = _ret_decay_tables(c)
    cos, s_lo, s_hi = _rot_tables(pos)

    def row_map(col):
        return lambda b, ci: (blk0 + b * n_chunks + ci, col)

    def const2(b, ci):
        return (0, 0)

    state_spec = pl.BlockSpec((n_seq, RET_HEADS, RET_DK, RET_DV), lambda b, ci: (b, 0, 0, 0))
    in_specs = [
        pl.BlockSpec((rows, RET_QK), row_map(0)),
        pl.BlockSpec((rows, RET_QK), row_map(1)),
        pl.BlockSpec((rows, RET_W), row_map(1)),
        pl.BlockSpec((rows, RET_W), row_map(2)),
        pl.BlockSpec((c, RET_QK), lambda b, ci: (ci, 0)),
        pl.BlockSpec((c, RET_QK), lambda b, ci: (ci, 0)),
        pl.BlockSpec((c, RET_QK), lambda b, ci: (ci, 0)),
        pl.BlockSpec((RET_HEADS, c, c), lambda b, ci: (0, 0, 0)),
        pl.BlockSpec((c, RET_W), const2),
        pl.BlockSpec((c, RET_QK), const2),
        pl.BlockSpec((1, RET_W), const2),
        state_spec,
    ]
    return pl.pallas_call(
        functools.partial(_ret_body, n_seq=n_seq, c=c, c_dec=c_dec),
        grid=(n_batch // n_seq, n_chunks),
        in_specs=in_specs,
        out_specs=[pl.BlockSpec((rows, RET_W), lambda b, ci: (b * n_chunks + ci, 0)), state_spec],
        out_shape=[jax.ShapeDtypeStruct((n_batch * t, RET_W), F32),
                   jax.ShapeDtypeStruct((n_batch, RET_HEADS, RET_DK, RET_DV), F32)],
        scratch_shapes=[pltpu.VMEM((n_seq, RET_HEADS, RET_DK, RET_DV), F32)],
        compiler_params=_params("parallel", "arbitrary"),
    )(proj, proj, proj, proj, cos, s_lo, s_hi, mask, q_dec, k_dec, ret_gn.reshape(1, RET_W).astype(F32), s0)


def _rwkv_pre_body(r_ref, k_ref, v_ref, lo_ref, shift_ref, mu_ref, w0_ref, w2_ref, a0_ref, a2_ref, g2_ref,
                   ro_ref, ko_ref, vo_ref, wo_ref, ao_ref, go_ref, prev_scr, *, n_seq, c, transposed):
    ci = pl.program_id(1)

    @pl.when(ci == 0)
    def _():
        for g in range(n_seq):
            prev_scr[g] = shift_ref[g].astype(F32)

    first_row = lax.broadcasted_iota(jnp.int32, (c, 1), 0) == 0

    def shifted(x_ref, g, col0):
        w = x_ref.shape[1]
        x = x_ref[g * c:(g + 1) * c, :].astype(F32)
        prev_row = prev_scr[g, :, col0:col0 + w]
        prev = jnp.where(first_row, prev_row, pltpu.roll(x, 1, 0))
        prev_scr[g, :, col0:col0 + w] = x[c - 1:c, :]
        return x + (prev - x) * mu_ref[:, col0:col0 + w]

    def put(o_ref, g, val):
        if transposed:
            o_ref[...] = val.T
        else:
            o_ref[g * c:(g + 1) * c, :] = val

    for g in range(n_seq):
        put(ro_ref, g, shifted(r_ref, g, 0))
        put(ko_ref, g, shifted(k_ref, g, RWKV_W))
        put(vo_ref, g, shifted(v_ref, g, 2 * RWKV_W))
        lo = shifted(lo_ref, g, 3 * RWKV_W)
        hw = lo[:, :LORA_W]
        ha = lo[:, LORA_W:LORA_W + LORA_A]
        hg = lo[:, LORA_W + LORA_A:]
        u = w0_ref[...] + jnp.dot(jnp.tanh(hw).astype(BF16), w2_ref[...], preferred_element_type=F32)
        put(wo_ref, g, jnp.exp(-float(np.exp(-0.5)) * jax.nn.sigmoid(u)))
        put(ao_ref, g, jax.nn.sigmoid(
            a0_ref[...] + jnp.dot(ha.astype(BF16), a2_ref[...], preferred_element_type=F32)))
        go_ref[g * c:(g + 1) * c, :] = jnp.dot(jax.nn.sigmoid(hg).astype(BF16), g2_ref[...],
                                                preferred_element_type=F32)


def _rwkv_pre(proj, s_shift, mu, w0, w2, a0, a2, g2, *, row0, n_batch, t, n_seq, c, transposed):
    n_chunks = t // c
    rows = n_seq * c
    assert t % c == 0 and n_batch % n_seq == 0 and row0 % rows == 0 and (n_seq == 1 or n_chunks == 1)
    assert not transposed or n_seq == 1
    blk0 = row0 // rows
    col_r = N_RET_COLS // RWKV_W
    lo_w = LORA_W + LORA_A + LORA_G
    col_lo = (N_RET_COLS + 3 * RWKV_W) // lo_w
    assert col_r * RWKV_W == N_RET_COLS and col_lo * lo_w == N_RET_COLS + 3 * RWKV_W

    def row_map(col):
        return lambda b, ci: (blk0 + b * n_chunks + ci, col)

    def const2(b, ci):
        return (0, 0)

    in_specs = [
        pl.BlockSpec((rows, RWKV_W), row_map(col_r)),
        pl.BlockSpec((rows, RWKV_W), row_map(col_r + 1)),
        pl.BlockSpec((rows, RWKV_W), row_map(col_r + 2)),
        pl.BlockSpec((rows, lo_w), row_map(col_lo)),
        pl.BlockSpec((n_seq, 1, N_RWKV_COLS), lambda b, ci: (b, 0, 0)),
        pl.BlockSpec((1, N_RWKV_COLS), const2),
        pl.BlockSpec((1, RWKV_W), const2),
        pl.BlockSpec((LORA_W, RWKV_W), const2),
        pl.BlockSpec((1, RWKV_W), const2),
        pl.BlockSpec((LORA_A, RWKV_W), const2),
        pl.BlockSpec((LORA_G, RWKV_W), const2),
    ]
    nat_spec = pl.BlockSpec((rows, RWKV_W), lambda b, ci: (b * n_chunks + ci, 0))
    nat_shape = jax.ShapeDtypeStruct((n_batch * t, RWKV_W), F32)
    if transposed:
        vec_spec = pl.BlockSpec((None, RWKV_W, c), lambda b, ci: (b, 0, ci))
        vec_shape = jax.ShapeDtypeStruct((n_batch, RWKV_W, t), F32)
    else:
        vec_spec, vec_shape = nat_spec, nat_shape
    return pl.pallas_call(
        functools.partial(_rwkv_pre_body, n_seq=n_seq, c=c, transposed=transposed),
        grid=(n_batch // n_seq, n_chunks),
        in_specs=in_specs,
        out_specs=[vec_spec] * 5 + [nat_spec],
        out_shape=[vec_shape] * 5 + [nat_shape],
        scratch_shapes=[pltpu.VMEM((n_seq, 1, N_RWKV_COLS), F32)],
        compiler_params=_params("parallel", "arbitrary"),
    )(proj, proj, proj, proj, s_shift.reshape(n_batch, 1, N_RWKV_COLS),
      mu.reshape(1, -1), w0.reshape(1, -1), w2.astype(BF16), a0.reshape(1, -1),
      a2.astype(BF16), g2.astype(BF16))


def _scan_body(r_ref, k_ref, w_ref, a_ref, v_ref, kk_ref, ka_ref, rk_ref, lw_ref, lb_ref, s0_ref,
               y_ref, sout_ref, s_scr, a_scr, b_scr, km_scr, *, tc, vr, halves):
    ci = pl.program_id(1)

    @pl.when(ci == 0)
    def _():
        s_scr[...] = s0_ref[...].astype(F32)

    def ksum(x):
        return jnp.sum(x, axis=-2, keepdims=True)

    def vsum(x):
        if halves == 2:
            x2 = x.reshape(tc * vr, LANES)
            x = (x2 + pltpu.roll(x2, LANES // 2, 1)).reshape(tc, vr, LANES)
        return jnp.sum(x, axis=1, keepdims=True)

    kr = k_ref[...]
    a = a_ref[...]
    kk = kr * kk_ref[...]
    kk = kk / jnp.maximum(jnp.sqrt(ksum(kk * kk)), 1e-12)
    a_scr[...] = -kk
    b_scr[...] = kk * a
    km_scr[...] = kr * (1.0 + (a - 1.0) * ka_ref[...])

    def token(t, carry):
        r, w, avec, bvec, kmod = r_ref[t], w_ref[t], a_scr[t], b_scr[t], km_scr[t]

        def value_row(i, c2):
            s = s_scr[i]
            sa = ksum(s * avec)
            s = s * w + sa * bvec + v_ref[t, pl.ds(i, 1), :] * kmod
            s_scr[i] = s
            y_ref[t, pl.ds(i, 1), :] = ksum(s * r)
            return c2

        lax.fori_loop(0, vr, value_row, 0, unroll=16)
        return carry

    lax.fori_loop(0, tc, token, 0)

    y = y_ref[...]
    d = y - vsum(y) * (1.0 / RWKV_N)
    var = vsum(d * d) * (1.0 / RWKV_N)
    bonus = ksum(r_ref[...] * km_scr[...] * rk_ref[...])
    y_ref[...] = d * lax.rsqrt(var + LNX_EPS) * lw_ref[...] + lb_ref[...] + bonus * v_ref[...]

    @pl.when(ci == pl.num_programs(1) - 1)
    def _():
        sout_ref[...] = s_scr[...]


def _rwkv_scan(r, k, w, a, v, k_k, k_a, r_k, lnx_w, lnx_b, s0, *, tc, halves):
    n_grp, t, _, lanes = r.shape
    vr = v.shape[2]
    assert lanes == LANES and t % tc == 0 and vr * halves == RWKV_N

    def tok_spec(rows):
        return pl.BlockSpec((None, tc, rows, LANES), lambda g, ci: (g, ci, 0, 0))

    def par_spec(rows):
        return pl.BlockSpec((None, rows, LANES), lambda g, ci: (g, 0, 0))

    st_spec = pl.BlockSpec((None, vr, RWKV_N, LANES), lambda g, ci: (g, 0, 0, 0))
    key_scratch = pltpu.VMEM((tc, RWKV_N, LANES), F32)
    return pl.pallas_call(
        functools.partial(_scan_body, tc=tc, vr=vr, halves=halves),
        grid=(n_grp, t // tc),
        in_specs=[tok_spec(RWKV_N)] * 4 + [tok_spec(vr)] + [par_spec(RWKV_N)] * 3 + [par_spec(vr)] * 2 + [st_spec],
        out_specs=[tok_spec(vr), st_spec],
        out_shape=[jax.ShapeDtypeStruct((n_grp, t, vr, LANES), F32),
                   jax.ShapeDtypeStruct((n_grp, vr, RWKV_N, LANES), F32)],
        scratch_shapes=[pltpu.VMEM((vr, RWKV_N, LANES), F32), key_scratch, key_scratch, key_scratch],
        compiler_params=_params("parallel", "arbitrary"),
    )(r, k, w, a, v, k_k, k_a, r_k, lnx_w, lnx_b, s0)


SCAN_TC = 128
SCAN_SUB = 64
MEM_CHUNKS = 4


def _scan_prompt_body(r_ref, k_ref, w_ref, a_ref, v_ref, kk_ref, ka_ref, rk_ref, lw_ref, lb_ref, s0_ref,
                      memk_hbm, memv_hbm, y_ref, sout_ref, memk_out, memv_out,
                      s_scr, r_c, w_c, a_c, b_c, km_c, v_c, y_c, mem_stage, mem_in_sem, mem_out_sem,
                      *, n_b, mem_seqs):
    ci = pl.program_id(0)
    vr = RWKV_N // 2
    ts = SCAN_SUB
    tile = RWKV_HEADS
    half_lanes = LANES // 2

    @pl.when(ci == 0)
    def _():
        s_scr[...] = s0_ref[...].astype(F32)

    chunk_seqs = mem_seqs // MEM_CHUNKS

    def mem_in(chunk, slot):
        copies = []
        for j in range(chunk_seqs):
            seq = (ci * MEM_CHUNKS + chunk) * chunk_seqs + j
            for h in range(MEM_HEADS):
                copies.append(pltpu.make_async_copy(memk_hbm.at[seq, :, h, :], mem_stage.at[slot, 0, j, h],
                                                    mem_in_sem.at[slot]))
                copies.append(pltpu.make_async_copy(memv_hbm.at[seq, :, h, :], mem_stage.at[slot, 1, j, h],
                                                    mem_in_sem.at[slot]))
        return copies

    def mem_out(chunk, slot):
        seqs = pl.ds((ci * MEM_CHUNKS + chunk) * chunk_seqs, chunk_seqs)
        return [pltpu.make_async_copy(mem_stage.at[slot, 0], memk_out.at[seqs], mem_out_sem.at[slot]),
                pltpu.make_async_copy(mem_stage.at[slot, 1], memv_out.at[seqs], mem_out_sem.at[slot])]

    def mem_phase(p):
        if 1 <= p <= MEM_CHUNKS:
            for cp in mem_in(p - 1, (p - 1) % 2):
                cp.wait()
            for cp in mem_out(p - 1, (p - 1) % 2):
                cp.start()
        if 2 <= p <= MEM_CHUNKS + 1:
            for cp in mem_out(p - 2, p % 2):
                cp.wait()
        if p < MEM_CHUNKS:
            for cp in mem_in(p, p % 2):
                cp.start()

    mem_phase(0)

    low = lax.broadcasted_iota(jnp.int32, (ts, LANES), 1) < half_lanes

    def feature_pair_rows(x_ref, base):
        tiles = [x_ref[b, pl.ds(base + f * tile, tile), :] for f in range(2) for b in range(n_b)]
        return jnp.concatenate(tiles, axis=0).T

    def key_to_chain(x_ref, dst, t0):
        def group(g, c):
            rows = []
            for j in range(4):
                mt = feature_pair_rows(x_ref, pl.multiple_of((g * 4 + j) * 2 * tile, 2 * tile))[t0:t0 + ts]
                sw = pltpu.roll(mt, half_lanes, 1)
                rows += [jnp.where(low, mt, sw), jnp.where(low, sw, mt)]
            dst[:, pl.ds(pl.multiple_of(g * 8, 8), 8), :] = jnp.swapaxes(jnp.stack(rows, axis=0), 0, 1)
            return c
        lax.fori_loop(0, RWKV_N // 8, group, 0)

    def value_to_chain(g, c):
        rows = [feature_pair_rows(v_ref, pl.multiple_of((g * 8 + j) * 2 * tile, 2 * tile)) for j in range(8)]
        v_c[:, pl.ds(pl.multiple_of(g * 8, 8), 8), :] = jnp.swapaxes(jnp.stack(rows, axis=0), 0, 1)
        return c
    lax.fori_loop(0, vr // 8, value_to_chain, 0)

    def ksum(x):
        return jnp.sum(x, axis=-2, keepdims=True)

    for t0 in range(0, SCAN_TC, ts):
        key_to_chain(r_ref, r_c, t0)
        key_to_chain(w_ref, w_c, t0)
        key_to_chain(k_ref, km_c, t0)
        key_to_chain(a_ref, b_c, t0)

        def prep(g8, c):
            toks = pl.ds(pl.multiple_of(g8 * 8, 8), 8)
            kr = km_c[toks]
            a = b_c[toks]
            kk = kr * kk_ref[...]
            kk = kk / jnp.maximum(jnp.sqrt(ksum(kk * kk)), 1e-12)
            a_c[toks] = -kk
            b_c[toks] = kk * a
            km_c[toks] = kr * (1.0 + (a - 1.0) * ka_ref[...])
            return c
        lax.fori_loop(0, ts // 8, prep, 0)
        mem_phase(1 + 2 * (t0 // ts))

        def token(t, carry):
            r, w, avec, bvec, kmod = r_c[t], w_c[t], a_c[t], b_c[t], km_c[t]

            def value_row(i, c2):
                s = s_scr[i]
                sa = ksum(s * avec)
                s = s * w + sa * bvec + v_c[t0 + t, pl.ds(i, 1), :] * kmod
                s_scr[i] = s
                y_c[t0 + t, pl.ds(i, 1), :] = ksum(s * r)
                return c2

            lax.fori_loop(0, vr, value_row, 0, unroll=True)
            return carry

        lax.fori_loop(0, ts, token, 0)
        mem_phase(2 + 2 * (t0 // ts))

        def post(g8, c):
            ktoks = pl.ds(pl.multiple_of(g8 * 8, 8), 8)
            vtoks = pl.ds(pl.multiple_of(t0 + g8 * 8, 8), 8)

            def vsum(x):
                x2 = x.reshape(8 * vr, LANES)
                x2 = x2 + pltpu.roll(x2, half_lanes, 1)
                return jnp.sum(x2.reshape(8, vr, LANES), axis=1, keepdims=True)

            y = y_c[vtoks]
            d = y - vsum(y) * (1.0 / RWKV_N)
            var = vsum(d * d) * (1.0 / RWKV_N)
            bonus = ksum(r_c[ktoks] * km_c[ktoks] * rk_ref[...])
            y_c[vtoks] = d * lax.rsqrt(var + LNX_EPS) * lw_ref[...] + lb_ref[...] + bonus * v_c[vtoks]
            return c
        lax.fori_loop(0, ts // 8, post, 0)

    def value_from_chain(g, c):
        blk = jnp.swapaxes(y_c[:, pl.ds(pl.multiple_of(g * 8, 8), 8), :], 0, 1)
        for j in range(8):
            mt = blk[j].T
            base = pl.multiple_of((g * 8 + j) * 2 * tile, 2 * tile)
            for hf in range(2):
                for b in range(n_b):
                    row0 = (hf * n_b + b) * tile
                    y_ref[b, pl.ds(base + hf * tile, tile), :] = mt[row0:row0 + tile, :]
        return c
    lax.fori_loop(0, vr // 8, value_from_chain, 0)

    assert 2 * (SCAN_TC // ts) == MEM_CHUNKS
    mem_phase(MEM_CHUNKS + 1)

    @pl.when(ci == pl.num_programs(0) - 1)
    def _():
        sout_ref[...] = s_scr[...]


def _rwkv_scan_prompt(r, k, w, a, v, k_k, k_a, r_k, lnx_w, lnx_b, s0, mem_k, mem_v):
    n_b, _, t = r.shape
    vr = RWKV_N // 2
    n_steps = t // SCAN_TC
    n_mem_seq = mem_k.shape[0]
    assert t % SCAN_TC == 0 and 2 * n_b * RWKV_HEADS == LANES and n_mem_seq % (n_steps * MEM_CHUNKS) == 0
    chunk_seqs = n_mem_seq // (n_steps * MEM_CHUNKS)
    any_spec = pl.BlockSpec(memory_space=pl.ANY)
    mem_shape = jax.ShapeDtypeStruct((n_mem_seq, MEM_HEADS, N_MEM, MEM_DH), F32)
    tok_spec = pl.BlockSpec((n_b, RWKV_W, SCAN_TC), lambda ci: (0, 0, ci))
    key_par = pl.BlockSpec((RWKV_N, LANES), lambda ci: (0, 0))
    val_par = pl.BlockSpec((vr, LANES), lambda ci: (0, 0))
    st_spec = pl.BlockSpec((vr, RWKV_N, LANES), lambda ci: (0, 0, 0))
    key_chain = pltpu.VMEM((SCAN_SUB, RWKV_N, LANES), F32)
    val_chain = pltpu.VMEM((SCAN_TC, vr, LANES), F32)
    return pl.pallas_call(
        functools.partial(_scan_prompt_body, n_b=n_b, mem_seqs=n_mem_seq // n_steps),
        grid=(n_steps,),
        in_specs=[tok_spec] * 5 + [key_par] * 3 + [val_par] * 2 + [st_spec, any_spec, any_spec],
        out_specs=[tok_spec, st_spec, any_spec, any_spec],
        out_shape=[jax.ShapeDtypeStruct((n_b, RWKV_W, t), F32),
                   jax.ShapeDtypeStruct((vr, RWKV_N, LANES), F32), mem_shape, mem_shape],
        scratch_shapes=([pltpu.VMEM((vr, RWKV_N, LANES), F32)] + [key_chain] * 5 + [val_chain] * 2
                        + [pltpu.VMEM((2, 2, chunk_seqs, MEM_HEADS, N_MEM, MEM_DH), F32),
                           pltpu.SemaphoreType.DMA((2,)), pltpu.SemaphoreType.DMA((2,))]),
        compiler_params=_params("arbitrary"),
    )(r, k, w, a, v, k_k, k_a, r_k, lnx_w, lnx_b, s0, mem_k, mem_v)


def _merge_body(*refs, part_counts):
    n = len(part_counts)
    x_refs, oret_refs, g_refs = refs[:n], refs[n:2 * n], refs[2 * n:3 * n]
    yt_ref, ys_ref, wt_ref, wb_ref, o_ref = refs[3 * n:]
    x = _read_row_parts(x_refs, part_counts)
    y = jnp.where(pl.program_id(0) >= part_counts[0], ys_ref[...], yt_ref[...].T)
    yb = (y * _read_row_parts(g_refs, part_counts)).astype(BF16)
    ob = _read_row_parts(oret_refs, part_counts).astype(BF16)
    n_chunk = 256
    for j in range(0, D_MODEL, n_chunk):
        acc = jnp.dot(ob, wt_ref[:, j:j + n_chunk], preferred_element_type=F32)
        acc = acc + jnp.dot(yb, wb_ref[:, j:j + n_chunk], preferred_element_type=F32)
        o_ref[:, j:j + n_chunk] = x[:, j:j + n_chunk] + acc


def _merge(x_parts, oret_parts, g_parts, y_first_t, y_second, w_ret, w_rwkv):
    m = sum(part.shape[0] for part in x_parts)
    in_specs, part_counts = [], None
    for parts in (x_parts, oret_parts, g_parts):
        specs, part_counts = _row_part_specs(parts, TM)
        in_specs += specs
    assert len(part_counts) == 2
    n_first = part_counts[0]
    tiles = y_first_t.shape[2] // TM
    assert y_first_t.shape[0] * tiles == n_first
    yt_spec = pl.BlockSpec((None, RWKV_W, TM),
                           lambda i: (jnp.minimum(i, n_first - 1) // tiles, 0, jnp.minimum(i, n_first - 1) % tiles))
    ys_spec = pl.BlockSpec((TM, RWKV_W), lambda i: (jnp.clip(i - n_first, 0, part_counts[1] - 1), 0))
    wspec = pl.BlockSpec((RET_W, D_MODEL), lambda i: (0, 0))
    return pl.pallas_call(
        functools.partial(_merge_body, part_counts=part_counts),
        grid=(m // TM,),
        in_specs=in_specs + [yt_spec, ys_spec, wspec, wspec],
        out_specs=pl.BlockSpec((TM, D_MODEL), lambda i: (i, 0)),
        out_shape=jax.ShapeDtypeStruct((m, D_MODEL), F32),
        compiler_params=_params("parallel"),
    )(*x_parts, *oret_parts, *g_parts, y_first_t, y_second, w_ret.astype(BF16), w_rwkv.astype(BF16))


def _mem_kv_body(x_ref, g_ref, wk_ref, wv_ref, k_ref, v_ref, kh_ref, vh_ref, *, n_seq):
    x = x_ref[...].astype(F32)
    xb = (x * lax.rsqrt(jnp.mean(x * x, axis=-1, keepdims=True) + EPS) * g_ref[...]).astype(BF16)
    for w_ref, o_ref, oh_ref in ((wk_ref, k_ref, kh_ref), (wv_ref, v_ref, vh_ref)):
        for h in range(MEM_HEADS):
            acc = jnp.dot(xb, w_ref[:, h * MEM_DH:(h + 1) * MEM_DH], preferred_element_type=F32)
            o_ref[:, h, :] = acc
            for s in range(n_seq):
                oh_ref[s, h] = acc[s * N_MEM:(s + 1) * N_MEM]


def _mem_kv(mem, gain, w_k, w_v):
    n_b = mem.shape[0]
    n_seq = TM // N_MEM
    assert n_seq * N_MEM == TM and n_b % n_seq == 0
    wspec = pl.BlockSpec((D_MODEL, D_MODEL), lambda i: (0, 0))
    tok_spec = pl.BlockSpec((TM, MEM_HEADS, MEM_DH), lambda i: (i, 0, 0))
    head_spec = pl.BlockSpec((n_seq, MEM_HEADS, N_MEM, MEM_DH), lambda i: (i, 0, 0, 0))
    tok_shape = jax.ShapeDtypeStruct((n_b * N_MEM, MEM_HEADS, MEM_DH), F32)
    head_shape = jax.ShapeDtypeStruct((n_b, MEM_HEADS, N_MEM, MEM_DH), F32)
    return pl.pallas_call(
        functools.partial(_mem_kv_body, n_seq=n_seq),
        grid=(n_b // n_seq,),
        in_specs=[pl.BlockSpec((TM, D_MODEL), lambda i: (i, 0)), pl.BlockSpec((1, D_MODEL), lambda i: (0, 0)),
                  wspec, wspec],
        out_specs=[tok_spec, tok_spec, head_spec, head_spec],
        out_shape=[tok_shape, tok_shape, head_shape, head_shape],
        compiler_params=_params("parallel"),
    )(mem.reshape(n_b * N_MEM, D_MODEL), gain.reshape(1, D_MODEL), w_k.astype(BF16), w_v.astype(BF16))


def _attn_body(q_ref, k_ref, v_ref, o_ref, *, n_seq, tq):
    nt = (((1,), (1,)), ((), ()))
    for g in range(n_seq):
        rows = slice(g * tq, (g + 1) * tq)
        q = q_ref[rows, :].astype(BF16)
        s = lax.dot_general(q, k_ref[g].astype(BF16), nt, preferred_element_type=F32) * (MEM_DH ** -0.5)
        p = jnp.exp(s - jnp.max(s, axis=-1, keepdims=True))
        l = jnp.sum(p, axis=-1, keepdims=True)
        o = jnp.dot(p.astype(BF16), v_ref[g].astype(BF16), preferred_element_type=F32)
        o_ref[rows, :] = o / l


def _attention(q, mem_k, mem_v, *, row0, n_batch, t, n_seq, tq):
    q_tiles = t // tq
    rows = n_seq * tq
    assert t % tq == 0 and n_batch % n_seq == 0 and row0 % rows == 0 and (n_seq == 1 or q_tiles == 1)
    blk0 = row0 // rows
    kv_spec = pl.BlockSpec((n_seq, None, N_MEM, MEM_DH), lambda b, h, qi: (b, h, 0, 0))
    return pl.pallas_call(
        functools.partial(_attn_body, n_seq=n_seq, tq=tq),
        grid=(n_batch // n_seq, MEM_HEADS, q_tiles),
        in_specs=[pl.BlockSpec((rows, MEM_DH), lambda b, h, qi: (blk0 + b * q_tiles + qi, h)), kv_spec, kv_spec],
        out_specs=pl.BlockSpec((rows, MEM_DH), lambda b, h, qi: (b * q_tiles + qi, h)),
        out_shape=jax.ShapeDtypeStruct((n_batch * t, D_MODEL), F32),
        compiler_params=_params("parallel", "parallel", "parallel"),
    )(q, mem_k, mem_v)


ROW_TILE = (D_MODEL // LANES, LANES)


def _rows_to_tiles(x):
    chunks = [x[:, j * LANES:(j + 1) * LANES] for j in range(ROW_TILE[0])]
    return jnp.swapaxes(jnp.stack(chunks, axis=0), 0, 1)


def _tiles_to_rows(x):
    chunks = jnp.swapaxes(x, 0, 1)
    return jnp.concatenate([chunks[j] for j in range(ROW_TILE[0])], axis=1)


def _router_body(h_ref, g_ref, w_ref, b_ref, hn_ref, ids_ref, comb_ref):
    x = h_ref[...]
    hn = x * lax.rsqrt(jnp.mean(x * x, axis=-1, keepdims=True) + EPS) * g_ref[...]
    hn_ref[...] = _rows_to_tiles(hn)
    logits = jnp.dot(hn, w_ref[...], precision=lax.Precision.HIGHEST, preferred_element_type=F32) + b_ref[...]
    lane = lax.broadcasted_iota(jnp.int32, logits.shape, 1).astype(F32)
    neg = -jnp.inf

    def first_argmax(vals):
        m = jnp.max(vals, axis=-1, keepdims=True)
        return m, jnp.min(jnp.where(vals == m, lane, float(LANES)), axis=-1, keepdims=True)

    gl = jnp.where(lane < N_GROUPS, logits, neg)
    gmax, gsel = first_argmax(gl)
    pg_sel = 1.0 / jnp.sum(jnp.exp(gl - gmax), axis=-1, keepdims=True)
    e0 = N_GROUPS + gsel * EXP_PER_GROUP
    el = jnp.where((lane >= e0) & (lane < e0 + EXP_PER_GROUP), logits, neg)
    m1, i1 = first_argmax(el)
    m2, i2 = first_argmax(jnp.where(lane == i1, neg, el))
    e21 = jnp.exp(m2 - m1)
    c1 = pg_sel / (1.0 + e21)
    c2 = c1 * e21
    ids = jnp.where(lane == 0, i1 - N_GROUPS, jnp.where(lane == 1, i2 - N_GROUPS, 0.0))
    ids_ref[...] = ids.astype(jnp.int32)
    comb_ref[...] = jnp.where(lane == 0, c1, jnp.where(lane == 1, c2, 0.0))


def _router(h, g_ffn, w_gr, b_gr, w_er, b_er):
    m = h.shape[0]
    pad = LANES - N_GROUPS - N_EXPERTS
    w = jnp.concatenate([w_gr, w_er, jnp.zeros((D_MODEL, pad), F32)], axis=1)
    b = jnp.concatenate([b_gr, b_er, jnp.zeros((pad,), F32)]).reshape(1, LANES)
    row = lambda n: pl.BlockSpec((TM, n), lambda i: (i, 0))
    return pl.pallas_call(
        _router_body,
        grid=(m // TM,),
        in_specs=[row(D_MODEL), pl.BlockSpec((1, D_MODEL), lambda i: (0, 0)),
                  pl.BlockSpec((D_MODEL, LANES), lambda i: (0, 0)), pl.BlockSpec((1, LANES), lambda i: (0, 0))],
        out_specs=[pl.BlockSpec((TM,) + ROW_TILE, lambda i: (i, 0, 0)), row(LANES), row(LANES)],
        out_shape=[jax.ShapeDtypeStruct((m,) + ROW_TILE, F32), jax.ShapeDtypeStruct((m, LANES), jnp.int32),
                   jax.ShapeDtypeStruct((m, LANES), F32)],
        compiler_params=_params("parallel"),
    )(h, g_ffn.reshape(1, D_MODEL), w, b)


def _expert_body(blk_e_ref, idx_ref, idx_next_ref, x_hbm, wg_ref, wu_ref, wd_ref, out_hbm,
                 xbuf, ybuf, gsem, ssem):
    del blk_e_ref
    i = pl.program_id(0)
    n = pl.num_programs(0)
    slot = i % 2

    def start_gather(ref, sl):
        def body(r, c):
            pltpu.make_async_copy(x_hbm.at[ref[0, 0, r]], xbuf.at[sl, r], gsem.at[sl]).start()
            return c
        lax.fori_loop(0, MOE_ROWS, body, 0, unroll=8)

    def wait_gather(sl):
        pltpu.make_async_copy(x_hbm.at[pl.ds(0, MOE_ROWS)], xbuf.at[sl], gsem.at[sl]).wait()

    def wait_scatter(sl):
        pltpu.make_async_copy(ybuf.at[sl], out_hbm.at[pl.ds(0, MOE_ROWS)], ssem.at[sl]).wait()

    @pl.when(i == 0)
    def _():
        start_gather(idx_ref, 0)

    @pl.when(i + 1 < n)
    def _():
        start_gather(idx_next_ref, 1 - slot)

    wait_gather(slot)

    @pl.when(i >= 2)
    def _():
        wait_scatter(slot)

    x = _tiles_to_rows(xbuf[slot]).astype(BF16)
    hg = jnp.dot(x, wg_ref[0].astype(BF16), preferred_element_type=F32)
    hu = jnp.dot(x, wu_ref[0].astype(BF16), preferred_element_type=F32)
    act = (hg * jax.nn.sigmoid(hg) * hu).astype(BF16)
    ybuf[slot] = _rows_to_tiles(jnp.dot(act, wd_ref[0].astype(BF16), preferred_element_type=F32))

    def start_scatter(r, c):
        pltpu.make_async_copy(ybuf.at[slot, r], out_hbm.at[idx_ref[0, 1, r]], ssem.at[slot]).start()
        return c
    lax.fori_loop(0, MOE_ROWS, start_scatter, 0, unroll=8)

    @pl.when(i == n - 1)
    def _():
        wait_scatter(slot)

        @pl.when(n >= 2)
        def _():
            wait_scatter(1 - slot)


def _experts(hn, blk_e, idx, w_gate, w_up, w_down):
    n_blocks = idx.shape[0]
    p = n_blocks * MOE_ROWS
    idx_spec = lambda f: pl.BlockSpec((1, 2, MOE_ROWS), f, memory_space=pltpu.SMEM)
    grid_spec = pltpu.PrefetchScalarGridSpec(
        num_scalar_prefetch=1,
        grid=(n_blocks,),
        in_specs=[
            idx_spec(lambda i, be: (i, 0, 0)),
            idx_spec(lambda i, be: (jnp.minimum(i + 1, n_blocks - 1), 0, 0)),
            pl.BlockSpec(memory_space=pl.ANY),
            pl.BlockSpec((1, D_MODEL, D_EXPERT), lambda i, be: (be[i], 0, 0)),
            pl.BlockSpec((1, D_MODEL, D_EXPERT), lambda i, be: (be[i], 0, 0)),
            pl.BlockSpec((1, D_EXPERT, D_MODEL), lambda i, be: (be[i], 0, 0)),
        ],
        out_specs=pl.BlockSpec(memory_space=pl.ANY),
        scratch_shapes=[pltpu.VMEM((2, MOE_ROWS) + ROW_TILE, F32), pltpu.VMEM((2, MOE_ROWS) + ROW_TILE, F32),
                        pltpu.SemaphoreType.DMA((2,)), pltpu.SemaphoreType.DMA((2,))],
    )
    return pl.pallas_call(
        _expert_body,
        grid_spec=grid_spec,
        out_shape=jax.ShapeDtypeStruct((p,) + ROW_TILE, F32),
        compiler_params=_params("arbitrary"),
    )(blk_e, idx, idx, hn, w_gate, w_up, w_down)


def _route_plan(ids):
    n_tok = ids.shape[0]
    n_pairs = ids.size
    n_blocks = -(-(n_pairs + N_EXPERTS * (MOE_ROWS - 1)) // MOE_ROWS)
    p = n_blocks * MOE_ROWS
    flat_e = ids.reshape(n_pairs)
    onehot = (flat_e[:, None] == jnp.arange(N_EXPERTS, dtype=jnp.int32)[None, :]).astype(jnp.int32)
    csum = jnp.cumsum(onehot, axis=0)
    rank = jnp.sum(onehot * csum, axis=1) - 1
    counts = csum[-1]
    pcounts = (counts + MOE_ROWS - 1) // MOE_ROWS * MOE_ROWS
    pends = jnp.cumsum(pcounts)
    pstarts = pends - pcounts
    dest = jnp.sum(onehot * pstarts[None, :], axis=1) + rank
    pair = jnp.arange(n_pairs, dtype=jnp.int32)
    row_pair = jnp.full((p,), -1, jnp.int32).at[dest].set(pair)
    is_pad = row_pair < 0
    pad_rank = jnp.cumsum(is_pad.astype(jnp.int32)) - 1
    row_tok = jnp.where(is_pad, 0, row_pair // TOP_K)
    row_dst = jnp.where(is_pad, n_pairs + pad_rank, (row_pair % TOP_K) * n_tok + row_tok)
    block_start = jnp.arange(n_blocks, dtype=jnp.int32) * MOE_ROWS
    blk_e = jnp.minimum(jnp.sum((block_start[:, None] >= pends[None, :]).astype(jnp.int32), axis=1),
                        N_EXPERTS - 1).astype(jnp.int32)
    idx = jnp.stack([row_tok.reshape(n_blocks, MOE_ROWS), row_dst.reshape(n_blocks, MOE_ROWS)], axis=1)
    return blk_e, idx.astype(jnp.int32)


def _final_body(h_ref, first_ref, second_ref, comb_ref, g_ref, o_ref):
    first, second = _tiles_to_rows(first_ref[...]), _tiles_to_rows(second_ref[...])
    x = h_ref[...] + (first * comb_ref[:, 0:1] + second * comb_ref[:, 1:2])
    o_ref[...] = x * lax.rsqrt(jnp.mean(x * x, axis=-1, keepdims=True) + EPS) * g_ref[...]


def _final(h, pair_out, comb, g_final, *, row0, n_rows):
    n_tok = h.shape[0]
    assert row0 % TM == 0 and n_rows % TM == 0 and n_tok % TM == 0
    blk0, plane = row0 // TM, n_tok // TM
    row = lambda n, off: pl.BlockSpec((TM, n), lambda i: (off + i, 0))
    tiles = lambda off: pl.BlockSpec((TM,) + ROW_TILE, lambda i: (off + i, 0, 0))
    return pl.pallas_call(
        _final_body,
        grid=(n_rows // TM,),
        in_specs=[row(D_MODEL, blk0), tiles(blk0), tiles(plane + blk0), row(LANES, blk0),
                  pl.BlockSpec((1, D_MODEL), lambda i: (0, 0))],
        out_specs=pl.BlockSpec((TM, D_MODEL), lambda i: (i, 0)),
        out_shape=jax.ShapeDtypeStruct((n_rows, D_MODEL), F32),
        compiler_params=_params("parallel"),
    )(h, pair_out, pair_out, comb, g_final.reshape(1, D_MODEL))


def _reorder_last(x, shape, order):
    lead = x.shape[:-1]
    n = len(lead)
    y = x.reshape(lead + shape).transpose(tuple(range(n)) + tuple(n + o for o in order))
    return y.reshape(lead + (x.shape[-1],))


HALF_N = RWKV_N // 2


def _key_major(x):
    return _reorder_last(x, (RWKV_HEADS, RWKV_N), (1, 0))


def _key_major_inv(x):
    return _reorder_last(x, (RWKV_N, RWKV_HEADS), (1, 0))


def _value_major(x):
    return _reorder_last(x, (RWKV_HEADS, 2, HALF_N), (2, 1, 0))


def _value_major_inv(x):
    return _reorder_last(x, (HALF_N, 2, RWKV_HEADS), (2, 1, 0))


def _rwkv_cols(x, key_fn, value_fn):
    return jnp.concatenate([key_fn(x[..., :RWKV_W]), key_fn(x[..., RWKV_W:2 * RWKV_W]),
                            value_fn(x[..., 2 * RWKV_W:3 * RWKV_W]), x[..., 3 * RWKV_W:]], axis=-1)


def _sample_key_layout(x, b, t):
    return x.reshape(b, t, RWKV_N, RWKV_HEADS).transpose(3, 1, 2, 0)


def _sample_value_layout(x, b, t):
    return x.reshape(b, t, HALF_N, 2, RWKV_HEADS).transpose(4, 1, 3, 2, 0).reshape(RWKV_HEADS, t, RWKV_N, b)


def _sample_value_unlayout(y, b, t):
    return y.reshape(RWKV_HEADS, t, 2, HALF_N, b).transpose(4, 1, 3, 2, 0).reshape(b * t, RWKV_W)


def kernel(x_prompt, x_sample, mem_prompt, state_ret, state_rwkv, state_shift, cache_mem_k, cache_mem_v,
           g_mix, w_in, ret_gn, rwkv_mu, rwkv_w0, rwkv_w2, rwkv_a0, rwkv_a2, rwkv_g2, rwkv_k_k, rwkv_k_a,
           rwkv_r_k, rwkv_lnx_w, rwkv_lnx_b, w_out, g_mem_q, g_mem_kv, w_mq, w_mk, w_mv, w_mo, g_ffn,
           w_group_router, b_group_router, w_expert_router, b_expert_router, w_e_gate, w_e_up, w_e_down,
           g_final):
    assert w_in.shape[0] == 1, "single-layer decoder"
    bp, tp, d = x_prompt.shape
    bs, ts, _ = x_sample.shape
    np_tok, ns_tok = bp * tp, bs * ts
    assert d == D_MODEL and bp * RWKV_HEADS * 2 == LANES and bs == LANES
    l = 0
    x_parts = [x_prompt.reshape(np_tok, d), x_sample.reshape(ns_tok, d)]

    w_in_l = jnp.concatenate([w_in[l][:, :N_RET_COLS], _rwkv_cols(w_in[l][:, N_RET_COLS:], _key_major, _value_major)],
                             axis=1)
    (proj,) = _matmul(x_parts, [w_in_l.astype(BF16)], gain=g_mix[l])

    pos_p = np.arange(tp)
    pos_s = PAST_LEN + np.arange(ts)
    zero_ret = jnp.zeros((bp, RET_HEADS, RET_DK, RET_DV), F32)
    oret_p, sret_p = _retention(proj, zero_ret, ret_gn[l], pos_p, row0=0, n_batch=bp, t=tp, n_seq=1)
    oret_s, sret_s = _retention(proj, state_ret[l], ret_gn[l], pos_s, row0=np_tok, n_batch=bs, t=ts, n_seq=16)

    pre_w = (_rwkv_cols(rwkv_mu[l], _key_major, _value_major), _key_major(rwkv_w0[l]), _key_major(rwkv_w2[l]),
             _key_major(rwkv_a0[l]), _key_major(rwkv_a2[l]), _value_major(rwkv_g2[l]))
    zero_shift = jnp.zeros((bp, N_RWKV_COLS), F32)
    shift_in = _rwkv_cols(state_shift[l], _key_major, _value_major)
    r_p, k_p, v_p, w_p, a_p, gate_p = _rwkv_pre(proj, zero_shift, *pre_w, row0=0, n_batch=bp, t=tp,
                                                 n_seq=1, c=256, transposed=True)
    r_s, k_s, v_s, w_s, a_s, gate_s = _rwkv_pre(proj, shift_in, *pre_w, row0=np_tok, n_batch=bs, t=ts,
                                                 n_seq=16, c=ts, transposed=False)

    kvec = lambda v: v.reshape(RWKV_HEADS, RWKV_N)
    key_par = lambda v: jnp.broadcast_to(kvec(v).T[:, None, None, :], (RWKV_N, 2, bp, RWKV_HEADS)).reshape(
        RWKV_N, LANES)

    val_par = lambda v: jnp.broadcast_to(
        v.reshape(RWKV_HEADS, 2, HALF_N).transpose(2, 1, 0)[:, :, None, :],
        (HALF_N, 2, bp, RWKV_HEADS)).reshape(HALF_N, LANES)
    mem_shape = (N_MEM, MEM_HEADS, MEM_DH)
    y_p, srw_p, cache_k_heads, cache_v_heads = _rwkv_scan_prompt(
        r_p, k_p, w_p, a_p, v_p, key_par(rwkv_k_k[l]), key_par(rwkv_k_a[l]), key_par(rwkv_r_k[l]),
        val_par(rwkv_lnx_w[l]), val_par(rwkv_lnx_b[l]), jnp.zeros((HALF_N, RWKV_N, LANES), F32),
        cache_mem_k.reshape(bs, *mem_shape), cache_mem_v.reshape(bs, *mem_shape))
    srw_p = srw_p.reshape(HALF_N, RWKV_N, 2, bp, RWKV_HEADS).transpose(3, 4, 2, 0, 1).reshape(
        bp, RWKV_HEADS, RWKV_N, RWKV_N)
    head_par = lambda v: jnp.broadcast_to(kvec(v)[:, :, None], (RWKV_HEADS, RWKV_N, LANES))
    y_s, srw_s = _rwkv_scan(
        _sample_key_layout(r_s, bs, ts), _sample_key_layout(k_s, bs, ts), _sample_key_layout(w_s, bs, ts),
        _sample_key_layout(a_s, bs, ts), _sample_value_layout(v_s, bs, ts),
        head_par(rwkv_k_k[l]), head_par(rwkv_k_a[l]), head_par(rwkv_r_k[l]),
        head_par(rwkv_lnx_w[l]), head_par(rwkv_lnx_b[l]),
        state_rwkv[l].astype(F32).transpose(1, 2, 3, 0), tc=ts, halves=1)
    y_s = _sample_value_unlayout(y_s, bs, ts)
    srw_s = srw_s.transpose(3, 0, 1, 2)

    w_rwkv_out = _value_major(w_out[l][RET_W:].T).T
    h = _merge(x_parts, [oret_p, oret_s], [gate_p, gate_s], y_p, y_s, w_out[l][:RET_W], w_rwkv_out)

    mk, mv, mk_heads, mv_heads = _mem_kv(mem_prompt, g_mem_kv[l], w_mk[l], w_mv[l])
    (q,) = _matmul([h], [w_mq[l].astype(BF16)], gain=g_mem_q[l])
    att_p = _attention(q, mk_heads, mv_heads, row0=0, n_batch=bp, t=tp, n_seq=1, tq=TM)
    att_s = _attention(q, cache_k_heads, cache_v_heads, row0=np_tok, n_batch=bs, t=ts, n_seq=16, tq=ts)
    (h,) = _matmul([att_p, att_s], [w_mo[l].astype(BF16)], residual=h)

    hn, ids, comb = _router(h, g_ffn[l], w_group_router[l], b_group_router[l], w_expert_router[l],
                            b_expert_router[l])
    blk_e, idx = _route_plan(ids[:, :TOP_K])
    pair_out = _experts(hn, blk_e, idx, w_e_gate[l], w_e_up[l], w_e_down[l])
    y_prompt = _final(h, pair_out, comb, g_final, row0=0, n_rows=np_tok).reshape(bp, tp, d)
    y_sample = _final(h, pair_out, comb, g_final, row0=np_tok, n_rows=ns_tok).reshape(bs, ts, d)

    shift_p = lax.slice(proj, (tp - 1, N_RET_COLS), (np_tok, N_IN_COLS), (tp, 1))
    shift_s = lax.slice(proj, (np_tok + ts - 1, N_RET_COLS), (np_tok + ns_tok, N_IN_COLS), (ts, 1))
    shift_p = _rwkv_cols(shift_p, _key_major_inv, _value_major_inv)
    shift_s = _rwkv_cols(shift_s, _key_major_inv, _value_major_inv)
    return (y_prompt, y_sample, sret_p[None], srw_p[None], shift_p[None],
            mk.reshape(1, bp, *mem_shape), mv.reshape(1, bp, *mem_shape),
            sret_s[None], srw_s[None], shift_s[None])
```

```python
import functools

import numpy as np
import jax
import jax.numpy as jnp
from jax import lax
from jax.experimental import pallas as pl
from jax.experimental.pallas import tpu as pltpu

F32 = jnp.float32
BF16 = jnp.bfloat16

D_MODEL = 1024
PAST_LEN = 16384
N_MEM = 256
MEM_HEADS = 4
MEM_DH = D_MODEL // MEM_HEADS
RET_HEADS = 4
RET_W = D_MODEL // 2
RET_DV = RET_W // RET_HEADS
RET_DK = RET_DV // 2
RET_QK = RET_HEADS * RET_DK
RET_CHUNK = 128
ROPE_BASE = 10000.0
RWKV_N = 64
RWKV_W = D_MODEL - RET_W
RWKV_HEADS = RWKV_W // RWKV_N
LORA_W = 64
LORA_A = 64
LORA_G = 128
LNX_EPS = 64e-5
N_RET_COLS = 2 * RET_QK + 2 * RET_W
N_RWKV_COLS = 3 * RWKV_W + LORA_W + LORA_A + LORA_G
N_IN_COLS = N_RET_COLS + N_RWKV_COLS
N_GROUPS = 4
EXP_PER_GROUP = 8
N_EXPERTS = N_GROUPS * EXP_PER_GROUP
TOP_K = 2
D_EXPERT = D_MODEL // 2
EPS = 1e-6

LANES = 128
MOE_ROWS = 128
TM = 512


def _params(*sem):
    return pltpu.CompilerParams(dimension_semantics=sem)


def _row_part_specs(parts, tm):
    specs, counts, start = [], [], 0
    for part in parts:
        nb = part.shape[0] // tm
        assert nb * tm == part.shape[0]
        specs.append(pl.BlockSpec((tm, part.shape[1]), lambda i, s=start, n=nb: (jnp.clip(i - s, 0, n - 1), 0)))
        counts.append(nb)
        start += nb
    return specs, counts


def _read_row_parts(refs, counts):
    i = pl.program_id(0)
    x = refs[0][...]
    start = counts[0]
    for ref, nb in zip(refs[1:], counts[1:]):
        x = jnp.where(i >= start, ref[...], x)
        start += nb
    return x


def _mm_body(*refs, part_counts, n_w, has_gain, has_res, n_chunk):
    it = iter(refs)
    x_refs = [next(it) for _ in part_counts]
    g_ref = next(it) if has_gain else None
    w_refs = [next(it) for _ in range(n_w)]
    r_ref = next(it) if has_res else None
    o_refs = [next(it) for _ in range(n_w)]
    x = _read_row_parts(x_refs, part_counts).astype(F32)
    if has_gain:
        x = x * lax.rsqrt(jnp.mean(x * x, axis=-1, keepdims=True) + EPS) * g_ref[...]
    xb = x.astype(BF16)
    for w_ref, o_ref in zip(w_refs, o_refs):
        for j in range(0, w_ref.shape[1], n_chunk):
            acc = jnp.dot(xb, w_ref[:, j:j + n_chunk], preferred_element_type=F32)
            if has_res:
                acc = acc + r_ref[:, j:j + n_chunk]
            o_ref[:, j:j + n_chunk] = acc


def _matmul(x_parts, ws, *, gain=None, residual=None, tm=TM):
    k = x_parts[0].shape[1]
    m = sum(part.shape[0] for part in x_parts)
    n_chunk = 256
    assert all(w.shape[1] % n_chunk == 0 for w in ws)
    assert residual is None or len(ws) == 1
    in_specs, part_counts = _row_part_specs(x_parts, tm)
    args = list(x_parts)
    if gain is not None:
        in_specs.append(pl.BlockSpec((1, k), lambda i: (0, 0)))
        args.append(gain.reshape(1, k).astype(F32))
    for w in ws:
        in_specs.append(pl.BlockSpec(w.shape, lambda i: (0, 0)))
        args.append(w)
    if residual is not None:
        in_specs.append(pl.BlockSpec((tm, ws[0].shape[1]), lambda i: (i, 0)))
        args.append(residual)
    return pl.pallas_call(
        functools.partial(_mm_body, part_counts=part_counts, n_w=len(ws), has_gain=gain is not None,
                          has_res=residual is not None, n_chunk=n_chunk),
        grid=(m // tm,),
        in_specs=in_specs,
        out_specs=[pl.BlockSpec((tm, w.shape[1]), lambda i: (i, 0)) for w in ws],
        out_shape=[jax.ShapeDtypeStruct((m, w.shape[1]), F32) for w in ws],
        compiler_params=_params("parallel"),
    )(*args)


def _rot_tables(pos):
    half = RET_DK // 2
    inv_freq = ROPE_BASE ** (-(np.arange(half, dtype=np.float64) / half))
    ang = pos.astype(np.float64)[:, None] * inv_freq[None, :]
    cos, sin = np.cos(ang), np.sin(ang)
    zero = np.zeros_like(sin)
    c = np.tile(np.concatenate([cos, cos], axis=1), (1, RET_HEADS))
    s_lo = np.tile(np.concatenate([-sin, zero], axis=1), (1, RET_HEADS))
    s_hi = np.tile(np.concatenate([zero, sin], axis=1), (1, RET_HEADS))
    return [jnp.asarray(t, F32) for t in (c, s_lo, s_hi)]


def _ret_decay_tables(c):
    lg = np.log1p(-np.exp2(-5.0 - np.arange(RET_HEADS, dtype=np.float64)))
    idx = np.arange(c, dtype=np.float64)
    diff = idx[:, None] - idx[None, :]
    mask = np.where(diff[None] >= 0, np.exp(np.maximum(diff, 0.0)[None] * lg[:, None, None]), 0.0)
    q_dec = np.repeat(np.exp((idx[:, None] + 1.0) * lg[None, :]), RET_DV, axis=1)
    k_dec = np.repeat(np.exp((c - 1.0 - idx)[:, None] * lg[None, :]), RET_DK, axis=1)
    c_dec = [float(v) for v in np.exp(c * lg)]
    return jnp.asarray(mask, F32), jnp.asarray(q_dec, F32), jnp.asarray(k_dec, F32), c_dec


def _ret_body(q_ref, k_ref, v_ref, gate_ref, c_ref, slo_ref, shi_ref, mask_ref, qdec_ref, kdec_ref,
              gn_ref, s0_ref, o_ref, sout_ref, s_scr, *, n_seq, c, c_dec):
    ci = pl.program_id(1)

    @pl.when(ci == 0)
    def _():
        s_scr[...] = s0_ref[...].astype(F32)

    cos, s_lo, s_hi = c_ref[...], slo_ref[...], shi_ref[...]
    half = RET_DK // 2

    def rope(x):
        return x * cos + pltpu.roll(x, RET_QK - half, 1) * s_lo + pltpu.roll(x, half, 1) * s_hi

    nt = (((1,), (1,)), ((), ()))
    tn = (((0,), (0,)), ((), ()))
    for g in range(n_seq):
        rows = slice(g * c, (g + 1) * c)
        q = rope(q_ref[rows, :].astype(F32))
        k = rope(k_ref[rows, :].astype(F32)) * (RET_DK ** -0.5)
        k_st = k * kdec_ref[...]
        for h in range(RET_HEADS):
            kc = slice(h * RET_DK, (h + 1) * RET_DK)
            vc = slice(h * RET_DV, (h + 1) * RET_DV)
            qh = q[:, kc].astype(BF16)
            vh = v_ref[rows, vc].astype(BF16)
            s_h = s_scr[g, h]
            att = lax.dot_general(qh, k[:, kc].astype(BF16), nt, preferred_element_type=F32) * mask_ref[h]
            o = jnp.dot(att.astype(BF16), vh, preferred_element_type=F32)
            o = o + jnp.dot(qh, s_h.astype(BF16), preferred_element_type=F32) * qdec_ref[:, vc]
            s_scr[g, h] = s_h * c_dec[h] + lax.dot_general(
                k_st[:, kc].astype(BF16), vh, tn, preferred_element_type=F32)
            o = o * lax.rsqrt(jnp.mean(o * o, axis=-1, keepdims=True) + EPS)
            gate = gate_ref[rows, vc].astype(F32)
            o_ref[rows, vc] = o * gn_ref[:, vc] * (gate * jax.nn.sigmoid(gate))

    @pl.when(ci == pl.num_programs(1) - 1)
    def _():
        sout_ref[...] = s_scr[...]


def _retention(proj, s0, ret_gn, pos, *, row0, n_batch, t, n_seq):
    c = RET_CHUNK if t % RET_CHUNK == 0 else t
    n_chunks = t // c
    rows = n_seq * c
    assert n_batch % n_seq == 0 and row0 % rows == 0 and (n_seq == 1 or n_chunks == 1)
    blk0 = row0 // rows
    mask, q_dec, k_dec, c_dec = _ret_decay_tables(c)
    cos, s_lo, s_hi = _rot_tables(pos)

    def row_map(col):
        return lambda b, ci: (blk0 + b * n_chunks + ci, col)

    def const2(b, ci):
        return (0, 0)

    state_spec = pl.BlockSpec((n_seq, RET_HEADS, RET_DK, RET_DV), lambda b, ci: (b, 0, 0, 0))
    in_specs = [
        pl.BlockSpec((rows, RET_QK), row_map(0)),
        pl.BlockSpec((rows, RET_QK), row_map(1)),
        pl.BlockSpec((rows, RET_W), row_map(1)),
        pl.BlockSpec((rows, RET_W), row_map(2)),
        pl.BlockSpec((c, RET_QK), lambda b, ci: (ci, 0)),
        pl.BlockSpec((c, RET_QK), lambda b, ci: (ci, 0)),
        pl.BlockSpec((c, RET_QK), lambda b, ci: (ci, 0)),
        pl.BlockSpec((RET_HEADS, c, c), lambda b, ci: (0, 0, 0)),
        pl.BlockSpec((c, RET_W), const2),
        pl.BlockSpec((c, RET_QK), const2),
        pl.BlockSpec((1, RET_W), const2),
        state_spec,
    ]
    return pl.pallas_call(
        functools.partial(_ret_body, n_seq=n_seq, c=c, c_dec=c_dec),
        grid=(n_batch // n_seq, n_chunks),
        in_specs=in_specs,
        out_specs=[pl.BlockSpec((rows, RET_W), lambda b, ci: (b * n_chunks + ci, 0)), state_spec],
        out_shape=[jax.ShapeDtypeStruct((n_batch * t, RET_W), F32),
                   jax.ShapeDtypeStruct((n_batch, RET_HEADS, RET_DK, RET_DV), F32)],
        scratch_shapes=[pltpu.VMEM((n_seq, RET_HEADS, RET_DK, RET_DV), F32)],
        compiler_params=_params("parallel", "arbitrary"),
    )(proj, proj, proj, proj, cos, s_lo, s_hi, mask, q_dec, k_dec, ret_gn.reshape(1, RET_W).astype(F32), s0)


def _rwkv_pre_body(r_ref, k_ref, v_ref, lo_ref, shift_ref, mu_ref, w0_ref, w2_ref, a0_ref, a2_ref, g2_ref,
                   ro_ref, ko_ref, vo_ref, wo_ref, ao_ref, go_ref, prev_scr, *, n_seq, c, transposed):
    ci = pl.program_id(1)

    @pl.when(ci == 0)
    def _():
        for g in range(n_seq):
            prev_scr[g] = shift_ref[g].astype(F32)

    first_row = lax.broadcasted_iota(jnp.int32, (c, 1), 0) == 0

    def shifted(x_ref, g, col0):
        w = x_ref.shape[1]
        x = x_ref[g * c:(g + 1) * c, :].astype(F32)
        prev_row = prev_scr[g, :, col0:col0 + w]
        prev = jnp.where(first_row, prev_row, pltpu.roll(x, 1, 0))
        prev_scr[g, :, col0:col0 + w] = x[c - 1:c, :]
        return x + (prev - x) * mu_ref[:, col0:col0 + w]

    def put(o_ref, g, val):
        if transposed:
            o_ref[...] = val.T
        else:
            o_ref[g * c:(g + 1) * c, :] = val

    for g in range(n_seq):
        put(ro_ref, g, shifted(r_ref, g, 0))
        put(ko_ref, g, shifted(k_ref, g, RWKV_W))
        put(vo_ref, g, shifted(v_ref, g, 2 * RWKV_W))
        lo = shifted(lo_ref, g, 3 * RWKV_W)
        hw = lo[:, :LORA_W]
        ha = lo[:, LORA_W:LORA_W + LORA_A]
        hg = lo[:, LORA_W + LORA_A:]
        u = w0_ref[...] + jnp.dot(jnp.tanh(hw).astype(BF16), w2_ref[...], preferred_element_type=F32)
        put(wo_ref, g, jnp.exp(-float(np.exp(-0.5)) * jax.nn.sigmoid(u)))
        put(ao_ref, g, jax.nn.sigmoid(
            a0_ref[...] + jnp.dot(ha.astype(BF16), a2_ref[...], preferred_element_type=F32)))
        go_ref[g * c:(g + 1) * c, :] = jnp.dot(jax.nn.sigmoid(hg).astype(BF16), g2_ref[...],
                                                preferred_element_type=F32)


def _rwkv_pre(proj, s_shift, mu, w0, w2, a0, a2, g2, *, row0, n_batch, t, n_seq, c, transposed):
    n_chunks = t // c
    rows = n_seq * c
    assert t % c == 0 and n_batch % n_seq == 0 and row0 % rows == 0 and (n_seq == 1 or n_chunks == 1)
    assert not transposed or n_seq == 1
    blk0 = row0 // rows
    col_r = N_RET_COLS // RWKV_W
    lo_w = LORA_W + LORA_A + LORA_G
    col_lo = (N_RET_COLS + 3 * RWKV_W) // lo_w
    assert col_r * RWKV_W == N_RET_COLS and col_lo * lo_w == N_RET_COLS + 3 * RWKV_W

    def row_map(col):
        return lambda b, ci: (blk0 + b * n_chunks + ci, col)

    def const2(b, ci):
        return (0, 0)

    in_specs = [
        pl.BlockSpec((rows, RWKV_W), row_map(col_r)),
        pl.BlockSpec((rows, RWKV_W), row_map(col_r + 1)),
        pl.BlockSpec((rows, RWKV_W), row_map(col_r + 2)),
        pl.BlockSpec((rows, lo_w), row_map(col_lo)),
        pl.BlockSpec((n_seq, 1, N_RWKV_COLS), lambda b, ci: (b, 0, 0)),
        pl.BlockSpec((1, N_RWKV_COLS), const2),
        pl.BlockSpec((1, RWKV_W), const2),
        pl.BlockSpec((LORA_W, RWKV_W), const2),
        pl.BlockSpec((1, RWKV_W), const2),
        pl.BlockSpec((LORA_A, RWKV_W), const2),
        pl.BlockSpec((LORA_G, RWKV_W), const2),
    ]
    nat_spec = pl.BlockSpec((rows, RWKV_W), lambda b, ci: (b * n_chunks + ci, 0))
    nat_shape = jax.ShapeDtypeStruct((n_batch * t, RWKV_W), F32)
    if transposed:
        vec_spec = pl.BlockSpec((None, RWKV_W, c), lambda b, ci: (b, 0, ci))
        vec_shape = jax.ShapeDtypeStruct((n_batch, RWKV_W, t), F32)
    else:
        vec_spec, vec_shape = nat_spec, nat_shape
    return pl.pallas_call(
        functools.partial(_rwkv_pre_body, n_seq=n_seq, c=c, transposed=transposed),
        grid=(n_batch // n_seq, n_chunks),
        in_specs=in_specs,
        out_specs=[vec_spec] * 5 + [nat_spec],
        out_shape=[vec_shape] * 5 + [nat_shape],
        scratch_shapes=[pltpu.VMEM((n_seq, 1, N_RWKV_COLS), F32)],
        compiler_params=_params("parallel", "arbitrary"),
    )(proj, proj, proj, proj, s_shift.reshape(n_batch, 1, N_RWKV_COLS),
      mu.reshape(1, -1), w0.reshape(1, -1), w2.astype(BF16), a0.reshape(1, -1),
      a2.astype(BF16), g2.astype(BF16))


def _scan_body(r_ref, k_ref, w_ref, a_ref, v_ref, kk_ref, ka_ref, rk_ref, lw_ref, lb_ref, s0_ref,
               y_ref, sout_ref, s_scr, a_scr, b_scr, km_scr, *, tc, vr, halves):
    ci = pl.program_id(1)

    @pl.when(ci == 0)
    def _():
        s_scr[...] = s0_ref[...].astype(F32)

    def ksum(x):
        return jnp.sum(x, axis=-2, keepdims=True)

    def vsum(x):
        if halves == 2:
            x2 = x.reshape(tc * vr, LANES)
            x = (x2 + pltpu.roll(x2, LANES // 2, 1)).reshape(tc, vr, LANES)
        return jnp.sum(x, axis=1, keepdims=True)

    kr = k_ref[...]
    a = a_ref[...]
    kk = kr * kk_ref[...]
    kk = kk / jnp.maximum(jnp.sqrt(ksum(kk * kk)), 1e-12)
    a_scr[...] = -kk
    b_scr[...] = kk * a
    km_scr[...] = kr * (1.0 + (a - 1.0) * ka_ref[...])

    def token(t, carry):
        r, w, avec, bvec, kmod = r_ref[t], w_ref[t], a_scr[t], b_scr[t], km_scr[t]

        def value_row(i, c2):
            s = s_scr[i]
            sa = ksum(s * avec)
            s = s * w + sa * bvec + v_ref[t, pl.ds(i, 1), :] * kmod
            s_scr[i] = s
            y_ref[t, pl.ds(i, 1), :] = ksum(s * r)
            return c2

        lax.fori_loop(0, vr, value_row, 0, unroll=16)
        return carry

    lax.fori_loop(0, tc, token, 0)

    y = y_ref[...]
    d = y - vsum(y) * (1.0 / RWKV_N)
    var = vsum(d * d) * (1.0 / RWKV_N)
    bonus = ksum(r_ref[...] * km_scr[...] * rk_ref[...])
    y_ref[...] = d * lax.rsqrt(var + LNX_EPS) * lw_ref[...] + lb_ref[...] + bonus * v_ref[...]

    @pl.when(ci == pl.num_programs(1) - 1)
    def _():
        sout_ref[...] = s_scr[...]


def _rwkv_scan(r, k, w, a, v, k_k, k_a, r_k, lnx_w, lnx_b, s0, *, tc, halves):
    n_grp, t, _, lanes = r.shape
    vr = v.shape[2]
    assert lanes == LANES and t % tc == 0 and vr * halves == RWKV_N

    def tok_spec(rows):
        return pl.BlockSpec((None, tc, rows, LANES), lambda g, ci: (g, ci, 0, 0))

    def par_spec(rows):
        return pl.BlockSpec((None, rows, LANES), lambda g, ci: (g, 0, 0))

    st_spec = pl.BlockSpec((None, vr, RWKV_N, LANES), lambda g, ci: (g, 0, 0, 0))
    key_scratch = pltpu.VMEM((tc, RWKV_N, LANES), F32)
    return pl.pallas_call(
        functools.partial(_scan_body, tc=tc, vr=vr, halves=halves),
        grid=(n_grp, t // tc),
        in_specs=[tok_spec(RWKV_N)] * 4 + [tok_spec(vr)] + [par_spec(RWKV_N)] * 3 + [par_spec(vr)] * 2 + [st_spec],
        out_specs=[tok_spec(vr), st_spec],
        out_shape=[jax.ShapeDtypeStruct((n_grp, t, vr, LANES), F32),
                   jax.ShapeDtypeStruct((n_grp, vr, RWKV_N, LANES), F32)],
        scratch_shapes=[pltpu.VMEM((vr, RWKV_N, LANES), F32), key_scratch, key_scratch, key_scratch],
        compiler_params=_params("parallel", "arbitrary"),
    )(r, k, w, a, v, k_k, k_a, r_k, lnx_w, lnx_b, s0)


SCAN_TC = 128
SCAN_SUB = 64
MEM_CHUNKS = 4


def _scan_prompt_body(r_ref, k_ref, w_ref, a_ref, v_ref, kk_ref, ka_ref, rk_ref, lw_ref, lb_ref, s0_ref,
                      memk_hbm, memv_hbm, y_ref, sout_ref, memk_out, memv_out,
                      s_scr, r_c, w_c, a_c, b_c, km_c, v_c, y_c, mem_stage, mem_in_sem, mem_out_sem,
                      *, n_b, mem_seqs):
    ci = pl.program_id(0)
    vr = RWKV_N // 2
    ts = SCAN_SUB
    tile = RWKV_HEADS
    half_lanes = LANES // 2

    @pl.when(ci == 0)
    def _():
        s_scr[...] = s0_ref[...].astype(F32)

    chunk_seqs = mem_seqs // MEM_CHUNKS

    def mem_in(chunk, slot):
        copies = []
        for j in range(chunk_seqs):
            seq = (ci * MEM_CHUNKS + chunk) * chunk_seqs + j
            for h in range(MEM_HEADS):
                copies.append(pltpu.make_async_copy(memk_hbm.at[seq, :, h, :], mem_stage.at[slot, 0, j, h],
                                                    mem_in_sem.at[slot]))
                copies.append(pltpu.make_async_copy(memv_hbm.at[seq, :, h, :], mem_stage.at[slot, 1, j, h],
                                                    mem_in_sem.at[slot]))
        return copies

    def mem_out(chunk, slot):
        seqs = pl.ds((ci * MEM_CHUNKS + chunk) * chunk_seqs, chunk_seqs)
        return [pltpu.make_async_copy(mem_stage.at[slot, 0], memk_out.at[seqs], mem_out_sem.at[slot]),
                pltpu.make_async_copy(mem_stage.at[slot, 1], memv_out.at[seqs], mem_out_sem.at[slot])]

    def mem_phase(p):
        if 1 <= p <= MEM_CHUNKS:
            for cp in mem_in(p - 1, (p - 1) % 2):
                cp.wait()
            for cp in mem_out(p - 1, (p - 1) % 2):
                cp.start()
        if 2 <= p <= MEM_CHUNKS + 1:
            for cp in mem_out(p - 2, p % 2):
                cp.wait()
        if p < MEM_CHUNKS:
            for cp in mem_in(p, p % 2):
                cp.start()

    mem_phase(0)

    low = lax.broadcasted_iota(jnp.int32, (ts, LANES), 1) < half_lanes

    def feature_pair_rows(x_ref, base):
        tiles = [x_ref[b, pl.ds(base + f * tile, tile), :] for f in range(2) for b in range(n_b)]
        return jnp.concatenate(tiles, axis=0).T

    def key_to_chain(x_ref, dst, t0):
        def group(g, c):
            rows = []
            for j in range(4):
                mt = feature_pair_rows(x_ref, pl.multiple_of((g * 4 + j) * 2 * tile, 2 * tile))[t0:t0 + ts]
                sw = pltpu.roll(mt, half_lanes, 1)
                rows += [jnp.where(low, mt, sw), jnp.where(low, sw, mt)]
            dst[:, pl.ds(pl.multiple_of(g * 8, 8), 8), :] = jnp.swapaxes(jnp.stack(rows, axis=0), 0, 1)
            return c
        lax.fori_loop(0, RWKV_N // 8, group, 0)

    def value_to_chain(g, c):
        rows = [feature_pair_rows(v_ref, pl.multiple_of((g * 8 + j) * 2 * tile, 2 * tile)) for j in range(8)]
        v_c[:, pl.ds(pl.multiple_of(g * 8, 8), 8), :] = jnp.swapaxes(jnp.stack(rows, axis=0), 0, 1)
        return c
    lax.fori_loop(0, vr // 8, value_to_chain, 0)

    def ksum(x):
        return jnp.sum(x, axis=-2, keepdims=True)

    for t0 in range(0, SCAN_TC, ts):
        key_to_chain(r_ref, r_c, t0)
        key_to_chain(w_ref, w_c, t0)
        key_to_chain(k_ref, km_c, t0)
        key_to_chain(a_ref, b_c, t0)

        def prep(g8, c):
            toks = pl.ds(pl.multiple_of(g8 * 8, 8), 8)
            kr = km_c[toks]
            a = b_c[toks]
            kk = kr * kk_ref[...]
            kk = kk / jnp.maximum(jnp.sqrt(ksum(kk * kk)), 1e-12)
            a_c[toks] = -kk
            b_c[toks] = kk * a
            km_c[toks] = kr * (1.0 + (a - 1.0) * ka_ref[...])
            return c
        lax.fori_loop(0, ts // 8, prep, 0)
        mem_phase(1 + 2 * (t0 // ts))

        def token(t, carry):
            r, w, avec, bvec, kmod = r_c[t], w_c[t], a_c[t], b_c[t], km_c[t]

            def value_row(i, c2):
                s = s_scr[i]
                sa = ksum(s * avec)
                s = s * w + sa * bvec + v_c[t0 + t, pl.ds(i, 1), :] * kmod
                s_scr[i] = s
                y_c[t0 + t, pl.ds(i, 1), :] = ksum(s * r)
                return c2

            lax.fori_loop(0, vr, value_row, 0, unroll=True)
            return carry

        lax.fori_loop(0, ts, token, 0)
        mem_phase(2 + 2 * (t0 // ts))

        def post(g8, c):
            ktoks = pl.ds(pl.multiple_of(g8 * 8, 8), 8)
            vtoks = pl.ds(pl.multiple_of(t0 + g8 * 8, 8), 8)

            def vsum(x):
                x2 = x.reshape(8 * vr, LANES)
                x2 = x2 + pltpu.roll(x2, half_lanes, 1)
                return jnp.sum(x2.reshape(8, vr, LANES), axis=1, keepdims=True)

            y = y_c[vtoks]
            d = y - vsum(y) * (1.0 / RWKV_N)
            var = vsum(d * d) * (1.0 / RWKV_N)
            bonus = ksum(r_c[ktoks] * km_c[ktoks] * rk_ref[...])
            y_c[vtoks] = d * lax.rsqrt(var + LNX_EPS) * lw_ref[...] + lb_ref[...] + bonus * v_c[vtoks]
            return c
        lax.fori_loop(0, ts // 8, post, 0)

    def value_from_chain(g, c):
        blk = jnp.swapaxes(y_c[:, pl.ds(pl.multiple_of(g * 8, 8), 8), :], 0, 1)
        for j in range(8):
            mt = blk[j].T
            base = pl.multiple_of((g * 8 + j) * 2 * tile, 2 * tile)
            for hf in range(2):
                for b in range(n_b):
                    row0 = (hf * n_b + b) * tile
                    y_ref[b, pl.ds(base + hf * tile, tile), :] = mt[row0:row0 + tile, :]
        return c
    lax.fori_loop(0, vr // 8, value_from_chain, 0)

    assert 2 * (SCAN_TC // ts) == MEM_CHUNKS
    mem_phase(MEM_CHUNKS + 1)

    @pl.when(ci == pl.num_programs(0) - 1)
    def _():
        sout_ref[...] = s_scr[...]


def _rwkv_scan_prompt(r, k, w, a, v, k_k, k_a, r_k, lnx_w, lnx_b, s0, mem_k, mem_v):
    n_b, _, t = r.shape
    vr = RWKV_N // 2
    n_steps = t // SCAN_TC
    n_mem_seq = mem_k.shape[0]
    assert t % SCAN_TC == 0 and 2 * n_b * RWKV_HEADS == LANES and n_mem_seq % (n_steps * MEM_CHUNKS) == 0
    chunk_seqs = n_mem_seq // (n_steps * MEM_CHUNKS)
    any_spec = pl.BlockSpec(memory_space=pl.ANY)
    mem_shape = jax.ShapeDtypeStruct((n_mem_seq, MEM_HEADS, N_MEM, MEM_DH), F32)
    tok_spec = pl.BlockSpec((n_b, RWKV_W, SCAN_TC), lambda ci: (0, 0, ci))
    key_par = pl.BlockSpec((RWKV_N, LANES), lambda ci: (0, 0))
    val_par = pl.BlockSpec((vr, LANES), lambda ci: (0, 0))
    st_spec = pl.BlockSpec((vr, RWKV_N, LANES), lambda ci: (0, 0, 0))
    key_chain = pltpu.VMEM((SCAN_SUB, RWKV_N, LANES), F32)
    val_chain = pltpu.VMEM((SCAN_TC, vr, LANES), F32)
    return pl.pallas_call(
        functools.partial(_scan_prompt_body, n_b=n_b, mem_seqs=n_mem_seq // n_steps),
        grid=(n_steps,),
        in_specs=[tok_spec] * 5 + [key_par] * 3 + [val_par] * 2 + [st_spec, any_spec, any_spec],
        out_specs=[tok_spec, st_spec, any_spec, any_spec],
        out_shape=[jax.ShapeDtypeStruct((n_b, RWKV_W, t), F32),
                   jax.ShapeDtypeStruct((vr, RWKV_N, LANES), F32), mem_shape, mem_shape],
        scratch_shapes=([pltpu.VMEM((vr, RWKV_N, LANES), F32)] + [key_chain] * 5 + [val_chain] * 2
                        + [pltpu.VMEM((2, 2, chunk_seqs, MEM_HEADS, N_MEM, MEM_DH), F32),
                           pltpu.SemaphoreType.DMA((2,)), pltpu.SemaphoreType.DMA((2,))]),
        compiler_params=_params("arbitrary"),
    )(r, k, w, a, v, k_k, k_a, r_k, lnx_w, lnx_b, s0, mem_k, mem_v)


def _merge_body(*refs, part_counts):
    n = len(part_counts)
    x_refs, oret_refs, g_refs = refs[:n], refs[n:2 * n], refs[2 * n:3 * n]
    yt_ref, ys_ref, wt_ref, wb_ref, o_ref = refs[3 * n:]
    x = _read_row_parts(x_refs, part_counts)
    y = jnp.where(pl.program_id(0) >= part_counts[0], ys_ref[...], yt_ref[...].T)
    yb = (y * _read_row_parts(g_refs, part_counts)).astype(BF16)
    ob = _read_row_parts(oret_refs, part_counts).astype(BF16)
    n_chunk = 256
    for j in range(0, D_MODEL, n_chunk):
        acc = jnp.dot(ob, wt_ref[:, j:j + n_chunk], preferred_element_type=F32)
        acc = acc + jnp.dot(yb, wb_ref[:, j:j + n_chunk], preferred_element_type=F32)
        o_ref[:, j:j + n_chunk] = x[:, j:j + n_chunk] + acc


def _merge(x_parts, oret_parts, g_parts, y_first_t, y_second, w_ret, w_rwkv):
    m = sum(part.shape[0] for part in x_parts)
    in_specs, part_counts = [], None
    for parts in (x_parts, oret_parts, g_parts):
        specs, part_counts = _row_part_specs(parts, TM)
        in_specs += specs
    assert len(part_counts) == 2
    n_first = part_counts[0]
    tiles = y_first_t.shape[2] // TM
    assert y_first_t.shape[0] * tiles == n_first
    yt_spec = pl.BlockSpec((None, RWKV_W, TM),
                           lambda i: (jnp.minimum(i, n_first - 1) // tiles, 0, jnp.minimum(i, n_first - 1) % tiles))
    ys_spec = pl.BlockSpec((TM, RWKV_W), lambda i: (jnp.clip(i - n_first, 0, part_counts[1] - 1), 0))
    wspec = pl.BlockSpec((RET_W, D_MODEL), lambda i: (0, 0))
    return pl.pallas_call(
        functools.partial(_merge_body, part_counts=part_counts),
        grid=(m // TM,),
        in_specs=in_specs + [yt_spec, ys_spec, wspec, wspec],
        out_specs=pl.BlockSpec((TM, D_MODEL), lambda i: (i, 0)),
        out_shape=jax.ShapeDtypeStruct((m, D_MODEL), F32),
        compiler_params=_params("parallel"),
    )(*x_parts, *oret_parts, *g_parts, y_first_t, y_second, w_ret.astype(BF16), w_rwkv.astype(BF16))


def _mem_kv_body(x_ref, g_ref, wk_ref, wv_ref, k_ref, v_ref, kh_ref, vh_ref, *, n_seq):
    x = x_ref[...].astype(F32)
    xb = (x * lax.rsqrt(jnp.mean(x * x, axis=-1, keepdims=True) + EPS) * g_ref[...]).astype(BF16)
    for w_ref, o_ref, oh_ref in ((wk_ref, k_ref, kh_ref), (wv_ref, v_ref, vh_ref)):
        for h in range(MEM_HEADS):
            acc = jnp.dot(xb, w_ref[:, h * MEM_DH:(h + 1) * MEM_DH], preferred_element_type=F32)
            o_ref[:, h, :] = acc
            for s in range(n_seq):
                oh_ref[s, h] = acc[s * N_MEM:(s + 1) * N_MEM]


def _mem_kv(mem, gain, w_k, w_v):
    n_b = mem.shape[0]
    n_seq = TM // N_MEM
    assert n_seq * N_MEM == TM and n_b % n_seq == 0
    wspec = pl.BlockSpec((D_MODEL, D_MODEL), lambda i: (0, 0))
    tok_spec = pl.BlockSpec((TM, MEM_HEADS, MEM_DH), lambda i: (i, 0, 0))
    head_spec = pl.BlockSpec((n_seq, MEM_HEADS, N_MEM, MEM_DH), lambda i: (i, 0, 0, 0))
    tok_shape = jax.ShapeDtypeStruct((n_b * N_MEM, MEM_HEADS, MEM_DH), F32)
    head_shape = jax.ShapeDtypeStruct((n_b, MEM_HEADS, N_MEM, MEM_DH), F32)
    return pl.pallas_call(
        functools.partial(_mem_kv_body, n_seq=n_seq),
        grid=(n_b // n_seq,),
        in_specs=[pl.BlockSpec((TM, D_MODEL), lambda i: (i, 0)), pl.BlockSpec((1, D_MODEL), lambda i: (0, 0)),
                  wspec, wspec],
        out_specs=[tok_spec, tok_spec, head_spec, head_spec],
        out_shape=[tok_shape, tok_shape, head_shape, head_shape],
        compiler_params=_params("parallel"),
    )(mem.reshape(n_b * N_MEM, D_MODEL), gain.reshape(1, D_MODEL), w_k.astype(BF16), w_v.astype(BF16))


def _attn_body(q_ref, k_ref, v_ref, o_ref, *, n_seq, tq):
    nt = (((1,), (1,)), ((), ()))
    for g in range(n_seq):
        rows = slice(g * tq, (g + 1) * tq)
        q = q_ref[rows, :].astype(BF16)
        s = lax.dot_general(q, k_ref[g].astype(BF16), nt, preferred_element_type=F32) * (MEM_DH ** -0.5)
        p = jnp.exp(s - jnp.max(s, axis=-1, keepdims=True))
        l = jnp.sum(p, axis=-1, keepdims=True)
        o = jnp.dot(p.astype(BF16), v_ref[g].astype(BF16), preferred_element_type=F32)
        o_ref[rows, :] = o / l


def _attention(q, mem_k, mem_v, *, row0, n_batch, t, n_seq, tq):
    q_tiles = t // tq
    rows = n_seq * tq
    assert t % tq == 0 and n_batch % n_seq == 0 and row0 % rows == 0 and (n_seq == 1 or q_tiles == 1)
    blk0 = row0 // rows
    kv_spec = pl.BlockSpec((n_seq, None, N_MEM, MEM_DH), lambda b, h, qi: (b, h, 0, 0))
    return pl.pallas_call(
        functools.partial(_attn_body, n_seq=n_seq, tq=tq),
        grid=(n_batch // n_seq, MEM_HEADS, q_tiles),
        in_specs=[pl.BlockSpec((rows, MEM_DH), lambda b, h, qi: (blk0 + b * q_tiles + qi, h)), kv_spec, kv_spec],
        out_specs=pl.BlockSpec((rows, MEM_DH), lambda b, h, qi: (b * q_tiles + qi, h)),
        out_shape=jax.ShapeDtypeStruct((n_batch * t, D_MODEL), F32),
        compiler_params=_params("parallel", "parallel", "parallel"),
    )(q, mem_k, mem_v)


ROW_TILE = (D_MODEL // LANES, LANES)


def _rows_to_tiles(x):
    chunks = [x[:, j * LANES:(j + 1) * LANES] for j in range(ROW_TILE[0])]
    return jnp.swapaxes(jnp.stack(chunks, axis=0), 0, 1)


def _tiles_to_rows(x):
    chunks = jnp.swapaxes(x, 0, 1)
    return jnp.concatenate([chunks[j] for j in range(ROW_TILE[0])], axis=1)


def _router_body(h_ref, g_ref, w_ref, b_ref, hn_ref, ids_ref, comb_ref):
    x = h_ref[...]
    hn = x * lax.rsqrt(jnp.mean(x * x, axis=-1, keepdims=True) + EPS) * g_ref[...]
    hn_ref[...] = _rows_to_tiles(hn)
    logits = jnp.dot(hn, w_ref[...], precision=lax.Precision.HIGHEST, preferred_element_type=F32) + b_ref[...]
    lane = lax.broadcasted_iota(jnp.int32, logits.shape, 1).astype(F32)
    neg = -jnp.inf

    def first_argmax(vals):
        m = jnp.max(vals, axis=-1, keepdims=True)
        return m, jnp.min(jnp.where(vals == m, lane, float(LANES)), axis=-1, keepdims=True)

    gl = jnp.where(lane < N_GROUPS, logits, neg)
    gmax, gsel = first_argmax(gl)
    pg_sel = 1.0 / jnp.sum(jnp.exp(gl - gmax), axis=-1, keepdims=True)
    e0 = N_GROUPS + gsel * EXP_PER_GROUP
    el = jnp.where((lane >= e0) & (lane < e0 + EXP_PER_GROUP), logits, neg)
    m1, i1 = first_argmax(el)
    m2, i2 = first_argmax(jnp.where(lane == i1, neg, el))
    e21 = jnp.exp(m2 - m1)
    c1 = pg_sel / (1.0 + e21)
    c2 = c1 * e21
    ids = jnp.where(lane == 0, i1 - N_GROUPS, jnp.where(lane == 1, i2 - N_GROUPS, 0.0))
    ids_ref[...] = ids.astype(jnp.int32)
    comb_ref[...] = jnp.where(lane == 0, c1, jnp.where(lane == 1, c2, 0.0))


def _router(h, g_ffn, w_gr, b_gr, w_er, b_er):
    m = h.shape[0]
    pad = LANES - N_GROUPS - N_EXPERTS
    w = jnp.concatenate([w_gr, w_er, jnp.zeros((D_MODEL, pad), F32)], axis=1)
    b = jnp.concatenate([b_gr, b_er, jnp.zeros((pad,), F32)]).reshape(1, LANES)
    row = lambda n: pl.BlockSpec((TM, n), lambda i: (i, 0))
    return pl.pallas_call(
        _router_body,
        grid=(m // TM,),
        in_specs=[row(D_MODEL), pl.BlockSpec((1, D_MODEL), lambda i: (0, 0)),
                  pl.BlockSpec((D_MODEL, LANES), lambda i: (0, 0)), pl.BlockSpec((1, LANES), lambda i: (0, 0))],
        out_specs=[pl.BlockSpec((TM,) + ROW_TILE, lambda i: (i, 0, 0)), row(LANES), row(LANES)],
        out_shape=[jax.ShapeDtypeStruct((m,) + ROW_TILE, F32), jax.ShapeDtypeStruct((m, LANES), jnp.int32),
                   jax.ShapeDtypeStruct((m, LANES), F32)],
        compiler_params=_params("parallel"),
    )(h, g_ffn.reshape(1, D_MODEL), w, b)


def _dispatch_body(dest_ref, hn_ref, sorted_in, sorted_out, sem):
    del sorted_in

    def issue(r, c):
        for k in range(TOP_K):
            pltpu.make_async_copy(hn_ref.at[r], sorted_out.at[dest_ref[0, k, r]], sem.at[k]).start()
        return c
    lax.fori_loop(0, TM, issue, 0, unroll=8)
    for k in range(TOP_K):
        pltpu.make_async_copy(hn_ref, sorted_out.at[pl.ds(0, TM)], sem.at[k]).wait()


def _dispatch(hn, dest, n_sorted):
    n_tok = hn.shape[0]
    return pl.pallas_call(
        _dispatch_body,
        grid=(n_tok // TM,),
        in_specs=[pl.BlockSpec((1, TOP_K, TM), lambda i: (i, 0, 0), memory_space=pltpu.SMEM),
                  pl.BlockSpec((TM,) + ROW_TILE, lambda i: (i, 0, 0)),
                  pl.BlockSpec(memory_space=pl.ANY)],
        out_specs=pl.BlockSpec(memory_space=pl.ANY),
        out_shape=jax.ShapeDtypeStruct((n_sorted,) + ROW_TILE, F32),
        scratch_shapes=[pltpu.SemaphoreType.DMA((TOP_K,))],
        input_output_aliases={2: 0},
        compiler_params=_params("arbitrary"),
    )(dest, hn, jnp.zeros((n_sorted,) + ROW_TILE, F32))


def _expert_body(blk_e_ref, x_ref, wg_ref, wu_ref, wd_ref, o_ref):
    del blk_e_ref
    x = _tiles_to_rows(x_ref[...]).astype(BF16)
    hg = jnp.dot(x, wg_ref[0].astype(BF16), preferred_element_type=F32)
    hu = jnp.dot(x, wu_ref[0].astype(BF16), preferred_element_type=F32)
    act = (hg * jax.nn.sigmoid(hg) * hu).astype(BF16)
    o_ref[...] = _rows_to_tiles(jnp.dot(act, wd_ref[0].astype(BF16), preferred_element_type=F32))


def _experts(x_sorted, blk_e, w_gate, w_up, w_down):
    n_blocks = blk_e.shape[0]
    row_spec = pl.BlockSpec((MOE_ROWS,) + ROW_TILE, lambda i, be: (i, 0, 0))
    grid_spec = pltpu.PrefetchScalarGridSpec(
        num_scalar_prefetch=1,
        grid=(n_blocks,),
        in_specs=[
            row_spec,
            pl.BlockSpec((1, D_MODEL, D_EXPERT), lambda i, be: (be[i], 0, 0)),
            pl.BlockSpec((1, D_MODEL, D_EXPERT), lambda i, be: (be[i], 0, 0)),
            pl.BlockSpec((1, D_EXPERT, D_MODEL), lambda i, be: (be[i], 0, 0)),
        ],
        out_specs=row_spec,
    )
    return pl.pallas_call(
        _expert_body,
        grid_spec=grid_spec,
        out_shape=jax.ShapeDtypeStruct(x_sorted.shape, F32),
        compiler_params=_params("arbitrary"),
    )(blk_e, x_sorted, w_gate, w_up, w_down)


def _route_plan(ids):
    n_tok = ids.shape[0]
    n_pairs = ids.size
    n_blocks = -(-(n_pairs + N_EXPERTS * (MOE_ROWS - 1)) // MOE_ROWS)
    flat_e = ids.reshape(n_pairs)
    onehot = (flat_e[:, None] == jnp.arange(N_EXPERTS, dtype=jnp.int32)[None, :]).astype(jnp.int32)
    csum = jnp.cumsum(onehot, axis=0)
    rank = jnp.sum(onehot * csum, axis=1) - 1
    counts = csum[-1]
    pcounts = (counts + MOE_ROWS - 1) // MOE_ROWS * MOE_ROWS
    pends = jnp.cumsum(pcounts)
    pstarts = pends - pcounts
    dest = jnp.sum(onehot * pstarts[None, :], axis=1) + rank
    block_start = jnp.arange(n_blocks, dtype=jnp.int32) * MOE_ROWS
    blk_e = jnp.minimum(jnp.sum((block_start[:, None] >= pends[None, :]).astype(jnp.int32), axis=1),
                        N_EXPERTS - 1).astype(jnp.int32)
    dest = dest.astype(jnp.int32).reshape(n_tok // TM, TM, TOP_K).transpose(0, 2, 1)
    return blk_e, dest, n_blocks * MOE_ROWS


def _final_body(dest_ref, dest_next_ref, h_ref, comb_ref, g_ref, y_hbm, o_ref, ybuf, sem):
    i = pl.program_id(0)
    n = pl.num_programs(0)
    slot = i % 2

    def start_gather(ref, sl):
        def issue(r, c):
            for k in range(TOP_K):
                pltpu.make_async_copy(y_hbm.at[ref[0, k, r]], ybuf.at[sl, k, r], sem.at[sl]).start()
            return c
        lax.fori_loop(0, TM, issue, 0, unroll=8)

    @pl.when(i == 0)
    def _():
        start_gather(dest_ref, 0)

    @pl.when(i + 1 < n)
    def _():
        start_gather(dest_next_ref, 1 - slot)

    for k in range(TOP_K):
        pltpu.make_async_copy(y_hbm.at[pl.ds(0, TM)], ybuf.at[slot, k], sem.at[slot]).wait()

    first, second = _tiles_to_rows(ybuf[slot, 0]), _tiles_to_rows(ybuf[slot, 1])
    x = h_ref[...] + (first * comb_ref[:, 0:1] + second * comb_ref[:, 1:2])
    o_ref[...] = x * lax.rsqrt(jnp.mean(x * x, axis=-1, keepdims=True) + EPS) * g_ref[...]


def _final(h, y_sorted, dest, comb, g_final, *, row0, n_rows):
    assert row0 % TM == 0 and n_rows % TM == 0
    blk0, n_steps = row0 // TM, n_rows // TM
    row = lambda n: pl.BlockSpec((TM, n), lambda i: (blk0 + i, 0))
    dest_spec = lambda f: pl.BlockSpec((1, TOP_K, TM), f, memory_space=pltpu.SMEM)
    return pl.pallas_call(
        _final_body,
        grid=(n_steps,),
        in_specs=[dest_spec(lambda i: (blk0 + i, 0, 0)),
                  dest_spec(lambda i: (blk0 + jnp.minimum(i + 1, n_steps - 1), 0, 0)),
                  row(D_MODEL), row(LANES), pl.BlockSpec((1, D_MODEL), lambda i: (0, 0)),
                  pl.BlockSpec(memory_space=pl.ANY)],
        out_specs=pl.BlockSpec((TM, D_MODEL), lambda i: (i, 0)),
        out_shape=jax.ShapeDtypeStruct((n_rows, D_MODEL), F32),
        scratch_shapes=[pltpu.VMEM((2, TOP_K, TM) + ROW_TILE, F32), pltpu.SemaphoreType.DMA((2,))],
        compiler_params=_params("arbitrary"),
    )(dest, dest, h, comb, g_final.reshape(1, D_MODEL), y_sorted)


def _reorder_last(x, shape, order):
    lead = x.shape[:-1]
    n = len(lead)
    y = x.reshape(lead + shape).transpose(tuple(range(n)) + tuple(n + o for o in order))
    return y.reshape(lead + (x.shape[-1],))


HALF_N = RWKV_N // 2


def _key_major(x):
    return _reorder_last(x, (RWKV_HEADS, RWKV_N), (1, 0))


def _key_major_inv(x):
    return _reorder_last(x, (RWKV_N, RWKV_HEADS), (1, 0))


def _value_major(x):
    return _reorder_last(x, (RWKV_HEADS, 2, HALF_N), (2, 1, 0))


def _value_major_inv(x):
    return _reorder_last(x, (HALF_N, 2, RWKV_HEADS), (2, 1, 0))


def _rwkv_cols(x, key_fn, value_fn):
    return jnp.concatenate([key_fn(x[..., :RWKV_W]), key_fn(x[..., RWKV_W:2 * RWKV_W]),
                            value_fn(x[..., 2 * RWKV_W:3 * RWKV_W]), x[..., 3 * RWKV_W:]], axis=-1)


def _sample_key_layout(x, b, t):
    return x.reshape(b, t, RWKV_N, RWKV_HEADS).transpose(3, 1, 2, 0)


def _sample_value_layout(x, b, t):
    return x.reshape(b, t, HALF_N, 2, RWKV_HEADS).transpose(4, 1, 3, 2, 0).reshape(RWKV_HEADS, t, RWKV_N, b)


def _sample_value_unlayout(y, b, t):
    return y.reshape(RWKV_HEADS, t, 2, HALF_N, b).transpose(4, 1, 3, 2, 0).reshape(b * t, RWKV_W)


def kernel(x_prompt, x_sample, mem_prompt, state_ret, state_rwkv, state_shift, cache_mem_k, cache_mem_v,
           g_mix, w_in, ret_gn, rwkv_mu, rwkv_w0, rwkv_w2, rwkv_a0, rwkv_a2, rwkv_g2, rwkv_k_k, rwkv_k_a,
           rwkv_r_k, rwkv_lnx_w, rwkv_lnx_b, w_out, g_mem_q, g_mem_kv, w_mq, w_mk, w_mv, w_mo, g_ffn,
           w_group_router, b_group_router, w_expert_router, b_expert_router, w_e_gate, w_e_up, w_e_down,
           g_final):
    assert w_in.shape[0] == 1, "single-layer decoder"
    bp, tp, d = x_prompt.shape
    bs, ts, _ = x_sample.shape
    np_tok, ns_tok = bp * tp, bs * ts
    assert d == D_MODEL and bp * RWKV_HEADS * 2 == LANES and bs == LANES
    l = 0
    x_parts = [x_prompt.reshape(np_tok, d), x_sample.reshape(ns_tok, d)]

    w_in_l = jnp.concatenate([w_in[l][:, :N_RET_COLS], _rwkv_cols(w_in[l][:, N_RET_COLS:], _key_major, _value_major)],
                             axis=1)
    (proj,) = _matmul(x_parts, [w_in_l.astype(BF16)], gain=g_mix[l])

    pos_p = np.arange(tp)
    pos_s = PAST_LEN + np.arange(ts)
    zero_ret = jnp.zeros((bp, RET_HEADS, RET_DK, RET_DV), F32)
    oret_p, sret_p = _retention(proj, zero_ret, ret_gn[l], pos_p, row0=0, n_batch=bp, t=tp, n_seq=1)
    oret_s, sret_s = _retention(proj, state_ret[l], ret_gn[l], pos_s, row0=np_tok, n_batch=bs, t=ts, n_seq=16)

    pre_w = (_rwkv_cols(rwkv_mu[l], _key_major, _value_major), _key_major(rwkv_w0[l]), _key_major(rwkv_w2[l]),
             _key_major(rwkv_a0[l]), _key_major(rwkv_a2[l]), _value_major(rwkv_g2[l]))
    zero_shift = jnp.zeros((bp, N_RWKV_COLS), F32)
    shift_in = _rwkv_cols(state_shift[l], _key_major, _value_major)
    r_p, k_p, v_p, w_p, a_p, gate_p = _rwkv_pre(proj, zero_shift, *pre_w, row0=0, n_batch=bp, t=tp,
                                                 n_seq=1, c=256, transposed=True)
    r_s, k_s, v_s, w_s, a_s, gate_s = _rwkv_pre(proj, shift_in, *pre_w, row0=np_tok, n_batch=bs, t=ts,
                                                 n_seq=16, c=ts, transposed=False)

    kvec = lambda v: v.reshape(RWKV_HEADS, RWKV_N)
    key_par = lambda v: jnp.broadcast_to(kvec(v).T[:, None, None, :], (RWKV_N, 2, bp, RWKV_HEADS)).reshape(
        RWKV_N, LANES)

    val_par = lambda v: jnp.broadcast_to(
        v.reshape(RWKV_HEADS, 2, HALF_N).transpose(2, 1, 0)[:, :, None, :],
        (HALF_N, 2, bp, RWKV_HEADS)).reshape(HALF_N, LANES)
    mem_shape = (N_MEM, MEM_HEADS, MEM_DH)
    y_p, srw_p, cache_k_heads, cache_v_heads = _rwkv_scan_prompt(
        r_p, k_p, w_p, a_p, v_p, key_par(rwkv_k_k[l]), key_par(rwkv_k_a[l]), key_par(rwkv_r_k[l]),
        val_par(rwkv_lnx_w[l]), val_par(rwkv_lnx_b[l]), jnp.zeros((HALF_N, RWKV_N, LANES), F32),
        cache_mem_k.reshape(bs, *mem_shape), cache_mem_v.reshape(bs, *mem_shape))
    srw_p = srw_p.reshape(HALF_N, RWKV_N, 2, bp, RWKV_HEADS).transpose(3, 4, 2, 0, 1).reshape(
        bp, RWKV_HEADS, RWKV_N, RWKV_N)
    head_par = lambda v: jnp.broadcast_to(kvec(v)[:, :, None], (RWKV_HEADS, RWKV_N, LANES))
    y_s, srw_s = _rwkv_scan(
        _sample_key_layout(r_s, bs, ts), _sample_key_layout(k_s, bs, ts), _sample_key_layout(w_s, bs, ts),
        _sample_key_layout(a_s, bs, ts), _sample_value_layout(v_s, bs, ts),
        head_par(rwkv_k_k[l]), head_par(rwkv_k_a[l]), head_par(rwkv_r_k[l]),
        head_par(rwkv_lnx_w[l]), head_par(rwkv_lnx_b[l]),
        state_rwkv[l].astype(F32).transpose(1, 2, 3, 0), tc=ts, halves=1)
    y_s = _sample_value_unlayout(y_s, bs, ts)
    srw_s = srw_s.transpose(3, 0, 1, 2)

    w_rwkv_out = _value_major(w_out[l][RET_W:].T).T
    h = _merge(x_parts, [oret_p, oret_s], [gate_p, gate_s], y_p, y_s, w_out[l][:RET_W], w_rwkv_out)

    mk, mv, mk_heads, mv_heads = _mem_kv(mem_prompt, g_mem_kv[l], w_mk[l], w_mv[l])
    (q,) = _matmul([h], [w_mq[l].astype(BF16)], gain=g_mem_q[l])
    att_p = _attention(q, mk_heads, mv_heads, row0=0, n_batch=bp, t=tp, n_seq=1, tq=TM)
    att_s = _attention(q, cache_k_heads, cache_v_heads, row0=np_tok, n_batch=bs, t=ts, n_seq=16, tq=ts)
    (h,) = _matmul([att_p, att_s], [w_mo[l].astype(BF16)], residual=h)

    hn, ids, comb = _router(h, g_ffn[l], w_group_router[l], b_group_router[l], w_expert_router[l],
                            b_expert_router[l])
    blk_e, dest, n_sorted = _route_plan(ids[:, :TOP_K])
    y_sorted = _experts(_dispatch(hn, dest, n_sorted), blk_e, w_e_gate[l], w_e_up[l], w_e_down[l])
    y = _final(h, y_sorted, dest, comb, g_final, row0=0, n_rows=np_tok + ns_tok)
    y_prompt = y[:np_tok].reshape(bp, tp, d)
    y_sample = y[np_tok:].reshape(bs, ts, d)

    shift_p = lax.slice(proj, (tp - 1, N_RET_COLS), (np_tok, N_IN_COLS), (tp, 1))
    shift_s = lax.slice(proj, (np_tok + ts - 1, N_RET_COLS), (np_tok + ns_tok, N_IN_COLS), (ts, 1))
    shift_p = _rwkv_cols(shift_p, _key_major_inv, _value_major_inv)
    shift_s = _rwkv_cols(shift_s, _key_major_inv, _value_major_inv)
    return (y_prompt, y_sample, sret_p[None], srw_p[None], shift_p[None],
            mk.reshape(1, bp, *mem_shape), mv.reshape(1, bp, *mem_shape),
            sret_s[None], srw_s[None], shift_s[None])
```

```python
import functools

import numpy as np
import jax
import jax.numpy as jnp
from jax import lax
from jax.experimental import pallas as pl
from jax.experimental.pallas import tpu as pltpu

F32 = jnp.float32
BF16 = jnp.bfloat16

D_MODEL = 1024
PAST_LEN = 16384
N_MEM = 256
MEM_HEADS = 4
MEM_DH = D_MODEL // MEM_HEADS
RET_HEADS = 4
RET_W = D_MODEL // 2
RET_DV = RET_W // RET_HEADS
RET_DK = RET_DV // 2
RET_QK = RET_HEADS * RET_DK
RET_CHUNK = 128
ROPE_BASE = 10000.0
RWKV_N = 64
RWKV_W = D_MODEL - RET_W
RWKV_HEADS = RWKV_W // RWKV_N
LORA_W = 64
LORA_A = 64
LORA_G = 128
LNX_EPS = 64e-5
N_RET_COLS = 2 * RET_QK + 2 * RET_W
N_RWKV_COLS = 3 * RWKV_W + LORA_W + LORA_A + LORA_G
N_IN_COLS = N_RET_COLS + N_RWKV_COLS
N_GROUPS = 4
EXP_PER_GROUP = 8
N_EXPERTS = N_GROUPS * EXP_PER_GROUP
TOP_K = 2
D_EXPERT = D_MODEL // 2
EPS = 1e-6

LANES = 128
MOE_ROWS = 256
TM = 512


def _params(*sem):
    return pltpu.CompilerParams(dimension_semantics=sem)


def _row_part_specs(parts, tm):
    specs, counts, start = [], [], 0
    for part in parts:
        nb = part.shape[0] // tm
        assert nb * tm == part.shape[0]
        specs.append(pl.BlockSpec((tm, part.shape[1]), lambda i, s=start, n=nb: (jnp.clip(i - s, 0, n - 1), 0)))
        counts.append(nb)
        start += nb
    return specs, counts


def _read_row_parts(refs, counts):
    i = pl.program_id(0)
    x = refs[0][...]
    start = counts[0]
    for ref, nb in zip(refs[1:], counts[1:]):
        x = jnp.where(i >= start, ref[...], x)
        start += nb
    return x


def _mm_body(*refs, part_counts, n_w, has_gain, has_res, n_chunk):
    it = iter(refs)
    x_refs = [next(it) for _ in part_counts]
    g_ref = next(it) if has_gain else None
    w_refs = [next(it) for _ in range(n_w)]
    r_ref = next(it) if has_res else None
    o_refs = [next(it) for _ in range(n_w)]
    x = _read_row_parts(x_refs, part_counts).astype(F32)
    if has_gain:
        x = x * lax.rsqrt(jnp.mean(x * x, axis=-1, keepdims=True) + EPS) * g_ref[...]
    xb = x.astype(BF16)
    for w_ref, o_ref in zip(w_refs, o_refs):
        for j in range(0, w_ref.shape[1], n_chunk):
            acc = jnp.dot(xb, w_ref[:, j:j + n_chunk], preferred_element_type=F32)
            if has_res:
                acc = acc + r_ref[:, j:j + n_chunk]
            o_ref[:, j:j + n_chunk] = acc


def _matmul(x_parts, ws, *, gain=None, residual=None, tm=TM):
    k = x_parts[0].shape[1]
    m = sum(part.shape[0] for part in x_parts)
    n_chunk = 256
    assert all(w.shape[1] % n_chunk == 0 for w in ws)
    assert residual is None or len(ws) == 1
    in_specs, part_counts = _row_part_specs(x_parts, tm)
    args = list(x_parts)
    if gain is not None:
        in_specs.append(pl.BlockSpec((1, k), lambda i: (0, 0)))
        args.append(gain.reshape(1, k).astype(F32))
    for w in ws:
        in_specs.append(pl.BlockSpec(w.shape, lambda i: (0, 0)))
        args.append(w)
    if residual is not None:
        in_specs.append(pl.BlockSpec((tm, ws[0].shape[1]), lambda i: (i, 0)))
        args.append(residual)
    return pl.pallas_call(
        functools.partial(_mm_body, part_counts=part_counts, n_w=len(ws), has_gain=gain is not None,
                          has_res=residual is not None, n_chunk=n_chunk),
        grid=(m // tm,),
        in_specs=in_specs,
        out_specs=[pl.BlockSpec((tm, w.shape[1]), lambda i: (i, 0)) for w in ws],
        out_shape=[jax.ShapeDtypeStruct((m, w.shape[1]), F32) for w in ws],
        compiler_params=_params("parallel"),
    )(*args)


def _rot_tables(pos):
    half = RET_DK // 2
    inv_freq = ROPE_BASE ** (-(np.arange(half, dtype=np.float64) / half))
    ang = pos.astype(np.float64)[:, None] * inv_freq[None, :]
    cos, sin = np.cos(ang), np.sin(ang)
    zero = np.zeros_like(sin)
    c = np.tile(np.concatenate([cos, cos], axis=1), (1, RET_HEADS))
    s_lo = np.tile(np.concatenate([-sin, zero], axis=1), (1, RET_HEADS))
    s_hi = np.tile(np.concatenate([zero, sin], axis=1), (1, RET_HEADS))
    return [jnp.asarray(t, F32) for t in (c, s_lo, s_hi)]


def _ret_decay_tables(c):
    lg = np.log1p(-np.exp2(-5.0 - np.arange(RET_HEADS, dtype=np.float64)))
    idx = np.arange(c, dtype=np.float64)
    diff = idx[:, None] - idx[None, :]
    mask = np.where(diff[None] >= 0, np.exp(np.maximum(diff, 0.0)[None] * lg[:, None, None]), 0.0)
    q_dec = np.repeat(np.exp((idx[:, None] + 1.0) * lg[None, :]), RET_DV, axis=1)
    k_dec = np.repeat(np.exp((c - 1.0 - idx)[:, None] * lg[None, :]), RET_DK, axis=1)
    c_dec = [float(v) for v in np.exp(c * lg)]
    return jnp.asarray(mask, F32), jnp.asarray(q_dec, F32), jnp.asarray(k_dec, F32), c_dec


def _ret_body(q_ref, k_ref, v_ref, gate_ref, c_ref, slo_ref, shi_ref, mask_ref, qdec_ref, kdec_ref,
              gn_ref, s0_ref, o_ref, sout_ref, s_scr, *, n_seq, c, c_dec):
    ci = pl.program_id(1)

    @pl.when(ci == 0)
    def _():
        s_scr[...] = s0_ref[...].astype(F32)

    cos, s_lo, s_hi = c_ref[...], slo_ref[...], shi_ref[...]
    half = RET_DK // 2

    def rope(x):
        return x * cos + pltpu.roll(x, RET_QK - half, 1) * s_lo + pltpu.roll(x, half, 1) * s_hi

    nt = (((1,), (1,)), ((), ()))
    tn = (((0,), (0,)), ((), ()))
    for g in range(n_seq):
        rows = slice(g * c, (g + 1) * c)
        q = rope(q_ref[rows, :].astype(F32))
        k = rope(k_ref[rows, :].astype(F32)) * (RET_DK ** -0.5)
        k_st = k * kdec_ref[...]
        for h in range(RET_HEADS):
            kc = slice(h * RET_DK, (h + 1) * RET_DK)
            vc = slice(h * RET_DV, (h + 1) * RET_DV)
            qh = q[:, kc].astype(BF16)
            vh = v_ref[rows, vc].astype(BF16)
            s_h = s_scr[g, h]
            att = lax.dot_general(qh, k[:, kc].astype(BF16), nt, preferred_element_type=F32) * mask_ref[h]
            o = jnp.dot(att.astype(BF16), vh, preferred_element_type=F32)
            o = o + jnp.dot(qh, s_h.astype(BF16), preferred_element_type=F32) * qdec_ref[:, vc]
            s_scr[g, h] = s_h * c_dec[h] + lax.dot_general(
                k_st[:, kc].astype(BF16), vh, tn, preferred_element_type=F32)
            o = o * lax.rsqrt(jnp.mean(o * o, axis=-1, keepdims=True) + EPS)
            gate = gate_ref[rows, vc].astype(F32)
            o_ref[rows, vc] = o * gn_ref[:, vc] * (gate * jax.nn.sigmoid(gate))

    @pl.when(ci == pl.num_programs(1) - 1)
    def _():
        sout_ref[...] = s_scr[...]


def _retention(proj, s0, ret_gn, pos, *, row0, n_batch, t, n_seq):
    c = RET_CHUNK if t % RET_CHUNK == 0 else t
    n_chunks = t // c
    rows = n_seq * c
    assert n_batch % n_seq == 0 and row0 % rows == 0 and (n_seq == 1 or n_chunks == 1)
    blk0 = row0 // rows
    mask, q_dec, k_dec, c_dec = _ret_decay_tables(c)
    cos, s_lo, s_hi = _rot_tables(pos)

    def row_map(col):
        return lambda b, ci: (blk0 + b * n_chunks + ci, col)

    def const2(b, ci):
        return (0, 0)

    state_spec = pl.BlockSpec((n_seq, RET_HEADS, RET_DK, RET_DV), lambda b, ci: (b, 0, 0, 0))
    in_specs = [
        pl.BlockSpec((rows, RET_QK), row_map(0)),
        pl.BlockSpec((rows, RET_QK), row_map(1)),
        pl.BlockSpec((rows, RET_W), row_map(1)),
        pl.BlockSpec((rows, RET_W), row_map(2)),
        pl.BlockSpec((c, RET_QK), lambda b, ci: (ci, 0)),
        pl.BlockSpec((c, RET_QK), lambda b, ci: (ci, 0)),
        pl.BlockSpec((c, RET_QK), lambda b, ci: (ci, 0)),
        pl.BlockSpec((RET_HEADS, c, c), lambda b, ci: (0, 0, 0)),
        pl.BlockSpec((c, RET_W), const2),
        pl.BlockSpec((c, RET_QK), const2),
        pl.BlockSpec((1, RET_W), const2),
        state_spec,
    ]
    return pl.pallas_call(
        functools.partial(_ret_body, n_seq=n_seq, c=c, c_dec=c_dec),
        grid=(n_batch // n_seq, n_chunks),
        in_specs=in_specs,
        out_specs=[pl.BlockSpec((rows, RET_W), lambda b, ci: (b * n_chunks + ci, 0)), state_spec],
        out_shape=[jax.ShapeDtypeStruct((n_batch * t, RET_W), F32),
                   jax.ShapeDtypeStruct((n_batch, RET_HEADS, RET_DK, RET_DV), F32)],
        scratch_shapes=[pltpu.VMEM((n_seq, RET_HEADS, RET_DK, RET_DV), F32)],
        compiler_params=_params("parallel", "arbitrary"),
    )(proj, proj, proj, proj, cos, s_lo, s_hi, mask, q_dec, k_dec, ret_gn.reshape(1, RET_W).astype(F32), s0)


def _rwkv_pre_body(r_ref, k_ref, v_ref, lo_ref, shift_ref, mu_ref, w0_ref, w2_ref, a0_ref, a2_ref, g2_ref,
                   ro_ref, ko_ref, vo_ref, wo_ref, ao_ref, go_ref, prev_scr, *, n_seq, c, transposed):
    ci = pl.program_id(1)

    @pl.when(ci == 0)
    def _():
        for g in range(n_seq):
            prev_scr[g] = shift_ref[g].astype(F32)

    first_row = lax.broadcasted_iota(jnp.int32, (c, 1), 0) == 0

    def shifted(x_ref, g, col0):
        w = x_ref.shape[1]
        x = x_ref[g * c:(g + 1) * c, :].astype(F32)
        prev_row = prev_scr[g, :, col0:col0 + w]
        prev = jnp.where(first_row, prev_row, pltpu.roll(x, 1, 0))
        prev_scr[g, :, col0:col0 + w] = x[c - 1:c, :]
        return x + (prev - x) * mu_ref[:, col0:col0 + w]

    def put(o_ref, g, val):
        if transposed:
            o_ref[...] = val.T
        else:
            o_ref[g * c:(g + 1) * c, :] = val

    for g in range(n_seq):
        put(ro_ref, g, shifted(r_ref, g, 0))
        put(ko_ref, g, shifted(k_ref, g, RWKV_W))
        put(vo_ref, g, shifted(v_ref, g, 2 * RWKV_W))
        lo = shifted(lo_ref, g, 3 * RWKV_W)
        hw = lo[:, :LORA_W]
        ha = lo[:, LORA_W:LORA_W + LORA_A]
        hg = lo[:, LORA_W + LORA_A:]
        u = w0_ref[...] + jnp.dot(jnp.tanh(hw).astype(BF16), w2_ref[...], preferred_element_type=F32)
        put(wo_ref, g, jnp.exp(-float(np.exp(-0.5)) * jax.nn.sigmoid(u)))
        put(ao_ref, g, jax.nn.sigmoid(
            a0_ref[...] + jnp.dot(ha.astype(BF16), a2_ref[...], preferred_element_type=F32)))
        go_ref[g * c:(g + 1) * c, :] = jnp.dot(jax.nn.sigmoid(hg).astype(BF16), g2_ref[...],
                                                preferred_element_type=F32)


def _rwkv_pre(proj, s_shift, mu, w0, w2, a0, a2, g2, *, row0, n_batch, t, n_seq, c, transposed):
    n_chunks = t // c
    rows = n_seq * c
    assert t % c == 0 and n_batch % n_seq == 0 and row0 % rows == 0 and (n_seq == 1 or n_chunks == 1)
    assert not transposed or n_seq == 1
    blk0 = row0 // rows
    col_r = N_RET_COLS // RWKV_W
    lo_w = LORA_W + LORA_A + LORA_G
    col_lo = (N_RET_COLS + 3 * RWKV_W) // lo_w
    assert col_r * RWKV_W == N_RET_COLS and col_lo * lo_w == N_RET_COLS + 3 * RWKV_W

    def row_map(col):
        return lambda b, ci: (blk0 + b * n_chunks + ci, col)

    def const2(b, ci):
        return (0, 0)

    in_specs = [
        pl.BlockSpec((rows, RWKV_W), row_map(col_r)),
        pl.BlockSpec((rows, RWKV_W), row_map(col_r + 1)),
        pl.BlockSpec((rows, RWKV_W), row_map(col_r + 2)),
        pl.BlockSpec((rows, lo_w), row_map(col_lo)),
        pl.BlockSpec((n_seq, 1, N_RWKV_COLS), lambda b, ci: (b, 0, 0)),
        pl.BlockSpec((1, N_RWKV_COLS), const2),
        pl.BlockSpec((1, RWKV_W), const2),
        pl.BlockSpec((LORA_W, RWKV_W), const2),
        pl.BlockSpec((1, RWKV_W), const2),
        pl.BlockSpec((LORA_A, RWKV_W), const2),
        pl.BlockSpec((LORA_G, RWKV_W), const2),
    ]
    nat_spec = pl.BlockSpec((rows, RWKV_W), lambda b, ci: (b * n_chunks + ci, 0))
    nat_shape = jax.ShapeDtypeStruct((n_batch * t, RWKV_W), F32)
    if transposed:
        vec_spec = pl.BlockSpec((None, RWKV_W, c), lambda b, ci: (b, 0, ci))
        vec_shape = jax.ShapeDtypeStruct((n_batch, RWKV_W, t), F32)
    else:
        vec_spec, vec_shape = nat_spec, nat_shape
    return pl.pallas_call(
        functools.partial(_rwkv_pre_body, n_seq=n_seq, c=c, transposed=transposed),
        grid=(n_batch // n_seq, n_chunks),
        in_specs=in_specs,
        out_specs=[vec_spec] * 5 + [nat_spec],
        out_shape=[vec_shape] * 5 + [nat_shape],
        scratch_shapes=[pltpu.VMEM((n_seq, 1, N_RWKV_COLS), F32)],
        compiler_params=_params("parallel", "arbitrary"),
    )(proj, proj, proj, proj, s_shift.reshape(n_batch, 1, N_RWKV_COLS),
      mu.reshape(1, -1), w0.reshape(1, -1), w2.astype(BF16), a0.reshape(1, -1),
      a2.astype(BF16), g2.astype(BF16))


def _scan_body(r_ref, k_ref, w_ref, a_ref, v_ref, kk_ref, ka_ref, rk_ref, lw_ref, lb_ref, s0_ref,
               y_ref, sout_ref, s_scr, a_scr, b_scr, km_scr, *, tc, vr, halves):
    ci = pl.program_id(1)

    @pl.when(ci == 0)
    def _():
        s_scr[...] = s0_ref[...].astype(F32)

    def ksum(x):
        return jnp.sum(x, axis=-2, keepdims=True)

    def vsum(x):
        if halves == 2:
            x2 = x.reshape(tc * vr, LANES)
            x = (x2 + pltpu.roll(x2, LANES // 2, 1)).reshape(tc, vr, LANES)
        return jnp.sum(x, axis=1, keepdims=True)

    kr = k_ref[...]
    a = a_ref[...]
    kk = kr * kk_ref[...]
    kk = kk / jnp.maximum(jnp.sqrt(ksum(kk * kk)), 1e-12)
    a_scr[...] = -kk
    b_scr[...] = kk * a
    km_scr[...] = kr * (1.0 + (a - 1.0) * ka_ref[...])

    def token(t, carry):
        r, w, avec, bvec, kmod = r_ref[t], w_ref[t], a_scr[t], b_scr[t], km_scr[t]

        def value_row(i, c2):
            s = s_scr[i]
            sa = ksum(s * avec)
            s = s * w + sa * bvec + v_ref[t, pl.ds(i, 1), :] * kmod
            s_scr[i] = s
            y_ref[t, pl.ds(i, 1), :] = ksum(s * r)
            return c2

        lax.fori_loop(0, vr, value_row, 0, unroll=16)
        return carry

    lax.fori_loop(0, tc, token, 0)

    y = y_ref[...]
    d = y - vsum(y) * (1.0 / RWKV_N)
    var = vsum(d * d) * (1.0 / RWKV_N)
    bonus = ksum(r_ref[...] * km_scr[...] * rk_ref[...])
    y_ref[...] = d * lax.rsqrt(var + LNX_EPS) * lw_ref[...] + lb_ref[...] + bonus * v_ref[...]

    @pl.when(ci == pl.num_programs(1) - 1)
    def _():
        sout_ref[...] = s_scr[...]


def _rwkv_scan(r, k, w, a, v, k_k, k_a, r_k, lnx_w, lnx_b, s0, *, tc, halves):
    n_grp, t, _, lanes = r.shape
    vr = v.shape[2]
    assert lanes == LANES and t % tc == 0 and vr * halves == RWKV_N

    def tok_spec(rows):
        return pl.BlockSpec((None, tc, rows, LANES), lambda g, ci: (g, ci, 0, 0))

    def par_spec(rows):
        return pl.BlockSpec((None, rows, LANES), lambda g, ci: (g, 0, 0))

    st_spec = pl.BlockSpec((None, vr, RWKV_N, LANES), lambda g, ci: (g, 0, 0, 0))
    key_scratch = pltpu.VMEM((tc, RWKV_N, LANES), F32)
    return pl.pallas_call(
        functools.partial(_scan_body, tc=tc, vr=vr, halves=halves),
        grid=(n_grp, t // tc),
        in_specs=[tok_spec(RWKV_N)] * 4 + [tok_spec(vr)] + [par_spec(RWKV_N)] * 3 + [par_spec(vr)] * 2 + [st_spec],
        out_specs=[tok_spec(vr), st_spec],
        out_shape=[jax.ShapeDtypeStruct((n_grp, t, vr, LANES), F32),
                   jax.ShapeDtypeStruct((n_grp, vr, RWKV_N, LANES), F32)],
        scratch_shapes=[pltpu.VMEM((vr, RWKV_N, LANES), F32), key_scratch, key_scratch, key_scratch],
        compiler_params=_params("parallel", "arbitrary"),
    )(r, k, w, a, v, k_k, k_a, r_k, lnx_w, lnx_b, s0)


SCAN_TC = 128
SCAN_SUB = 64
MEM_CHUNKS = 4


def _scan_prompt_body(r_ref, k_ref, w_ref, a_ref, v_ref, kk_ref, ka_ref, rk_ref, lw_ref, lb_ref, s0_ref,
                      memk_hbm, memv_hbm, y_ref, sout_ref, memk_out, memv_out,
                      s_scr, r_c, w_c, a_c, b_c, km_c, v_c, y_c, mem_stage, mem_in_sem, mem_out_sem,
                      *, n_b, mem_seqs):
    ci = pl.program_id(0)
    vr = RWKV_N // 2
    ts = SCAN_SUB
    tile = RWKV_HEADS
    half_lanes = LANES // 2

    @pl.when(ci == 0)
    def _():
        s_scr[...] = s0_ref[...].astype(F32)

    chunk_seqs = mem_seqs // MEM_CHUNKS

    def mem_in(chunk, slot):
        copies = []
        for j in range(chunk_seqs):
            seq = (ci * MEM_CHUNKS + chunk) * chunk_seqs + j
            for h in range(MEM_HEADS):
                copies.append(pltpu.make_async_copy(memk_hbm.at[seq, :, h, :], mem_stage.at[slot, 0, j, h],
                                                    mem_in_sem.at[slot]))
                copies.append(pltpu.make_async_copy(memv_hbm.at[seq, :, h, :], mem_stage.at[slot, 1, j, h],
                                                    mem_in_sem.at[slot]))
        return copies

    def mem_out(chunk, slot):
        seqs = pl.ds((ci * MEM_CHUNKS + chunk) * chunk_seqs, chunk_seqs)
        return [pltpu.make_async_copy(mem_stage.at[slot, 0], memk_out.at[seqs], mem_out_sem.at[slot]),
                pltpu.make_async_copy(mem_stage.at[slot, 1], memv_out.at[seqs], mem_out_sem.at[slot])]

    def mem_phase(p):
        if 1 <= p <= MEM_CHUNKS:
            for cp in mem_in(p - 1, (p - 1) % 2):
                cp.wait()
            for cp in mem_out(p - 1, (p - 1) % 2):
                cp.start()
        if 2 <= p <= MEM_CHUNKS + 1:
            for cp in mem_out(p - 2, p % 2):
                cp.wait()
        if p < MEM_CHUNKS:
            for cp in mem_in(p, p % 2):
                cp.start()

    mem_phase(0)

    low = lax.broadcasted_iota(jnp.int32, (ts, LANES), 1) < half_lanes

    def feature_pair_rows(x_ref, base):
        tiles = [x_ref[b, pl.ds(base + f * tile, tile), :] for f in range(2) for b in range(n_b)]
        return jnp.concatenate(tiles, axis=0).T

    def key_to_chain(x_ref, dst, t0):
        def group(g, c):
            rows = []
            for j in range(4):
                mt = feature_pair_rows(x_ref, pl.multiple_of((g * 4 + j) * 2 * tile, 2 * tile))[t0:t0 + ts]
                sw = pltpu.roll(mt, half_lanes, 1)
                rows += [jnp.where(low, mt, sw), jnp.where(low, sw, mt)]
            dst[:, pl.ds(pl.multiple_of(g * 8, 8), 8), :] = jnp.swapaxes(jnp.stack(rows, axis=0), 0, 1)
            return c
        lax.fori_loop(0, RWKV_N // 8, group, 0)

    def value_to_chain(g, c):
        rows = [feature_pair_rows(v_ref, pl.multiple_of((g * 8 + j) * 2 * tile, 2 * tile)) for j in range(8)]
        v_c[:, pl.ds(pl.multiple_of(g * 8, 8), 8), :] = jnp.swapaxes(jnp.stack(rows, axis=0), 0, 1)
        return c
    lax.fori_loop(0, vr // 8, value_to_chain, 0)

    def ksum(x):
        return jnp.sum(x, axis=-2, keepdims=True)

    for t0 in range(0, SCAN_TC, ts):
        key_to_chain(r_ref, r_c, t0)
        key_to_chain(w_ref, w_c, t0)
        key_to_chain(k_ref, km_c, t0)
        key_to_chain(a_ref, b_c, t0)

        def prep(g8, c):
            toks = pl.ds(pl.multiple_of(g8 * 8, 8), 8)
            kr = km_c[toks]
            a = b_c[toks]
            kk = kr * kk_ref[...]
            kk = kk / jnp.maximum(jnp.sqrt(ksum(kk * kk)), 1e-12)
            a_c[toks] = -kk
            b_c[toks] = kk * a
            km_c[toks] = kr * (1.0 + (a - 1.0) * ka_ref[...])
            return c
        lax.fori_loop(0, ts // 8, prep, 0)
        mem_phase(1 + 2 * (t0 // ts))

        def token(t, carry):
            r, w, avec, bvec, kmod = r_c[t], w_c[t], a_c[t], b_c[t], km_c[t]

            def value_row(i, c2):
                s = s_scr[i]
                sa = ksum(s * avec)
                s = s * w + sa * bvec + v_c[t0 + t, pl.ds(i, 1), :] * kmod
                s_scr[i] = s
                y_c[t0 + t, pl.ds(i, 1), :] = ksum(s * r)
                return c2

            lax.fori_loop(0, vr, value_row, 0, unroll=True)
            return carry

        lax.fori_loop(0, ts, token, 0)
        mem_phase(2 + 2 * (t0 // ts))

        def post(g8, c):
            ktoks = pl.ds(pl.multiple_of(g8 * 8, 8), 8)
            vtoks = pl.ds(pl.multiple_of(t0 + g8 * 8, 8), 8)

            def vsum(x):
                x2 = x.reshape(8 * vr, LANES)
                x2 = x2 + pltpu.roll(x2, half_lanes, 1)
                return jnp.sum(x2.reshape(8, vr, LANES), axis=1, keepdims=True)

            y = y_c[vtoks]
            d = y - vsum(y) * (1.0 / RWKV_N)
            var = vsum(d * d) * (1.0 / RWKV_N)
            bonus = ksum(r_c[ktoks] * km_c[ktoks] * rk_ref[...])
            y_c[vtoks] = d * lax.rsqrt(var + LNX_EPS) * lw_ref[...] + lb_ref[...] + bonus * v_c[vtoks]
            return c
        lax.fori_loop(0, ts // 8, post, 0)

    def value_from_chain(g, c):
        blk = jnp.swapaxes(y_c[:, pl.ds(pl.multiple_of(g * 8, 8), 8), :], 0, 1)
        for j in range(8):
            mt = blk[j].T
            base = pl.multiple_of((g * 8 + j) * 2 * tile, 2 * tile)
            for hf in range(2):
                for b in range(n_b):
                    row0 = (hf * n_b + b) * tile
                    y_ref[b, pl.ds(base + hf * tile, tile), :] = mt[row0:row0 + tile, :]
        return c
    lax.fori_loop(0, vr // 8, value_from_chain, 0)

    assert 2 * (SCAN_TC // ts) == MEM_CHUNKS
    mem_phase(MEM_CHUNKS + 1)

    @pl.when(ci == pl.num_programs(0) - 1)
    def _():
        sout_ref[...] = s_scr[...]


def _rwkv_scan_prompt(r, k, w, a, v, k_k, k_a, r_k, lnx_w, lnx_b, s0, mem_k, mem_v):
    n_b, _, t = r.shape
    vr = RWKV_N // 2
    n_steps = t // SCAN_TC
    n_mem_seq = mem_k.shape[0]
    assert t % SCAN_TC == 0 and 2 * n_b * RWKV_HEADS == LANES and n_mem_seq % (n_steps * MEM_CHUNKS) == 0
    chunk_seqs = n_mem_seq // (n_steps * MEM_CHUNKS)
    any_spec = pl.BlockSpec(memory_space=pl.ANY)
    mem_shape = jax.ShapeDtypeStruct((n_mem_seq, MEM_HEADS, N_MEM, MEM_DH), F32)
    tok_spec = pl.BlockSpec((n_b, RWKV_W, SCAN_TC), lambda ci: (0, 0, ci))
    key_par = pl.BlockSpec((RWKV_N, LANES), lambda ci: (0, 0))
    val_par = pl.BlockSpec((vr, LANES), lambda ci: (0, 0))
    st_spec = pl.BlockSpec((vr, RWKV_N, LANES), lambda ci: (0, 0, 0))
    key_chain = pltpu.VMEM((SCAN_SUB, RWKV_N, LANES), F32)
    val_chain = pltpu.VMEM((SCAN_TC, vr, LANES), F32)
    return pl.pallas_call(
        functools.partial(_scan_prompt_body, n_b=n_b, mem_seqs=n_mem_seq // n_steps),
        grid=(n_steps,),
        in_specs=[tok_spec] * 5 + [key_par] * 3 + [val_par] * 2 + [st_spec, any_spec, any_spec],
        out_specs=[tok_spec, st_spec, any_spec, any_spec],
        out_shape=[jax.ShapeDtypeStruct((n_b, RWKV_W, t), F32),
                   jax.ShapeDtypeStruct((vr, RWKV_N, LANES), F32), mem_shape, mem_shape],
        scratch_shapes=([pltpu.VMEM((vr, RWKV_N, LANES), F32)] + [key_chain] * 5 + [val_chain] * 2
                        + [pltpu.VMEM((2, 2, chunk_seqs, MEM_HEADS, N_MEM, MEM_DH), F32),
                           pltpu.SemaphoreType.DMA((2,)), pltpu.SemaphoreType.DMA((2,))]),
        compiler_params=_params("arbitrary"),
    )(r, k, w, a, v, k_k, k_a, r_k, lnx_w, lnx_b, s0, mem_k, mem_v)


def _merge_body(*refs, part_counts):
    n = len(part_counts)
    x_refs, oret_refs, g_refs = refs[:n], refs[n:2 * n], refs[2 * n:3 * n]
    yt_ref, ys_ref, wt_ref, wb_ref, o_ref = refs[3 * n:]
    x = _read_row_parts(x_refs, part_counts)
    y = jnp.where(pl.program_id(0) >= part_counts[0], ys_ref[...], yt_ref[...].T)
    yb = (y * _read_row_parts(g_refs, part_counts)).astype(BF16)
    ob = _read_row_parts(oret_refs, part_counts).astype(BF16)
    n_chunk = 256
    for j in range(0, D_MODEL, n_chunk):
        acc = jnp.dot(ob, wt_ref[:, j:j + n_chunk], preferred_element_type=F32)
        acc = acc + jnp.dot(yb, wb_ref[:, j:j + n_chunk], preferred_element_type=F32)
        o_ref[:, j:j + n_chunk] = x[:, j:j + n_chunk] + acc


def _merge(x_parts, oret_parts, g_parts, y_first_t, y_second, w_ret, w_rwkv):
    m = sum(part.shape[0] for part in x_parts)
    in_specs, part_counts = [], None
    for parts in (x_parts, oret_parts, g_parts):
        specs, part_counts = _row_part_specs(parts, TM)
        in_specs += specs
    assert len(part_counts) == 2
    n_first = part_counts[0]
    tiles = y_first_t.shape[2] // TM
    assert y_first_t.shape[0] * tiles == n_first
    yt_spec = pl.BlockSpec((None, RWKV_W, TM),
                           lambda i: (jnp.minimum(i, n_first - 1) // tiles, 0, jnp.minimum(i, n_first - 1) % tiles))
    ys_spec = pl.BlockSpec((TM, RWKV_W), lambda i: (jnp.clip(i - n_first, 0, part_counts[1] - 1), 0))
    wspec = pl.BlockSpec((RET_W, D_MODEL), lambda i: (0, 0))
    return pl.pallas_call(
        functools.partial(_merge_body, part_counts=part_counts),
        grid=(m // TM,),
        in_specs=in_specs + [yt_spec, ys_spec, wspec, wspec],
        out_specs=pl.BlockSpec((TM, D_MODEL), lambda i: (i, 0)),
        out_shape=jax.ShapeDtypeStruct((m, D_MODEL), F32),
        compiler_params=_params("parallel"),
    )(*x_parts, *oret_parts, *g_parts, y_first_t, y_second, w_ret.astype(BF16), w_rwkv.astype(BF16))


def _mem_kv_body(x_ref, g_ref, wk_ref, wv_ref, k_ref, v_ref, kh_ref, vh_ref, *, n_seq):
    x = x_ref[...].astype(F32)
    xb = (x * lax.rsqrt(jnp.mean(x * x, axis=-1, keepdims=True) + EPS) * g_ref[...]).astype(BF16)
    for w_ref, o_ref, oh_ref in ((wk_ref, k_ref, kh_ref), (wv_ref, v_ref, vh_ref)):
        for h in range(MEM_HEADS):
            acc = jnp.dot(xb, w_ref[:, h * MEM_DH:(h + 1) * MEM_DH], preferred_element_type=F32)
            o_ref[:, h, :] = acc
            for s in range(n_seq):
                oh_ref[s, h] = acc[s * N_MEM:(s + 1) * N_MEM]


def _mem_kv(mem, gain, w_k, w_v):
    n_b = mem.shape[0]
    n_seq = TM // N_MEM
    assert n_seq * N_MEM == TM and n_b % n_seq == 0
    wspec = pl.BlockSpec((D_MODEL, D_MODEL), lambda i: (0, 0))
    tok_spec = pl.BlockSpec((TM, MEM_HEADS, MEM_DH), lambda i: (i, 0, 0))
    head_spec = pl.BlockSpec((n_seq, MEM_HEADS, N_MEM, MEM_DH), lambda i: (i, 0, 0, 0))
    tok_shape = jax.ShapeDtypeStruct((n_b * N_MEM, MEM_HEADS, MEM_DH), F32)
    head_shape = jax.ShapeDtypeStruct((n_b, MEM_HEADS, N_MEM, MEM_DH), F32)
    return pl.pallas_call(
        functools.partial(_mem_kv_body, n_seq=n_seq),
        grid=(n_b // n_seq,),
        in_specs=[pl.BlockSpec((TM, D_MODEL), lambda i: (i, 0)), pl.BlockSpec((1, D_MODEL), lambda i: (0, 0)),
                  wspec, wspec],
        out_specs=[tok_spec, tok_spec, head_spec, head_spec],
        out_shape=[tok_shape, tok_shape, head_shape, head_shape],
        compiler_params=_params("parallel"),
    )(mem.reshape(n_b * N_MEM, D_MODEL), gain.reshape(1, D_MODEL), w_k.astype(BF16), w_v.astype(BF16))


def _attn_body(q_ref, k_ref, v_ref, o_ref, *, n_seq, tq):
    nt = (((1,), (1,)), ((), ()))
    for g in range(n_seq):
        rows = slice(g * tq, (g + 1) * tq)
        q = q_ref[rows, :].astype(BF16)
        s = lax.dot_general(q, k_ref[g].astype(BF16), nt, preferred_element_type=F32) * (MEM_DH ** -0.5)
        p = jnp.exp(s - jnp.max(s, axis=-1, keepdims=True))
        l = jnp.sum(p, axis=-1, keepdims=True)
        o = jnp.dot(p.astype(BF16), v_ref[g].astype(BF16), preferred_element_type=F32)
        o_ref[rows, :] = o / l


def _attention(q, mem_k, mem_v, *, row0, n_batch, t, n_seq, tq):
    q_tiles = t // tq
    rows = n_seq * tq
    assert t % tq == 0 and n_batch % n_seq == 0 and row0 % rows == 0 and (n_seq == 1 or q_tiles == 1)
    blk0 = row0 // rows
    kv_spec = pl.BlockSpec((n_seq, None, N_MEM, MEM_DH), lambda b, h, qi: (b, h, 0, 0))
    return pl.pallas_call(
        functools.partial(_attn_body, n_seq=n_seq, tq=tq),
        grid=(n_batch // n_seq, MEM_HEADS, q_tiles),
        in_specs=[pl.BlockSpec((rows, MEM_DH), lambda b, h, qi: (blk0 + b * q_tiles + qi, h)), kv_spec, kv_spec],
        out_specs=pl.BlockSpec((rows, MEM_DH), lambda b, h, qi: (b * q_tiles + qi, h)),
        out_shape=jax.ShapeDtypeStruct((n_batch * t, D_MODEL), F32),
        compiler_params=_params("parallel", "parallel", "parallel"),
    )(q, mem_k, mem_v)


ROW_TILE = (D_MODEL // LANES, LANES)


def _rows_to_tiles(x):
    chunks = [x[:, j * LANES:(j + 1) * LANES] for j in range(ROW_TILE[0])]
    return jnp.swapaxes(jnp.stack(chunks, axis=0), 0, 1)


def _tiles_to_rows(x):
    chunks = jnp.swapaxes(x, 0, 1)
    return jnp.concatenate([chunks[j] for j in range(ROW_TILE[0])], axis=1)


def _router_body(h_ref, g_ref, w_ref, b_ref, hn_ref, ids_ref, comb_ref):
    x = h_ref[...]
    hn = x * lax.rsqrt(jnp.mean(x * x, axis=-1, keepdims=True) + EPS) * g_ref[...]
    hn_ref[...] = _rows_to_tiles(hn)
    logits = jnp.dot(hn, w_ref[...], precision=lax.Precision.HIGHEST, preferred_element_type=F32) + b_ref[...]
    lane = lax.broadcasted_iota(jnp.int32, logits.shape, 1).astype(F32)
    neg = -jnp.inf

    def first_argmax(vals):
        m = jnp.max(vals, axis=-1, keepdims=True)
        return m, jnp.min(jnp.where(vals == m, lane, float(LANES)), axis=-1, keepdims=True)

    gl = jnp.where(lane < N_GROUPS, logits, neg)
    gmax, gsel = first_argmax(gl)
    pg_sel = 1.0 / jnp.sum(jnp.exp(gl - gmax), axis=-1, keepdims=True)
    e0 = N_GROUPS + gsel * EXP_PER_GROUP
    el = jnp.where((lane >= e0) & (lane < e0 + EXP_PER_GROUP), logits, neg)
    m1, i1 = first_argmax(el)
    m2, i2 = first_argmax(jnp.where(lane == i1, neg, el))
    e21 = jnp.exp(m2 - m1)
    c1 = pg_sel / (1.0 + e21)
    c2 = c1 * e21
    ids = jnp.where(lane == 0, i1 - N_GROUPS, jnp.where(lane == 1, i2 - N_GROUPS, 0.0))
    ids_ref[...] = ids.astype(jnp.int32)
    comb_ref[...] = jnp.where(lane == 0, c1, jnp.where(lane == 1, c2, 0.0))


def _router(h, g_ffn, w_gr, b_gr, w_er, b_er):
    m = h.shape[0]
    pad = LANES - N_GROUPS - N_EXPERTS
    w = jnp.concatenate([w_gr, w_er, jnp.zeros((D_MODEL, pad), F32)], axis=1)
    b = jnp.concatenate([b_gr, b_er, jnp.zeros((pad,), F32)]).reshape(1, LANES)
    row = lambda n: pl.BlockSpec((TM, n), lambda i: (i, 0))
    return pl.pallas_call(
        _router_body,
        grid=(m // TM,),
        in_specs=[row(D_MODEL), pl.BlockSpec((1, D_MODEL), lambda i: (0, 0)),
                  pl.BlockSpec((D_MODEL, LANES), lambda i: (0, 0)), pl.BlockSpec((1, LANES), lambda i: (0, 0))],
        out_specs=[pl.BlockSpec((TM,) + ROW_TILE, lambda i: (i, 0, 0)), row(LANES), row(LANES)],
        out_shape=[jax.ShapeDtypeStruct((m,) + ROW_TILE, F32), jax.ShapeDtypeStruct((m, LANES), jnp.int32),
                   jax.ShapeDtypeStruct((m, LANES), F32)],
        compiler_params=_params("parallel"),
    )(h, g_ffn.reshape(1, D_MODEL), w, b)


def _dispatch_body(dest_ref, hn_ref, sorted_in, sorted_out, sem):
    del sorted_in

    def issue(r, c):
        for k in range(TOP_K):
            pltpu.make_async_copy(hn_ref.at[r], sorted_out.at[dest_ref[0, k, r]], sem.at[k]).start(priority=k)
        return c
    lax.fori_loop(0, TM, issue, 0, unroll=8)
    for k in range(TOP_K):
        pltpu.make_async_copy(hn_ref, sorted_out.at[pl.ds(0, TM)], sem.at[k]).wait()


def _dispatch(hn, dest, n_sorted):
    n_tok = hn.shape[0]
    return pl.pallas_call(
        _dispatch_body,
        grid=(n_tok // TM,),
        in_specs=[pl.BlockSpec((1, TOP_K, TM), lambda i: (i, 0, 0), memory_space=pltpu.SMEM),
                  pl.BlockSpec((TM,) + ROW_TILE, lambda i: (i, 0, 0)),
                  pl.BlockSpec(memory_space=pl.ANY)],
        out_specs=pl.BlockSpec(memory_space=pl.ANY),
        out_shape=jax.ShapeDtypeStruct((n_sorted,) + ROW_TILE, F32),
        scratch_shapes=[pltpu.SemaphoreType.DMA((TOP_K,))],
        input_output_aliases={2: 0},
        compiler_params=_params("arbitrary"),
    )(dest, hn, jnp.zeros((n_sorted,) + ROW_TILE, F32))


def _expert_body(blk_e_ref, x_ref, wg_ref, wu_ref, wd_ref, o_ref):
    del blk_e_ref
    x = _tiles_to_rows(x_ref[...]).astype(BF16)
    hg = jnp.dot(x, wg_ref[0].astype(BF16), preferred_element_type=F32)
    hu = jnp.dot(x, wu_ref[0].astype(BF16), preferred_element_type=F32)
    act = (hg * jax.nn.sigmoid(hg) * hu).astype(BF16)
    o_ref[...] = _rows_to_tiles(jnp.dot(act, wd_ref[0].astype(BF16), preferred_element_type=F32))


def _experts(x_sorted, blk_e, w_gate, w_up, w_down):
    n_blocks = blk_e.shape[0]
    row_spec = pl.BlockSpec((MOE_ROWS,) + ROW_TILE, lambda i, be: (i, 0, 0))
    grid_spec = pltpu.PrefetchScalarGridSpec(
        num_scalar_prefetch=1,
        grid=(n_blocks,),
        in_specs=[
            row_spec,
            pl.BlockSpec((1, D_MODEL, D_EXPERT), lambda i, be: (be[i], 0, 0)),
            pl.BlockSpec((1, D_MODEL, D_EXPERT), lambda i, be: (be[i], 0, 0)),
            pl.BlockSpec((1, D_EXPERT, D_MODEL), lambda i, be: (be[i], 0, 0)),
        ],
        out_specs=row_spec,
    )
    return pl.pallas_call(
        _expert_body,
        grid_spec=grid_spec,
        out_shape=jax.ShapeDtypeStruct(x_sorted.shape, F32),
        compiler_params=_params("arbitrary"),
    )(blk_e, x_sorted, w_gate, w_up, w_down)


def _route_plan(ids):
    n_tok = ids.shape[0]
    n_pairs = ids.size
    n_blocks = -(-(n_pairs + N_EXPERTS * (MOE_ROWS - 1)) // MOE_ROWS)
    flat_e = ids.reshape(n_pairs)
    onehot = (flat_e[:, None] == jnp.arange(N_EXPERTS, dtype=jnp.int32)[None, :]).astype(jnp.int32)
    csum = jnp.cumsum(onehot, axis=0)
    rank = jnp.sum(onehot * csum, axis=1) - 1
    counts = csum[-1]
    pcounts = (counts + MOE_ROWS - 1) // MOE_ROWS * MOE_ROWS
    pends = jnp.cumsum(pcounts)
    pstarts = pends - pcounts
    dest = jnp.sum(onehot * pstarts[None, :], axis=1) + rank
    block_start = jnp.arange(n_blocks, dtype=jnp.int32) * MOE_ROWS
    blk_e = jnp.minimum(jnp.sum((block_start[:, None] >= pends[None, :]).astype(jnp.int32), axis=1),
                        N_EXPERTS - 1).astype(jnp.int32)
    dest = dest.astype(jnp.int32).reshape(n_tok // TM, TM, TOP_K).transpose(0, 2, 1)
    return blk_e, dest, n_blocks * MOE_ROWS


def _final_body(dest_ref, dest_next_ref, h_ref, comb_ref, g_ref, y_hbm, o_first, o_second, ybuf, sem, *, n_first):
    i = pl.program_id(0)
    n = pl.num_programs(0)
    slot = i % 2

    def start_gather(ref, sl):
        def issue(r, c):
            for k in range(TOP_K):
                pltpu.make_async_copy(y_hbm.at[ref[0, k, r]], ybuf.at[sl, k, r], sem.at[sl]).start(priority=k)
            return c
        lax.fori_loop(0, TM, issue, 0, unroll=8)

    @pl.when(i == 0)
    def _():
        start_gather(dest_ref, 0)

    @pl.when(i + 1 < n)
    def _():
        start_gather(dest_next_ref, 1 - slot)

    for k in range(TOP_K):
        pltpu.make_async_copy(y_hbm.at[pl.ds(0, TM)], ybuf.at[slot, k], sem.at[slot]).wait()

    first, second = _tiles_to_rows(ybuf[slot, 0]), _tiles_to_rows(ybuf[slot, 1])
    x = h_ref[...] + (first * comb_ref[:, 0:1] + second * comb_ref[:, 1:2])
    out = x * lax.rsqrt(jnp.mean(x * x, axis=-1, keepdims=True) + EPS) * g_ref[...]

    @pl.when(i < n_first)
    def _():
        o_first[...] = out

    @pl.when(i >= n_first)
    def _():
        o_second[...] = out


def _final(h, y_sorted, dest, comb, g_final, *, n_first_rows):
    n_rows = h.shape[0]
    assert n_rows % TM == 0 and n_first_rows % TM == 0 and 0 < n_first_rows < n_rows
    n_steps, n_first = n_rows // TM, n_first_rows // TM
    row = lambda n: pl.BlockSpec((TM, n), lambda i: (i, 0))
    dest_spec = lambda f: pl.BlockSpec((1, TOP_K, TM), f, memory_space=pltpu.SMEM)
    return pl.pallas_call(
        functools.partial(_final_body, n_first=n_first),
        grid=(n_steps,),
        in_specs=[dest_spec(lambda i: (i, 0, 0)),
                  dest_spec(lambda i: (jnp.minimum(i + 1, n_steps - 1), 0, 0)),
                  row(D_MODEL), row(LANES), pl.BlockSpec((1, D_MODEL), lambda i: (0, 0)),
                  pl.BlockSpec(memory_space=pl.ANY)],
        out_specs=[pl.BlockSpec((TM, D_MODEL), lambda i: (jnp.minimum(i, n_first - 1), 0)),
                   pl.BlockSpec((TM, D_MODEL), lambda i: (jnp.maximum(i - n_first, 0), 0))],
        out_shape=[jax.ShapeDtypeStruct((n_first_rows, D_MODEL), F32),
                   jax.ShapeDtypeStruct((n_rows - n_first_rows, D_MODEL), F32)],
        scratch_shapes=[pltpu.VMEM((2, TOP_K, TM) + ROW_TILE, F32), pltpu.SemaphoreType.DMA((2,))],
        compiler_params=_params("arbitrary"),
    )(dest, dest, h, comb, g_final.reshape(1, D_MODEL), y_sorted)


def _reorder_last(x, shape, order):
    lead = x.shape[:-1]
    n = len(lead)
    y = x.reshape(lead + shape).transpose(tuple(range(n)) + tuple(n + o for o in order))
    return y.reshape(lead + (x.shape[-1],))


HALF_N = RWKV_N // 2


def _key_major(x):
    return _reorder_last(x, (RWKV_HEADS, RWKV_N), (1, 0))


def _key_major_inv(x):
    return _reorder_last(x, (RWKV_N, RWKV_HEADS), (1, 0))


def _value_major(x):
    return _reorder_last(x, (RWKV_HEADS, 2, HALF_N), (2, 1, 0))


def _value_major_inv(x):
    return _reorder_last(x, (HALF_N, 2, RWKV_HEADS), (2, 1, 0))


def _rwkv_cols(x, key_fn, value_fn):
    return jnp.concatenate([key_fn(x[..., :RWKV_W]), key_fn(x[..., RWKV_W:2 * RWKV_W]),
                            value_fn(x[..., 2 * RWKV_W:3 * RWKV_W]), x[..., 3 * RWKV_W:]], axis=-1)


def _sample_key_layout(x, b, t):
    return x.reshape(b, t, RWKV_N, RWKV_HEADS).transpose(3, 1, 2, 0)


def _sample_value_layout(x, b, t):
    return x.reshape(b, t, HALF_N, 2, RWKV_HEADS).transpose(4, 1, 3, 2, 0).reshape(RWKV_HEADS, t, RWKV_N, b)


def _sample_value_unlayout(y, b, t):
    return y.reshape(RWKV_HEADS, t, 2, HALF_N, b).transpose(4, 1, 3, 2, 0).reshape(b * t, RWKV_W)


def kernel(x_prompt, x_sample, mem_prompt, state_ret, state_rwkv, state_shift, cache_mem_k, cache_mem_v,
           g_mix, w_in, ret_gn, rwkv_mu, rwkv_w0, rwkv_w2, rwkv_a0, rwkv_a2, rwkv_g2, rwkv_k_k, rwkv_k_a,
           rwkv_r_k, rwkv_lnx_w, rwkv_lnx_b, w_out, g_mem_q, g_mem_kv, w_mq, w_mk, w_mv, w_mo, g_ffn,
           w_group_router, b_group_router, w_expert_router, b_expert_router, w_e_gate, w_e_up, w_e_down,
           g_final):
    assert w_in.shape[0] == 1, "single-layer decoder"
    bp, tp, d = x_prompt.shape
    bs, ts, _ = x_sample.shape
    np_tok, ns_tok = bp * tp, bs * ts
    assert d == D_MODEL and bp * RWKV_HEADS * 2 == LANES and bs == LANES
    l = 0
    x_parts = [x_prompt.reshape(np_tok, d), x_sample.reshape(ns_tok, d)]

    w_in_l = jnp.concatenate([w_in[l][:, :N_RET_COLS], _rwkv_cols(w_in[l][:, N_RET_COLS:], _key_major, _value_major)],
                             axis=1)
    (proj,) = _matmul(x_parts, [w_in_l.astype(BF16)], gain=g_mix[l])

    pos_p = np.arange(tp)
    pos_s = PAST_LEN + np.arange(ts)
    zero_ret = jnp.zeros((bp, RET_HEADS, RET_DK, RET_DV), F32)
    oret_p, sret_p = _retention(proj, zero_ret, ret_gn[l], pos_p, row0=0, n_batch=bp, t=tp, n_seq=1)
    oret_s, sret_s = _retention(proj, state_ret[l], ret_gn[l], pos_s, row0=np_tok, n_batch=bs, t=ts, n_seq=16)

    pre_w = (_rwkv_cols(rwkv_mu[l], _key_major, _value_major), _key_major(rwkv_w0[l]), _key_major(rwkv_w2[l]),
             _key_major(rwkv_a0[l]), _key_major(rwkv_a2[l]), _value_major(rwkv_g2[l]))
    zero_shift = jnp.zeros((bp, N_RWKV_COLS), F32)
    shift_in = _rwkv_cols(state_shift[l], _key_major, _value_major)
    r_p, k_p, v_p, w_p, a_p, gate_p = _rwkv_pre(proj, zero_shift, *pre_w, row0=0, n_batch=bp, t=tp,
                                                 n_seq=1, c=256, transposed=True)
    r_s, k_s, v_s, w_s, a_s, gate_s = _rwkv_pre(proj, shift_in, *pre_w, row0=np_tok, n_batch=bs, t=ts,
                                                 n_seq=16, c=ts, transposed=False)

    kvec = lambda v: v.reshape(RWKV_HEADS, RWKV_N)
    key_par = lambda v: jnp.broadcast_to(kvec(v).T[:, None, None, :], (RWKV_N, 2, bp, RWKV_HEADS)).reshape(
        RWKV_N, LANES)

    val_par = lambda v: jnp.broadcast_to(
        v.reshape(RWKV_HEADS, 2, HALF_N).transpose(2, 1, 0)[:, :, None, :],
        (HALF_N, 2, bp, RWKV_HEADS)).reshape(HALF_N, LANES)
    mem_shape = (N_MEM, MEM_HEADS, MEM_DH)
    y_p, srw_p, cache_k_heads, cache_v_heads = _rwkv_scan_prompt(
        r_p, k_p, w_p, a_p, v_p, key_par(rwkv_k_k[l]), key_par(rwkv_k_a[l]), key_par(rwkv_r_k[l]),
        val_par(rwkv_lnx_w[l]), val_par(rwkv_lnx_b[l]), jnp.zeros((HALF_N, RWKV_N, LANES), F32),
        cache_mem_k.reshape(bs, *mem_shape), cache_mem_v.reshape(bs, *mem_shape))
    srw_p = srw_p.reshape(HALF_N, RWKV_N, 2, bp, RWKV_HEADS).transpose(3, 4, 2, 0, 1).reshape(
        bp, RWKV_HEADS, RWKV_N, RWKV_N)
    head_par = lambda v: jnp.broadcast_to(kvec(v)[:, :, None], (RWKV_HEADS, RWKV_N, LANES))
    y_s, srw_s = _rwkv_scan(
        _sample_key_layout(r_s, bs, ts), _sample_key_layout(k_s, bs, ts), _sample_key_layout(w_s, bs, ts),
        _sample_key_layout(a_s, bs, ts), _sample_value_layout(v_s, bs, ts),
        head_par(rwkv_k_k[l]), head_par(rwkv_k_a[l]), head_par(rwkv_r_k[l]),
        head_par(rwkv_lnx_w[l]), head_par(rwkv_lnx_b[l]),
        state_rwkv[l].astype(F32).transpose(1, 2, 3, 0), tc=ts, halves=1)
    y_s = _sample_value_unlayout(y_s, bs, ts)
    srw_s = srw_s.transpose(3, 0, 1, 2)

    w_rwkv_out = _value_major(w_out[l][RET_W:].T).T
    h = _merge(x_parts, [oret_p, oret_s], [gate_p, gate_s], y_p, y_s, w_out[l][:RET_W], w_rwkv_out)

    mk, mv, mk_heads, mv_heads = _mem_kv(mem_prompt, g_mem_kv[l], w_mk[l], w_mv[l])
    (q,) = _matmul([h], [w_mq[l].astype(BF16)], gain=g_mem_q[l])
    att_p = _attention(q, mk_heads, mv_heads, row0=0, n_batch=bp, t=tp, n_seq=1, tq=TM)
    att_s = _attention(q, cache_k_heads, cache_v_heads, row0=np_tok, n_batch=bs, t=ts, n_seq=16, tq=ts)
    (h,) = _matmul([att_p, att_s], [w_mo[l].astype(BF16)], residual=h)

    hn, ids, comb = _router(h, g_ffn[l], w_group_router[l], b_group_router[l], w_expert_router[l],
                            b_expert_router[l])
    blk_e, dest, n_sorted = _route_plan(ids[:, :TOP_K])
    y_sorted = _experts(_dispatch(hn, dest, n_sorted), blk_e, w_e_gate[l], w_e_up[l], w_e_down[l])
    y_prompt, y_sample = _final(h, y_sorted, dest, comb, g_final, n_first_rows=np_tok)
    y_prompt = y_prompt.reshape(bp, tp, d)
    y_sample = y_sample.reshape(bs, ts, d)

    shift_p = lax.slice(proj, (tp - 1, N_RET_COLS), (np_tok, N_IN_COLS), (tp, 1))
    shift_s = lax.slice(proj, (np_tok + ts - 1, N_RET_COLS), (np_tok + ns_tok, N_IN_COLS), (ts, 1))
    shift_p = _rwkv_cols(shift_p, _key_major_inv, _value_major_inv)
    shift_s = _rwkv_cols(shift_s, _key_major_inv, _value_major_inv)
    return (y_prompt, y_sample, sret_p[None], srw_p[None], shift_p[None],
            mk.reshape(1, bp, *mem_shape), mv.reshape(1, bp, *mem_shape),
            sret_s[None], srw_s[None], shift_s[None])
```

```python
import functools

import numpy as np
import jax
import jax.numpy as jnp
from jax import lax
from jax.experimental import pallas as pl
from jax.experimental.pallas import tpu as pltpu

F32 = jnp.float32
BF16 = jnp.bfloat16

D_MODEL = 1024
PAST_LEN = 16384
N_MEM = 256
MEM_HEADS = 4
MEM_DH = D_MODEL // MEM_HEADS
RET_HEADS = 4
RET_W = D_MODEL // 2
RET_DV = RET_W // RET_HEADS
RET_DK = RET_DV // 2
RET_QK = RET_HEADS * RET_DK
RET_CHUNK = 128
ROPE_BASE = 10000.0
RWKV_N = 64
RWKV_W = D_MODEL - RET_W
RWKV_HEADS = RWKV_W // RWKV_N
LORA_W = 64
LORA_A = 64
LORA_G = 128
LNX_EPS = 64e-5
N_RET_COLS = 2 * RET_QK + 2 * RET_W
N_RWKV_COLS = 3 * RWKV_W + LORA_W + LORA_A + LORA_G
N_IN_COLS = N_RET_COLS + N_RWKV_COLS
N_GROUPS = 4
EXP_PER_GROUP = 8
N_EXPERTS = N_GROUPS * EXP_PER_GROUP
TOP_K = 2
D_EXPERT = D_MODEL // 2
EPS = 1e-6

LANES = 128
MOE_ROWS = 256
TM = 512


def _params(*sem):
    return pltpu.CompilerParams(dimension_semantics=sem)


def _row_part_specs(parts, tm):
    specs, counts, start = [], [], 0
    for part in parts:
        nb = part.shape[0] // tm
        assert nb * tm == part.shape[0]
        specs.append(pl.BlockSpec((tm, part.shape[1]), lambda i, s=start, n=nb: (jnp.clip(i - s, 0, n - 1), 0)))
        counts.append(nb)
        start += nb
    return specs, counts


def _read_row_parts(refs, counts):
    i = pl.program_id(0)
    x = refs[0][...]
    start = counts[0]
    for ref, nb in zip(refs[1:], counts[1:]):
        x = jnp.where(i >= start, ref[...], x)
        start += nb
    return x


def _mm_body(*refs, part_counts, n_w, has_gain, has_res, n_chunk):
    it = iter(refs)
    x_refs = [next(it) for _ in part_counts]
    g_ref = next(it) if has_gain else None
    w_refs = [next(it) for _ in range(n_w)]
    r_ref = next(it) if has_res else None
    o_refs = [next(it) for _ in range(n_w)]
    x = _read_row_parts(x_refs, part_counts).astype(F32)
    if has_gain:
        x = x * lax.rsqrt(jnp.mean(x * x, axis=-1, keepdims=True) + EPS) * g_ref[...]
    xb = x.astype(BF16)
    for w_ref, o_ref in zip(w_refs, o_refs):
        for j in range(0, w_ref.shape[1], n_chunk):
            acc = jnp.dot(xb, w_ref[:, j:j + n_chunk], preferred_element_type=F32)
            if has_res:
                acc = acc + r_ref[:, j:j + n_chunk]
            o_ref[:, j:j + n_chunk] = acc


def _matmul(x_parts, ws, *, gain=None, residual=None, tm=TM):
    k = x_parts[0].shape[1]
    m = sum(part.shape[0] for part in x_parts)
    n_chunk = 256
    assert all(w.shape[1] % n_chunk == 0 for w in ws)
    assert residual is None or len(ws) == 1
    in_specs, part_counts = _row_part_specs(x_parts, tm)
    args = list(x_parts)
    if gain is not None:
        in_specs.append(pl.BlockSpec((1, k), lambda i: (0, 0)))
        args.append(gain.reshape(1, k).astype(F32))
    for w in ws:
        in_specs.append(pl.BlockSpec(w.shape, lambda i: (0, 0)))
        args.append(w)
    if residual is not None:
        in_specs.append(pl.BlockSpec((tm, ws[0].shape[1]), lambda i: (i, 0)))
        args.append(residual)
    return pl.pallas_call(
        functools.partial(_mm_body, part_counts=part_counts, n_w=len(ws), has_gain=gain is not None,
                          has_res=residual is not None, n_chunk=n_chunk),
        grid=(m // tm,),
        in_specs=in_specs,
        out_specs=[pl.BlockSpec((tm, w.shape[1]), lambda i: (i, 0)) for w in ws],
        out_shape=[jax.ShapeDtypeStruct((m, w.shape[1]), F32) for w in ws],
        compiler_params=_params("parallel"),
    )(*args)


def _rot_tables(pos):
    half = RET_DK // 2
    inv_freq = ROPE_BASE ** (-(np.arange(half, dtype=np.float64) / half))
    ang = pos.astype(np.float64)[:, None] * inv_freq[None, :]
    cos, sin = np.cos(ang), np.sin(ang)
    zero = np.zeros_like(sin)
    c = np.tile(np.concatenate([cos, cos], axis=1), (1, RET_HEADS))
    s_lo = np.tile(np.concatenate([-sin, zero], axis=1), (1, RET_HEADS))
    s_hi = np.tile(np.concatenate([zero, sin], axis=1), (1, RET_HEADS))
    return [jnp.asarray(t, F32) for t in (c, s_lo, s_hi)]


def _ret_decay_tables(c):
    lg = np.log1p(-np.exp2(-5.0 - np.arange(RET_HEADS, dtype=np.float64)))
    idx = np.arange(c, dtype=np.float64)
    diff = idx[:, None] - idx[None, :]
    mask = np.where(diff[None] >= 0, np.exp(np.maximum(diff, 0.0)[None] * lg[:, None, None]), 0.0)
    q_dec = np.repeat(np.exp((idx[:, None] + 1.0) * lg[None, :]), RET_DV, axis=1)
    k_dec = np.repeat(np.exp((c - 1.0 - idx)[:, None] * lg[None, :]), RET_DK, axis=1)
    c_dec = [float(v) for v in np.exp(c * lg)]
    return jnp.asarray(mask, F32), jnp.asarray(q_dec, F32), jnp.asarray(k_dec, F32), c_dec


def _ret_body(q_ref, k_ref, v_ref, gate_ref, c_ref, slo_ref, shi_ref, mask_ref, qdec_ref, kdec_ref,
              gn_ref, s0_ref, o_ref, sout_ref, s_scr, *, n_seq, c, c_dec):
    ci = pl.program_id(1)

    @pl.when(ci == 0)
    def _():
        s_scr[...] = s0_ref[...].astype(F32)

    cos, s_lo, s_hi = c_ref[...], slo_ref[...], shi_ref[...]
    half = RET_DK // 2

    def rope(x):
        return x * cos + pltpu.roll(x, RET_QK - half, 1) * s_lo + pltpu.roll(x, half, 1) * s_hi

    nt = (((1,), (1,)), ((), ()))
    tn = (((0,), (0,)), ((), ()))
    for g in range(n_seq):
        rows = slice(g * c, (g + 1) * c)
        q = rope(q_ref[rows, :].astype(F32))
        k = rope(k_ref[rows, :].astype(F32)) * (RET_DK ** -0.5)
        k_st = k * kdec_ref[...]
        for h in range(RET_HEADS):
            kc = slice(h * RET_DK, (h + 1) * RET_DK)
            vc = slice(h * RET_DV, (h + 1) * RET_DV)
            qh = q[:, kc].astype(BF16)
            vh = v_ref[rows, vc].astype(BF16)
            s_h = s_scr[g, h]
            att = lax.dot_general(qh, k[:, kc].astype(BF16), nt, preferred_element_type=F32) * mask_ref[h]
            o = jnp.dot(att.astype(BF16), vh, preferred_element_type=F32)
            o = o + jnp.dot(qh, s_h.astype(BF16), preferred_element_type=F32) * qdec_ref[:, vc]
            s_scr[g, h] = s_h * c_dec[h] + lax.dot_general(
                k_st[:, kc].astype(BF16), vh, tn, preferred_element_type=F32)
            o = o * lax.rsqrt(jnp.mean(o * o, axis=-1, keepdims=True) + EPS)
            gate = gate_ref[rows, vc].astype(F32)
            o_ref[rows, vc] = o * gn_ref[:, vc] * (gate * jax.nn.sigmoid(gate))

    @pl.when(ci == pl.num_programs(1) - 1)
    def _():
        sout_ref[...] = s_scr[...]


def _retention(proj, s0, ret_gn, pos, *, row0, n_batch, t, n_seq):
    c = RET_CHUNK if t % RET_CHUNK == 0 else t
    n_chunks = t // c
    rows = n_seq * c
    assert n_batch % n_seq == 0 and row0 % rows == 0 and (n_seq == 1 or n_chunks == 1)
    blk0 = row0 // rows
    mask, q_dec, k_dec, c_dec = _ret_decay_tables(c)
    cos, s_lo, s_hi = _rot_tables(pos)

    def row_map(col):
        return lambda b, ci: (blk0 + b * n_chunks + ci, col)

    def const2(b, ci):
        return (0, 0)

    state_spec = pl.BlockSpec((n_seq, RET_HEADS, RET_DK, RET_DV), lambda b, ci: (b, 0, 0, 0))
    in_specs = [
        pl.BlockSpec((rows, RET_QK), row_map(0)),
        pl.BlockSpec((rows, RET_QK), row_map(1)),
        pl.BlockSpec((rows, RET_W), row_map(1)),
        pl.BlockSpec((rows, RET_W), row_map(2)),
        pl.BlockSpec((c, RET_QK), lambda b, ci: (ci, 0)),
        pl.BlockSpec((c, RET_QK), lambda b, ci: (ci, 0)),
        pl.BlockSpec((c, RET_QK), lambda b, ci: (ci, 0)),
        pl.BlockSpec((RET_HEADS, c, c), lambda b, ci: (0, 0, 0)),
        pl.BlockSpec((c, RET_W), const2),
        pl.BlockSpec((c, RET_QK), const2),
        pl.BlockSpec((1, RET_W), const2),
        state_spec,
    ]
    return pl.pallas_call(
        functools.partial(_ret_body, n_seq=n_seq, c=c, c_dec=c_dec),
        grid=(n_batch // n_seq, n_chunks),
        in_specs=in_specs,
        out_specs=[pl.BlockSpec((rows, RET_W), lambda b, ci: (b * n_chunks + ci, 0)), state_spec],
        out_shape=[jax.ShapeDtypeStruct((n_batch * t, RET_W), F32),
                   jax.ShapeDtypeStruct((n_batch, RET_HEADS, RET_DK, RET_DV), F32)],
        scratch_shapes=[pltpu.VMEM((n_seq, RET_HEADS, RET_DK, RET_DV), F32)],
        compiler_params=_params("parallel", "arbitrary"),
    )(proj, proj, proj, proj, cos, s_lo, s_hi, mask, q_dec, k_dec, ret_gn.reshape(1, RET_W).astype(F32), s0)


LORA_COLS = LORA_W + LORA_A + LORA_G


def _rwkv_lora_terms(lo, w0_ref, w2_ref, a0_ref, a2_ref, g2_ref):
    hw = lo[:, :LORA_W]
    ha = lo[:, LORA_W:LORA_W + LORA_A]
    hg = lo[:, LORA_W + LORA_A:]
    u = w0_ref[...] + jnp.dot(jnp.tanh(hw).astype(BF16), w2_ref[...], preferred_element_type=F32)
    decay = jnp.exp(-float(np.exp(-0.5)) * jax.nn.sigmoid(u))
    rate = jax.nn.sigmoid(a0_ref[...] + jnp.dot(ha.astype(BF16), a2_ref[...], preferred_element_type=F32))
    gate = jnp.dot(jax.nn.sigmoid(hg).astype(BF16), g2_ref[...], preferred_element_type=F32)
    return decay, rate, gate


def _rwkv_pre_body(r_ref, k_ref, v_ref, lo_ref, shift_ref, mu_ref, w0_ref, w2_ref, a0_ref, a2_ref, g2_ref,
                   ro_ref, ko_ref, vo_ref, wo_ref, ao_ref, go_ref, so_ref, prev_scr):
    ci = pl.program_id(1)
    c = r_ref.shape[0]

    @pl.when(ci == 0)
    def _():
        prev_scr[...] = shift_ref[0].astype(F32)

    first_row = lax.broadcasted_iota(jnp.int32, (c, 1), 0) == 0

    def shifted(x_ref, col0):
        w = x_ref.shape[1]
        x = x_ref[...].astype(F32)
        prev = jnp.where(first_row, prev_scr[:, col0:col0 + w], pltpu.roll(x, 1, 0))
        prev_scr[:, col0:col0 + w] = x[c - 1:c, :]
        return x + (prev - x) * mu_ref[:, col0:col0 + w]

    ro_ref[...] = shifted(r_ref, 0).T
    ko_ref[...] = shifted(k_ref, RWKV_W).T
    vo_ref[...] = shifted(v_ref, 2 * RWKV_W).T
    decay, rate, gate = _rwkv_lora_terms(shifted(lo_ref, 3 * RWKV_W), w0_ref, w2_ref, a0_ref, a2_ref, g2_ref)
    wo_ref[...] = decay.T
    ao_ref[...] = rate.T
    go_ref[...] = gate

    @pl.when(ci == pl.num_programs(1) - 1)
    def _():
        so_ref[0] = prev_scr[...]


def _rwkv_pre_args(s_shift, n_batch, mu, w0, w2, a0, a2, g2):
    return (s_shift.reshape(n_batch, 1, N_RWKV_COLS), mu.reshape(1, -1), w0.reshape(1, -1), w2.astype(BF16),
            a0.reshape(1, -1), a2.astype(BF16), g2.astype(BF16))


def _rwkv_pre_weight_specs(const):
    return [pl.BlockSpec((1, N_RWKV_COLS), const), pl.BlockSpec((1, RWKV_W), const),
            pl.BlockSpec((LORA_W, RWKV_W), const), pl.BlockSpec((1, RWKV_W), const),
            pl.BlockSpec((LORA_A, RWKV_W), const), pl.BlockSpec((LORA_G, RWKV_W), const)]


def _rwkv_pre(proj, s_shift, mu, w0, w2, a0, a2, g2, *, row0, n_batch, t, c):
    n_chunks = t // c
    assert t % c == 0 and row0 % c == 0
    blk0 = row0 // c
    col_r = N_RET_COLS // RWKV_W
    col_lo = (N_RET_COLS + 3 * RWKV_W) // LORA_COLS
    assert col_r * RWKV_W == N_RET_COLS and col_lo * LORA_COLS == N_RET_COLS + 3 * RWKV_W

    def row_map(col):
        return lambda b, ci: (blk0 + b * n_chunks + ci, col)

    state_spec = pl.BlockSpec((1, 1, N_RWKV_COLS), lambda b, ci: (b, 0, 0))
    in_specs = [pl.BlockSpec((c, RWKV_W), row_map(col_r)), pl.BlockSpec((c, RWKV_W), row_map(col_r + 1)),
                pl.BlockSpec((c, RWKV_W), row_map(col_r + 2)), pl.BlockSpec((c, LORA_COLS), row_map(col_lo)),
                state_spec] + _rwkv_pre_weight_specs(lambda b, ci: (0, 0))
    vec_spec = pl.BlockSpec((None, RWKV_W, c), lambda b, ci: (b, 0, ci))
    vec_shape = jax.ShapeDtypeStruct((n_batch, RWKV_W, t), F32)
    return pl.pallas_call(
        _rwkv_pre_body,
        grid=(n_batch, n_chunks),
        in_specs=in_specs,
        out_specs=[vec_spec] * 5 + [pl.BlockSpec((c, RWKV_W), lambda b, ci: (b * n_chunks + ci, 0)), state_spec],
        out_shape=[vec_shape] * 5 + [jax.ShapeDtypeStruct((n_batch * t, RWKV_W), F32),
                                     jax.ShapeDtypeStruct((n_batch, 1, N_RWKV_COLS), F32)],
        scratch_shapes=[pltpu.VMEM((1, N_RWKV_COLS), F32)],
        compiler_params=_params("parallel", "arbitrary"),
    )(proj, proj, proj, proj, *_rwkv_pre_args(s_shift, n_batch, mu, w0, w2, a0, a2, g2))


def _rwkv_pre_short_body(r_ref, k_ref, v_ref, lo_ref, shift_ref, mu_ref, w0_ref, w2_ref, a0_ref, a2_ref, g2_ref,
                         ro_ref, ko_ref, vo_ref, wo_ref, ao_ref, go_ref, so_ref, *, n_b, t):
    rows = n_b * t
    first_tok = (lax.broadcasted_iota(jnp.int32, (rows, 1), 0) & (t - 1)) == 0

    def shifted(x_ref, col0):
        w = x_ref.shape[1]
        x = x_ref[...].astype(F32)
        carried = jnp.broadcast_to(shift_ref[:, :, col0:col0 + w].astype(F32), (n_b, t, w)).reshape(rows, w)
        prev = jnp.where(first_tok, carried, pltpu.roll(x, 1, 0))
        so_ref[:, :, col0:col0 + w] = x.reshape(n_b, t, w)[:, t - 1:t, :]
        return x + (prev - x) * mu_ref[:, col0:col0 + w]

    def put(o_ref, x):
        by_tok = jnp.swapaxes(x.reshape(n_b, t, RWKV_W), 0, 1)
        for ti in range(t):
            feat = by_tok[ti].T.reshape(RWKV_N, RWKV_HEADS, n_b)
            o_ref[:, ti] = jnp.swapaxes(feat, 0, 1)

    put(ro_ref, shifted(r_ref, 0))
    put(ko_ref, shifted(k_ref, RWKV_W))
    put(vo_ref, shifted(v_ref, 2 * RWKV_W))
    decay, rate, gate = _rwkv_lora_terms(shifted(lo_ref, 3 * RWKV_W), w0_ref, w2_ref, a0_ref, a2_ref, g2_ref)
    put(wo_ref, decay)
    put(ao_ref, rate)
    go_ref[...] = gate


def _rwkv_pre_short(proj, s_shift, mu, w0, w2, a0, a2, g2, *, row0, n_batch, t):
    rows = n_batch * t
    assert row0 % rows == 0 and t & (t - 1) == 0
    blk0 = row0 // rows
    col_r = N_RET_COLS // RWKV_W
    col_lo = (N_RET_COLS + 3 * RWKV_W) // LORA_COLS
    state_spec = pl.BlockSpec((n_batch, 1, N_RWKV_COLS), lambda i: (0, 0, 0))
    in_specs = [pl.BlockSpec((rows, RWKV_W), lambda i: (blk0, col_r)),
                pl.BlockSpec((rows, RWKV_W), lambda i: (blk0, col_r + 1)),
                pl.BlockSpec((rows, RWKV_W), lambda i: (blk0, col_r + 2)),
                pl.BlockSpec((rows, LORA_COLS), lambda i: (blk0, col_lo)),
                state_spec] + _rwkv_pre_weight_specs(lambda i: (0, 0))
    vec_shape = (RWKV_HEADS, t, RWKV_N, n_batch)
    vec_spec = pl.BlockSpec(vec_shape, lambda i: (0, 0, 0, 0))
    return pl.pallas_call(
        functools.partial(_rwkv_pre_short_body, n_b=n_batch, t=t),
        grid=(1,),
        in_specs=in_specs,
        out_specs=[vec_spec] * 5 + [pl.BlockSpec((rows, RWKV_W), lambda i: (0, 0)), state_spec],
        out_shape=[jax.ShapeDtypeStruct(vec_shape, F32)] * 5 + [
            jax.ShapeDtypeStruct((rows, RWKV_W), F32), jax.ShapeDtypeStruct((n_batch, 1, N_RWKV_COLS), F32)],
        compiler_params=_params("arbitrary"),
    )(proj, proj, proj, proj, *_rwkv_pre_args(s_shift, n_batch, mu, w0, w2, a0, a2, g2))


def _scan_body(r_ref, k_ref, w_ref, a_ref, v_ref, kk_ref, ka_ref, rk_ref, lw_ref, lb_ref, s0_ref,
               y_ref, sout_ref, s_scr, a_scr, b_scr, km_scr, *, tc, vr, halves):
    ci = pl.program_id(1)

    @pl.when(ci == 0)
    def _():
        s_scr[...] = s0_ref[...].astype(F32)

    def ksum(x):
        return jnp.sum(x, axis=-2, keepdims=True)

    def vsum(x):
        if halves == 2:
            x2 = x.reshape(tc * vr, LANES)
            x = (x2 + pltpu.roll(x2, LANES // 2, 1)).reshape(tc, vr, LANES)
        return jnp.sum(x, axis=1, keepdims=True)

    kr = k_ref[...]
    a = a_ref[...]
    kk = kr * kk_ref[...]
    kk = kk / jnp.maximum(jnp.sqrt(ksum(kk * kk)), 1e-12)
    a_scr[...] = -kk
    b_scr[...] = kk * a
    km_scr[...] = kr * (1.0 + (a - 1.0) * ka_ref[...])

    def token(t, carry):
        r, w, avec, bvec, kmod = r_ref[t], w_ref[t], a_scr[t], b_scr[t], km_scr[t]

        def value_row(i, c2):
            s = s_scr[i]
            sa = ksum(s * avec)
            s = s * w + sa * bvec + v_ref[t, pl.ds(i, 1), :] * kmod
            s_scr[i] = s
            y_ref[t, pl.ds(i, 1), :] = ksum(s * r)
            return c2

        lax.fori_loop(0, vr, value_row, 0, unroll=16)
        return carry

    lax.fori_loop(0, tc, token, 0)

    y = y_ref[...]
    d = y - vsum(y) * (1.0 / RWKV_N)
    var = vsum(d * d) * (1.0 / RWKV_N)
    bonus = ksum(r_ref[...] * km_scr[...] * rk_ref[...])
    y_ref[...] = d * lax.rsqrt(var + LNX_EPS) * lw_ref[...] + lb_ref[...] + bonus * v_ref[...]

    @pl.when(ci == pl.num_programs(1) - 1)
    def _():
        sout_ref[...] = s_scr[...]


def _rwkv_scan(r, k, w, a, v, k_k, k_a, r_k, lnx_w, lnx_b, s0, *, tc, halves):
    n_grp, t, _, lanes = r.shape
    vr = v.shape[2]
    assert lanes == LANES and t % tc == 0 and vr * halves == RWKV_N

    def tok_spec(rows):
        return pl.BlockSpec((None, tc, rows, LANES), lambda g, ci: (g, ci, 0, 0))

    def par_spec(rows):
        return pl.BlockSpec((None, rows, LANES), lambda g, ci: (g, 0, 0))

    st_spec = pl.BlockSpec((None, vr, RWKV_N, LANES), lambda g, ci: (g, 0, 0, 0))
    key_scratch = pltpu.VMEM((tc, RWKV_N, LANES), F32)
    return pl.pallas_call(
        functools.partial(_scan_body, tc=tc, vr=vr, halves=halves),
        grid=(n_grp, t // tc),
        in_specs=[tok_spec(RWKV_N)] * 4 + [tok_spec(vr)] + [par_spec(RWKV_N)] * 3 + [par_spec(vr)] * 2 + [st_spec],
        out_specs=[tok_spec(vr), st_spec],
        out_shape=[jax.ShapeDtypeStruct((n_grp, t, vr, LANES), F32),
                   jax.ShapeDtypeStruct((n_grp, vr, RWKV_N, LANES), F32)],
        scratch_shapes=[pltpu.VMEM((vr, RWKV_N, LANES), F32), key_scratch, key_scratch, key_scratch],
        compiler_params=_params("parallel", "arbitrary"),
    )(r, k, w, a, v, k_k, k_a, r_k, lnx_w, lnx_b, s0)


SCAN_TC = 128
SCAN_SUB = 64
MEM_CHUNKS = 4


def _scan_prompt_body(r_ref, k_ref, w_ref, a_ref, v_ref, kk_ref, ka_ref, rk_ref, lw_ref, lb_ref, s0_ref,
                      memk_hbm, memv_hbm, y_ref, sout_ref, memk_out, memv_out,
                      s_scr, r_c, w_c, a_c, b_c, km_c, v_c, y_c, stash, mem_stage, mem_in_sem, mem_out_sem,
                      *, n_b, mem_seqs):
    ci = pl.program_id(0)
    vr = RWKV_N // 2
    ts = SCAN_SUB
    tile = RWKV_HEADS
    half_lanes = LANES // 2

    @pl.when(ci == 0)
    def _():
        s_scr[...] = s0_ref[...].astype(F32)

    chunk_seqs = mem_seqs // MEM_CHUNKS

    def mem_in(chunk, slot):
        copies = []
        for j in range(chunk_seqs):
            seq = (ci * MEM_CHUNKS + chunk) * chunk_seqs + j
            for h in range(MEM_HEADS):
                copies.append(pltpu.make_async_copy(memk_hbm.at[seq, :, h, :], mem_stage.at[slot, 0, j, h],
                                                    mem_in_sem.at[slot]))
                copies.append(pltpu.make_async_copy(memv_hbm.at[seq, :, h, :], mem_stage.at[slot, 1, j, h],
                                                    mem_in_sem.at[slot]))
        return copies

    def mem_out(chunk, slot):
        seqs = pl.ds((ci * MEM_CHUNKS + chunk) * chunk_seqs, chunk_seqs)
        return [pltpu.make_async_copy(mem_stage.at[slot, 0], memk_out.at[seqs], mem_out_sem.at[slot]),
                pltpu.make_async_copy(mem_stage.at[slot, 1], memv_out.at[seqs], mem_out_sem.at[slot])]

    def mem_phase(p):
        if 1 <= p <= MEM_CHUNKS:
            for cp in mem_in(p - 1, (p - 1) % 2):
                cp.wait()
            for cp in mem_out(p - 1, (p - 1) % 2):
                cp.start()
        if 2 <= p <= MEM_CHUNKS + 1:
            for cp in mem_out(p - 2, p % 2):
                cp.wait()
        if p < MEM_CHUNKS:
            for cp in mem_in(p, p % 2):
                cp.start()

    mem_phase(0)

    low = lax.broadcasted_iota(jnp.int32, (ts, LANES), 1) < half_lanes

    def feature_pair_rows(x_ref, base):
        tiles = [x_ref[b, pl.ds(base + f * tile, tile), :] for f in range(2) for b in range(n_b)]
        return jnp.concatenate(tiles, axis=0).T

    def key_to_chain(x_ref, dst, stash, t0):
        def group(g, c):
            rows = []
            for j in range(4):
                pair = g * 4 + j
                if t0 == 0:
                    full = feature_pair_rows(x_ref, pl.multiple_of(pair * 2 * tile, 2 * tile))
                    mt = full[:ts]
                    stash[pair] = full[ts:]
                else:
                    mt = stash[pair]
                sw = pltpu.roll(mt, half_lanes, 1)
                rows += [jnp.where(low, mt, sw), jnp.where(low, sw, mt)]
            dst[:, pl.ds(pl.multiple_of(g * 8, 8), 8), :] = jnp.swapaxes(jnp.stack(rows, axis=0), 0, 1)
            return c
        lax.fori_loop(0, RWKV_N // 8, group, 0)

    def value_to_chain(g, c):
        rows = [feature_pair_rows(v_ref, pl.multiple_of((g * 8 + j) * 2 * tile, 2 * tile)) for j in range(8)]
        v_c[:, pl.ds(pl.multiple_of(g * 8, 8), 8), :] = jnp.swapaxes(jnp.stack(rows, axis=0), 0, 1)
        return c
    lax.fori_loop(0, vr // 8, value_to_chain, 0)

    def ksum(x):
        return jnp.sum(x, axis=-2, keepdims=True)

    for t0 in range(0, SCAN_TC, ts):
        key_to_chain(r_ref, r_c, stash.at[0], t0)
        key_to_chain(w_ref, w_c, stash.at[1], t0)
        key_to_chain(k_ref, km_c, stash.at[2], t0)
        key_to_chain(a_ref, b_c, stash.at[3], t0)

        def prep(g8, c):
            toks = pl.ds(pl.multiple_of(g8 * 8, 8), 8)
            kr = km_c[toks]
            a = b_c[toks]
            kk = kr * kk_ref[...]
            kk = kk / jnp.maximum(jnp.sqrt(ksum(kk * kk)), 1e-12)
            a_c[toks] = -kk
            b_c[toks] = kk * a
            km_c[toks] = kr * (1.0 + (a - 1.0) * ka_ref[...])
            return c
        lax.fori_loop(0, ts // 8, prep, 0)
        mem_phase(1 + 2 * (t0 // ts))

        def token(t, carry):
            r, w, avec, bvec, kmod = r_c[t], w_c[t], a_c[t], b_c[t], km_c[t]

            def value_row(i, c2):
                s = s_scr[i]
                sa = ksum(s * avec)
                s = s * w + sa * bvec + v_c[t0 + t, pl.ds(i, 1), :] * kmod
                s_scr[i] = s
                y_c[t0 + t, pl.ds(i, 1), :] = ksum(s * r)
                return c2

            lax.fori_loop(0, vr, value_row, 0, unroll=True)
            return carry

        lax.fori_loop(0, ts, token, 0)
        mem_phase(2 + 2 * (t0 // ts))

        def post(g8, c):
            ktoks = pl.ds(pl.multiple_of(g8 * 8, 8), 8)
            vtoks = pl.ds(pl.multiple_of(t0 + g8 * 8, 8), 8)

            def vsum(x):
                x2 = x.reshape(8 * vr, LANES)
                x2 = x2 + pltpu.roll(x2, half_lanes, 1)
                return jnp.sum(x2.reshape(8, vr, LANES), axis=1, keepdims=True)

            y = y_c[vtoks]
            d = y - vsum(y) * (1.0 / RWKV_N)
            var = vsum(d * d) * (1.0 / RWKV_N)
            bonus = ksum(r_c[ktoks] * km_c[ktoks] * rk_ref[...])
            y_c[vtoks] = d * lax.rsqrt(var + LNX_EPS) * lw_ref[...] + lb_ref[...] + bonus * v_c[vtoks]
            return c
        lax.fori_loop(0, ts // 8, post, 0)

    def value_from_chain(g, c):
        blk = jnp.swapaxes(y_c[:, pl.ds(pl.multiple_of(g * 8, 8), 8), :], 0, 1)
        for j in range(8):
            mt = blk[j].T
            base = pl.multiple_of((g * 8 + j) * 2 * tile, 2 * tile)
            for hf in range(2):
                for b in range(n_b):
                    row0 = (hf * n_b + b) * tile
                    y_ref[b, pl.ds(base + hf * tile, tile), :] = mt[row0:row0 + tile, :]
        return c
    lax.fori_loop(0, vr // 8, value_from_chain, 0)

    assert 2 * (SCAN_TC // ts) == MEM_CHUNKS
    mem_phase(MEM_CHUNKS + 1)

    @pl.when(ci == pl.num_programs(0) - 1)
    def _():
        sout_ref[...] = s_scr[...]


def _rwkv_scan_prompt(r, k, w, a, v, k_k, k_a, r_k, lnx_w, lnx_b, s0, mem_k, mem_v):
    n_b, _, t = r.shape
    vr = RWKV_N // 2
    n_steps = t // SCAN_TC
    n_mem_seq = mem_k.shape[0]
    assert t % SCAN_TC == 0 and 2 * n_b * RWKV_HEADS == LANES and n_mem_seq % (n_steps * MEM_CHUNKS) == 0
    chunk_seqs = n_mem_seq // (n_steps * MEM_CHUNKS)
    any_spec = pl.BlockSpec(memory_space=pl.ANY)
    mem_shape = jax.ShapeDtypeStruct((n_mem_seq, MEM_HEADS, N_MEM, MEM_DH), F32)
    tok_spec = pl.BlockSpec((n_b, RWKV_W, SCAN_TC), lambda ci: (0, 0, ci))
    key_par = pl.BlockSpec((RWKV_N, LANES), lambda ci: (0, 0))
    val_par = pl.BlockSpec((vr, LANES), lambda ci: (0, 0))
    st_spec = pl.BlockSpec((vr, RWKV_N, LANES), lambda ci: (0, 0, 0))
    key_chain = pltpu.VMEM((SCAN_SUB, RWKV_N, LANES), F32)
    val_chain = pltpu.VMEM((SCAN_TC, vr, LANES), F32)
    return pl.pallas_call(
        functools.partial(_scan_prompt_body, n_b=n_b, mem_seqs=n_mem_seq // n_steps),
        grid=(n_steps,),
        in_specs=[tok_spec] * 5 + [key_par] * 3 + [val_par] * 2 + [st_spec, any_spec, any_spec],
        out_specs=[tok_spec, st_spec, any_spec, any_spec],
        out_shape=[jax.ShapeDtypeStruct((n_b, RWKV_W, t), F32),
                   jax.ShapeDtypeStruct((vr, RWKV_N, LANES), F32), mem_shape, mem_shape],
        scratch_shapes=([pltpu.VMEM((vr, RWKV_N, LANES), F32)] + [key_chain] * 5 + [val_chain] * 2
                        + [pltpu.VMEM((4, RWKV_N // 2, SCAN_TC - SCAN_SUB, LANES), F32),
                           pltpu.VMEM((2, 2, chunk_seqs, MEM_HEADS, N_MEM, MEM_DH), F32),
                           pltpu.SemaphoreType.DMA((2,)), pltpu.SemaphoreType.DMA((2,))]),
        compiler_params=_params("arbitrary"),
    )(r, k, w, a, v, k_k, k_a, r_k, lnx_w, lnx_b, s0, mem_k, mem_v)


def _merge_body(*refs, part_counts):
    n = len(part_counts)
    x_refs, oret_refs, g_refs = refs[:n], refs[n:2 * n], refs[2 * n:3 * n]
    yt_ref, ys_ref, wt_ref, wb_ref, o_ref = refs[3 * n:]
    x = _read_row_parts(x_refs, part_counts)
    y = jnp.where(pl.program_id(0) >= part_counts[0], ys_ref[...], yt_ref[...].T)
    yb = (y * _read_row_parts(g_refs, part_counts)).astype(BF16)
    ob = _read_row_parts(oret_refs, part_counts).astype(BF16)
    n_chunk = 256
    for j in range(0, D_MODEL, n_chunk):
        acc = jnp.dot(ob, wt_ref[:, j:j + n_chunk], preferred_element_type=F32)
        acc = acc + jnp.dot(yb, wb_ref[:, j:j + n_chunk], preferred_element_type=F32)
        o_ref[:, j:j + n_chunk] = x[:, j:j + n_chunk] + acc


def _merge(x_parts, oret_parts, g_parts, y_first_t, y_second, w_ret, w_rwkv):
    m = sum(part.shape[0] for part in x_parts)
    in_specs, part_counts = [], None
    for parts in (x_parts, oret_parts, g_parts):
        specs, part_counts = _row_part_specs(parts, TM)
        in_specs += specs
    assert len(part_counts) == 2
    n_first = part_counts[0]
    tiles = y_first_t.shape[2] // TM
    assert y_first_t.shape[0] * tiles == n_first
    yt_spec = pl.BlockSpec((None, RWKV_W, TM),
                           lambda i: (jnp.minimum(i, n_first - 1) // tiles, 0, jnp.minimum(i, n_first - 1) % tiles))
    ys_spec = pl.BlockSpec((TM, RWKV_W), lambda i: (jnp.clip(i - n_first, 0, part_counts[1] - 1), 0))
    wspec = pl.BlockSpec((RET_W, D_MODEL), lambda i: (0, 0))
    return pl.pallas_call(
        functools.partial(_merge_body, part_counts=part_counts),
        grid=(m // TM,),
        in_specs=in_specs + [yt_spec, ys_spec, wspec, wspec],
        out_specs=pl.BlockSpec((TM, D_MODEL), lambda i: (i, 0)),
        out_shape=jax.ShapeDtypeStruct((m, D_MODEL), F32),
        compiler_params=_params("parallel"),
    )(*x_parts, *oret_parts, *g_parts, y_first_t, y_second, w_ret.astype(BF16), w_rwkv.astype(BF16))


def _mem_kv_body(x_ref, g_ref, wk_ref, wv_ref, k_ref, v_ref, kh_ref, vh_ref, *, n_seq):
    x = x_ref[...].astype(F32)
    xb = (x * lax.rsqrt(jnp.mean(x * x, axis=-1, keepdims=True) + EPS) * g_ref[...]).astype(BF16)
    for w_ref, o_ref, oh_ref in ((wk_ref, k_ref, kh_ref), (wv_ref, v_ref, vh_ref)):
        for h in range(MEM_HEADS):
            acc = jnp.dot(xb, w_ref[:, h * MEM_DH:(h + 1) * MEM_DH], preferred_element_type=F32)
            o_ref[:, h, :] = acc
            for s in range(n_seq):
                oh_ref[s, h] = acc[s * N_MEM:(s + 1) * N_MEM]


def _mem_kv(mem, gain, w_k, w_v):
    n_b = mem.shape[0]
    n_seq = TM // N_MEM
    assert n_seq * N_MEM == TM and n_b % n_seq == 0
    wspec = pl.BlockSpec((D_MODEL, D_MODEL), lambda i: (0, 0))
    tok_spec = pl.BlockSpec((TM, MEM_HEADS, MEM_DH), lambda i: (i, 0, 0))
    head_spec = pl.BlockSpec((n_seq, MEM_HEADS, N_MEM, MEM_DH), lambda i: (i, 0, 0, 0))
    tok_shape = jax.ShapeDtypeStruct((n_b * N_MEM, MEM_HEADS, MEM_DH), F32)
    head_shape = jax.ShapeDtypeStruct((n_b, MEM_HEADS, N_MEM, MEM_DH), F32)
    return pl.pallas_call(
        functools.partial(_mem_kv_body, n_seq=n_seq),
        grid=(n_b // n_seq,),
        in_specs=[pl.BlockSpec((TM, D_MODEL), lambda i: (i, 0)), pl.BlockSpec((1, D_MODEL), lambda i: (0, 0)),
                  wspec, wspec],
        out_specs=[tok_spec, tok_spec, head_spec, head_spec],
        out_shape=[tok_shape, tok_shape, head_shape, head_shape],
        compiler_params=_params("parallel"),
    )(mem.reshape(n_b * N_MEM, D_MODEL), gain.reshape(1, D_MODEL), w_k.astype(BF16), w_v.astype(BF16))


def _attn_body(q_ref, k_ref, v_ref, o_ref, *, n_seq, tq):
    nt = (((1,), (1,)), ((), ()))
    for g in range(n_seq):
        rows = slice(g * tq, (g + 1) * tq)
        q = q_ref[rows, :].astype(BF16)
        s = lax.dot_general(q, k_ref[g].astype(BF16), nt, preferred_element_type=F32) * (MEM_DH ** -0.5)
        p = jnp.exp(s - jnp.max(s, axis=-1, keepdims=True))
        l = jnp.sum(p, axis=-1, keepdims=True)
        o = jnp.dot(p.astype(BF16), v_ref[g].astype(BF16), preferred_element_type=F32)
        o_ref[rows, :] = o / l


def _attention(q, mem_k, mem_v, *, row0, n_batch, t, n_seq, tq):
    q_tiles = t // tq
    rows = n_seq * tq
    assert t % tq == 0 and n_batch % n_seq == 0 and row0 % rows == 0 and (n_seq == 1 or q_tiles == 1)
    blk0 = row0 // rows
    kv_spec = pl.BlockSpec((n_seq, None, N_MEM, MEM_DH), lambda b, h, qi: (b, h, 0, 0))
    return pl.pallas_call(
        functools.partial(_attn_body, n_seq=n_seq, tq=tq),
        grid=(n_batch // n_seq, MEM_HEADS, q_tiles),
        in_specs=[pl.BlockSpec((rows, MEM_DH), lambda b, h, qi: (blk0 + b * q_tiles + qi, h)), kv_spec, kv_spec],
        out_specs=pl.BlockSpec((rows, MEM_DH), lambda b, h, qi: (b * q_tiles + qi, h)),
        out_shape=jax.ShapeDtypeStruct((n_batch * t, D_MODEL), F32),
        compiler_params=_params("parallel", "parallel", "parallel"),
    )(q, mem_k, mem_v)


ROW_TILE = (D_MODEL // LANES, LANES)


def _rows_to_tiles(x):
    chunks = [x[:, j * LANES:(j + 1) * LANES] for j in range(ROW_TILE[0])]
    return jnp.swapaxes(jnp.stack(chunks, axis=0), 0, 1)


def _tiles_to_rows(x):
    chunks = jnp.swapaxes(x, 0, 1)
    return jnp.concatenate([chunks[j] for j in range(ROW_TILE[0])], axis=1)


def _router_body(h_ref, g_ref, w_ref, b_ref, hn_ref, ids_ref, comb_ref):
    x = h_ref[...]
    hn = x * lax.rsqrt(jnp.mean(x * x, axis=-1, keepdims=True) + EPS) * g_ref[...]
    hn_ref[...] = _rows_to_tiles(hn)
    logits = jnp.dot(hn, w_ref[...], precision=lax.Precision.HIGHEST, preferred_element_type=F32) + b_ref[...]
    lane = lax.broadcasted_iota(jnp.int32, logits.shape, 1).astype(F32)
    neg = -jnp.inf

    def first_argmax(vals):
        m = jnp.max(vals, axis=-1, keepdims=True)
        return m, jnp.min(jnp.where(vals == m, lane, float(LANES)), axis=-1, keepdims=True)

    gl = jnp.where(lane < N_GROUPS, logits, neg)
    gmax, gsel = first_argmax(gl)
    pg_sel = 1.0 / jnp.sum(jnp.exp(gl - gmax), axis=-1, keepdims=True)
    e0 = N_GROUPS + gsel * EXP_PER_GROUP
    el = jnp.where((lane >= e0) & (lane < e0 + EXP_PER_GROUP), logits, neg)
    m1, i1 = first_argmax(el)
    m2, i2 = first_argmax(jnp.where(lane == i1, neg, el))
    e21 = jnp.exp(m2 - m1)
    c1 = pg_sel / (1.0 + e21)
    c2 = c1 * e21
    ids = jnp.where(lane == 0, i1 - N_GROUPS, jnp.where(lane == 1, i2 - N_GROUPS, 0.0))
    ids_ref[...] = ids.astype(jnp.int32)
    comb_ref[...] = jnp.where(lane == 0, c1, jnp.where(lane == 1, c2, 0.0))


def _router(h, g_ffn, w_gr, b_gr, w_er, b_er):
    m = h.shape[0]
    pad = LANES - N_GROUPS - N_EXPERTS
    w = jnp.concatenate([w_gr, w_er, jnp.zeros((D_MODEL, pad), F32)], axis=1)
    b = jnp.concatenate([b_gr, b_er, jnp.zeros((pad,), F32)]).reshape(1, LANES)
    row = lambda n: pl.BlockSpec((TM, n), lambda i: (i, 0))
    return pl.pallas_call(
        _router_body,
        grid=(m // TM,),
        in_specs=[row(D_MODEL), pl.BlockSpec((1, D_MODEL), lambda i: (0, 0)),
                  pl.BlockSpec((D_MODEL, LANES), lambda i: (0, 0)), pl.BlockSpec((1, LANES), lambda i: (0, 0))],
        out_specs=[pl.BlockSpec((TM,) + ROW_TILE, lambda i: (i, 0, 0)), row(LANES), row(LANES)],
        out_shape=[jax.ShapeDtypeStruct((m,) + ROW_TILE, F32), jax.ShapeDtypeStruct((m, LANES), jnp.int32),
                   jax.ShapeDtypeStruct((m, LANES), F32)],
        compiler_params=_params("parallel"),
    )(h, g_ffn.reshape(1, D_MODEL), w, b)


def _dispatch_body(dest_ref, hn_ref, sorted_in, sorted_out, sem):
    del sorted_in

    def issue(r, c):
        for k in range(TOP_K):
            pltpu.make_async_copy(hn_ref.at[r], sorted_out.at[dest_ref[0, k, r]], sem.at[k]).start(priority=k)
        return c
    lax.fori_loop(0, TM, issue, 0, unroll=8)
    for k in range(TOP_K):
        pltpu.make_async_copy(hn_ref, sorted_out.at[pl.ds(0, TM)], sem.at[k]).wait()


def _dispatch(hn, dest, n_sorted):
    n_tok = hn.shape[0]
    return pl.pallas_call(
        _dispatch_body,
        grid=(n_tok // TM,),
        in_specs=[pl.BlockSpec((1, TOP_K, TM), lambda i: (i, 0, 0), memory_space=pltpu.SMEM),
                  pl.BlockSpec((TM,) + ROW_TILE, lambda i: (i, 0, 0)),
                  pl.BlockSpec(memory_space=pl.ANY)],
        out_specs=pl.BlockSpec(memory_space=pl.ANY),
        out_shape=jax.ShapeDtypeStruct((n_sorted,) + ROW_TILE, F32),
        scratch_shapes=[pltpu.SemaphoreType.DMA((TOP_K,))],
        input_output_aliases={2: 0},
        compiler_params=_params("arbitrary"),
    )(dest, hn, jnp.zeros((n_sorted,) + ROW_TILE, F32))


def _expert_body(blk_e_ref, x_ref, wg_ref, wu_ref, wd_ref, o_ref):
    del blk_e_ref
    x = _tiles_to_rows(x_ref[...]).astype(BF16)
    hg = jnp.dot(x, wg_ref[0].astype(BF16), preferred_element_type=F32)
    hu = jnp.dot(x, wu_ref[0].astype(BF16), preferred_element_type=F32)
    act = (hg * jax.nn.sigmoid(hg) * hu).astype(BF16)
    o_ref[...] = _rows_to_tiles(jnp.dot(act, wd_ref[0].astype(BF16), preferred_element_type=F32))


def _experts(x_sorted, blk_e, w_gate, w_up, w_down):
    n_blocks = blk_e.shape[0]
    row_spec = pl.BlockSpec((MOE_ROWS,) + ROW_TILE, lambda i, be: (i, 0, 0))
    grid_spec = pltpu.PrefetchScalarGridSpec(
        num_scalar_prefetch=1,
        grid=(n_blocks,),
        in_specs=[
            row_spec,
            pl.BlockSpec((1, D_MODEL, D_EXPERT), lambda i, be: (be[i], 0, 0)),
            pl.BlockSpec((1, D_MODEL, D_EXPERT), lambda i, be: (be[i], 0, 0)),
            pl.BlockSpec((1, D_EXPERT, D_MODEL), lambda i, be: (be[i], 0, 0)),
        ],
        out_specs=row_spec,
    )
    return pl.pallas_call(
        _expert_body,
        grid_spec=grid_spec,
        out_shape=jax.ShapeDtypeStruct(x_sorted.shape, F32),
        compiler_params=_params("arbitrary"),
    )(blk_e, x_sorted, w_gate, w_up, w_down)


def _route_plan(ids):
    n_tok = ids.shape[0]
    n_pairs = ids.size
    n_blocks = -(-(n_pairs + N_EXPERTS * (MOE_ROWS - 1)) // MOE_ROWS)
    flat_e = ids.reshape(n_pairs)
    onehot = (flat_e[:, None] == jnp.arange(N_EXPERTS, dtype=jnp.int32)[None, :]).astype(jnp.int32)
    csum = jnp.cumsum(onehot, axis=0)
    rank = jnp.sum(onehot * csum, axis=1) - 1
    counts = csum[-1]
    pcounts = (counts + MOE_ROWS - 1) // MOE_ROWS * MOE_ROWS
    pends = jnp.cumsum(pcounts)
    pstarts = pends - pcounts
    dest = jnp.sum(onehot * pstarts[None, :], axis=1) + rank
    block_start = jnp.arange(n_blocks, dtype=jnp.int32) * MOE_ROWS
    blk_e = jnp.minimum(jnp.sum((block_start[:, None] >= pends[None, :]).astype(jnp.int32), axis=1),
                        N_EXPERTS - 1).astype(jnp.int32)
    dest = dest.astype(jnp.int32).reshape(n_tok // TM, TM, TOP_K).transpose(0, 2, 1)
    return blk_e, dest, n_blocks * MOE_ROWS


def _final_body(dest_ref, dest_next_ref, h_ref, comb_ref, g_ref, y_hbm, o_first, o_second, ybuf, sem, *, n_first):
    i = pl.program_id(0)
    n = pl.num_programs(0)
    slot = i % 2

    def start_gather(ref, sl):
        def issue(r, c):
            for k in range(TOP_K):
                pltpu.make_async_copy(y_hbm.at[ref[0, k, r]], ybuf.at[sl, k, r], sem.at[sl]).start(priority=k)
            return c
        lax.fori_loop(0, TM, issue, 0, unroll=8)

    @pl.when(i == 0)
    def _():
        start_gather(dest_ref, 0)

    @pl.when(i + 1 < n)
    def _():
        start_gather(dest_next_ref, 1 - slot)

    for k in range(TOP_K):
        pltpu.make_async_copy(y_hbm.at[pl.ds(0, TM)], ybuf.at[slot, k], sem.at[slot]).wait()

    first, second = _tiles_to_rows(ybuf[slot, 0]), _tiles_to_rows(ybuf[slot, 1])
    x = h_ref[...] + (first * comb_ref[:, 0:1] + second * comb_ref[:, 1:2])
    out = x * lax.rsqrt(jnp.mean(x * x, axis=-1, keepdims=True) + EPS) * g_ref[...]

    @pl.when(i < n_first)
    def _():
        o_first[...] = out

    @pl.when(i >= n_first)
    def _():
        o_second[...] = out


def _final(h, y_sorted, dest, comb, g_final, *, n_first_rows):
    n_rows = h.shape[0]
    assert n_rows % TM == 0 and n_first_rows % TM == 0 and 0 < n_first_rows < n_rows
    n_steps, n_first = n_rows // TM, n_first_rows // TM
    row = lambda n: pl.BlockSpec((TM, n), lambda i: (i, 0))
    dest_spec = lambda f: pl.BlockSpec((1, TOP_K, TM), f, memory_space=pltpu.SMEM)
    return pl.pallas_call(
        functools.partial(_final_body, n_first=n_first),
        grid=(n_steps,),
        in_specs=[dest_spec(lambda i: (i, 0, 0)),
                  dest_spec(lambda i: (jnp.minimum(i + 1, n_steps - 1), 0, 0)),
                  row(D_MODEL), row(LANES), pl.BlockSpec((1, D_MODEL), lambda i: (0, 0)),
                  pl.BlockSpec(memory_space=pl.ANY)],
        out_specs=[pl.BlockSpec((TM, D_MODEL), lambda i: (jnp.minimum(i, n_first - 1), 0)),
                   pl.BlockSpec((TM, D_MODEL), lambda i: (jnp.maximum(i - n_first, 0), 0))],
        out_shape=[jax.ShapeDtypeStruct((n_first_rows, D_MODEL), F32),
                   jax.ShapeDtypeStruct((n_rows - n_first_rows, D_MODEL), F32)],
        scratch_shapes=[pltpu.VMEM((2, TOP_K, TM) + ROW_TILE, F32), pltpu.SemaphoreType.DMA((2,))],
        compiler_params=_params("arbitrary"),
    )(dest, dest, h, comb, g_final.reshape(1, D_MODEL), y_sorted)


def _reorder_last(x, shape, order):
    lead = x.shape[:-1]
    n = len(lead)
    y = x.reshape(lead + shape).transpose(tuple(range(n)) + tuple(n + o for o in order))
    return y.reshape(lead + (x.shape[-1],))


HALF_N = RWKV_N // 2


def _key_major(x):
    return _reorder_last(x, (RWKV_HEADS, RWKV_N), (1, 0))


def _key_major_inv(x):
    return _reorder_last(x, (RWKV_N, RWKV_HEADS), (1, 0))


def _value_major(x):
    return _reorder_last(x, (RWKV_HEADS, 2, HALF_N), (2, 1, 0))


def _value_major_inv(x):
    return _reorder_last(x, (HALF_N, 2, RWKV_HEADS), (2, 1, 0))


def _rwkv_cols(x, key_fn, value_fn):
    return jnp.concatenate([key_fn(x[..., :RWKV_W]), key_fn(x[..., RWKV_W:2 * RWKV_W]),
                            value_fn(x[..., 2 * RWKV_W:3 * RWKV_W]), x[..., 3 * RWKV_W:]], axis=-1)


def kernel(x_prompt, x_sample, mem_prompt, state_ret, state_rwkv, state_shift, cache_mem_k, cache_mem_v,
           g_mix, w_in, ret_gn, rwkv_mu, rwkv_w0, rwkv_w2, rwkv_a0, rwkv_a2, rwkv_g2, rwkv_k_k, rwkv_k_a,
           rwkv_r_k, rwkv_lnx_w, rwkv_lnx_b, w_out, g_mem_q, g_mem_kv, w_mq, w_mk, w_mv, w_mo, g_ffn,
           w_group_router, b_group_router, w_expert_router, b_expert_router, w_e_gate, w_e_up, w_e_down,
           g_final):
    assert w_in.shape[0] == 1, "single-layer decoder"
    bp, tp, d = x_prompt.shape
    bs, ts, _ = x_sample.shape
    np_tok, ns_tok = bp * tp, bs * ts
    assert d == D_MODEL and bp * RWKV_HEADS * 2 == LANES and bs == LANES
    l = 0
    x_parts = [x_prompt.reshape(np_tok, d), x_sample.reshape(ns_tok, d)]

    w_in_l = jnp.concatenate([w_in[l][:, :N_RET_COLS], _rwkv_cols(w_in[l][:, N_RET_COLS:], _key_major, _value_major)],
                             axis=1)
    (proj,) = _matmul(x_parts, [w_in_l.astype(BF16)], gain=g_mix[l])

    pos_p = np.arange(tp)
    pos_s = PAST_LEN + np.arange(ts)
    zero_ret = jnp.zeros((bp, RET_HEADS, RET_DK, RET_DV), F32)
    oret_p, sret_p = _retention(proj, zero_ret, ret_gn[l], pos_p, row0=0, n_batch=bp, t=tp, n_seq=1)
    oret_s, sret_s = _retention(proj, state_ret[l], ret_gn[l], pos_s, row0=np_tok, n_batch=bs, t=ts, n_seq=16)

    pre_w = (_rwkv_cols(rwkv_mu[l], _key_major, _value_major), _key_major(rwkv_w0[l]), _key_major(rwkv_w2[l]),
             _key_major(rwkv_a0[l]), _key_major(rwkv_a2[l]), _value_major(rwkv_g2[l]))
    zero_shift = jnp.zeros((bp, N_RWKV_COLS), F32)
    shift_in = _rwkv_cols(state_shift[l], _key_major, _value_major)
    r_p, k_p, v_p, w_p, a_p, gate_p, shift_p = _rwkv_pre(proj, zero_shift, *pre_w, row0=0, n_batch=bp, t=tp, c=256)
    r_s, k_s, v_s, w_s, a_s, gate_s, shift_s = _rwkv_pre_short(proj, shift_in, *pre_w, row0=np_tok, n_batch=bs,
                                                                t=ts)

    kvec = lambda v: v.reshape(RWKV_HEADS, RWKV_N)
    key_par = lambda v: jnp.broadcast_to(kvec(v).T[:, None, None, :], (RWKV_N, 2, bp, RWKV_HEADS)).reshape(
        RWKV_N, LANES)

    val_par = lambda v: jnp.broadcast_to(
        v.reshape(RWKV_HEADS, 2, HALF_N).transpose(2, 1, 0)[:, :, None, :],
        (HALF_N, 2, bp, RWKV_HEADS)).reshape(HALF_N, LANES)
    mem_shape = (N_MEM, MEM_HEADS, MEM_DH)
    y_p, srw_p, cache_k_heads, cache_v_heads = _rwkv_scan_prompt(
        r_p, k_p, w_p, a_p, v_p, key_par(rwkv_k_k[l]), key_par(rwkv_k_a[l]), key_par(rwkv_r_k[l]),
        val_par(rwkv_lnx_w[l]), val_par(rwkv_lnx_b[l]), jnp.zeros((HALF_N, RWKV_N, LANES), F32),
        cache_mem_k.reshape(bs, *mem_shape), cache_mem_v.reshape(bs, *mem_shape))
    srw_p = srw_p.reshape(HALF_N, RWKV_N, 2, bp, RWKV_HEADS).transpose(3, 4, 2, 0, 1).reshape(
        bp, RWKV_HEADS, RWKV_N, RWKV_N)
    head_par = lambda v: jnp.broadcast_to(kvec(v)[:, :, None], (RWKV_HEADS, RWKV_N, LANES))
    val_rows = lambda v: v.reshape(RWKV_HEADS, 2, HALF_N).transpose(0, 2, 1).reshape(RWKV_HEADS * RWKV_N)
    state_s = state_rwkv[l].astype(F32).reshape(bs, RWKV_HEADS, 2, HALF_N, RWKV_N).transpose(1, 3, 2, 4, 0)
    y_s, srw_s = _rwkv_scan(
        r_s, k_s, w_s, a_s, v_s, head_par(rwkv_k_k[l]), head_par(rwkv_k_a[l]), head_par(rwkv_r_k[l]),
        head_par(val_rows(rwkv_lnx_w[l])), head_par(val_rows(rwkv_lnx_b[l])),
        state_s.reshape(RWKV_HEADS, RWKV_N, RWKV_N, bs), tc=ts, halves=1)
    y_s = y_s.transpose(3, 1, 2, 0).reshape(ns_tok, RWKV_W)
    srw_s = srw_s.reshape(RWKV_HEADS, HALF_N, 2, RWKV_N, bs).transpose(4, 0, 2, 1, 3).reshape(
        bs, RWKV_HEADS, RWKV_N, RWKV_N)

    w_rwkv_out = _value_major(w_out[l][RET_W:].T).T
    h = _merge(x_parts, [oret_p, oret_s], [gate_p, gate_s], y_p, y_s, w_out[l][:RET_W], w_rwkv_out)

    mk, mv, mk_heads, mv_heads = _mem_kv(mem_prompt, g_mem_kv[l], w_mk[l], w_mv[l])
    (q,) = _matmul([h], [w_mq[l].astype(BF16)], gain=g_mem_q[l])
    att_p = _attention(q, mk_heads, mv_heads, row0=0, n_batch=bp, t=tp, n_seq=1, tq=TM)
    att_s = _attention(q, cache_k_heads, cache_v_heads, row0=np_tok, n_batch=bs, t=ts, n_seq=16, tq=ts)
    (h,) = _matmul([att_p, att_s], [w_mo[l].astype(BF16)], residual=h)

    hn, ids, comb = _router(h, g_ffn[l], w_group_router[l], b_group_router[l], w_expert_router[l],
                            b_expert_router[l])
    blk_e, dest, n_sorted = _route_plan(ids[:, :TOP_K])
    y_sorted = _experts(_dispatch(hn, dest, n_sorted), blk_e, w_e_gate[l], w_e_up[l], w_e_down[l])
    y_prompt, y_sample = _final(h, y_sorted, dest, comb, g_final, n_first_rows=np_tok)
    y_prompt = y_prompt.reshape(bp, tp, d)
    y_sample = y_sample.reshape(bs, ts, d)

    shift_p = _rwkv_cols(shift_p.reshape(bp, N_RWKV_COLS), _key_major_inv, _value_major_inv)
    shift_s = _rwkv_cols(shift_s.reshape(bs, N_RWKV_COLS), _key_major_inv, _value_major_inv)
    return (y_prompt, y_sample, sret_p[None], srw_p[None], shift_p[None],
            mk.reshape(1, bp, *mem_shape), mv.reshape(1, bp, *mem_shape),
            sret_s[None], srw_s[None], shift_s[None])
```

```python
import functools

import numpy as np
import jax
import jax.numpy as jnp
from jax import lax
from jax.experimental import pallas as pl
from jax.experimental.pallas import tpu as pltpu

F32 = jnp.float32
BF16 = jnp.bfloat16

D_MODEL = 1024
PAST_LEN = 16384
N_MEM = 256
MEM_HEADS = 4
MEM_DH = D_MODEL // MEM_HEADS
RET_HEADS = 4
RET_W = D_MODEL // 2
RET_DV = RET_W // RET_HEADS
RET_DK = RET_DV // 2
RET_QK = RET_HEADS * RET_DK
RET_CHUNK = 128
ROPE_BASE = 10000.0
RWKV_N = 64
RWKV_W = D_MODEL - RET_W
RWKV_HEADS = RWKV_W // RWKV_N
LORA_W = 64
LORA_A = 64
LORA_G = 128
LNX_EPS = 64e-5
N_RET_COLS = 2 * RET_QK + 2 * RET_W
N_RWKV_COLS = 3 * RWKV_W + LORA_W + LORA_A + LORA_G
N_IN_COLS = N_RET_COLS + N_RWKV_COLS
N_GROUPS = 4
EXP_PER_GROUP = 8
N_EXPERTS = N_GROUPS * EXP_PER_GROUP
TOP_K = 2
D_EXPERT = D_MODEL // 2
EPS = 1e-6

LANES = 128
MOE_ROWS = 256
TM = 512


def _params(*sem):
    return pltpu.CompilerParams(dimension_semantics=sem)


def _row_part_specs(parts, tm):
    specs, counts, start = [], [], 0
    for part in parts:
        nb = part.shape[0] // tm
        assert nb * tm == part.shape[0]
        specs.append(pl.BlockSpec((tm, part.shape[1]), lambda i, s=start, n=nb: (jnp.clip(i - s, 0, n - 1), 0)))
        counts.append(nb)
        start += nb
    return specs, counts


def _read_row_parts(refs, counts):
    i = pl.program_id(0)
    x = refs[0][...]
    start = counts[0]
    for ref, nb in zip(refs[1:], counts[1:]):
        x = jnp.where(i >= start, ref[...], x)
        start += nb
    return x


def _proj_body(*refs, part_counts):
    n = len(part_counts)
    g_ref, w_ref, o_ref = refs[n:]
    x = _read_row_parts(refs[:n], part_counts).astype(F32)
    xb = (x * lax.rsqrt(jnp.mean(x * x, axis=-1, keepdims=True) + EPS) * g_ref[...]).astype(BF16)
    n_chunk = 256
    for j in range(0, w_ref.shape[1], n_chunk):
        o_ref[:, j:j + n_chunk] = jnp.dot(xb, w_ref[:, j:j + n_chunk], preferred_element_type=F32)


def _input_projection(x_parts, gain, w):
    k, n_out = w.shape
    m = sum(part.shape[0] for part in x_parts)
    assert n_out % 256 == 0
    in_specs, part_counts = _row_part_specs(x_parts, TM)
    return pl.pallas_call(
        functools.partial(_proj_body, part_counts=part_counts),
        grid=(m // TM,),
        in_specs=in_specs + [pl.BlockSpec((1, k), lambda i: (0, 0)), pl.BlockSpec((k, n_out), lambda i: (0, 0))],
        out_specs=pl.BlockSpec((TM, n_out), lambda i: (i, 0)),
        out_shape=jax.ShapeDtypeStruct((m, n_out), F32),
        compiler_params=_params("parallel"),
    )(*x_parts, gain.reshape(1, k).astype(F32), w.astype(BF16))


def _rot_tables(pos):
    half = RET_DK // 2
    inv_freq = ROPE_BASE ** (-(np.arange(half, dtype=np.float64) / half))
    ang = pos.astype(np.float64)[:, None] * inv_freq[None, :]
    cos, sin = np.cos(ang), np.sin(ang)
    zero = np.zeros_like(sin)
    c = np.tile(np.concatenate([cos, cos], axis=1), (1, RET_HEADS))
    s_lo = np.tile(np.concatenate([-sin, zero], axis=1), (1, RET_HEADS))
    s_hi = np.tile(np.concatenate([zero, sin], axis=1), (1, RET_HEADS))
    return [jnp.asarray(t, F32) for t in (c, s_lo, s_hi)]


def _ret_decay_tables(c):
    lg = np.log1p(-np.exp2(-5.0 - np.arange(RET_HEADS, dtype=np.float64)))
    idx = np.arange(c, dtype=np.float64)
    diff = idx[:, None] - idx[None, :]
    mask = np.where(diff[None] >= 0, np.exp(np.maximum(diff, 0.0)[None] * lg[:, None, None]), 0.0)
    q_dec = np.repeat(np.exp((idx[:, None] + 1.0) * lg[None, :]), RET_DV, axis=1)
    k_dec = np.repeat(np.exp((c - 1.0 - idx)[:, None] * lg[None, :]), RET_DK, axis=1)
    c_dec = [float(v) for v in np.exp(c * lg)]
    return jnp.asarray(mask, F32), jnp.asarray(q_dec, F32), jnp.asarray(k_dec, F32), c_dec


def _ret_body(q_ref, k_ref, v_ref, gate_ref, c_ref, slo_ref, shi_ref, mask_ref, qdec_ref, kdec_ref,
              gn_ref, s0_ref, o_ref, sout_ref, s_scr, *, n_seq, c, c_dec):
    ci = pl.program_id(1)

    @pl.when(ci == 0)
    def _():
        s_scr[...] = s0_ref[...].astype(F32)

    cos, s_lo, s_hi = c_ref[...], slo_ref[...], shi_ref[...]
    half = RET_DK // 2

    def rope(x):
        return x * cos + pltpu.roll(x, RET_QK - half, 1) * s_lo + pltpu.roll(x, half, 1) * s_hi

    nt = (((1,), (1,)), ((), ()))
    tn = (((0,), (0,)), ((), ()))
    for g in range(n_seq):
        rows = slice(g * c, (g + 1) * c)
        q = rope(q_ref[rows, :].astype(F32))
        k = rope(k_ref[rows, :].astype(F32)) * (RET_DK ** -0.5)
        k_st = k * kdec_ref[...]
        for h in range(RET_HEADS):
            kc = slice(h * RET_DK, (h + 1) * RET_DK)
            vc = slice(h * RET_DV, (h + 1) * RET_DV)
            qh = q[:, kc].astype(BF16)
            vh = v_ref[rows, vc].astype(BF16)
            s_h = s_scr[g, h]
            att = lax.dot_general(qh, k[:, kc].astype(BF16), nt, preferred_element_type=F32) * mask_ref[h]
            o = jnp.dot(att.astype(BF16), vh, preferred_element_type=F32)
            o = o + jnp.dot(qh, s_h.astype(BF16), preferred_element_type=F32) * qdec_ref[:, vc]
            s_scr[g, h] = s_h * c_dec[h] + lax.dot_general(
                k_st[:, kc].astype(BF16), vh, tn, preferred_element_type=F32)
            o = o * lax.rsqrt(jnp.mean(o * o, axis=-1, keepdims=True) + EPS)
            gate = gate_ref[rows, vc].astype(F32)
            o_ref[rows, vc] = o * gn_ref[:, vc] * (gate * jax.nn.sigmoid(gate))

    @pl.when(ci == pl.num_programs(1) - 1)
    def _():
        sout_ref[...] = s_scr[...]


def _retention(proj, s0, ret_gn, pos, *, row0, n_batch, t, n_seq):
    c = RET_CHUNK if t % RET_CHUNK == 0 else t
    n_chunks = t // c
    rows = n_seq * c
    assert n_batch % n_seq == 0 and row0 % rows == 0 and (n_seq == 1 or n_chunks == 1)
    blk0 = row0 // rows
    mask, q_dec, k_dec, c_dec = _ret_decay_tables(c)
    cos, s_lo, s_hi = _rot_tables(pos)

    def row_map(col):
        return lambda b, ci: (blk0 + b * n_chunks + ci, col)

    def const2(b, ci):
        return (0, 0)

    state_spec = pl.BlockSpec((n_seq, RET_HEADS, RET_DK, RET_DV), lambda b, ci: (b, 0, 0, 0))
    in_specs = [
        pl.BlockSpec((rows, RET_QK), row_map(0)),
        pl.BlockSpec((rows, RET_QK), row_map(1)),
        pl.BlockSpec((rows, RET_W), row_map(1)),
        pl.BlockSpec((rows, RET_W), row_map(2)),
        pl.BlockSpec((c, RET_QK), lambda b, ci: (ci, 0)),
        pl.BlockSpec((c, RET_QK), lambda b, ci: (ci, 0)),
        pl.BlockSpec((c, RET_QK), lambda b, ci: (ci, 0)),
        pl.BlockSpec((RET_HEADS, c, c), lambda b, ci: (0, 0, 0)),
        pl.BlockSpec((c, RET_W), const2),
        pl.BlockSpec((c, RET_QK), const2),
        pl.BlockSpec((1, RET_W), const2),
        state_spec,
    ]
    return pl.pallas_call(
        functools.partial(_ret_body, n_seq=n_seq, c=c, c_dec=c_dec),
        grid=(n_batch // n_seq, n_chunks),
        in_specs=in_specs,
        out_specs=[pl.BlockSpec((rows, RET_W), lambda b, ci: (b * n_chunks + ci, 0)), state_spec],
        out_shape=[jax.ShapeDtypeStruct((n_batch * t, RET_W), F32),
                   jax.ShapeDtypeStruct((n_batch, RET_HEADS, RET_DK, RET_DV), F32)],
        scratch_shapes=[pltpu.VMEM((n_seq, RET_HEADS, RET_DK, RET_DV), F32)],
        compiler_params=_params("parallel", "arbitrary"),
    )(proj, proj, proj, proj, cos, s_lo, s_hi, mask, q_dec, k_dec, ret_gn.reshape(1, RET_W).astype(F32), s0)


LORA_COLS = LORA_W + LORA_A + LORA_G


def _rwkv_lora_terms(lo, w0_ref, w2_ref, a0_ref, a2_ref, g2_ref):
    hw = lo[:, :LORA_W]
    ha = lo[:, LORA_W:LORA_W + LORA_A]
    hg = lo[:, LORA_W + LORA_A:]
    u = w0_ref[...] + jnp.dot(jnp.tanh(hw).astype(BF16), w2_ref[...], preferred_element_type=F32)
    decay = jnp.exp(-float(np.exp(-0.5)) * jax.nn.sigmoid(u))
    rate = jax.nn.sigmoid(a0_ref[...] + jnp.dot(ha.astype(BF16), a2_ref[...], preferred_element_type=F32))
    gate = jnp.dot(jax.nn.sigmoid(hg).astype(BF16), g2_ref[...], preferred_element_type=F32)
    return decay, rate, gate


def _rwkv_pre_body(r_ref, k_ref, v_ref, lo_ref, shift_ref, mu_ref, w0_ref, w2_ref, a0_ref, a2_ref, g2_ref,
                   ro_ref, ko_ref, vo_ref, wo_ref, ao_ref, go_ref, so_ref, prev_scr):
    ci = pl.program_id(1)
    c = r_ref.shape[0]

    @pl.when(ci == 0)
    def _():
        prev_scr[...] = shift_ref[0].astype(F32)

    first_row = lax.broadcasted_iota(jnp.int32, (c, 1), 0) == 0

    def shifted(x_ref, col0):
        w = x_ref.shape[1]
        x = x_ref[...].astype(F32)
        prev = jnp.where(first_row, prev_scr[:, col0:col0 + w], pltpu.roll(x, 1, 0))
        prev_scr[:, col0:col0 + w] = x[c - 1:c, :]
        return x + (prev - x) * mu_ref[:, col0:col0 + w]

    ro_ref[...] = shifted(r_ref, 0).T
    ko_ref[...] = shifted(k_ref, RWKV_W).T
    vo_ref[...] = shifted(v_ref, 2 * RWKV_W).T
    decay, rate, gate = _rwkv_lora_terms(shifted(lo_ref, 3 * RWKV_W), w0_ref, w2_ref, a0_ref, a2_ref, g2_ref)
    wo_ref[...] = decay.T
    ao_ref[...] = rate.T
    go_ref[...] = gate

    @pl.when(ci == pl.num_programs(1) - 1)
    def _():
        so_ref[0] = prev_scr[...]


def _rwkv_pre_args(s_shift, n_batch, mu, w0, w2, a0, a2, g2):
    return (s_shift.reshape(n_batch, 1, N_RWKV_COLS), mu.reshape(1, -1), w0.reshape(1, -1), w2.astype(BF16),
            a0.reshape(1, -1), a2.astype(BF16), g2.astype(BF16))


def _rwkv_pre_weight_specs(const):
    return [pl.BlockSpec((1, N_RWKV_COLS), const), pl.BlockSpec((1, RWKV_W), const),
            pl.BlockSpec((LORA_W, RWKV_W), const), pl.BlockSpec((1, RWKV_W), const),
            pl.BlockSpec((LORA_A, RWKV_W), const), pl.BlockSpec((LORA_G, RWKV_W), const)]


def _rwkv_pre(proj, s_shift, mu, w0, w2, a0, a2, g2, *, row0, n_batch, t, c):
    n_chunks = t // c
    assert t % c == 0 and row0 % c == 0
    blk0 = row0 // c
    col_r = N_RET_COLS // RWKV_W
    col_lo = (N_RET_COLS + 3 * RWKV_W) // LORA_COLS
    assert col_r * RWKV_W == N_RET_COLS and col_lo * LORA_COLS == N_RET_COLS + 3 * RWKV_W

    def row_map(col):
        return lambda b, ci: (blk0 + b * n_chunks + ci, col)

    state_spec = pl.BlockSpec((1, 1, N_RWKV_COLS), lambda b, ci: (b, 0, 0))
    in_specs = [pl.BlockSpec((c, RWKV_W), row_map(col_r)), pl.BlockSpec((c, RWKV_W), row_map(col_r + 1)),
                pl.BlockSpec((c, RWKV_W), row_map(col_r + 2)), pl.BlockSpec((c, LORA_COLS), row_map(col_lo)),
                state_spec] + _rwkv_pre_weight_specs(lambda b, ci: (0, 0))
    vec_spec = pl.BlockSpec((None, RWKV_W, c), lambda b, ci: (b, 0, ci))
    vec_shape = jax.ShapeDtypeStruct((n_batch, RWKV_W, t), F32)
    return pl.pallas_call(
        _rwkv_pre_body,
        grid=(n_batch, n_chunks),
        in_specs=in_specs,
        out_specs=[vec_spec] * 5 + [pl.BlockSpec((c, RWKV_W), lambda b, ci: (b * n_chunks + ci, 0)), state_spec],
        out_shape=[vec_shape] * 5 + [jax.ShapeDtypeStruct((n_batch * t, RWKV_W), F32),
                                     jax.ShapeDtypeStruct((n_batch, 1, N_RWKV_COLS), F32)],
        scratch_shapes=[pltpu.VMEM((1, N_RWKV_COLS), F32)],
        compiler_params=_params("parallel", "arbitrary"),
    )(proj, proj, proj, proj, *_rwkv_pre_args(s_shift, n_batch, mu, w0, w2, a0, a2, g2))


def _rwkv_pre_short_body(r_ref, k_ref, v_ref, lo_ref, shift_ref, mu_ref, w0_ref, w2_ref, a0_ref, a2_ref, g2_ref,
                         ro_ref, ko_ref, vo_ref, wo_ref, ao_ref, go_ref, so_ref, *, n_b, t):
    rows = n_b * t
    first_tok = (lax.broadcasted_iota(jnp.int32, (rows, 1), 0) & (t - 1)) == 0

    def shifted(x_ref, col0):
        w = x_ref.shape[1]
        x = x_ref[...].astype(F32)
        carried = jnp.broadcast_to(shift_ref[:, :, col0:col0 + w].astype(F32), (n_b, t, w)).reshape(rows, w)
        prev = jnp.where(first_tok, carried, pltpu.roll(x, 1, 0))
        so_ref[:, :, col0:col0 + w] = x.reshape(n_b, t, w)[:, t - 1:t, :]
        return x + (prev - x) * mu_ref[:, col0:col0 + w]

    def put(o_ref, x):
        by_tok = jnp.swapaxes(x.reshape(n_b, t, RWKV_W), 0, 1)
        for ti in range(t):
            feat = by_tok[ti].T.reshape(RWKV_N, RWKV_HEADS, n_b)
            o_ref[:, ti] = jnp.swapaxes(feat, 0, 1)

    put(ro_ref, shifted(r_ref, 0))
    put(ko_ref, shifted(k_ref, RWKV_W))
    put(vo_ref, shifted(v_ref, 2 * RWKV_W))
    decay, rate, gate = _rwkv_lora_terms(shifted(lo_ref, 3 * RWKV_W), w0_ref, w2_ref, a0_ref, a2_ref, g2_ref)
    put(wo_ref, decay)
    put(ao_ref, rate)
    go_ref[...] = gate


def _rwkv_pre_short(proj, s_shift, mu, w0, w2, a0, a2, g2, *, row0, n_batch, t):
    rows = n_batch * t
    assert row0 % rows == 0 and t & (t - 1) == 0
    blk0 = row0 // rows
    col_r = N_RET_COLS // RWKV_W
    col_lo = (N_RET_COLS + 3 * RWKV_W) // LORA_COLS
    state_spec = pl.BlockSpec((n_batch, 1, N_RWKV_COLS), lambda i: (0, 0, 0))
    in_specs = [pl.BlockSpec((rows, RWKV_W), lambda i: (blk0, col_r)),
                pl.BlockSpec((rows, RWKV_W), lambda i: (blk0, col_r + 1)),
                pl.BlockSpec((rows, RWKV_W), lambda i: (blk0, col_r + 2)),
                pl.BlockSpec((rows, LORA_COLS), lambda i: (blk0, col_lo)),
                state_spec] + _rwkv_pre_weight_specs(lambda i: (0, 0))
    vec_shape = (RWKV_HEADS, t, RWKV_N, n_batch)
    vec_spec = pl.BlockSpec(vec_shape, lambda i: (0, 0, 0, 0))
    return pl.pallas_call(
        functools.partial(_rwkv_pre_short_body, n_b=n_batch, t=t),
        grid=(1,),
        in_specs=in_specs,
        out_specs=[vec_spec] * 5 + [pl.BlockSpec((rows, RWKV_W), lambda i: (0, 0)), state_spec],
        out_shape=[jax.ShapeDtypeStruct(vec_shape, F32)] * 5 + [
            jax.ShapeDtypeStruct((rows, RWKV_W), F32), jax.ShapeDtypeStruct((n_batch, 1, N_RWKV_COLS), F32)],
        compiler_params=_params("arbitrary"),
    )(proj, proj, proj, proj, *_rwkv_pre_args(s_shift, n_batch, mu, w0, w2, a0, a2, g2))


def _scan_body(r_ref, k_ref, w_ref, a_ref, v_ref, kk_ref, ka_ref, rk_ref, lw_ref, lb_ref, s0_ref,
               y_ref, sout_ref, s_scr, a_scr, b_scr, km_scr, *, tc, vr, halves):
    ci = pl.program_id(1)

    @pl.when(ci == 0)
    def _():
        s_scr[...] = s0_ref[...].astype(F32)

    def ksum(x):
        return jnp.sum(x, axis=-2, keepdims=True)

    def vsum(x):
        if halves == 2:
            x2 = x.reshape(tc * vr, LANES)
            x = (x2 + pltpu.roll(x2, LANES // 2, 1)).reshape(tc, vr, LANES)
        return jnp.sum(x, axis=1, keepdims=True)

    kr = k_ref[...]
    a = a_ref[...]
    kk = kr * kk_ref[...]
    kk = kk / jnp.maximum(jnp.sqrt(ksum(kk * kk)), 1e-12)
    a_scr[...] = -kk
    b_scr[...] = kk * a
    km_scr[...] = kr * (1.0 + (a - 1.0) * ka_ref[...])

    def token(t, carry):
        r, w, avec, bvec, kmod = r_ref[t], w_ref[t], a_scr[t], b_scr[t], km_scr[t]

        def value_row(i, c2):
            s = s_scr[i]
            sa = ksum(s * avec)
            s = s * w + sa * bvec + v_ref[t, pl.ds(i, 1), :] * kmod
            s_scr[i] = s
            y_ref[t, pl.ds(i, 1), :] = ksum(s * r)
            return c2

        lax.fori_loop(0, vr, value_row, 0, unroll=16)
        return carry

    lax.fori_loop(0, tc, token, 0)

    y = y_ref[...]
    d = y - vsum(y) * (1.0 / RWKV_N)
    var = vsum(d * d) * (1.0 / RWKV_N)
    bonus = ksum(r_ref[...] * km_scr[...] * rk_ref[...])
    y_ref[...] = d * lax.rsqrt(var + LNX_EPS) * lw_ref[...] + lb_ref[...] + bonus * v_ref[...]

    @pl.when(ci == pl.num_programs(1) - 1)
    def _():
        sout_ref[...] = s_scr[...]


def _rwkv_scan(r, k, w, a, v, k_k, k_a, r_k, lnx_w, lnx_b, s0, *, tc, halves):
    n_grp, t, _, lanes = r.shape
    vr = v.shape[2]
    assert lanes == LANES and t % tc == 0 and vr * halves == RWKV_N

    def tok_spec(rows):
        return pl.BlockSpec((None, tc, rows, LANES), lambda g, ci: (g, ci, 0, 0))

    def par_spec(rows):
        return pl.BlockSpec((None, rows, LANES), lambda g, ci: (g, 0, 0))

    st_spec = pl.BlockSpec((None, vr, RWKV_N, LANES), lambda g, ci: (g, 0, 0, 0))
    key_scratch = pltpu.VMEM((tc, RWKV_N, LANES), F32)
    return pl.pallas_call(
        functools.partial(_scan_body, tc=tc, vr=vr, halves=halves),
        grid=(n_grp, t // tc),
        in_specs=[tok_spec(RWKV_N)] * 4 + [tok_spec(vr)] + [par_spec(RWKV_N)] * 3 + [par_spec(vr)] * 2 + [st_spec],
        out_specs=[tok_spec(vr), st_spec],
        out_shape=[jax.ShapeDtypeStruct((n_grp, t, vr, LANES), F32),
                   jax.ShapeDtypeStruct((n_grp, vr, RWKV_N, LANES), F32)],
        scratch_shapes=[pltpu.VMEM((vr, RWKV_N, LANES), F32), key_scratch, key_scratch, key_scratch],
        compiler_params=_params("parallel", "arbitrary"),
    )(r, k, w, a, v, k_k, k_a, r_k, lnx_w, lnx_b, s0)


SCAN_TC = 128
SCAN_SUB = 64
MEM_CHUNKS = 4


def _scan_prompt_body(r_ref, k_ref, w_ref, a_ref, v_ref, kk_ref, ka_ref, rk_ref, lw_ref, lb_ref, s0_ref,
                      memk_hbm, memv_hbm, y_ref, sout_ref, memk_out, memv_out,
                      s_scr, r_c, w_c, a_c, b_c, km_c, v_c, y_c, stash, mem_stage, mem_in_sem, mem_out_sem,
                      *, n_b, mem_seqs):
    ci = pl.program_id(0)
    vr = RWKV_N // 2
    ts = SCAN_SUB
    tile = RWKV_HEADS
    half_lanes = LANES // 2

    @pl.when(ci == 0)
    def _():
        s_scr[...] = s0_ref[...].astype(F32)

    chunk_seqs = mem_seqs // MEM_CHUNKS

    def mem_in(chunk, slot):
        copies = []
        for j in range(chunk_seqs):
            seq = (ci * MEM_CHUNKS + chunk) * chunk_seqs + j
            for h in range(MEM_HEADS):
                copies.append(pltpu.make_async_copy(memk_hbm.at[seq, :, h, :], mem_stage.at[slot, 0, j, h],
                                                    mem_in_sem.at[slot]))
                copies.append(pltpu.make_async_copy(memv_hbm.at[seq, :, h, :], mem_stage.at[slot, 1, j, h],
                                                    mem_in_sem.at[slot]))
        return copies

    def mem_out(chunk, slot):
        seqs = pl.ds((ci * MEM_CHUNKS + chunk) * chunk_seqs, chunk_seqs)
        return [pltpu.make_async_copy(mem_stage.at[slot, 0], memk_out.at[seqs], mem_out_sem.at[slot]),
                pltpu.make_async_copy(mem_stage.at[slot, 1], memv_out.at[seqs], mem_out_sem.at[slot])]

    def mem_phase(p):
        if 1 <= p <= MEM_CHUNKS:
            for cp in mem_in(p - 1, (p - 1) % 2):
                cp.wait()
            for cp in mem_out(p - 1, (p - 1) % 2):
                cp.start()
        if 2 <= p <= MEM_CHUNKS + 1:
            for cp in mem_out(p - 2, p % 2):
                cp.wait()
        if p < MEM_CHUNKS:
            for cp in mem_in(p, p % 2):
                cp.start()

    mem_phase(0)

    low = lax.broadcasted_iota(jnp.int32, (ts, LANES), 1) < half_lanes

    def feature_pair_rows(x_ref, base):
        tiles = [x_ref[b, pl.ds(base + f * tile, tile), :] for f in range(2) for b in range(n_b)]
        return jnp.concatenate(tiles, axis=0).T

    def key_to_chain(x_ref, dst, stash, t0):
        def group(g, c):
            rows = []
            for j in range(4):
                pair = g * 4 + j
                if t0 == 0:
                    full = feature_pair_rows(x_ref, pl.multiple_of(pair * 2 * tile, 2 * tile))
                    mt = full[:ts]
                    stash[pair] = full[ts:]
                else:
                    mt = stash[pair]
                sw = pltpu.roll(mt, half_lanes, 1)
                rows += [jnp.where(low, mt, sw), jnp.where(low, sw, mt)]
            dst[:, pl.ds(pl.multiple_of(g * 8, 8), 8), :] = jnp.swapaxes(jnp.stack(rows, axis=0), 0, 1)
            return c
        lax.fori_loop(0, RWKV_N // 8, group, 0)

    def value_to_chain(g, c):
        rows = [feature_pair_rows(v_ref, pl.multiple_of((g * 8 + j) * 2 * tile, 2 * tile)) for j in range(8)]
        v_c[:, pl.ds(pl.multiple_of(g * 8, 8), 8), :] = jnp.swapaxes(jnp.stack(rows, axis=0), 0, 1)
        return c
    lax.fori_loop(0, vr // 8, value_to_chain, 0)

    def ksum(x):
        return jnp.sum(x, axis=-2, keepdims=True)

    for t0 in range(0, SCAN_TC, ts):
        key_to_chain(r_ref, r_c, stash.at[0], t0)
        key_to_chain(w_ref, w_c, stash.at[1], t0)
        key_to_chain(k_ref, km_c, stash.at[2], t0)
        key_to_chain(a_ref, b_c, stash.at[3], t0)

        def prep(g8, c):
            toks = pl.ds(pl.multiple_of(g8 * 8, 8), 8)
            kr = km_c[toks]
            a = b_c[toks]
            kk = kr * kk_ref[...]
            kk = kk / jnp.maximum(jnp.sqrt(ksum(kk * kk)), 1e-12)
            a_c[toks] = -kk
            b_c[toks] = kk * a
            km_c[toks] = kr * (1.0 + (a - 1.0) * ka_ref[...])
            return c
        lax.fori_loop(0, ts // 8, prep, 0)
        mem_phase(1 + 2 * (t0 // ts))

        def token(t, carry):
            r, w, avec, bvec, kmod = r_c[t], w_c[t], a_c[t], b_c[t], km_c[t]

            def value_row(i, c2):
                s = s_scr[i]
                sa = ksum(s * avec)
                s = s * w + sa * bvec + v_c[t0 + t, pl.ds(i, 1), :] * kmod
                s_scr[i] = s
                y_c[t0 + t, pl.ds(i, 1), :] = ksum(s * r)
                return c2

            lax.fori_loop(0, vr, value_row, 0, unroll=True)
            return carry

        lax.fori_loop(0, ts, token, 0)
        mem_phase(2 + 2 * (t0 // ts))

        def post(g8, c):
            ktoks = pl.ds(pl.multiple_of(g8 * 8, 8), 8)
            vtoks = pl.ds(pl.multiple_of(t0 + g8 * 8, 8), 8)

            def vsum(x):
                x2 = x.reshape(8 * vr, LANES)
                x2 = x2 + pltpu.roll(x2, half_lanes, 1)
                return jnp.sum(x2.reshape(8, vr, LANES), axis=1, keepdims=True)

            y = y_c[vtoks]
            d = y - vsum(y) * (1.0 / RWKV_N)
            var = vsum(d * d) * (1.0 / RWKV_N)
            bonus = ksum(r_c[ktoks] * km_c[ktoks] * rk_ref[...])
            y_c[vtoks] = d * lax.rsqrt(var + LNX_EPS) * lw_ref[...] + lb_ref[...] + bonus * v_c[vtoks]
            return c
        lax.fori_loop(0, ts // 8, post, 0)

    def value_from_chain(g, c):
        blk = jnp.swapaxes(y_c[:, pl.ds(pl.multiple_of(g * 8, 8), 8), :], 0, 1)
        for j in range(8):
            mt = blk[j].T
            base = pl.multiple_of((g * 8 + j) * 2 * tile, 2 * tile)
            for hf in range(2):
                for b in range(n_b):
                    row0 = (hf * n_b + b) * tile
                    y_ref[b, pl.ds(base + hf * tile, tile), :] = mt[row0:row0 + tile, :]
        return c
    lax.fori_loop(0, vr // 8, value_from_chain, 0)

    assert 2 * (SCAN_TC // ts) == MEM_CHUNKS
    mem_phase(MEM_CHUNKS + 1)

    @pl.when(ci == pl.num_programs(0) - 1)
    def _():
        sout_ref[...] = s_scr[...]


def _rwkv_scan_prompt(r, k, w, a, v, k_k, k_a, r_k, lnx_w, lnx_b, s0, mem_k, mem_v):
    n_b, _, t = r.shape
    vr = RWKV_N // 2
    n_steps = t // SCAN_TC
    n_mem_seq = mem_k.shape[0]
    assert t % SCAN_TC == 0 and 2 * n_b * RWKV_HEADS == LANES and n_mem_seq % (n_steps * MEM_CHUNKS) == 0
    chunk_seqs = n_mem_seq // (n_steps * MEM_CHUNKS)
    any_spec = pl.BlockSpec(memory_space=pl.ANY)
    mem_shape = jax.ShapeDtypeStruct((n_mem_seq, MEM_HEADS, N_MEM, MEM_DH), F32)
    tok_spec = pl.BlockSpec((n_b, RWKV_W, SCAN_TC), lambda ci: (0, 0, ci))
    key_par = pl.BlockSpec((RWKV_N, LANES), lambda ci: (0, 0))
    val_par = pl.BlockSpec((vr, LANES), lambda ci: (0, 0))
    st_spec = pl.BlockSpec((vr, RWKV_N, LANES), lambda ci: (0, 0, 0))
    key_chain = pltpu.VMEM((SCAN_SUB, RWKV_N, LANES), F32)
    val_chain = pltpu.VMEM((SCAN_TC, vr, LANES), F32)
    return pl.pallas_call(
        functools.partial(_scan_prompt_body, n_b=n_b, mem_seqs=n_mem_seq // n_steps),
        grid=(n_steps,),
        in_specs=[tok_spec] * 5 + [key_par] * 3 + [val_par] * 2 + [st_spec, any_spec, any_spec],
        out_specs=[tok_spec, st_spec, any_spec, any_spec],
        out_shape=[jax.ShapeDtypeStruct((n_b, RWKV_W, t), F32),
                   jax.ShapeDtypeStruct((vr, RWKV_N, LANES), F32), mem_shape, mem_shape],
        scratch_shapes=([pltpu.VMEM((vr, RWKV_N, LANES), F32)] + [key_chain] * 5 + [val_chain] * 2
                        + [pltpu.VMEM((4, RWKV_N // 2, SCAN_TC - SCAN_SUB, LANES), F32),
                           pltpu.VMEM((2, 2, chunk_seqs, MEM_HEADS, N_MEM, MEM_DH), F32),
                           pltpu.SemaphoreType.DMA((2,)), pltpu.SemaphoreType.DMA((2,))]),
        compiler_params=_params("arbitrary"),
    )(r, k, w, a, v, k_k, k_a, r_k, lnx_w, lnx_b, s0, mem_k, mem_v)


def _merge_body(*refs, part_counts):
    n = len(part_counts)
    x_refs, oret_refs, g_refs = refs[:n], refs[n:2 * n], refs[2 * n:3 * n]
    yt_ref, ys_ref, wt_ref, wb_ref, gq_ref, wq_ref, o_ref, q_ref = refs[3 * n:]
    x = _read_row_parts(x_refs, part_counts)
    y = jnp.where(pl.program_id(0) >= part_counts[0], ys_ref[...], yt_ref[...].T)
    yb = (y * _read_row_parts(g_refs, part_counts)).astype(BF16)
    ob = _read_row_parts(oret_refs, part_counts).astype(BF16)
    n_chunk = 256
    for j in range(0, D_MODEL, n_chunk):
        acc = jnp.dot(ob, wt_ref[:, j:j + n_chunk], preferred_element_type=F32)
        acc = acc + jnp.dot(yb, wb_ref[:, j:j + n_chunk], preferred_element_type=F32)
        o_ref[:, j:j + n_chunk] = x[:, j:j + n_chunk] + acc
    h = o_ref[...]
    hb = (h * lax.rsqrt(jnp.mean(h * h, axis=-1, keepdims=True) + EPS) * gq_ref[...]).astype(BF16)
    for j in range(0, D_MODEL, n_chunk):
        q_ref[:, j:j + n_chunk] = jnp.dot(hb, wq_ref[:, j:j + n_chunk], preferred_element_type=F32)


def _merge(x_parts, oret_parts, g_parts, y_first_t, y_second, w_ret, w_rwkv, gain_q, w_q):
    m = sum(part.shape[0] for part in x_parts)
    in_specs, part_counts = [], None
    for parts in (x_parts, oret_parts, g_parts):
        specs, part_counts = _row_part_specs(parts, TM)
        in_specs += specs
    assert len(part_counts) == 2
    n_first = part_counts[0]
    tiles = y_first_t.shape[2] // TM
    assert y_first_t.shape[0] * tiles == n_first
    yt_spec = pl.BlockSpec((None, RWKV_W, TM),
                           lambda i: (jnp.minimum(i, n_first - 1) // tiles, 0, jnp.minimum(i, n_first - 1) % tiles))
    ys_spec = pl.BlockSpec((TM, RWKV_W), lambda i: (jnp.clip(i - n_first, 0, part_counts[1] - 1), 0))
    wspec = pl.BlockSpec((RET_W, D_MODEL), lambda i: (0, 0))
    row_spec = pl.BlockSpec((TM, D_MODEL), lambda i: (i, 0))
    row_shape = jax.ShapeDtypeStruct((m, D_MODEL), F32)
    return pl.pallas_call(
        functools.partial(_merge_body, part_counts=part_counts),
        grid=(m // TM,),
        in_specs=in_specs + [yt_spec, ys_spec, wspec, wspec, pl.BlockSpec((1, D_MODEL), lambda i: (0, 0)),
                             pl.BlockSpec((D_MODEL, D_MODEL), lambda i: (0, 0))],
        out_specs=[row_spec, row_spec],
        out_shape=[row_shape, row_shape],
        compiler_params=_params("parallel"),
    )(*x_parts, *oret_parts, *g_parts, y_first_t, y_second, w_ret.astype(BF16), w_rwkv.astype(BF16),
      gain_q.reshape(1, D_MODEL), w_q.astype(BF16))


def _mem_kv_body(x_ref, g_ref, wk_ref, wv_ref, k_ref, v_ref, kh_ref, vh_ref, *, n_seq):
    x = x_ref[...].astype(F32)
    xb = (x * lax.rsqrt(jnp.mean(x * x, axis=-1, keepdims=True) + EPS) * g_ref[...]).astype(BF16)
    for w_ref, o_ref, oh_ref in ((wk_ref, k_ref, kh_ref), (wv_ref, v_ref, vh_ref)):
        for h in range(MEM_HEADS):
            acc = jnp.dot(xb, w_ref[:, h * MEM_DH:(h + 1) * MEM_DH], preferred_element_type=F32)
            o_ref[:, h, :] = acc
            for s in range(n_seq):
                oh_ref[s, h] = acc[s * N_MEM:(s + 1) * N_MEM]


def _mem_kv(mem, gain, w_k, w_v):
    n_b = mem.shape[0]
    n_seq = TM // N_MEM
    assert n_seq * N_MEM == TM and n_b % n_seq == 0
    wspec = pl.BlockSpec((D_MODEL, D_MODEL), lambda i: (0, 0))
    tok_spec = pl.BlockSpec((TM, MEM_HEADS, MEM_DH), lambda i: (i, 0, 0))
    head_spec = pl.BlockSpec((n_seq, MEM_HEADS, N_MEM, MEM_DH), lambda i: (i, 0, 0, 0))
    tok_shape = jax.ShapeDtypeStruct((n_b * N_MEM, MEM_HEADS, MEM_DH), F32)
    head_shape = jax.ShapeDtypeStruct((n_b, MEM_HEADS, N_MEM, MEM_DH), F32)
    return pl.pallas_call(
        functools.partial(_mem_kv_body, n_seq=n_seq),
        grid=(n_b // n_seq,),
        in_specs=[pl.BlockSpec((TM, D_MODEL), lambda i: (i, 0)), pl.BlockSpec((1, D_MODEL), lambda i: (0, 0)),
                  wspec, wspec],
        out_specs=[tok_spec, tok_spec, head_spec, head_spec],
        out_shape=[tok_shape, tok_shape, head_shape, head_shape],
        compiler_params=_params("parallel"),
    )(mem.reshape(n_b * N_MEM, D_MODEL), gain.reshape(1, D_MODEL), w_k.astype(BF16), w_v.astype(BF16))


def _attn_body(q_ref, k_ref, v_ref, o_ref, *, n_seq, tq):
    nt = (((1,), (1,)), ((), ()))
    for g in range(n_seq):
        rows = slice(g * tq, (g + 1) * tq)
        q = q_ref[rows, :].astype(BF16)
        s = lax.dot_general(q, k_ref[g].astype(BF16), nt, preferred_element_type=F32) * (MEM_DH ** -0.5)
        p = jnp.exp(s - jnp.max(s, axis=-1, keepdims=True))
        l = jnp.sum(p, axis=-1, keepdims=True)
        o = jnp.dot(p.astype(BF16), v_ref[g].astype(BF16), preferred_element_type=F32)
        o_ref[rows, :] = o / l


def _attention(q, mem_k, mem_v, *, row0, n_batch, t, n_seq, tq):
    q_tiles = t // tq
    rows = n_seq * tq
    assert t % tq == 0 and n_batch % n_seq == 0 and row0 % rows == 0 and (n_seq == 1 or q_tiles == 1)
    blk0 = row0 // rows
    kv_spec = pl.BlockSpec((n_seq, None, N_MEM, MEM_DH), lambda b, h, qi: (b, h, 0, 0))
    return pl.pallas_call(
        functools.partial(_attn_body, n_seq=n_seq, tq=tq),
        grid=(n_batch // n_seq, MEM_HEADS, q_tiles),
        in_specs=[pl.BlockSpec((rows, MEM_DH), lambda b, h, qi: (blk0 + b * q_tiles + qi, h)), kv_spec, kv_spec],
        out_specs=pl.BlockSpec((rows, MEM_DH), lambda b, h, qi: (b * q_tiles + qi, h)),
        out_shape=jax.ShapeDtypeStruct((n_batch * t, D_MODEL), F32),
        compiler_params=_params("parallel", "parallel", "parallel"),
    )(q, mem_k, mem_v)


ROW_TILE = (D_MODEL // LANES, LANES)


def _rows_to_tiles(x):
    chunks = [x[:, j * LANES:(j + 1) * LANES] for j in range(ROW_TILE[0])]
    return jnp.swapaxes(jnp.stack(chunks, axis=0), 0, 1)


def _tiles_to_rows(x):
    chunks = jnp.swapaxes(x, 0, 1)
    return jnp.concatenate([chunks[j] for j in range(ROW_TILE[0])], axis=1)


def _router_body(*refs, part_counts):
    n = len(part_counts)
    att_refs = refs[:n]
    wo_ref, res_ref, g_ref, w_ref, b_ref, h_ref, hn_ref, ids_ref, comb_ref = refs[n:]
    ab = _read_row_parts(att_refs, part_counts).astype(BF16)
    n_chunk = 256
    for j in range(0, D_MODEL, n_chunk):
        h_ref[:, j:j + n_chunk] = res_ref[:, j:j + n_chunk] + jnp.dot(
            ab, wo_ref[:, j:j + n_chunk], preferred_element_type=F32)
    x = h_ref[...]
    hn = x * lax.rsqrt(jnp.mean(x * x, axis=-1, keepdims=True) + EPS) * g_ref[...]
    hn_ref[...] = _rows_to_tiles(hn)
    logits = jnp.dot(hn, w_ref[...], precision=lax.Precision.HIGHEST, preferred_element_type=F32) + b_ref[...]
    lane = lax.broadcasted_iota(jnp.int32, logits.shape, 1).astype(F32)
    neg = -jnp.inf

    def first_argmax(vals):
        m = jnp.max(vals, axis=-1, keepdims=True)
        return m, jnp.min(jnp.where(vals == m, lane, float(LANES)), axis=-1, keepdims=True)

    gl = jnp.where(lane < N_GROUPS, logits, neg)
    gmax, gsel = first_argmax(gl)
    pg_sel = 1.0 / jnp.sum(jnp.exp(gl - gmax), axis=-1, keepdims=True)
    e0 = N_GROUPS + gsel * EXP_PER_GROUP
    el = jnp.where((lane >= e0) & (lane < e0 + EXP_PER_GROUP), logits, neg)
    m1, i1 = first_argmax(el)
    m2, i2 = first_argmax(jnp.where(lane == i1, neg, el))
    e21 = jnp.exp(m2 - m1)
    c1 = pg_sel / (1.0 + e21)
    c2 = c1 * e21
    ids = jnp.where(lane == 0, i1 - N_GROUPS, jnp.where(lane == 1, i2 - N_GROUPS, 0.0))
    ids_ref[...] = ids.astype(jnp.int32)
    comb_ref[...] = jnp.where(lane == 0, c1, jnp.where(lane == 1, c2, 0.0))


def _router(att_parts, w_mo, residual, g_ffn, w_gr, b_gr, w_er, b_er):
    m = residual.shape[0]
    pad = LANES - N_GROUPS - N_EXPERTS
    w = jnp.concatenate([w_gr, w_er, jnp.zeros((D_MODEL, pad), F32)], axis=1)
    b = jnp.concatenate([b_gr, b_er, jnp.zeros((pad,), F32)]).reshape(1, LANES)
    row = lambda n: pl.BlockSpec((TM, n), lambda i: (i, 0))
    att_specs, part_counts = _row_part_specs(att_parts, TM)
    return pl.pallas_call(
        functools.partial(_router_body, part_counts=part_counts),
        grid=(m // TM,),
        in_specs=att_specs + [pl.BlockSpec((D_MODEL, D_MODEL), lambda i: (0, 0)), row(D_MODEL),
                              pl.BlockSpec((1, D_MODEL), lambda i: (0, 0)),
                              pl.BlockSpec((D_MODEL, LANES), lambda i: (0, 0)),
                              pl.BlockSpec((1, LANES), lambda i: (0, 0))],
        out_specs=[row(D_MODEL), pl.BlockSpec((TM,) + ROW_TILE, lambda i: (i, 0, 0)), row(LANES), row(LANES)],
        out_shape=[jax.ShapeDtypeStruct((m, D_MODEL), F32), jax.ShapeDtypeStruct((m,) + ROW_TILE, F32),
                   jax.ShapeDtypeStruct((m, LANES), jnp.int32), jax.ShapeDtypeStruct((m, LANES), F32)],
        compiler_params=_params("parallel"),
    )(*att_parts, w_mo.astype(BF16), residual, g_ffn.reshape(1, D_MODEL), w, b)


def _dispatch_body(dest_ref, hn_ref, sorted_in, sorted_out, sem):
    del sorted_in

    def issue(r, c):
        for k in range(TOP_K):
            pltpu.make_async_copy(hn_ref.at[r], sorted_out.at[dest_ref[0, k, r]], sem.at[k]).start(priority=k)
        return c
    lax.fori_loop(0, TM, issue, 0, unroll=8)
    for k in range(TOP_K):
        pltpu.make_async_copy(hn_ref, sorted_out.at[pl.ds(0, TM)], sem.at[k]).wait()


def _dispatch(hn, dest, n_sorted):
    n_tok = hn.shape[0]
    return pl.pallas_call(
        _dispatch_body,
        grid=(n_tok // TM,),
        in_specs=[pl.BlockSpec((1, TOP_K, TM), lambda i: (i, 0, 0), memory_space=pltpu.SMEM),
                  pl.BlockSpec((TM,) + ROW_TILE, lambda i: (i, 0, 0)),
                  pl.BlockSpec(memory_space=pl.ANY)],
        out_specs=pl.BlockSpec(memory_space=pl.ANY),
        out_shape=jax.ShapeDtypeStruct((n_sorted,) + ROW_TILE, F32),
        scratch_shapes=[pltpu.SemaphoreType.DMA((TOP_K,))],
        input_output_aliases={2: 0},
        compiler_params=_params("arbitrary"),
    )(dest, hn, jnp.zeros((n_sorted,) + ROW_TILE, F32))


def _expert_body(blk_e_ref, x_ref, wg_ref, wu_ref, wd_ref, o_ref, wg_b, wu_b, wd_b):
    i = pl.program_id(0)

    @pl.when((i == 0) | (blk_e_ref[i] != blk_e_ref[jnp.maximum(i - 1, 0)]))
    def _():
        wg_b[...] = wg_ref[0].astype(BF16)
        wu_b[...] = wu_ref[0].astype(BF16)
        wd_b[...] = wd_ref[0].astype(BF16)

    x = _tiles_to_rows(x_ref[...]).astype(BF16)
    hg = jnp.dot(x, wg_b[...], preferred_element_type=F32)
    hu = jnp.dot(x, wu_b[...], preferred_element_type=F32)
    act = (hg * jax.nn.sigmoid(hg) * hu).astype(BF16)
    o_ref[...] = _rows_to_tiles(jnp.dot(act, wd_b[...], preferred_element_type=F32))


def _experts(x_sorted, blk_e, w_gate, w_up, w_down):
    n_blocks = blk_e.shape[0]
    row_spec = pl.BlockSpec((MOE_ROWS,) + ROW_TILE, lambda i, be: (i, 0, 0))
    grid_spec = pltpu.PrefetchScalarGridSpec(
        num_scalar_prefetch=1,
        grid=(n_blocks,),
        in_specs=[
            row_spec,
            pl.BlockSpec((1, D_MODEL, D_EXPERT), lambda i, be: (be[i], 0, 0)),
            pl.BlockSpec((1, D_MODEL, D_EXPERT), lambda i, be: (be[i], 0, 0)),
            pl.BlockSpec((1, D_EXPERT, D_MODEL), lambda i, be: (be[i], 0, 0)),
        ],
        out_specs=row_spec,
        scratch_shapes=[pltpu.VMEM((D_MODEL, D_EXPERT), BF16), pltpu.VMEM((D_MODEL, D_EXPERT), BF16),
                        pltpu.VMEM((D_EXPERT, D_MODEL), BF16)],
    )
    return pl.pallas_call(
        _expert_body,
        grid_spec=grid_spec,
        out_shape=jax.ShapeDtypeStruct(x_sorted.shape, F32),
        compiler_params=_params("arbitrary"),
    )(blk_e, x_sorted, w_gate, w_up, w_down)


def _route_plan(ids):
    n_tok = ids.shape[0]
    n_pairs = ids.size
    n_blocks = -(-(n_pairs + N_EXPERTS * (MOE_ROWS - 1)) // MOE_ROWS)
    flat_e = ids.reshape(n_pairs)
    onehot = (flat_e[:, None] == jnp.arange(N_EXPERTS, dtype=jnp.int32)[None, :]).astype(jnp.int32)
    csum = jnp.cumsum(onehot, axis=0)
    rank = jnp.sum(onehot * csum, axis=1) - 1
    counts = csum[-1]
    pcounts = (counts + MOE_ROWS - 1) // MOE_ROWS * MOE_ROWS
    pends = jnp.cumsum(pcounts)
    pstarts = pends - pcounts
    dest = jnp.sum(onehot * pstarts[None, :], axis=1) + rank
    block_start = jnp.arange(n_blocks, dtype=jnp.int32) * MOE_ROWS
    blk_e = jnp.minimum(jnp.sum((block_start[:, None] >= pends[None, :]).astype(jnp.int32), axis=1),
                        N_EXPERTS - 1).astype(jnp.int32)
    dest = dest.astype(jnp.int32).reshape(n_tok // TM, TM, TOP_K).transpose(0, 2, 1)
    return blk_e, dest, n_blocks * MOE_ROWS


def _final_body(dest_ref, dest_next_ref, h_ref, comb_ref, g_ref, y_hbm, o_first, o_second, ybuf, sem, *, n_first):
    i = pl.program_id(0)
    n = pl.num_programs(0)
    slot = i % 2

    def start_gather(ref, sl):
        def issue(r, c):
            for k in range(TOP_K):
                pltpu.make_async_copy(y_hbm.at[ref[0, k, r]], ybuf.at[sl, k, r], sem.at[sl]).start(priority=k)
            return c
        lax.fori_loop(0, TM, issue, 0, unroll=8)

    @pl.when(i == 0)
    def _():
        start_gather(dest_ref, 0)

    @pl.when(i + 1 < n)
    def _():
        start_gather(dest_next_ref, 1 - slot)

    for k in range(TOP_K):
        pltpu.make_async_copy(y_hbm.at[pl.ds(0, TM)], ybuf.at[slot, k], sem.at[slot]).wait()

    first, second = _tiles_to_rows(ybuf[slot, 0]), _tiles_to_rows(ybuf[slot, 1])
    x = h_ref[...] + (first * comb_ref[:, 0:1] + second * comb_ref[:, 1:2])
    out = x * lax.rsqrt(jnp.mean(x * x, axis=-1, keepdims=True) + EPS) * g_ref[...]

    @pl.when(i < n_first)
    def _():
        o_first[...] = out

    @pl.when(i >= n_first)
    def _():
        o_second[...] = out


def _final(h, y_sorted, dest, comb, g_final, *, n_first_rows):
    n_rows = h.shape[0]
    assert n_rows % TM == 0 and n_first_rows % TM == 0 and 0 < n_first_rows < n_rows
    n_steps, n_first = n_rows // TM, n_first_rows // TM
    row = lambda n: pl.BlockSpec((TM, n), lambda i: (i, 0))
    dest_spec = lambda f: pl.BlockSpec((1, TOP_K, TM), f, memory_space=pltpu.SMEM)
    return pl.pallas_call(
        functools.partial(_final_body, n_first=n_first),
        grid=(n_steps,),
        in_specs=[dest_spec(lambda i: (i, 0, 0)),
                  dest_spec(lambda i: (jnp.minimum(i + 1, n_steps - 1), 0, 0)),
                  row(D_MODEL), row(LANES), pl.BlockSpec((1, D_MODEL), lambda i: (0, 0)),
                  pl.BlockSpec(memory_space=pl.ANY)],
        out_specs=[pl.BlockSpec((TM, D_MODEL), lambda i: (jnp.minimum(i, n_first - 1), 0)),
                   pl.BlockSpec((TM, D_MODEL), lambda i: (jnp.maximum(i - n_first, 0), 0))],
        out_shape=[jax.ShapeDtypeStruct((n_first_rows, D_MODEL), F32),
                   jax.ShapeDtypeStruct((n_rows - n_first_rows, D_MODEL), F32)],
        scratch_shapes=[pltpu.VMEM((2, TOP_K, TM) + ROW_TILE, F32), pltpu.SemaphoreType.DMA((2,))],
        compiler_params=_params("arbitrary"),
    )(dest, dest, h, comb, g_final.reshape(1, D_MODEL), y_sorted)


def _reorder_last(x, shape, order):
    lead = x.shape[:-1]
    n = len(lead)
    y = x.reshape(lead + shape).transpose(tuple(range(n)) + tuple(n + o for o in order))
    return y.reshape(lead + (x.shape[-1],))


HALF_N = RWKV_N // 2


def _key_major(x):
    return _reorder_last(x, (RWKV_HEADS, RWKV_N), (1, 0))


def _key_major_inv(x):
    return _reorder_last(x, (RWKV_N, RWKV_HEADS), (1, 0))


def _value_major(x):
    return _reorder_last(x, (RWKV_HEADS, 2, HALF_N), (2, 1, 0))


def _value_major_inv(x):
    return _reorder_last(x, (HALF_N, 2, RWKV_HEADS), (2, 1, 0))


def _rwkv_cols(x, key_fn, value_fn):
    return jnp.concatenate([key_fn(x[..., :RWKV_W]), key_fn(x[..., RWKV_W:2 * RWKV_W]),
                            value_fn(x[..., 2 * RWKV_W:3 * RWKV_W]), x[..., 3 * RWKV_W:]], axis=-1)


def kernel(x_prompt, x_sample, mem_prompt, state_ret, state_rwkv, state_shift, cache_mem_k, cache_mem_v,
           g_mix, w_in, ret_gn, rwkv_mu, rwkv_w0, rwkv_w2, rwkv_a0, rwkv_a2, rwkv_g2, rwkv_k_k, rwkv_k_a,
           rwkv_r_k, rwkv_lnx_w, rwkv_lnx_b, w_out, g_mem_q, g_mem_kv, w_mq, w_mk, w_mv, w_mo, g_ffn,
           w_group_router, b_group_router, w_expert_router, b_expert_router, w_e_gate, w_e_up, w_e_down,
           g_final):
    assert w_in.shape[0] == 1, "single-layer decoder"
    bp, tp, d = x_prompt.shape
    bs, ts, _ = x_sample.shape
    np_tok, ns_tok = bp * tp, bs * ts
    assert d == D_MODEL and bp * RWKV_HEADS * 2 == LANES and bs == LANES
    l = 0
    x_parts = [x_prompt.reshape(np_tok, d), x_sample.reshape(ns_tok, d)]

    w_in_l = jnp.concatenate([w_in[l][:, :N_RET_COLS], _rwkv_cols(w_in[l][:, N_RET_COLS:], _key_major, _value_major)],
                             axis=1)
    proj = _input_projection(x_parts, g_mix[l], w_in_l)

    pos_p = np.arange(tp)
    pos_s = PAST_LEN + np.arange(ts)
    zero_ret = jnp.zeros((bp, RET_HEADS, RET_DK, RET_DV), F32)
    oret_p, sret_p = _retention(proj, zero_ret, ret_gn[l], pos_p, row0=0, n_batch=bp, t=tp, n_seq=1)
    oret_s, sret_s = _retention(proj, state_ret[l], ret_gn[l], pos_s, row0=np_tok, n_batch=bs, t=ts, n_seq=16)

    pre_w = (_rwkv_cols(rwkv_mu[l], _key_major, _value_major), _key_major(rwkv_w0[l]), _key_major(rwkv_w2[l]),
             _key_major(rwkv_a0[l]), _key_major(rwkv_a2[l]), _value_major(rwkv_g2[l]))
    zero_shift = jnp.zeros((bp, N_RWKV_COLS), F32)
    shift_in = _rwkv_cols(state_shift[l], _key_major, _value_major)
    r_p, k_p, v_p, w_p, a_p, gate_p, shift_p = _rwkv_pre(proj, zero_shift, *pre_w, row0=0, n_batch=bp, t=tp, c=256)
    r_s, k_s, v_s, w_s, a_s, gate_s, shift_s = _rwkv_pre_short(proj, shift_in, *pre_w, row0=np_tok, n_batch=bs,
                                                                t=ts)

    kvec = lambda v: v.reshape(RWKV_HEADS, RWKV_N)
    key_par = lambda v: jnp.broadcast_to(kvec(v).T[:, None, None, :], (RWKV_N, 2, bp, RWKV_HEADS)).reshape(
        RWKV_N, LANES)

    val_par = lambda v: jnp.broadcast_to(
        v.reshape(RWKV_HEADS, 2, HALF_N).transpose(2, 1, 0)[:, :, None, :],
        (HALF_N, 2, bp, RWKV_HEADS)).reshape(HALF_N, LANES)
    mem_shape = (N_MEM, MEM_HEADS, MEM_DH)
    y_p, srw_p, cache_k_heads, cache_v_heads = _rwkv_scan_prompt(
        r_p, k_p, w_p, a_p, v_p, key_par(rwkv_k_k[l]), key_par(rwkv_k_a[l]), key_par(rwkv_r_k[l]),
        val_par(rwkv_lnx_w[l]), val_par(rwkv_lnx_b[l]), jnp.zeros((HALF_N, RWKV_N, LANES), F32),
        cache_mem_k.reshape(bs, *mem_shape), cache_mem_v.reshape(bs, *mem_shape))
    srw_p = srw_p.reshape(HALF_N, RWKV_N, 2, bp, RWKV_HEADS).transpose(3, 4, 2, 0, 1).reshape(
        bp, RWKV_HEADS, RWKV_N, RWKV_N)
    head_par = lambda v: jnp.broadcast_to(kvec(v)[:, :, None], (RWKV_HEADS, RWKV_N, LANES))
    val_rows = lambda v: v.reshape(RWKV_HEADS, 2, HALF_N).transpose(0, 2, 1).reshape(RWKV_HEADS * RWKV_N)
    state_s = state_rwkv[l].astype(F32).reshape(bs, RWKV_HEADS, 2, HALF_N, RWKV_N).transpose(1, 3, 2, 4, 0)
    y_s, srw_s = _rwkv_scan(
        r_s, k_s, w_s, a_s, v_s, head_par(rwkv_k_k[l]), head_par(rwkv_k_a[l]), head_par(rwkv_r_k[l]),
        head_par(val_rows(rwkv_lnx_w[l])), head_par(val_rows(rwkv_lnx_b[l])),
        state_s.reshape(RWKV_HEADS, RWKV_N, RWKV_N, bs), tc=ts, halves=1)
    y_s = y_s.transpose(3, 1, 2, 0).reshape(ns_tok, RWKV_W)
    srw_s = srw_s.reshape(RWKV_HEADS, HALF_N, 2, RWKV_N, bs).transpose(4, 0, 2, 1, 3).reshape(
        bs, RWKV_HEADS, RWKV_N, RWKV_N)

    w_rwkv_out = _value_major(w_out[l][RET_W:].T).T
    h, q = _merge(x_parts, [oret_p, oret_s], [gate_p, gate_s], y_p, y_s, w_out[l][:RET_W], w_rwkv_out,
                  g_mem_q[l], w_mq[l])

    mk, mv, mk_heads, mv_heads = _mem_kv(mem_prompt, g_mem_kv[l], w_mk[l], w_mv[l])
    att_p = _attention(q, mk_heads, mv_heads, row0=0, n_batch=bp, t=tp, n_seq=1, tq=tp)
    att_s = _attention(q, cache_k_heads, cache_v_heads, row0=np_tok, n_batch=bs, t=ts, n_seq=16, tq=ts)

    h, hn, ids, comb = _router([att_p, att_s], w_mo[l], h, g_ffn[l], w_group_router[l], b_group_router[l],
                               w_expert_router[l], b_expert_router[l])
    blk_e, dest, n_sorted = _route_plan(ids[:, :TOP_K])
    y_sorted = _experts(_dispatch(hn, dest, n_sorted), blk_e, w_e_gate[l], w_e_up[l], w_e_down[l])
    y_prompt, y_sample = _final(h, y_sorted, dest, comb, g_final, n_first_rows=np_tok)
    y_prompt = y_prompt.reshape(bp, tp, d)
    y_sample = y_sample.reshape(bs, ts, d)

    shift_p = _rwkv_cols(shift_p.reshape(bp, N_RWKV_COLS), _key_major_inv, _value_major_inv)
    shift_s = _rwkv_cols(shift_s.reshape(bs, N_RWKV_COLS), _key_major_inv, _value_major_inv)
    return (y_prompt, y_sample, sret_p[None], srw_p[None], shift_p[None],
            mk.reshape(1, bp, *mem_shape), mv.reshape(1, bp, *mem_shape),
            sret_s[None], srw_s[None], shift_s[None])
```

```python
import functools

import numpy as np
import jax
import jax.numpy as jnp
from jax import lax
from jax.experimental import pallas as pl
from jax.experimental.pallas import tpu as pltpu

F32 = jnp.float32
BF16 = jnp.bfloat16

D_MODEL = 1024
PAST_LEN = 16384
N_MEM = 256
MEM_HEADS = 4
MEM_DH = D_MODEL // MEM_HEADS
RET_HEADS = 4
RET_W = D_MODEL // 2
RET_DV = RET_W // RET_HEADS
RET_DK = RET_DV // 2
RET_QK = RET_HEADS * RET_DK
RET_CHUNK = 128
ROPE_BASE = 10000.0
RWKV_N = 64
RWKV_W = D_MODEL - RET_W
RWKV_HEADS = RWKV_W // RWKV_N
LORA_W = 64
LORA_A = 64
LORA_G = 128
LNX_EPS = 64e-5
N_RET_COLS = 2 * RET_QK + 2 * RET_W
N_RWKV_COLS = 3 * RWKV_W + LORA_W + LORA_A + LORA_G
N_IN_COLS = N_RET_COLS + N_RWKV_COLS
N_GROUPS = 4
EXP_PER_GROUP = 8
N_EXPERTS = N_GROUPS * EXP_PER_GROUP
TOP_K = 2
D_EXPERT = D_MODEL // 2
EPS = 1e-6

LANES = 128
MOE_ROWS = 256
TM = 512


def _params(*sem):
    return pltpu.CompilerParams(dimension_semantics=sem)


def _row_part_specs(parts, tm):
    specs, counts, start = [], [], 0
    for part in parts:
        nb = part.shape[0] // tm
        assert nb * tm == part.shape[0]
        specs.append(pl.BlockSpec((tm, part.shape[1]), lambda i, s=start, n=nb: (jnp.clip(i - s, 0, n - 1), 0)))
        counts.append(nb)
        start += nb
    return specs, counts


def _read_row_parts(refs, counts):
    i = pl.program_id(0)
    x = refs[0][...]
    start = counts[0]
    for ref, nb in zip(refs[1:], counts[1:]):
        x = jnp.where(i >= start, ref[...], x)
        start += nb
    return x


def _proj_body(*refs, part_counts):
    n = len(part_counts)
    g_ref, w_ref, o_ref = refs[n:]
    x = _read_row_parts(refs[:n], part_counts).astype(F32)
    xb = (x * lax.rsqrt(jnp.mean(x * x, axis=-1, keepdims=True) + EPS) * g_ref[...]).astype(BF16)
    n_chunk = 256
    for j in range(0, w_ref.shape[1], n_chunk):
        o_ref[:, j:j + n_chunk] = jnp.dot(xb, w_ref[:, j:j + n_chunk], preferred_element_type=F32)


def _input_projection(x_parts, gain, w):
    k, n_out = w.shape
    m = sum(part.shape[0] for part in x_parts)
    assert n_out % 256 == 0
    in_specs, part_counts = _row_part_specs(x_parts, TM)
    return pl.pallas_call(
        functools.partial(_proj_body, part_counts=part_counts),
        grid=(m // TM,),
        in_specs=in_specs + [pl.BlockSpec((1, k), lambda i: (0, 0)), pl.BlockSpec((k, n_out), lambda i: (0, 0))],
        out_specs=pl.BlockSpec((TM, n_out), lambda i: (i, 0)),
        out_shape=jax.ShapeDtypeStruct((m, n_out), F32),
        compiler_params=_params("parallel"),
    )(*x_parts, gain.reshape(1, k).astype(F32), w.astype(BF16))


def _rot_tables(pos):
    half = RET_DK // 2
    inv_freq = ROPE_BASE ** (-(np.arange(half, dtype=np.float64) / half))
    ang = pos.astype(np.float64)[:, None] * inv_freq[None, :]
    cos, sin = np.cos(ang), np.sin(ang)
    zero = np.zeros_like(sin)
    c = np.tile(np.concatenate([cos, cos], axis=1), (1, RET_HEADS))
    s_lo = np.tile(np.concatenate([-sin, zero], axis=1), (1, RET_HEADS))
    s_hi = np.tile(np.concatenate([zero, sin], axis=1), (1, RET_HEADS))
    return [jnp.asarray(t, F32) for t in (c, s_lo, s_hi)]


def _ret_decay_tables(c):
    lg = np.log1p(-np.exp2(-5.0 - np.arange(RET_HEADS, dtype=np.float64)))
    idx = np.arange(c, dtype=np.float64)
    diff = idx[:, None] - idx[None, :]
    mask = np.where(diff[None] >= 0, np.exp(np.maximum(diff, 0.0)[None] * lg[:, None, None]), 0.0)
    q_dec = np.repeat(np.exp((idx[:, None] + 1.0) * lg[None, :]), RET_DV, axis=1)
    k_dec = np.repeat(np.exp((c - 1.0 - idx)[:, None] * lg[None, :]), RET_DK, axis=1)
    c_dec = [float(v) for v in np.exp(c * lg)]
    return jnp.asarray(mask, F32), jnp.asarray(q_dec, F32), jnp.asarray(k_dec, F32), c_dec


def _ret_body(q_ref, k_ref, v_ref, gate_ref, c_ref, slo_ref, shi_ref, mask_ref, qdec_ref, kdec_ref,
              gn_ref, s0_ref, o_ref, sout_ref, s_scr, *, n_seq, c, c_dec):
    ci = pl.program_id(1)

    @pl.when(ci == 0)
    def _():
        s_scr[...] = s0_ref[...].astype(F32)

    cos, s_lo, s_hi = c_ref[...], slo_ref[...], shi_ref[...]
    half = RET_DK // 2

    def rope(x):
        return x * cos + pltpu.roll(x, RET_QK - half, 1) * s_lo + pltpu.roll(x, half, 1) * s_hi

    nt = (((1,), (1,)), ((), ()))
    tn = (((0,), (0,)), ((), ()))
    for g in range(n_seq):
        rows = slice(g * c, (g + 1) * c)
        q = rope(q_ref[rows, :].astype(F32))
        k = rope(k_ref[rows, :].astype(F32)) * (RET_DK ** -0.5)
        k_st = k * kdec_ref[...]
        for h in range(RET_HEADS):
            kc = slice(h * RET_DK, (h + 1) * RET_DK)
            vc = slice(h * RET_DV, (h + 1) * RET_DV)
            qh = q[:, kc].astype(BF16)
            vh = v_ref[rows, vc].astype(BF16)
            s_h = s_scr[g, h]
            att = lax.dot_general(qh, k[:, kc].astype(BF16), nt, preferred_element_type=F32) * mask_ref[h]
            o = jnp.dot(att.astype(BF16), vh, preferred_element_type=F32)
            o = o + jnp.dot(qh, s_h.astype(BF16), preferred_element_type=F32) * qdec_ref[:, vc]
            s_scr[g, h] = s_h * c_dec[h] + lax.dot_general(
                k_st[:, kc].astype(BF16), vh, tn, preferred_element_type=F32)
            o = o * lax.rsqrt(jnp.mean(o * o, axis=-1, keepdims=True) + EPS)
            gate = gate_ref[rows, vc].astype(F32)
            o_ref[rows, vc] = o * gn_ref[:, vc] * (gate * jax.nn.sigmoid(gate))

    @pl.when(ci == pl.num_programs(1) - 1)
    def _():
        sout_ref[...] = s_scr[...]


def _retention(proj, s0, ret_gn, pos, *, row0, n_batch, t, n_seq):
    c = RET_CHUNK if t % RET_CHUNK == 0 else t
    n_chunks = t // c
    rows = n_seq * c
    assert n_batch % n_seq == 0 and row0 % rows == 0 and (n_seq == 1 or n_chunks == 1)
    blk0 = row0 // rows
    mask, q_dec, k_dec, c_dec = _ret_decay_tables(c)
    cos, s_lo, s_hi = _rot_tables(pos)

    def row_map(col):
        return lambda b, ci: (blk0 + b * n_chunks + ci, col)

    def const2(b, ci):
        return (0, 0)

    state_spec = pl.BlockSpec((n_seq, RET_HEADS, RET_DK, RET_DV), lambda b, ci: (b, 0, 0, 0))
    in_specs = [
        pl.BlockSpec((rows, RET_QK), row_map(0)),
        pl.BlockSpec((rows, RET_QK), row_map(1)),
        pl.BlockSpec((rows, RET_W), row_map(1)),
        pl.BlockSpec((rows, RET_W), row_map(2)),
        pl.BlockSpec((c, RET_QK), lambda b, ci: (ci, 0)),
        pl.BlockSpec((c, RET_QK), lambda b, ci: (ci, 0)),
        pl.BlockSpec((c, RET_QK), lambda b, ci: (ci, 0)),
        pl.BlockSpec((RET_HEADS, c, c), lambda b, ci: (0, 0, 0)),
        pl.BlockSpec((c, RET_W), const2),
        pl.BlockSpec((c, RET_QK), const2),
        pl.BlockSpec((1, RET_W), const2),
        state_spec,
    ]
    return pl.pallas_call(
        functools.partial(_ret_body, n_seq=n_seq, c=c, c_dec=c_dec),
        grid=(n_batch // n_seq, n_chunks),
        in_specs=in_specs,
        out_specs=[pl.BlockSpec((rows, RET_W), lambda b, ci: (b * n_chunks + ci, 0)), state_spec],
        out_shape=[jax.ShapeDtypeStruct((n_batch * t, RET_W), F32),
                   jax.ShapeDtypeStruct((n_batch, RET_HEADS, RET_DK, RET_DV), F32)],
        scratch_shapes=[pltpu.VMEM((n_seq, RET_HEADS, RET_DK, RET_DV), F32)],
        compiler_params=_params("parallel", "arbitrary"),
    )(proj, proj, proj, proj, cos, s_lo, s_hi, mask, q_dec, k_dec, ret_gn.reshape(1, RET_W).astype(F32), s0)


LORA_COLS = LORA_W + LORA_A + LORA_G


def _rwkv_lora_terms(lo, w0_ref, w2_ref, a0_ref, a2_ref, g2_ref):
    hw = lo[:, :LORA_W]
    ha = lo[:, LORA_W:LORA_W + LORA_A]
    hg = lo[:, LORA_W + LORA_A:]
    u = w0_ref[...] + jnp.dot(jnp.tanh(hw).astype(BF16), w2_ref[...], preferred_element_type=F32)
    decay = jnp.exp(-float(np.exp(-0.5)) * jax.nn.sigmoid(u))
    rate = jax.nn.sigmoid(a0_ref[...] + jnp.dot(ha.astype(BF16), a2_ref[...], preferred_element_type=F32))
    gate = jnp.dot(jax.nn.sigmoid(hg).astype(BF16), g2_ref[...], preferred_element_type=F32)
    return decay, rate, gate


def _rwkv_pre_body(r_ref, k_ref, v_ref, lo_ref, shift_ref, mu_ref, w0_ref, w2_ref, a0_ref, a2_ref, g2_ref,
                   ro_ref, ko_ref, vo_ref, wo_ref, ao_ref, go_ref, so_ref, prev_scr):
    ci = pl.program_id(1)
    c = r_ref.shape[0]

    @pl.when(ci == 0)
    def _():
        prev_scr[...] = shift_ref[0].astype(F32)

    first_row = lax.broadcasted_iota(jnp.int32, (c, 1), 0) == 0

    def shifted(x_ref, col0):
        w = x_ref.shape[1]
        x = x_ref[...].astype(F32)
        prev = jnp.where(first_row, prev_scr[:, col0:col0 + w], pltpu.roll(x, 1, 0))
        prev_scr[:, col0:col0 + w] = x[c - 1:c, :]
        return x + (prev - x) * mu_ref[:, col0:col0 + w]

    ro_ref[...] = shifted(r_ref, 0).T
    ko_ref[...] = shifted(k_ref, RWKV_W).T
    vo_ref[...] = shifted(v_ref, 2 * RWKV_W).T
    decay, rate, gate = _rwkv_lora_terms(shifted(lo_ref, 3 * RWKV_W), w0_ref, w2_ref, a0_ref, a2_ref, g2_ref)
    wo_ref[...] = decay.T
    ao_ref[...] = rate.T
    go_ref[...] = gate

    @pl.when(ci == pl.num_programs(1) - 1)
    def _():
        so_ref[0] = prev_scr[...]


def _rwkv_pre_args(s_shift, n_batch, mu, w0, w2, a0, a2, g2):
    return (s_shift.reshape(n_batch, 1, N_RWKV_COLS), mu.reshape(1, -1), w0.reshape(1, -1), w2.astype(BF16),
            a0.reshape(1, -1), a2.astype(BF16), g2.astype(BF16))


def _rwkv_pre_weight_specs(const):
    return [pl.BlockSpec((1, N_RWKV_COLS), const), pl.BlockSpec((1, RWKV_W), const),
            pl.BlockSpec((LORA_W, RWKV_W), const), pl.BlockSpec((1, RWKV_W), const),
            pl.BlockSpec((LORA_A, RWKV_W), const), pl.BlockSpec((LORA_G, RWKV_W), const)]


def _rwkv_pre(proj, s_shift, mu, w0, w2, a0, a2, g2, *, row0, n_batch, t, c):
    n_chunks = t // c
    assert t % c == 0 and row0 % c == 0
    blk0 = row0 // c
    col_r = N_RET_COLS // RWKV_W
    col_lo = (N_RET_COLS + 3 * RWKV_W) // LORA_COLS
    assert col_r * RWKV_W == N_RET_COLS and col_lo * LORA_COLS == N_RET_COLS + 3 * RWKV_W

    def row_map(col):
        return lambda b, ci: (blk0 + b * n_chunks + ci, col)

    state_spec = pl.BlockSpec((1, 1, N_RWKV_COLS), lambda b, ci: (b, 0, 0))
    in_specs = [pl.BlockSpec((c, RWKV_W), row_map(col_r)), pl.BlockSpec((c, RWKV_W), row_map(col_r + 1)),
                pl.BlockSpec((c, RWKV_W), row_map(col_r + 2)), pl.BlockSpec((c, LORA_COLS), row_map(col_lo)),
                state_spec] + _rwkv_pre_weight_specs(lambda b, ci: (0, 0))
    vec_spec = pl.BlockSpec((None, RWKV_W, c), lambda b, ci: (b, 0, ci))
    vec_shape = jax.ShapeDtypeStruct((n_batch, RWKV_W, t), F32)
    return pl.pallas_call(
        _rwkv_pre_body,
        grid=(n_batch, n_chunks),
        in_specs=in_specs,
        out_specs=[vec_spec] * 5 + [pl.BlockSpec((c, RWKV_W), lambda b, ci: (b * n_chunks + ci, 0)), state_spec],
        out_shape=[vec_shape] * 5 + [jax.ShapeDtypeStruct((n_batch * t, RWKV_W), F32),
                                     jax.ShapeDtypeStruct((n_batch, 1, N_RWKV_COLS), F32)],
        scratch_shapes=[pltpu.VMEM((1, N_RWKV_COLS), F32)],
        compiler_params=_params("parallel", "arbitrary"),
    )(proj, proj, proj, proj, *_rwkv_pre_args(s_shift, n_batch, mu, w0, w2, a0, a2, g2))


def _rwkv_pre_short_body(r_ref, k_ref, v_ref, lo_ref, shift_ref, mu_ref, w0_ref, w2_ref, a0_ref, a2_ref, g2_ref,
                         ro_ref, ko_ref, vo_ref, wo_ref, ao_ref, go_ref, so_ref, *, n_b, t):
    rows = n_b * t
    first_tok = (lax.broadcasted_iota(jnp.int32, (rows, 1), 0) & (t - 1)) == 0

    def shifted(x_ref, col0):
        w = x_ref.shape[1]
        x = x_ref[...].astype(F32)
        carried = jnp.broadcast_to(shift_ref[:, :, col0:col0 + w].astype(F32), (n_b, t, w)).reshape(rows, w)
        prev = jnp.where(first_tok, carried, pltpu.roll(x, 1, 0))
        so_ref[:, :, col0:col0 + w] = x.reshape(n_b, t, w)[:, t - 1:t, :]
        return x + (prev - x) * mu_ref[:, col0:col0 + w]

    def put(o_ref, x):
        by_tok = jnp.swapaxes(x.reshape(n_b, t, RWKV_W), 0, 1)
        for ti in range(t):
            feat = by_tok[ti].T.reshape(RWKV_N, RWKV_HEADS, n_b)
            o_ref[:, ti] = jnp.swapaxes(feat, 0, 1)

    put(ro_ref, shifted(r_ref, 0))
    put(ko_ref, shifted(k_ref, RWKV_W))
    put(vo_ref, shifted(v_ref, 2 * RWKV_W))
    decay, rate, gate = _rwkv_lora_terms(shifted(lo_ref, 3 * RWKV_W), w0_ref, w2_ref, a0_ref, a2_ref, g2_ref)
    put(wo_ref, decay)
    put(ao_ref, rate)
    go_ref[...] = gate


def _rwkv_pre_short(proj, s_shift, mu, w0, w2, a0, a2, g2, *, row0, n_batch, t):
    rows = n_batch * t
    assert row0 % rows == 0 and t & (t - 1) == 0
    blk0 = row0 // rows
    col_r = N_RET_COLS // RWKV_W
    col_lo = (N_RET_COLS + 3 * RWKV_W) // LORA_COLS
    state_spec = pl.BlockSpec((n_batch, 1, N_RWKV_COLS), lambda i: (0, 0, 0))
    in_specs = [pl.BlockSpec((rows, RWKV_W), lambda i: (blk0, col_r)),
                pl.BlockSpec((rows, RWKV_W), lambda i: (blk0, col_r + 1)),
                pl.BlockSpec((rows, RWKV_W), lambda i: (blk0, col_r + 2)),
                pl.BlockSpec((rows, LORA_COLS), lambda i: (blk0, col_lo)),
                state_spec] + _rwkv_pre_weight_specs(lambda i: (0, 0))
    vec_shape = (RWKV_HEADS, t, RWKV_N, n_batch)
    vec_spec = pl.BlockSpec(vec_shape, lambda i: (0, 0, 0, 0))
    return pl.pallas_call(
        functools.partial(_rwkv_pre_short_body, n_b=n_batch, t=t),
        grid=(1,),
        in_specs=in_specs,
        out_specs=[vec_spec] * 5 + [pl.BlockSpec((rows, RWKV_W), lambda i: (0, 0)), state_spec],
        out_shape=[jax.ShapeDtypeStruct(vec_shape, F32)] * 5 + [
            jax.ShapeDtypeStruct((rows, RWKV_W), F32), jax.ShapeDtypeStruct((n_batch, 1, N_RWKV_COLS), F32)],
        compiler_params=_params("arbitrary"),
    )(proj, proj, proj, proj, *_rwkv_pre_args(s_shift, n_batch, mu, w0, w2, a0, a2, g2))


def _scan_body(r_ref, k_ref, w_ref, a_ref, v_ref, kk_ref, ka_ref, rk_ref, lw_ref, lb_ref, s0_ref,
               y_ref, sout_ref, s_scr, a_scr, b_scr, km_scr, *, tc, vr, halves):
    ci = pl.program_id(1)

    @pl.when(ci == 0)
    def _():
        s_scr[...] = s0_ref[...].astype(F32)

    def ksum(x):
        return jnp.sum(x, axis=-2, keepdims=True)

    def vsum(x):
        if halves == 2:
            x2 = x.reshape(tc * vr, LANES)
            x = (x2 + pltpu.roll(x2, LANES // 2, 1)).reshape(tc, vr, LANES)
        return jnp.sum(x, axis=1, keepdims=True)

    kr = k_ref[...]
    a = a_ref[...]
    kk = kr * kk_ref[...]
    kk = kk / jnp.maximum(jnp.sqrt(ksum(kk * kk)), 1e-12)
    a_scr[...] = -kk
    b_scr[...] = kk * a
    km_scr[...] = kr * (1.0 + (a - 1.0) * ka_ref[...])

    def token(t, carry):
        r, w, avec, bvec, kmod = r_ref[t], w_ref[t], a_scr[t], b_scr[t], km_scr[t]

        def value_row(i, c2):
            s = s_scr[i]
            sa = ksum(s * avec)
            s = s * w + sa * bvec + v_ref[t, pl.ds(i, 1), :] * kmod
            s_scr[i] = s
            y_ref[t, pl.ds(i, 1), :] = ksum(s * r)
            return c2

        lax.fori_loop(0, vr, value_row, 0, unroll=16)
        return carry

    lax.fori_loop(0, tc, token, 0)

    y = y_ref[...]
    d = y - vsum(y) * (1.0 / RWKV_N)
    var = vsum(d * d) * (1.0 / RWKV_N)
    bonus = ksum(r_ref[...] * km_scr[...] * rk_ref[...])
    y_ref[...] = d * lax.rsqrt(var + LNX_EPS) * lw_ref[...] + lb_ref[...] + bonus * v_ref[...]

    @pl.when(ci == pl.num_programs(1) - 1)
    def _():
        sout_ref[...] = s_scr[...]


def _rwkv_scan(r, k, w, a, v, k_k, k_a, r_k, lnx_w, lnx_b, s0, *, tc, halves):
    n_grp, t, _, lanes = r.shape
    vr = v.shape[2]
    assert lanes == LANES and t % tc == 0 and vr * halves == RWKV_N

    def tok_spec(rows):
        return pl.BlockSpec((None, tc, rows, LANES), lambda g, ci: (g, ci, 0, 0))

    def par_spec(rows):
        return pl.BlockSpec((None, rows, LANES), lambda g, ci: (g, 0, 0))

    st_spec = pl.BlockSpec((None, vr, RWKV_N, LANES), lambda g, ci: (g, 0, 0, 0))
    key_scratch = pltpu.VMEM((tc, RWKV_N, LANES), F32)
    return pl.pallas_call(
        functools.partial(_scan_body, tc=tc, vr=vr, halves=halves),
        grid=(n_grp, t // tc),
        in_specs=[tok_spec(RWKV_N)] * 4 + [tok_spec(vr)] + [par_spec(RWKV_N)] * 3 + [par_spec(vr)] * 2 + [st_spec],
        out_specs=[tok_spec(vr), st_spec],
        out_shape=[jax.ShapeDtypeStruct((n_grp, t, vr, LANES), F32),
                   jax.ShapeDtypeStruct((n_grp, vr, RWKV_N, LANES), F32)],
        scratch_shapes=[pltpu.VMEM((vr, RWKV_N, LANES), F32), key_scratch, key_scratch, key_scratch],
        compiler_params=_params("parallel", "arbitrary"),
    )(r, k, w, a, v, k_k, k_a, r_k, lnx_w, lnx_b, s0)


SCAN_TC = 128
SCAN_SUB = 64
MEM_CHUNKS = 4


def _scan_prompt_body(r_ref, k_ref, w_ref, a_ref, v_ref, kk_ref, ka_ref, rk_ref, lw_ref, lb_ref, s0_ref,
                      memk_hbm, memv_hbm, y_ref, sout_ref, memk_out, memv_out,
                      s_scr, r_c, w_c, a_c, b_c, km_c, v_c, y_c, stash, mem_stage, mem_in_sem, mem_out_sem,
                      *, n_b, mem_seqs):
    ci = pl.program_id(0)
    vr = RWKV_N // 2
    ts = SCAN_SUB
    tile = RWKV_HEADS
    half_lanes = LANES // 2

    @pl.when(ci == 0)
    def _():
        s_scr[...] = s0_ref[...].astype(F32)

    chunk_seqs = mem_seqs // MEM_CHUNKS

    def mem_in(chunk, slot):
        copies = []
        for j in range(chunk_seqs):
            seq = (ci * MEM_CHUNKS + chunk) * chunk_seqs + j
            for h in range(MEM_HEADS):
                copies.append(pltpu.make_async_copy(memk_hbm.at[seq, :, h, :], mem_stage.at[slot, 0, j, h],
                                                    mem_in_sem.at[slot]))
                copies.append(pltpu.make_async_copy(memv_hbm.at[seq, :, h, :], mem_stage.at[slot, 1, j, h],
                                                    mem_in_sem.at[slot]))
        return copies

    def mem_out(chunk, slot):
        seqs = pl.ds((ci * MEM_CHUNKS + chunk) * chunk_seqs, chunk_seqs)
        return [pltpu.make_async_copy(mem_stage.at[slot, 0], memk_out.at[seqs], mem_out_sem.at[slot]),
                pltpu.make_async_copy(mem_stage.at[slot, 1], memv_out.at[seqs], mem_out_sem.at[slot])]

    def mem_phase(p):
        if 1 <= p <= MEM_CHUNKS:
            for cp in mem_in(p - 1, (p - 1) % 2):
                cp.wait()
            for cp in mem_out(p - 1, (p - 1) % 2):
                cp.start()
        if 2 <= p <= MEM_CHUNKS + 1:
            for cp in mem_out(p - 2, p % 2):
                cp.wait()
        if p < MEM_CHUNKS:
            for cp in mem_in(p, p % 2):
                cp.start()

    mem_phase(0)

    low = lax.broadcasted_iota(jnp.int32, (ts, LANES), 1) < half_lanes

    def feature_pair_rows(x_ref, base):
        tiles = [x_ref[b, pl.ds(base + f * tile, tile), :] for f in range(2) for b in range(n_b)]
        return jnp.concatenate(tiles, axis=0).T

    def key_to_chain(x_ref, dst, stash, t0):
        def group(g, c):
            rows = []
            for j in range(4):
                pair = g * 4 + j
                if t0 == 0:
                    full = feature_pair_rows(x_ref, pl.multiple_of(pair * 2 * tile, 2 * tile))
                    mt = full[:ts]
                    stash[pair] = full[ts:]
                else:
                    mt = stash[pair]
                sw = pltpu.roll(mt, half_lanes, 1)
                rows += [jnp.where(low, mt, sw), jnp.where(low, sw, mt)]
            dst[:, pl.ds(pl.multiple_of(g * 8, 8), 8), :] = jnp.swapaxes(jnp.stack(rows, axis=0), 0, 1)
            return c
        lax.fori_loop(0, RWKV_N // 8, group, 0, unroll=4)

    def value_to_chain(g, c):
        rows = [feature_pair_rows(v_ref, pl.multiple_of((g * 8 + j) * 2 * tile, 2 * tile)) for j in range(8)]
        v_c[:, pl.ds(pl.multiple_of(g * 8, 8), 8), :] = jnp.swapaxes(jnp.stack(rows, axis=0), 0, 1)
        return c
    lax.fori_loop(0, vr // 8, value_to_chain, 0, unroll=2)

    def ksum(x):
        return jnp.sum(x, axis=-2, keepdims=True)

    for t0 in range(0, SCAN_TC, ts):
        key_to_chain(r_ref, r_c, stash.at[0], t0)
        key_to_chain(w_ref, w_c, stash.at[1], t0)
        key_to_chain(k_ref, km_c, stash.at[2], t0)
        key_to_chain(a_ref, b_c, stash.at[3], t0)

        def prep(g8, c):
            toks = pl.ds(pl.multiple_of(g8 * 8, 8), 8)
            kr = km_c[toks]
            a = b_c[toks]
            kk = kr * kk_ref[...]
            kk = kk / jnp.maximum(jnp.sqrt(ksum(kk * kk)), 1e-12)
            a_c[toks] = -kk
            b_c[toks] = kk * a
            km_c[toks] = kr * (1.0 + (a - 1.0) * ka_ref[...])
            return c
        lax.fori_loop(0, ts // 8, prep, 0)
        mem_phase(1 + 2 * (t0 // ts))

        def token(t, carry):
            r, w, avec, bvec, kmod = r_c[t], w_c[t], a_c[t], b_c[t], km_c[t]

            def value_row(i, c2):
                s = s_scr[i]
                sa = ksum(s * avec)
                s = s * w + sa * bvec + v_c[t0 + t, pl.ds(i, 1), :] * kmod
                s_scr[i] = s
                y_c[t0 + t, pl.ds(i, 1), :] = ksum(s * r)
                return c2

            lax.fori_loop(0, vr, value_row, 0, unroll=True)
            return carry

        lax.fori_loop(0, ts, token, 0)
        mem_phase(2 + 2 * (t0 // ts))

        def post(g8, c):
            ktoks = pl.ds(pl.multiple_of(g8 * 8, 8), 8)
            vtoks = pl.ds(pl.multiple_of(t0 + g8 * 8, 8), 8)

            def vsum(x):
                x2 = x.reshape(8 * vr, LANES)
                x2 = x2 + pltpu.roll(x2, half_lanes, 1)
                return jnp.sum(x2.reshape(8, vr, LANES), axis=1, keepdims=True)

            y = y_c[vtoks]
            d = y - vsum(y) * (1.0 / RWKV_N)
            var = vsum(d * d) * (1.0 / RWKV_N)
            bonus = ksum(r_c[ktoks] * km_c[ktoks] * rk_ref[...])
            y_c[vtoks] = d * lax.rsqrt(var + LNX_EPS) * lw_ref[...] + lb_ref[...] + bonus * v_c[vtoks]
            return c
        lax.fori_loop(0, ts // 8, post, 0, unroll=4)

    def value_from_chain(g, c):
        blk = jnp.swapaxes(y_c[:, pl.ds(pl.multiple_of(g * 8, 8), 8), :], 0, 1)
        for j in range(8):
            mt = blk[j].T
            base = pl.multiple_of((g * 8 + j) * 2 * tile, 2 * tile)
            for hf in range(2):
                for b in range(n_b):
                    row0 = (hf * n_b + b) * tile
                    y_ref[b, pl.ds(base + hf * tile, tile), :] = mt[row0:row0 + tile, :]
        return c
    lax.fori_loop(0, vr // 8, value_from_chain, 0, unroll=2)

    assert 2 * (SCAN_TC // ts) == MEM_CHUNKS
    mem_phase(MEM_CHUNKS + 1)

    @pl.when(ci == pl.num_programs(0) - 1)
    def _():
        sout_ref[...] = s_scr[...]


def _rwkv_scan_prompt(r, k, w, a, v, k_k, k_a, r_k, lnx_w, lnx_b, s0, mem_k, mem_v):
    n_b, _, t = r.shape
    vr = RWKV_N // 2
    n_steps = t // SCAN_TC
    n_mem_seq = mem_k.shape[0]
    assert t % SCAN_TC == 0 and 2 * n_b * RWKV_HEADS == LANES and n_mem_seq % (n_steps * MEM_CHUNKS) == 0
    chunk_seqs = n_mem_seq // (n_steps * MEM_CHUNKS)
    any_spec = pl.BlockSpec(memory_space=pl.ANY)
    mem_shape = jax.ShapeDtypeStruct((n_mem_seq, MEM_HEADS, N_MEM, MEM_DH), F32)
    tok_spec = pl.BlockSpec((n_b, RWKV_W, SCAN_TC), lambda ci: (0, 0, ci))
    key_par = pl.BlockSpec((RWKV_N, LANES), lambda ci: (0, 0))
    val_par = pl.BlockSpec((vr, LANES), lambda ci: (0, 0))
    st_spec = pl.BlockSpec((vr, RWKV_N, LANES), lambda ci: (0, 0, 0))
    key_chain = pltpu.VMEM((SCAN_SUB, RWKV_N, LANES), F32)
    val_chain = pltpu.VMEM((SCAN_TC, vr, LANES), F32)
    return pl.pallas_call(
        functools.partial(_scan_prompt_body, n_b=n_b, mem_seqs=n_mem_seq // n_steps),
        grid=(n_steps,),
        in_specs=[tok_spec] * 5 + [key_par] * 3 + [val_par] * 2 + [st_spec, any_spec, any_spec],
        out_specs=[tok_spec, st_spec, any_spec, any_spec],
        out_shape=[jax.ShapeDtypeStruct((n_b, RWKV_W, t), F32),
                   jax.ShapeDtypeStruct((vr, RWKV_N, LANES), F32), mem_shape, mem_shape],
        scratch_shapes=([pltpu.VMEM((vr, RWKV_N, LANES), F32)] + [key_chain] * 5 + [val_chain] * 2
                        + [pltpu.VMEM((4, RWKV_N // 2, SCAN_TC - SCAN_SUB, LANES), F32),
                           pltpu.VMEM((2, 2, chunk_seqs, MEM_HEADS, N_MEM, MEM_DH), F32),
                           pltpu.SemaphoreType.DMA((2,)), pltpu.SemaphoreType.DMA((2,))]),
        compiler_params=_params("arbitrary"),
    )(r, k, w, a, v, k_k, k_a, r_k, lnx_w, lnx_b, s0, mem_k, mem_v)


def _merge_body(*refs, part_counts):
    n = len(part_counts)
    x_refs, oret_refs, g_refs = refs[:n], refs[n:2 * n], refs[2 * n:3 * n]
    yt_ref, ys_ref, wt_ref, wb_ref, gq_ref, wq_ref, o_ref, q_ref = refs[3 * n:]
    x = _read_row_parts(x_refs, part_counts)
    y = jnp.where(pl.program_id(0) >= part_counts[0], ys_ref[...], yt_ref[...].T)
    yb = (y * _read_row_parts(g_refs, part_counts)).astype(BF16)
    ob = _read_row_parts(oret_refs, part_counts).astype(BF16)
    n_chunk = 256
    for j in range(0, D_MODEL, n_chunk):
        acc = jnp.dot(ob, wt_ref[:, j:j + n_chunk], preferred_element_type=F32)
        acc = acc + jnp.dot(yb, wb_ref[:, j:j + n_chunk], preferred_element_type=F32)
        o_ref[:, j:j + n_chunk] = x[:, j:j + n_chunk] + acc
    h = o_ref[...]
    hb = (h * lax.rsqrt(jnp.mean(h * h, axis=-1, keepdims=True) + EPS) * gq_ref[...]).astype(BF16)
    for j in range(0, D_MODEL, n_chunk):
        q_ref[:, j:j + n_chunk] = jnp.dot(hb, wq_ref[:, j:j + n_chunk], preferred_element_type=F32)


def _merge(x_parts, oret_parts, g_parts, y_first_t, y_second, w_ret, w_rwkv, gain_q, w_q):
    m = sum(part.shape[0] for part in x_parts)
    in_specs, part_counts = [], None
    for parts in (x_parts, oret_parts, g_parts):
        specs, part_counts = _row_part_specs(parts, TM)
        in_specs += specs
    assert len(part_counts) == 2
    n_first = part_counts[0]
    tiles = y_first_t.shape[2] // TM
    assert y_first_t.shape[0] * tiles == n_first
    yt_spec = pl.BlockSpec((None, RWKV_W, TM),
                           lambda i: (jnp.minimum(i, n_first - 1) // tiles, 0, jnp.minimum(i, n_first - 1) % tiles))
    ys_spec = pl.BlockSpec((TM, RWKV_W), lambda i: (jnp.clip(i - n_first, 0, part_counts[1] - 1), 0))
    wspec = pl.BlockSpec((RET_W, D_MODEL), lambda i: (0, 0))
    row_spec = pl.BlockSpec((TM, D_MODEL), lambda i: (i, 0))
    row_shape = jax.ShapeDtypeStruct((m, D_MODEL), F32)
    return pl.pallas_call(
        functools.partial(_merge_body, part_counts=part_counts),
        grid=(m // TM,),
        in_specs=in_specs + [yt_spec, ys_spec, wspec, wspec, pl.BlockSpec((1, D_MODEL), lambda i: (0, 0)),
                             pl.BlockSpec((D_MODEL, D_MODEL), lambda i: (0, 0))],
        out_specs=[row_spec, row_spec],
        out_shape=[row_shape, row_shape],
        compiler_params=_params("parallel"),
    )(*x_parts, *oret_parts, *g_parts, y_first_t, y_second, w_ret.astype(BF16), w_rwkv.astype(BF16),
      gain_q.reshape(1, D_MODEL), w_q.astype(BF16))


def _mem_kv_body(x_ref, g_ref, wk_ref, wv_ref, k_ref, v_ref, kh_ref, vh_ref, *, n_seq):
    x = x_ref[...].astype(F32)
    xb = (x * lax.rsqrt(jnp.mean(x * x, axis=-1, keepdims=True) + EPS) * g_ref[...]).astype(BF16)
    for w_ref, o_ref, oh_ref in ((wk_ref, k_ref, kh_ref), (wv_ref, v_ref, vh_ref)):
        for h in range(MEM_HEADS):
            acc = jnp.dot(xb, w_ref[:, h * MEM_DH:(h + 1) * MEM_DH], preferred_element_type=F32)
            o_ref[:, h, :] = acc
            for s in range(n_seq):
                oh_ref[s, h] = acc[s * N_MEM:(s + 1) * N_MEM]


def _mem_kv(mem, gain, w_k, w_v):
    n_b = mem.shape[0]
    n_seq = TM // N_MEM
    assert n_seq * N_MEM == TM and n_b % n_seq == 0
    wspec = pl.BlockSpec((D_MODEL, D_MODEL), lambda i: (0, 0))
    tok_spec = pl.BlockSpec((TM, MEM_HEADS, MEM_DH), lambda i: (i, 0, 0))
    head_spec = pl.BlockSpec((n_seq, MEM_HEADS, N_MEM, MEM_DH), lambda i: (i, 0, 0, 0))
    tok_shape = jax.ShapeDtypeStruct((n_b * N_MEM, MEM_HEADS, MEM_DH), F32)
    head_shape = jax.ShapeDtypeStruct((n_b, MEM_HEADS, N_MEM, MEM_DH), F32)
    return pl.pallas_call(
        functools.partial(_mem_kv_body, n_seq=n_seq),
        grid=(n_b // n_seq,),
        in_specs=[pl.BlockSpec((TM, D_MODEL), lambda i: (i, 0)), pl.BlockSpec((1, D_MODEL), lambda i: (0, 0)),
                  wspec, wspec],
        out_specs=[tok_spec, tok_spec, head_spec, head_spec],
        out_shape=[tok_shape, tok_shape, head_shape, head_shape],
        compiler_params=_params("parallel"),
    )(mem.reshape(n_b * N_MEM, D_MODEL), gain.reshape(1, D_MODEL), w_k.astype(BF16), w_v.astype(BF16))


def _attn_body(q_ref, k_ref, v_ref, o_ref, *, n_seq, tq):
    nt = (((1,), (1,)), ((), ()))
    for g in range(n_seq):
        rows = slice(g * tq, (g + 1) * tq)
        q = q_ref[rows, :].astype(BF16)
        s = lax.dot_general(q, k_ref[g].astype(BF16), nt, preferred_element_type=F32) * (MEM_DH ** -0.5)
        p = jnp.exp(s - jnp.max(s, axis=-1, keepdims=True))
        l = jnp.sum(p, axis=-1, keepdims=True)
        o = jnp.dot(p.astype(BF16), v_ref[g].astype(BF16), preferred_element_type=F32)
        o_ref[rows, :] = o / l


def _attention(q, mem_k, mem_v, *, row0, n_batch, t, n_seq, tq):
    q_tiles = t // tq
    rows = n_seq * tq
    assert t % tq == 0 and n_batch % n_seq == 0 and row0 % rows == 0 and (n_seq == 1 or q_tiles == 1)
    blk0 = row0 // rows
    kv_spec = pl.BlockSpec((n_seq, None, N_MEM, MEM_DH), lambda b, h, qi: (b, h, 0, 0))
    return pl.pallas_call(
        functools.partial(_attn_body, n_seq=n_seq, tq=tq),
        grid=(n_batch // n_seq, MEM_HEADS, q_tiles),
        in_specs=[pl.BlockSpec((rows, MEM_DH), lambda b, h, qi: (blk0 + b * q_tiles + qi, h)), kv_spec, kv_spec],
        out_specs=pl.BlockSpec((rows, MEM_DH), lambda b, h, qi: (b * q_tiles + qi, h)),
        out_shape=jax.ShapeDtypeStruct((n_batch * t, D_MODEL), F32),
        compiler_params=_params("parallel", "parallel", "parallel"),
    )(q, mem_k, mem_v)


ROW_TILE = (D_MODEL // LANES, LANES)


def _rows_to_tiles(x):
    chunks = [x[:, j * LANES:(j + 1) * LANES] for j in range(ROW_TILE[0])]
    return jnp.swapaxes(jnp.stack(chunks, axis=0), 0, 1)


def _tiles_to_rows(x):
    chunks = jnp.swapaxes(x, 0, 1)
    return jnp.concatenate([chunks[j] for j in range(ROW_TILE[0])], axis=1)


def _attn_out_body(*refs, part_counts):
    n = len(part_counts)
    w_ref, res_ref, o_ref = refs[n:]
    ab = _read_row_parts(refs[:n], part_counts).astype(BF16)
    n_chunk = 256
    for j in range(0, D_MODEL, n_chunk):
        o_ref[:, j:j + n_chunk] = res_ref[:, j:j + n_chunk] + jnp.dot(
            ab, w_ref[:, j:j + n_chunk], preferred_element_type=F32)


def _attn_out(att_parts, w_mo, residual):
    m = residual.shape[0]
    att_specs, part_counts = _row_part_specs(att_parts, TM)
    row = pl.BlockSpec((TM, D_MODEL), lambda i: (i, 0))
    return pl.pallas_call(
        functools.partial(_attn_out_body, part_counts=part_counts),
        grid=(m // TM,),
        in_specs=att_specs + [pl.BlockSpec((D_MODEL, D_MODEL), lambda i: (0, 0)), row],
        out_specs=row,
        out_shape=jax.ShapeDtypeStruct((m, D_MODEL), F32),
        compiler_params=_params("parallel"),
    )(*att_parts, w_mo.astype(BF16), residual)


def _router_body(h_ref, g_ref, w_ref, b_ref, hn_ref, ids_ref, comb_ref):
    x = h_ref[...]
    hn = x * lax.rsqrt(jnp.mean(x * x, axis=-1, keepdims=True) + EPS) * g_ref[...]
    hn_ref[...] = _rows_to_tiles(hn)
    logits = jnp.dot(hn, w_ref[...], precision=lax.Precision.HIGHEST, preferred_element_type=F32) + b_ref[...]
    lane = lax.broadcasted_iota(jnp.int32, logits.shape, 1).astype(F32)
    neg = -jnp.inf

    def first_argmax(vals):
        m = jnp.max(vals, axis=-1, keepdims=True)
        return m, jnp.min(jnp.where(vals == m, lane, float(LANES)), axis=-1, keepdims=True)

    gl = jnp.where(lane < N_GROUPS, logits, neg)
    gmax, gsel = first_argmax(gl)
    pg_sel = 1.0 / jnp.sum(jnp.exp(gl - gmax), axis=-1, keepdims=True)
    e0 = N_GROUPS + gsel * EXP_PER_GROUP
    el = jnp.where((lane >= e0) & (lane < e0 + EXP_PER_GROUP), logits, neg)
    m1, i1 = first_argmax(el)
    m2, i2 = first_argmax(jnp.where(lane == i1, neg, el))
    e21 = jnp.exp(m2 - m1)
    c1 = pg_sel / (1.0 + e21)
    c2 = c1 * e21
    ids = jnp.where(lane == 0, i1 - N_GROUPS, jnp.where(lane == 1, i2 - N_GROUPS, 0.0))
    ids_ref[...] = ids.astype(jnp.int32)
    comb_ref[...] = jnp.where(lane == 0, c1, jnp.where(lane == 1, c2, 0.0))


def _router(h, g_ffn, w_gr, b_gr, w_er, b_er):
    m = h.shape[0]
    pad = LANES - N_GROUPS - N_EXPERTS
    w = jnp.concatenate([w_gr, w_er, jnp.zeros((D_MODEL, pad), F32)], axis=1)
    b = jnp.concatenate([b_gr, b_er, jnp.zeros((pad,), F32)]).reshape(1, LANES)
    row = lambda n: pl.BlockSpec((TM, n), lambda i: (i, 0))
    return pl.pallas_call(
        _router_body,
        grid=(m // TM,),
        in_specs=[row(D_MODEL), pl.BlockSpec((1, D_MODEL), lambda i: (0, 0)),
                  pl.BlockSpec((D_MODEL, LANES), lambda i: (0, 0)), pl.BlockSpec((1, LANES), lambda i: (0, 0))],
        out_specs=[pl.BlockSpec((TM,) + ROW_TILE, lambda i: (i, 0, 0)), row(LANES), row(LANES)],
        out_shape=[jax.ShapeDtypeStruct((m,) + ROW_TILE, F32), jax.ShapeDtypeStruct((m, LANES), jnp.int32),
                   jax.ShapeDtypeStruct((m, LANES), F32)],
        compiler_params=_params("parallel"),
    )(h, g_ffn.reshape(1, D_MODEL), w, b)


def _dispatch_body(dest_ref, hn_ref, sorted_in, sorted_out, sem):
    del sorted_in

    def issue(r, c):
        for k in range(TOP_K):
            pltpu.make_async_copy(hn_ref.at[r], sorted_out.at[dest_ref[0, k, r]], sem.at[k]).start(priority=k)
        return c
    lax.fori_loop(0, TM, issue, 0, unroll=8)
    for k in range(TOP_K):
        pltpu.make_async_copy(hn_ref, sorted_out.at[pl.ds(0, TM)], sem.at[k]).wait()


def _dispatch(hn, dest, n_sorted):
    n_tok = hn.shape[0]
    return pl.pallas_call(
        _dispatch_body,
        grid=(n_tok // TM,),
        in_specs=[pl.BlockSpec((1, TOP_K, TM), lambda i: (i, 0, 0), memory_space=pltpu.SMEM),
                  pl.BlockSpec((TM,) + ROW_TILE, lambda i: (i, 0, 0)),
                  pl.BlockSpec(memory_space=pl.ANY)],
        out_specs=pl.BlockSpec(memory_space=pl.ANY),
        out_shape=jax.ShapeDtypeStruct((n_sorted,) + ROW_TILE, F32),
        scratch_shapes=[pltpu.SemaphoreType.DMA((TOP_K,))],
        input_output_aliases={2: 0},
        compiler_params=_params("arbitrary"),
    )(dest, hn, jnp.zeros((n_sorted,) + ROW_TILE, F32))


def _expert_body(blk_e_ref, x_ref, wg_ref, wu_ref, wd_ref, o_ref):
    del blk_e_ref
    x = _tiles_to_rows(x_ref[...]).astype(BF16)
    hg = jnp.dot(x, wg_ref[0].astype(BF16), preferred_element_type=F32)
    hu = jnp.dot(x, wu_ref[0].astype(BF16), preferred_element_type=F32)
    act = (hg * jax.nn.sigmoid(hg) * hu).astype(BF16)
    o_ref[...] = _rows_to_tiles(jnp.dot(act, wd_ref[0].astype(BF16), preferred_element_type=F32))


def _experts(x_sorted, blk_e, w_gate, w_up, w_down):
    n_blocks = blk_e.shape[0]
    row_spec = pl.BlockSpec((MOE_ROWS,) + ROW_TILE, lambda i, be: (i, 0, 0))
    grid_spec = pltpu.PrefetchScalarGridSpec(
        num_scalar_prefetch=1,
        grid=(n_blocks,),
        in_specs=[
            row_spec,
            pl.BlockSpec((1, D_MODEL, D_EXPERT), lambda i, be: (be[i], 0, 0)),
            pl.BlockSpec((1, D_MODEL, D_EXPERT), lambda i, be: (be[i], 0, 0)),
            pl.BlockSpec((1, D_EXPERT, D_MODEL), lambda i, be: (be[i], 0, 0)),
        ],
        out_specs=row_spec,
    )
    return pl.pallas_call(
        _expert_body,
        grid_spec=grid_spec,
        out_shape=jax.ShapeDtypeStruct(x_sorted.shape, F32),
        compiler_params=_params("arbitrary"),
    )(blk_e, x_sorted, w_gate, w_up, w_down)


def _route_plan(ids):
    n_tok = ids.shape[0]
    n_pairs = ids.size
    n_blocks = -(-(n_pairs + N_EXPERTS * (MOE_ROWS - 1)) // MOE_ROWS)
    flat_e = ids.reshape(n_pairs)
    onehot = (flat_e[:, None] == jnp.arange(N_EXPERTS, dtype=jnp.int32)[None, :]).astype(jnp.int32)
    csum = jnp.cumsum(onehot, axis=0)
    rank = jnp.sum(onehot * csum, axis=1) - 1
    counts = csum[-1]
    pcounts = (counts + MOE_ROWS - 1) // MOE_ROWS * MOE_ROWS
    pends = jnp.cumsum(pcounts)
    pstarts = pends - pcounts
    dest = jnp.sum(onehot * pstarts[None, :], axis=1) + rank
    block_start = jnp.arange(n_blocks, dtype=jnp.int32) * MOE_ROWS
    blk_e = jnp.minimum(jnp.sum((block_start[:, None] >= pends[None, :]).astype(jnp.int32), axis=1),
                        N_EXPERTS - 1).astype(jnp.int32)
    dest = dest.astype(jnp.int32).reshape(n_tok // TM, TM, TOP_K).transpose(0, 2, 1)
    return blk_e, dest, n_blocks * MOE_ROWS


def _final_body(dest_ref, dest_next_ref, h_ref, comb_ref, g_ref, y_hbm, o_first, o_second, ybuf, sem, *, n_first):
    i = pl.program_id(0)
    n = pl.num_programs(0)
    slot = i % 2

    def start_gather(ref, sl):
        def issue(r, c):
            for k in range(TOP_K):
                pltpu.make_async_copy(y_hbm.at[ref[0, k, r]], ybuf.at[sl, k, r], sem.at[sl]).start(priority=k)
            return c
        lax.fori_loop(0, TM, issue, 0, unroll=8)

    @pl.when(i == 0)
    def _():
        start_gather(dest_ref, 0)

    @pl.when(i + 1 < n)
    def _():
        start_gather(dest_next_ref, 1 - slot)

    for k in range(TOP_K):
        pltpu.make_async_copy(y_hbm.at[pl.ds(0, TM)], ybuf.at[slot, k], sem.at[slot]).wait()

    first, second = _tiles_to_rows(ybuf[slot, 0]), _tiles_to_rows(ybuf[slot, 1])
    x = h_ref[...] + (first * comb_ref[:, 0:1] + second * comb_ref[:, 1:2])
    out = x * lax.rsqrt(jnp.mean(x * x, axis=-1, keepdims=True) + EPS) * g_ref[...]

    @pl.when(i < n_first)
    def _():
        o_first[...] = out

    @pl.when(i >= n_first)
    def _():
        o_second[...] = out


def _final(h, y_sorted, dest, comb, g_final, *, n_first_rows):
    n_rows = h.shape[0]
    assert n_rows % TM == 0 and n_first_rows % TM == 0 and 0 < n_first_rows < n_rows
    n_steps, n_first = n_rows // TM, n_first_rows // TM
    row = lambda n: pl.BlockSpec((TM, n), lambda i: (i, 0))
    dest_spec = lambda f: pl.BlockSpec((1, TOP_K, TM), f, memory_space=pltpu.SMEM)
    return pl.pallas_call(
        functools.partial(_final_body, n_first=n_first),
        grid=(n_steps,),
        in_specs=[dest_spec(lambda i: (i, 0, 0)),
                  dest_spec(lambda i: (jnp.minimum(i + 1, n_steps - 1), 0, 0)),
                  row(D_MODEL), row(LANES), pl.BlockSpec((1, D_MODEL), lambda i: (0, 0)),
                  pl.BlockSpec(memory_space=pl.ANY)],
        out_specs=[pl.BlockSpec((TM, D_MODEL), lambda i: (jnp.minimum(i, n_first - 1), 0)),
                   pl.BlockSpec((TM, D_MODEL), lambda i: (jnp.maximum(i - n_first, 0), 0))],
        out_shape=[jax.ShapeDtypeStruct((n_first_rows, D_MODEL), F32),
                   jax.ShapeDtypeStruct((n_rows - n_first_rows, D_MODEL), F32)],
        scratch_shapes=[pltpu.VMEM((2, TOP_K, TM) + ROW_TILE, F32), pltpu.SemaphoreType.DMA((2,))],
        compiler_params=_params("arbitrary"),
    )(dest, dest, h, comb, g_final.reshape(1, D_MODEL), y_sorted)


def _reorder_last(x, shape, order):
    lead = x.shape[:-1]
    n = len(lead)
    y = x.reshape(lead + shape).transpose(tuple(range(n)) + tuple(n + o for o in order))
    return y.reshape(lead + (x.shape[-1],))


HALF_N = RWKV_N // 2


def _key_major(x):
    return _reorder_last(x, (RWKV_HEADS, RWKV_N), (1, 0))


def _key_major_inv(x):
    return _reorder_last(x, (RWKV_N, RWKV_HEADS), (1, 0))


def _value_major(x):
    return _reorder_last(x, (RWKV_HEADS, 2, HALF_N), (2, 1, 0))


def _value_major_inv(x):
    return _reorder_last(x, (HALF_N, 2, RWKV_HEADS), (2, 1, 0))


def _rwkv_cols(x, key_fn, value_fn):
    return jnp.concatenate([key_fn(x[..., :RWKV_W]), key_fn(x[..., RWKV_W:2 * RWKV_W]),
                            value_fn(x[..., 2 * RWKV_W:3 * RWKV_W]), x[..., 3 * RWKV_W:]], axis=-1)


def kernel(x_prompt, x_sample, mem_prompt, state_ret, state_rwkv, state_shift, cache_mem_k, cache_mem_v,
           g_mix, w_in, ret_gn, rwkv_mu, rwkv_w0, rwkv_w2, rwkv_a0, rwkv_a2, rwkv_g2, rwkv_k_k, rwkv_k_a,
           rwkv_r_k, rwkv_lnx_w, rwkv_lnx_b, w_out, g_mem_q, g_mem_kv, w_mq, w_mk, w_mv, w_mo, g_ffn,
           w_group_router, b_group_router, w_expert_router, b_expert_router, w_e_gate, w_e_up, w_e_down,
           g_final):
    assert w_in.shape[0] == 1, "single-layer decoder"
    bp, tp, d = x_prompt.shape
    bs, ts, _ = x_sample.shape
    np_tok, ns_tok = bp * tp, bs * ts
    assert d == D_MODEL and bp * RWKV_HEADS * 2 == LANES and bs == LANES
    l = 0
    x_parts = [x_prompt.reshape(np_tok, d), x_sample.reshape(ns_tok, d)]

    w_in_l = jnp.concatenate([w_in[l][:, :N_RET_COLS], _rwkv_cols(w_in[l][:, N_RET_COLS:], _key_major, _value_major)],
                             axis=1)
    proj = _input_projection(x_parts, g_mix[l], w_in_l)

    pos_p = np.arange(tp)
    pos_s = PAST_LEN + np.arange(ts)
    zero_ret = jnp.zeros((bp, RET_HEADS, RET_DK, RET_DV), F32)
    oret_p, sret_p = _retention(proj, zero_ret, ret_gn[l], pos_p, row0=0, n_batch=bp, t=tp, n_seq=1)
    oret_s, sret_s = _retention(proj, state_ret[l], ret_gn[l], pos_s, row0=np_tok, n_batch=bs, t=ts, n_seq=16)

    pre_w = (_rwkv_cols(rwkv_mu[l], _key_major, _value_major), _key_major(rwkv_w0[l]), _key_major(rwkv_w2[l]),
             _key_major(rwkv_a0[l]), _key_major(rwkv_a2[l]), _value_major(rwkv_g2[l]))
    zero_shift = jnp.zeros((bp, N_RWKV_COLS), F32)
    shift_in = _rwkv_cols(state_shift[l], _key_major, _value_major)
    r_p, k_p, v_p, w_p, a_p, gate_p, shift_p = _rwkv_pre(proj, zero_shift, *pre_w, row0=0, n_batch=bp, t=tp, c=256)
    r_s, k_s, v_s, w_s, a_s, gate_s, shift_s = _rwkv_pre_short(proj, shift_in, *pre_w, row0=np_tok, n_batch=bs,
                                                                t=ts)

    kvec = lambda v: v.reshape(RWKV_HEADS, RWKV_N)
    key_par = lambda v: jnp.broadcast_to(kvec(v).T[:, None, None, :], (RWKV_N, 2, bp, RWKV_HEADS)).reshape(
        RWKV_N, LANES)

    val_par = lambda v: jnp.broadcast_to(
        v.reshape(RWKV_HEADS, 2, HALF_N).transpose(2, 1, 0)[:, :, None, :],
        (HALF_N, 2, bp, RWKV_HEADS)).reshape(HALF_N, LANES)
    mem_shape = (N_MEM, MEM_HEADS, MEM_DH)
    y_p, srw_p, cache_k_heads, cache_v_heads = _rwkv_scan_prompt(
        r_p, k_p, w_p, a_p, v_p, key_par(rwkv_k_k[l]), key_par(rwkv_k_a[l]), key_par(rwkv_r_k[l]),
        val_par(rwkv_lnx_w[l]), val_par(rwkv_lnx_b[l]), jnp.zeros((HALF_N, RWKV_N, LANES), F32),
        cache_mem_k.reshape(bs, *mem_shape), cache_mem_v.reshape(bs, *mem_shape))
    srw_p = srw_p.reshape(HALF_N, RWKV_N, 2, bp, RWKV_HEADS).transpose(3, 4, 2, 0, 1).reshape(
        bp, RWKV_HEADS, RWKV_N, RWKV_N)
    head_par = lambda v: jnp.broadcast_to(kvec(v)[:, :, None], (RWKV_HEADS, RWKV_N, LANES))
    val_rows = lambda v: v.reshape(RWKV_HEADS, 2, HALF_N).transpose(0, 2, 1).reshape(RWKV_HEADS * RWKV_N)
    state_s = state_rwkv[l].astype(F32).reshape(bs, RWKV_HEADS, 2, HALF_N, RWKV_N).transpose(1, 3, 2, 4, 0)
    y_s, srw_s = _rwkv_scan(
        r_s, k_s, w_s, a_s, v_s, head_par(rwkv_k_k[l]), head_par(rwkv_k_a[l]), head_par(rwkv_r_k[l]),
        head_par(val_rows(rwkv_lnx_w[l])), head_par(val_rows(rwkv_lnx_b[l])),
        state_s.reshape(RWKV_HEADS, RWKV_N, RWKV_N, bs), tc=ts, halves=1)
    y_s = y_s.transpose(3, 1, 2, 0).reshape(ns_tok, RWKV_W)
    srw_s = srw_s.reshape(RWKV_HEADS, HALF_N, 2, RWKV_N, bs).transpose(4, 0, 2, 1, 3).reshape(
        bs, RWKV_HEADS, RWKV_N, RWKV_N)

    w_rwkv_out = _value_major(w_out[l][RET_W:].T).T
    h, q = _merge(x_parts, [oret_p, oret_s], [gate_p, gate_s], y_p, y_s, w_out[l][:RET_W], w_rwkv_out,
                  g_mem_q[l], w_mq[l])

    mk, mv, mk_heads, mv_heads = _mem_kv(mem_prompt, g_mem_kv[l], w_mk[l], w_mv[l])
    att_p = _attention(q, mk_heads, mv_heads, row0=0, n_batch=bp, t=tp, n_seq=1, tq=tp)
    att_s = _attention(q, cache_k_heads, cache_v_heads, row0=np_tok, n_batch=bs, t=ts, n_seq=16, tq=ts)

    h = _attn_out([att_p, att_s], w_mo[l], h)

    hn, ids, comb = _router(h, g_ffn[l], w_group_router[l], b_group_router[l], w_expert_router[l],
                            b_expert_router[l])
    blk_e, dest, n_sorted = _route_plan(ids[:, :TOP_K])
    y_sorted = _experts(_dispatch(hn, dest, n_sorted), blk_e, w_e_gate[l], w_e_up[l], w_e_down[l])
    y_prompt, y_sample = _final(h, y_sorted, dest, comb, g_final, n_first_rows=np_tok)
    y_prompt = y_prompt.reshape(bp, tp, d)
    y_sample = y_sample.reshape(bs, ts, d)

    shift_p = _rwkv_cols(shift_p.reshape(bp, N_RWKV_COLS), _key_major_inv, _value_major_inv)
    shift_s = _rwkv_cols(shift_s.reshape(bs, N_RWKV_COLS), _key_major_inv, _value_major_inv)
    return (y_prompt, y_sample, sret_p[None], srw_p[None], shift_p[None],
            mk.reshape(1, bp, *mem_shape), mv.reshape(1, bp, *mem_shape),
            sret_s[None], srw_s[None], shift_s[None])
```

```python
import functools

import numpy as np
import jax
import jax.numpy as jnp
from jax import lax
from jax.experimental import pallas as pl
from jax.experimental.pallas import tpu as pltpu

F32 = jnp.float32
BF16 = jnp.bfloat16

D_MODEL = 1024
PAST_LEN = 16384
N_MEM = 256
MEM_HEADS = 4
MEM_DH = D_MODEL // MEM_HEADS
RET_HEADS = 4
RET_W = D_MODEL // 2
RET_DV = RET_W // RET_HEADS
RET_DK = RET_DV // 2
RET_QK = RET_HEADS * RET_DK
RET_CHUNK = 128
ROPE_BASE = 10000.0
RWKV_N = 64
RWKV_W = D_MODEL - RET_W
RWKV_HEADS = RWKV_W // RWKV_N
LORA_W = 64
LORA_A = 64
LORA_G = 128
LNX_EPS = 64e-5
N_RET_COLS = 2 * RET_QK + 2 * RET_W
N_RWKV_COLS = 3 * RWKV_W + LORA_W + LORA_A + LORA_G
N_IN_COLS = N_RET_COLS + N_RWKV_COLS
N_GROUPS = 4
EXP_PER_GROUP = 8
N_EXPERTS = N_GROUPS * EXP_PER_GROUP
TOP_K = 2
D_EXPERT = D_MODEL // 2
EPS = 1e-6

LANES = 128
MOE_ROWS = 256
TM = 512


def _params(*sem):
    return pltpu.CompilerParams(dimension_semantics=sem)


def _row_part_specs(parts, tm):
    specs, counts, start = [], [], 0
    for part in parts:
        nb = part.shape[0] // tm
        assert nb * tm == part.shape[0]
        specs.append(pl.BlockSpec((tm, part.shape[1]), lambda i, s=start, n=nb: (jnp.clip(i - s, 0, n - 1), 0)))
        counts.append(nb)
        start += nb
    return specs, counts


def _read_row_parts(refs, counts):
    i = pl.program_id(0)
    x = refs[0][...]
    start = counts[0]
    for ref, nb in zip(refs[1:], counts[1:]):
        x = jnp.where(i >= start, ref[...], x)
        start += nb
    return x


def _proj_body(*refs, part_counts):
    n = len(part_counts)
    g_ref, w_ref, o_ref = refs[n:]
    x = _read_row_parts(refs[:n], part_counts).astype(F32)
    xb = (x * lax.rsqrt(jnp.mean(x * x, axis=-1, keepdims=True) + EPS) * g_ref[...]).astype(BF16)
    n_chunk = 256
    for j in range(0, w_ref.shape[1], n_chunk):
        o_ref[:, j:j + n_chunk] = jnp.dot(xb, w_ref[:, j:j + n_chunk], preferred_element_type=F32)


def _input_projection(x_parts, gain, w):
    k, n_out = w.shape
    m = sum(part.shape[0] for part in x_parts)
    assert n_out % 256 == 0
    in_specs, part_counts = _row_part_specs(x_parts, TM)
    return pl.pallas_call(
        functools.partial(_proj_body, part_counts=part_counts),
        grid=(m // TM,),
        in_specs=in_specs + [pl.BlockSpec((1, k), lambda i: (0, 0)), pl.BlockSpec((k, n_out), lambda i: (0, 0))],
        out_specs=pl.BlockSpec((TM, n_out), lambda i: (i, 0)),
        out_shape=jax.ShapeDtypeStruct((m, n_out), F32),
        compiler_params=_params("parallel"),
    )(*x_parts, gain.reshape(1, k).astype(F32), w.astype(BF16))


def _rot_tables(pos):
    half = RET_DK // 2
    inv_freq = ROPE_BASE ** (-(np.arange(half, dtype=np.float64) / half))
    ang = pos.astype(np.float64)[:, None] * inv_freq[None, :]
    cos, sin = np.cos(ang), np.sin(ang)
    zero = np.zeros_like(sin)
    c = np.tile(np.concatenate([cos, cos], axis=1), (1, RET_HEADS))
    s_lo = np.tile(np.concatenate([-sin, zero], axis=1), (1, RET_HEADS))
    s_hi = np.tile(np.concatenate([zero, sin], axis=1), (1, RET_HEADS))
    return [jnp.asarray(t, F32) for t in (c, s_lo, s_hi)]


def _ret_decay_tables(c):
    lg = np.log1p(-np.exp2(-5.0 - np.arange(RET_HEADS, dtype=np.float64)))
    idx = np.arange(c, dtype=np.float64)
    diff = idx[:, None] - idx[None, :]
    mask = np.where(diff[None] >= 0, np.exp(np.maximum(diff, 0.0)[None] * lg[:, None, None]), 0.0)
    q_dec = np.repeat(np.exp((idx[:, None] + 1.0) * lg[None, :]), RET_DV, axis=1)
    k_dec = np.repeat(np.exp((c - 1.0 - idx)[:, None] * lg[None, :]), RET_DK, axis=1)
    c_dec = [float(v) for v in np.exp(c * lg)]
    return jnp.asarray(mask, F32), jnp.asarray(q_dec, F32), jnp.asarray(k_dec, F32), c_dec


def _ret_body(*refs, n_blk, per_blk, c, c_dec):
    q_refs, k_refs, v_refs, gate_refs = (refs[j * n_blk:(j + 1) * n_blk] for j in range(4))
    (c_ref, slo_ref, shi_ref, mask_ref, qdec_ref, kdec_ref, gn_ref, s0_ref, o_ref, sout_ref,
     s_scr) = refs[4 * n_blk:]
    n_seq = n_blk * per_blk
    ci = pl.program_id(1)

    @pl.when(ci == 0)
    def _():
        s_scr[...] = s0_ref[...].astype(F32)

    cos, s_lo, s_hi = c_ref[...], slo_ref[...], shi_ref[...]
    half = RET_DK // 2

    def rope(x):
        return x * cos + pltpu.roll(x, RET_QK - half, 1) * s_lo + pltpu.roll(x, half, 1) * s_hi

    nt = (((1,), (1,)), ((), ()))
    tn = (((0,), (0,)), ((), ()))
    for g in range(n_seq):
        blk = g // per_blk
        rows = slice((g % per_blk) * c, (g % per_blk + 1) * c)
        q = rope(q_refs[blk][rows, :].astype(F32))
        k = rope(k_refs[blk][rows, :].astype(F32)) * (RET_DK ** -0.5)
        k_st = k * kdec_ref[...]
        for h in range(RET_HEADS):
            kc = slice(h * RET_DK, (h + 1) * RET_DK)
            vc = slice(h * RET_DV, (h + 1) * RET_DV)
            qh = q[:, kc].astype(BF16)
            vh = v_refs[blk][rows, vc].astype(BF16)
            s_h = s_scr[g, h]
            att = lax.dot_general(qh, k[:, kc].astype(BF16), nt, preferred_element_type=F32) * mask_ref[h]
            o = jnp.dot(att.astype(BF16), vh, preferred_element_type=F32)
            o = o + jnp.dot(qh, s_h.astype(BF16), preferred_element_type=F32) * qdec_ref[:, vc]
            s_scr[g, h] = s_h * c_dec[h] + lax.dot_general(
                k_st[:, kc].astype(BF16), vh, tn, preferred_element_type=F32)
            o = o * lax.rsqrt(jnp.mean(o * o, axis=-1, keepdims=True) + EPS)
            gate = gate_refs[blk][rows, vc].astype(F32)
            o_ref[g, :, vc] = o * gn_ref[:, vc] * (gate * jax.nn.sigmoid(gate))

    @pl.when(ci == pl.num_programs(1) - 1)
    def _():
        sout_ref[...] = s_scr[...]


def _retention(proj, s0, ret_gn, pos, *, row0, n_batch, t, n_blk, per_blk):
    c = RET_CHUNK if t % RET_CHUNK == 0 else t
    n_chunks = t // c
    rows = per_blk * c
    n_seq = n_blk * per_blk
    assert n_batch % n_seq == 0 and row0 % rows == 0 and (per_blk == 1 or n_chunks == 1)
    blk0 = row0 // rows
    mask, q_dec, k_dec, c_dec = _ret_decay_tables(c)
    cos, s_lo, s_hi = _rot_tables(pos)

    def const2(b, ci):
        return (0, 0)

    def row_specs(width, col):
        return [pl.BlockSpec((rows, width), lambda b, ci, j=j: (blk0 + (b * n_blk + j) * n_chunks + ci, col))
                for j in range(n_blk)]

    state_spec = pl.BlockSpec((n_seq, RET_HEADS, RET_DK, RET_DV), lambda b, ci: (b, 0, 0, 0))
    in_specs = row_specs(RET_QK, 0) + row_specs(RET_QK, 1) + row_specs(RET_W, 1) + row_specs(RET_W, 2) + [
        pl.BlockSpec((c, RET_QK), lambda b, ci: (ci, 0)),
        pl.BlockSpec((c, RET_QK), lambda b, ci: (ci, 0)),
        pl.BlockSpec((c, RET_QK), lambda b, ci: (ci, 0)),
        pl.BlockSpec((RET_HEADS, c, c), lambda b, ci: (0, 0, 0)),
        pl.BlockSpec((c, RET_W), const2),
        pl.BlockSpec((c, RET_QK), const2),
        pl.BlockSpec((1, RET_W), const2),
        state_spec,
    ]
    return pl.pallas_call(
        functools.partial(_ret_body, n_blk=n_blk, per_blk=per_blk, c=c, c_dec=c_dec),
        grid=(n_batch // n_seq, n_chunks),
        in_specs=in_specs,
        out_specs=[pl.BlockSpec((n_seq, c, RET_W), lambda b, ci: (b, ci, 0)), state_spec],
        out_shape=[jax.ShapeDtypeStruct((n_batch, t, RET_W), F32),
                   jax.ShapeDtypeStruct((n_batch, RET_HEADS, RET_DK, RET_DV), F32)],
        scratch_shapes=[pltpu.VMEM((n_seq, RET_HEADS, RET_DK, RET_DV), F32)],
        compiler_params=_params("parallel", "arbitrary"),
    )(*([proj] * (4 * n_blk)), cos, s_lo, s_hi, mask, q_dec, k_dec, ret_gn.reshape(1, RET_W).astype(F32), s0)


LORA_COLS = LORA_W + LORA_A + LORA_G


def _rwkv_lora_terms(lo, w0_ref, w2_ref, a0_ref, a2_ref, g2_ref):
    hw = lo[:, :LORA_W]
    ha = lo[:, LORA_W:LORA_W + LORA_A]
    hg = lo[:, LORA_W + LORA_A:]
    u = w0_ref[...] + jnp.dot(jnp.tanh(hw).astype(BF16), w2_ref[...], preferred_element_type=F32)
    decay = jnp.exp(-float(np.exp(-0.5)) * jax.nn.sigmoid(u))
    rate = jax.nn.sigmoid(a0_ref[...] + jnp.dot(ha.astype(BF16), a2_ref[...], preferred_element_type=F32))
    gate = jnp.dot(jax.nn.sigmoid(hg).astype(BF16), g2_ref[...], preferred_element_type=F32)
    return decay, rate, gate


def _rwkv_pre_body(r_ref, k_ref, v_ref, lo_ref, shift_ref, mu_ref, w0_ref, w2_ref, a0_ref, a2_ref, g2_ref,
                   ro_ref, ko_ref, vo_ref, wo_ref, ao_ref, go_ref, so_ref, prev_scr):
    ci = pl.program_id(1)
    c = r_ref.shape[0]

    @pl.when(ci == 0)
    def _():
        prev_scr[...] = shift_ref[0].astype(F32)

    first_row = lax.broadcasted_iota(jnp.int32, (c, 1), 0) == 0

    def shifted(x_ref, col0):
        w = x_ref.shape[1]
        x = x_ref[...].astype(F32)
        prev = jnp.where(first_row, prev_scr[:, col0:col0 + w], pltpu.roll(x, 1, 0))
        prev_scr[:, col0:col0 + w] = x[c - 1:c, :]
        return x + (prev - x) * mu_ref[:, col0:col0 + w]

    ro_ref[...] = shifted(r_ref, 0).T
    ko_ref[...] = shifted(k_ref, RWKV_W).T
    vo_ref[...] = shifted(v_ref, 2 * RWKV_W).T
    decay, rate, gate = _rwkv_lora_terms(shifted(lo_ref, 3 * RWKV_W), w0_ref, w2_ref, a0_ref, a2_ref, g2_ref)
    wo_ref[...] = decay.T
    ao_ref[...] = rate.T
    go_ref[...] = gate

    @pl.when(ci == pl.num_programs(1) - 1)
    def _():
        so_ref[0] = prev_scr[...]


def _rwkv_pre_args(s_shift, n_batch, mu, w0, w2, a0, a2, g2):
    return (s_shift.reshape(n_batch, 1, N_RWKV_COLS), mu.reshape(1, -1), w0.reshape(1, -1), w2.astype(BF16),
            a0.reshape(1, -1), a2.astype(BF16), g2.astype(BF16))


def _rwkv_pre_weight_specs(const):
    return [pl.BlockSpec((1, N_RWKV_COLS), const), pl.BlockSpec((1, RWKV_W), const),
            pl.BlockSpec((LORA_W, RWKV_W), const), pl.BlockSpec((1, RWKV_W), const),
            pl.BlockSpec((LORA_A, RWKV_W), const), pl.BlockSpec((LORA_G, RWKV_W), const)]


def _rwkv_pre(proj, s_shift, mu, w0, w2, a0, a2, g2, *, row0, n_batch, t, c):
    n_chunks = t // c
    assert t % c == 0 and row0 % c == 0
    blk0 = row0 // c
    col_r = N_RET_COLS // RWKV_W
    col_lo = (N_RET_COLS + 3 * RWKV_W) // LORA_COLS
    assert col_r * RWKV_W == N_RET_COLS and col_lo * LORA_COLS == N_RET_COLS + 3 * RWKV_W

    def row_map(col):
        return lambda b, ci: (blk0 + b * n_chunks + ci, col)

    state_spec = pl.BlockSpec((1, 1, N_RWKV_COLS), lambda b, ci: (b, 0, 0))
    in_specs = [pl.BlockSpec((c, RWKV_W), row_map(col_r)), pl.BlockSpec((c, RWKV_W), row_map(col_r + 1)),
                pl.BlockSpec((c, RWKV_W), row_map(col_r + 2)), pl.BlockSpec((c, LORA_COLS), row_map(col_lo)),
                state_spec] + _rwkv_pre_weight_specs(lambda b, ci: (0, 0))
    vec_spec = pl.BlockSpec((None, RWKV_W, c), lambda b, ci: (b, 0, ci))
    vec_shape = jax.ShapeDtypeStruct((n_batch, RWKV_W, t), F32)
    return pl.pallas_call(
        _rwkv_pre_body,
        grid=(n_batch, n_chunks),
        in_specs=in_specs,
        out_specs=[vec_spec] * 5 + [pl.BlockSpec((c, RWKV_W), lambda b, ci: (b * n_chunks + ci, 0)), state_spec],
        out_shape=[vec_shape] * 5 + [jax.ShapeDtypeStruct((n_batch * t, RWKV_W), F32),
                                     jax.ShapeDtypeStruct((n_batch, 1, N_RWKV_COLS), F32)],
        scratch_shapes=[pltpu.VMEM((1, N_RWKV_COLS), F32)],
        compiler_params=_params("parallel", "arbitrary"),
    )(proj, proj, proj, proj, *_rwkv_pre_args(s_shift, n_batch, mu, w0, w2, a0, a2, g2))


def _rwkv_pre_short_body(r_ref, k_ref, v_ref, lo_ref, shift_ref, mu_ref, w0_ref, w2_ref, a0_ref, a2_ref, g2_ref,
                         ro_ref, ko_ref, vo_ref, wo_ref, ao_ref, go_ref, so_ref, *, n_b, t):
    rows = n_b * t
    first_tok = (lax.broadcasted_iota(jnp.int32, (rows, 1), 0) & (t - 1)) == 0

    def shifted(x_ref, col0):
        w = x_ref.shape[1]
        x = x_ref[...].astype(F32)
        carried = jnp.broadcast_to(shift_ref[:, :, col0:col0 + w].astype(F32), (n_b, t, w)).reshape(rows, w)
        prev = jnp.where(first_tok, carried, pltpu.roll(x, 1, 0))
        so_ref[:, :, col0:col0 + w] = x.reshape(n_b, t, w)[:, t - 1:t, :]
        return x + (prev - x) * mu_ref[:, col0:col0 + w]

    def put(o_ref, x):
        by_tok = jnp.swapaxes(x.reshape(n_b, t, RWKV_W), 0, 1)
        for ti in range(t):
            feat = by_tok[ti].T.reshape(RWKV_N, RWKV_HEADS, n_b)
            o_ref[:, ti] = jnp.swapaxes(feat, 0, 1)

    put(ro_ref, shifted(r_ref, 0))
    put(ko_ref, shifted(k_ref, RWKV_W))
    put(vo_ref, shifted(v_ref, 2 * RWKV_W))
    decay, rate, gate = _rwkv_lora_terms(shifted(lo_ref, 3 * RWKV_W), w0_ref, w2_ref, a0_ref, a2_ref, g2_ref)
    put(wo_ref, decay)
    put(ao_ref, rate)
    go_ref[...] = gate


def _rwkv_pre_short(proj, s_shift, mu, w0, w2, a0, a2, g2, *, row0, n_batch, t):
    rows = n_batch * t
    assert row0 % rows == 0 and t & (t - 1) == 0
    blk0 = row0 // rows
    col_r = N_RET_COLS // RWKV_W
    col_lo = (N_RET_COLS + 3 * RWKV_W) // LORA_COLS
    state_spec = pl.BlockSpec((n_batch, 1, N_RWKV_COLS), lambda i: (0, 0, 0))
    in_specs = [pl.BlockSpec((rows, RWKV_W), lambda i: (blk0, col_r)),
                pl.BlockSpec((rows, RWKV_W), lambda i: (blk0, col_r + 1)),
                pl.BlockSpec((rows, RWKV_W), lambda i: (blk0, col_r + 2)),
                pl.BlockSpec((rows, LORA_COLS), lambda i: (blk0, col_lo)),
                state_spec] + _rwkv_pre_weight_specs(lambda i: (0, 0))
    vec_shape = (RWKV_HEADS, t, RWKV_N, n_batch)
    vec_spec = pl.BlockSpec(vec_shape, lambda i: (0, 0, 0, 0))
    return pl.pallas_call(
        functools.partial(_rwkv_pre_short_body, n_b=n_batch, t=t),
        grid=(1,),
        in_specs=in_specs,
        out_specs=[vec_spec] * 5 + [pl.BlockSpec((rows, RWKV_W), lambda i: (0, 0)), state_spec],
        out_shape=[jax.ShapeDtypeStruct(vec_shape, F32)] * 5 + [
            jax.ShapeDtypeStruct((rows, RWKV_W), F32), jax.ShapeDtypeStruct((n_batch, 1, N_RWKV_COLS), F32)],
        compiler_params=_params("arbitrary"),
    )(proj, proj, proj, proj, *_rwkv_pre_args(s_shift, n_batch, mu, w0, w2, a0, a2, g2))


def _scan_body(r_ref, k_ref, w_ref, a_ref, v_ref, kk_ref, ka_ref, rk_ref, lw_ref, lb_ref, s0_ref,
               y_ref, sout_ref, s_scr, a_scr, b_scr, km_scr, *, tc, vr, halves):
    ci = pl.program_id(1)

    @pl.when(ci == 0)
    def _():
        s_scr[...] = s0_ref[...].astype(F32)

    def ksum(x):
        return jnp.sum(x, axis=-2, keepdims=True)

    def vsum(x):
        if halves == 2:
            x2 = x.reshape(tc * vr, LANES)
            x = (x2 + pltpu.roll(x2, LANES // 2, 1)).reshape(tc, vr, LANES)
        return jnp.sum(x, axis=1, keepdims=True)

    kr = k_ref[...]
    a = a_ref[...]
    kk = kr * kk_ref[...]
    kk = kk / jnp.maximum(jnp.sqrt(ksum(kk * kk)), 1e-12)
    a_scr[...] = -kk
    b_scr[...] = kk * a
    km_scr[...] = kr * (1.0 + (a - 1.0) * ka_ref[...])

    def token(t, carry):
        r, w, avec, bvec, kmod = r_ref[t], w_ref[t], a_scr[t], b_scr[t], km_scr[t]

        def value_row(i, c2):
            s = s_scr[i]
            sa = ksum(s * avec)
            s = s * w + sa * bvec + v_ref[t, pl.ds(i, 1), :] * kmod
            s_scr[i] = s
            y_ref[t, pl.ds(i, 1), :] = ksum(s * r)
            return c2

        lax.fori_loop(0, vr, value_row, 0, unroll=16)
        return carry

    lax.fori_loop(0, tc, token, 0)

    y = y_ref[...]
    d = y - vsum(y) * (1.0 / RWKV_N)
    var = vsum(d * d) * (1.0 / RWKV_N)
    bonus = ksum(r_ref[...] * km_scr[...] * rk_ref[...])
    y_ref[...] = d * lax.rsqrt(var + LNX_EPS) * lw_ref[...] + lb_ref[...] + bonus * v_ref[...]

    @pl.when(ci == pl.num_programs(1) - 1)
    def _():
        sout_ref[...] = s_scr[...]


def _rwkv_scan(r, k, w, a, v, k_k, k_a, r_k, lnx_w, lnx_b, s0, *, tc, halves):
    n_grp, t, _, lanes = r.shape
    vr = v.shape[2]
    assert lanes == LANES and t % tc == 0 and vr * halves == RWKV_N

    def tok_spec(rows):
        return pl.BlockSpec((None, tc, rows, LANES), lambda g, ci: (g, ci, 0, 0))

    def par_spec(rows):
        return pl.BlockSpec((None, rows, LANES), lambda g, ci: (g, 0, 0))

    st_spec = pl.BlockSpec((None, vr, RWKV_N, LANES), lambda g, ci: (g, 0, 0, 0))
    key_scratch = pltpu.VMEM((tc, RWKV_N, LANES), F32)
    return pl.pallas_call(
        functools.partial(_scan_body, tc=tc, vr=vr, halves=halves),
        grid=(n_grp, t // tc),
        in_specs=[tok_spec(RWKV_N)] * 4 + [tok_spec(vr)] + [par_spec(RWKV_N)] * 3 + [par_spec(vr)] * 2 + [st_spec],
        out_specs=[tok_spec(vr), st_spec],
        out_shape=[jax.ShapeDtypeStruct((n_grp, t, vr, LANES), F32),
                   jax.ShapeDtypeStruct((n_grp, vr, RWKV_N, LANES), F32)],
        scratch_shapes=[pltpu.VMEM((vr, RWKV_N, LANES), F32), key_scratch, key_scratch, key_scratch],
        compiler_params=_params("parallel", "arbitrary"),
    )(r, k, w, a, v, k_k, k_a, r_k, lnx_w, lnx_b, s0)


SCAN_TC = 128
SCAN_SUB = 64
MEM_CHUNKS = 4


def _scan_prompt_body(r_ref, k_ref, w_ref, a_ref, v_ref, kk_ref, ka_ref, rk_ref, lw_ref, lb_ref, s0_ref,
                      memk_hbm, memv_hbm, y_ref, sout_ref, memk_out, memv_out,
                      s_scr, r_c, w_c, a_c, b_c, km_c, v_c, y_c, stash, mem_stage, mem_in_sem, mem_out_sem,
                      *, n_b, mem_seqs):
    ci = pl.program_id(0)
    vr = RWKV_N // 2
    ts = SCAN_SUB
    tile = RWKV_HEADS
    half_lanes = LANES // 2

    @pl.when(ci == 0)
    def _():
        s_scr[...] = s0_ref[...].astype(F32)

    chunk_seqs = mem_seqs // MEM_CHUNKS

    def mem_in(chunk, slot):
        copies = []
        for j in range(chunk_seqs):
            seq = (ci * MEM_CHUNKS + chunk) * chunk_seqs + j
            for h in range(MEM_HEADS):
                copies.append(pltpu.make_async_copy(memk_hbm.at[seq, :, h, :], mem_stage.at[slot, 0, j, h],
                                                    mem_in_sem.at[slot]))
                copies.append(pltpu.make_async_copy(memv_hbm.at[seq, :, h, :], mem_stage.at[slot, 1, j, h],
                                                    mem_in_sem.at[slot]))
        return copies

    def mem_out(chunk, slot):
        seqs = pl.ds((ci * MEM_CHUNKS + chunk) * chunk_seqs, chunk_seqs)
        return [pltpu.make_async_copy(mem_stage.at[slot, 0], memk_out.at[seqs], mem_out_sem.at[slot]),
                pltpu.make_async_copy(mem_stage.at[slot, 1], memv_out.at[seqs], mem_out_sem.at[slot])]

    def mem_phase(p):
        if 1 <= p <= MEM_CHUNKS:
            for cp in mem_in(p - 1, (p - 1) % 2):
                cp.wait()
            for cp in mem_out(p - 1, (p - 1) % 2):
                cp.start()
        if 2 <= p <= MEM_CHUNKS + 1:
            for cp in mem_out(p - 2, p % 2):
                cp.wait()
        if p < MEM_CHUNKS:
            for cp in mem_in(p, p % 2):
                cp.start()

    mem_phase(0)

    low = lax.broadcasted_iota(jnp.int32, (ts, LANES), 1) < half_lanes

    def feature_pair_rows(x_ref, base):
        tiles = [x_ref[b, pl.ds(base + f * tile, tile), :] for f in range(2) for b in range(n_b)]
        return jnp.concatenate(tiles, axis=0).T

    def key_to_chain(x_ref, dst, stash, t0):
        def group(g, c):
            rows = []
            for j in range(4):
                pair = g * 4 + j
                if t0 == 0:
                    full = feature_pair_rows(x_ref, pl.multiple_of(pair * 2 * tile, 2 * tile))
                    mt = full[:ts]
                    stash[pair] = full[ts:]
                else:
                    mt = stash[pair]
                sw = pltpu.roll(mt, half_lanes, 1)
                rows += [jnp.where(low, mt, sw), jnp.where(low, sw, mt)]
            dst[:, pl.ds(pl.multiple_of(g * 8, 8), 8), :] = jnp.swapaxes(jnp.stack(rows, axis=0), 0, 1)
            return c
        lax.fori_loop(0, RWKV_N // 8, group, 0, unroll=4)

    def value_to_chain(g, c):
        rows = [feature_pair_rows(v_ref, pl.multiple_of((g * 8 + j) * 2 * tile, 2 * tile)) for j in range(8)]
        v_c[:, pl.ds(pl.multiple_of(g * 8, 8), 8), :] = jnp.swapaxes(jnp.stack(rows, axis=0), 0, 1)
        return c
    lax.fori_loop(0, vr // 8, value_to_chain, 0, unroll=2)

    def ksum(x):
        return jnp.sum(x, axis=-2, keepdims=True)

    for t0 in range(0, SCAN_TC, ts):
        key_to_chain(r_ref, r_c, stash.at[0], t0)
        key_to_chain(w_ref, w_c, stash.at[1], t0)
        key_to_chain(k_ref, km_c, stash.at[2], t0)
        key_to_chain(a_ref, b_c, stash.at[3], t0)

        def prep(g8, c):
            toks = pl.ds(pl.multiple_of(g8 * 8, 8), 8)
            kr = km_c[toks]
            a = b_c[toks]
            kk = kr * kk_ref[...]
            kk = kk / jnp.maximum(jnp.sqrt(ksum(kk * kk)), 1e-12)
            a_c[toks] = -kk
            b_c[toks] = kk * a
            km_c[toks] = kr * (1.0 + (a - 1.0) * ka_ref[...])
            return c
        lax.fori_loop(0, ts // 8, prep, 0)
        mem_phase(1 + 2 * (t0 // ts))

        def token(t, carry):
            r, w, avec, bvec, kmod = r_c[t], w_c[t], a_c[t], b_c[t], km_c[t]

            def value_row(i, c2):
                s = s_scr[i]
                sa = ksum(s * avec)
                s = s * w + sa * bvec + v_c[t0 + t, pl.ds(i, 1), :] * kmod
                s_scr[i] = s
                y_c[t0 + t, pl.ds(i, 1), :] = ksum(s * r)
                return c2

            lax.fori_loop(0, vr, value_row, 0, unroll=True)
            return carry

        lax.fori_loop(0, ts, token, 0)
        mem_phase(2 + 2 * (t0 // ts))

        def post(g8, c):
            ktoks = pl.ds(pl.multiple_of(g8 * 8, 8), 8)
            vtoks = pl.ds(pl.multiple_of(t0 + g8 * 8, 8), 8)

            def vsum(x):
                x2 = x.reshape(8 * vr, LANES)
                x2 = x2 + pltpu.roll(x2, half_lanes, 1)
                return jnp.sum(x2.reshape(8, vr, LANES), axis=1, keepdims=True)

            y = y_c[vtoks]
            d = y - vsum(y) * (1.0 / RWKV_N)
            var = vsum(d * d) * (1.0 / RWKV_N)
            bonus = ksum(r_c[ktoks] * km_c[ktoks] * rk_ref[...])
            y_c[vtoks] = d * lax.rsqrt(var + LNX_EPS) * lw_ref[...] + lb_ref[...] + bonus * v_c[vtoks]
            return c
        lax.fori_loop(0, ts // 8, post, 0, unroll=4)

    def value_from_chain(g, c):
        blk = jnp.swapaxes(y_c[:, pl.ds(pl.multiple_of(g * 8, 8), 8), :], 0, 1)
        for j in range(8):
            mt = blk[j].T
            base = pl.multiple_of((g * 8 + j) * 2 * tile, 2 * tile)
            for hf in range(2):
                for b in range(n_b):
                    row0 = (hf * n_b + b) * tile
                    y_ref[b, pl.ds(base + hf * tile, tile), :] = mt[row0:row0 + tile, :]
        return c
    lax.fori_loop(0, vr // 8, value_from_chain, 0, unroll=2)

    assert 2 * (SCAN_TC // ts) == MEM_CHUNKS
    mem_phase(MEM_CHUNKS + 1)

    @pl.when(ci == pl.num_programs(0) - 1)
    def _():
        sout_ref[...] = s_scr[...]


def _rwkv_scan_prompt(r, k, w, a, v, k_k, k_a, r_k, lnx_w, lnx_b, s0, mem_k, mem_v):
    n_b, _, t = r.shape
    vr = RWKV_N // 2
    n_steps = t // SCAN_TC
    n_mem_seq = mem_k.shape[0]
    assert t % SCAN_TC == 0 and 2 * n_b * RWKV_HEADS == LANES and n_mem_seq % (n_steps * MEM_CHUNKS) == 0
    chunk_seqs = n_mem_seq // (n_steps * MEM_CHUNKS)
    any_spec = pl.BlockSpec(memory_space=pl.ANY)
    mem_shape = jax.ShapeDtypeStruct((n_mem_seq, MEM_HEADS, N_MEM, MEM_DH), F32)
    tok_spec = pl.BlockSpec((n_b, RWKV_W, SCAN_TC), lambda ci: (0, 0, ci))
    key_par = pl.BlockSpec((RWKV_N, LANES), lambda ci: (0, 0))
    val_par = pl.BlockSpec((vr, LANES), lambda ci: (0, 0))
    st_spec = pl.BlockSpec((vr, RWKV_N, LANES), lambda ci: (0, 0, 0))
    key_chain = pltpu.VMEM((SCAN_SUB, RWKV_N, LANES), F32)
    val_chain = pltpu.VMEM((SCAN_TC, vr, LANES), F32)
    return pl.pallas_call(
        functools.partial(_scan_prompt_body, n_b=n_b, mem_seqs=n_mem_seq // n_steps),
        grid=(n_steps,),
        in_specs=[tok_spec] * 5 + [key_par] * 3 + [val_par] * 2 + [st_spec, any_spec, any_spec],
        out_specs=[tok_spec, st_spec, any_spec, any_spec],
        out_shape=[jax.ShapeDtypeStruct((n_b, RWKV_W, t), F32),
                   jax.ShapeDtypeStruct((vr, RWKV_N, LANES), F32), mem_shape, mem_shape],
        scratch_shapes=([pltpu.VMEM((vr, RWKV_N, LANES), F32)] + [key_chain] * 5 + [val_chain] * 2
                        + [pltpu.VMEM((4, RWKV_N // 2, SCAN_TC - SCAN_SUB, LANES), F32),
                           pltpu.VMEM((2, 2, chunk_seqs, MEM_HEADS, N_MEM, MEM_DH), F32),
                           pltpu.SemaphoreType.DMA((2,)), pltpu.SemaphoreType.DMA((2,))]),
        compiler_params=_params("arbitrary"),
    )(r, k, w, a, v, k_k, k_a, r_k, lnx_w, lnx_b, s0, mem_k, mem_v)


def _merge_body(*refs, part_counts):
    n = len(part_counts)
    x_refs, oret_refs, g_refs = refs[:n], refs[n:2 * n], refs[2 * n:3 * n]
    yt_ref, ys_ref, wt_ref, wb_ref, gq_ref, wq_ref, o_ref, q_ref = refs[3 * n:]
    x = _read_row_parts(x_refs, part_counts)
    y = jnp.where(pl.program_id(0) >= part_counts[0], ys_ref[...], yt_ref[...].T)
    yb = (y * _read_row_parts(g_refs, part_counts)).astype(BF16)
    ob = _read_row_parts(oret_refs, part_counts).astype(BF16)
    n_chunk = 256
    for j in range(0, D_MODEL, n_chunk):
        acc = jnp.dot(ob, wt_ref[:, j:j + n_chunk], preferred_element_type=F32)
        acc = acc + jnp.dot(yb, wb_ref[:, j:j + n_chunk], preferred_element_type=F32)
        o_ref[:, j:j + n_chunk] = x[:, j:j + n_chunk] + acc
    h = o_ref[...]
    hb = (h * lax.rsqrt(jnp.mean(h * h, axis=-1, keepdims=True) + EPS) * gq_ref[...]).astype(BF16)
    for j in range(0, D_MODEL, n_chunk):
        q_ref[:, j:j + n_chunk] = jnp.dot(hb, wq_ref[:, j:j + n_chunk], preferred_element_type=F32)


def _merge(x_parts, oret_parts, g_parts, y_first_t, y_second, w_ret, w_rwkv, gain_q, w_q):
    m = sum(part.shape[0] for part in x_parts)
    in_specs, part_counts = [], None
    for parts in (x_parts, oret_parts, g_parts):
        specs, part_counts = _row_part_specs(parts, TM)
        in_specs += specs
    assert len(part_counts) == 2
    n_first = part_counts[0]
    tiles = y_first_t.shape[2] // TM
    assert y_first_t.shape[0] * tiles == n_first
    yt_spec = pl.BlockSpec((None, RWKV_W, TM),
                           lambda i: (jnp.minimum(i, n_first - 1) // tiles, 0, jnp.minimum(i, n_first - 1) % tiles))
    ys_spec = pl.BlockSpec((TM, RWKV_W), lambda i: (jnp.clip(i - n_first, 0, part_counts[1] - 1), 0))
    wspec = pl.BlockSpec((RET_W, D_MODEL), lambda i: (0, 0))
    row_spec = pl.BlockSpec((TM, D_MODEL), lambda i: (i, 0))
    row_shape = jax.ShapeDtypeStruct((m, D_MODEL), F32)
    return pl.pallas_call(
        functools.partial(_merge_body, part_counts=part_counts),
        grid=(m // TM,),
        in_specs=in_specs + [yt_spec, ys_spec, wspec, wspec, pl.BlockSpec((1, D_MODEL), lambda i: (0, 0)),
                             pl.BlockSpec((D_MODEL, D_MODEL), lambda i: (0, 0))],
        out_specs=[row_spec, row_spec],
        out_shape=[row_shape, row_shape],
        compiler_params=_params("parallel"),
    )(*x_parts, *oret_parts, *g_parts, y_first_t, y_second, w_ret.astype(BF16), w_rwkv.astype(BF16),
      gain_q.reshape(1, D_MODEL), w_q.astype(BF16))


def _mem_kv_body(x_ref, g_ref, wk_ref, wv_ref, k_ref, v_ref, kh_ref, vh_ref, *, n_seq):
    x = x_ref[...].astype(F32)
    xb = (x * lax.rsqrt(jnp.mean(x * x, axis=-1, keepdims=True) + EPS) * g_ref[...]).astype(BF16)
    for w_ref, o_ref, oh_ref in ((wk_ref, k_ref, kh_ref), (wv_ref, v_ref, vh_ref)):
        for h in range(MEM_HEADS):
            acc = jnp.dot(xb, w_ref[:, h * MEM_DH:(h + 1) * MEM_DH], preferred_element_type=F32)
            o_ref[:, h, :] = acc
            for s in range(n_seq):
                oh_ref[s, h] = acc[s * N_MEM:(s + 1) * N_MEM]


def _mem_kv(mem, gain, w_k, w_v):
    n_b = mem.shape[0]
    n_seq = TM // N_MEM
    assert n_seq * N_MEM == TM and n_b % n_seq == 0
    wspec = pl.BlockSpec((D_MODEL, D_MODEL), lambda i: (0, 0))
    tok_spec = pl.BlockSpec((TM, MEM_HEADS, MEM_DH), lambda i: (i, 0, 0))
    head_spec = pl.BlockSpec((n_seq, MEM_HEADS, N_MEM, MEM_DH), lambda i: (i, 0, 0, 0))
    tok_shape = jax.ShapeDtypeStruct((n_b * N_MEM, MEM_HEADS, MEM_DH), F32)
    head_shape = jax.ShapeDtypeStruct((n_b, MEM_HEADS, N_MEM, MEM_DH), F32)
    return pl.pallas_call(
        functools.partial(_mem_kv_body, n_seq=n_seq),
        grid=(n_b // n_seq,),
        in_specs=[pl.BlockSpec((TM, D_MODEL), lambda i: (i, 0)), pl.BlockSpec((1, D_MODEL), lambda i: (0, 0)),
                  wspec, wspec],
        out_specs=[tok_spec, tok_spec, head_spec, head_spec],
        out_shape=[tok_shape, tok_shape, head_shape, head_shape],
        compiler_params=_params("parallel"),
    )(mem.reshape(n_b * N_MEM, D_MODEL), gain.reshape(1, D_MODEL), w_k.astype(BF16), w_v.astype(BF16))


def _attn_body(q_ref, k_ref, v_ref, o_ref, *, n_seq, tq):
    nt = (((1,), (1,)), ((), ()))
    for g in range(n_seq):
        rows = slice(g * tq, (g + 1) * tq)
        q = q_ref[rows, :].astype(BF16)
        s = lax.dot_general(q, k_ref[g].astype(BF16), nt, preferred_element_type=F32) * (MEM_DH ** -0.5)
        p = jnp.exp(s - jnp.max(s, axis=-1, keepdims=True))
        l = jnp.sum(p, axis=-1, keepdims=True)
        o = jnp.dot(p.astype(BF16), v_ref[g].astype(BF16), preferred_element_type=F32)
        o_ref[rows, :] = o / l


def _attention(q, mem_k, mem_v, *, row0, n_batch, t, n_seq, tq):
    q_tiles = t // tq
    rows = n_seq * tq
    assert t % tq == 0 and n_batch % n_seq == 0 and row0 % rows == 0 and (n_seq == 1 or q_tiles == 1)
    blk0 = row0 // rows
    kv_spec = pl.BlockSpec((n_seq, None, N_MEM, MEM_DH), lambda b, h, qi: (b, h, 0, 0))
    return pl.pallas_call(
        functools.partial(_attn_body, n_seq=n_seq, tq=tq),
        grid=(n_batch // n_seq, MEM_HEADS, q_tiles),
        in_specs=[pl.BlockSpec((rows, MEM_DH), lambda b, h, qi: (blk0 + b * q_tiles + qi, h)), kv_spec, kv_spec],
        out_specs=pl.BlockSpec((rows, MEM_DH), lambda b, h, qi: (b * q_tiles + qi, h)),
        out_shape=jax.ShapeDtypeStruct((n_batch * t, D_MODEL), F32),
        compiler_params=_params("parallel", "parallel", "parallel"),
    )(q, mem_k, mem_v)


ROW_TILE = (D_MODEL // LANES, LANES)


def _rows_to_tiles(x):
    chunks = [x[:, j * LANES:(j + 1) * LANES] for j in range(ROW_TILE[0])]
    return jnp.swapaxes(jnp.stack(chunks, axis=0), 0, 1)


def _tiles_to_rows(x):
    chunks = jnp.swapaxes(x, 0, 1)
    return jnp.concatenate([chunks[j] for j in range(ROW_TILE[0])], axis=1)


def _attn_out_body(*refs, part_counts):
    n = len(part_counts)
    w_ref, res_ref, o_ref = refs[n:]
    ab = _read_row_parts(refs[:n], part_counts).astype(BF16)
    n_chunk = 256
    for j in range(0, D_MODEL, n_chunk):
        o_ref[:, j:j + n_chunk] = res_ref[:, j:j + n_chunk] + jnp.dot(
            ab, w_ref[:, j:j + n_chunk], preferred_element_type=F32)


def _attn_out(att_parts, w_mo, residual):
    m = residual.shape[0]
    att_specs, part_counts = _row_part_specs(att_parts, TM)
    row = pl.BlockSpec((TM, D_MODEL), lambda i: (i, 0))
    return pl.pallas_call(
        functools.partial(_attn_out_body, part_counts=part_counts),
        grid=(m // TM,),
        in_specs=att_specs + [pl.BlockSpec((D_MODEL, D_MODEL), lambda i: (0, 0)), row],
        out_specs=row,
        out_shape=jax.ShapeDtypeStruct((m, D_MODEL), F32),
        compiler_params=_params("parallel"),
    )(*att_parts, w_mo.astype(BF16), residual)


def _router_body(h_ref, g_ref, w_ref, b_ref, hn_ref, ids_ref, comb_ref):
    x = h_ref[...]
    hn = x * lax.rsqrt(jnp.mean(x * x, axis=-1, keepdims=True) + EPS) * g_ref[...]
    hn_ref[...] = _rows_to_tiles(hn)
    logits = jnp.dot(hn, w_ref[...], precision=lax.Precision.HIGHEST, preferred_element_type=F32) + b_ref[...]
    lane = lax.broadcasted_iota(jnp.int32, logits.shape, 1).astype(F32)
    neg = -jnp.inf

    def first_argmax(vals):
        m = jnp.max(vals, axis=-1, keepdims=True)
        return m, jnp.min(jnp.where(vals == m, lane, float(LANES)), axis=-1, keepdims=True)

    gl = jnp.where(lane < N_GROUPS, logits, neg)
    gmax, gsel = first_argmax(gl)
    pg_sel = 1.0 / jnp.sum(jnp.exp(gl - gmax), axis=-1, keepdims=True)
    e0 = N_GROUPS + gsel * EXP_PER_GROUP
    el = jnp.where((lane >= e0) & (lane < e0 + EXP_PER_GROUP), logits, neg)
    m1, i1 = first_argmax(el)
    m2, i2 = first_argmax(jnp.where(lane == i1, neg, el))
    e21 = jnp.exp(m2 - m1)
    c1 = pg_sel / (1.0 + e21)
    c2 = c1 * e21
    ids = jnp.where(lane == 0, i1 - N_GROUPS, jnp.where(lane == 1, i2 - N_GROUPS, 0.0))
    ids_ref[...] = ids.astype(jnp.int32)
    comb_ref[...] = jnp.where(lane == 0, c1, jnp.where(lane == 1, c2, 0.0))


def _router(h, g_ffn, w_gr, b_gr, w_er, b_er):
    m = h.shape[0]
    pad = LANES - N_GROUPS - N_EXPERTS
    w = jnp.concatenate([w_gr, w_er, jnp.zeros((D_MODEL, pad), F32)], axis=1)
    b = jnp.concatenate([b_gr, b_er, jnp.zeros((pad,), F32)]).reshape(1, LANES)
    row = lambda n: pl.BlockSpec((TM, n), lambda i: (i, 0))
    return pl.pallas_call(
        _router_body,
        grid=(m // TM,),
        in_specs=[row(D_MODEL), pl.BlockSpec((1, D_MODEL), lambda i: (0, 0)),
                  pl.BlockSpec((D_MODEL, LANES), lambda i: (0, 0)), pl.BlockSpec((1, LANES), lambda i: (0, 0))],
        out_specs=[pl.BlockSpec((TM,) + ROW_TILE, lambda i: (i, 0, 0)), row(LANES), row(LANES)],
        out_shape=[jax.ShapeDtypeStruct((m,) + ROW_TILE, F32), jax.ShapeDtypeStruct((m, LANES), jnp.int32),
                   jax.ShapeDtypeStruct((m, LANES), F32)],
        compiler_params=_params("parallel"),
    )(h, g_ffn.reshape(1, D_MODEL), w, b)


def _dispatch_body(dest_ref, hn_ref, sorted_in, sorted_out, sem):
    del sorted_in

    def issue(r, c):
        for k in range(TOP_K):
            pltpu.make_async_copy(hn_ref.at[r], sorted_out.at[dest_ref[0, k, r]], sem.at[k]).start(priority=k)
        return c
    lax.fori_loop(0, TM, issue, 0, unroll=8)
    for k in range(TOP_K):
        pltpu.make_async_copy(hn_ref, sorted_out.at[pl.ds(0, TM)], sem.at[k]).wait()


def _dispatch(hn, dest, n_sorted):
    n_tok = hn.shape[0]
    return pl.pallas_call(
        _dispatch_body,
        grid=(n_tok // TM,),
        in_specs=[pl.BlockSpec((1, TOP_K, TM), lambda i: (i, 0, 0), memory_space=pltpu.SMEM),
                  pl.BlockSpec((TM,) + ROW_TILE, lambda i: (i, 0, 0)),
                  pl.BlockSpec(memory_space=pl.ANY)],
        out_specs=pl.BlockSpec(memory_space=pl.ANY),
        out_shape=jax.ShapeDtypeStruct((n_sorted,) + ROW_TILE, F32),
        scratch_shapes=[pltpu.SemaphoreType.DMA((TOP_K,))],
        input_output_aliases={2: 0},
        compiler_params=_params("arbitrary"),
    )(dest, hn, jnp.zeros((n_sorted,) + ROW_TILE, F32))


def _expert_body(blk_e_ref, x_ref, wg_ref, wu_ref, wd_ref, o_ref):
    del blk_e_ref
    x = _tiles_to_rows(x_ref[...]).astype(BF16)
    hg = jnp.dot(x, wg_ref[0].astype(BF16), preferred_element_type=F32)
    hu = jnp.dot(x, wu_ref[0].astype(BF16), preferred_element_type=F32)
    act = (hg * jax.nn.sigmoid(hg) * hu).astype(BF16)
    o_ref[...] = _rows_to_tiles(jnp.dot(act, wd_ref[0].astype(BF16), preferred_element_type=F32))


def _experts(x_sorted, blk_e, w_gate, w_up, w_down):
    n_blocks = blk_e.shape[0]
    row_spec = pl.BlockSpec((MOE_ROWS,) + ROW_TILE, lambda i, be: (i, 0, 0))
    grid_spec = pltpu.PrefetchScalarGridSpec(
        num_scalar_prefetch=1,
        grid=(n_blocks,),
        in_specs=[
            row_spec,
            pl.BlockSpec((1, D_MODEL, D_EXPERT), lambda i, be: (be[i], 0, 0)),
            pl.BlockSpec((1, D_MODEL, D_EXPERT), lambda i, be: (be[i], 0, 0)),
            pl.BlockSpec((1, D_EXPERT, D_MODEL), lambda i, be: (be[i], 0, 0)),
        ],
        out_specs=row_spec,
    )
    return pl.pallas_call(
        _expert_body,
        grid_spec=grid_spec,
        out_shape=jax.ShapeDtypeStruct(x_sorted.shape, F32),
        compiler_params=_params("arbitrary"),
    )(blk_e, x_sorted, w_gate, w_up, w_down)


def _route_plan(ids):
    n_tok = ids.shape[0]
    n_pairs = ids.size
    n_blocks = -(-(n_pairs + N_EXPERTS * (MOE_ROWS - 1)) // MOE_ROWS)
    flat_e = ids.reshape(n_pairs)
    onehot = (flat_e[:, None] == jnp.arange(N_EXPERTS, dtype=jnp.int32)[None, :]).astype(jnp.int32)
    csum = jnp.cumsum(onehot, axis=0)
    rank = jnp.sum(onehot * csum, axis=1) - 1
    counts = csum[-1]
    pcounts = (counts + MOE_ROWS - 1) // MOE_ROWS * MOE_ROWS
    pends = jnp.cumsum(pcounts)
    pstarts = pends - pcounts
    dest = jnp.sum(onehot * pstarts[None, :], axis=1) + rank
    block_start = jnp.arange(n_blocks, dtype=jnp.int32) * MOE_ROWS
    blk_e = jnp.minimum(jnp.sum((block_start[:, None] >= pends[None, :]).astype(jnp.int32), axis=1),
                        N_EXPERTS - 1).astype(jnp.int32)
    dest = dest.astype(jnp.int32).reshape(n_tok // TM, TM, TOP_K).transpose(0, 2, 1)
    return blk_e, dest, n_blocks * MOE_ROWS


def _final_body(dest_ref, dest_next_ref, h_ref, comb_ref, g_ref, y_hbm, o_first, o_second, ybuf, sem, *, n_first):
    i = pl.program_id(0)
    n = pl.num_programs(0)
    slot = i % 2

    def start_gather(ref, sl):
        def issue(r, c):
            for k in range(TOP_K):
                pltpu.make_async_copy(y_hbm.at[ref[0, k, r]], ybuf.at[sl, k, r], sem.at[sl]).start(priority=k)
            return c
        lax.fori_loop(0, TM, issue, 0, unroll=8)

    @pl.when(i == 0)
    def _():
        start_gather(dest_ref, 0)

    @pl.when(i + 1 < n)
    def _():
        start_gather(dest_next_ref, 1 - slot)

    for k in range(TOP_K):
        pltpu.make_async_copy(y_hbm.at[pl.ds(0, TM)], ybuf.at[slot, k], sem.at[slot]).wait()

    first, second = _tiles_to_rows(ybuf[slot, 0]), _tiles_to_rows(ybuf[slot, 1])
    x = h_ref[...] + (first * comb_ref[:, 0:1] + second * comb_ref[:, 1:2])
    out = x * lax.rsqrt(jnp.mean(x * x, axis=-1, keepdims=True) + EPS) * g_ref[...]

    @pl.when(i < n_first)
    def _():
        o_first[...] = out

    @pl.when(i >= n_first)
    def _():
        o_second[...] = out


def _final(h, y_sorted, dest, comb, g_final, *, n_first_rows):
    n_rows = h.shape[0]
    assert n_rows % TM == 0 and n_first_rows % TM == 0 and 0 < n_first_rows < n_rows
    n_steps, n_first = n_rows // TM, n_first_rows // TM
    row = lambda n: pl.BlockSpec((TM, n), lambda i: (i, 0))
    dest_spec = lambda f: pl.BlockSpec((1, TOP_K, TM), f, memory_space=pltpu.SMEM)
    return pl.pallas_call(
        functools.partial(_final_body, n_first=n_first),
        grid=(n_steps,),
        in_specs=[dest_spec(lambda i: (i, 0, 0)),
                  dest_spec(lambda i: (jnp.minimum(i + 1, n_steps - 1), 0, 0)),
                  row(D_MODEL), row(LANES), pl.BlockSpec((1, D_MODEL), lambda i: (0, 0)),
                  pl.BlockSpec(memory_space=pl.ANY)],
        out_specs=[pl.BlockSpec((TM, D_MODEL), lambda i: (jnp.minimum(i, n_first - 1), 0)),
                   pl.BlockSpec((TM, D_MODEL), lambda i: (jnp.maximum(i - n_first, 0), 0))],
        out_shape=[jax.ShapeDtypeStruct((n_first_rows, D_MODEL), F32),
                   jax.ShapeDtypeStruct((n_rows - n_first_rows, D_MODEL), F32)],
        scratch_shapes=[pltpu.VMEM((2, TOP_K, TM) + ROW_TILE, F32), pltpu.SemaphoreType.DMA((2,))],
        compiler_params=_params("arbitrary"),
    )(dest, dest, h, comb, g_final.reshape(1, D_MODEL), y_sorted)


def _reorder_last(x, shape, order):
    lead = x.shape[:-1]
    n = len(lead)
    y = x.reshape(lead + shape).transpose(tuple(range(n)) + tuple(n + o for o in order))
    return y.reshape(lead + (x.shape[-1],))


HALF_N = RWKV_N // 2


def _key_major(x):
    return _reorder_last(x, (RWKV_HEADS, RWKV_N), (1, 0))


def _key_major_inv(x):
    return _reorder_last(x, (RWKV_N, RWKV_HEADS), (1, 0))


def _value_major(x):
    return _reorder_last(x, (RWKV_HEADS, 2, HALF_N), (2, 1, 0))


def _value_major_inv(x):
    return _reorder_last(x, (HALF_N, 2, RWKV_HEADS), (2, 1, 0))


def _rwkv_cols(x, key_fn, value_fn):
    return jnp.concatenate([key_fn(x[..., :RWKV_W]), key_fn(x[..., RWKV_W:2 * RWKV_W]),
                            value_fn(x[..., 2 * RWKV_W:3 * RWKV_W]), x[..., 3 * RWKV_W:]], axis=-1)


def kernel(x_prompt, x_sample, mem_prompt, state_ret, state_rwkv, state_shift, cache_mem_k, cache_mem_v,
           g_mix, w_in, ret_gn, rwkv_mu, rwkv_w0, rwkv_w2, rwkv_a0, rwkv_a2, rwkv_g2, rwkv_k_k, rwkv_k_a,
           rwkv_r_k, rwkv_lnx_w, rwkv_lnx_b, w_out, g_mem_q, g_mem_kv, w_mq, w_mk, w_mv, w_mo, g_ffn,
           w_group_router, b_group_router, w_expert_router, b_expert_router, w_e_gate, w_e_up, w_e_down,
           g_final):
    assert w_in.shape[0] == 1, "single-layer decoder"
    bp, tp, d = x_prompt.shape
    bs, ts, _ = x_sample.shape
    np_tok, ns_tok = bp * tp, bs * ts
    assert d == D_MODEL and bp * RWKV_HEADS * 2 == LANES and bs == LANES
    l = 0
    x_parts = [x_prompt.reshape(np_tok, d), x_sample.reshape(ns_tok, d)]

    w_in_l = jnp.concatenate([w_in[l][:, :N_RET_COLS], _rwkv_cols(w_in[l][:, N_RET_COLS:], _key_major, _value_major)],
                             axis=1)
    proj = _input_projection(x_parts, g_mix[l], w_in_l)

    pos_p = np.arange(tp)
    pos_s = PAST_LEN + np.arange(ts)
    zero_ret = jnp.zeros((bp, RET_HEADS, RET_DK, RET_DV), F32)
    oret_p, sret_p = _retention(proj, zero_ret, ret_gn[l], pos_p, row0=0, n_batch=bp, t=tp, n_blk=2, per_blk=1)
    oret_s, sret_s = _retention(proj, state_ret[l], ret_gn[l], pos_s, row0=np_tok, n_batch=bs, t=ts,
                                n_blk=1, per_blk=16)
    oret_p, oret_s = oret_p.reshape(np_tok, RET_W), oret_s.reshape(ns_tok, RET_W)

    pre_w = (_rwkv_cols(rwkv_mu[l], _key_major, _value_major), _key_major(rwkv_w0[l]), _key_major(rwkv_w2[l]),
             _key_major(rwkv_a0[l]), _key_major(rwkv_a2[l]), _value_major(rwkv_g2[l]))
    zero_shift = jnp.zeros((bp, N_RWKV_COLS), F32)
    shift_in = _rwkv_cols(state_shift[l], _key_major, _value_major)
    r_p, k_p, v_p, w_p, a_p, gate_p, shift_p = _rwkv_pre(proj, zero_shift, *pre_w, row0=0, n_batch=bp, t=tp, c=512)
    r_s, k_s, v_s, w_s, a_s, gate_s, shift_s = _rwkv_pre_short(proj, shift_in, *pre_w, row0=np_tok, n_batch=bs,
                                                                t=ts)

    kvec = lambda v: v.reshape(RWKV_HEADS, RWKV_N)
    key_par = lambda v: jnp.broadcast_to(kvec(v).T[:, None, None, :], (RWKV_N, 2, bp, RWKV_HEADS)).reshape(
        RWKV_N, LANES)

    val_par = lambda v: jnp.broadcast_to(
        v.reshape(RWKV_HEADS, 2, HALF_N).transpose(2, 1, 0)[:, :, None, :],
        (HALF_N, 2, bp, RWKV_HEADS)).reshape(HALF_N, LANES)
    mem_shape = (N_MEM, MEM_HEADS, MEM_DH)
    y_p, srw_p, cache_k_heads, cache_v_heads = _rwkv_scan_prompt(
        r_p, k_p, w_p, a_p, v_p, key_par(rwkv_k_k[l]), key_par(rwkv_k_a[l]), key_par(rwkv_r_k[l]),
        val_par(rwkv_lnx_w[l]), val_par(rwkv_lnx_b[l]), jnp.zeros((HALF_N, RWKV_N, LANES), F32),
        cache_mem_k.reshape(bs, *mem_shape), cache_mem_v.reshape(bs, *mem_shape))
    srw_p = srw_p.reshape(HALF_N, RWKV_N, 2, bp, RWKV_HEADS).transpose(3, 4, 2, 0, 1).reshape(
        bp, RWKV_HEADS, RWKV_N, RWKV_N)
    head_par = lambda v: jnp.broadcast_to(kvec(v)[:, :, None], (RWKV_HEADS, RWKV_N, LANES))
    val_rows = lambda v: v.reshape(RWKV_HEADS, 2, HALF_N).transpose(0, 2, 1).reshape(RWKV_HEADS * RWKV_N)
    state_s = state_rwkv[l].astype(F32).reshape(bs, RWKV_HEADS, 2, HALF_N, RWKV_N).transpose(1, 3, 2, 4, 0)
    y_s, srw_s = _rwkv_scan(
        r_s, k_s, w_s, a_s, v_s, head_par(rwkv_k_k[l]), head_par(rwkv_k_a[l]), head_par(rwkv_r_k[l]),
        head_par(val_rows(rwkv_lnx_w[l])), head_par(val_rows(rwkv_lnx_b[l])),
        state_s.reshape(RWKV_HEADS, RWKV_N, RWKV_N, bs), tc=ts, halves=1)
    y_s = y_s.transpose(3, 1, 2, 0).reshape(ns_tok, RWKV_W)
    srw_s = srw_s.reshape(RWKV_HEADS, HALF_N, 2, RWKV_N, bs).transpose(4, 0, 2, 1, 3).reshape(
        bs, RWKV_HEADS, RWKV_N, RWKV_N)

    w_rwkv_out = _value_major(w_out[l][RET_W:].T).T
    h, q = _merge(x_parts, [oret_p, oret_s], [gate_p, gate_s], y_p, y_s, w_out[l][:RET_W], w_rwkv_out,
                  g_mem_q[l], w_mq[l])

    mk, mv, mk_heads, mv_heads = _mem_kv(mem_prompt, g_mem_kv[l], w_mk[l], w_mv[l])
    att_p = _attention(q, mk_heads, mv_heads, row0=0, n_batch=bp, t=tp, n_seq=1, tq=tp)
    att_s = _attention(q, cache_k_heads, cache_v_heads, row0=np_tok, n_batch=bs, t=ts, n_seq=16, tq=ts)

    h = _attn_out([att_p, att_s], w_mo[l], h)

    hn, ids, comb = _router(h, g_ffn[l], w_group_router[l], b_group_router[l], w_expert_router[l],
                            b_expert_router[l])
    blk_e, dest, n_sorted = _route_plan(ids[:, :TOP_K])
    y_sorted = _experts(_dispatch(hn, dest, n_sorted), blk_e, w_e_gate[l], w_e_up[l], w_e_down[l])
    y_prompt, y_sample = _final(h, y_sorted, dest, comb, g_final, n_first_rows=np_tok)
    y_prompt = y_prompt.reshape(bp, tp, d)
    y_sample = y_sample.reshape(bs, ts, d)

    shift_p = _rwkv_cols(shift_p.reshape(bp, N_RWKV_COLS), _key_major_inv, _value_major_inv)
    shift_s = _rwkv_cols(shift_s.reshape(bs, N_RWKV_COLS), _key_major_inv, _value_major_inv)
    return (y_prompt, y_sample, sret_p[None], srw_p[None], shift_p[None],
            mk.reshape(1, bp, *mem_shape), mv.reshape(1, bp, *mem_shape),
            sret_s[None], srw_s[None], shift_s[None])
```

```python
import functools

import numpy as np
import jax
import jax.numpy as jnp
from jax import lax
from jax.experimental import pallas as pl
from jax.experimental.pallas import tpu as pltpu

F32 = jnp.float32
BF16 = jnp.bfloat16

D_MODEL = 1024
PAST_LEN = 16384
N_MEM = 256
MEM_HEADS = 4
MEM_DH = D_MODEL // MEM_HEADS
RET_HEADS = 4
RET_W = D_MODEL // 2
RET_DV = RET_W // RET_HEADS
RET_DK = RET_DV // 2
RET_QK = RET_HEADS * RET_DK
RET_CHUNK = 128
ROPE_BASE = 10000.0
RWKV_N = 64
RWKV_W = D_MODEL - RET_W
RWKV_HEADS = RWKV_W // RWKV_N
LORA_W = 64
LORA_A = 64
LORA_G = 128
LNX_EPS = 64e-5
N_RET_COLS = 2 * RET_QK + 2 * RET_W
N_RWKV_COLS = 3 * RWKV_W + LORA_W + LORA_A + LORA_G
N_IN_COLS = N_RET_COLS + N_RWKV_COLS
N_GROUPS = 4
EXP_PER_GROUP = 8
N_EXPERTS = N_GROUPS * EXP_PER_GROUP
TOP_K = 2
D_EXPERT = D_MODEL // 2
EPS = 1e-6

LANES = 128
MOE_ROWS = 512
TM = 512


def _params(*sem):
    return pltpu.CompilerParams(dimension_semantics=sem)


def _row_part_specs(parts, tm):
    specs, counts, start = [], [], 0
    for part in parts:
        nb = part.shape[0] // tm
        assert nb * tm == part.shape[0]
        specs.append(pl.BlockSpec((tm, part.shape[1]), lambda i, s=start, n=nb: (jnp.clip(i - s, 0, n - 1), 0)))
        counts.append(nb)
        start += nb
    return specs, counts


def _read_row_parts(refs, counts):
    i = pl.program_id(0)
    x = refs[0][...]
    start = counts[0]
    for ref, nb in zip(refs[1:], counts[1:]):
        x = jnp.where(i >= start, ref[...], x)
        start += nb
    return x


def _proj_body(*refs, part_counts):
    n = len(part_counts)
    g_ref, w_ref, o_ref = refs[n:]
    x = _read_row_parts(refs[:n], part_counts).astype(F32)
    xb = (x * lax.rsqrt(jnp.mean(x * x, axis=-1, keepdims=True) + EPS) * g_ref[...]).astype(BF16)
    n_chunk = 256
    for j in range(0, w_ref.shape[1], n_chunk):
        o_ref[:, j:j + n_chunk] = jnp.dot(xb, w_ref[:, j:j + n_chunk], preferred_element_type=F32)


def _input_projection(x_parts, gain, w):
    k, n_out = w.shape
    m = sum(part.shape[0] for part in x_parts)
    assert n_out % 256 == 0
    in_specs, part_counts = _row_part_specs(x_parts, TM)
    return pl.pallas_call(
        functools.partial(_proj_body, part_counts=part_counts),
        grid=(m // TM,),
        in_specs=in_specs + [pl.BlockSpec((1, k), lambda i: (0, 0)), pl.BlockSpec((k, n_out), lambda i: (0, 0))],
        out_specs=pl.BlockSpec((TM, n_out), lambda i: (i, 0)),
        out_shape=jax.ShapeDtypeStruct((m, n_out), F32),
        compiler_params=_params("parallel"),
    )(*x_parts, gain.reshape(1, k).astype(F32), w.astype(BF16))


def _rot_tables(pos):
    half = RET_DK // 2
    inv_freq = ROPE_BASE ** (-(np.arange(half, dtype=np.float64) / half))
    ang = pos.astype(np.float64)[:, None] * inv_freq[None, :]
    cos, sin = np.cos(ang), np.sin(ang)
    zero = np.zeros_like(sin)
    c = np.tile(np.concatenate([cos, cos], axis=1), (1, RET_HEADS))
    s_lo = np.tile(np.concatenate([-sin, zero], axis=1), (1, RET_HEADS))
    s_hi = np.tile(np.concatenate([zero, sin], axis=1), (1, RET_HEADS))
    return [jnp.asarray(t, F32) for t in (c, s_lo, s_hi)]


def _ret_decay_tables(c):
    lg = np.log1p(-np.exp2(-5.0 - np.arange(RET_HEADS, dtype=np.float64)))
    idx = np.arange(c, dtype=np.float64)
    diff = idx[:, None] - idx[None, :]
    mask = np.where(diff[None] >= 0, np.exp(np.maximum(diff, 0.0)[None] * lg[:, None, None]), 0.0)
    q_dec = np.repeat(np.exp((idx[:, None] + 1.0) * lg[None, :]), RET_DV, axis=1)
    k_dec = np.repeat(np.exp((c - 1.0 - idx)[:, None] * lg[None, :]), RET_DK, axis=1)
    c_dec = [float(v) for v in np.exp(c * lg)]
    return jnp.asarray(mask, F32), jnp.asarray(q_dec, F32), jnp.asarray(k_dec, F32), c_dec


def _ret_body(*refs, n_blk, per_blk, c, c_dec):
    q_refs, k_refs, v_refs, gate_refs = (refs[j * n_blk:(j + 1) * n_blk] for j in range(4))
    (c_ref, slo_ref, shi_ref, mask_ref, qdec_ref, kdec_ref, gn_ref, s0_ref, o_ref, sout_ref,
     s_scr) = refs[4 * n_blk:]
    n_seq = n_blk * per_blk
    ci = pl.program_id(1)

    @pl.when(ci == 0)
    def _():
        s_scr[...] = s0_ref[...].astype(F32)

    cos, s_lo, s_hi = c_ref[...], slo_ref[...], shi_ref[...]
    half = RET_DK // 2

    def rope(x):
        return x * cos + pltpu.roll(x, RET_QK - half, 1) * s_lo + pltpu.roll(x, half, 1) * s_hi

    nt = (((1,), (1,)), ((), ()))
    tn = (((0,), (0,)), ((), ()))
    for g in range(n_seq):
        blk = g // per_blk
        rows = slice((g % per_blk) * c, (g % per_blk + 1) * c)
        q = rope(q_refs[blk][rows, :].astype(F32))
        k = rope(k_refs[blk][rows, :].astype(F32)) * (RET_DK ** -0.5)
        k_st = k * kdec_ref[...]
        for h in range(RET_HEADS):
            kc = slice(h * RET_DK, (h + 1) * RET_DK)
            vc = slice(h * RET_DV, (h + 1) * RET_DV)
            qh = q[:, kc].astype(BF16)
            vh = v_refs[blk][rows, vc].astype(BF16)
            s_h = s_scr[g, h]
            att = lax.dot_general(qh, k[:, kc].astype(BF16), nt, preferred_element_type=F32) * mask_ref[h]
            o = jnp.dot(att.astype(BF16), vh, preferred_element_type=F32)
            o = o + jnp.dot(qh, s_h.astype(BF16), preferred_element_type=F32) * qdec_ref[:, vc]
            s_scr[g, h] = s_h * c_dec[h] + lax.dot_general(
                k_st[:, kc].astype(BF16), vh, tn, preferred_element_type=F32)
            o = o * lax.rsqrt(jnp.mean(o * o, axis=-1, keepdims=True) + EPS)
            gate = gate_refs[blk][rows, vc].astype(F32)
            o_ref[g, :, vc] = o * gn_ref[:, vc] * (gate * jax.nn.sigmoid(gate))

    @pl.when(ci == pl.num_programs(1) - 1)
    def _():
        sout_ref[...] = s_scr[...]


def _retention(proj, s0, ret_gn, pos, *, row0, n_batch, t, n_blk, per_blk):
    c = RET_CHUNK if t % RET_CHUNK == 0 else t
    n_chunks = t // c
    rows = per_blk * c
    n_seq = n_blk * per_blk
    assert n_batch % n_seq == 0 and row0 % rows == 0 and (per_blk == 1 or n_chunks == 1)
    blk0 = row0 // rows
    mask, q_dec, k_dec, c_dec = _ret_decay_tables(c)
    cos, s_lo, s_hi = _rot_tables(pos)

    def const2(b, ci):
        return (0, 0)

    def row_specs(width, col):
        return [pl.BlockSpec((rows, width), lambda b, ci, j=j: (blk0 + (b * n_blk + j) * n_chunks + ci, col))
                for j in range(n_blk)]

    state_spec = pl.BlockSpec((n_seq, RET_HEADS, RET_DK, RET_DV), lambda b, ci: (b, 0, 0, 0))
    in_specs = row_specs(RET_QK, 0) + row_specs(RET_QK, 1) + row_specs(RET_W, 1) + row_specs(RET_W, 2) + [
        pl.BlockSpec((c, RET_QK), lambda b, ci: (ci, 0)),
        pl.BlockSpec((c, RET_QK), lambda b, ci: (ci, 0)),
        pl.BlockSpec((c, RET_QK), lambda b, ci: (ci, 0)),
        pl.BlockSpec((RET_HEADS, c, c), lambda b, ci: (0, 0, 0)),
        pl.BlockSpec((c, RET_W), const2),
        pl.BlockSpec((c, RET_QK), const2),
        pl.BlockSpec((1, RET_W), const2),
        state_spec,
    ]
    return pl.pallas_call(
        functools.partial(_ret_body, n_blk=n_blk, per_blk=per_blk, c=c, c_dec=c_dec),
        grid=(n_batch // n_seq, n_chunks),
        in_specs=in_specs,
        out_specs=[pl.BlockSpec((n_seq, c, RET_W), lambda b, ci: (b, ci, 0)), state_spec],
        out_shape=[jax.ShapeDtypeStruct((n_batch, t, RET_W), F32),
                   jax.ShapeDtypeStruct((n_batch, RET_HEADS, RET_DK, RET_DV), F32)],
        scratch_shapes=[pltpu.VMEM((n_seq, RET_HEADS, RET_DK, RET_DV), F32)],
        compiler_params=_params("parallel", "arbitrary"),
    )(*([proj] * (4 * n_blk)), cos, s_lo, s_hi, mask, q_dec, k_dec, ret_gn.reshape(1, RET_W).astype(F32), s0)


LORA_COLS = LORA_W + LORA_A + LORA_G


def _rwkv_lora_terms(lo, w0_ref, w2_ref, a0_ref, a2_ref, g2_ref):
    hw = lo[:, :LORA_W]
    ha = lo[:, LORA_W:LORA_W + LORA_A]
    hg = lo[:, LORA_W + LORA_A:]
    u = w0_ref[...] + jnp.dot(jnp.tanh(hw).astype(BF16), w2_ref[...], preferred_element_type=F32)
    decay = jnp.exp(-float(np.exp(-0.5)) * jax.nn.sigmoid(u))
    rate = jax.nn.sigmoid(a0_ref[...] + jnp.dot(ha.astype(BF16), a2_ref[...], preferred_element_type=F32))
    gate = jnp.dot(jax.nn.sigmoid(hg).astype(BF16), g2_ref[...], preferred_element_type=F32)
    return decay, rate, gate


def _rwkv_pre_body(r_ref, k_ref, v_ref, lo_ref, shift_ref, mu_ref, w0_ref, w2_ref, a0_ref, a2_ref, g2_ref,
                   ro_ref, ko_ref, vo_ref, wo_ref, ao_ref, go_ref, so_ref, prev_scr):
    ci = pl.program_id(1)
    c = r_ref.shape[0]

    @pl.when(ci == 0)
    def _():
        prev_scr[...] = shift_ref[0].astype(F32)

    first_row = lax.broadcasted_iota(jnp.int32, (c, 1), 0) == 0

    def shifted(x_ref, col0):
        w = x_ref.shape[1]
        x = x_ref[...].astype(F32)
        prev = jnp.where(first_row, prev_scr[:, col0:col0 + w], pltpu.roll(x, 1, 0))
        prev_scr[:, col0:col0 + w] = x[c - 1:c, :]
        return x + (prev - x) * mu_ref[:, col0:col0 + w]

    ro_ref[...] = shifted(r_ref, 0).T
    ko_ref[...] = shifted(k_ref, RWKV_W).T
    vo_ref[...] = shifted(v_ref, 2 * RWKV_W).T
    decay, rate, gate = _rwkv_lora_terms(shifted(lo_ref, 3 * RWKV_W), w0_ref, w2_ref, a0_ref, a2_ref, g2_ref)
    wo_ref[...] = decay.T
    ao_ref[...] = rate.T
    go_ref[...] = gate

    @pl.when(ci == pl.num_programs(1) - 1)
    def _():
        so_ref[0] = prev_scr[...]


def _rwkv_pre_args(s_shift, n_batch, mu, w0, w2, a0, a2, g2):
    return (s_shift.reshape(n_batch, 1, N_RWKV_COLS), mu.reshape(1, -1), w0.reshape(1, -1), w2.astype(BF16),
            a0.reshape(1, -1), a2.astype(BF16), g2.astype(BF16))


def _rwkv_pre_weight_specs(const):
    return [pl.BlockSpec((1, N_RWKV_COLS), const), pl.BlockSpec((1, RWKV_W), const),
            pl.BlockSpec((LORA_W, RWKV_W), const), pl.BlockSpec((1, RWKV_W), const),
            pl.BlockSpec((LORA_A, RWKV_W), const), pl.BlockSpec((LORA_G, RWKV_W), const)]


def _rwkv_pre(proj, s_shift, mu, w0, w2, a0, a2, g2, *, row0, n_batch, t, c):
    n_chunks = t // c
    assert t % c == 0 and row0 % c == 0
    blk0 = row0 // c
    col_r = N_RET_COLS // RWKV_W
    col_lo = (N_RET_COLS + 3 * RWKV_W) // LORA_COLS
    assert col_r * RWKV_W == N_RET_COLS and col_lo * LORA_COLS == N_RET_COLS + 3 * RWKV_W

    def row_map(col):
        return lambda b, ci: (blk0 + b * n_chunks + ci, col)

    state_spec = pl.BlockSpec((1, 1, N_RWKV_COLS), lambda b, ci: (b, 0, 0))
    in_specs = [pl.BlockSpec((c, RWKV_W), row_map(col_r)), pl.BlockSpec((c, RWKV_W), row_map(col_r + 1)),
                pl.BlockSpec((c, RWKV_W), row_map(col_r + 2)), pl.BlockSpec((c, LORA_COLS), row_map(col_lo)),
                state_spec] + _rwkv_pre_weight_specs(lambda b, ci: (0, 0))
    vec_spec = pl.BlockSpec((None, RWKV_W, c), lambda b, ci: (b, 0, ci))
    vec_shape = jax.ShapeDtypeStruct((n_batch, RWKV_W, t), F32)
    return pl.pallas_call(
        _rwkv_pre_body,
        grid=(n_batch, n_chunks),
        in_specs=in_specs,
        out_specs=[vec_spec] * 5 + [pl.BlockSpec((c, RWKV_W), lambda b, ci: (b * n_chunks + ci, 0)), state_spec],
        out_shape=[vec_shape] * 5 + [jax.ShapeDtypeStruct((n_batch * t, RWKV_W), F32),
                                     jax.ShapeDtypeStruct((n_batch, 1, N_RWKV_COLS), F32)],
        scratch_shapes=[pltpu.VMEM((1, N_RWKV_COLS), F32)],
        compiler_params=_params("parallel", "arbitrary"),
    )(proj, proj, proj, proj, *_rwkv_pre_args(s_shift, n_batch, mu, w0, w2, a0, a2, g2))


def _rwkv_pre_short_body(r_ref, k_ref, v_ref, lo_ref, shift_ref, mu_ref, w0_ref, w2_ref, a0_ref, a2_ref, g2_ref,
                         ro_ref, ko_ref, vo_ref, wo_ref, ao_ref, go_ref, so_ref, *, n_b, t):
    rows = n_b * t
    first_tok = (lax.broadcasted_iota(jnp.int32, (rows, 1), 0) & (t - 1)) == 0

    def shifted(x_ref, col0):
        w = x_ref.shape[1]
        x = x_ref[...].astype(F32)
        carried = jnp.broadcast_to(shift_ref[:, :, col0:col0 + w].astype(F32), (n_b, t, w)).reshape(rows, w)
        prev = jnp.where(first_tok, carried, pltpu.roll(x, 1, 0))
        so_ref[:, :, col0:col0 + w] = x.reshape(n_b, t, w)[:, t - 1:t, :]
        return x + (prev - x) * mu_ref[:, col0:col0 + w]

    def put(o_ref, x):
        by_tok = jnp.swapaxes(x.reshape(n_b, t, RWKV_W), 0, 1)
        for ti in range(t):
            feat = by_tok[ti].T.reshape(RWKV_N, RWKV_HEADS, n_b)
            o_ref[:, ti] = jnp.swapaxes(feat, 0, 1)

    put(ro_ref, shifted(r_ref, 0))
    put(ko_ref, shifted(k_ref, RWKV_W))
    put(vo_ref, shifted(v_ref, 2 * RWKV_W))
    decay, rate, gate = _rwkv_lora_terms(shifted(lo_ref, 3 * RWKV_W), w0_ref, w2_ref, a0_ref, a2_ref, g2_ref)
    put(wo_ref, decay)
    put(ao_ref, rate)
    go_ref[...] = gate


def _rwkv_pre_short(proj, s_shift, mu, w0, w2, a0, a2, g2, *, row0, n_batch, t):
    rows = n_batch * t
    assert row0 % rows == 0 and t & (t - 1) == 0
    blk0 = row0 // rows
    col_r = N_RET_COLS // RWKV_W
    col_lo = (N_RET_COLS + 3 * RWKV_W) // LORA_COLS
    state_spec = pl.BlockSpec((n_batch, 1, N_RWKV_COLS), lambda i: (0, 0, 0))
    in_specs = [pl.BlockSpec((rows, RWKV_W), lambda i: (blk0, col_r)),
                pl.BlockSpec((rows, RWKV_W), lambda i: (blk0, col_r + 1)),
                pl.BlockSpec((rows, RWKV_W), lambda i: (blk0, col_r + 2)),
                pl.BlockSpec((rows, LORA_COLS), lambda i: (blk0, col_lo)),
                state_spec] + _rwkv_pre_weight_specs(lambda i: (0, 0))
    vec_shape = (RWKV_HEADS, t, RWKV_N, n_batch)
    vec_spec = pl.BlockSpec(vec_shape, lambda i: (0, 0, 0, 0))
    return pl.pallas_call(
        functools.partial(_rwkv_pre_short_body, n_b=n_batch, t=t),
        grid=(1,),
        in_specs=in_specs,
        out_specs=[vec_spec] * 5 + [pl.BlockSpec((rows, RWKV_W), lambda i: (0, 0)), state_spec],
        out_shape=[jax.ShapeDtypeStruct(vec_shape, F32)] * 5 + [
            jax.ShapeDtypeStruct((rows, RWKV_W), F32), jax.ShapeDtypeStruct((n_batch, 1, N_RWKV_COLS), F32)],
        compiler_params=_params("arbitrary"),
    )(proj, proj, proj, proj, *_rwkv_pre_args(s_shift, n_batch, mu, w0, w2, a0, a2, g2))


def _scan_body(r_ref, k_ref, w_ref, a_ref, v_ref, kk_ref, ka_ref, rk_ref, lw_ref, lb_ref, s0_ref,
               y_ref, sout_ref, s_scr, a_scr, b_scr, km_scr, *, tc, vr, halves):
    ci = pl.program_id(1)

    @pl.when(ci == 0)
    def _():
        s_scr[...] = s0_ref[...].astype(F32)

    def ksum(x):
        return jnp.sum(x, axis=-2, keepdims=True)

    def vsum(x):
        if halves == 2:
            x2 = x.reshape(tc * vr, LANES)
            x = (x2 + pltpu.roll(x2, LANES // 2, 1)).reshape(tc, vr, LANES)
        return jnp.sum(x, axis=1, keepdims=True)

    kr = k_ref[...]
    a = a_ref[...]
    kk = kr * kk_ref[...]
    kk = kk / jnp.maximum(jnp.sqrt(ksum(kk * kk)), 1e-12)
    a_scr[...] = -kk
    b_scr[...] = kk * a
    km_scr[...] = kr * (1.0 + (a - 1.0) * ka_ref[...])

    def token(t, carry):
        r, w, avec, bvec, kmod = r_ref[t], w_ref[t], a_scr[t], b_scr[t], km_scr[t]

        def value_row(i, c2):
            s = s_scr[i]
            sa = ksum(s * avec)
            s = s * w + sa * bvec + v_ref[t, pl.ds(i, 1), :] * kmod
            s_scr[i] = s
            y_ref[t, pl.ds(i, 1), :] = ksum(s * r)
            return c2

        lax.fori_loop(0, vr, value_row, 0, unroll=16)
        return carry

    lax.fori_loop(0, tc, token, 0)

    y = y_ref[...]
    d = y - vsum(y) * (1.0 / RWKV_N)
    var = vsum(d * d) * (1.0 / RWKV_N)
    bonus = ksum(r_ref[...] * km_scr[...] * rk_ref[...])
    y_ref[...] = d * lax.rsqrt(var + LNX_EPS) * lw_ref[...] + lb_ref[...] + bonus * v_ref[...]

    @pl.when(ci == pl.num_programs(1) - 1)
    def _():
        sout_ref[...] = s_scr[...]


def _rwkv_scan(r, k, w, a, v, k_k, k_a, r_k, lnx_w, lnx_b, s0, *, tc, halves):
    n_grp, t, _, lanes = r.shape
    vr = v.shape[2]
    assert lanes == LANES and t % tc == 0 and vr * halves == RWKV_N

    def tok_spec(rows):
        return pl.BlockSpec((None, tc, rows, LANES), lambda g, ci: (g, ci, 0, 0))

    def par_spec(rows):
        return pl.BlockSpec((None, rows, LANES), lambda g, ci: (g, 0, 0))

    st_spec = pl.BlockSpec((None, vr, RWKV_N, LANES), lambda g, ci: (g, 0, 0, 0))
    key_scratch = pltpu.VMEM((tc, RWKV_N, LANES), F32)
    return pl.pallas_call(
        functools.partial(_scan_body, tc=tc, vr=vr, halves=halves),
        grid=(n_grp, t // tc),
        in_specs=[tok_spec(RWKV_N)] * 4 + [tok_spec(vr)] + [par_spec(RWKV_N)] * 3 + [par_spec(vr)] * 2 + [st_spec],
        out_specs=[tok_spec(vr), st_spec],
        out_shape=[jax.ShapeDtypeStruct((n_grp, t, vr, LANES), F32),
                   jax.ShapeDtypeStruct((n_grp, vr, RWKV_N, LANES), F32)],
        scratch_shapes=[pltpu.VMEM((vr, RWKV_N, LANES), F32), key_scratch, key_scratch, key_scratch],
        compiler_params=_params("parallel", "arbitrary"),
    )(r, k, w, a, v, k_k, k_a, r_k, lnx_w, lnx_b, s0)


SCAN_TC = 128
SCAN_SUB = 64
MEM_CHUNKS = 4


def _scan_prompt_body(r_ref, k_ref, w_ref, a_ref, v_ref, kk_ref, ka_ref, rk_ref, lw_ref, lb_ref, s0_ref,
                      memk_hbm, memv_hbm, y_ref, sout_ref, memk_out, memv_out,
                      s_scr, r_c, w_c, a_c, b_c, km_c, v_c, y_c, stash, mem_stage, mem_in_sem, mem_out_sem,
                      *, n_b, mem_seqs):
    ci = pl.program_id(0)
    vr = RWKV_N // 2
    ts = SCAN_SUB
    tile = RWKV_HEADS
    half_lanes = LANES // 2

    @pl.when(ci == 0)
    def _():
        s_scr[...] = s0_ref[...].astype(F32)

    chunk_seqs = mem_seqs // MEM_CHUNKS

    def mem_in(chunk, slot):
        copies = []
        for j in range(chunk_seqs):
            seq = (ci * MEM_CHUNKS + chunk) * chunk_seqs + j
            for h in range(MEM_HEADS):
                copies.append(pltpu.make_async_copy(memk_hbm.at[seq, :, h, :], mem_stage.at[slot, 0, j, h],
                                                    mem_in_sem.at[slot]))
                copies.append(pltpu.make_async_copy(memv_hbm.at[seq, :, h, :], mem_stage.at[slot, 1, j, h],
                                                    mem_in_sem.at[slot]))
        return copies

    def mem_out(chunk, slot):
        seqs = pl.ds((ci * MEM_CHUNKS + chunk) * chunk_seqs, chunk_seqs)
        return [pltpu.make_async_copy(mem_stage.at[slot, 0], memk_out.at[seqs], mem_out_sem.at[slot]),
                pltpu.make_async_copy(mem_stage.at[slot, 1], memv_out.at[seqs], mem_out_sem.at[slot])]

    def mem_phase(p):
        if 1 <= p <= MEM_CHUNKS:
            for cp in mem_in(p - 1, (p - 1) % 2):
                cp.wait()
            for cp in mem_out(p - 1, (p - 1) % 2):
                cp.start()
        if 2 <= p <= MEM_CHUNKS + 1:
            for cp in mem_out(p - 2, p % 2):
                cp.wait()
        if p < MEM_CHUNKS:
            for cp in mem_in(p, p % 2):
                cp.start()

    mem_phase(0)

    low = lax.broadcasted_iota(jnp.int32, (ts, LANES), 1) < half_lanes

    def feature_pair_rows(x_ref, base):
        tiles = [x_ref[b, pl.ds(base + f * tile, tile), :] for f in range(2) for b in range(n_b)]
        return jnp.concatenate(tiles, axis=0).T

    def key_to_chain(x_ref, dst, stash, t0):
        def group(g, c):
            rows = []
            for j in range(4):
                pair = g * 4 + j
                if t0 == 0:
                    full = feature_pair_rows(x_ref, pl.multiple_of(pair * 2 * tile, 2 * tile))
                    mt = full[:ts]
                    stash[pair] = full[ts:]
                else:
                    mt = stash[pair]
                sw = pltpu.roll(mt, half_lanes, 1)
                rows += [jnp.where(low, mt, sw), jnp.where(low, sw, mt)]
            dst[:, pl.ds(pl.multiple_of(g * 8, 8), 8), :] = jnp.swapaxes(jnp.stack(rows, axis=0), 0, 1)
            return c
        lax.fori_loop(0, RWKV_N // 8, group, 0, unroll=4)

    def value_to_chain(g, c):
        rows = [feature_pair_rows(v_ref, pl.multiple_of((g * 8 + j) * 2 * tile, 2 * tile)) for j in range(8)]
        v_c[:, pl.ds(pl.multiple_of(g * 8, 8), 8), :] = jnp.swapaxes(jnp.stack(rows, axis=0), 0, 1)
        return c
    lax.fori_loop(0, vr // 8, value_to_chain, 0, unroll=2)

    def ksum(x):
        return jnp.sum(x, axis=-2, keepdims=True)

    for t0 in range(0, SCAN_TC, ts):
        key_to_chain(r_ref, r_c, stash.at[0], t0)
        key_to_chain(w_ref, w_c, stash.at[1], t0)
        key_to_chain(k_ref, km_c, stash.at[2], t0)
        key_to_chain(a_ref, b_c, stash.at[3], t0)

        def prep(g8, c):
            toks = pl.ds(pl.multiple_of(g8 * 8, 8), 8)
            kr = km_c[toks]
            a = b_c[toks]
            kk = kr * kk_ref[...]
            kk = kk / jnp.maximum(jnp.sqrt(ksum(kk * kk)), 1e-12)
            a_c[toks] = -kk
            b_c[toks] = kk * a
            km_c[toks] = kr * (1.0 + (a - 1.0) * ka_ref[...])
            return c
        lax.fori_loop(0, ts // 8, prep, 0)
        mem_phase(1 + 2 * (t0 // ts))

        def token(t, carry):
            r, w, avec, bvec, kmod = r_c[t], w_c[t], a_c[t], b_c[t], km_c[t]

            def value_row(i, c2):
                s = s_scr[i]
                sa = ksum(s * avec)
                s = s * w + sa * bvec + v_c[t0 + t, pl.ds(i, 1), :] * kmod
                s_scr[i] = s
                y_c[t0 + t, pl.ds(i, 1), :] = ksum(s * r)
                return c2

            lax.fori_loop(0, vr, value_row, 0, unroll=True)
            return carry

        lax.fori_loop(0, ts, token, 0)
        mem_phase(2 + 2 * (t0 // ts))

        def post(g8, c):
            ktoks = pl.ds(pl.multiple_of(g8 * 8, 8), 8)
            vtoks = pl.ds(pl.multiple_of(t0 + g8 * 8, 8), 8)

            def vsum(x):
                x2 = x.reshape(8 * vr, LANES)
                x2 = x2 + pltpu.roll(x2, half_lanes, 1)
                return jnp.sum(x2.reshape(8, vr, LANES), axis=1, keepdims=True)

            y = y_c[vtoks]
            d = y - vsum(y) * (1.0 / RWKV_N)
            var = vsum(d * d) * (1.0 / RWKV_N)
            bonus = ksum(r_c[ktoks] * km_c[ktoks] * rk_ref[...])
            y_c[vtoks] = d * lax.rsqrt(var + LNX_EPS) * lw_ref[...] + lb_ref[...] + bonus * v_c[vtoks]
            return c
        lax.fori_loop(0, ts // 8, post, 0, unroll=4)

    def value_from_chain(g, c):
        blk = jnp.swapaxes(y_c[:, pl.ds(pl.multiple_of(g * 8, 8), 8), :], 0, 1)
        for j in range(8):
            mt = blk[j].T
            base = pl.multiple_of((g * 8 + j) * 2 * tile, 2 * tile)
            for hf in range(2):
                for b in range(n_b):
                    row0 = (hf * n_b + b) * tile
                    y_ref[b, pl.ds(base + hf * tile, tile), :] = mt[row0:row0 + tile, :]
        return c
    lax.fori_loop(0, vr // 8, value_from_chain, 0, unroll=2)

    assert 2 * (SCAN_TC // ts) == MEM_CHUNKS
    mem_phase(MEM_CHUNKS + 1)

    @pl.when(ci == pl.num_programs(0) - 1)
    def _():
        sout_ref[...] = s_scr[...]


def _rwkv_scan_prompt(r, k, w, a, v, k_k, k_a, r_k, lnx_w, lnx_b, s0, mem_k, mem_v):
    n_b, _, t = r.shape
    vr = RWKV_N // 2
    n_steps = t // SCAN_TC
    n_mem_seq = mem_k.shape[0]
    assert t % SCAN_TC == 0 and 2 * n_b * RWKV_HEADS == LANES and n_mem_seq % (n_steps * MEM_CHUNKS) == 0
    chunk_seqs = n_mem_seq // (n_steps * MEM_CHUNKS)
    any_spec = pl.BlockSpec(memory_space=pl.ANY)
    mem_shape = jax.ShapeDtypeStruct((n_mem_seq, MEM_HEADS, N_MEM, MEM_DH), F32)
    tok_spec = pl.BlockSpec((n_b, RWKV_W, SCAN_TC), lambda ci: (0, 0, ci))
    key_par = pl.BlockSpec((RWKV_N, LANES), lambda ci: (0, 0))
    val_par = pl.BlockSpec((vr, LANES), lambda ci: (0, 0))
    st_spec = pl.BlockSpec((vr, RWKV_N, LANES), lambda ci: (0, 0, 0))
    key_chain = pltpu.VMEM((SCAN_SUB, RWKV_N, LANES), F32)
    val_chain = pltpu.VMEM((SCAN_TC, vr, LANES), F32)
    return pl.pallas_call(
        functools.partial(_scan_prompt_body, n_b=n_b, mem_seqs=n_mem_seq // n_steps),
        grid=(n_steps,),
        in_specs=[tok_spec] * 5 + [key_par] * 3 + [val_par] * 2 + [st_spec, any_spec, any_spec],
        out_specs=[tok_spec, st_spec, any_spec, any_spec],
        out_shape=[jax.ShapeDtypeStruct((n_b, RWKV_W, t), F32),
                   jax.ShapeDtypeStruct((vr, RWKV_N, LANES), F32), mem_shape, mem_shape],
        scratch_shapes=([pltpu.VMEM((vr, RWKV_N, LANES), F32)] + [key_chain] * 5 + [val_chain] * 2
                        + [pltpu.VMEM((4, RWKV_N // 2, SCAN_TC - SCAN_SUB, LANES), F32),
                           pltpu.VMEM((2, 2, chunk_seqs, MEM_HEADS, N_MEM, MEM_DH), F32),
                           pltpu.SemaphoreType.DMA((2,)), pltpu.SemaphoreType.DMA((2,))]),
        compiler_params=_params("arbitrary"),
    )(r, k, w, a, v, k_k, k_a, r_k, lnx_w, lnx_b, s0, mem_k, mem_v)


def _merge_body(*refs, part_counts):
    n = len(part_counts)
    x_refs, oret_refs, g_refs = refs[:n], refs[n:2 * n], refs[2 * n:3 * n]
    yt_ref, ys_ref, wt_ref, wb_ref, gq_ref, wq_ref, o_ref, q_ref = refs[3 * n:]
    x = _read_row_parts(x_refs, part_counts)
    y = jnp.where(pl.program_id(0) >= part_counts[0], ys_ref[...], yt_ref[...].T)
    yb = (y * _read_row_parts(g_refs, part_counts)).astype(BF16)
    ob = _read_row_parts(oret_refs, part_counts).astype(BF16)
    n_chunk = 256
    for j in range(0, D_MODEL, n_chunk):
        acc = jnp.dot(ob, wt_ref[:, j:j + n_chunk], preferred_element_type=F32)
        acc = acc + jnp.dot(yb, wb_ref[:, j:j + n_chunk], preferred_element_type=F32)
        o_ref[:, j:j + n_chunk] = x[:, j:j + n_chunk] + acc
    h = o_ref[...]
    hb = (h * lax.rsqrt(jnp.mean(h * h, axis=-1, keepdims=True) + EPS) * gq_ref[...]).astype(BF16)
    for j in range(0, D_MODEL, n_chunk):
        q_ref[:, j:j + n_chunk] = jnp.dot(hb, wq_ref[:, j:j + n_chunk], preferred_element_type=F32)


def _merge(x_parts, oret_parts, g_parts, y_first_t, y_second, w_ret, w_rwkv, gain_q, w_q):
    m = sum(part.shape[0] for part in x_parts)
    in_specs, part_counts = [], None
    for parts in (x_parts, oret_parts, g_parts):
        specs, part_counts = _row_part_specs(parts, TM)
        in_specs += specs
    assert len(part_counts) == 2
    n_first = part_counts[0]
    tiles = y_first_t.shape[2] // TM
    assert y_first_t.shape[0] * tiles == n_first
    yt_spec = pl.BlockSpec((None, RWKV_W, TM),
                           lambda i: (jnp.minimum(i, n_first - 1) // tiles, 0, jnp.minimum(i, n_first - 1) % tiles))
    ys_spec = pl.BlockSpec((TM, RWKV_W), lambda i: (jnp.clip(i - n_first, 0, part_counts[1] - 1), 0))
    wspec = pl.BlockSpec((RET_W, D_MODEL), lambda i: (0, 0))
    row_spec = pl.BlockSpec((TM, D_MODEL), lambda i: (i, 0))
    row_shape = jax.ShapeDtypeStruct((m, D_MODEL), F32)
    return pl.pallas_call(
        functools.partial(_merge_body, part_counts=part_counts),
        grid=(m // TM,),
        in_specs=in_specs + [yt_spec, ys_spec, wspec, wspec, pl.BlockSpec((1, D_MODEL), lambda i: (0, 0)),
                             pl.BlockSpec((D_MODEL, D_MODEL), lambda i: (0, 0))],
        out_specs=[row_spec, row_spec],
        out_shape=[row_shape, row_shape],
        compiler_params=_params("parallel"),
    )(*x_parts, *oret_parts, *g_parts, y_first_t, y_second, w_ret.astype(BF16), w_rwkv.astype(BF16),
      gain_q.reshape(1, D_MODEL), w_q.astype(BF16))


def _mem_kv_body(x_ref, g_ref, wk_ref, wv_ref, k_ref, v_ref, kh_ref, vh_ref, *, n_seq):
    x = x_ref[...].astype(F32)
    xb = (x * lax.rsqrt(jnp.mean(x * x, axis=-1, keepdims=True) + EPS) * g_ref[...]).astype(BF16)
    for w_ref, o_ref, oh_ref in ((wk_ref, k_ref, kh_ref), (wv_ref, v_ref, vh_ref)):
        for h in range(MEM_HEADS):
            acc = jnp.dot(xb, w_ref[:, h * MEM_DH:(h + 1) * MEM_DH], preferred_element_type=F32)
            o_ref[:, h, :] = acc
            for s in range(n_seq):
                oh_ref[s, h] = acc[s * N_MEM:(s + 1) * N_MEM]


def _mem_kv(mem, gain, w_k, w_v):
    n_b = mem.shape[0]
    n_seq = TM // N_MEM
    assert n_seq * N_MEM == TM and n_b % n_seq == 0
    wspec = pl.BlockSpec((D_MODEL, D_MODEL), lambda i: (0, 0))
    tok_spec = pl.BlockSpec((TM, MEM_HEADS, MEM_DH), lambda i: (i, 0, 0))
    head_spec = pl.BlockSpec((n_seq, MEM_HEADS, N_MEM, MEM_DH), lambda i: (i, 0, 0, 0))
    tok_shape = jax.ShapeDtypeStruct((n_b * N_MEM, MEM_HEADS, MEM_DH), F32)
    head_shape = jax.ShapeDtypeStruct((n_b, MEM_HEADS, N_MEM, MEM_DH), F32)
    return pl.pallas_call(
        functools.partial(_mem_kv_body, n_seq=n_seq),
        grid=(n_b // n_seq,),
        in_specs=[pl.BlockSpec((TM, D_MODEL), lambda i: (i, 0)), pl.BlockSpec((1, D_MODEL), lambda i: (0, 0)),
                  wspec, wspec],
        out_specs=[tok_spec, tok_spec, head_spec, head_spec],
        out_shape=[tok_shape, tok_shape, head_shape, head_shape],
        compiler_params=_params("parallel"),
    )(mem.reshape(n_b * N_MEM, D_MODEL), gain.reshape(1, D_MODEL), w_k.astype(BF16), w_v.astype(BF16))


def _attn_body(q_ref, k_ref, v_ref, o_ref, *, n_seq, tq):
    nt = (((1,), (1,)), ((), ()))
    for g in range(n_seq):
        rows = slice(g * tq, (g + 1) * tq)
        q = q_ref[rows, :].astype(BF16)
        s = lax.dot_general(q, k_ref[g].astype(BF16), nt, preferred_element_type=F32) * (MEM_DH ** -0.5)
        p = jnp.exp(s - jnp.max(s, axis=-1, keepdims=True))
        l = jnp.sum(p, axis=-1, keepdims=True)
        o = jnp.dot(p.astype(BF16), v_ref[g].astype(BF16), preferred_element_type=F32)
        o_ref[rows, :] = o / l


def _attention(q, mem_k, mem_v, *, row0, n_batch, t, n_seq, tq):
    q_tiles = t // tq
    rows = n_seq * tq
    assert t % tq == 0 and n_batch % n_seq == 0 and row0 % rows == 0 and (n_seq == 1 or q_tiles == 1)
    blk0 = row0 // rows
    kv_spec = pl.BlockSpec((n_seq, None, N_MEM, MEM_DH), lambda b, h, qi: (b, h, 0, 0))
    return pl.pallas_call(
        functools.partial(_attn_body, n_seq=n_seq, tq=tq),
        grid=(n_batch // n_seq, MEM_HEADS, q_tiles),
        in_specs=[pl.BlockSpec((rows, MEM_DH), lambda b, h, qi: (blk0 + b * q_tiles + qi, h)), kv_spec, kv_spec],
        out_specs=pl.BlockSpec((rows, MEM_DH), lambda b, h, qi: (b * q_tiles + qi, h)),
        out_shape=jax.ShapeDtypeStruct((n_batch * t, D_MODEL), F32),
        compiler_params=_params("parallel", "parallel", "parallel"),
    )(q, mem_k, mem_v)


ROW_TILE = (D_MODEL // LANES, LANES)


def _rows_to_tiles(x):
    chunks = [x[:, j * LANES:(j + 1) * LANES] for j in range(ROW_TILE[0])]
    return jnp.swapaxes(jnp.stack(chunks, axis=0), 0, 1)


def _tiles_to_rows(x):
    chunks = jnp.swapaxes(x, 0, 1)
    return jnp.concatenate([chunks[j] for j in range(ROW_TILE[0])], axis=1)


def _attn_out_body(*refs, part_counts):
    n = len(part_counts)
    w_ref, res_ref, o_ref = refs[n:]
    ab = _read_row_parts(refs[:n], part_counts).astype(BF16)
    n_chunk = 256
    for j in range(0, D_MODEL, n_chunk):
        o_ref[:, j:j + n_chunk] = res_ref[:, j:j + n_chunk] + jnp.dot(
            ab, w_ref[:, j:j + n_chunk], preferred_element_type=F32)


def _attn_out(att_parts, w_mo, residual):
    m = residual.shape[0]
    att_specs, part_counts = _row_part_specs(att_parts, TM)
    row = pl.BlockSpec((TM, D_MODEL), lambda i: (i, 0))
    return pl.pallas_call(
        functools.partial(_attn_out_body, part_counts=part_counts),
        grid=(m // TM,),
        in_specs=att_specs + [pl.BlockSpec((D_MODEL, D_MODEL), lambda i: (0, 0)), row],
        out_specs=row,
        out_shape=jax.ShapeDtypeStruct((m, D_MODEL), F32),
        compiler_params=_params("parallel"),
    )(*att_parts, w_mo.astype(BF16), residual)


def _router_body(h_ref, g_ref, w_ref, b_ref, hn_ref, ids_ref, comb_ref):
    x = h_ref[...]
    hn = x * lax.rsqrt(jnp.mean(x * x, axis=-1, keepdims=True) + EPS) * g_ref[...]
    hn_ref[...] = _rows_to_tiles(hn)
    logits = jnp.dot(hn, w_ref[...], precision=lax.Precision.HIGHEST, preferred_element_type=F32) + b_ref[...]
    lane = lax.broadcasted_iota(jnp.int32, logits.shape, 1).astype(F32)
    neg = -jnp.inf

    def first_argmax(vals):
        m = jnp.max(vals, axis=-1, keepdims=True)
        return m, jnp.min(jnp.where(vals == m, lane, float(LANES)), axis=-1, keepdims=True)

    gl = jnp.where(lane < N_GROUPS, logits, neg)
    gmax, gsel = first_argmax(gl)
    pg_sel = 1.0 / jnp.sum(jnp.exp(gl - gmax), axis=-1, keepdims=True)
    e0 = N_GROUPS + gsel * EXP_PER_GROUP
    el = jnp.where((lane >= e0) & (lane < e0 + EXP_PER_GROUP), logits, neg)
    m1, i1 = first_argmax(el)
    m2, i2 = first_argmax(jnp.where(lane == i1, neg, el))
    e21 = jnp.exp(m2 - m1)
    c1 = pg_sel / (1.0 + e21)
    c2 = c1 * e21
    ids = jnp.where(lane == 0, i1 - N_GROUPS, jnp.where(lane == 1, i2 - N_GROUPS, 0.0))
    ids_ref[...] = ids.astype(jnp.int32)
    comb_ref[...] = jnp.where(lane == 0, c1, jnp.where(lane == 1, c2, 0.0))


def _router(h, g_ffn, w_gr, b_gr, w_er, b_er):
    m = h.shape[0]
    pad = LANES - N_GROUPS - N_EXPERTS
    w = jnp.concatenate([w_gr, w_er, jnp.zeros((D_MODEL, pad), F32)], axis=1)
    b = jnp.concatenate([b_gr, b_er, jnp.zeros((pad,), F32)]).reshape(1, LANES)
    row = lambda n: pl.BlockSpec((TM, n), lambda i: (i, 0))
    return pl.pallas_call(
        _router_body,
        grid=(m // TM,),
        in_specs=[row(D_MODEL), pl.BlockSpec((1, D_MODEL), lambda i: (0, 0)),
                  pl.BlockSpec((D_MODEL, LANES), lambda i: (0, 0)), pl.BlockSpec((1, LANES), lambda i: (0, 0))],
        out_specs=[pl.BlockSpec((TM,) + ROW_TILE, lambda i: (i, 0, 0)), row(LANES), row(LANES)],
        out_shape=[jax.ShapeDtypeStruct((m,) + ROW_TILE, F32), jax.ShapeDtypeStruct((m, LANES), jnp.int32),
                   jax.ShapeDtypeStruct((m, LANES), F32)],
        compiler_params=_params("parallel"),
    )(h, g_ffn.reshape(1, D_MODEL), w, b)


def _dispatch_body(dest_ref, hn_ref, sorted_in, sorted_out, sem):
    del sorted_in

    def issue(r, c):
        for k in range(TOP_K):
            pltpu.make_async_copy(hn_ref.at[r], sorted_out.at[dest_ref[0, k, r]], sem.at[k]).start(priority=k)
        return c
    lax.fori_loop(0, TM, issue, 0, unroll=8)
    for k in range(TOP_K):
        pltpu.make_async_copy(hn_ref, sorted_out.at[pl.ds(0, TM)], sem.at[k]).wait()


def _dispatch(hn, dest, n_sorted):
    n_tok = hn.shape[0]
    return pl.pallas_call(
        _dispatch_body,
        grid=(n_tok // TM,),
        in_specs=[pl.BlockSpec((1, TOP_K, TM), lambda i: (i, 0, 0), memory_space=pltpu.SMEM),
                  pl.BlockSpec((TM,) + ROW_TILE, lambda i: (i, 0, 0)),
                  pl.BlockSpec(memory_space=pl.ANY)],
        out_specs=pl.BlockSpec(memory_space=pl.ANY),
        out_shape=jax.ShapeDtypeStruct((n_sorted,) + ROW_TILE, F32),
        scratch_shapes=[pltpu.SemaphoreType.DMA((TOP_K,))],
        input_output_aliases={2: 0},
        compiler_params=_params("arbitrary"),
    )(dest, hn, jnp.zeros((n_sorted,) + ROW_TILE, F32))


def _expert_body(blk_e_ref, x_ref, wg_ref, wu_ref, wd_ref, o_ref):
    del blk_e_ref
    x = _tiles_to_rows(x_ref[...]).astype(BF16)
    hg = jnp.dot(x, wg_ref[0].astype(BF16), preferred_element_type=F32)
    hu = jnp.dot(x, wu_ref[0].astype(BF16), preferred_element_type=F32)
    act = (hg * jax.nn.sigmoid(hg) * hu).astype(BF16)
    o_ref[...] = _rows_to_tiles(jnp.dot(act, wd_ref[0].astype(BF16), preferred_element_type=F32))


def _experts(x_sorted, blk_e, w_gate, w_up, w_down):
    n_blocks = blk_e.shape[0]
    row_spec = pl.BlockSpec((MOE_ROWS,) + ROW_TILE, lambda i, be: (i, 0, 0))
    grid_spec = pltpu.PrefetchScalarGridSpec(
        num_scalar_prefetch=1,
        grid=(n_blocks,),
        in_specs=[
            row_spec,
            pl.BlockSpec((1, D_MODEL, D_EXPERT), lambda i, be: (be[i], 0, 0)),
            pl.BlockSpec((1, D_MODEL, D_EXPERT), lambda i, be: (be[i], 0, 0)),
            pl.BlockSpec((1, D_EXPERT, D_MODEL), lambda i, be: (be[i], 0, 0)),
        ],
        out_specs=row_spec,
    )
    return pl.pallas_call(
        _expert_body,
        grid_spec=grid_spec,
        out_shape=jax.ShapeDtypeStruct(x_sorted.shape, F32),
        compiler_params=_params("arbitrary"),
    )(blk_e, x_sorted, w_gate, w_up, w_down)


def _route_plan(ids):
    n_tok = ids.shape[0]
    n_pairs = ids.size
    n_blocks = -(-(n_pairs + N_EXPERTS * (MOE_ROWS - 1)) // MOE_ROWS)
    flat_e = ids.reshape(n_pairs)
    onehot = (flat_e[:, None] == jnp.arange(N_EXPERTS, dtype=jnp.int32)[None, :]).astype(jnp.int32)
    csum = jnp.cumsum(onehot, axis=0)
    rank = jnp.sum(onehot * csum, axis=1) - 1
    counts = csum[-1]
    pcounts = (counts + MOE_ROWS - 1) // MOE_ROWS * MOE_ROWS
    pends = jnp.cumsum(pcounts)
    pstarts = pends - pcounts
    dest = jnp.sum(onehot * pstarts[None, :], axis=1) + rank
    block_start = jnp.arange(n_blocks, dtype=jnp.int32) * MOE_ROWS
    blk_e = jnp.minimum(jnp.sum((block_start[:, None] >= pends[None, :]).astype(jnp.int32), axis=1),
                        N_EXPERTS - 1).astype(jnp.int32)
    dest = dest.astype(jnp.int32).reshape(n_tok // TM, TM, TOP_K).transpose(0, 2, 1)
    return blk_e, dest, n_blocks * MOE_ROWS


def _final_body(dest_ref, dest_next_ref, h_ref, comb_ref, g_ref, y_hbm, o_first, o_second, ybuf, sem, *, n_first):
    i = pl.program_id(0)
    n = pl.num_programs(0)
    slot = i % 2

    def start_gather(ref, sl):
        def issue(r, c):
            for k in range(TOP_K):
                pltpu.make_async_copy(y_hbm.at[ref[0, k, r]], ybuf.at[sl, k, r], sem.at[sl]).start(priority=k)
            return c
        lax.fori_loop(0, TM, issue, 0, unroll=8)

    @pl.when(i == 0)
    def _():
        start_gather(dest_ref, 0)

    @pl.when(i + 1 < n)
    def _():
        start_gather(dest_next_ref, 1 - slot)

    for k in range(TOP_K):
        pltpu.make_async_copy(y_hbm.at[pl.ds(0, TM)], ybuf.at[slot, k], sem.at[slot]).wait()

    first, second = _tiles_to_rows(ybuf[slot, 0]), _tiles_to_rows(ybuf[slot, 1])
    x = h_ref[...] + (first * comb_ref[:, 0:1] + second * comb_ref[:, 1:2])
    out = x * lax.rsqrt(jnp.mean(x * x, axis=-1, keepdims=True) + EPS) * g_ref[...]

    @pl.when(i < n_first)
    def _():
        o_first[...] = out

    @pl.when(i >= n_first)
    def _():
        o_second[...] = out


def _final(h, y_sorted, dest, comb, g_final, *, n_first_rows):
    n_rows = h.shape[0]
    assert n_rows % TM == 0 and n_first_rows % TM == 0 and 0 < n_first_rows < n_rows
    n_steps, n_first = n_rows // TM, n_first_rows // TM
    row = lambda n: pl.BlockSpec((TM, n), lambda i: (i, 0))
    dest_spec = lambda f: pl.BlockSpec((1, TOP_K, TM), f, memory_space=pltpu.SMEM)
    return pl.pallas_call(
        functools.partial(_final_body, n_first=n_first),
        grid=(n_steps,),
        in_specs=[dest_spec(lambda i: (i, 0, 0)),
                  dest_spec(lambda i: (jnp.minimum(i + 1, n_steps - 1), 0, 0)),
                  row(D_MODEL), row(LANES), pl.BlockSpec((1, D_MODEL), lambda i: (0, 0)),
                  pl.BlockSpec(memory_space=pl.ANY)],
        out_specs=[pl.BlockSpec((TM, D_MODEL), lambda i: (jnp.minimum(i, n_first - 1), 0)),
                   pl.BlockSpec((TM, D_MODEL), lambda i: (jnp.maximum(i - n_first, 0), 0))],
        out_shape=[jax.ShapeDtypeStruct((n_first_rows, D_MODEL), F32),
                   jax.ShapeDtypeStruct((n_rows - n_first_rows, D_MODEL), F32)],
        scratch_shapes=[pltpu.VMEM((2, TOP_K, TM) + ROW_TILE, F32), pltpu.SemaphoreType.DMA((2,))],
        compiler_params=_params("arbitrary"),
    )(dest, dest, h, comb, g_final.reshape(1, D_MODEL), y_sorted)


def _reorder_last(x, shape, order):
    lead = x.shape[:-1]
    n = len(lead)
    y = x.reshape(lead + shape).transpose(tuple(range(n)) + tuple(n + o for o in order))
    return y.reshape(lead + (x.shape[-1],))


HALF_N = RWKV_N // 2


def _key_major(x):
    return _reorder_last(x, (RWKV_HEADS, RWKV_N), (1, 0))


def _key_major_inv(x):
    return _reorder_last(x, (RWKV_N, RWKV_HEADS), (1, 0))


def _value_major(x):
    return _reorder_last(x, (RWKV_HEADS, 2, HALF_N), (2, 1, 0))


def _value_major_inv(x):
    return _reorder_last(x, (HALF_N, 2, RWKV_HEADS), (2, 1, 0))


def _rwkv_cols(x, key_fn, value_fn):
    return jnp.concatenate([key_fn(x[..., :RWKV_W]), key_fn(x[..., RWKV_W:2 * RWKV_W]),
                            value_fn(x[..., 2 * RWKV_W:3 * RWKV_W]), x[..., 3 * RWKV_W:]], axis=-1)


def kernel(x_prompt, x_sample, mem_prompt, state_ret, state_rwkv, state_shift, cache_mem_k, cache_mem_v,
           g_mix, w_in, ret_gn, rwkv_mu, rwkv_w0, rwkv_w2, rwkv_a0, rwkv_a2, rwkv_g2, rwkv_k_k, rwkv_k_a,
           rwkv_r_k, rwkv_lnx_w, rwkv_lnx_b, w_out, g_mem_q, g_mem_kv, w_mq, w_mk, w_mv, w_mo, g_ffn,
           w_group_router, b_group_router, w_expert_router, b_expert_router, w_e_gate, w_e_up, w_e_down,
           g_final):
    assert w_in.shape[0] == 1, "single-layer decoder"
    bp, tp, d = x_prompt.shape
    bs, ts, _ = x_sample.shape
    np_tok, ns_tok = bp * tp, bs * ts
    assert d == D_MODEL and bp * RWKV_HEADS * 2 == LANES and bs == LANES
    l = 0
    x_parts = [x_prompt.reshape(np_tok, d), x_sample.reshape(ns_tok, d)]

    w_in_l = jnp.concatenate([w_in[l][:, :N_RET_COLS], _rwkv_cols(w_in[l][:, N_RET_COLS:], _key_major, _value_major)],
                             axis=1)
    proj = _input_projection(x_parts, g_mix[l], w_in_l)

    pos_p = np.arange(tp)
    pos_s = PAST_LEN + np.arange(ts)
    zero_ret = jnp.zeros((bp, RET_HEADS, RET_DK, RET_DV), F32)
    oret_p, sret_p = _retention(proj, zero_ret, ret_gn[l], pos_p, row0=0, n_batch=bp, t=tp, n_blk=4, per_blk=1)
    oret_s, sret_s = _retention(proj, state_ret[l], ret_gn[l], pos_s, row0=np_tok, n_batch=bs, t=ts,
                                n_blk=1, per_blk=16)
    oret_p, oret_s = oret_p.reshape(np_tok, RET_W), oret_s.reshape(ns_tok, RET_W)

    pre_w = (_rwkv_cols(rwkv_mu[l], _key_major, _value_major), _key_major(rwkv_w0[l]), _key_major(rwkv_w2[l]),
             _key_major(rwkv_a0[l]), _key_major(rwkv_a2[l]), _value_major(rwkv_g2[l]))
    zero_shift = jnp.zeros((bp, N_RWKV_COLS), F32)
    shift_in = _rwkv_cols(state_shift[l], _key_major, _value_major)
    r_p, k_p, v_p, w_p, a_p, gate_p, shift_p = _rwkv_pre(proj, zero_shift, *pre_w, row0=0, n_batch=bp, t=tp, c=512)
    r_s, k_s, v_s, w_s, a_s, gate_s, shift_s = _rwkv_pre_short(proj, shift_in, *pre_w, row0=np_tok, n_batch=bs,
                                                                t=ts)

    kvec = lambda v: v.reshape(RWKV_HEADS, RWKV_N)
    key_par = lambda v: jnp.broadcast_to(kvec(v).T[:, None, None, :], (RWKV_N, 2, bp, RWKV_HEADS)).reshape(
        RWKV_N, LANES)

    val_par = lambda v: jnp.broadcast_to(
        v.reshape(RWKV_HEADS, 2, HALF_N).transpose(2, 1, 0)[:, :, None, :],
        (HALF_N, 2, bp, RWKV_HEADS)).reshape(HALF_N, LANES)
    mem_shape = (N_MEM, MEM_HEADS, MEM_DH)
    y_p, srw_p, cache_k_heads, cache_v_heads = _rwkv_scan_prompt(
        r_p, k_p, w_p, a_p, v_p, key_par(rwkv_k_k[l]), key_par(rwkv_k_a[l]), key_par(rwkv_r_k[l]),
        val_par(rwkv_lnx_w[l]), val_par(rwkv_lnx_b[l]), jnp.zeros((HALF_N, RWKV_N, LANES), F32),
        cache_mem_k.reshape(bs, *mem_shape), cache_mem_v.reshape(bs, *mem_shape))
    srw_p = srw_p.reshape(HALF_N, RWKV_N, 2, bp, RWKV_HEADS).transpose(3, 4, 2, 0, 1).reshape(
        bp, RWKV_HEADS, RWKV_N, RWKV_N)
    head_par = lambda v: jnp.broadcast_to(kvec(v)[:, :, None], (RWKV_HEADS, RWKV_N, LANES))
    val_rows = lambda v: v.reshape(RWKV_HEADS, 2, HALF_N).transpose(0, 2, 1).reshape(RWKV_HEADS * RWKV_N)
    state_s = state_rwkv[l].astype(F32).reshape(bs, RWKV_HEADS, 2, HALF_N, RWKV_N).transpose(1, 3, 2, 4, 0)
    y_s, srw_s = _rwkv_scan(
        r_s, k_s, w_s, a_s, v_s, head_par(rwkv_k_k[l]), head_par(rwkv_k_a[l]), head_par(rwkv_r_k[l]),
        head_par(val_rows(rwkv_lnx_w[l])), head_par(val_rows(rwkv_lnx_b[l])),
        state_s.reshape(RWKV_HEADS, RWKV_N, RWKV_N, bs), tc=ts, halves=1)
    y_s = y_s.transpose(3, 1, 2, 0).reshape(ns_tok, RWKV_W)
    srw_s = srw_s.reshape(RWKV_HEADS, HALF_N, 2, RWKV_N, bs).transpose(4, 0, 2, 1, 3).reshape(
        bs, RWKV_HEADS, RWKV_N, RWKV_N)

    w_rwkv_out = _value_major(w_out[l][RET_W:].T).T
    h, q = _merge(x_parts, [oret_p, oret_s], [gate_p, gate_s], y_p, y_s, w_out[l][:RET_W], w_rwkv_out,
                  g_mem_q[l], w_mq[l])

    mk, mv, mk_heads, mv_heads = _mem_kv(mem_prompt, g_mem_kv[l], w_mk[l], w_mv[l])
    att_p = _attention(q, mk_heads, mv_heads, row0=0, n_batch=bp, t=tp, n_seq=1, tq=tp)
    att_s = _attention(q, cache_k_heads, cache_v_heads, row0=np_tok, n_batch=bs, t=ts, n_seq=16, tq=ts)

    h = _attn_out([att_p, att_s], w_mo[l], h)

    hn, ids, comb = _router(h, g_ffn[l], w_group_router[l], b_group_router[l], w_expert_router[l],
                            b_expert_router[l])
    blk_e, dest, n_sorted = _route_plan(ids[:, :TOP_K])
    y_sorted = _experts(_dispatch(hn, dest, n_sorted), blk_e, w_e_gate[l], w_e_up[l], w_e_down[l])
    y_prompt, y_sample = _final(h, y_sorted, dest, comb, g_final, n_first_rows=np_tok)
    y_prompt = y_prompt.reshape(bp, tp, d)
    y_sample = y_sample.reshape(bs, ts, d)

    shift_p = _rwkv_cols(shift_p.reshape(bp, N_RWKV_COLS), _key_major_inv, _value_major_inv)
    shift_s = _rwkv_cols(shift_s.reshape(bs, N_RWKV_COLS), _key_major_inv, _value_major_inv)
    return (y_prompt, y_sample, sret_p[None], srw_p[None], shift_p[None],
            mk.reshape(1, bp, *mem_shape), mv.reshape(1, bp, *mem_shape),
            sret_s[None], srw_s[None], shift_s[None])
```

```python
import functools

import numpy as np
import jax
import jax.numpy as jnp
from jax import lax
from jax.experimental import pallas as pl
from jax.experimental.pallas import tpu as pltpu

F32 = jnp.float32
BF16 = jnp.bfloat16

D_MODEL = 1024
PAST_LEN = 16384
N_MEM = 256
MEM_HEADS = 4
MEM_DH = D_MODEL // MEM_HEADS
RET_HEADS = 4
RET_W = D_MODEL // 2
RET_DV = RET_W // RET_HEADS
RET_DK = RET_DV // 2
RET_QK = RET_HEADS * RET_DK
RET_CHUNK = 128
ROPE_BASE = 10000.0
RWKV_N = 64
RWKV_W = D_MODEL - RET_W
RWKV_HEADS = RWKV_W // RWKV_N
LORA_W = 64
LORA_A = 64
LORA_G = 128
LNX_EPS = 64e-5
N_RET_COLS = 2 * RET_QK + 2 * RET_W
N_RWKV_COLS = 3 * RWKV_W + LORA_W + LORA_A + LORA_G
N_IN_COLS = N_RET_COLS + N_RWKV_COLS
N_GROUPS = 4
EXP_PER_GROUP = 8
N_EXPERTS = N_GROUPS * EXP_PER_GROUP
TOP_K = 2
D_EXPERT = D_MODEL // 2
EPS = 1e-6

LANES = 128
MOE_ROWS = 512
TM = 512


def _params(*sem):
    return pltpu.CompilerParams(dimension_semantics=sem)


def _row_part_specs(parts, tm):
    specs, counts, start = [], [], 0
    for part in parts:
        nb = part.shape[0] // tm
        assert nb * tm == part.shape[0]
        specs.append(pl.BlockSpec((tm, part.shape[1]), lambda i, s=start, n=nb: (jnp.clip(i - s, 0, n - 1), 0)))
        counts.append(nb)
        start += nb
    return specs, counts


def _read_row_parts(refs, counts):
    i = pl.program_id(0)
    x = refs[0][...]
    start = counts[0]
    for ref, nb in zip(refs[1:], counts[1:]):
        x = jnp.where(i >= start, ref[...], x)
        start += nb
    return x


def _proj_body(*refs, part_counts):
    n = len(part_counts)
    g_ref, w_ref, o_ref = refs[n:]
    x = _read_row_parts(refs[:n], part_counts).astype(F32)
    xb = (x * lax.rsqrt(jnp.mean(x * x, axis=-1, keepdims=True) + EPS) * g_ref[...]).astype(BF16)
    n_chunk = 256
    for j in range(0, w_ref.shape[1], n_chunk):
        o_ref[:, j:j + n_chunk] = jnp.dot(xb, w_ref[:, j:j + n_chunk], preferred_element_type=F32)


def _input_projection(x_parts, gain, w):
    k, n_out = w.shape
    m = sum(part.shape[0] for part in x_parts)
    assert n_out % 256 == 0
    in_specs, part_counts = _row_part_specs(x_parts, TM)
    return pl.pallas_call(
        functools.partial(_proj_body, part_counts=part_counts),
        grid=(m // TM,),
        in_specs=in_specs + [pl.BlockSpec((1, k), lambda i: (0, 0)), pl.BlockSpec((k, n_out), lambda i: (0, 0))],
        out_specs=pl.BlockSpec((TM, n_out), lambda i: (i, 0)),
        out_shape=jax.ShapeDtypeStruct((m, n_out), F32),
        compiler_params=_params("parallel"),
    )(*x_parts, gain.reshape(1, k).astype(F32), w.astype(BF16))


def _rot_tables(pos):
    half = RET_DK // 2
    inv_freq = ROPE_BASE ** (-(np.arange(half, dtype=np.float64) / half))
    ang = pos.astype(np.float64)[:, None] * inv_freq[None, :]
    cos, sin = np.cos(ang), np.sin(ang)
    zero = np.zeros_like(sin)
    c = np.tile(np.concatenate([cos, cos], axis=1), (1, RET_HEADS))
    s_lo = np.tile(np.concatenate([-sin, zero], axis=1), (1, RET_HEADS))
    s_hi = np.tile(np.concatenate([zero, sin], axis=1), (1, RET_HEADS))
    return [jnp.asarray(t, F32) for t in (c, s_lo, s_hi)]


def _ret_decay_tables(c):
    lg = np.log1p(-np.exp2(-5.0 - np.arange(RET_HEADS, dtype=np.float64)))
    idx = np.arange(c, dtype=np.float64)
    diff = idx[:, None] - idx[None, :]
    mask = np.where(diff[None] >= 0, np.exp(np.maximum(diff, 0.0)[None] * lg[:, None, None]), 0.0)
    q_dec = np.repeat(np.exp((idx[:, None] + 1.0) * lg[None, :]), RET_DV, axis=1)
    k_dec = np.repeat(np.exp((c - 1.0 - idx)[:, None] * lg[None, :]), RET_DK, axis=1)
    c_dec = [float(v) for v in np.exp(c * lg)]
    return jnp.asarray(mask, F32), jnp.asarray(q_dec, F32), jnp.asarray(k_dec, F32), c_dec


def _ret_body(*refs, n_blk, per_blk, c, c_dec):
    q_refs, k_refs, v_refs, gate_refs = (refs[j * n_blk:(j + 1) * n_blk] for j in range(4))
    (c_ref, slo_ref, shi_ref, mask_ref, qdec_ref, kdec_ref, gn_ref, s0_ref, o_ref, sout_ref,
     s_scr) = refs[4 * n_blk:]
    n_seq = n_blk * per_blk
    ci = pl.program_id(1)

    @pl.when(ci == 0)
    def _():
        s_scr[...] = s0_ref[...].astype(F32)

    cos, s_lo, s_hi = c_ref[...], slo_ref[...], shi_ref[...]
    half = RET_DK // 2

    def rope(x):
        return x * cos + pltpu.roll(x, RET_QK - half, 1) * s_lo + pltpu.roll(x, half, 1) * s_hi

    nt = (((1,), (1,)), ((), ()))
    tn = (((0,), (0,)), ((), ()))
    for g in range(n_seq):
        blk = g // per_blk
        rows = slice((g % per_blk) * c, (g % per_blk + 1) * c)
        q = rope(q_refs[blk][rows, :].astype(F32))
        k = rope(k_refs[blk][rows, :].astype(F32)) * (RET_DK ** -0.5)
        k_st = k * kdec_ref[...]
        for h in range(RET_HEADS):
            kc = slice(h * RET_DK, (h + 1) * RET_DK)
            vc = slice(h * RET_DV, (h + 1) * RET_DV)
            qh = q[:, kc].astype(BF16)
            vh = v_refs[blk][rows, vc].astype(BF16)
            s_h = s_scr[g, h]
            att = lax.dot_general(qh, k[:, kc].astype(BF16), nt, preferred_element_type=F32) * mask_ref[h]
            o = jnp.dot(att.astype(BF16), vh, preferred_element_type=F32)
            o = o + jnp.dot(qh, s_h.astype(BF16), preferred_element_type=F32) * qdec_ref[:, vc]
            s_scr[g, h] = s_h * c_dec[h] + lax.dot_general(
                k_st[:, kc].astype(BF16), vh, tn, preferred_element_type=F32)
            o = o * lax.rsqrt(jnp.mean(o * o, axis=-1, keepdims=True) + EPS)
            gate = gate_refs[blk][rows, vc].astype(F32)
            o_ref[g, :, vc] = o * gn_ref[:, vc] * (gate * jax.nn.sigmoid(gate))

    @pl.when(ci == pl.num_programs(1) - 1)
    def _():
        sout_ref[...] = s_scr[...]


def _retention(proj, s0, ret_gn, pos, *, row0, n_batch, t, n_blk, per_blk):
    c = RET_CHUNK if t % RET_CHUNK == 0 else t
    n_chunks = t // c
    rows = per_blk * c
    n_seq = n_blk * per_blk
    assert n_batch % n_seq == 0 and row0 % rows == 0 and (per_blk == 1 or n_chunks == 1)
    blk0 = row0 // rows
    mask, q_dec, k_dec, c_dec = _ret_decay_tables(c)
    cos, s_lo, s_hi = _rot_tables(pos)

    def const2(b, ci):
        return (0, 0)

    def row_specs(width, col):
        return [pl.BlockSpec((rows, width), lambda b, ci, j=j: (blk0 + (b * n_blk + j) * n_chunks + ci, col))
                for j in range(n_blk)]

    state_spec = pl.BlockSpec((n_seq, RET_HEADS, RET_DK, RET_DV), lambda b, ci: (b, 0, 0, 0))
    in_specs = row_specs(RET_QK, 0) + row_specs(RET_QK, 1) + row_specs(RET_W, 1) + row_specs(RET_W, 2) + [
        pl.BlockSpec((c, RET_QK), lambda b, ci: (ci, 0)),
        pl.BlockSpec((c, RET_QK), lambda b, ci: (ci, 0)),
        pl.BlockSpec((c, RET_QK), lambda b, ci: (ci, 0)),
        pl.BlockSpec((RET_HEADS, c, c), lambda b, ci: (0, 0, 0)),
        pl.BlockSpec((c, RET_W), const2),
        pl.BlockSpec((c, RET_QK), const2),
        pl.BlockSpec((1, RET_W), const2),
        state_spec,
    ]
    return pl.pallas_call(
        functools.partial(_ret_body, n_blk=n_blk, per_blk=per_blk, c=c, c_dec=c_dec),
        grid=(n_batch // n_seq, n_chunks),
        in_specs=in_specs,
        out_specs=[pl.BlockSpec((n_seq, c, RET_W), lambda b, ci: (b, ci, 0)), state_spec],
        out_shape=[jax.ShapeDtypeStruct((n_batch, t, RET_W), F32),
                   jax.ShapeDtypeStruct((n_batch, RET_HEADS, RET_DK, RET_DV), F32)],
        scratch_shapes=[pltpu.VMEM((n_seq, RET_HEADS, RET_DK, RET_DV), F32)],
        compiler_params=_params("parallel", "arbitrary"),
    )(*([proj] * (4 * n_blk)), cos, s_lo, s_hi, mask, q_dec, k_dec, ret_gn.reshape(1, RET_W).astype(F32), s0)


LORA_COLS = LORA_W + LORA_A + LORA_G


def _rwkv_lora_terms(lo, w0_ref, w2_ref, a0_ref, a2_ref, g2_ref):
    hw = lo[:, :LORA_W]
    ha = lo[:, LORA_W:LORA_W + LORA_A]
    hg = lo[:, LORA_W + LORA_A:]
    u = w0_ref[...] + jnp.dot(jnp.tanh(hw).astype(BF16), w2_ref[...], preferred_element_type=F32)
    decay = jnp.exp(-float(np.exp(-0.5)) * jax.nn.sigmoid(u))
    rate = jax.nn.sigmoid(a0_ref[...] + jnp.dot(ha.astype(BF16), a2_ref[...], preferred_element_type=F32))
    gate = jnp.dot(jax.nn.sigmoid(hg).astype(BF16), g2_ref[...], preferred_element_type=F32)
    return decay, rate, gate


def _rwkv_pre_body(r_ref, k_ref, v_ref, lo_ref, shift_ref, mu_ref, w0_ref, w2_ref, a0_ref, a2_ref, g2_ref,
                   ro_ref, ko_ref, vo_ref, wo_ref, ao_ref, go_ref, so_ref, prev_scr):
    ci = pl.program_id(1)
    c = r_ref.shape[0]

    @pl.when(ci == 0)
    def _():
        prev_scr[...] = shift_ref[0].astype(F32)

    first_row = lax.broadcasted_iota(jnp.int32, (c, 1), 0) == 0

    def shifted(x_ref, col0):
        w = x_ref.shape[1]
        x = x_ref[...].astype(F32)
        prev = jnp.where(first_row, prev_scr[:, col0:col0 + w], pltpu.roll(x, 1, 0))
        prev_scr[:, col0:col0 + w] = x[c - 1:c, :]
        return x + (prev - x) * mu_ref[:, col0:col0 + w]

    ro_ref[...] = shifted(r_ref, 0).T
    ko_ref[...] = shifted(k_ref, RWKV_W).T
    vo_ref[...] = shifted(v_ref, 2 * RWKV_W).T
    decay, rate, gate = _rwkv_lora_terms(shifted(lo_ref, 3 * RWKV_W), w0_ref, w2_ref, a0_ref, a2_ref, g2_ref)
    wo_ref[...] = decay.T
    ao_ref[...] = rate.T
    go_ref[...] = gate

    @pl.when(ci == pl.num_programs(1) - 1)
    def _():
        so_ref[0] = prev_scr[...]


def _rwkv_pre_args(s_shift, n_batch, mu, w0, w2, a0, a2, g2):
    return (s_shift.reshape(n_batch, 1, N_RWKV_COLS), mu.reshape(1, -1), w0.reshape(1, -1), w2.astype(BF16),
            a0.reshape(1, -1), a2.astype(BF16), g2.astype(BF16))


def _rwkv_pre_weight_specs(const):
    return [pl.BlockSpec((1, N_RWKV_COLS), const), pl.BlockSpec((1, RWKV_W), const),
            pl.BlockSpec((LORA_W, RWKV_W), const), pl.BlockSpec((1, RWKV_W), const),
            pl.BlockSpec((LORA_A, RWKV_W), const), pl.BlockSpec((LORA_G, RWKV_W), const)]


def _rwkv_pre(proj, s_shift, mu, w0, w2, a0, a2, g2, *, row0, n_batch, t, c):
    n_chunks = t // c
    assert t % c == 0 and row0 % c == 0
    blk0 = row0 // c
    col_r = N_RET_COLS // RWKV_W
    col_lo = (N_RET_COLS + 3 * RWKV_W) // LORA_COLS
    assert col_r * RWKV_W == N_RET_COLS and col_lo * LORA_COLS == N_RET_COLS + 3 * RWKV_W

    def row_map(col):
        return lambda b, ci: (blk0 + b * n_chunks + ci, col)

    state_spec = pl.BlockSpec((1, 1, N_RWKV_COLS), lambda b, ci: (b, 0, 0))
    in_specs = [pl.BlockSpec((c, RWKV_W), row_map(col_r)), pl.BlockSpec((c, RWKV_W), row_map(col_r + 1)),
                pl.BlockSpec((c, RWKV_W), row_map(col_r + 2)), pl.BlockSpec((c, LORA_COLS), row_map(col_lo)),
                state_spec] + _rwkv_pre_weight_specs(lambda b, ci: (0, 0))
    vec_spec = pl.BlockSpec((None, RWKV_W, c), lambda b, ci: (b, 0, ci))
    vec_shape = jax.ShapeDtypeStruct((n_batch, RWKV_W, t), F32)
    return pl.pallas_call(
        _rwkv_pre_body,
        grid=(n_batch, n_chunks),
        in_specs=in_specs,
        out_specs=[vec_spec] * 5 + [pl.BlockSpec((c, RWKV_W), lambda b, ci: (b * n_chunks + ci, 0)), state_spec],
        out_shape=[vec_shape] * 5 + [jax.ShapeDtypeStruct((n_batch * t, RWKV_W), F32),
                                     jax.ShapeDtypeStruct((n_batch, 1, N_RWKV_COLS), F32)],
        scratch_shapes=[pltpu.VMEM((1, N_RWKV_COLS), F32)],
        compiler_params=_params("parallel", "arbitrary"),
    )(proj, proj, proj, proj, *_rwkv_pre_args(s_shift, n_batch, mu, w0, w2, a0, a2, g2))


def _rwkv_pre_short_body(r_ref, k_ref, v_ref, lo_ref, shift_ref, mu_ref, w0_ref, w2_ref, a0_ref, a2_ref, g2_ref,
                         ro_ref, ko_ref, vo_ref, wo_ref, ao_ref, go_ref, so_ref, *, n_b, t):
    rows = n_b * t
    first_tok = (lax.broadcasted_iota(jnp.int32, (rows, 1), 0) & (t - 1)) == 0

    def shifted(x_ref, col0):
        w = x_ref.shape[1]
        x = x_ref[...].astype(F32)
        carried = jnp.broadcast_to(shift_ref[:, :, col0:col0 + w].astype(F32), (n_b, t, w)).reshape(rows, w)
        prev = jnp.where(first_tok, carried, pltpu.roll(x, 1, 0))
        so_ref[:, :, col0:col0 + w] = x.reshape(n_b, t, w)[:, t - 1:t, :]
        return x + (prev - x) * mu_ref[:, col0:col0 + w]

    def put(o_ref, x):
        by_tok = jnp.swapaxes(x.reshape(n_b, t, RWKV_W), 0, 1)
        for ti in range(t):
            feat = by_tok[ti].T.reshape(RWKV_N, RWKV_HEADS, n_b)
            o_ref[:, ti] = jnp.swapaxes(feat, 0, 1)

    put(ro_ref, shifted(r_ref, 0))
    put(ko_ref, shifted(k_ref, RWKV_W))
    put(vo_ref, shifted(v_ref, 2 * RWKV_W))
    decay, rate, gate = _rwkv_lora_terms(shifted(lo_ref, 3 * RWKV_W), w0_ref, w2_ref, a0_ref, a2_ref, g2_ref)
    put(wo_ref, decay)
    put(ao_ref, rate)
    go_ref[...] = gate


def _rwkv_pre_short(proj, s_shift, mu, w0, w2, a0, a2, g2, *, row0, n_batch, t):
    rows = n_batch * t
    assert row0 % rows == 0 and t & (t - 1) == 0
    blk0 = row0 // rows
    col_r = N_RET_COLS // RWKV_W
    col_lo = (N_RET_COLS + 3 * RWKV_W) // LORA_COLS
    state_spec = pl.BlockSpec((n_batch, 1, N_RWKV_COLS), lambda i: (0, 0, 0))
    in_specs = [pl.BlockSpec((rows, RWKV_W), lambda i: (blk0, col_r)),
                pl.BlockSpec((rows, RWKV_W), lambda i: (blk0, col_r + 1)),
                pl.BlockSpec((rows, RWKV_W), lambda i: (blk0, col_r + 2)),
                pl.BlockSpec((rows, LORA_COLS), lambda i: (blk0, col_lo)),
                state_spec] + _rwkv_pre_weight_specs(lambda i: (0, 0))
    vec_shape = (RWKV_HEADS, t, RWKV_N, n_batch)
    vec_spec = pl.BlockSpec(vec_shape, lambda i: (0, 0, 0, 0))
    return pl.pallas_call(
        functools.partial(_rwkv_pre_short_body, n_b=n_batch, t=t),
        grid=(1,),
        in_specs=in_specs,
        out_specs=[vec_spec] * 5 + [pl.BlockSpec((rows, RWKV_W), lambda i: (0, 0)), state_spec],
        out_shape=[jax.ShapeDtypeStruct(vec_shape, F32)] * 5 + [
            jax.ShapeDtypeStruct((rows, RWKV_W), F32), jax.ShapeDtypeStruct((n_batch, 1, N_RWKV_COLS), F32)],
        compiler_params=_params("arbitrary"),
    )(proj, proj, proj, proj, *_rwkv_pre_args(s_shift, n_batch, mu, w0, w2, a0, a2, g2))


def _scan_body(r_ref, k_ref, w_ref, a_ref, v_ref, kk_ref, ka_ref, rk_ref, lw_ref, lb_ref, s0_ref,
               y_ref, sout_ref, s_scr, a_scr, b_scr, km_scr, *, tc, vr, halves):
    ci = pl.program_id(1)

    @pl.when(ci == 0)
    def _():
        s_scr[...] = s0_ref[...].astype(F32)

    def ksum(x):
        return jnp.sum(x, axis=-2, keepdims=True)

    def vsum(x):
        if halves == 2:
            x2 = x.reshape(tc * vr, LANES)
            x = (x2 + pltpu.roll(x2, LANES // 2, 1)).reshape(tc, vr, LANES)
        return jnp.sum(x, axis=1, keepdims=True)

    kr = k_ref[...]
    a = a_ref[...]
    kk = kr * kk_ref[...]
    kk = kk / jnp.maximum(jnp.sqrt(ksum(kk * kk)), 1e-12)
    a_scr[...] = -kk
    b_scr[...] = kk * a
    km_scr[...] = kr * (1.0 + (a - 1.0) * ka_ref[...])

    def token(t, carry):
        r, w, avec, bvec, kmod = r_ref[t], w_ref[t], a_scr[t], b_scr[t], km_scr[t]

        def value_row(i, c2):
            s = s_scr[i]
            sa = ksum(s * avec)
            s = s * w + sa * bvec + v_ref[t, pl.ds(i, 1), :] * kmod
            s_scr[i] = s
            y_ref[t, pl.ds(i, 1), :] = ksum(s * r)
            return c2

        lax.fori_loop(0, vr, value_row, 0, unroll=16)
        return carry

    lax.fori_loop(0, tc, token, 0)

    y = y_ref[...]
    d = y - vsum(y) * (1.0 / RWKV_N)
    var = vsum(d * d) * (1.0 / RWKV_N)
    bonus = ksum(r_ref[...] * km_scr[...] * rk_ref[...])
    y_ref[...] = d * lax.rsqrt(var + LNX_EPS) * lw_ref[...] + lb_ref[...] + bonus * v_ref[...]

    @pl.when(ci == pl.num_programs(1) - 1)
    def _():
        sout_ref[...] = s_scr[...]


def _rwkv_scan(r, k, w, a, v, k_k, k_a, r_k, lnx_w, lnx_b, s0, *, tc, halves):
    n_grp, t, _, lanes = r.shape
    vr = v.shape[2]
    assert lanes == LANES and t % tc == 0 and vr * halves == RWKV_N

    def tok_spec(rows):
        return pl.BlockSpec((None, tc, rows, LANES), lambda g, ci: (g, ci, 0, 0))

    def par_spec(rows):
        return pl.BlockSpec((None, rows, LANES), lambda g, ci: (g, 0, 0))

    st_spec = pl.BlockSpec((None, vr, RWKV_N, LANES), lambda g, ci: (g, 0, 0, 0))
    key_scratch = pltpu.VMEM((tc, RWKV_N, LANES), F32)
    return pl.pallas_call(
        functools.partial(_scan_body, tc=tc, vr=vr, halves=halves),
        grid=(n_grp, t // tc),
        in_specs=[tok_spec(RWKV_N)] * 4 + [tok_spec(vr)] + [par_spec(RWKV_N)] * 3 + [par_spec(vr)] * 2 + [st_spec],
        out_specs=[tok_spec(vr), st_spec],
        out_shape=[jax.ShapeDtypeStruct((n_grp, t, vr, LANES), F32),
                   jax.ShapeDtypeStruct((n_grp, vr, RWKV_N, LANES), F32)],
        scratch_shapes=[pltpu.VMEM((vr, RWKV_N, LANES), F32), key_scratch, key_scratch, key_scratch],
        compiler_params=_params("parallel", "arbitrary"),
    )(r, k, w, a, v, k_k, k_a, r_k, lnx_w, lnx_b, s0)


SCAN_TC = 128
SCAN_SUB = 64
MEM_CHUNKS = 4


def _scan_prompt_body(r_ref, k_ref, w_ref, a_ref, v_ref, kk_ref, ka_ref, rk_ref, lw_ref, lb_ref, s0_ref,
                      memk_hbm, memv_hbm, y_ref, sout_ref, memk_out, memv_out,
                      s_scr, r_c, w_c, a_c, b_c, km_c, v_c, y_c, stash, mem_stage, mem_in_sem, mem_out_sem,
                      *, n_b, mem_seqs):
    ci = pl.program_id(0)
    vr = RWKV_N // 2
    ts = SCAN_SUB
    tile = RWKV_HEADS
    half_lanes = LANES // 2

    @pl.when(ci == 0)
    def _():
        s_scr[...] = s0_ref[...].astype(F32)

    chunk_seqs = mem_seqs // MEM_CHUNKS

    def mem_in(chunk, slot):
        copies = []
        for j in range(chunk_seqs):
            seq = (ci * MEM_CHUNKS + chunk) * chunk_seqs + j
            for h in range(MEM_HEADS):
                copies.append(pltpu.make_async_copy(memk_hbm.at[seq, :, h, :], mem_stage.at[slot, 0, j, h],
                                                    mem_in_sem.at[slot]))
                copies.append(pltpu.make_async_copy(memv_hbm.at[seq, :, h, :], mem_stage.at[slot, 1, j, h],
                                                    mem_in_sem.at[slot]))
        return copies

    def mem_out(chunk, slot):
        seqs = pl.ds((ci * MEM_CHUNKS + chunk) * chunk_seqs, chunk_seqs)
        return [pltpu.make_async_copy(mem_stage.at[slot, 0], memk_out.at[seqs], mem_out_sem.at[slot]),
                pltpu.make_async_copy(mem_stage.at[slot, 1], memv_out.at[seqs], mem_out_sem.at[slot])]

    def mem_phase(p):
        if 1 <= p <= MEM_CHUNKS:
            for cp in mem_in(p - 1, (p - 1) % 2):
                cp.wait()
            for cp in mem_out(p - 1, (p - 1) % 2):
                cp.start()
        if 2 <= p <= MEM_CHUNKS + 1:
            for cp in mem_out(p - 2, p % 2):
                cp.wait()
        if p < MEM_CHUNKS:
            for cp in mem_in(p, p % 2):
                cp.start()

    mem_phase(0)

    low = lax.broadcasted_iota(jnp.int32, (ts, LANES), 1) < half_lanes

    def feature_pair_rows(x_ref, base):
        tiles = [x_ref[b, pl.ds(base + f * tile, tile), :] for f in range(2) for b in range(n_b)]
        return jnp.concatenate(tiles, axis=0).T

    def key_to_chain(x_ref, dst, stash, t0):
        def group(g, c):
            rows = []
            for j in range(4):
                pair = g * 4 + j
                if t0 == 0:
                    full = feature_pair_rows(x_ref, pl.multiple_of(pair * 2 * tile, 2 * tile))
                    mt = full[:ts]
                    stash[pair] = full[ts:]
                else:
                    mt = stash[pair]
                sw = pltpu.roll(mt, half_lanes, 1)
                rows += [jnp.where(low, mt, sw), jnp.where(low, sw, mt)]
            dst[:, pl.ds(pl.multiple_of(g * 8, 8), 8), :] = jnp.swapaxes(jnp.stack(rows, axis=0), 0, 1)
            return c
        lax.fori_loop(0, RWKV_N // 8, group, 0, unroll=4)

    def value_to_chain(g, c):
        rows = [feature_pair_rows(v_ref, pl.multiple_of((g * 8 + j) * 2 * tile, 2 * tile)) for j in range(8)]
        v_c[:, pl.ds(pl.multiple_of(g * 8, 8), 8), :] = jnp.swapaxes(jnp.stack(rows, axis=0), 0, 1)
        return c
    lax.fori_loop(0, vr // 8, value_to_chain, 0, unroll=2)

    def ksum(x):
        return jnp.sum(x, axis=-2, keepdims=True)

    for t0 in range(0, SCAN_TC, ts):
        key_to_chain(r_ref, r_c, stash.at[0], t0)
        key_to_chain(w_ref, w_c, stash.at[1], t0)
        key_to_chain(k_ref, km_c, stash.at[2], t0)
        key_to_chain(a_ref, b_c, stash.at[3], t0)

        def prep(g8, c):
            toks = pl.ds(pl.multiple_of(g8 * 8, 8), 8)
            kr = km_c[toks]
            a = b_c[toks]
            kk = kr * kk_ref[...]
            kk = kk / jnp.maximum(jnp.sqrt(ksum(kk * kk)), 1e-12)
            a_c[toks] = -kk
            b_c[toks] = kk * a
            km_c[toks] = kr * (1.0 + (a - 1.0) * ka_ref[...])
            return c
        lax.fori_loop(0, ts // 8, prep, 0, unroll=2)
        mem_phase(1 + 2 * (t0 // ts))

        def token(t, carry):
            r, w, avec, bvec, kmod = r_c[t], w_c[t], a_c[t], b_c[t], km_c[t]

            def value_row(i, c2):
                s = s_scr[i]
                sa = ksum(s * avec)
                s = s * w + sa * bvec + v_c[t0 + t, pl.ds(i, 1), :] * kmod
                s_scr[i] = s
                y_c[t0 + t, pl.ds(i, 1), :] = ksum(s * r)
                return c2

            lax.fori_loop(0, vr, value_row, 0, unroll=True)
            return carry

        lax.fori_loop(0, ts, token, 0)
        mem_phase(2 + 2 * (t0 // ts))

        def post(g8, c):
            ktoks = pl.ds(pl.multiple_of(g8 * 8, 8), 8)
            vtoks = pl.ds(pl.multiple_of(t0 + g8 * 8, 8), 8)

            def vsum(x):
                x2 = x.reshape(8 * vr, LANES)
                x2 = x2 + pltpu.roll(x2, half_lanes, 1)
                return jnp.sum(x2.reshape(8, vr, LANES), axis=1, keepdims=True)

            y = y_c[vtoks]
            d = y - vsum(y) * (1.0 / RWKV_N)
            var = vsum(d * d) * (1.0 / RWKV_N)
            bonus = ksum(r_c[ktoks] * km_c[ktoks] * rk_ref[...])
            y_c[vtoks] = d * lax.rsqrt(var + LNX_EPS) * lw_ref[...] + lb_ref[...] + bonus * v_c[vtoks]
            return c
        lax.fori_loop(0, ts // 8, post, 0, unroll=4)

    def value_from_chain(g, c):
        blk = jnp.swapaxes(y_c[:, pl.ds(pl.multiple_of(g * 8, 8), 8), :], 0, 1)
        for j in range(8):
            mt = blk[j].T
            base = pl.multiple_of((g * 8 + j) * 2 * tile, 2 * tile)
            for hf in range(2):
                for b in range(n_b):
                    row0 = (hf * n_b + b) * tile
                    y_ref[b, pl.ds(base + hf * tile, tile), :] = mt[row0:row0 + tile, :]
        return c
    lax.fori_loop(0, vr // 8, value_from_chain, 0, unroll=2)

    assert 2 * (SCAN_TC // ts) == MEM_CHUNKS
    mem_phase(MEM_CHUNKS + 1)

    @pl.when(ci == pl.num_programs(0) - 1)
    def _():
        sout_ref[...] = s_scr[...]


def _rwkv_scan_prompt(r, k, w, a, v, k_k, k_a, r_k, lnx_w, lnx_b, s0, mem_k, mem_v):
    n_b, _, t = r.shape
    vr = RWKV_N // 2
    n_steps = t // SCAN_TC
    n_mem_seq = mem_k.shape[0]
    assert t % SCAN_TC == 0 and 2 * n_b * RWKV_HEADS == LANES and n_mem_seq % (n_steps * MEM_CHUNKS) == 0
    chunk_seqs = n_mem_seq // (n_steps * MEM_CHUNKS)
    any_spec = pl.BlockSpec(memory_space=pl.ANY)
    mem_shape = jax.ShapeDtypeStruct((n_mem_seq, MEM_HEADS, N_MEM, MEM_DH), F32)
    tok_spec = pl.BlockSpec((n_b, RWKV_W, SCAN_TC), lambda ci: (0, 0, ci))
    key_par = pl.BlockSpec((RWKV_N, LANES), lambda ci: (0, 0))
    val_par = pl.BlockSpec((vr, LANES), lambda ci: (0, 0))
    st_spec = pl.BlockSpec((vr, RWKV_N, LANES), lambda ci: (0, 0, 0))
    key_chain = pltpu.VMEM((SCAN_SUB, RWKV_N, LANES), F32)
    val_chain = pltpu.VMEM((SCAN_TC, vr, LANES), F32)
    return pl.pallas_call(
        functools.partial(_scan_prompt_body, n_b=n_b, mem_seqs=n_mem_seq // n_steps),
        grid=(n_steps,),
        in_specs=[tok_spec] * 5 + [key_par] * 3 + [val_par] * 2 + [st_spec, any_spec, any_spec],
        out_specs=[tok_spec, st_spec, any_spec, any_spec],
        out_shape=[jax.ShapeDtypeStruct((n_b, RWKV_W, t), F32),
                   jax.ShapeDtypeStruct((vr, RWKV_N, LANES), F32), mem_shape, mem_shape],
        scratch_shapes=([pltpu.VMEM((vr, RWKV_N, LANES), F32)] + [key_chain] * 5 + [val_chain] * 2
                        + [pltpu.VMEM((4, RWKV_N // 2, SCAN_TC - SCAN_SUB, LANES), F32),
                           pltpu.VMEM((2, 2, chunk_seqs, MEM_HEADS, N_MEM, MEM_DH), F32),
                           pltpu.SemaphoreType.DMA((2,)), pltpu.SemaphoreType.DMA((2,))]),
        compiler_params=_params("arbitrary"),
    )(r, k, w, a, v, k_k, k_a, r_k, lnx_w, lnx_b, s0, mem_k, mem_v)


def _merge_body(*refs, part_counts):
    n = len(part_counts)
    x_refs, oret_refs, g_refs = refs[:n], refs[n:2 * n], refs[2 * n:3 * n]
    yt_ref, ys_ref, wt_ref, wb_ref, gq_ref, wq_ref, o_ref, q_ref = refs[3 * n:]
    x = _read_row_parts(x_refs, part_counts)
    y = jnp.where(pl.program_id(0) >= part_counts[0], ys_ref[...], yt_ref[...].T)
    yb = (y * _read_row_parts(g_refs, part_counts)).astype(BF16)
    ob = _read_row_parts(oret_refs, part_counts).astype(BF16)
    n_chunk = 256
    for j in range(0, D_MODEL, n_chunk):
        acc = jnp.dot(ob, wt_ref[:, j:j + n_chunk], preferred_element_type=F32)
        acc = acc + jnp.dot(yb, wb_ref[:, j:j + n_chunk], preferred_element_type=F32)
        o_ref[:, j:j + n_chunk] = x[:, j:j + n_chunk] + acc
    h = o_ref[...]
    hb = (h * lax.rsqrt(jnp.mean(h * h, axis=-1, keepdims=True) + EPS) * gq_ref[...]).astype(BF16)
    for j in range(0, D_MODEL, n_chunk):
        q_ref[:, j:j + n_chunk] = jnp.dot(hb, wq_ref[:, j:j + n_chunk], preferred_element_type=F32)


def _merge(x_parts, oret_parts, g_parts, y_first_t, y_second, w_ret, w_rwkv, gain_q, w_q):
    m = sum(part.shape[0] for part in x_parts)
    in_specs, part_counts = [], None
    for parts in (x_parts, oret_parts, g_parts):
        specs, part_counts = _row_part_specs(parts, TM)
        in_specs += specs
    assert len(part_counts) == 2
    n_first = part_counts[0]
    tiles = y_first_t.shape[2] // TM
    assert y_first_t.shape[0] * tiles == n_first
    yt_spec = pl.BlockSpec((None, RWKV_W, TM),
                           lambda i: (jnp.minimum(i, n_first - 1) // tiles, 0, jnp.minimum(i, n_first - 1) % tiles))
    ys_spec = pl.BlockSpec((TM, RWKV_W), lambda i: (jnp.clip(i - n_first, 0, part_counts[1] - 1), 0))
    wspec = pl.BlockSpec((RET_W, D_MODEL), lambda i: (0, 0))
    row_spec = pl.BlockSpec((TM, D_MODEL), lambda i: (i, 0))
    row_shape = jax.ShapeDtypeStruct((m, D_MODEL), F32)
    return pl.pallas_call(
        functools.partial(_merge_body, part_counts=part_counts),
        grid=(m // TM,),
        in_specs=in_specs + [yt_spec, ys_spec, wspec, wspec, pl.BlockSpec((1, D_MODEL), lambda i: (0, 0)),
                             pl.BlockSpec((D_MODEL, D_MODEL), lambda i: (0, 0))],
        out_specs=[row_spec, row_spec],
        out_shape=[row_shape, row_shape],
        compiler_params=_params("parallel"),
    )(*x_parts, *oret_parts, *g_parts, y_first_t, y_second, w_ret.astype(BF16), w_rwkv.astype(BF16),
      gain_q.reshape(1, D_MODEL), w_q.astype(BF16))


def _mem_kv_body(x_ref, g_ref, wk_ref, wv_ref, k_ref, v_ref, kh_ref, vh_ref, *, n_seq):
    x = x_ref[...].astype(F32)
    xb = (x * lax.rsqrt(jnp.mean(x * x, axis=-1, keepdims=True) + EPS) * g_ref[...]).astype(BF16)
    for w_ref, o_ref, oh_ref in ((wk_ref, k_ref, kh_ref), (wv_ref, v_ref, vh_ref)):
        for h in range(MEM_HEADS):
            acc = jnp.dot(xb, w_ref[:, h * MEM_DH:(h + 1) * MEM_DH], preferred_element_type=F32)
            o_ref[:, h, :] = acc
            for s in range(n_seq):
                oh_ref[s, h] = acc[s * N_MEM:(s + 1) * N_MEM]


def _mem_kv(mem, gain, w_k, w_v):
    n_b = mem.shape[0]
    n_seq = TM // N_MEM
    assert n_seq * N_MEM == TM and n_b % n_seq == 0
    wspec = pl.BlockSpec((D_MODEL, D_MODEL), lambda i: (0, 0))
    tok_spec = pl.BlockSpec((TM, MEM_HEADS, MEM_DH), lambda i: (i, 0, 0))
    head_spec = pl.BlockSpec((n_seq, MEM_HEADS, N_MEM, MEM_DH), lambda i: (i, 0, 0, 0))
    tok_shape = jax.ShapeDtypeStruct((n_b * N_MEM, MEM_HEADS, MEM_DH), F32)
    head_shape = jax.ShapeDtypeStruct((n_b, MEM_HEADS, N_MEM, MEM_DH), F32)
    return pl.pallas_call(
        functools.partial(_mem_kv_body, n_seq=n_seq),
        grid=(n_b // n_seq,),
        in_specs=[pl.BlockSpec((TM, D_MODEL), lambda i: (i, 0)), pl.BlockSpec((1, D_MODEL), lambda i: (0, 0)),
                  wspec, wspec],
        out_specs=[tok_spec, tok_spec, head_spec, head_spec],
        out_shape=[tok_shape, tok_shape, head_shape, head_shape],
        compiler_params=_params("parallel"),
    )(mem.reshape(n_b * N_MEM, D_MODEL), gain.reshape(1, D_MODEL), w_k.astype(BF16), w_v.astype(BF16))


def _attn_body(q_ref, k_ref, v_ref, o_ref, *, n_seq, tq):
    nt = (((1,), (1,)), ((), ()))
    for g in range(n_seq):
        rows = slice(g * tq, (g + 1) * tq)
        q = q_ref[rows, :].astype(BF16)
        s = lax.dot_general(q, k_ref[g].astype(BF16), nt, preferred_element_type=F32) * (MEM_DH ** -0.5)
        p = jnp.exp(s - jnp.max(s, axis=-1, keepdims=True))
        l = jnp.sum(p, axis=-1, keepdims=True)
        o = jnp.dot(p.astype(BF16), v_ref[g].astype(BF16), preferred_element_type=F32)
        o_ref[rows, :] = o / l


def _attention(q, mem_k, mem_v, *, row0, n_batch, t, n_seq, tq):
    q_tiles = t // tq
    rows = n_seq * tq
    assert t % tq == 0 and n_batch % n_seq == 0 and row0 % rows == 0 and (n_seq == 1 or q_tiles == 1)
    blk0 = row0 // rows
    kv_spec = pl.BlockSpec((n_seq, None, N_MEM, MEM_DH), lambda b, h, qi: (b, h, 0, 0))
    return pl.pallas_call(
        functools.partial(_attn_body, n_seq=n_seq, tq=tq),
        grid=(n_batch // n_seq, MEM_HEADS, q_tiles),
        in_specs=[pl.BlockSpec((rows, MEM_DH), lambda b, h, qi: (blk0 + b * q_tiles + qi, h)), kv_spec, kv_spec],
        out_specs=pl.BlockSpec((rows, MEM_DH), lambda b, h, qi: (b * q_tiles + qi, h)),
        out_shape=jax.ShapeDtypeStruct((n_batch * t, D_MODEL), F32),
        compiler_params=_params("parallel", "parallel", "parallel"),
    )(q, mem_k, mem_v)


ROW_TILE = (D_MODEL // LANES, LANES)


def _rows_to_tiles(x):
    chunks = [x[:, j * LANES:(j + 1) * LANES] for j in range(ROW_TILE[0])]
    return jnp.swapaxes(jnp.stack(chunks, axis=0), 0, 1)


def _tiles_to_rows(x):
    chunks = jnp.swapaxes(x, 0, 1)
    return jnp.concatenate([chunks[j] for j in range(ROW_TILE[0])], axis=1)


def _attn_out_body(*refs, part_counts):
    n = len(part_counts)
    w_ref, res_ref, o_ref = refs[n:]
    ab = _read_row_parts(refs[:n], part_counts).astype(BF16)
    n_chunk = 256
    for j in range(0, D_MODEL, n_chunk):
        o_ref[:, j:j + n_chunk] = res_ref[:, j:j + n_chunk] + jnp.dot(
            ab, w_ref[:, j:j + n_chunk], preferred_element_type=F32)


def _attn_out(att_parts, w_mo, residual):
    m = residual.shape[0]
    att_specs, part_counts = _row_part_specs(att_parts, TM)
    row = pl.BlockSpec((TM, D_MODEL), lambda i: (i, 0))
    return pl.pallas_call(
        functools.partial(_attn_out_body, part_counts=part_counts),
        grid=(m // TM,),
        in_specs=att_specs + [pl.BlockSpec((D_MODEL, D_MODEL), lambda i: (0, 0)), row],
        out_specs=row,
        out_shape=jax.ShapeDtypeStruct((m, D_MODEL), F32),
        compiler_params=_params("parallel"),
    )(*att_parts, w_mo.astype(BF16), residual)


def _router_body(h_ref, g_ref, w_ref, b_ref, hn_ref, ids_ref, comb_ref):
    x = h_ref[...]
    hn = x * lax.rsqrt(jnp.mean(x * x, axis=-1, keepdims=True) + EPS) * g_ref[...]
    hn_ref[...] = _rows_to_tiles(hn)
    logits = jnp.dot(hn, w_ref[...], precision=lax.Precision.HIGHEST, preferred_element_type=F32) + b_ref[...]
    lane = lax.broadcasted_iota(jnp.int32, logits.shape, 1).astype(F32)
    neg = -jnp.inf

    def first_argmax(vals):
        m = jnp.max(vals, axis=-1, keepdims=True)
        return m, jnp.min(jnp.where(vals == m, lane, float(LANES)), axis=-1, keepdims=True)

    gl = jnp.where(lane < N_GROUPS, logits, neg)
    gmax, gsel = first_argmax(gl)
    pg_sel = 1.0 / jnp.sum(jnp.exp(gl - gmax), axis=-1, keepdims=True)
    e0 = N_GROUPS + gsel * EXP_PER_GROUP
    el = jnp.where((lane >= e0) & (lane < e0 + EXP_PER_GROUP), logits, neg)
    m1, i1 = first_argmax(el)
    m2, i2 = first_argmax(jnp.where(lane == i1, neg, el))
    e21 = jnp.exp(m2 - m1)
    c1 = pg_sel / (1.0 + e21)
    c2 = c1 * e21
    ids = jnp.where(lane == 0, i1 - N_GROUPS, jnp.where(lane == 1, i2 - N_GROUPS, 0.0))
    ids_ref[...] = ids.astype(jnp.int32)
    comb_ref[...] = jnp.where(lane == 0, c1, jnp.where(lane == 1, c2, 0.0))


def _router(h, g_ffn, w_gr, b_gr, w_er, b_er):
    m = h.shape[0]
    pad = LANES - N_GROUPS - N_EXPERTS
    w = jnp.concatenate([w_gr, w_er, jnp.zeros((D_MODEL, pad), F32)], axis=1)
    b = jnp.concatenate([b_gr, b_er, jnp.zeros((pad,), F32)]).reshape(1, LANES)
    row = lambda n: pl.BlockSpec((TM, n), lambda i: (i, 0))
    return pl.pallas_call(
        _router_body,
        grid=(m // TM,),
        in_specs=[row(D_MODEL), pl.BlockSpec((1, D_MODEL), lambda i: (0, 0)),
                  pl.BlockSpec((D_MODEL, LANES), lambda i: (0, 0)), pl.BlockSpec((1, LANES), lambda i: (0, 0))],
        out_specs=[pl.BlockSpec((TM,) + ROW_TILE, lambda i: (i, 0, 0)), row(LANES), row(LANES)],
        out_shape=[jax.ShapeDtypeStruct((m,) + ROW_TILE, F32), jax.ShapeDtypeStruct((m, LANES), jnp.int32),
                   jax.ShapeDtypeStruct((m, LANES), F32)],
        compiler_params=_params("parallel"),
    )(h, g_ffn.reshape(1, D_MODEL), w, b)


def _dispatch_body(dest_ref, hn_ref, sorted_in, sorted_out, sem):
    del sorted_in

    def issue(r, c):
        for k in range(TOP_K):
            pltpu.make_async_copy(hn_ref.at[r], sorted_out.at[dest_ref[0, k, r]], sem.at[k]).start(priority=k)
        return c
    lax.fori_loop(0, TM, issue, 0, unroll=8)
    for k in range(TOP_K):
        pltpu.make_async_copy(hn_ref, sorted_out.at[pl.ds(0, TM)], sem.at[k]).wait()


def _dispatch(hn, dest, n_sorted):
    n_tok = hn.shape[0]
    return pl.pallas_call(
        _dispatch_body,
        grid=(n_tok // TM,),
        in_specs=[pl.BlockSpec((1, TOP_K, TM), lambda i: (i, 0, 0), memory_space=pltpu.SMEM),
                  pl.BlockSpec((TM,) + ROW_TILE, lambda i: (i, 0, 0)),
                  pl.BlockSpec(memory_space=pl.ANY)],
        out_specs=pl.BlockSpec(memory_space=pl.ANY),
        out_shape=jax.ShapeDtypeStruct((n_sorted,) + ROW_TILE, F32),
        scratch_shapes=[pltpu.SemaphoreType.DMA((TOP_K,))],
        input_output_aliases={2: 0},
        compiler_params=_params("arbitrary"),
    )(dest, hn, jnp.zeros((n_sorted,) + ROW_TILE, F32))


def _expert_body(blk_e_ref, x_ref, wg_ref, wu_ref, wd_ref, o_ref):
    del blk_e_ref
    x = _tiles_to_rows(x_ref[...]).astype(BF16)
    hg = jnp.dot(x, wg_ref[0].astype(BF16), preferred_element_type=F32)
    hu = jnp.dot(x, wu_ref[0].astype(BF16), preferred_element_type=F32)
    act = (hg * jax.nn.sigmoid(hg) * hu).astype(BF16)
    o_ref[...] = _rows_to_tiles(jnp.dot(act, wd_ref[0].astype(BF16), preferred_element_type=F32))


def _experts(x_sorted, blk_e, w_gate, w_up, w_down):
    n_blocks = blk_e.shape[0]
    row_spec = pl.BlockSpec((MOE_ROWS,) + ROW_TILE, lambda i, be: (i, 0, 0))
    grid_spec = pltpu.PrefetchScalarGridSpec(
        num_scalar_prefetch=1,
        grid=(n_blocks,),
        in_specs=[
            row_spec,
            pl.BlockSpec((1, D_MODEL, D_EXPERT), lambda i, be: (be[i], 0, 0)),
            pl.BlockSpec((1, D_MODEL, D_EXPERT), lambda i, be: (be[i], 0, 0)),
            pl.BlockSpec((1, D_EXPERT, D_MODEL), lambda i, be: (be[i], 0, 0)),
        ],
        out_specs=row_spec,
    )
    return pl.pallas_call(
        _expert_body,
        grid_spec=grid_spec,
        out_shape=jax.ShapeDtypeStruct(x_sorted.shape, F32),
        compiler_params=_params("arbitrary"),
    )(blk_e, x_sorted, w_gate, w_up, w_down)


def _route_plan(ids):
    n_tok = ids.shape[0]
    n_pairs = ids.size
    n_blocks = -(-(n_pairs + N_EXPERTS * (MOE_ROWS - 1)) // MOE_ROWS)
    flat_e = ids.reshape(n_pairs)
    onehot = (flat_e[:, None] == jnp.arange(N_EXPERTS, dtype=jnp.int32)[None, :]).astype(jnp.int32)
    csum = jnp.cumsum(onehot, axis=0)
    rank = jnp.sum(onehot * csum, axis=1) - 1
    counts = csum[-1]
    pcounts = (counts + MOE_ROWS - 1) // MOE_ROWS * MOE_ROWS
    pends = jnp.cumsum(pcounts)
    pstarts = pends - pcounts
    dest = jnp.sum(onehot * pstarts[None, :], axis=1) + rank
    block_start = jnp.arange(n_blocks, dtype=jnp.int32) * MOE_ROWS
    blk_e = jnp.minimum(jnp.sum((block_start[:, None] >= pends[None, :]).astype(jnp.int32), axis=1),
                        N_EXPERTS - 1).astype(jnp.int32)
    dest = dest.astype(jnp.int32).reshape(n_tok // TM, TM, TOP_K).transpose(0, 2, 1)
    return blk_e, dest, n_blocks * MOE_ROWS


def _final_body(dest_ref, dest_next_ref, h_ref, comb_ref, g_ref, y_hbm, o_first, o_second, ybuf, sem, *, n_first):
    i = pl.program_id(0)
    n = pl.num_programs(0)
    slot = i % 2

    def start_gather(ref, sl):
        def issue(r, c):
            for k in range(TOP_K):
                pltpu.make_async_copy(y_hbm.at[ref[0, k, r]], ybuf.at[sl, k, r], sem.at[sl]).start(priority=k)
            return c
        lax.fori_loop(0, TM, issue, 0, unroll=8)

    @pl.when(i == 0)
    def _():
        start_gather(dest_ref, 0)

    @pl.when(i + 1 < n)
    def _():
        start_gather(dest_next_ref, 1 - slot)

    for k in range(TOP_K):
        pltpu.make_async_copy(y_hbm.at[pl.ds(0, TM)], ybuf.at[slot, k], sem.at[slot]).wait()

    first, second = _tiles_to_rows(ybuf[slot, 0]), _tiles_to_rows(ybuf[slot, 1])
    x = h_ref[...] + (first * comb_ref[:, 0:1] + second * comb_ref[:, 1:2])
    out = x * lax.rsqrt(jnp.mean(x * x, axis=-1, keepdims=True) + EPS) * g_ref[...]

    @pl.when(i < n_first)
    def _():
        o_first[...] = out

    @pl.when(i >= n_first)
    def _():
        o_second[...] = out


def _final(h, y_sorted, dest, comb, g_final, *, n_first_rows):
    n_rows = h.shape[0]
    assert n_rows % TM == 0 and n_first_rows % TM == 0 and 0 < n_first_rows < n_rows
    n_steps, n_first = n_rows // TM, n_first_rows // TM
    row = lambda n: pl.BlockSpec((TM, n), lambda i: (i, 0))
    dest_spec = lambda f: pl.BlockSpec((1, TOP_K, TM), f, memory_space=pltpu.SMEM)
    return pl.pallas_call(
        functools.partial(_final_body, n_first=n_first),
        grid=(n_steps,),
        in_specs=[dest_spec(lambda i: (i, 0, 0)),
                  dest_spec(lambda i: (jnp.minimum(i + 1, n_steps - 1), 0, 0)),
                  row(D_MODEL), row(LANES), pl.BlockSpec((1, D_MODEL), lambda i: (0, 0)),
                  pl.BlockSpec(memory_space=pl.ANY)],
        out_specs=[pl.BlockSpec((TM, D_MODEL), lambda i: (jnp.minimum(i, n_first - 1), 0)),
                   pl.BlockSpec((TM, D_MODEL), lambda i: (jnp.maximum(i - n_first, 0), 0))],
        out_shape=[jax.ShapeDtypeStruct((n_first_rows, D_MODEL), F32),
                   jax.ShapeDtypeStruct((n_rows - n_first_rows, D_MODEL), F32)],
        scratch_shapes=[pltpu.VMEM((2, TOP_K, TM) + ROW_TILE, F32), pltpu.SemaphoreType.DMA((2,))],
        compiler_params=_params("arbitrary"),
    )(dest, dest, h, comb, g_final.reshape(1, D_MODEL), y_sorted)


def _reorder_last(x, shape, order):
    lead = x.shape[:-1]
    n = len(lead)
    y = x.reshape(lead + shape).transpose(tuple(range(n)) + tuple(n + o for o in order))
    return y.reshape(lead + (x.shape[-1],))


HALF_N = RWKV_N // 2


def _key_major(x):
    return _reorder_last(x, (RWKV_HEADS, RWKV_N), (1, 0))


def _key_major_inv(x):
    return _reorder_last(x, (RWKV_N, RWKV_HEADS), (1, 0))


def _value_major(x):
    return _reorder_last(x, (RWKV_HEADS, 2, HALF_N), (2, 1, 0))


def _value_major_inv(x):
    return _reorder_last(x, (HALF_N, 2, RWKV_HEADS), (2, 1, 0))


def _rwkv_cols(x, key_fn, value_fn):
    return jnp.concatenate([key_fn(x[..., :RWKV_W]), key_fn(x[..., RWKV_W:2 * RWKV_W]),
                            value_fn(x[..., 2 * RWKV_W:3 * RWKV_W]), x[..., 3 * RWKV_W:]], axis=-1)


def kernel(x_prompt, x_sample, mem_prompt, state_ret, state_rwkv, state_shift, cache_mem_k, cache_mem_v,
           g_mix, w_in, ret_gn, rwkv_mu, rwkv_w0, rwkv_w2, rwkv_a0, rwkv_a2, rwkv_g2, rwkv_k_k, rwkv_k_a,
           rwkv_r_k, rwkv_lnx_w, rwkv_lnx_b, w_out, g_mem_q, g_mem_kv, w_mq, w_mk, w_mv, w_mo, g_ffn,
           w_group_router, b_group_router, w_expert_router, b_expert_router, w_e_gate, w_e_up, w_e_down,
           g_final):
    assert w_in.shape[0] == 1, "single-layer decoder"
    bp, tp, d = x_prompt.shape
    bs, ts, _ = x_sample.shape
    np_tok, ns_tok = bp * tp, bs * ts
    assert d == D_MODEL and bp * RWKV_HEADS * 2 == LANES and bs == LANES
    l = 0
    x_parts = [x_prompt.reshape(np_tok, d), x_sample.reshape(ns_tok, d)]

    w_in_l = jnp.concatenate([w_in[l][:, :N_RET_COLS], _rwkv_cols(w_in[l][:, N_RET_COLS:], _key_major, _value_major)],
                             axis=1)
    proj = _input_projection(x_parts, g_mix[l], w_in_l)

    pos_p = np.arange(tp)
    pos_s = PAST_LEN + np.arange(ts)
    zero_ret = jnp.zeros((bp, RET_HEADS, RET_DK, RET_DV), F32)
    oret_p, sret_p = _retention(proj, zero_ret, ret_gn[l], pos_p, row0=0, n_batch=bp, t=tp, n_blk=8, per_blk=1)
    oret_s, sret_s = _retention(proj, state_ret[l], ret_gn[l], pos_s, row0=np_tok, n_batch=bs, t=ts,
                                n_blk=1, per_blk=16)
    oret_p, oret_s = oret_p.reshape(np_tok, RET_W), oret_s.reshape(ns_tok, RET_W)

    pre_w = (_rwkv_cols(rwkv_mu[l], _key_major, _value_major), _key_major(rwkv_w0[l]), _key_major(rwkv_w2[l]),
             _key_major(rwkv_a0[l]), _key_major(rwkv_a2[l]), _value_major(rwkv_g2[l]))
    zero_shift = jnp.zeros((bp, N_RWKV_COLS), F32)
    shift_in = _rwkv_cols(state_shift[l], _key_major, _value_major)
    r_p, k_p, v_p, w_p, a_p, gate_p, shift_p = _rwkv_pre(proj, zero_shift, *pre_w, row0=0, n_batch=bp, t=tp, c=512)
    r_s, k_s, v_s, w_s, a_s, gate_s, shift_s = _rwkv_pre_short(proj, shift_in, *pre_w, row0=np_tok, n_batch=bs,
                                                                t=ts)

    kvec = lambda v: v.reshape(RWKV_HEADS, RWKV_N)
    key_par = lambda v: jnp.broadcast_to(kvec(v).T[:, None, None, :], (RWKV_N, 2, bp, RWKV_HEADS)).reshape(
        RWKV_N, LANES)

    val_par = lambda v: jnp.broadcast_to(
        v.reshape(RWKV_HEADS, 2, HALF_N).transpose(2, 1, 0)[:, :, None, :],
        (HALF_N, 2, bp, RWKV_HEADS)).reshape(HALF_N, LANES)
    mem_shape = (N_MEM, MEM_HEADS, MEM_DH)
    y_p, srw_p, cache_k_heads, cache_v_heads = _rwkv_scan_prompt(
        r_p, k_p, w_p, a_p, v_p, key_par(rwkv_k_k[l]), key_par(rwkv_k_a[l]), key_par(rwkv_r_k[l]),
        val_par(rwkv_lnx_w[l]), val_par(rwkv_lnx_b[l]), jnp.zeros((HALF_N, RWKV_N, LANES), F32),
        cache_mem_k.reshape(bs, *mem_shape), cache_mem_v.reshape(bs, *mem_shape))
    srw_p = srw_p.reshape(HALF_N, RWKV_N, 2, bp, RWKV_HEADS).transpose(3, 4, 2, 0, 1).reshape(
        bp, RWKV_HEADS, RWKV_N, RWKV_N)
    head_par = lambda v: jnp.broadcast_to(kvec(v)[:, :, None], (RWKV_HEADS, RWKV_N, LANES))
    val_rows = lambda v: v.reshape(RWKV_HEADS, 2, HALF_N).transpose(0, 2, 1).reshape(RWKV_HEADS * RWKV_N)
    state_s = state_rwkv[l].astype(F32).reshape(bs, RWKV_HEADS, 2, HALF_N, RWKV_N).transpose(1, 3, 2, 4, 0)
    y_s, srw_s = _rwkv_scan(
        r_s, k_s, w_s, a_s, v_s, head_par(rwkv_k_k[l]), head_par(rwkv_k_a[l]), head_par(rwkv_r_k[l]),
        head_par(val_rows(rwkv_lnx_w[l])), head_par(val_rows(rwkv_lnx_b[l])),
        state_s.reshape(RWKV_HEADS, RWKV_N, RWKV_N, bs), tc=ts, halves=1)
    y_s = y_s.transpose(3, 1, 2, 0).reshape(ns_tok, RWKV_W)
    srw_s = srw_s.reshape(RWKV_HEADS, HALF_N, 2, RWKV_N, bs).transpose(4, 0, 2, 1, 3).reshape(
        bs, RWKV_HEADS, RWKV_N, RWKV_N)

    w_rwkv_out = _value_major(w_out[l][RET_W:].T).T
    h, q = _merge(x_parts, [oret_p, oret_s], [gate_p, gate_s], y_p, y_s, w_out[l][:RET_W], w_rwkv_out,
                  g_mem_q[l], w_mq[l])

    mk, mv, mk_heads, mv_heads = _mem_kv(mem_prompt, g_mem_kv[l], w_mk[l], w_mv[l])
    att_p = _attention(q, mk_heads, mv_heads, row0=0, n_batch=bp, t=tp, n_seq=1, tq=tp)
    att_s = _attention(q, cache_k_heads, cache_v_heads, row0=np_tok, n_batch=bs, t=ts, n_seq=32, tq=ts)

    h = _attn_out([att_p, att_s], w_mo[l], h)

    hn, ids, comb = _router(h, g_ffn[l], w_group_router[l], b_group_router[l], w_expert_router[l],
                            b_expert_router[l])
    blk_e, dest, n_sorted = _route_plan(ids[:, :TOP_K])
    y_sorted = _experts(_dispatch(hn, dest, n_sorted), blk_e, w_e_gate[l], w_e_up[l], w_e_down[l])
    y_prompt, y_sample = _final(h, y_sorted, dest, comb, g_final, n_first_rows=np_tok)
    y_prompt = y_prompt.reshape(bp, tp, d)
    y_sample = y_sample.reshape(bs, ts, d)

    shift_p = _rwkv_cols(shift_p.reshape(bp, N_RWKV_COLS), _key_major_inv, _value_major_inv)
    shift_s = _rwkv_cols(shift_s.reshape(bs, N_RWKV_COLS), _key_major_inv, _value_major_inv)
    return (y_prompt, y_sample, sret_p[None], srw_p[None], shift_p[None],
            mk.reshape(1, bp, *mem_shape), mv.reshape(1, bp, *mem_shape),
            sret_s[None], srw_s[None], shift_s[None])
```

```python
import functools

import numpy as np
import jax
import jax.numpy as jnp
from jax import lax
from jax.experimental import pallas as pl
from jax.experimental.pallas import tpu as pltpu

F32 = jnp.float32
BF16 = jnp.bfloat16

D_MODEL = 1024
PAST_LEN = 16384
N_MEM = 256
MEM_HEADS = 4
MEM_DH = D_MODEL // MEM_HEADS
RET_HEADS = 4
RET_W = D_MODEL // 2
RET_DV = RET_W // RET_HEADS
RET_DK = RET_DV // 2
RET_QK = RET_HEADS * RET_DK
RET_CHUNK = 128
ROPE_BASE = 10000.0
RWKV_N = 64
RWKV_W = D_MODEL - RET_W
RWKV_HEADS = RWKV_W // RWKV_N
LORA_W = 64
LORA_A = 64
LORA_G = 128
LNX_EPS = 64e-5
N_RET_COLS = 2 * RET_QK + 2 * RET_W
N_RWKV_COLS = 3 * RWKV_W + LORA_W + LORA_A + LORA_G
N_IN_COLS = N_RET_COLS + N_RWKV_COLS
N_GROUPS = 4
EXP_PER_GROUP = 8
N_EXPERTS = N_GROUPS * EXP_PER_GROUP
TOP_K = 2
D_EXPERT = D_MODEL // 2
EPS = 1e-6

LANES = 128
MOE_ROWS = 512
TM = 512


def _params(*sem):
    return pltpu.CompilerParams(dimension_semantics=sem)


def _row_part_specs(parts, tm):
    specs, counts, start = [], [], 0
    for part in parts:
        nb = part.shape[0] // tm
        assert nb * tm == part.shape[0]
        specs.append(pl.BlockSpec((tm, part.shape[1]), lambda i, s=start, n=nb: (jnp.clip(i - s, 0, n - 1), 0)))
        counts.append(nb)
        start += nb
    return specs, counts


def _read_row_parts(refs, counts):
    i = pl.program_id(0)
    x = refs[0][...]
    start = counts[0]
    for ref, nb in zip(refs[1:], counts[1:]):
        x = jnp.where(i >= start, ref[...], x)
        start += nb
    return x


def _proj_body(*refs, part_counts):
    n = len(part_counts)
    g_ref, w_ref, o_ref = refs[n:]
    x = _read_row_parts(refs[:n], part_counts).astype(F32)
    xb = (x * lax.rsqrt(jnp.mean(x * x, axis=-1, keepdims=True) + EPS) * g_ref[...]).astype(BF16)
    n_chunk = 256
    for j in range(0, w_ref.shape[1], n_chunk):
        o_ref[:, j:j + n_chunk] = jnp.dot(xb, w_ref[:, j:j + n_chunk], preferred_element_type=F32)


def _input_projection(x_parts, gain, w):
    k, n_out = w.shape
    m = sum(part.shape[0] for part in x_parts)
    assert n_out % 256 == 0
    in_specs, part_counts = _row_part_specs(x_parts, TM)
    return pl.pallas_call(
        functools.partial(_proj_body, part_counts=part_counts),
        grid=(m // TM,),
        in_specs=in_specs + [pl.BlockSpec((1, k), lambda i: (0, 0)), pl.BlockSpec((k, n_out), lambda i: (0, 0))],
        out_specs=pl.BlockSpec((TM, n_out), lambda i: (i, 0)),
        out_shape=jax.ShapeDtypeStruct((m, n_out), F32),
        compiler_params=_params("parallel"),
    )(*x_parts, gain.reshape(1, k).astype(F32), w.astype(BF16))


def _rot_tables(pos):
    half = RET_DK // 2
    inv_freq = ROPE_BASE ** (-(np.arange(half, dtype=np.float64) / half))
    ang = pos.astype(np.float64)[:, None] * inv_freq[None, :]
    cos, sin = np.cos(ang), np.sin(ang)
    zero = np.zeros_like(sin)
    c = np.tile(np.concatenate([cos, cos], axis=1), (1, RET_HEADS))
    s_lo = np.tile(np.concatenate([-sin, zero], axis=1), (1, RET_HEADS))
    s_hi = np.tile(np.concatenate([zero, sin], axis=1), (1, RET_HEADS))
    return [jnp.asarray(t, F32) for t in (c, s_lo, s_hi)]


def _ret_decay_tables(c):
    lg = np.log1p(-np.exp2(-5.0 - np.arange(RET_HEADS, dtype=np.float64)))
    idx = np.arange(c, dtype=np.float64)
    diff = idx[:, None] - idx[None, :]
    mask = np.where(diff[None] >= 0, np.exp(np.maximum(diff, 0.0)[None] * lg[:, None, None]), 0.0)
    q_dec = np.repeat(np.exp((idx[:, None] + 1.0) * lg[None, :]), RET_DV, axis=1)
    k_dec = np.repeat(np.exp((c - 1.0 - idx)[:, None] * lg[None, :]), RET_DK, axis=1)
    c_dec = [float(v) for v in np.exp(c * lg)]
    return jnp.asarray(mask, F32), jnp.asarray(q_dec, F32), jnp.asarray(k_dec, F32), c_dec


def _ret_body(*refs, n_blk, per_blk, c, c_dec):
    q_refs, k_refs, v_refs, gate_refs = (refs[j * n_blk:(j + 1) * n_blk] for j in range(4))
    (c_ref, slo_ref, shi_ref, mask_ref, qdec_ref, kdec_ref, gn_ref, s0_ref, o_ref, sout_ref,
     s_scr) = refs[4 * n_blk:]
    n_seq = n_blk * per_blk
    ci = pl.program_id(1)

    @pl.when(ci == 0)
    def _():
        s_scr[...] = s0_ref[...].astype(F32)

    cos, s_lo, s_hi = c_ref[...], slo_ref[...], shi_ref[...]
    half = RET_DK // 2

    def rope(x):
        return x * cos + pltpu.roll(x, RET_QK - half, 1) * s_lo + pltpu.roll(x, half, 1) * s_hi

    nt = (((1,), (1,)), ((), ()))
    tn = (((0,), (0,)), ((), ()))
    for g in range(n_seq):
        blk = g // per_blk
        rows = slice((g % per_blk) * c, (g % per_blk + 1) * c)
        q = rope(q_refs[blk][rows, :].astype(F32))
        k = rope(k_refs[blk][rows, :].astype(F32)) * (RET_DK ** -0.5)
        k_st = k * kdec_ref[...]
        for h in range(RET_HEADS):
            kc = slice(h * RET_DK, (h + 1) * RET_DK)
            vc = slice(h * RET_DV, (h + 1) * RET_DV)
            qh = q[:, kc].astype(BF16)
            vh = v_refs[blk][rows, vc].astype(BF16)
            s_h = s_scr[g, h]
            att = lax.dot_general(qh, k[:, kc].astype(BF16), nt, preferred_element_type=F32) * mask_ref[h]
            o = jnp.dot(att.astype(BF16), vh, preferred_element_type=F32)
            o = o + jnp.dot(qh, s_h.astype(BF16), preferred_element_type=F32) * qdec_ref[:, vc]
            s_scr[g, h] = s_h * c_dec[h] + lax.dot_general(
                k_st[:, kc].astype(BF16), vh, tn, preferred_element_type=F32)
            o = o * lax.rsqrt(jnp.mean(o * o, axis=-1, keepdims=True) + EPS)
            gate = gate_refs[blk][rows, vc].astype(F32)
            o_ref[g, :, vc] = o * gn_ref[:, vc] * (gate * jax.nn.sigmoid(gate))

    @pl.when(ci == pl.num_programs(1) - 1)
    def _():
        sout_ref[...] = s_scr[...]


def _retention(proj, s0, ret_gn, pos, *, row0, n_batch, t, n_blk, per_blk):
    c = RET_CHUNK if t % RET_CHUNK == 0 else t
    n_chunks = t // c
    rows = per_blk * c
    n_seq = n_blk * per_blk
    assert n_batch % n_seq == 0 and row0 % rows == 0 and (per_blk == 1 or n_chunks == 1)
    blk0 = row0 // rows
    mask, q_dec, k_dec, c_dec = _ret_decay_tables(c)
    cos, s_lo, s_hi = _rot_tables(pos)

    def const2(b, ci):
        return (0, 0)

    def row_specs(width, col):
        return [pl.BlockSpec((rows, width), lambda b, ci, j=j: (blk0 + (b * n_blk + j) * n_chunks + ci, col))
                for j in range(n_blk)]

    state_spec = pl.BlockSpec((n_seq, RET_HEADS, RET_DK, RET_DV), lambda b, ci: (b, 0, 0, 0))
    in_specs = row_specs(RET_QK, 0) + row_specs(RET_QK, 1) + row_specs(RET_W, 1) + row_specs(RET_W, 2) + [
        pl.BlockSpec((c, RET_QK), lambda b, ci: (ci, 0)),
        pl.BlockSpec((c, RET_QK), lambda b, ci: (ci, 0)),
        pl.BlockSpec((c, RET_QK), lambda b, ci: (ci, 0)),
        pl.BlockSpec((RET_HEADS, c, c), lambda b, ci: (0, 0, 0)),
        pl.BlockSpec((c, RET_W), const2),
        pl.BlockSpec((c, RET_QK), const2),
        pl.BlockSpec((1, RET_W), const2),
        state_spec,
    ]
    return pl.pallas_call(
        functools.partial(_ret_body, n_blk=n_blk, per_blk=per_blk, c=c, c_dec=c_dec),
        grid=(n_batch // n_seq, n_chunks),
        in_specs=in_specs,
        out_specs=[pl.BlockSpec((n_seq, c, RET_W), lambda b, ci: (b, ci, 0)), state_spec],
        out_shape=[jax.ShapeDtypeStruct((n_batch, t, RET_W), F32),
                   jax.ShapeDtypeStruct((n_batch, RET_HEADS, RET_DK, RET_DV), F32)],
        scratch_shapes=[pltpu.VMEM((n_seq, RET_HEADS, RET_DK, RET_DV), F32)],
        compiler_params=_params("parallel", "arbitrary"),
    )(*([proj] * (4 * n_blk)), cos, s_lo, s_hi, mask, q_dec, k_dec, ret_gn.reshape(1, RET_W).astype(F32), s0)


LORA_COLS = LORA_W + LORA_A + LORA_G


def _rwkv_lora_terms(lo, w0_ref, w2_ref, a0_ref, a2_ref, g2_ref):
    hw = lo[:, :LORA_W]
    ha = lo[:, LORA_W:LORA_W + LORA_A]
    hg = lo[:, LORA_W + LORA_A:]
    u = w0_ref[...] + jnp.dot(jnp.tanh(hw).astype(BF16), w2_ref[...], preferred_element_type=F32)
    decay = jnp.exp(-float(np.exp(-0.5)) * jax.nn.sigmoid(u))
    rate = jax.nn.sigmoid(a0_ref[...] + jnp.dot(ha.astype(BF16), a2_ref[...], preferred_element_type=F32))
    gate = jnp.dot(jax.nn.sigmoid(hg).astype(BF16), g2_ref[...], preferred_element_type=F32)
    return decay, rate, gate


def _rwkv_pre_body(r_ref, k_ref, v_ref, lo_ref, shift_ref, mu_ref, w0_ref, w2_ref, a0_ref, a2_ref, g2_ref,
                   ro_ref, ko_ref, vo_ref, wo_ref, ao_ref, go_ref, so_ref, prev_scr):
    ci = pl.program_id(1)
    c = r_ref.shape[0]

    @pl.when(ci == 0)
    def _():
        prev_scr[...] = shift_ref[0].astype(F32)

    first_row = lax.broadcasted_iota(jnp.int32, (c, 1), 0) == 0

    def shifted(x_ref, col0):
        w = x_ref.shape[1]
        x = x_ref[...].astype(F32)
        prev = jnp.where(first_row, prev_scr[:, col0:col0 + w], pltpu.roll(x, 1, 0))
        prev_scr[:, col0:col0 + w] = x[c - 1:c, :]
        return x + (prev - x) * mu_ref[:, col0:col0 + w]

    ro_ref[...] = shifted(r_ref, 0).T
    ko_ref[...] = shifted(k_ref, RWKV_W).T
    vo_ref[...] = shifted(v_ref, 2 * RWKV_W).T
    decay, rate, gate = _rwkv_lora_terms(shifted(lo_ref, 3 * RWKV_W), w0_ref, w2_ref, a0_ref, a2_ref, g2_ref)
    wo_ref[...] = decay.T
    ao_ref[...] = rate.T
    go_ref[...] = gate

    @pl.when(ci == pl.num_programs(1) - 1)
    def _():
        so_ref[0] = prev_scr[...]


def _rwkv_pre_args(s_shift, n_batch, mu, w0, w2, a0, a2, g2):
    return (s_shift.reshape(n_batch, 1, N_RWKV_COLS), mu.reshape(1, -1), w0.reshape(1, -1), w2.astype(BF16),
            a0.reshape(1, -1), a2.astype(BF16), g2.astype(BF16))


def _rwkv_pre_weight_specs(const):
    return [pl.BlockSpec((1, N_RWKV_COLS), const), pl.BlockSpec((1, RWKV_W), const),
            pl.BlockSpec((LORA_W, RWKV_W), const), pl.BlockSpec((1, RWKV_W), const),
            pl.BlockSpec((LORA_A, RWKV_W), const), pl.BlockSpec((LORA_G, RWKV_W), const)]


def _rwkv_pre(proj, s_shift, mu, w0, w2, a0, a2, g2, *, row0, n_batch, t, c):
    n_chunks = t // c
    assert t % c == 0 and row0 % c == 0
    blk0 = row0 // c
    col_r = N_RET_COLS // RWKV_W
    col_lo = (N_RET_COLS + 3 * RWKV_W) // LORA_COLS
    assert col_r * RWKV_W == N_RET_COLS and col_lo * LORA_COLS == N_RET_COLS + 3 * RWKV_W

    def row_map(col):
        return lambda b, ci: (blk0 + b * n_chunks + ci, col)

    state_spec = pl.BlockSpec((1, 1, N_RWKV_COLS), lambda b, ci: (b, 0, 0))
    in_specs = [pl.BlockSpec((c, RWKV_W), row_map(col_r)), pl.BlockSpec((c, RWKV_W), row_map(col_r + 1)),
                pl.BlockSpec((c, RWKV_W), row_map(col_r + 2)), pl.BlockSpec((c, LORA_COLS), row_map(col_lo)),
                state_spec] + _rwkv_pre_weight_specs(lambda b, ci: (0, 0))
    vec_spec = pl.BlockSpec((None, RWKV_W, c), lambda b, ci: (b, 0, ci))
    vec_shape = jax.ShapeDtypeStruct((n_batch, RWKV_W, t), F32)
    return pl.pallas_call(
        _rwkv_pre_body,
        grid=(n_batch, n_chunks),
        in_specs=in_specs,
        out_specs=[vec_spec] * 5 + [pl.BlockSpec((c, RWKV_W), lambda b, ci: (b * n_chunks + ci, 0)), state_spec],
        out_shape=[vec_shape] * 5 + [jax.ShapeDtypeStruct((n_batch * t, RWKV_W), F32),
                                     jax.ShapeDtypeStruct((n_batch, 1, N_RWKV_COLS), F32)],
        scratch_shapes=[pltpu.VMEM((1, N_RWKV_COLS), F32)],
        compiler_params=_params("parallel", "arbitrary"),
    )(proj, proj, proj, proj, *_rwkv_pre_args(s_shift, n_batch, mu, w0, w2, a0, a2, g2))


def _rwkv_pre_short_body(r_ref, k_ref, v_ref, lo_ref, shift_ref, mu_ref, w0_ref, w2_ref, a0_ref, a2_ref, g2_ref,
                         ro_ref, ko_ref, vo_ref, wo_ref, ao_ref, go_ref, so_ref, *, n_b, t):
    rows = n_b * t
    first_tok = (lax.broadcasted_iota(jnp.int32, (rows, 1), 0) & (t - 1)) == 0

    def shifted(x_ref, col0):
        w = x_ref.shape[1]
        x = x_ref[...].astype(F32)
        carried = jnp.broadcast_to(shift_ref[:, :, col0:col0 + w].astype(F32), (n_b, t, w)).reshape(rows, w)
        prev = jnp.where(first_tok, carried, pltpu.roll(x, 1, 0))
        so_ref[:, :, col0:col0 + w] = x.reshape(n_b, t, w)[:, t - 1:t, :]
        return x + (prev - x) * mu_ref[:, col0:col0 + w]

    def put(o_ref, x):
        by_tok = jnp.swapaxes(x.reshape(n_b, t, RWKV_W), 0, 1)
        for ti in range(t):
            feat = by_tok[ti].T.reshape(RWKV_N, RWKV_HEADS, n_b)
            o_ref[:, ti] = jnp.swapaxes(feat, 0, 1)

    put(ro_ref, shifted(r_ref, 0))
    put(ko_ref, shifted(k_ref, RWKV_W))
    put(vo_ref, shifted(v_ref, 2 * RWKV_W))
    decay, rate, gate = _rwkv_lora_terms(shifted(lo_ref, 3 * RWKV_W), w0_ref, w2_ref, a0_ref, a2_ref, g2_ref)
    put(wo_ref, decay)
    put(ao_ref, rate)
    go_ref[...] = gate


def _rwkv_pre_short(proj, s_shift, mu, w0, w2, a0, a2, g2, *, row0, n_batch, t):
    rows = n_batch * t
    assert row0 % rows == 0 and t & (t - 1) == 0
    blk0 = row0 // rows
    col_r = N_RET_COLS // RWKV_W
    col_lo = (N_RET_COLS + 3 * RWKV_W) // LORA_COLS
    state_spec = pl.BlockSpec((n_batch, 1, N_RWKV_COLS), lambda i: (0, 0, 0))
    in_specs = [pl.BlockSpec((rows, RWKV_W), lambda i: (blk0, col_r)),
                pl.BlockSpec((rows, RWKV_W), lambda i: (blk0, col_r + 1)),
                pl.BlockSpec((rows, RWKV_W), lambda i: (blk0, col_r + 2)),
                pl.BlockSpec((rows, LORA_COLS), lambda i: (blk0, col_lo)),
                state_spec] + _rwkv_pre_weight_specs(lambda i: (0, 0))
    vec_shape = (RWKV_HEADS, t, RWKV_N, n_batch)
    vec_spec = pl.BlockSpec(vec_shape, lambda i: (0, 0, 0, 0))
    return pl.pallas_call(
        functools.partial(_rwkv_pre_short_body, n_b=n_batch, t=t),
        grid=(1,),
        in_specs=in_specs,
        out_specs=[vec_spec] * 5 + [pl.BlockSpec((rows, RWKV_W), lambda i: (0, 0)), state_spec],
        out_shape=[jax.ShapeDtypeStruct(vec_shape, F32)] * 5 + [
            jax.ShapeDtypeStruct((rows, RWKV_W), F32), jax.ShapeDtypeStruct((n_batch, 1, N_RWKV_COLS), F32)],
        compiler_params=_params("arbitrary"),
    )(proj, proj, proj, proj, *_rwkv_pre_args(s_shift, n_batch, mu, w0, w2, a0, a2, g2))


def _scan_body(r_ref, k_ref, w_ref, a_ref, v_ref, kk_ref, ka_ref, rk_ref, lw_ref, lb_ref, s0_ref,
               y_ref, sout_ref, s_scr, a_scr, b_scr, km_scr, *, tc, vr, halves):
    ci = pl.program_id(1)

    @pl.when(ci == 0)
    def _():
        s_scr[...] = s0_ref[...].astype(F32)

    def ksum(x):
        return jnp.sum(x, axis=-2, keepdims=True)

    def vsum(x):
        if halves == 2:
            x2 = x.reshape(tc * vr, LANES)
            x = (x2 + pltpu.roll(x2, LANES // 2, 1)).reshape(tc, vr, LANES)
        return jnp.sum(x, axis=1, keepdims=True)

    kr = k_ref[...]
    a = a_ref[...]
    kk = kr * kk_ref[...]
    kk = kk / jnp.maximum(jnp.sqrt(ksum(kk * kk)), 1e-12)
    a_scr[...] = -kk
    b_scr[...] = kk * a
    km_scr[...] = kr * (1.0 + (a - 1.0) * ka_ref[...])

    def token(t, carry):
        r, w, avec, bvec, kmod = r_ref[t], w_ref[t], a_scr[t], b_scr[t], km_scr[t]

        def value_row(i, c2):
            s = s_scr[i]
            sa = ksum(s * avec)
            s = s * w + sa * bvec + v_ref[t, pl.ds(i, 1), :] * kmod
            s_scr[i] = s
            y_ref[t, pl.ds(i, 1), :] = ksum(s * r)
            return c2

        lax.fori_loop(0, vr, value_row, 0, unroll=16)
        return carry

    lax.fori_loop(0, tc, token, 0)

    y = y_ref[...]
    d = y - vsum(y) * (1.0 / RWKV_N)
    var = vsum(d * d) * (1.0 / RWKV_N)
    bonus = ksum(r_ref[...] * km_scr[...] * rk_ref[...])
    y_ref[...] = d * lax.rsqrt(var + LNX_EPS) * lw_ref[...] + lb_ref[...] + bonus * v_ref[...]

    @pl.when(ci == pl.num_programs(1) - 1)
    def _():
        sout_ref[...] = s_scr[...]


def _rwkv_scan(r, k, w, a, v, k_k, k_a, r_k, lnx_w, lnx_b, s0, *, tc, halves):
    n_grp, t, _, lanes = r.shape
    vr = v.shape[2]
    assert lanes == LANES and t % tc == 0 and vr * halves == RWKV_N

    def tok_spec(rows):
        return pl.BlockSpec((None, tc, rows, LANES), lambda g, ci: (g, ci, 0, 0))

    def par_spec(rows):
        return pl.BlockSpec((None, rows, LANES), lambda g, ci: (g, 0, 0))

    st_spec = pl.BlockSpec((None, vr, RWKV_N, LANES), lambda g, ci: (g, 0, 0, 0))
    key_scratch = pltpu.VMEM((tc, RWKV_N, LANES), F32)
    return pl.pallas_call(
        functools.partial(_scan_body, tc=tc, vr=vr, halves=halves),
        grid=(n_grp, t // tc),
        in_specs=[tok_spec(RWKV_N)] * 4 + [tok_spec(vr)] + [par_spec(RWKV_N)] * 3 + [par_spec(vr)] * 2 + [st_spec],
        out_specs=[tok_spec(vr), st_spec],
        out_shape=[jax.ShapeDtypeStruct((n_grp, t, vr, LANES), F32),
                   jax.ShapeDtypeStruct((n_grp, vr, RWKV_N, LANES), F32)],
        scratch_shapes=[pltpu.VMEM((vr, RWKV_N, LANES), F32), key_scratch, key_scratch, key_scratch],
        compiler_params=_params("parallel", "arbitrary"),
    )(r, k, w, a, v, k_k, k_a, r_k, lnx_w, lnx_b, s0)


SCAN_TC = 128
SCAN_SUB = 64
MEM_CHUNKS = 4


def _scan_prompt_body(r_ref, k_ref, w_ref, a_ref, v_ref, kk_ref, ka_ref, rk_ref, lw_ref, lb_ref, s0_ref,
                      memk_hbm, memv_hbm, y_ref, sout_ref, memk_out, memv_out,
                      s_scr, r_c, w_c, a_c, b_c, km_c, v_c, y_c, stash, mem_stage, mem_in_sem, mem_out_sem,
                      *, n_b, mem_seqs):
    ci = pl.program_id(0)
    vr = RWKV_N // 2
    ts = SCAN_SUB
    tile = RWKV_HEADS
    half_lanes = LANES // 2

    @pl.when(ci == 0)
    def _():
        s_scr[...] = s0_ref[...].astype(F32)

    chunk_seqs = mem_seqs // MEM_CHUNKS

    def mem_in(chunk, slot):
        copies = []
        for j in range(chunk_seqs):
            seq = (ci * MEM_CHUNKS + chunk) * chunk_seqs + j
            for h in range(MEM_HEADS):
                copies.append(pltpu.make_async_copy(memk_hbm.at[seq, :, h, :], mem_stage.at[slot, 0, j, h],
                                                    mem_in_sem.at[slot]))
                copies.append(pltpu.make_async_copy(memv_hbm.at[seq, :, h, :], mem_stage.at[slot, 1, j, h],
                                                    mem_in_sem.at[slot]))
        return copies

    def mem_out(chunk, slot):
        seqs = pl.ds((ci * MEM_CHUNKS + chunk) * chunk_seqs, chunk_seqs)
        return [pltpu.make_async_copy(mem_stage.at[slot, 0], memk_out.at[seqs], mem_out_sem.at[slot]),
                pltpu.make_async_copy(mem_stage.at[slot, 1], memv_out.at[seqs], mem_out_sem.at[slot])]

    def mem_phase(p):
        if 1 <= p <= MEM_CHUNKS:
            for cp in mem_in(p - 1, (p - 1) % 2):
                cp.wait()
            for cp in mem_out(p - 1, (p - 1) % 2):
                cp.start()
        if 2 <= p <= MEM_CHUNKS + 1:
            for cp in mem_out(p - 2, p % 2):
                cp.wait()
        if p < MEM_CHUNKS:
            for cp in mem_in(p, p % 2):
                cp.start()

    mem_phase(0)

    low = lax.broadcasted_iota(jnp.int32, (ts, LANES), 1) < half_lanes

    def feature_pair_rows(x_ref, base):
        tiles = [x_ref[b, pl.ds(base + f * tile, tile), :] for f in range(2) for b in range(n_b)]
        return jnp.concatenate(tiles, axis=0).T

    def key_to_chain(x_ref, dst, stash, t0):
        def group(g, c):
            rows = []
            for j in range(4):
                pair = g * 4 + j
                if t0 == 0:
                    full = feature_pair_rows(x_ref, pl.multiple_of(pair * 2 * tile, 2 * tile))
                    mt = full[:ts]
                    stash[pair] = full[ts:]
                else:
                    mt = stash[pair]
                sw = pltpu.roll(mt, half_lanes, 1)
                rows += [jnp.where(low, mt, sw), jnp.where(low, sw, mt)]
            dst[:, pl.ds(pl.multiple_of(g * 8, 8), 8), :] = jnp.swapaxes(jnp.stack(rows, axis=0), 0, 1)
            return c
        lax.fori_loop(0, RWKV_N // 8, group, 0, unroll=4)

    def value_to_chain(g, c):
        rows = [feature_pair_rows(v_ref, pl.multiple_of((g * 8 + j) * 2 * tile, 2 * tile)) for j in range(8)]
        v_c[:, pl.ds(pl.multiple_of(g * 8, 8), 8), :] = jnp.swapaxes(jnp.stack(rows, axis=0), 0, 1)
        return c
    lax.fori_loop(0, vr // 8, value_to_chain, 0, unroll=2)

    def ksum(x):
        return jnp.sum(x, axis=-2, keepdims=True)

    for t0 in range(0, SCAN_TC, ts):
        key_to_chain(r_ref, r_c, stash.at[0], t0)
        key_to_chain(w_ref, w_c, stash.at[1], t0)
        key_to_chain(k_ref, km_c, stash.at[2], t0)
        key_to_chain(a_ref, b_c, stash.at[3], t0)

        def prep(g8, c):
            toks = pl.ds(pl.multiple_of(g8 * 8, 8), 8)
            kr = km_c[toks]
            a = b_c[toks]
            kk = kr * kk_ref[...]
            kk = kk / jnp.maximum(jnp.sqrt(ksum(kk * kk)), 1e-12)
            a_c[toks] = -kk
            b_c[toks] = kk * a
            km_c[toks] = kr * (1.0 + (a - 1.0) * ka_ref[...])
            return c
        lax.fori_loop(0, ts // 8, prep, 0, unroll=2)
        mem_phase(1 + 2 * (t0 // ts))

        def token(t, carry):
            r, w, avec, bvec, kmod = r_c[t], w_c[t], a_c[t], b_c[t], km_c[t]

            def value_row(i, c2):
                s = s_scr[i]
                sa = ksum(s * avec)
                s = s * w + sa * bvec + v_c[t0 + t, pl.ds(i, 1), :] * kmod
                s_scr[i] = s
                y_c[t0 + t, pl.ds(i, 1), :] = ksum(s * r)
                return c2

            lax.fori_loop(0, vr, value_row, 0, unroll=True)
            return carry

        lax.fori_loop(0, ts, token, 0)
        mem_phase(2 + 2 * (t0 // ts))

        def post(g8, c):
            ktoks = pl.ds(pl.multiple_of(g8 * 8, 8), 8)
            vtoks = pl.ds(pl.multiple_of(t0 + g8 * 8, 8), 8)

            def vsum(x):
                x2 = x.reshape(8 * vr, LANES)
                x2 = x2 + pltpu.roll(x2, half_lanes, 1)
                return jnp.sum(x2.reshape(8, vr, LANES), axis=1, keepdims=True)

            y = y_c[vtoks]
            d = y - vsum(y) * (1.0 / RWKV_N)
            var = vsum(d * d) * (1.0 / RWKV_N)
            bonus = ksum(r_c[ktoks] * km_c[ktoks] * rk_ref[...])
            y_c[vtoks] = d * lax.rsqrt(var + LNX_EPS) * lw_ref[...] + lb_ref[...] + bonus * v_c[vtoks]
            return c
        lax.fori_loop(0, ts // 8, post, 0, unroll=4)

    def value_from_chain(g, c):
        blk = jnp.swapaxes(y_c[:, pl.ds(pl.multiple_of(g * 8, 8), 8), :], 0, 1)
        for j in range(8):
            mt = blk[j].T
            base = pl.multiple_of((g * 8 + j) * 2 * tile, 2 * tile)
            for hf in range(2):
                for b in range(n_b):
                    row0 = (hf * n_b + b) * tile
                    y_ref[b, pl.ds(base + hf * tile, tile), :] = mt[row0:row0 + tile, :]
        return c
    lax.fori_loop(0, vr // 8, value_from_chain, 0, unroll=2)

    assert 2 * (SCAN_TC // ts) == MEM_CHUNKS
    mem_phase(MEM_CHUNKS + 1)

    @pl.when(ci == pl.num_programs(0) - 1)
    def _():
        sout_ref[...] = s_scr[...]


def _rwkv_scan_prompt(r, k, w, a, v, k_k, k_a, r_k, lnx_w, lnx_b, s0, mem_k, mem_v):
    n_b, _, t = r.shape
    vr = RWKV_N // 2
    n_steps = t // SCAN_TC
    n_mem_seq = mem_k.shape[0]
    assert t % SCAN_TC == 0 and 2 * n_b * RWKV_HEADS == LANES and n_mem_seq % (n_steps * MEM_CHUNKS) == 0
    chunk_seqs = n_mem_seq // (n_steps * MEM_CHUNKS)
    any_spec = pl.BlockSpec(memory_space=pl.ANY)
    mem_shape = jax.ShapeDtypeStruct((n_mem_seq, MEM_HEADS, N_MEM, MEM_DH), F32)
    tok_spec = pl.BlockSpec((n_b, RWKV_W, SCAN_TC), lambda ci: (0, 0, ci))
    key_par = pl.BlockSpec((RWKV_N, LANES), lambda ci: (0, 0))
    val_par = pl.BlockSpec((vr, LANES), lambda ci: (0, 0))
    st_spec = pl.BlockSpec((vr, RWKV_N, LANES), lambda ci: (0, 0, 0))
    key_chain = pltpu.VMEM((SCAN_SUB, RWKV_N, LANES), F32)
    val_chain = pltpu.VMEM((SCAN_TC, vr, LANES), F32)
    return pl.pallas_call(
        functools.partial(_scan_prompt_body, n_b=n_b, mem_seqs=n_mem_seq // n_steps),
        grid=(n_steps,),
        in_specs=[tok_spec] * 5 + [key_par] * 3 + [val_par] * 2 + [st_spec, any_spec, any_spec],
        out_specs=[tok_spec, st_spec, any_spec, any_spec],
        out_shape=[jax.ShapeDtypeStruct((n_b, RWKV_W, t), F32),
                   jax.ShapeDtypeStruct((vr, RWKV_N, LANES), F32), mem_shape, mem_shape],
        scratch_shapes=([pltpu.VMEM((vr, RWKV_N, LANES), F32)] + [key_chain] * 5 + [val_chain] * 2
                        + [pltpu.VMEM((4, RWKV_N // 2, SCAN_TC - SCAN_SUB, LANES), F32),
                           pltpu.VMEM((2, 2, chunk_seqs, MEM_HEADS, N_MEM, MEM_DH), F32),
                           pltpu.SemaphoreType.DMA((2,)), pltpu.SemaphoreType.DMA((2,))]),
        compiler_params=_params("arbitrary"),
    )(r, k, w, a, v, k_k, k_a, r_k, lnx_w, lnx_b, s0, mem_k, mem_v)


def _merge_body(*refs, part_counts):
    n = len(part_counts)
    x_refs, oret_refs, g_refs = refs[:n], refs[n:2 * n], refs[2 * n:3 * n]
    yt_ref, ys_ref, wt_ref, wb_ref, gq_ref, wq_ref, o_ref, q_ref = refs[3 * n:]
    x = _read_row_parts(x_refs, part_counts)
    y = jnp.where(pl.program_id(0) >= part_counts[0], ys_ref[...], yt_ref[...].T)
    yb = (y * _read_row_parts(g_refs, part_counts)).astype(BF16)
    ob = _read_row_parts(oret_refs, part_counts).astype(BF16)
    n_chunk = 256
    for j in range(0, D_MODEL, n_chunk):
        acc = jnp.dot(ob, wt_ref[:, j:j + n_chunk], preferred_element_type=F32)
        acc = acc + jnp.dot(yb, wb_ref[:, j:j + n_chunk], preferred_element_type=F32)
        o_ref[:, j:j + n_chunk] = x[:, j:j + n_chunk] + acc
    h = o_ref[...]
    hb = (h * lax.rsqrt(jnp.mean(h * h, axis=-1, keepdims=True) + EPS) * gq_ref[...]).astype(BF16)
    for j in range(0, D_MODEL, n_chunk):
        q_ref[:, j:j + n_chunk] = jnp.dot(hb, wq_ref[:, j:j + n_chunk], preferred_element_type=F32)


def _merge(x_parts, oret_parts, g_parts, y_first_t, y_second, w_ret, w_rwkv, gain_q, w_q):
    m = sum(part.shape[0] for part in x_parts)
    in_specs, part_counts = [], None
    for parts in (x_parts, oret_parts, g_parts):
        specs, part_counts = _row_part_specs(parts, TM)
        in_specs += specs
    assert len(part_counts) == 2
    n_first = part_counts[0]
    tiles = y_first_t.shape[2] // TM
    assert y_first_t.shape[0] * tiles == n_first
    yt_spec = pl.BlockSpec((None, RWKV_W, TM),
                           lambda i: (jnp.minimum(i, n_first - 1) // tiles, 0, jnp.minimum(i, n_first - 1) % tiles))
    ys_spec = pl.BlockSpec((TM, RWKV_W), lambda i: (jnp.clip(i - n_first, 0, part_counts[1] - 1), 0))
    wspec = pl.BlockSpec((RET_W, D_MODEL), lambda i: (0, 0))
    row_spec = pl.BlockSpec((TM, D_MODEL), lambda i: (i, 0))
    row_shape = jax.ShapeDtypeStruct((m, D_MODEL), F32)
    return pl.pallas_call(
        functools.partial(_merge_body, part_counts=part_counts),
        grid=(m // TM,),
        in_specs=in_specs + [yt_spec, ys_spec, wspec, wspec, pl.BlockSpec((1, D_MODEL), lambda i: (0, 0)),
                             pl.BlockSpec((D_MODEL, D_MODEL), lambda i: (0, 0))],
        out_specs=[row_spec, row_spec],
        out_shape=[row_shape, row_shape],
        compiler_params=_params("parallel"),
    )(*x_parts, *oret_parts, *g_parts, y_first_t, y_second, w_ret.astype(BF16), w_rwkv.astype(BF16),
      gain_q.reshape(1, D_MODEL), w_q.astype(BF16))


def _mem_kv_body(x_ref, g_ref, wk_ref, wv_ref, k_ref, v_ref, kh_ref, vh_ref, *, n_seq):
    x = x_ref[...].astype(F32)
    xb = (x * lax.rsqrt(jnp.mean(x * x, axis=-1, keepdims=True) + EPS) * g_ref[...]).astype(BF16)
    for w_ref, o_ref, oh_ref in ((wk_ref, k_ref, kh_ref), (wv_ref, v_ref, vh_ref)):
        for h in range(MEM_HEADS):
            acc = jnp.dot(xb, w_ref[:, h * MEM_DH:(h + 1) * MEM_DH], preferred_element_type=F32)
            o_ref[:, h, :] = acc
            for s in range(n_seq):
                oh_ref[s, h] = acc[s * N_MEM:(s + 1) * N_MEM]


def _mem_kv(mem, gain, w_k, w_v):
    n_b = mem.shape[0]
    n_seq = TM // N_MEM
    assert n_seq * N_MEM == TM and n_b % n_seq == 0
    wspec = pl.BlockSpec((D_MODEL, D_MODEL), lambda i: (0, 0))
    tok_spec = pl.BlockSpec((TM, MEM_HEADS, MEM_DH), lambda i: (i, 0, 0))
    head_spec = pl.BlockSpec((n_seq, MEM_HEADS, N_MEM, MEM_DH), lambda i: (i, 0, 0, 0))
    tok_shape = jax.ShapeDtypeStruct((n_b * N_MEM, MEM_HEADS, MEM_DH), F32)
    head_shape = jax.ShapeDtypeStruct((n_b, MEM_HEADS, N_MEM, MEM_DH), F32)
    return pl.pallas_call(
        functools.partial(_mem_kv_body, n_seq=n_seq),
        grid=(n_b // n_seq,),
        in_specs=[pl.BlockSpec((TM, D_MODEL), lambda i: (i, 0)), pl.BlockSpec((1, D_MODEL), lambda i: (0, 0)),
                  wspec, wspec],
        out_specs=[tok_spec, tok_spec, head_spec, head_spec],
        out_shape=[tok_shape, tok_shape, head_shape, head_shape],
        compiler_params=_params("parallel"),
    )(mem.reshape(n_b * N_MEM, D_MODEL), gain.reshape(1, D_MODEL), w_k.astype(BF16), w_v.astype(BF16))


def _attn_body(q_ref, k_ref, v_ref, o_ref, *, n_seq, tq):
    nt = (((1,), (1,)), ((), ()))
    for g in range(n_seq):
        rows = slice(g * tq, (g + 1) * tq)
        q = q_ref[rows, :].astype(BF16)
        s = lax.dot_general(q, k_ref[g].astype(BF16), nt, preferred_element_type=F32) * (MEM_DH ** -0.5)
        p = jnp.exp(s - jnp.max(s, axis=-1, keepdims=True))
        l = jnp.sum(p, axis=-1, keepdims=True)
        o = jnp.dot(p.astype(BF16), v_ref[g].astype(BF16), preferred_element_type=F32)
        o_ref[rows, :] = o / l


def _attention(q, mem_k, mem_v, *, row0, n_batch, t, n_seq, tq):
    q_tiles = t // tq
    rows = n_seq * tq
    assert t % tq == 0 and n_batch % n_seq == 0 and row0 % rows == 0 and (n_seq == 1 or q_tiles == 1)
    blk0 = row0 // rows
    kv_spec = pl.BlockSpec((n_seq, None, N_MEM, MEM_DH), lambda b, h, qi: (b, h, 0, 0))
    return pl.pallas_call(
        functools.partial(_attn_body, n_seq=n_seq, tq=tq),
        grid=(n_batch // n_seq, MEM_HEADS, q_tiles),
        in_specs=[pl.BlockSpec((rows, MEM_DH), lambda b, h, qi: (blk0 + b * q_tiles + qi, h)), kv_spec, kv_spec],
        out_specs=pl.BlockSpec((rows, MEM_DH), lambda b, h, qi: (b * q_tiles + qi, h)),
        out_shape=jax.ShapeDtypeStruct((n_batch * t, D_MODEL), F32),
        compiler_params=_params("parallel", "parallel", "parallel"),
    )(q, mem_k, mem_v)


ROW_TILE = (D_MODEL // LANES, LANES)


def _rows_to_tiles(x):
    chunks = [x[:, j * LANES:(j + 1) * LANES] for j in range(ROW_TILE[0])]
    return jnp.swapaxes(jnp.stack(chunks, axis=0), 0, 1)


def _tiles_to_rows(x):
    chunks = jnp.swapaxes(x, 0, 1)
    return jnp.concatenate([chunks[j] for j in range(ROW_TILE[0])], axis=1)


def _attn_out_body(*refs, part_counts):
    n = len(part_counts)
    w_ref, res_ref, o_ref = refs[n:]
    ab = _read_row_parts(refs[:n], part_counts).astype(BF16)
    n_chunk = 256
    for j in range(0, D_MODEL, n_chunk):
        o_ref[:, j:j + n_chunk] = res_ref[:, j:j + n_chunk] + jnp.dot(
            ab, w_ref[:, j:j + n_chunk], preferred_element_type=F32)


def _attn_out(att_parts, w_mo, residual):
    m = residual.shape[0]
    att_specs, part_counts = _row_part_specs(att_parts, TM)
    row = pl.BlockSpec((TM, D_MODEL), lambda i: (i, 0))
    return pl.pallas_call(
        functools.partial(_attn_out_body, part_counts=part_counts),
        grid=(m // TM,),
        in_specs=att_specs + [pl.BlockSpec((D_MODEL, D_MODEL), lambda i: (0, 0)), row],
        out_specs=row,
        out_shape=jax.ShapeDtypeStruct((m, D_MODEL), F32),
        compiler_params=_params("parallel"),
    )(*att_parts, w_mo.astype(BF16), residual)


def _router_body(h_ref, g_ref, w_ref, b_ref, hn_ref, ids_ref, comb_ref):
    x = h_ref[...]
    hn = x * lax.rsqrt(jnp.mean(x * x, axis=-1, keepdims=True) + EPS) * g_ref[...]
    hn_ref[...] = _rows_to_tiles(hn)
    logits = jnp.dot(hn, w_ref[...], precision=lax.Precision.HIGHEST, preferred_element_type=F32) + b_ref[...]
    lane = lax.broadcasted_iota(jnp.int32, logits.shape, 1).astype(F32)
    neg = -jnp.inf

    def first_argmax(vals):
        m = jnp.max(vals, axis=-1, keepdims=True)
        return m, jnp.min(jnp.where(vals == m, lane, float(LANES)), axis=-1, keepdims=True)

    gl = jnp.where(lane < N_GROUPS, logits, neg)
    gmax, gsel = first_argmax(gl)
    pg_sel = 1.0 / jnp.sum(jnp.exp(gl - gmax), axis=-1, keepdims=True)
    e0 = N_GROUPS + gsel * EXP_PER_GROUP
    el = jnp.where((lane >= e0) & (lane < e0 + EXP_PER_GROUP), logits, neg)
    m1, i1 = first_argmax(el)
    m2, i2 = first_argmax(jnp.where(lane == i1, neg, el))
    e21 = jnp.exp(m2 - m1)
    c1 = pg_sel / (1.0 + e21)
    c2 = c1 * e21
    ids = jnp.where(lane == 0, i1 - N_GROUPS, jnp.where(lane == 1, i2 - N_GROUPS, 0.0))
    ids_ref[...] = ids.astype(jnp.int32)
    comb_ref[...] = jnp.where(lane == 0, c1, jnp.where(lane == 1, c2, 0.0))


def _router(h, g_ffn, w_gr, b_gr, w_er, b_er):
    m = h.shape[0]
    pad = LANES - N_GROUPS - N_EXPERTS
    w = jnp.concatenate([w_gr, w_er, jnp.zeros((D_MODEL, pad), F32)], axis=1)
    b = jnp.concatenate([b_gr, b_er, jnp.zeros((pad,), F32)]).reshape(1, LANES)
    row = lambda n: pl.BlockSpec((TM, n), lambda i: (i, 0))
    return pl.pallas_call(
        _router_body,
        grid=(m // TM,),
        in_specs=[row(D_MODEL), pl.BlockSpec((1, D_MODEL), lambda i: (0, 0)),
                  pl.BlockSpec((D_MODEL, LANES), lambda i: (0, 0)), pl.BlockSpec((1, LANES), lambda i: (0, 0))],
        out_specs=[pl.BlockSpec((TM,) + ROW_TILE, lambda i: (i, 0, 0)), row(LANES), row(LANES)],
        out_shape=[jax.ShapeDtypeStruct((m,) + ROW_TILE, F32), jax.ShapeDtypeStruct((m, LANES), jnp.int32),
                   jax.ShapeDtypeStruct((m, LANES), F32)],
        compiler_params=_params("parallel"),
    )(h, g_ffn.reshape(1, D_MODEL), w, b)


def _dispatch_body(pad_start_ref, pad_count_ref, dest_ref, hn_ref, sorted_out, zeros, sem, *, n_pad_rows):
    @pl.when(pl.program_id(0) == 0)
    def _():
        zeros[...] = jnp.zeros(zeros.shape, F32)

        def fill_tail(e, c):
            count = pad_count_ref[e]
            row = pad_start_ref[e]
            bit = MOE_ROWS // 2
            while bit:
                @pl.when((count & bit) != 0)
                def _(row=row, bit=bit):
                    pltpu.make_async_copy(zeros.at[pl.ds(0, bit)], sorted_out.at[pl.ds(row, bit)],
                                          sem.at[TOP_K]).start()
                row = row + (count & bit)
                bit //= 2
            return c
        lax.fori_loop(0, N_EXPERTS, fill_tail, 0)

        def fill_block(j, c):
            pltpu.make_async_copy(zeros, sorted_out.at[pl.ds(pad_start_ref[N_EXPERTS] + j * MOE_ROWS, MOE_ROWS)],
                                  sem.at[TOP_K]).start()
            return c
        lax.fori_loop(0, pad_count_ref[N_EXPERTS] // MOE_ROWS, fill_block, 0)

        def drain(j, c):
            pltpu.make_async_copy(zeros, sorted_out.at[pl.ds(0, MOE_ROWS)], sem.at[TOP_K]).wait()
            return c
        lax.fori_loop(0, n_pad_rows // MOE_ROWS, drain, 0)

    def issue(r, c):
        for k in range(TOP_K):
            pltpu.make_async_copy(hn_ref.at[r], sorted_out.at[dest_ref[0, k, r]], sem.at[k]).start(priority=k)
        return c
    lax.fori_loop(0, TM, issue, 0, unroll=8)
    for k in range(TOP_K):
        pltpu.make_async_copy(hn_ref, sorted_out.at[pl.ds(0, TM)], sem.at[k]).wait()


def _dispatch(hn, dest, pad_start, pad_count, n_sorted):
    n_tok = hn.shape[0]
    n_pad_rows = n_sorted - TOP_K * n_tok
    assert n_pad_rows % MOE_ROWS == 0
    grid_spec = pltpu.PrefetchScalarGridSpec(
        num_scalar_prefetch=2,
        grid=(n_tok // TM,),
        in_specs=[pl.BlockSpec((1, TOP_K, TM), lambda i, ps, pc: (i, 0, 0), memory_space=pltpu.SMEM),
                  pl.BlockSpec((TM,) + ROW_TILE, lambda i, ps, pc: (i, 0, 0))],
        out_specs=pl.BlockSpec(memory_space=pl.ANY),
        scratch_shapes=[pltpu.VMEM((MOE_ROWS,) + ROW_TILE, F32), pltpu.SemaphoreType.DMA((TOP_K + 1,))],
    )
    return pl.pallas_call(
        functools.partial(_dispatch_body, n_pad_rows=n_pad_rows),
        grid_spec=grid_spec,
        out_shape=jax.ShapeDtypeStruct((n_sorted,) + ROW_TILE, F32),
        compiler_params=_params("arbitrary"),
    )(pad_start, pad_count, dest, hn)


def _expert_body(blk_e_ref, x_ref, wg_ref, wu_ref, wd_ref, o_ref):
    del blk_e_ref
    x = _tiles_to_rows(x_ref[...]).astype(BF16)
    hg = jnp.dot(x, wg_ref[0].astype(BF16), preferred_element_type=F32)
    hu = jnp.dot(x, wu_ref[0].astype(BF16), preferred_element_type=F32)
    act = (hg * jax.nn.sigmoid(hg) * hu).astype(BF16)
    o_ref[...] = _rows_to_tiles(jnp.dot(act, wd_ref[0].astype(BF16), preferred_element_type=F32))


def _experts(x_sorted, blk_e, w_gate, w_up, w_down):
    n_blocks = blk_e.shape[0]
    row_spec = pl.BlockSpec((MOE_ROWS,) + ROW_TILE, lambda i, be: (i, 0, 0))
    grid_spec = pltpu.PrefetchScalarGridSpec(
        num_scalar_prefetch=1,
        grid=(n_blocks,),
        in_specs=[
            row_spec,
            pl.BlockSpec((1, D_MODEL, D_EXPERT), lambda i, be: (be[i], 0, 0)),
            pl.BlockSpec((1, D_MODEL, D_EXPERT), lambda i, be: (be[i], 0, 0)),
            pl.BlockSpec((1, D_EXPERT, D_MODEL), lambda i, be: (be[i], 0, 0)),
        ],
        out_specs=row_spec,
    )
    return pl.pallas_call(
        _expert_body,
        grid_spec=grid_spec,
        out_shape=jax.ShapeDtypeStruct(x_sorted.shape, F32),
        compiler_params=_params("arbitrary"),
    )(blk_e, x_sorted, w_gate, w_up, w_down)


def _route_plan(ids):
    n_tok = ids.shape[0]
    n_pairs = ids.size
    n_blocks = -(-(n_pairs + N_EXPERTS * (MOE_ROWS - 1)) // MOE_ROWS)
    flat_e = ids.reshape(n_pairs)
    onehot = (flat_e[:, None] == jnp.arange(N_EXPERTS, dtype=jnp.int32)[None, :]).astype(jnp.int32)
    csum = jnp.cumsum(onehot, axis=0)
    rank = jnp.sum(onehot * csum, axis=1) - 1
    counts = csum[-1]
    pcounts = (counts + MOE_ROWS - 1) // MOE_ROWS * MOE_ROWS
    pends = jnp.cumsum(pcounts)
    pstarts = pends - pcounts
    dest = jnp.sum(onehot * pstarts[None, :], axis=1) + rank
    block_start = jnp.arange(n_blocks, dtype=jnp.int32) * MOE_ROWS
    blk_e = jnp.minimum(jnp.sum((block_start[:, None] >= pends[None, :]).astype(jnp.int32), axis=1),
                        N_EXPERTS - 1).astype(jnp.int32)
    dest = dest.astype(jnp.int32).reshape(n_tok // TM, TM, TOP_K).transpose(0, 2, 1)
    n_sorted = n_blocks * MOE_ROWS
    pad_start = jnp.concatenate([pstarts + counts, pends[-1:]]).astype(jnp.int32)
    pad_count = jnp.concatenate([pcounts - counts, n_sorted - pends[-1:]]).astype(jnp.int32)
    return blk_e, dest, pad_start, pad_count, n_sorted


def _final_body(dest_ref, dest_next_ref, h_ref, comb_ref, g_ref, y_hbm, o_first, o_second, ybuf, sem, *, n_first):
    i = pl.program_id(0)
    n = pl.num_programs(0)
    slot = i % 2

    def start_gather(ref, sl):
        def issue(r, c):
            for k in range(TOP_K):
                pltpu.make_async_copy(y_hbm.at[ref[0, k, r]], ybuf.at[sl, k, r], sem.at[sl]).start(priority=k)
            return c
        lax.fori_loop(0, TM, issue, 0, unroll=8)

    @pl.when(i == 0)
    def _():
        start_gather(dest_ref, 0)

    @pl.when(i + 1 < n)
    def _():
        start_gather(dest_next_ref, 1 - slot)

    for k in range(TOP_K):
        pltpu.make_async_copy(y_hbm.at[pl.ds(0, TM)], ybuf.at[slot, k], sem.at[slot]).wait()

    first, second = _tiles_to_rows(ybuf[slot, 0]), _tiles_to_rows(ybuf[slot, 1])
    x = h_ref[...] + (first * comb_ref[:, 0:1] + second * comb_ref[:, 1:2])
    out = x * lax.rsqrt(jnp.mean(x * x, axis=-1, keepdims=True) + EPS) * g_ref[...]

    @pl.when(i < n_first)
    def _():
        o_first[...] = out

    @pl.when(i >= n_first)
    def _():
        o_second[...] = out


def _final(h, y_sorted, dest, comb, g_final, *, n_first_rows):
    n_rows = h.shape[0]
    assert n_rows % TM == 0 and n_first_rows % TM == 0 and 0 < n_first_rows < n_rows
    n_steps, n_first = n_rows // TM, n_first_rows // TM
    row = lambda n: pl.BlockSpec((TM, n), lambda i: (i, 0))
    dest_spec = lambda f: pl.BlockSpec((1, TOP_K, TM), f, memory_space=pltpu.SMEM)
    return pl.pallas_call(
        functools.partial(_final_body, n_first=n_first),
        grid=(n_steps,),
        in_specs=[dest_spec(lambda i: (i, 0, 0)),
                  dest_spec(lambda i: (jnp.minimum(i + 1, n_steps - 1), 0, 0)),
                  row(D_MODEL), row(LANES), pl.BlockSpec((1, D_MODEL), lambda i: (0, 0)),
                  pl.BlockSpec(memory_space=pl.ANY)],
        out_specs=[pl.BlockSpec((TM, D_MODEL), lambda i: (jnp.minimum(i, n_first - 1), 0)),
                   pl.BlockSpec((TM, D_MODEL), lambda i: (jnp.maximum(i - n_first, 0), 0))],
        out_shape=[jax.ShapeDtypeStruct((n_first_rows, D_MODEL), F32),
                   jax.ShapeDtypeStruct((n_rows - n_first_rows, D_MODEL), F32)],
        scratch_shapes=[pltpu.VMEM((2, TOP_K, TM) + ROW_TILE, F32), pltpu.SemaphoreType.DMA((2,))],
        compiler_params=_params("arbitrary"),
    )(dest, dest, h, comb, g_final.reshape(1, D_MODEL), y_sorted)


def _reorder_last(x, shape, order):
    lead = x.shape[:-1]
    n = len(lead)
    y = x.reshape(lead + shape).transpose(tuple(range(n)) + tuple(n + o for o in order))
    return y.reshape(lead + (x.shape[-1],))


HALF_N = RWKV_N // 2


def _key_major(x):
    return _reorder_last(x, (RWKV_HEADS, RWKV_N), (1, 0))


def _key_major_inv(x):
    return _reorder_last(x, (RWKV_N, RWKV_HEADS), (1, 0))


def _value_major(x):
    return _reorder_last(x, (RWKV_HEADS, 2, HALF_N), (2, 1, 0))


def _value_major_inv(x):
    return _reorder_last(x, (HALF_N, 2, RWKV_HEADS), (2, 1, 0))


def _rwkv_cols(x, key_fn, value_fn):
    return jnp.concatenate([key_fn(x[..., :RWKV_W]), key_fn(x[..., RWKV_W:2 * RWKV_W]),
                            value_fn(x[..., 2 * RWKV_W:3 * RWKV_W]), x[..., 3 * RWKV_W:]], axis=-1)


def kernel(x_prompt, x_sample, mem_prompt, state_ret, state_rwkv, state_shift, cache_mem_k, cache_mem_v,
           g_mix, w_in, ret_gn, rwkv_mu, rwkv_w0, rwkv_w2, rwkv_a0, rwkv_a2, rwkv_g2, rwkv_k_k, rwkv_k_a,
           rwkv_r_k, rwkv_lnx_w, rwkv_lnx_b, w_out, g_mem_q, g_mem_kv, w_mq, w_mk, w_mv, w_mo, g_ffn,
           w_group_router, b_group_router, w_expert_router, b_expert_router, w_e_gate, w_e_up, w_e_down,
           g_final):
    assert w_in.shape[0] == 1, "single-layer decoder"
    bp, tp, d = x_prompt.shape
    bs, ts, _ = x_sample.shape
    np_tok, ns_tok = bp * tp, bs * ts
    assert d == D_MODEL and bp * RWKV_HEADS * 2 == LANES and bs == LANES
    l = 0
    x_parts = [x_prompt.reshape(np_tok, d), x_sample.reshape(ns_tok, d)]

    w_in_l = jnp.concatenate([w_in[l][:, :N_RET_COLS], _rwkv_cols(w_in[l][:, N_RET_COLS:], _key_major, _value_major)],
                             axis=1)
    proj = _input_projection(x_parts, g_mix[l], w_in_l)

    pos_p = np.arange(tp)
    pos_s = PAST_LEN + np.arange(ts)
    zero_ret = jnp.zeros((bp, RET_HEADS, RET_DK, RET_DV), F32)
    oret_p, sret_p = _retention(proj, zero_ret, ret_gn[l], pos_p, row0=0, n_batch=bp, t=tp, n_blk=8, per_blk=1)
    oret_s, sret_s = _retention(proj, state_ret[l], ret_gn[l], pos_s, row0=np_tok, n_batch=bs, t=ts,
                                n_blk=1, per_blk=16)
    oret_p, oret_s = oret_p.reshape(np_tok, RET_W), oret_s.reshape(ns_tok, RET_W)

    pre_w = (_rwkv_cols(rwkv_mu[l], _key_major, _value_major), _key_major(rwkv_w0[l]), _key_major(rwkv_w2[l]),
             _key_major(rwkv_a0[l]), _key_major(rwkv_a2[l]), _value_major(rwkv_g2[l]))
    zero_shift = jnp.zeros((bp, N_RWKV_COLS), F32)
    shift_in = _rwkv_cols(state_shift[l], _key_major, _value_major)
    r_p, k_p, v_p, w_p, a_p, gate_p, shift_p = _rwkv_pre(proj, zero_shift, *pre_w, row0=0, n_batch=bp, t=tp, c=512)
    r_s, k_s, v_s, w_s, a_s, gate_s, shift_s = _rwkv_pre_short(proj, shift_in, *pre_w, row0=np_tok, n_batch=bs,
                                                                t=ts)

    kvec = lambda v: v.reshape(RWKV_HEADS, RWKV_N)
    key_par = lambda v: jnp.broadcast_to(kvec(v).T[:, None, None, :], (RWKV_N, 2, bp, RWKV_HEADS)).reshape(
        RWKV_N, LANES)

    val_par = lambda v: jnp.broadcast_to(
        v.reshape(RWKV_HEADS, 2, HALF_N).transpose(2, 1, 0)[:, :, None, :],
        (HALF_N, 2, bp, RWKV_HEADS)).reshape(HALF_N, LANES)
    mem_shape = (N_MEM, MEM_HEADS, MEM_DH)
    y_p, srw_p, cache_k_heads, cache_v_heads = _rwkv_scan_prompt(
        r_p, k_p, w_p, a_p, v_p, key_par(rwkv_k_k[l]), key_par(rwkv_k_a[l]), key_par(rwkv_r_k[l]),
        val_par(rwkv_lnx_w[l]), val_par(rwkv_lnx_b[l]), jnp.zeros((HALF_N, RWKV_N, LANES), F32),
        cache_mem_k.reshape(bs, *mem_shape), cache_mem_v.reshape(bs, *mem_shape))
    srw_p = srw_p.reshape(HALF_N, RWKV_N, 2, bp, RWKV_HEADS).transpose(3, 4, 2, 0, 1).reshape(
        bp, RWKV_HEADS, RWKV_N, RWKV_N)
    head_par = lambda v: jnp.broadcast_to(kvec(v)[:, :, None], (RWKV_HEADS, RWKV_N, LANES))
    val_rows = lambda v: v.reshape(RWKV_HEADS, 2, HALF_N).transpose(0, 2, 1).reshape(RWKV_HEADS * RWKV_N)
    state_s = state_rwkv[l].astype(F32).reshape(bs, RWKV_HEADS, 2, HALF_N, RWKV_N).transpose(1, 3, 2, 4, 0)
    y_s, srw_s = _rwkv_scan(
        r_s, k_s, w_s, a_s, v_s, head_par(rwkv_k_k[l]), head_par(rwkv_k_a[l]), head_par(rwkv_r_k[l]),
        head_par(val_rows(rwkv_lnx_w[l])), head_par(val_rows(rwkv_lnx_b[l])),
        state_s.reshape(RWKV_HEADS, RWKV_N, RWKV_N, bs), tc=ts, halves=1)
    y_s = y_s.transpose(3, 1, 2, 0).reshape(ns_tok, RWKV_W)
    srw_s = srw_s.reshape(RWKV_HEADS, HALF_N, 2, RWKV_N, bs).transpose(4, 0, 2, 1, 3).reshape(
        bs, RWKV_HEADS, RWKV_N, RWKV_N)

    w_rwkv_out = _value_major(w_out[l][RET_W:].T).T
    h, q = _merge(x_parts, [oret_p, oret_s], [gate_p, gate_s], y_p, y_s, w_out[l][:RET_W], w_rwkv_out,
                  g_mem_q[l], w_mq[l])

    mk, mv, mk_heads, mv_heads = _mem_kv(mem_prompt, g_mem_kv[l], w_mk[l], w_mv[l])
    att_p = _attention(q, mk_heads, mv_heads, row0=0, n_batch=bp, t=tp, n_seq=1, tq=tp)
    att_s = _attention(q, cache_k_heads, cache_v_heads, row0=np_tok, n_batch=bs, t=ts, n_seq=32, tq=ts)

    h = _attn_out([att_p, att_s], w_mo[l], h)

    hn, ids, comb = _router(h, g_ffn[l], w_group_router[l], b_group_router[l], w_expert_router[l],
                            b_expert_router[l])
    blk_e, dest, pad_start, pad_count, n_sorted = _route_plan(ids[:, :TOP_K])
    x_sorted = _dispatch(hn, dest, pad_start, pad_count, n_sorted)
    y_sorted = _experts(x_sorted, blk_e, w_e_gate[l], w_e_up[l], w_e_down[l])
    y_prompt, y_sample = _final(h, y_sorted, dest, comb, g_final, n_first_rows=np_tok)
    y_prompt = y_prompt.reshape(bp, tp, d)
    y_sample = y_sample.reshape(bs, ts, d)

    shift_p = _rwkv_cols(shift_p.reshape(bp, N_RWKV_COLS), _key_major_inv, _value_major_inv)
    shift_s = _rwkv_cols(shift_s.reshape(bs, N_RWKV_COLS), _key_major_inv, _value_major_inv)
    return (y_prompt, y_sample, sret_p[None], srw_p[None], shift_p[None],
            mk.reshape(1, bp, *mem_shape), mv.reshape(1, bp, *mem_shape),
            sret_s[None], srw_s[None], shift_s[None])
```

```python
import functools

import numpy as np
import jax
import jax.numpy as jnp
from jax import lax
from jax.experimental import pallas as pl
from jax.experimental.pallas import tpu as pltpu

F32 = jnp.float32
BF16 = jnp.bfloat16

D_MODEL = 1024
PAST_LEN = 16384
N_MEM = 256
MEM_HEADS = 4
MEM_DH = D_MODEL // MEM_HEADS
RET_HEADS = 4
RET_W = D_MODEL // 2
RET_DV = RET_W // RET_HEADS
RET_DK = RET_DV // 2
RET_QK = RET_HEADS * RET_DK
RET_CHUNK = 128
ROPE_BASE = 10000.0
RWKV_N = 64
RWKV_W = D_MODEL - RET_W
RWKV_HEADS = RWKV_W // RWKV_N
LORA_W = 64
LORA_A = 64
LORA_G = 128
LNX_EPS = 64e-5
N_RET_COLS = 2 * RET_QK + 2 * RET_W
N_RWKV_COLS = 3 * RWKV_W + LORA_W + LORA_A + LORA_G
N_IN_COLS = N_RET_COLS + N_RWKV_COLS
N_GROUPS = 4
EXP_PER_GROUP = 8
N_EXPERTS = N_GROUPS * EXP_PER_GROUP
TOP_K = 2
D_EXPERT = D_MODEL // 2
EPS = 1e-6

LANES = 128
MXU_COLS = 256
MOE_ROWS = 512
TM = 512
PRE_CHUNK = 512
RET_SEQS_LONG = 8
RET_SEQS_SHORT = 16
ATTN_SEQS_SHORT = 32


def _params(*sem):
    return pltpu.CompilerParams(dimension_semantics=sem)


def _row_part_specs(parts, tm):
    specs, counts, start = [], [], 0
    for part in parts:
        nb = part.shape[0] // tm
        assert nb * tm == part.shape[0]
        specs.append(pl.BlockSpec((tm, part.shape[1]), lambda i, s=start, n=nb: (jnp.clip(i - s, 0, n - 1), 0)))
        counts.append(nb)
        start += nb
    return specs, counts


def _read_row_parts(refs, counts):
    i = pl.program_id(0)
    x = refs[0][...]
    start = counts[0]
    for ref, nb in zip(refs[1:], counts[1:]):
        x = jnp.where(i >= start, ref[...], x)
        start += nb
    return x


def _proj_body(*refs, part_counts):
    n = len(part_counts)
    g_ref, w_ref, o_ref = refs[n:]
    x = _read_row_parts(refs[:n], part_counts).astype(F32)
    xb = (x * lax.rsqrt(jnp.mean(x * x, axis=-1, keepdims=True) + EPS) * g_ref[...]).astype(BF16)
    n_chunk = MXU_COLS
    for j in range(0, w_ref.shape[1], n_chunk):
        o_ref[:, j:j + n_chunk] = jnp.dot(xb, w_ref[:, j:j + n_chunk], preferred_element_type=F32)


def _input_projection(x_parts, gain, w):
    k, n_out = w.shape
    m = sum(part.shape[0] for part in x_parts)
    assert n_out % MXU_COLS == 0
    in_specs, part_counts = _row_part_specs(x_parts, TM)
    return pl.pallas_call(
        functools.partial(_proj_body, part_counts=part_counts),
        grid=(m // TM,),
        in_specs=in_specs + [pl.BlockSpec((1, k), lambda i: (0, 0)), pl.BlockSpec((k, n_out), lambda i: (0, 0))],
        out_specs=pl.BlockSpec((TM, n_out), lambda i: (i, 0)),
        out_shape=jax.ShapeDtypeStruct((m, n_out), F32),
        compiler_params=_params("parallel"),
    )(*x_parts, gain.reshape(1, k).astype(F32), w.astype(BF16))


def _rot_tables(pos):
    half = RET_DK // 2
    inv_freq = ROPE_BASE ** (-(np.arange(half, dtype=np.float64) / half))
    ang = pos.astype(np.float64)[:, None] * inv_freq[None, :]
    cos, sin = np.cos(ang), np.sin(ang)
    zero = np.zeros_like(sin)
    c = np.tile(np.concatenate([cos, cos], axis=1), (1, RET_HEADS))
    s_lo = np.tile(np.concatenate([-sin, zero], axis=1), (1, RET_HEADS))
    s_hi = np.tile(np.concatenate([zero, sin], axis=1), (1, RET_HEADS))
    return [jnp.asarray(t, F32) for t in (c, s_lo, s_hi)]


def _ret_decay_tables(c):
    lg = np.log1p(-np.exp2(-5.0 - np.arange(RET_HEADS, dtype=np.float64)))
    idx = np.arange(c, dtype=np.float64)
    diff = idx[:, None] - idx[None, :]
    mask = np.where(diff[None] >= 0, np.exp(np.maximum(diff, 0.0)[None] * lg[:, None, None]), 0.0)
    q_dec = np.repeat(np.exp((idx[:, None] + 1.0) * lg[None, :]), RET_DV, axis=1)
    k_dec = np.repeat(np.exp((c - 1.0 - idx)[:, None] * lg[None, :]), RET_DK, axis=1)
    c_dec = [float(v) for v in np.exp(c * lg)]
    return jnp.asarray(mask, F32), jnp.asarray(q_dec, F32), jnp.asarray(k_dec, F32), c_dec


def _ret_body(*refs, n_blk, per_blk, c, c_dec):
    q_refs, k_refs, v_refs, gate_refs = (refs[j * n_blk:(j + 1) * n_blk] for j in range(4))
    (c_ref, slo_ref, shi_ref, mask_ref, qdec_ref, kdec_ref, gn_ref, s0_ref, o_ref, sout_ref,
     s_scr) = refs[4 * n_blk:]
    n_seq = n_blk * per_blk
    ci = pl.program_id(1)

    @pl.when(ci == 0)
    def _():
        s_scr[...] = s0_ref[...].astype(F32)

    cos, s_lo, s_hi = c_ref[...], slo_ref[...], shi_ref[...]
    half = RET_DK // 2

    def rope(x):
        return x * cos + pltpu.roll(x, RET_QK - half, 1) * s_lo + pltpu.roll(x, half, 1) * s_hi

    nt = (((1,), (1,)), ((), ()))
    tn = (((0,), (0,)), ((), ()))
    for g in range(n_seq):
        blk = g // per_blk
        rows = slice((g % per_blk) * c, (g % per_blk + 1) * c)
        q = rope(q_refs[blk][rows, :].astype(F32))
        k = rope(k_refs[blk][rows, :].astype(F32)) * (RET_DK ** -0.5)
        k_st = k * kdec_ref[...]
        for h in range(RET_HEADS):
            kc = slice(h * RET_DK, (h + 1) * RET_DK)
            vc = slice(h * RET_DV, (h + 1) * RET_DV)
            qh = q[:, kc].astype(BF16)
            vh = v_refs[blk][rows, vc].astype(BF16)
            s_h = s_scr[g, h]
            att = lax.dot_general(qh, k[:, kc].astype(BF16), nt, preferred_element_type=F32) * mask_ref[h]
            o = jnp.dot(att.astype(BF16), vh, preferred_element_type=F32)
            o = o + jnp.dot(qh, s_h.astype(BF16), preferred_element_type=F32) * qdec_ref[:, vc]
            s_scr[g, h] = s_h * c_dec[h] + lax.dot_general(
                k_st[:, kc].astype(BF16), vh, tn, preferred_element_type=F32)
            o = o * lax.rsqrt(jnp.mean(o * o, axis=-1, keepdims=True) + EPS)
            gate = gate_refs[blk][rows, vc].astype(F32)
            o_ref[g, :, vc] = o * gn_ref[:, vc] * (gate * jax.nn.sigmoid(gate))

    @pl.when(ci == pl.num_programs(1) - 1)
    def _():
        sout_ref[...] = s_scr[...]


def _retention(proj, s0, ret_gn, pos, *, row0, n_batch, t, n_blk, per_blk):
    c = RET_CHUNK if t % RET_CHUNK == 0 else t
    n_chunks = t // c
    rows = per_blk * c
    n_seq = n_blk * per_blk
    assert n_batch % n_seq == 0 and row0 % rows == 0 and (per_blk == 1 or n_chunks == 1)
    blk0 = row0 // rows
    mask, q_dec, k_dec, c_dec = _ret_decay_tables(c)
    cos, s_lo, s_hi = _rot_tables(pos)

    def const2(b, ci):
        return (0, 0)

    def row_specs(width, col):
        return [pl.BlockSpec((rows, width), lambda b, ci, j=j: (blk0 + (b * n_blk + j) * n_chunks + ci, col))
                for j in range(n_blk)]

    state_spec = pl.BlockSpec((n_seq, RET_HEADS, RET_DK, RET_DV), lambda b, ci: (b, 0, 0, 0))
    in_specs = row_specs(RET_QK, 0) + row_specs(RET_QK, 1) + row_specs(RET_W, 1) + row_specs(RET_W, 2) + [
        pl.BlockSpec((c, RET_QK), lambda b, ci: (ci, 0)),
        pl.BlockSpec((c, RET_QK), lambda b, ci: (ci, 0)),
        pl.BlockSpec((c, RET_QK), lambda b, ci: (ci, 0)),
        pl.BlockSpec((RET_HEADS, c, c), lambda b, ci: (0, 0, 0)),
        pl.BlockSpec((c, RET_W), const2),
        pl.BlockSpec((c, RET_QK), const2),
        pl.BlockSpec((1, RET_W), const2),
        state_spec,
    ]
    return pl.pallas_call(
        functools.partial(_ret_body, n_blk=n_blk, per_blk=per_blk, c=c, c_dec=c_dec),
        grid=(n_batch // n_seq, n_chunks),
        in_specs=in_specs,
        out_specs=[pl.BlockSpec((n_seq, c, RET_W), lambda b, ci: (b, ci, 0)), state_spec],
        out_shape=[jax.ShapeDtypeStruct((n_batch, t, RET_W), F32),
                   jax.ShapeDtypeStruct((n_batch, RET_HEADS, RET_DK, RET_DV), F32)],
        scratch_shapes=[pltpu.VMEM((n_seq, RET_HEADS, RET_DK, RET_DV), F32)],
        compiler_params=_params("parallel", "arbitrary"),
    )(*([proj] * (4 * n_blk)), cos, s_lo, s_hi, mask, q_dec, k_dec, ret_gn.reshape(1, RET_W).astype(F32), s0)


LORA_COLS = LORA_W + LORA_A + LORA_G


def _rwkv_lora_terms(lo, w0_ref, w2_ref, a0_ref, a2_ref, g2_ref):
    hw = lo[:, :LORA_W]
    ha = lo[:, LORA_W:LORA_W + LORA_A]
    hg = lo[:, LORA_W + LORA_A:]
    u = w0_ref[...] + jnp.dot(jnp.tanh(hw).astype(BF16), w2_ref[...], preferred_element_type=F32)
    decay = jnp.exp(-float(np.exp(-0.5)) * jax.nn.sigmoid(u))
    rate = jax.nn.sigmoid(a0_ref[...] + jnp.dot(ha.astype(BF16), a2_ref[...], preferred_element_type=F32))
    gate = jnp.dot(jax.nn.sigmoid(hg).astype(BF16), g2_ref[...], preferred_element_type=F32)
    return decay, rate, gate


def _rwkv_pre_body(r_ref, k_ref, v_ref, lo_ref, shift_ref, mu_ref, w0_ref, w2_ref, a0_ref, a2_ref, g2_ref,
                   ro_ref, ko_ref, vo_ref, wo_ref, ao_ref, go_ref, so_ref, prev_scr):
    ci = pl.program_id(1)
    c = r_ref.shape[0]

    @pl.when(ci == 0)
    def _():
        prev_scr[...] = shift_ref[0].astype(F32)

    first_row = lax.broadcasted_iota(jnp.int32, (c, 1), 0) == 0

    def shifted(x_ref, col0):
        w = x_ref.shape[1]
        x = x_ref[...].astype(F32)
        prev = jnp.where(first_row, prev_scr[:, col0:col0 + w], pltpu.roll(x, 1, 0))
        prev_scr[:, col0:col0 + w] = x[c - 1:c, :]
        return x + (prev - x) * mu_ref[:, col0:col0 + w]

    ro_ref[...] = shifted(r_ref, 0).T
    ko_ref[...] = shifted(k_ref, RWKV_W).T
    vo_ref[...] = shifted(v_ref, 2 * RWKV_W).T
    decay, rate, gate = _rwkv_lora_terms(shifted(lo_ref, 3 * RWKV_W), w0_ref, w2_ref, a0_ref, a2_ref, g2_ref)
    wo_ref[...] = decay.T
    ao_ref[...] = rate.T
    go_ref[...] = gate

    @pl.when(ci == pl.num_programs(1) - 1)
    def _():
        so_ref[0] = prev_scr[...]


def _rwkv_pre_args(s_shift, n_batch, mu, w0, w2, a0, a2, g2):
    return (s_shift.reshape(n_batch, 1, N_RWKV_COLS), mu.reshape(1, -1), w0.reshape(1, -1), w2.astype(BF16),
            a0.reshape(1, -1), a2.astype(BF16), g2.astype(BF16))


def _rwkv_pre_weight_specs(const):
    return [pl.BlockSpec((1, N_RWKV_COLS), const), pl.BlockSpec((1, RWKV_W), const),
            pl.BlockSpec((LORA_W, RWKV_W), const), pl.BlockSpec((1, RWKV_W), const),
            pl.BlockSpec((LORA_A, RWKV_W), const), pl.BlockSpec((LORA_G, RWKV_W), const)]


def _rwkv_pre(proj, s_shift, mu, w0, w2, a0, a2, g2, *, row0, n_batch, t, c):
    n_chunks = t // c
    assert t % c == 0 and row0 % c == 0
    blk0 = row0 // c
    col_r = N_RET_COLS // RWKV_W
    col_lo = (N_RET_COLS + 3 * RWKV_W) // LORA_COLS
    assert col_r * RWKV_W == N_RET_COLS and col_lo * LORA_COLS == N_RET_COLS + 3 * RWKV_W

    def row_map(col):
        return lambda b, ci: (blk0 + b * n_chunks + ci, col)

    state_spec = pl.BlockSpec((1, 1, N_RWKV_COLS), lambda b, ci: (b, 0, 0))
    in_specs = [pl.BlockSpec((c, RWKV_W), row_map(col_r)), pl.BlockSpec((c, RWKV_W), row_map(col_r + 1)),
                pl.BlockSpec((c, RWKV_W), row_map(col_r + 2)), pl.BlockSpec((c, LORA_COLS), row_map(col_lo)),
                state_spec] + _rwkv_pre_weight_specs(lambda b, ci: (0, 0))
    vec_spec = pl.BlockSpec((None, RWKV_W, c), lambda b, ci: (b, 0, ci))
    vec_shape = jax.ShapeDtypeStruct((n_batch, RWKV_W, t), F32)
    return pl.pallas_call(
        _rwkv_pre_body,
        grid=(n_batch, n_chunks),
        in_specs=in_specs,
        out_specs=[vec_spec] * 5 + [pl.BlockSpec((c, RWKV_W), lambda b, ci: (b * n_chunks + ci, 0)), state_spec],
        out_shape=[vec_shape] * 5 + [jax.ShapeDtypeStruct((n_batch * t, RWKV_W), F32),
                                     jax.ShapeDtypeStruct((n_batch, 1, N_RWKV_COLS), F32)],
        scratch_shapes=[pltpu.VMEM((1, N_RWKV_COLS), F32)],
        compiler_params=_params("parallel", "arbitrary"),
    )(proj, proj, proj, proj, *_rwkv_pre_args(s_shift, n_batch, mu, w0, w2, a0, a2, g2))


def _rwkv_pre_short_body(r_ref, k_ref, v_ref, lo_ref, shift_ref, mu_ref, w0_ref, w2_ref, a0_ref, a2_ref, g2_ref,
                         ro_ref, ko_ref, vo_ref, wo_ref, ao_ref, go_ref, so_ref, *, n_b, t):
    rows = n_b * t
    first_tok = (lax.broadcasted_iota(jnp.int32, (rows, 1), 0) & (t - 1)) == 0

    def shifted(x_ref, col0):
        w = x_ref.shape[1]
        x = x_ref[...].astype(F32)
        carried = jnp.broadcast_to(shift_ref[:, :, col0:col0 + w].astype(F32), (n_b, t, w)).reshape(rows, w)
        prev = jnp.where(first_tok, carried, pltpu.roll(x, 1, 0))
        so_ref[:, :, col0:col0 + w] = x.reshape(n_b, t, w)[:, t - 1:t, :]
        return x + (prev - x) * mu_ref[:, col0:col0 + w]

    def put(o_ref, x):
        by_tok = jnp.swapaxes(x.reshape(n_b, t, RWKV_W), 0, 1)
        for ti in range(t):
            feat = by_tok[ti].T.reshape(RWKV_N, RWKV_HEADS, n_b)
            o_ref[:, ti] = jnp.swapaxes(feat, 0, 1)

    put(ro_ref, shifted(r_ref, 0))
    put(ko_ref, shifted(k_ref, RWKV_W))
    put(vo_ref, shifted(v_ref, 2 * RWKV_W))
    decay, rate, gate = _rwkv_lora_terms(shifted(lo_ref, 3 * RWKV_W), w0_ref, w2_ref, a0_ref, a2_ref, g2_ref)
    put(wo_ref, decay)
    put(ao_ref, rate)
    go_ref[...] = gate


def _rwkv_pre_short(proj, s_shift, mu, w0, w2, a0, a2, g2, *, row0, n_batch, t):
    rows = n_batch * t
    assert row0 % rows == 0 and t & (t - 1) == 0
    blk0 = row0 // rows
    col_r = N_RET_COLS // RWKV_W
    col_lo = (N_RET_COLS + 3 * RWKV_W) // LORA_COLS
    state_spec = pl.BlockSpec((n_batch, 1, N_RWKV_COLS), lambda i: (0, 0, 0))
    in_specs = [pl.BlockSpec((rows, RWKV_W), lambda i: (blk0, col_r)),
                pl.BlockSpec((rows, RWKV_W), lambda i: (blk0, col_r + 1)),
                pl.BlockSpec((rows, RWKV_W), lambda i: (blk0, col_r + 2)),
                pl.BlockSpec((rows, LORA_COLS), lambda i: (blk0, col_lo)),
                state_spec] + _rwkv_pre_weight_specs(lambda i: (0, 0))
    vec_shape = (RWKV_HEADS, t, RWKV_N, n_batch)
    vec_spec = pl.BlockSpec(vec_shape, lambda i: (0, 0, 0, 0))
    return pl.pallas_call(
        functools.partial(_rwkv_pre_short_body, n_b=n_batch, t=t),
        grid=(1,),
        in_specs=in_specs,
        out_specs=[vec_spec] * 5 + [pl.BlockSpec((rows, RWKV_W), lambda i: (0, 0)), state_spec],
        out_shape=[jax.ShapeDtypeStruct(vec_shape, F32)] * 5 + [
            jax.ShapeDtypeStruct((rows, RWKV_W), F32), jax.ShapeDtypeStruct((n_batch, 1, N_RWKV_COLS), F32)],
        compiler_params=_params("arbitrary"),
    )(proj, proj, proj, proj, *_rwkv_pre_args(s_shift, n_batch, mu, w0, w2, a0, a2, g2))


def _scan_body(r_ref, k_ref, w_ref, a_ref, v_ref, kk_ref, ka_ref, rk_ref, lw_ref, lb_ref, s0_ref,
               y_ref, sout_ref, s_scr, a_scr, b_scr, km_scr, *, tc):
    ci = pl.program_id(1)
    vr = RWKV_N

    @pl.when(ci == 0)
    def _():
        s_scr[...] = s0_ref[...].astype(F32)

    def ksum(x):
        return jnp.sum(x, axis=-2, keepdims=True)

    def vsum(x):
        return jnp.sum(x, axis=1, keepdims=True)

    kr = k_ref[...]
    a = a_ref[...]
    kk = kr * kk_ref[...]
    kk = kk / jnp.maximum(jnp.sqrt(ksum(kk * kk)), 1e-12)
    a_scr[...] = -kk
    b_scr[...] = kk * a
    km_scr[...] = kr * (1.0 + (a - 1.0) * ka_ref[...])

    def token(t, carry):
        r, w, avec, bvec, kmod = r_ref[t], w_ref[t], a_scr[t], b_scr[t], km_scr[t]

        def value_row(i, c2):
            s = s_scr[i]
            sa = ksum(s * avec)
            s = s * w + sa * bvec + v_ref[t, pl.ds(i, 1), :] * kmod
            s_scr[i] = s
            y_ref[t, pl.ds(i, 1), :] = ksum(s * r)
            return c2

        lax.fori_loop(0, vr, value_row, 0, unroll=16)
        return carry

    lax.fori_loop(0, tc, token, 0)

    y = y_ref[...]
    d = y - vsum(y) * (1.0 / RWKV_N)
    var = vsum(d * d) * (1.0 / RWKV_N)
    bonus = ksum(r_ref[...] * km_scr[...] * rk_ref[...])
    y_ref[...] = d * lax.rsqrt(var + LNX_EPS) * lw_ref[...] + lb_ref[...] + bonus * v_ref[...]

    @pl.when(ci == pl.num_programs(1) - 1)
    def _():
        sout_ref[...] = s_scr[...]


def _rwkv_scan(r, k, w, a, v, k_k, k_a, r_k, lnx_w, lnx_b, s0, *, tc):
    n_grp, t, _, lanes = r.shape
    vr = v.shape[2]
    assert lanes == LANES and t % tc == 0 and vr == RWKV_N

    def tok_spec(rows):
        return pl.BlockSpec((None, tc, rows, LANES), lambda g, ci: (g, ci, 0, 0))

    def par_spec(rows):
        return pl.BlockSpec((None, rows, LANES), lambda g, ci: (g, 0, 0))

    st_spec = pl.BlockSpec((None, vr, RWKV_N, LANES), lambda g, ci: (g, 0, 0, 0))
    key_scratch = pltpu.VMEM((tc, RWKV_N, LANES), F32)
    return pl.pallas_call(
        functools.partial(_scan_body, tc=tc),
        grid=(n_grp, t // tc),
        in_specs=[tok_spec(RWKV_N)] * 4 + [tok_spec(vr)] + [par_spec(RWKV_N)] * 3 + [par_spec(vr)] * 2 + [st_spec],
        out_specs=[tok_spec(vr), st_spec],
        out_shape=[jax.ShapeDtypeStruct((n_grp, t, vr, LANES), F32),
                   jax.ShapeDtypeStruct((n_grp, vr, RWKV_N, LANES), F32)],
        scratch_shapes=[pltpu.VMEM((vr, RWKV_N, LANES), F32), key_scratch, key_scratch, key_scratch],
        compiler_params=_params("parallel", "arbitrary"),
    )(r, k, w, a, v, k_k, k_a, r_k, lnx_w, lnx_b, s0)


SCAN_TC = 128
SCAN_SUB = 64
MEM_CHUNKS = 4


def _scan_prompt_body(r_ref, k_ref, w_ref, a_ref, v_ref, kk_ref, ka_ref, rk_ref, lw_ref, lb_ref, s0_ref,
                      memk_hbm, memv_hbm, y_ref, sout_ref, memk_out, memv_out,
                      s_scr, r_c, w_c, a_c, b_c, km_c, v_c, y_c, stash, mem_stage, mem_in_sem, mem_out_sem,
                      *, n_b, mem_seqs):
    ci = pl.program_id(0)
    vr = RWKV_N // 2
    ts = SCAN_SUB
    tile = RWKV_HEADS
    half_lanes = LANES // 2

    @pl.when(ci == 0)
    def _():
        s_scr[...] = s0_ref[...].astype(F32)

    chunk_seqs = mem_seqs // MEM_CHUNKS

    def mem_in(chunk, slot):
        copies = []
        for j in range(chunk_seqs):
            seq = (ci * MEM_CHUNKS + chunk) * chunk_seqs + j
            for h in range(MEM_HEADS):
                copies.append(pltpu.make_async_copy(memk_hbm.at[seq, :, h, :], mem_stage.at[slot, 0, j, h],
                                                    mem_in_sem.at[slot]))
                copies.append(pltpu.make_async_copy(memv_hbm.at[seq, :, h, :], mem_stage.at[slot, 1, j, h],
                                                    mem_in_sem.at[slot]))
        return copies

    def mem_out(chunk, slot):
        seqs = pl.ds((ci * MEM_CHUNKS + chunk) * chunk_seqs, chunk_seqs)
        return [pltpu.make_async_copy(mem_stage.at[slot, 0], memk_out.at[seqs], mem_out_sem.at[slot]),
                pltpu.make_async_copy(mem_stage.at[slot, 1], memv_out.at[seqs], mem_out_sem.at[slot])]

    def mem_phase(p):
        if 1 <= p <= MEM_CHUNKS:
            for cp in mem_in(p - 1, (p - 1) % 2):
                cp.wait()
            for cp in mem_out(p - 1, (p - 1) % 2):
                cp.start()
        if 2 <= p <= MEM_CHUNKS + 1:
            for cp in mem_out(p - 2, p % 2):
                cp.wait()
        if p < MEM_CHUNKS:
            for cp in mem_in(p, p % 2):
                cp.start()

    mem_phase(0)

    low = lax.broadcasted_iota(jnp.int32, (ts, LANES), 1) < half_lanes

    def feature_pair_rows(x_ref, base):
        tiles = [x_ref[b, pl.ds(base + f * tile, tile), :] for f in range(2) for b in range(n_b)]
        return jnp.concatenate(tiles, axis=0).T

    def key_to_chain(x_ref, dst, stash, t0):
        def group(g, c):
            rows = []
            for j in range(4):
                pair = g * 4 + j
                if t0 == 0:
                    full = feature_pair_rows(x_ref, pl.multiple_of(pair * 2 * tile, 2 * tile))
                    mt = full[:ts]
                    stash[pair] = full[ts:]
                else:
                    mt = stash[pair]
                sw = pltpu.roll(mt, half_lanes, 1)
                rows += [jnp.where(low, mt, sw), jnp.where(low, sw, mt)]
            dst[:, pl.ds(pl.multiple_of(g * 8, 8), 8), :] = jnp.swapaxes(jnp.stack(rows, axis=0), 0, 1)
            return c
        lax.fori_loop(0, RWKV_N // 8, group, 0, unroll=4)

    def value_to_chain(g, c):
        rows = [feature_pair_rows(v_ref, pl.multiple_of((g * 8 + j) * 2 * tile, 2 * tile)) for j in range(8)]
        v_c[:, pl.ds(pl.multiple_of(g * 8, 8), 8), :] = jnp.swapaxes(jnp.stack(rows, axis=0), 0, 1)
        return c
    lax.fori_loop(0, vr // 8, value_to_chain, 0, unroll=2)

    def ksum(x):
        return jnp.sum(x, axis=-2, keepdims=True)

    for t0 in range(0, SCAN_TC, ts):
        key_to_chain(r_ref, r_c, stash.at[0], t0)
        key_to_chain(w_ref, w_c, stash.at[1], t0)
        key_to_chain(k_ref, km_c, stash.at[2], t0)
        key_to_chain(a_ref, b_c, stash.at[3], t0)

        def prep(g8, c):
            toks = pl.ds(pl.multiple_of(g8 * 8, 8), 8)
            kr = km_c[toks]
            a = b_c[toks]
            kk = kr * kk_ref[...]
            kk = kk / jnp.maximum(jnp.sqrt(ksum(kk * kk)), 1e-12)
            a_c[toks] = -kk
            b_c[toks] = kk * a
            km_c[toks] = kr * (1.0 + (a - 1.0) * ka_ref[...])
            return c
        lax.fori_loop(0, ts // 8, prep, 0, unroll=2)
        mem_phase(1 + 2 * (t0 // ts))

        def token(t, carry):
            r, w, avec, bvec, kmod = r_c[t], w_c[t], a_c[t], b_c[t], km_c[t]

            def value_row(i, c2):
                s = s_scr[i]
                sa = ksum(s * avec)
                s = s * w + sa * bvec + v_c[t0 + t, pl.ds(i, 1), :] * kmod
                s_scr[i] = s
                y_c[t0 + t, pl.ds(i, 1), :] = ksum(s * r)
                return c2

            lax.fori_loop(0, vr, value_row, 0, unroll=True)
            return carry

        lax.fori_loop(0, ts, token, 0)
        mem_phase(2 + 2 * (t0 // ts))

        def post(g8, c):
            ktoks = pl.ds(pl.multiple_of(g8 * 8, 8), 8)
            vtoks = pl.ds(pl.multiple_of(t0 + g8 * 8, 8), 8)

            def vsum(x):
                x2 = x.reshape(8 * vr, LANES)
                x2 = x2 + pltpu.roll(x2, half_lanes, 1)
                return jnp.sum(x2.reshape(8, vr, LANES), axis=1, keepdims=True)

            y = y_c[vtoks]
            d = y - vsum(y) * (1.0 / RWKV_N)
            var = vsum(d * d) * (1.0 / RWKV_N)
            bonus = ksum(r_c[ktoks] * km_c[ktoks] * rk_ref[...])
            y_c[vtoks] = d * lax.rsqrt(var + LNX_EPS) * lw_ref[...] + lb_ref[...] + bonus * v_c[vtoks]
            return c
        lax.fori_loop(0, ts // 8, post, 0, unroll=4)

    def value_from_chain(g, c):
        blk = jnp.swapaxes(y_c[:, pl.ds(pl.multiple_of(g * 8, 8), 8), :], 0, 1)
        for j in range(8):
            mt = blk[j].T
            base = pl.multiple_of((g * 8 + j) * 2 * tile, 2 * tile)
            for hf in range(2):
                for b in range(n_b):
                    row0 = (hf * n_b + b) * tile
                    y_ref[b, pl.ds(base + hf * tile, tile), :] = mt[row0:row0 + tile, :]
        return c
    lax.fori_loop(0, vr // 8, value_from_chain, 0, unroll=2)

    assert 2 * (SCAN_TC // ts) == MEM_CHUNKS
    mem_phase(MEM_CHUNKS + 1)

    @pl.when(ci == pl.num_programs(0) - 1)
    def _():
        sout_ref[...] = s_scr[...]


def _rwkv_scan_prompt(r, k, w, a, v, k_k, k_a, r_k, lnx_w, lnx_b, s0, mem_k, mem_v):
    n_b, _, t = r.shape
    vr = RWKV_N // 2
    n_steps = t // SCAN_TC
    n_mem_seq = mem_k.shape[0]
    assert t % SCAN_TC == 0 and 2 * n_b * RWKV_HEADS == LANES and n_mem_seq % (n_steps * MEM_CHUNKS) == 0
    chunk_seqs = n_mem_seq // (n_steps * MEM_CHUNKS)
    any_spec = pl.BlockSpec(memory_space=pl.ANY)
    mem_shape = jax.ShapeDtypeStruct((n_mem_seq, MEM_HEADS, N_MEM, MEM_DH), F32)
    tok_spec = pl.BlockSpec((n_b, RWKV_W, SCAN_TC), lambda ci: (0, 0, ci))
    key_par = pl.BlockSpec((RWKV_N, LANES), lambda ci: (0, 0))
    val_par = pl.BlockSpec((vr, LANES), lambda ci: (0, 0))
    st_spec = pl.BlockSpec((vr, RWKV_N, LANES), lambda ci: (0, 0, 0))
    key_chain = pltpu.VMEM((SCAN_SUB, RWKV_N, LANES), F32)
    val_chain = pltpu.VMEM((SCAN_TC, vr, LANES), F32)
    return pl.pallas_call(
        functools.partial(_scan_prompt_body, n_b=n_b, mem_seqs=n_mem_seq // n_steps),
        grid=(n_steps,),
        in_specs=[tok_spec] * 5 + [key_par] * 3 + [val_par] * 2 + [st_spec, any_spec, any_spec],
        out_specs=[tok_spec, st_spec, any_spec, any_spec],
        out_shape=[jax.ShapeDtypeStruct((n_b, RWKV_W, t), F32),
                   jax.ShapeDtypeStruct((vr, RWKV_N, LANES), F32), mem_shape, mem_shape],
        scratch_shapes=([pltpu.VMEM((vr, RWKV_N, LANES), F32)] + [key_chain] * 5 + [val_chain] * 2
                        + [pltpu.VMEM((4, RWKV_N // 2, SCAN_TC - SCAN_SUB, LANES), F32),
                           pltpu.VMEM((2, 2, chunk_seqs, MEM_HEADS, N_MEM, MEM_DH), F32),
                           pltpu.SemaphoreType.DMA((2,)), pltpu.SemaphoreType.DMA((2,))]),
        compiler_params=_params("arbitrary"),
    )(r, k, w, a, v, k_k, k_a, r_k, lnx_w, lnx_b, s0, mem_k, mem_v)


def _merge_body(*refs, part_counts):
    n = len(part_counts)
    x_refs, oret_refs, g_refs = refs[:n], refs[n:2 * n], refs[2 * n:3 * n]
    yt_ref, ys_ref, wt_ref, wb_ref, gq_ref, wq_ref, o_ref, q_ref = refs[3 * n:]
    x = _read_row_parts(x_refs, part_counts)
    y = jnp.where(pl.program_id(0) >= part_counts[0], ys_ref[...], yt_ref[...].T)
    yb = (y * _read_row_parts(g_refs, part_counts)).astype(BF16)
    ob = _read_row_parts(oret_refs, part_counts).astype(BF16)
    n_chunk = MXU_COLS
    for j in range(0, D_MODEL, n_chunk):
        acc = jnp.dot(ob, wt_ref[:, j:j + n_chunk], preferred_element_type=F32)
        acc = acc + jnp.dot(yb, wb_ref[:, j:j + n_chunk], preferred_element_type=F32)
        o_ref[:, j:j + n_chunk] = x[:, j:j + n_chunk] + acc
    h = o_ref[...]
    hb = (h * lax.rsqrt(jnp.mean(h * h, axis=-1, keepdims=True) + EPS) * gq_ref[...]).astype(BF16)
    for j in range(0, D_MODEL, n_chunk):
        q_ref[:, j:j + n_chunk] = jnp.dot(hb, wq_ref[:, j:j + n_chunk], preferred_element_type=F32)


def _merge(x_parts, oret_parts, g_parts, y_first_t, y_second, w_ret, w_rwkv, gain_q, w_q):
    m = sum(part.shape[0] for part in x_parts)
    in_specs, part_counts = [], None
    for parts in (x_parts, oret_parts, g_parts):
        specs, part_counts = _row_part_specs(parts, TM)
        in_specs += specs
    assert len(part_counts) == 2
    n_first = part_counts[0]
    tiles = y_first_t.shape[2] // TM
    assert y_first_t.shape[0] * tiles == n_first
    yt_spec = pl.BlockSpec((None, RWKV_W, TM),
                           lambda i: (jnp.minimum(i, n_first - 1) // tiles, 0, jnp.minimum(i, n_first - 1) % tiles))
    ys_spec = pl.BlockSpec((TM, RWKV_W), lambda i: (jnp.clip(i - n_first, 0, part_counts[1] - 1), 0))
    wspec = pl.BlockSpec((RET_W, D_MODEL), lambda i: (0, 0))
    row_spec = pl.BlockSpec((TM, D_MODEL), lambda i: (i, 0))
    row_shape = jax.ShapeDtypeStruct((m, D_MODEL), F32)
    return pl.pallas_call(
        functools.partial(_merge_body, part_counts=part_counts),
        grid=(m // TM,),
        in_specs=in_specs + [yt_spec, ys_spec, wspec, wspec, pl.BlockSpec((1, D_MODEL), lambda i: (0, 0)),
                             pl.BlockSpec((D_MODEL, D_MODEL), lambda i: (0, 0))],
        out_specs=[row_spec, row_spec],
        out_shape=[row_shape, row_shape],
        compiler_params=_params("parallel"),
    )(*x_parts, *oret_parts, *g_parts, y_first_t, y_second, w_ret.astype(BF16), w_rwkv.astype(BF16),
      gain_q.reshape(1, D_MODEL), w_q.astype(BF16))


def _mem_kv_body(x_ref, g_ref, wk_ref, wv_ref, k_ref, v_ref, kh_ref, vh_ref, *, n_seq):
    x = x_ref[...].astype(F32)
    xb = (x * lax.rsqrt(jnp.mean(x * x, axis=-1, keepdims=True) + EPS) * g_ref[...]).astype(BF16)
    for w_ref, o_ref, oh_ref in ((wk_ref, k_ref, kh_ref), (wv_ref, v_ref, vh_ref)):
        for h in range(MEM_HEADS):
            acc = jnp.dot(xb, w_ref[:, h * MEM_DH:(h + 1) * MEM_DH], preferred_element_type=F32)
            o_ref[:, h, :] = acc
            for s in range(n_seq):
                oh_ref[s, h] = acc[s * N_MEM:(s + 1) * N_MEM]


def _mem_kv(mem, gain, w_k, w_v):
    n_b = mem.shape[0]
    n_seq = TM // N_MEM
    assert n_seq * N_MEM == TM and n_b % n_seq == 0
    wspec = pl.BlockSpec((D_MODEL, D_MODEL), lambda i: (0, 0))
    tok_spec = pl.BlockSpec((TM, MEM_HEADS, MEM_DH), lambda i: (i, 0, 0))
    head_spec = pl.BlockSpec((n_seq, MEM_HEADS, N_MEM, MEM_DH), lambda i: (i, 0, 0, 0))
    tok_shape = jax.ShapeDtypeStruct((n_b * N_MEM, MEM_HEADS, MEM_DH), F32)
    head_shape = jax.ShapeDtypeStruct((n_b, MEM_HEADS, N_MEM, MEM_DH), F32)
    return pl.pallas_call(
        functools.partial(_mem_kv_body, n_seq=n_seq),
        grid=(n_b // n_seq,),
        in_specs=[pl.BlockSpec((TM, D_MODEL), lambda i: (i, 0)), pl.BlockSpec((1, D_MODEL), lambda i: (0, 0)),
                  wspec, wspec],
        out_specs=[tok_spec, tok_spec, head_spec, head_spec],
        out_shape=[tok_shape, tok_shape, head_shape, head_shape],
        compiler_params=_params("parallel"),
    )(mem.reshape(n_b * N_MEM, D_MODEL), gain.reshape(1, D_MODEL), w_k.astype(BF16), w_v.astype(BF16))


def _attn_body(q_ref, k_ref, v_ref, o_ref, *, n_seq, tq):
    nt = (((1,), (1,)), ((), ()))
    for g in range(n_seq):
        rows = slice(g * tq, (g + 1) * tq)
        q = q_ref[rows, :].astype(BF16)
        s = lax.dot_general(q, k_ref[g].astype(BF16), nt, preferred_element_type=F32) * (MEM_DH ** -0.5)
        p = jnp.exp(s - jnp.max(s, axis=-1, keepdims=True))
        l = jnp.sum(p, axis=-1, keepdims=True)
        o = jnp.dot(p.astype(BF16), v_ref[g].astype(BF16), preferred_element_type=F32)
        o_ref[rows, :] = o / l


def _attention(q, mem_k, mem_v, *, row0, n_batch, t, n_seq, tq):
    q_tiles = t // tq
    rows = n_seq * tq
    assert t % tq == 0 and n_batch % n_seq == 0 and row0 % rows == 0 and (n_seq == 1 or q_tiles == 1)
    blk0 = row0 // rows
    kv_spec = pl.BlockSpec((n_seq, None, N_MEM, MEM_DH), lambda b, h, qi: (b, h, 0, 0))
    return pl.pallas_call(
        functools.partial(_attn_body, n_seq=n_seq, tq=tq),
        grid=(n_batch // n_seq, MEM_HEADS, q_tiles),
        in_specs=[pl.BlockSpec((rows, MEM_DH), lambda b, h, qi: (blk0 + b * q_tiles + qi, h)), kv_spec, kv_spec],
        out_specs=pl.BlockSpec((rows, MEM_DH), lambda b, h, qi: (b * q_tiles + qi, h)),
        out_shape=jax.ShapeDtypeStruct((n_batch * t, D_MODEL), F32),
        compiler_params=_params("parallel", "parallel", "parallel"),
    )(q, mem_k, mem_v)


ROW_TILE = (D_MODEL // LANES, LANES)


def _rows_to_tiles(x):
    chunks = [x[:, j * LANES:(j + 1) * LANES] for j in range(ROW_TILE[0])]
    return jnp.swapaxes(jnp.stack(chunks, axis=0), 0, 1)


def _tiles_to_rows(x):
    chunks = jnp.swapaxes(x, 0, 1)
    return jnp.concatenate([chunks[j] for j in range(ROW_TILE[0])], axis=1)


def _attn_out_body(*refs, part_counts):
    n = len(part_counts)
    w_ref, res_ref, o_ref = refs[n:]
    ab = _read_row_parts(refs[:n], part_counts).astype(BF16)
    n_chunk = MXU_COLS
    for j in range(0, D_MODEL, n_chunk):
        o_ref[:, j:j + n_chunk] = res_ref[:, j:j + n_chunk] + jnp.dot(
            ab, w_ref[:, j:j + n_chunk], preferred_element_type=F32)


def _attn_out(att_parts, w_mo, residual):
    m = residual.shape[0]
    att_specs, part_counts = _row_part_specs(att_parts, TM)
    row = pl.BlockSpec((TM, D_MODEL), lambda i: (i, 0))
    return pl.pallas_call(
        functools.partial(_attn_out_body, part_counts=part_counts),
        grid=(m // TM,),
        in_specs=att_specs + [pl.BlockSpec((D_MODEL, D_MODEL), lambda i: (0, 0)), row],
        out_specs=row,
        out_shape=jax.ShapeDtypeStruct((m, D_MODEL), F32),
        compiler_params=_params("parallel"),
    )(*att_parts, w_mo.astype(BF16), residual)


def _router_body(h_ref, g_ref, w_ref, b_ref, hn_ref, ids_ref, comb_ref):
    x = h_ref[...]
    hn = x * lax.rsqrt(jnp.mean(x * x, axis=-1, keepdims=True) + EPS) * g_ref[...]
    hn_ref[...] = _rows_to_tiles(hn)
    logits = jnp.dot(hn, w_ref[...], precision=lax.Precision.HIGHEST, preferred_element_type=F32) + b_ref[...]
    lane = lax.broadcasted_iota(jnp.int32, logits.shape, 1).astype(F32)
    neg = -jnp.inf

    def first_argmax(vals):
        m = jnp.max(vals, axis=-1, keepdims=True)
        return m, jnp.min(jnp.where(vals == m, lane, float(LANES)), axis=-1, keepdims=True)

    gl = jnp.where(lane < N_GROUPS, logits, neg)
    gmax, gsel = first_argmax(gl)
    pg_sel = 1.0 / jnp.sum(jnp.exp(gl - gmax), axis=-1, keepdims=True)
    e0 = N_GROUPS + gsel * EXP_PER_GROUP
    el = jnp.where((lane >= e0) & (lane < e0 + EXP_PER_GROUP), logits, neg)
    m1, i1 = first_argmax(el)
    m2, i2 = first_argmax(jnp.where(lane == i1, neg, el))
    e21 = jnp.exp(m2 - m1)
    c1 = pg_sel / (1.0 + e21)
    c2 = c1 * e21
    ids = jnp.where(lane == 0, i1 - N_GROUPS, jnp.where(lane == 1, i2 - N_GROUPS, 0.0))
    ids_ref[...] = ids.astype(jnp.int32)
    comb_ref[...] = jnp.where(lane == 0, c1, jnp.where(lane == 1, c2, 0.0))


def _router(h, g_ffn, w_gr, b_gr, w_er, b_er):
    m = h.shape[0]
    pad = LANES - N_GROUPS - N_EXPERTS
    w = jnp.concatenate([w_gr, w_er, jnp.zeros((D_MODEL, pad), F32)], axis=1)
    b = jnp.concatenate([b_gr, b_er, jnp.zeros((pad,), F32)]).reshape(1, LANES)
    row = lambda n: pl.BlockSpec((TM, n), lambda i: (i, 0))
    return pl.pallas_call(
        _router_body,
        grid=(m // TM,),
        in_specs=[row(D_MODEL), pl.BlockSpec((1, D_MODEL), lambda i: (0, 0)),
                  pl.BlockSpec((D_MODEL, LANES), lambda i: (0, 0)), pl.BlockSpec((1, LANES), lambda i: (0, 0))],
        out_specs=[pl.BlockSpec((TM,) + ROW_TILE, lambda i: (i, 0, 0)), row(LANES), row(LANES)],
        out_shape=[jax.ShapeDtypeStruct((m,) + ROW_TILE, F32), jax.ShapeDtypeStruct((m, LANES), jnp.int32),
                   jax.ShapeDtypeStruct((m, LANES), F32)],
        compiler_params=_params("parallel"),
    )(h, g_ffn.reshape(1, D_MODEL), w, b)


def _dispatch_body(pad_start_ref, pad_count_ref, dest_ref, hn_ref, sorted_out, zeros, sem, *, n_pad_rows):
    @pl.when(pl.program_id(0) == 0)
    def _():
        zeros[...] = jnp.zeros(zeros.shape, F32)

        def fill_tail(e, c):
            count = pad_count_ref[e]
            row = pad_start_ref[e]
            bit = MOE_ROWS // 2
            while bit:
                @pl.when((count & bit) != 0)
                def _(row=row, bit=bit):
                    pltpu.make_async_copy(zeros.at[pl.ds(0, bit)], sorted_out.at[pl.ds(row, bit)],
                                          sem.at[TOP_K]).start()
                row = row + (count & bit)
                bit //= 2
            return c
        lax.fori_loop(0, N_EXPERTS, fill_tail, 0)

        def fill_block(j, c):
            pltpu.make_async_copy(zeros, sorted_out.at[pl.ds(pad_start_ref[N_EXPERTS] + j * MOE_ROWS, MOE_ROWS)],
                                  sem.at[TOP_K]).start()
            return c
        lax.fori_loop(0, pad_count_ref[N_EXPERTS] // MOE_ROWS, fill_block, 0)

    def issue(r, c):
        for k in range(TOP_K):
            pltpu.make_async_copy(hn_ref.at[r], sorted_out.at[dest_ref[0, k, r]], sem.at[k]).start(priority=k)
        return c
    lax.fori_loop(0, TM, issue, 0, unroll=8)
    for k in range(TOP_K):
        pltpu.make_async_copy(hn_ref, sorted_out.at[pl.ds(0, TM)], sem.at[k]).wait()

    @pl.when(pl.program_id(0) == pl.num_programs(0) - 1)
    def _():
        def drain(j, c):
            pltpu.make_async_copy(zeros, sorted_out.at[pl.ds(0, MOE_ROWS)], sem.at[TOP_K]).wait()
            return c
        lax.fori_loop(0, n_pad_rows // MOE_ROWS, drain, 0)


def _dispatch(hn, dest, pad_start, pad_count, n_sorted):
    n_tok = hn.shape[0]
    n_pad_rows = n_sorted - TOP_K * n_tok
    assert n_pad_rows % MOE_ROWS == 0
    grid_spec = pltpu.PrefetchScalarGridSpec(
        num_scalar_prefetch=2,
        grid=(n_tok // TM,),
        in_specs=[pl.BlockSpec((1, TOP_K, TM), lambda i, ps, pc: (i, 0, 0), memory_space=pltpu.SMEM),
                  pl.BlockSpec((TM,) + ROW_TILE, lambda i, ps, pc: (i, 0, 0))],
        out_specs=pl.BlockSpec(memory_space=pl.ANY),
        scratch_shapes=[pltpu.VMEM((MOE_ROWS,) + ROW_TILE, F32), pltpu.SemaphoreType.DMA((TOP_K + 1,))],
    )
    return pl.pallas_call(
        functools.partial(_dispatch_body, n_pad_rows=n_pad_rows),
        grid_spec=grid_spec,
        out_shape=jax.ShapeDtypeStruct((n_sorted,) + ROW_TILE, F32),
        compiler_params=_params("arbitrary"),
    )(pad_start, pad_count, dest, hn)


def _expert_body(blk_e_ref, x_ref, wg_ref, wu_ref, wd_ref, o_ref):
    del blk_e_ref
    x = _tiles_to_rows(x_ref[...]).astype(BF16)
    hg = jnp.dot(x, wg_ref[0].astype(BF16), preferred_element_type=F32)
    hu = jnp.dot(x, wu_ref[0].astype(BF16), preferred_element_type=F32)
    act = (hg * jax.nn.sigmoid(hg) * hu).astype(BF16)
    o_ref[...] = _rows_to_tiles(jnp.dot(act, wd_ref[0].astype(BF16), preferred_element_type=F32))


def _experts(x_sorted, blk_e, w_gate, w_up, w_down):
    n_blocks = blk_e.shape[0]
    row_spec = pl.BlockSpec((MOE_ROWS,) + ROW_TILE, lambda i, be: (i, 0, 0))
    grid_spec = pltpu.PrefetchScalarGridSpec(
        num_scalar_prefetch=1,
        grid=(n_blocks,),
        in_specs=[
            row_spec,
            pl.BlockSpec((1, D_MODEL, D_EXPERT), lambda i, be: (be[i], 0, 0)),
            pl.BlockSpec((1, D_MODEL, D_EXPERT), lambda i, be: (be[i], 0, 0)),
            pl.BlockSpec((1, D_EXPERT, D_MODEL), lambda i, be: (be[i], 0, 0)),
        ],
        out_specs=row_spec,
    )
    return pl.pallas_call(
        _expert_body,
        grid_spec=grid_spec,
        out_shape=jax.ShapeDtypeStruct(x_sorted.shape, F32),
        compiler_params=_params("arbitrary"),
    )(blk_e, x_sorted, w_gate, w_up, w_down)


def _route_plan(ids):
    n_tok = ids.shape[0]
    n_pairs = ids.size
    n_blocks = -(-(n_pairs + N_EXPERTS * (MOE_ROWS - 1)) // MOE_ROWS)
    flat_e = ids.reshape(n_pairs)
    onehot = (flat_e[:, None] == jnp.arange(N_EXPERTS, dtype=jnp.int32)[None, :]).astype(jnp.int32)
    csum = jnp.cumsum(onehot, axis=0)
    rank = jnp.sum(onehot * csum, axis=1) - 1
    counts = csum[-1]
    pcounts = (counts + MOE_ROWS - 1) // MOE_ROWS * MOE_ROWS
    pends = jnp.cumsum(pcounts)
    pstarts = pends - pcounts
    dest = jnp.sum(onehot * pstarts[None, :], axis=1) + rank
    block_start = jnp.arange(n_blocks, dtype=jnp.int32) * MOE_ROWS
    blk_e = jnp.minimum(jnp.sum((block_start[:, None] >= pends[None, :]).astype(jnp.int32), axis=1),
                        N_EXPERTS - 1).astype(jnp.int32)
    dest = dest.astype(jnp.int32).reshape(n_tok // TM, TM, TOP_K).transpose(0, 2, 1)
    n_sorted = n_blocks * MOE_ROWS
    pad_start = jnp.concatenate([pstarts + counts, pends[-1:]]).astype(jnp.int32)
    pad_count = jnp.concatenate([pcounts - counts, n_sorted - pends[-1:]]).astype(jnp.int32)
    return blk_e, dest, pad_start, pad_count, n_sorted


def _final_body(dest_ref, dest_next_ref, h_ref, comb_ref, g_ref, y_hbm, o_first, o_second, ybuf, sem, *, n_first):
    i = pl.program_id(0)
    n = pl.num_programs(0)
    slot = i % 2

    def start_gather(ref, sl):
        def issue(r, c):
            for k in range(TOP_K):
                pltpu.make_async_copy(y_hbm.at[ref[0, k, r]], ybuf.at[sl, k, r], sem.at[sl]).start(priority=k)
            return c
        lax.fori_loop(0, TM, issue, 0, unroll=8)

    @pl.when(i == 0)
    def _():
        start_gather(dest_ref, 0)

    @pl.when(i + 1 < n)
    def _():
        start_gather(dest_next_ref, 1 - slot)

    for k in range(TOP_K):
        pltpu.make_async_copy(y_hbm.at[pl.ds(0, TM)], ybuf.at[slot, k], sem.at[slot]).wait()

    first, second = _tiles_to_rows(ybuf[slot, 0]), _tiles_to_rows(ybuf[slot, 1])
    x = h_ref[...] + (first * comb_ref[:, 0:1] + second * comb_ref[:, 1:2])
    out = x * lax.rsqrt(jnp.mean(x * x, axis=-1, keepdims=True) + EPS) * g_ref[...]

    @pl.when(i < n_first)
    def _():
        o_first[...] = out

    @pl.when(i >= n_first)
    def _():
        o_second[...] = out


def _final(h, y_sorted, dest, comb, g_final, *, n_first_rows):
    n_rows = h.shape[0]
    assert n_rows % TM == 0 and n_first_rows % TM == 0 and 0 < n_first_rows < n_rows
    n_steps, n_first = n_rows // TM, n_first_rows // TM
    row = lambda n: pl.BlockSpec((TM, n), lambda i: (i, 0))
    dest_spec = lambda f: pl.BlockSpec((1, TOP_K, TM), f, memory_space=pltpu.SMEM)
    return pl.pallas_call(
        functools.partial(_final_body, n_first=n_first),
        grid=(n_steps,),
        in_specs=[dest_spec(lambda i: (i, 0, 0)),
                  dest_spec(lambda i: (jnp.minimum(i + 1, n_steps - 1), 0, 0)),
                  row(D_MODEL), row(LANES), pl.BlockSpec((1, D_MODEL), lambda i: (0, 0)),
                  pl.BlockSpec(memory_space=pl.ANY)],
        out_specs=[pl.BlockSpec((TM, D_MODEL), lambda i: (jnp.minimum(i, n_first - 1), 0)),
                   pl.BlockSpec((TM, D_MODEL), lambda i: (jnp.maximum(i - n_first, 0), 0))],
        out_shape=[jax.ShapeDtypeStruct((n_first_rows, D_MODEL), F32),
                   jax.ShapeDtypeStruct((n_rows - n_first_rows, D_MODEL), F32)],
        scratch_shapes=[pltpu.VMEM((2, TOP_K, TM) + ROW_TILE, F32), pltpu.SemaphoreType.DMA((2,))],
        compiler_params=_params("arbitrary"),
    )(dest, dest, h, comb, g_final.reshape(1, D_MODEL), y_sorted)


def _reorder_last(x, shape, order):
    lead = x.shape[:-1]
    n = len(lead)
    y = x.reshape(lead + shape).transpose(tuple(range(n)) + tuple(n + o for o in order))
    return y.reshape(lead + (x.shape[-1],))


HALF_N = RWKV_N // 2


def _key_major(x):
    return _reorder_last(x, (RWKV_HEADS, RWKV_N), (1, 0))


def _key_major_inv(x):
    return _reorder_last(x, (RWKV_N, RWKV_HEADS), (1, 0))


def _value_major(x):
    return _reorder_last(x, (RWKV_HEADS, 2, HALF_N), (2, 1, 0))


def _value_major_inv(x):
    return _reorder_last(x, (HALF_N, 2, RWKV_HEADS), (2, 1, 0))


def _rwkv_cols(x, key_fn, value_fn):
    return jnp.concatenate([key_fn(x[..., :RWKV_W]), key_fn(x[..., RWKV_W:2 * RWKV_W]),
                            value_fn(x[..., 2 * RWKV_W:3 * RWKV_W]), x[..., 3 * RWKV_W:]], axis=-1)


def kernel(x_prompt, x_sample, mem_prompt, state_ret, state_rwkv, state_shift, cache_mem_k, cache_mem_v,
           g_mix, w_in, ret_gn, rwkv_mu, rwkv_w0, rwkv_w2, rwkv_a0, rwkv_a2, rwkv_g2, rwkv_k_k, rwkv_k_a,
           rwkv_r_k, rwkv_lnx_w, rwkv_lnx_b, w_out, g_mem_q, g_mem_kv, w_mq, w_mk, w_mv, w_mo, g_ffn,
           w_group_router, b_group_router, w_expert_router, b_expert_router, w_e_gate, w_e_up, w_e_down,
           g_final):
    assert w_in.shape[0] == 1, "single-layer decoder"
    bp, tp, d = x_prompt.shape
    bs, ts, _ = x_sample.shape
    np_tok, ns_tok = bp * tp, bs * ts
    assert d == D_MODEL and bp * RWKV_HEADS * 2 == LANES and bs == LANES
    l = 0
    x_parts = [x_prompt.reshape(np_tok, d), x_sample.reshape(ns_tok, d)]

    w_in_l = jnp.concatenate([w_in[l][:, :N_RET_COLS], _rwkv_cols(w_in[l][:, N_RET_COLS:], _key_major, _value_major)],
                             axis=1)
    proj = _input_projection(x_parts, g_mix[l], w_in_l)

    pos_p = np.arange(tp)
    pos_s = PAST_LEN + np.arange(ts)
    zero_ret = jnp.zeros((bp, RET_HEADS, RET_DK, RET_DV), F32)
    oret_p, sret_p = _retention(proj, zero_ret, ret_gn[l], pos_p, row0=0, n_batch=bp, t=tp,
                                n_blk=RET_SEQS_LONG, per_blk=1)
    oret_s, sret_s = _retention(proj, state_ret[l], ret_gn[l], pos_s, row0=np_tok, n_batch=bs, t=ts,
                                n_blk=1, per_blk=RET_SEQS_SHORT)
    oret_p, oret_s = oret_p.reshape(np_tok, RET_W), oret_s.reshape(ns_tok, RET_W)

    pre_w = (_rwkv_cols(rwkv_mu[l], _key_major, _value_major), _key_major(rwkv_w0[l]), _key_major(rwkv_w2[l]),
             _key_major(rwkv_a0[l]), _key_major(rwkv_a2[l]), _value_major(rwkv_g2[l]))
    zero_shift = jnp.zeros((bp, N_RWKV_COLS), F32)
    shift_in = _rwkv_cols(state_shift[l], _key_major, _value_major)
    r_p, k_p, v_p, w_p, a_p, gate_p, shift_p = _rwkv_pre(proj, zero_shift, *pre_w, row0=0, n_batch=bp, t=tp,
                                                          c=PRE_CHUNK)
    r_s, k_s, v_s, w_s, a_s, gate_s, shift_s = _rwkv_pre_short(proj, shift_in, *pre_w, row0=np_tok, n_batch=bs,
                                                                t=ts)

    kvec = lambda v: v.reshape(RWKV_HEADS, RWKV_N)
    key_par = lambda v: jnp.broadcast_to(kvec(v).T[:, None, None, :], (RWKV_N, 2, bp, RWKV_HEADS)).reshape(
        RWKV_N, LANES)

    val_par = lambda v: jnp.broadcast_to(
        v.reshape(RWKV_HEADS, 2, HALF_N).transpose(2, 1, 0)[:, :, None, :],
        (HALF_N, 2, bp, RWKV_HEADS)).reshape(HALF_N, LANES)
    mem_shape = (N_MEM, MEM_HEADS, MEM_DH)
    y_p, srw_p, cache_k_heads, cache_v_heads = _rwkv_scan_prompt(
        r_p, k_p, w_p, a_p, v_p, key_par(rwkv_k_k[l]), key_par(rwkv_k_a[l]), key_par(rwkv_r_k[l]),
        val_par(rwkv_lnx_w[l]), val_par(rwkv_lnx_b[l]), jnp.zeros((HALF_N, RWKV_N, LANES), F32),
        cache_mem_k.reshape(bs, *mem_shape), cache_mem_v.reshape(bs, *mem_shape))
    srw_p = srw_p.reshape(HALF_N, RWKV_N, 2, bp, RWKV_HEADS).transpose(3, 4, 2, 0, 1).reshape(
        bp, RWKV_HEADS, RWKV_N, RWKV_N)
    head_par = lambda v: jnp.broadcast_to(kvec(v)[:, :, None], (RWKV_HEADS, RWKV_N, LANES))
    val_rows = lambda v: v.reshape(RWKV_HEADS, 2, HALF_N).transpose(0, 2, 1).reshape(RWKV_HEADS * RWKV_N)
    state_s = state_rwkv[l].astype(F32).reshape(bs, RWKV_HEADS, 2, HALF_N, RWKV_N).transpose(1, 3, 2, 4, 0)
    y_s, srw_s = _rwkv_scan(
        r_s, k_s, w_s, a_s, v_s, head_par(rwkv_k_k[l]), head_par(rwkv_k_a[l]), head_par(rwkv_r_k[l]),
        head_par(val_rows(rwkv_lnx_w[l])), head_par(val_rows(rwkv_lnx_b[l])),
        state_s.reshape(RWKV_HEADS, RWKV_N, RWKV_N, bs), tc=ts)
    y_s = y_s.transpose(3, 1, 2, 0).reshape(ns_tok, RWKV_W)
    srw_s = srw_s.reshape(RWKV_HEADS, HALF_N, 2, RWKV_N, bs).transpose(4, 0, 2, 1, 3).reshape(
        bs, RWKV_HEADS, RWKV_N, RWKV_N)

    w_rwkv_out = _value_major(w_out[l][RET_W:].T).T
    h, q = _merge(x_parts, [oret_p, oret_s], [gate_p, gate_s], y_p, y_s, w_out[l][:RET_W], w_rwkv_out,
                  g_mem_q[l], w_mq[l])

    mk, mv, mk_heads, mv_heads = _mem_kv(mem_prompt, g_mem_kv[l], w_mk[l], w_mv[l])
    att_p = _attention(q, mk_heads, mv_heads, row0=0, n_batch=bp, t=tp, n_seq=1, tq=tp)
    att_s = _attention(q, cache_k_heads, cache_v_heads, row0=np_tok, n_batch=bs, t=ts,
                       n_seq=ATTN_SEQS_SHORT, tq=ts)

    h = _attn_out([att_p, att_s], w_mo[l], h)

    hn, ids, comb = _router(h, g_ffn[l], w_group_router[l], b_group_router[l], w_expert_router[l],
                            b_expert_router[l])
    blk_e, dest, pad_start, pad_count, n_sorted = _route_plan(ids[:, :TOP_K])
    x_sorted = _dispatch(hn, dest, pad_start, pad_count, n_sorted)
    y_sorted = _experts(x_sorted, blk_e, w_e_gate[l], w_e_up[l], w_e_down[l])
    y_prompt, y_sample = _final(h, y_sorted, dest, comb, g_final, n_first_rows=np_tok)
    y_prompt = y_prompt.reshape(bp, tp, d)
    y_sample = y_sample.reshape(bs, ts, d)

    shift_p = _rwkv_cols(shift_p.reshape(bp, N_RWKV_COLS), _key_major_inv, _value_major_inv)
    shift_s = _rwkv_cols(shift_s.reshape(bs, N_RWKV_COLS), _key_major_inv, _value_major_inv)
    return (y_prompt, y_sample, sret_p[None], srw_p[None], shift_p[None],
            mk.reshape(1, bp, *mem_shape), mv.reshape(1, bp, *mem_shape),
            sret_s[None], srw_s[None], shift_s[None])
```

```python
import functools

import numpy as np
import jax
import jax.numpy as jnp
from jax import lax
from jax.experimental import pallas as pl
from jax.experimental.pallas import tpu as pltpu

F32 = jnp.float32
BF16 = jnp.bfloat16

D_MODEL = 1024
PAST_LEN = 16384
N_MEM = 256
MEM_HEADS = 4
MEM_DH = D_MODEL // MEM_HEADS
RET_HEADS = 4
RET_W = D_MODEL // 2
RET_DV = RET_W // RET_HEADS
RET_DK = RET_DV // 2
RET_QK = RET_HEADS * RET_DK
RET_CHUNK = 128
ROPE_BASE = 10000.0
RWKV_N = 64
RWKV_W = D_MODEL - RET_W
RWKV_HEADS = RWKV_W // RWKV_N
LORA_W = 64
LORA_A = 64
LORA_G = 128
LNX_EPS = 64e-5
N_RET_COLS = 2 * RET_QK + 2 * RET_W
N_RWKV_COLS = 3 * RWKV_W + LORA_W + LORA_A + LORA_G
N_IN_COLS = N_RET_COLS + N_RWKV_COLS
N_GROUPS = 4
EXP_PER_GROUP = 8
N_EXPERTS = N_GROUPS * EXP_PER_GROUP
TOP_K = 2
D_EXPERT = D_MODEL // 2
EPS = 1e-6

LANES = 128
MXU_COLS = 256
MOE_ROWS = 512
TM = 512
PRE_CHUNK = 512
RET_SEQS_LONG = 8
RET_SEQS_SHORT = 16
ATTN_SEQS_SHORT = 32


def _params(*sem):
    return pltpu.CompilerParams(dimension_semantics=sem)


def _row_part_specs(parts, tm):
    specs, counts, start = [], [], 0
    for part in parts:
        nb = part.shape[0] // tm
        assert nb * tm == part.shape[0]
        specs.append(pl.BlockSpec((tm, part.shape[1]), lambda i, s=start, n=nb: (jnp.clip(i - s, 0, n - 1), 0)))
        counts.append(nb)
        start += nb
    return specs, counts


def _read_row_parts(refs, counts):
    i = pl.program_id(0)
    x = refs[0][...]
    start = counts[0]
    for ref, nb in zip(refs[1:], counts[1:]):
        x = jnp.where(i >= start, ref[...], x)
        start += nb
    return x


def _proj_body(*refs, part_counts):
    n = len(part_counts)
    g_ref, w_ref, o_ref = refs[n:]
    x = _read_row_parts(refs[:n], part_counts).astype(F32)
    xb = (x * lax.rsqrt(jnp.mean(x * x, axis=-1, keepdims=True) + EPS) * g_ref[...]).astype(BF16)
    n_chunk = MXU_COLS
    for j in range(0, w_ref.shape[1], n_chunk):
        o_ref[:, j:j + n_chunk] = jnp.dot(xb, w_ref[:, j:j + n_chunk], preferred_element_type=F32)


def _input_projection(x_parts, gain, w):
    k, n_out = w.shape
    m = sum(part.shape[0] for part in x_parts)
    assert n_out % MXU_COLS == 0
    in_specs, part_counts = _row_part_specs(x_parts, TM)
    return pl.pallas_call(
        functools.partial(_proj_body, part_counts=part_counts),
        grid=(m // TM,),
        in_specs=in_specs + [pl.BlockSpec((1, k), lambda i: (0, 0)), pl.BlockSpec((k, n_out), lambda i: (0, 0))],
        out_specs=pl.BlockSpec((TM, n_out), lambda i: (i, 0)),
        out_shape=jax.ShapeDtypeStruct((m, n_out), F32),
        compiler_params=_params("parallel"),
    )(*x_parts, gain.reshape(1, k).astype(F32), w.astype(BF16))


def _rot_tables(pos):
    half = RET_DK // 2
    inv_freq = ROPE_BASE ** (-(np.arange(half, dtype=np.float64) / half))
    ang = pos.astype(np.float64)[:, None] * inv_freq[None, :]
    cos, sin = np.cos(ang), np.sin(ang)
    zero = np.zeros_like(sin)
    c = np.tile(np.concatenate([cos, cos], axis=1), (1, RET_HEADS))
    s_lo = np.tile(np.concatenate([-sin, zero], axis=1), (1, RET_HEADS))
    s_hi = np.tile(np.concatenate([zero, sin], axis=1), (1, RET_HEADS))
    return [jnp.asarray(t, F32) for t in (c, s_lo, s_hi)]


def _ret_decay_tables(c):
    lg = np.log1p(-np.exp2(-5.0 - np.arange(RET_HEADS, dtype=np.float64)))
    idx = np.arange(c, dtype=np.float64)
    diff = idx[:, None] - idx[None, :]
    mask = np.where(diff[None] >= 0, np.exp(np.maximum(diff, 0.0)[None] * lg[:, None, None]), 0.0)
    q_dec = np.repeat(np.exp((idx[:, None] + 1.0) * lg[None, :]), RET_DV, axis=1)
    k_dec = np.repeat(np.exp((c - 1.0 - idx)[:, None] * lg[None, :]), RET_DK, axis=1)
    c_dec = [float(v) for v in np.exp(c * lg)]
    return jnp.asarray(mask, F32), jnp.asarray(q_dec, F32), jnp.asarray(k_dec, F32), c_dec


def _ret_body(*refs, n_blk, per_blk, c, c_dec):
    q_refs, k_refs, v_refs, gate_refs = (refs[j * n_blk:(j + 1) * n_blk] for j in range(4))
    (c_ref, slo_ref, shi_ref, mask_ref, qdec_ref, kdec_ref, gn_ref, s0_ref, o_ref, sout_ref,
     s_scr) = refs[4 * n_blk:]
    n_seq = n_blk * per_blk
    ci = pl.program_id(1)

    @pl.when(ci == 0)
    def _():
        s_scr[...] = s0_ref[...].astype(F32)

    cos, s_lo, s_hi = c_ref[...], slo_ref[...], shi_ref[...]
    half = RET_DK // 2

    def rope(x):
        return x * cos + pltpu.roll(x, RET_QK - half, 1) * s_lo + pltpu.roll(x, half, 1) * s_hi

    nt = (((1,), (1,)), ((), ()))
    tn = (((0,), (0,)), ((), ()))
    for g in range(n_seq):
        blk = g // per_blk
        rows = slice((g % per_blk) * c, (g % per_blk + 1) * c)
        q = rope(q_refs[blk][rows, :].astype(F32))
        k = rope(k_refs[blk][rows, :].astype(F32)) * (RET_DK ** -0.5)
        k_st = k * kdec_ref[...]
        for h in range(RET_HEADS):
            kc = slice(h * RET_DK, (h + 1) * RET_DK)
            vc = slice(h * RET_DV, (h + 1) * RET_DV)
            qh = q[:, kc].astype(BF16)
            vh = v_refs[blk][rows, vc].astype(BF16)
            s_h = s_scr[g, h]
            att = lax.dot_general(qh, k[:, kc].astype(BF16), nt, preferred_element_type=F32) * mask_ref[h]
            o = jnp.dot(att.astype(BF16), vh, preferred_element_type=F32)
            o = o + jnp.dot(qh, s_h.astype(BF16), preferred_element_type=F32) * qdec_ref[:, vc]
            s_scr[g, h] = s_h * c_dec[h] + lax.dot_general(
                k_st[:, kc].astype(BF16), vh, tn, preferred_element_type=F32)
            o = o * lax.rsqrt(jnp.mean(o * o, axis=-1, keepdims=True) + EPS)
            gate = gate_refs[blk][rows, vc].astype(F32)
            o_ref[g, :, vc] = o * gn_ref[:, vc] * (gate * jax.nn.sigmoid(gate))

    @pl.when(ci == pl.num_programs(1) - 1)
    def _():
        sout_ref[...] = s_scr[...]


def _retention(proj, s0, ret_gn, pos, *, row0, n_batch, t, n_blk, per_blk):
    c = RET_CHUNK if t % RET_CHUNK == 0 else t
    n_chunks = t // c
    rows = per_blk * c
    n_seq = n_blk * per_blk
    assert n_batch % n_seq == 0 and row0 % rows == 0 and (per_blk == 1 or n_chunks == 1)
    blk0 = row0 // rows
    mask, q_dec, k_dec, c_dec = _ret_decay_tables(c)
    cos, s_lo, s_hi = _rot_tables(pos)

    def const2(b, ci):
        return (0, 0)

    def row_specs(width, col):
        return [pl.BlockSpec((rows, width), lambda b, ci, j=j: (blk0 + (b * n_blk + j) * n_chunks + ci, col))
                for j in range(n_blk)]

    state_spec = pl.BlockSpec((n_seq, RET_HEADS, RET_DK, RET_DV), lambda b, ci: (b, 0, 0, 0))
    in_specs = row_specs(RET_QK, 0) + row_specs(RET_QK, 1) + row_specs(RET_W, 1) + row_specs(RET_W, 2) + [
        pl.BlockSpec((c, RET_QK), lambda b, ci: (ci, 0)),
        pl.BlockSpec((c, RET_QK), lambda b, ci: (ci, 0)),
        pl.BlockSpec((c, RET_QK), lambda b, ci: (ci, 0)),
        pl.BlockSpec((RET_HEADS, c, c), lambda b, ci: (0, 0, 0)),
        pl.BlockSpec((c, RET_W), const2),
        pl.BlockSpec((c, RET_QK), const2),
        pl.BlockSpec((1, RET_W), const2),
        state_spec,
    ]
    return pl.pallas_call(
        functools.partial(_ret_body, n_blk=n_blk, per_blk=per_blk, c=c, c_dec=c_dec),
        grid=(n_batch // n_seq, n_chunks),
        in_specs=in_specs,
        out_specs=[pl.BlockSpec((n_seq, c, RET_W), lambda b, ci: (b, ci, 0)), state_spec],
        out_shape=[jax.ShapeDtypeStruct((n_batch, t, RET_W), F32),
                   jax.ShapeDtypeStruct((n_batch, RET_HEADS, RET_DK, RET_DV), F32)],
        scratch_shapes=[pltpu.VMEM((n_seq, RET_HEADS, RET_DK, RET_DV), F32)],
        compiler_params=_params("parallel", "arbitrary"),
    )(*([proj] * (4 * n_blk)), cos, s_lo, s_hi, mask, q_dec, k_dec, ret_gn.reshape(1, RET_W).astype(F32), s0)


LORA_COLS = LORA_W + LORA_A + LORA_G


def _rwkv_lora_terms(lo, w0_ref, w2_ref, a0_ref, a2_ref, g2_ref):
    hw = lo[:, :LORA_W]
    ha = lo[:, LORA_W:LORA_W + LORA_A]
    hg = lo[:, LORA_W + LORA_A:]
    u = w0_ref[...] + jnp.dot(jnp.tanh(hw).astype(BF16), w2_ref[...], preferred_element_type=F32)
    decay = jnp.exp(-float(np.exp(-0.5)) * jax.nn.sigmoid(u))
    rate = jax.nn.sigmoid(a0_ref[...] + jnp.dot(ha.astype(BF16), a2_ref[...], preferred_element_type=F32))
    gate = jnp.dot(jax.nn.sigmoid(hg).astype(BF16), g2_ref[...], preferred_element_type=F32)
    return decay, rate, gate


def _rwkv_pre_body(r_ref, k_ref, v_ref, lo_ref, shift_ref, mu_ref, w0_ref, w2_ref, a0_ref, a2_ref, g2_ref,
                   ro_ref, ko_ref, vo_ref, wo_ref, ao_ref, go_ref, so_ref, prev_scr):
    ci = pl.program_id(1)
    c = r_ref.shape[0]

    @pl.when(ci == 0)
    def _():
        prev_scr[...] = shift_ref[0].astype(F32)

    first_row = lax.broadcasted_iota(jnp.int32, (c, 1), 0) == 0

    def shifted(x_ref, col0):
        w = x_ref.shape[1]
        x = x_ref[...].astype(F32)
        prev = jnp.where(first_row, prev_scr[:, col0:col0 + w], pltpu.roll(x, 1, 0))
        prev_scr[:, col0:col0 + w] = x[c - 1:c, :]
        return x + (prev - x) * mu_ref[:, col0:col0 + w]

    ro_ref[...] = shifted(r_ref, 0).T
    ko_ref[...] = shifted(k_ref, RWKV_W).T
    vo_ref[...] = shifted(v_ref, 2 * RWKV_W).T
    decay, rate, gate = _rwkv_lora_terms(shifted(lo_ref, 3 * RWKV_W), w0_ref, w2_ref, a0_ref, a2_ref, g2_ref)
    wo_ref[...] = decay.T
    ao_ref[...] = rate.T
    go_ref[...] = gate

    @pl.when(ci == pl.num_programs(1) - 1)
    def _():
        so_ref[0] = prev_scr[...]


def _rwkv_pre_args(s_shift, n_batch, mu, w0, w2, a0, a2, g2):
    return (s_shift.reshape(n_batch, 1, N_RWKV_COLS), mu.reshape(1, -1), w0.reshape(1, -1), w2.astype(BF16),
            a0.reshape(1, -1), a2.astype(BF16), g2.astype(BF16))


def _rwkv_pre_weight_specs(const):
    return [pl.BlockSpec((1, N_RWKV_COLS), const), pl.BlockSpec((1, RWKV_W), const),
            pl.BlockSpec((LORA_W, RWKV_W), const), pl.BlockSpec((1, RWKV_W), const),
            pl.BlockSpec((LORA_A, RWKV_W), const), pl.BlockSpec((LORA_G, RWKV_W), const)]


def _rwkv_pre(proj, s_shift, mu, w0, w2, a0, a2, g2, *, row0, n_batch, t, c):
    n_chunks = t // c
    assert t % c == 0 and row0 % c == 0
    blk0 = row0 // c
    col_r = N_RET_COLS // RWKV_W
    col_lo = (N_RET_COLS + 3 * RWKV_W) // LORA_COLS
    assert col_r * RWKV_W == N_RET_COLS and col_lo * LORA_COLS == N_RET_COLS + 3 * RWKV_W

    def row_map(col):
        return lambda b, ci: (blk0 + b * n_chunks + ci, col)

    state_spec = pl.BlockSpec((1, 1, N_RWKV_COLS), lambda b, ci: (b, 0, 0))
    in_specs = [pl.BlockSpec((c, RWKV_W), row_map(col_r)), pl.BlockSpec((c, RWKV_W), row_map(col_r + 1)),
                pl.BlockSpec((c, RWKV_W), row_map(col_r + 2)), pl.BlockSpec((c, LORA_COLS), row_map(col_lo)),
                state_spec] + _rwkv_pre_weight_specs(lambda b, ci: (0, 0))
    vec_spec = pl.BlockSpec((None, RWKV_W, c), lambda b, ci: (b, 0, ci))
    vec_shape = jax.ShapeDtypeStruct((n_batch, RWKV_W, t), F32)
    return pl.pallas_call(
        _rwkv_pre_body,
        grid=(n_batch, n_chunks),
        in_specs=in_specs,
        out_specs=[vec_spec] * 5 + [pl.BlockSpec((c, RWKV_W), lambda b, ci: (b * n_chunks + ci, 0)), state_spec],
        out_shape=[vec_shape] * 5 + [jax.ShapeDtypeStruct((n_batch * t, RWKV_W), F32),
                                     jax.ShapeDtypeStruct((n_batch, 1, N_RWKV_COLS), F32)],
        scratch_shapes=[pltpu.VMEM((1, N_RWKV_COLS), F32)],
        compiler_params=_params("parallel", "arbitrary"),
    )(proj, proj, proj, proj, *_rwkv_pre_args(s_shift, n_batch, mu, w0, w2, a0, a2, g2))


def _rwkv_pre_short_body(r_ref, k_ref, v_ref, lo_ref, shift_ref, mu_ref, w0_ref, w2_ref, a0_ref, a2_ref, g2_ref,
                         ro_ref, ko_ref, vo_ref, wo_ref, ao_ref, go_ref, so_ref, *, n_b, t):
    rows = n_b * t
    first_tok = (lax.broadcasted_iota(jnp.int32, (rows, 1), 0) & (t - 1)) == 0

    def shifted(x_ref, col0):
        w = x_ref.shape[1]
        x = x_ref[...].astype(F32)
        carried = jnp.broadcast_to(shift_ref[:, :, col0:col0 + w].astype(F32), (n_b, t, w)).reshape(rows, w)
        prev = jnp.where(first_tok, carried, pltpu.roll(x, 1, 0))
        so_ref[:, :, col0:col0 + w] = x.reshape(n_b, t, w)[:, t - 1:t, :]
        return x + (prev - x) * mu_ref[:, col0:col0 + w]

    def put(o_ref, x):
        by_tok = jnp.swapaxes(x.reshape(n_b, t, RWKV_W), 0, 1)
        for ti in range(t):
            feat = by_tok[ti].T.reshape(RWKV_N, RWKV_HEADS, n_b)
            o_ref[:, ti] = jnp.swapaxes(feat, 0, 1)

    put(ro_ref, shifted(r_ref, 0))
    put(ko_ref, shifted(k_ref, RWKV_W))
    put(vo_ref, shifted(v_ref, 2 * RWKV_W))
    decay, rate, gate = _rwkv_lora_terms(shifted(lo_ref, 3 * RWKV_W), w0_ref, w2_ref, a0_ref, a2_ref, g2_ref)
    put(wo_ref, decay)
    put(ao_ref, rate)
    go_ref[...] = gate


def _rwkv_pre_short(proj, s_shift, mu, w0, w2, a0, a2, g2, *, row0, n_batch, t):
    rows = n_batch * t
    assert row0 % rows == 0 and t & (t - 1) == 0
    blk0 = row0 // rows
    col_r = N_RET_COLS // RWKV_W
    col_lo = (N_RET_COLS + 3 * RWKV_W) // LORA_COLS
    state_spec = pl.BlockSpec((n_batch, 1, N_RWKV_COLS), lambda i: (0, 0, 0))
    in_specs = [pl.BlockSpec((rows, RWKV_W), lambda i: (blk0, col_r)),
                pl.BlockSpec((rows, RWKV_W), lambda i: (blk0, col_r + 1)),
                pl.BlockSpec((rows, RWKV_W), lambda i: (blk0, col_r + 2)),
                pl.BlockSpec((rows, LORA_COLS), lambda i: (blk0, col_lo)),
                state_spec] + _rwkv_pre_weight_specs(lambda i: (0, 0))
    vec_shape = (RWKV_HEADS, t, RWKV_N, n_batch)
    vec_spec = pl.BlockSpec(vec_shape, lambda i: (0, 0, 0, 0))
    return pl.pallas_call(
        functools.partial(_rwkv_pre_short_body, n_b=n_batch, t=t),
        grid=(1,),
        in_specs=in_specs,
        out_specs=[vec_spec] * 5 + [pl.BlockSpec((rows, RWKV_W), lambda i: (0, 0)), state_spec],
        out_shape=[jax.ShapeDtypeStruct(vec_shape, F32)] * 5 + [
            jax.ShapeDtypeStruct((rows, RWKV_W), F32), jax.ShapeDtypeStruct((n_batch, 1, N_RWKV_COLS), F32)],
        compiler_params=_params("arbitrary"),
    )(proj, proj, proj, proj, *_rwkv_pre_args(s_shift, n_batch, mu, w0, w2, a0, a2, g2))


def _scan_body(r_ref, k_ref, w_ref, a_ref, v_ref, kk_ref, ka_ref, rk_ref, lw_ref, lb_ref, s0_ref,
               y_ref, sout_ref, s_scr, a_scr, b_scr, km_scr, *, tc):
    ci = pl.program_id(1)
    vr = RWKV_N

    @pl.when(ci == 0)
    def _():
        s_scr[...] = s0_ref[...].astype(F32)

    def ksum(x):
        return jnp.sum(x, axis=-2, keepdims=True)

    def vsum(x):
        return jnp.sum(x, axis=1, keepdims=True)

    kr = k_ref[...]
    a = a_ref[...]
    kk = kr * kk_ref[...]
    kk = kk / jnp.maximum(jnp.sqrt(ksum(kk * kk)), 1e-12)
    a_scr[...] = -kk
    b_scr[...] = kk * a
    km_scr[...] = kr * (1.0 + (a - 1.0) * ka_ref[...])

    def token(t, carry):
        r, w, avec, bvec, kmod = r_ref[t], w_ref[t], a_scr[t], b_scr[t], km_scr[t]

        def value_row(i, c2):
            s = s_scr[i]
            sa = ksum(s * avec)
            s = s * w + sa * bvec + v_ref[t, pl.ds(i, 1), :] * kmod
            s_scr[i] = s
            y_ref[t, pl.ds(i, 1), :] = ksum(s * r)
            return c2

        lax.fori_loop(0, vr, value_row, 0, unroll=16)
        return carry

    lax.fori_loop(0, tc, token, 0)

    y = y_ref[...]
    d = y - vsum(y) * (1.0 / RWKV_N)
    var = vsum(d * d) * (1.0 / RWKV_N)
    bonus = ksum(r_ref[...] * km_scr[...] * rk_ref[...])
    y_ref[...] = d * lax.rsqrt(var + LNX_EPS) * lw_ref[...] + lb_ref[...] + bonus * v_ref[...]

    @pl.when(ci == pl.num_programs(1) - 1)
    def _():
        sout_ref[...] = s_scr[...]


def _rwkv_scan(r, k, w, a, v, k_k, k_a, r_k, lnx_w, lnx_b, s0, *, tc):
    n_grp, t, _, lanes = r.shape
    vr = v.shape[2]
    assert lanes == LANES and t % tc == 0 and vr == RWKV_N

    def tok_spec(rows):
        return pl.BlockSpec((None, tc, rows, LANES), lambda g, ci: (g, ci, 0, 0))

    def par_spec(rows):
        return pl.BlockSpec((None, rows, LANES), lambda g, ci: (g, 0, 0))

    st_spec = pl.BlockSpec((None, vr, RWKV_N, LANES), lambda g, ci: (g, 0, 0, 0))
    key_scratch = pltpu.VMEM((tc, RWKV_N, LANES), F32)
    return pl.pallas_call(
        functools.partial(_scan_body, tc=tc),
        grid=(n_grp, t // tc),
        in_specs=[tok_spec(RWKV_N)] * 4 + [tok_spec(vr)] + [par_spec(RWKV_N)] * 3 + [par_spec(vr)] * 2 + [st_spec],
        out_specs=[tok_spec(vr), st_spec],
        out_shape=[jax.ShapeDtypeStruct((n_grp, t, vr, LANES), F32),
                   jax.ShapeDtypeStruct((n_grp, vr, RWKV_N, LANES), F32)],
        scratch_shapes=[pltpu.VMEM((vr, RWKV_N, LANES), F32), key_scratch, key_scratch, key_scratch],
        compiler_params=_params("parallel", "arbitrary"),
    )(r, k, w, a, v, k_k, k_a, r_k, lnx_w, lnx_b, s0)


SCAN_TC = 128
SCAN_SUB = 64
MEM_CHUNKS = 4
DECAY_GROUP = 8


def _scan_prompt_body(r_ref, k_ref, w_ref, a_ref, v_ref, kk_ref, ka_ref, rk_ref, lw_ref, lb_ref, s0_ref,
                      memk_hbm, memv_hbm, y_ref, sout_ref, memk_out, memv_out,
                      s_scr, r_c, w_c, a_c, b_c, km_c, v_c, y_c, stash, mem_stage, mem_in_sem, mem_out_sem,
                      *, n_b, mem_seqs):
    ci = pl.program_id(0)
    vr = RWKV_N // 2
    ts = SCAN_SUB
    tile = RWKV_HEADS
    half_lanes = LANES // 2

    @pl.when(ci == 0)
    def _():
        s_scr[...] = s0_ref[...].astype(F32)

    chunk_seqs = mem_seqs // MEM_CHUNKS

    def mem_in(chunk, slot):
        copies = []
        for j in range(chunk_seqs):
            seq = (ci * MEM_CHUNKS + chunk) * chunk_seqs + j
            for h in range(MEM_HEADS):
                copies.append(pltpu.make_async_copy(memk_hbm.at[seq, :, h, :], mem_stage.at[slot, 0, j, h],
                                                    mem_in_sem.at[slot]))
                copies.append(pltpu.make_async_copy(memv_hbm.at[seq, :, h, :], mem_stage.at[slot, 1, j, h],
                                                    mem_in_sem.at[slot]))
        return copies

    def mem_out(chunk, slot):
        seqs = pl.ds((ci * MEM_CHUNKS + chunk) * chunk_seqs, chunk_seqs)
        return [pltpu.make_async_copy(mem_stage.at[slot, 0], memk_out.at[seqs], mem_out_sem.at[slot]),
                pltpu.make_async_copy(mem_stage.at[slot, 1], memv_out.at[seqs], mem_out_sem.at[slot])]

    def mem_phase(p):
        if 1 <= p <= MEM_CHUNKS:
            for cp in mem_in(p - 1, (p - 1) % 2):
                cp.wait()
            for cp in mem_out(p - 1, (p - 1) % 2):
                cp.start()
        if 2 <= p <= MEM_CHUNKS + 1:
            for cp in mem_out(p - 2, p % 2):
                cp.wait()
        if p < MEM_CHUNKS:
            for cp in mem_in(p, p % 2):
                cp.start()

    mem_phase(0)

    low = lax.broadcasted_iota(jnp.int32, (ts, LANES), 1) < half_lanes

    def feature_pair_rows(x_ref, base):
        tiles = [x_ref[b, pl.ds(base + f * tile, tile), :] for f in range(2) for b in range(n_b)]
        return jnp.concatenate(tiles, axis=0).T

    def key_to_chain(x_ref, dst, stash, t0):
        def group(g, c):
            rows = []
            for j in range(4):
                pair = g * 4 + j
                if t0 == 0:
                    full = feature_pair_rows(x_ref, pl.multiple_of(pair * 2 * tile, 2 * tile))
                    mt = full[:ts]
                    stash[pair] = full[ts:]
                else:
                    mt = stash[pair]
                sw = pltpu.roll(mt, half_lanes, 1)
                rows += [jnp.where(low, mt, sw), jnp.where(low, sw, mt)]
            dst[:, pl.ds(pl.multiple_of(g * 8, 8), 8), :] = jnp.swapaxes(jnp.stack(rows, axis=0), 0, 1)
            return c
        lax.fori_loop(0, RWKV_N // 8, group, 0, unroll=4)

    def value_to_chain(g, c):
        rows = [feature_pair_rows(v_ref, pl.multiple_of((g * 8 + j) * 2 * tile, 2 * tile)) for j in range(8)]
        v_c[:, pl.ds(pl.multiple_of(g * 8, 8), 8), :] = jnp.swapaxes(jnp.stack(rows, axis=0), 0, 1)
        return c
    lax.fori_loop(0, vr // 8, value_to_chain, 0, unroll=2)

    def ksum(x):
        return jnp.sum(x, axis=-2, keepdims=True)

    for t0 in range(0, SCAN_TC, ts):
        key_to_chain(r_ref, r_c, stash.at[0], t0)
        key_to_chain(w_ref, w_c, stash.at[1], t0)
        key_to_chain(k_ref, km_c, stash.at[2], t0)
        key_to_chain(a_ref, b_c, stash.at[3], t0)

        def prep(g8, c):
            toks = pl.ds(pl.multiple_of(g8 * 8, 8), 8)
            kr = km_c[toks]
            a = b_c[toks]
            kk = kr * kk_ref[...]
            kk = kk / jnp.maximum(jnp.sqrt(ksum(kk * kk)), 1e-12)
            w = w_c[toks]
            prods = [w[0]]
            for j in range(1, DECAY_GROUP):
                prods.append(prods[-1] * w[j])
            p = jnp.stack(prods, axis=0)
            p_prev = jnp.stack([jnp.ones_like(prods[0])] + prods[:-1], axis=0)
            inv_p = 1.0 / p
            a_c[toks] = -kk * p_prev
            b_c[toks] = kk * a * inv_p
            km_c[toks] = kr * (1.0 + (a - 1.0) * ka_ref[...]) * inv_p
            r_c[toks] = r_c[toks] * p
            w_c[toks] = p
            return c
        lax.fori_loop(0, ts // 8, prep, 0, unroll=2)
        mem_phase(1 + 2 * (t0 // ts))

        def token_group(g8, carry):
            def token(j, c1):
                t = g8 * DECAY_GROUP + j
                r, avec, bvec, kmod = r_c[t], a_c[t], b_c[t], km_c[t]

                def value_row(i, c2):
                    s = s_scr[i]
                    sa = ksum(s * avec)
                    s = s + sa * bvec + v_c[t0 + t, pl.ds(i, 1), :] * kmod
                    s_scr[i] = s
                    y_c[t0 + t, pl.ds(i, 1), :] = ksum(s * r)
                    return c2

                lax.fori_loop(0, vr, value_row, 0, unroll=True)
                return c1

            lax.fori_loop(0, DECAY_GROUP, token, 0)
            p_last = w_c[g8 * DECAY_GROUP + DECAY_GROUP - 1]

            def rescale(i, c2):
                s_scr[i] = s_scr[i] * p_last
                return c2
            lax.fori_loop(0, vr, rescale, 0, unroll=True)
            return carry

        lax.fori_loop(0, ts // DECAY_GROUP, token_group, 0)
        mem_phase(2 + 2 * (t0 // ts))

        def post(g8, c):
            ktoks = pl.ds(pl.multiple_of(g8 * 8, 8), 8)
            vtoks = pl.ds(pl.multiple_of(t0 + g8 * 8, 8), 8)

            def vsum(x):
                x2 = x.reshape(8 * vr, LANES)
                x2 = x2 + pltpu.roll(x2, half_lanes, 1)
                return jnp.sum(x2.reshape(8, vr, LANES), axis=1, keepdims=True)

            y = y_c[vtoks]
            d = y - vsum(y) * (1.0 / RWKV_N)
            var = vsum(d * d) * (1.0 / RWKV_N)
            bonus = ksum(r_c[ktoks] * km_c[ktoks] * rk_ref[...])
            y_c[vtoks] = d * lax.rsqrt(var + LNX_EPS) * lw_ref[...] + lb_ref[...] + bonus * v_c[vtoks]
            return c
        lax.fori_loop(0, ts // 8, post, 0, unroll=4)

    def value_from_chain(g, c):
        blk = jnp.swapaxes(y_c[:, pl.ds(pl.multiple_of(g * 8, 8), 8), :], 0, 1)
        for j in range(8):
            mt = blk[j].T
            base = pl.multiple_of((g * 8 + j) * 2 * tile, 2 * tile)
            for hf in range(2):
                for b in range(n_b):
                    row0 = (hf * n_b + b) * tile
                    y_ref[b, pl.ds(base + hf * tile, tile), :] = mt[row0:row0 + tile, :]
        return c
    lax.fori_loop(0, vr // 8, value_from_chain, 0, unroll=2)

    assert 2 * (SCAN_TC // ts) == MEM_CHUNKS
    mem_phase(MEM_CHUNKS + 1)

    @pl.when(ci == pl.num_programs(0) - 1)
    def _():
        sout_ref[...] = s_scr[...]


def _rwkv_scan_prompt(r, k, w, a, v, k_k, k_a, r_k, lnx_w, lnx_b, s0, mem_k, mem_v):
    n_b, _, t = r.shape
    vr = RWKV_N // 2
    n_steps = t // SCAN_TC
    n_mem_seq = mem_k.shape[0]
    assert t % SCAN_TC == 0 and 2 * n_b * RWKV_HEADS == LANES and n_mem_seq % (n_steps * MEM_CHUNKS) == 0
    chunk_seqs = n_mem_seq // (n_steps * MEM_CHUNKS)
    any_spec = pl.BlockSpec(memory_space=pl.ANY)
    mem_shape = jax.ShapeDtypeStruct((n_mem_seq, MEM_HEADS, N_MEM, MEM_DH), F32)
    tok_spec = pl.BlockSpec((n_b, RWKV_W, SCAN_TC), lambda ci: (0, 0, ci))
    key_par = pl.BlockSpec((RWKV_N, LANES), lambda ci: (0, 0))
    val_par = pl.BlockSpec((vr, LANES), lambda ci: (0, 0))
    st_spec = pl.BlockSpec((vr, RWKV_N, LANES), lambda ci: (0, 0, 0))
    key_chain = pltpu.VMEM((SCAN_SUB, RWKV_N, LANES), F32)
    val_chain = pltpu.VMEM((SCAN_TC, vr, LANES), F32)
    return pl.pallas_call(
        functools.partial(_scan_prompt_body, n_b=n_b, mem_seqs=n_mem_seq // n_steps),
        grid=(n_steps,),
        in_specs=[tok_spec] * 5 + [key_par] * 3 + [val_par] * 2 + [st_spec, any_spec, any_spec],
        out_specs=[tok_spec, st_spec, any_spec, any_spec],
        out_shape=[jax.ShapeDtypeStruct((n_b, RWKV_W, t), F32),
                   jax.ShapeDtypeStruct((vr, RWKV_N, LANES), F32), mem_shape, mem_shape],
        scratch_shapes=([pltpu.VMEM((vr, RWKV_N, LANES), F32)] + [key_chain] * 5 + [val_chain] * 2
                        + [pltpu.VMEM((4, RWKV_N // 2, SCAN_TC - SCAN_SUB, LANES), F32),
                           pltpu.VMEM((2, 2, chunk_seqs, MEM_HEADS, N_MEM, MEM_DH), F32),
                           pltpu.SemaphoreType.DMA((2,)), pltpu.SemaphoreType.DMA((2,))]),
        compiler_params=_params("arbitrary"),
    )(r, k, w, a, v, k_k, k_a, r_k, lnx_w, lnx_b, s0, mem_k, mem_v)


def _merge_body(*refs, part_counts):
    n = len(part_counts)
    x_refs, oret_refs, g_refs = refs[:n], refs[n:2 * n], refs[2 * n:3 * n]
    yt_ref, ys_ref, wt_ref, wb_ref, gq_ref, wq_ref, o_ref, q_ref = refs[3 * n:]
    x = _read_row_parts(x_refs, part_counts)
    y = jnp.where(pl.program_id(0) >= part_counts[0], ys_ref[...], yt_ref[...].T)
    yb = (y * _read_row_parts(g_refs, part_counts)).astype(BF16)
    ob = _read_row_parts(oret_refs, part_counts).astype(BF16)
    n_chunk = MXU_COLS
    for j in range(0, D_MODEL, n_chunk):
        acc = jnp.dot(ob, wt_ref[:, j:j + n_chunk], preferred_element_type=F32)
        acc = acc + jnp.dot(yb, wb_ref[:, j:j + n_chunk], preferred_element_type=F32)
        o_ref[:, j:j + n_chunk] = x[:, j:j + n_chunk] + acc
    h = o_ref[...]
    hb = (h * lax.rsqrt(jnp.mean(h * h, axis=-1, keepdims=True) + EPS) * gq_ref[...]).astype(BF16)
    for j in range(0, D_MODEL, n_chunk):
        q_ref[:, j:j + n_chunk] = jnp.dot(hb, wq_ref[:, j:j + n_chunk], preferred_element_type=F32)


def _merge(x_parts, oret_parts, g_parts, y_first_t, y_second, w_ret, w_rwkv, gain_q, w_q):
    m = sum(part.shape[0] for part in x_parts)
    in_specs, part_counts = [], None
    for parts in (x_parts, oret_parts, g_parts):
        specs, part_counts = _row_part_specs(parts, TM)
        in_specs += specs
    assert len(part_counts) == 2
    n_first = part_counts[0]
    tiles = y_first_t.shape[2] // TM
    assert y_first_t.shape[0] * tiles == n_first
    yt_spec = pl.BlockSpec((None, RWKV_W, TM),
                           lambda i: (jnp.minimum(i, n_first - 1) // tiles, 0, jnp.minimum(i, n_first - 1) % tiles))
    ys_spec = pl.BlockSpec((TM, RWKV_W), lambda i: (jnp.clip(i - n_first, 0, part_counts[1] - 1), 0))
    wspec = pl.BlockSpec((RET_W, D_MODEL), lambda i: (0, 0))
    row_spec = pl.BlockSpec((TM, D_MODEL), lambda i: (i, 0))
    row_shape = jax.ShapeDtypeStruct((m, D_MODEL), F32)
    return pl.pallas_call(
        functools.partial(_merge_body, part_counts=part_counts),
        grid=(m // TM,),
        in_specs=in_specs + [yt_spec, ys_spec, wspec, wspec, pl.BlockSpec((1, D_MODEL), lambda i: (0, 0)),
                             pl.BlockSpec((D_MODEL, D_MODEL), lambda i: (0, 0))],
        out_specs=[row_spec, row_spec],
        out_shape=[row_shape, row_shape],
        compiler_params=_params("parallel"),
    )(*x_parts, *oret_parts, *g_parts, y_first_t, y_second, w_ret.astype(BF16), w_rwkv.astype(BF16),
      gain_q.reshape(1, D_MODEL), w_q.astype(BF16))


def _mem_kv_body(x_ref, g_ref, wk_ref, wv_ref, k_ref, v_ref, kh_ref, vh_ref, *, n_seq):
    x = x_ref[...].astype(F32)
    xb = (x * lax.rsqrt(jnp.mean(x * x, axis=-1, keepdims=True) + EPS) * g_ref[...]).astype(BF16)
    for w_ref, o_ref, oh_ref in ((wk_ref, k_ref, kh_ref), (wv_ref, v_ref, vh_ref)):
        for h in range(MEM_HEADS):
            acc = jnp.dot(xb, w_ref[:, h * MEM_DH:(h + 1) * MEM_DH], preferred_element_type=F32)
            o_ref[:, h, :] = acc
            for s in range(n_seq):
                oh_ref[s, h] = acc[s * N_MEM:(s + 1) * N_MEM]


def _mem_kv(mem, gain, w_k, w_v):
    n_b = mem.shape[0]
    n_seq = TM // N_MEM
    assert n_seq * N_MEM == TM and n_b % n_seq == 0
    wspec = pl.BlockSpec((D_MODEL, D_MODEL), lambda i: (0, 0))
    tok_spec = pl.BlockSpec((TM, MEM_HEADS, MEM_DH), lambda i: (i, 0, 0))
    head_spec = pl.BlockSpec((n_seq, MEM_HEADS, N_MEM, MEM_DH), lambda i: (i, 0, 0, 0))
    tok_shape = jax.ShapeDtypeStruct((n_b * N_MEM, MEM_HEADS, MEM_DH), F32)
    head_shape = jax.ShapeDtypeStruct((n_b, MEM_HEADS, N_MEM, MEM_DH), F32)
    return pl.pallas_call(
        functools.partial(_mem_kv_body, n_seq=n_seq),
        grid=(n_b // n_seq,),
        in_specs=[pl.BlockSpec((TM, D_MODEL), lambda i: (i, 0)), pl.BlockSpec((1, D_MODEL), lambda i: (0, 0)),
                  wspec, wspec],
        out_specs=[tok_spec, tok_spec, head_spec, head_spec],
        out_shape=[tok_shape, tok_shape, head_shape, head_shape],
        compiler_params=_params("parallel"),
    )(mem.reshape(n_b * N_MEM, D_MODEL), gain.reshape(1, D_MODEL), w_k.astype(BF16), w_v.astype(BF16))


def _attn_body(q_ref, k_ref, v_ref, o_ref, *, n_seq, tq):
    nt = (((1,), (1,)), ((), ()))
    for g in range(n_seq):
        rows = slice(g * tq, (g + 1) * tq)
        q = q_ref[rows, :].astype(BF16)
        s = lax.dot_general(q, k_ref[g].astype(BF16), nt, preferred_element_type=F32) * (MEM_DH ** -0.5)
        p = jnp.exp(s - jnp.max(s, axis=-1, keepdims=True))
        l = jnp.sum(p, axis=-1, keepdims=True)
        o = jnp.dot(p.astype(BF16), v_ref[g].astype(BF16), preferred_element_type=F32)
        o_ref[rows, :] = o / l


def _attention(q, mem_k, mem_v, *, row0, n_batch, t, n_seq, tq):
    q_tiles = t // tq
    rows = n_seq * tq
    assert t % tq == 0 and n_batch % n_seq == 0 and row0 % rows == 0 and (n_seq == 1 or q_tiles == 1)
    blk0 = row0 // rows
    kv_spec = pl.BlockSpec((n_seq, None, N_MEM, MEM_DH), lambda b, h, qi: (b, h, 0, 0))
    return pl.pallas_call(
        functools.partial(_attn_body, n_seq=n_seq, tq=tq),
        grid=(n_batch // n_seq, MEM_HEADS, q_tiles),
        in_specs=[pl.BlockSpec((rows, MEM_DH), lambda b, h, qi: (blk0 + b * q_tiles + qi, h)), kv_spec, kv_spec],
        out_specs=pl.BlockSpec((rows, MEM_DH), lambda b, h, qi: (b * q_tiles + qi, h)),
        out_shape=jax.ShapeDtypeStruct((n_batch * t, D_MODEL), F32),
        compiler_params=_params("parallel", "parallel", "parallel"),
    )(q, mem_k, mem_v)


ROW_TILE = (D_MODEL // LANES, LANES)


def _rows_to_tiles(x):
    chunks = [x[:, j * LANES:(j + 1) * LANES] for j in range(ROW_TILE[0])]
    return jnp.swapaxes(jnp.stack(chunks, axis=0), 0, 1)


def _tiles_to_rows(x):
    chunks = jnp.swapaxes(x, 0, 1)
    return jnp.concatenate([chunks[j] for j in range(ROW_TILE[0])], axis=1)


def _attn_out_body(*refs, part_counts):
    n = len(part_counts)
    w_ref, res_ref, o_ref = refs[n:]
    ab = _read_row_parts(refs[:n], part_counts).astype(BF16)
    n_chunk = MXU_COLS
    for j in range(0, D_MODEL, n_chunk):
        o_ref[:, j:j + n_chunk] = res_ref[:, j:j + n_chunk] + jnp.dot(
            ab, w_ref[:, j:j + n_chunk], preferred_element_type=F32)


def _attn_out(att_parts, w_mo, residual):
    m = residual.shape[0]
    att_specs, part_counts = _row_part_specs(att_parts, TM)
    row = pl.BlockSpec((TM, D_MODEL), lambda i: (i, 0))
    return pl.pallas_call(
        functools.partial(_attn_out_body, part_counts=part_counts),
        grid=(m // TM,),
        in_specs=att_specs + [pl.BlockSpec((D_MODEL, D_MODEL), lambda i: (0, 0)), row],
        out_specs=row,
        out_shape=jax.ShapeDtypeStruct((m, D_MODEL), F32),
        compiler_params=_params("parallel"),
    )(*att_parts, w_mo.astype(BF16), residual)


def _router_body(h_ref, g_ref, w_ref, b_ref, hn_ref, ids_ref, comb_ref):
    x = h_ref[...]
    hn = x * lax.rsqrt(jnp.mean(x * x, axis=-1, keepdims=True) + EPS) * g_ref[...]
    hn_ref[...] = _rows_to_tiles(hn)
    logits = jnp.dot(hn, w_ref[...], precision=lax.Precision.HIGHEST, preferred_element_type=F32) + b_ref[...]
    lane = lax.broadcasted_iota(jnp.int32, logits.shape, 1).astype(F32)
    neg = -jnp.inf

    def first_argmax(vals):
        m = jnp.max(vals, axis=-1, keepdims=True)
        return m, jnp.min(jnp.where(vals == m, lane, float(LANES)), axis=-1, keepdims=True)

    gl = jnp.where(lane < N_GROUPS, logits, neg)
    gmax, gsel = first_argmax(gl)
    pg_sel = 1.0 / jnp.sum(jnp.exp(gl - gmax), axis=-1, keepdims=True)
    e0 = N_GROUPS + gsel * EXP_PER_GROUP
    el = jnp.where((lane >= e0) & (lane < e0 + EXP_PER_GROUP), logits, neg)
    m1, i1 = first_argmax(el)
    m2, i2 = first_argmax(jnp.where(lane == i1, neg, el))
    e21 = jnp.exp(m2 - m1)
    c1 = pg_sel / (1.0 + e21)
    c2 = c1 * e21
    ids = jnp.where(lane == 0, i1 - N_GROUPS, jnp.where(lane == 1, i2 - N_GROUPS, 0.0))
    ids_ref[...] = ids.astype(jnp.int32)
    comb_ref[...] = jnp.where(lane == 0, c1, jnp.where(lane == 1, c2, 0.0))


def _router(h, g_ffn, w_gr, b_gr, w_er, b_er):
    m = h.shape[0]
    pad = LANES - N_GROUPS - N_EXPERTS
    w = jnp.concatenate([w_gr, w_er, jnp.zeros((D_MODEL, pad), F32)], axis=1)
    b = jnp.concatenate([b_gr, b_er, jnp.zeros((pad,), F32)]).reshape(1, LANES)
    row = lambda n: pl.BlockSpec((TM, n), lambda i: (i, 0))
    return pl.pallas_call(
        _router_body,
        grid=(m // TM,),
        in_specs=[row(D_MODEL), pl.BlockSpec((1, D_MODEL), lambda i: (0, 0)),
                  pl.BlockSpec((D_MODEL, LANES), lambda i: (0, 0)), pl.BlockSpec((1, LANES), lambda i: (0, 0))],
        out_specs=[pl.BlockSpec((TM,) + ROW_TILE, lambda i: (i, 0, 0)), row(LANES), row(LANES)],
        out_shape=[jax.ShapeDtypeStruct((m,) + ROW_TILE, F32), jax.ShapeDtypeStruct((m, LANES), jnp.int32),
                   jax.ShapeDtypeStruct((m, LANES), F32)],
        compiler_params=_params("parallel"),
    )(h, g_ffn.reshape(1, D_MODEL), w, b)


def _dispatch_body(pad_start_ref, pad_count_ref, dest_ref, hn_ref, sorted_out, zeros, sem, *, n_pad_rows):
    @pl.when(pl.program_id(0) == 0)
    def _():
        zeros[...] = jnp.zeros(zeros.shape, F32)

        def fill_tail(e, c):
            count = pad_count_ref[e]
            row = pad_start_ref[e]
            bit = MOE_ROWS // 2
            while bit:
                @pl.when((count & bit) != 0)
                def _(row=row, bit=bit):
                    pltpu.make_async_copy(zeros.at[pl.ds(0, bit)], sorted_out.at[pl.ds(row, bit)],
                                          sem.at[TOP_K]).start()
                row = row + (count & bit)
                bit //= 2
            return c
        lax.fori_loop(0, N_EXPERTS, fill_tail, 0)

        def fill_block(j, c):
            pltpu.make_async_copy(zeros, sorted_out.at[pl.ds(pad_start_ref[N_EXPERTS] + j * MOE_ROWS, MOE_ROWS)],
                                  sem.at[TOP_K]).start()
            return c
        lax.fori_loop(0, pad_count_ref[N_EXPERTS] // MOE_ROWS, fill_block, 0)

    def issue(r, c):
        for k in range(TOP_K):
            pltpu.make_async_copy(hn_ref.at[r], sorted_out.at[dest_ref[0, k, r]], sem.at[k]).start(priority=k)
        return c
    lax.fori_loop(0, TM, issue, 0, unroll=8)
    for k in range(TOP_K):
        pltpu.make_async_copy(hn_ref, sorted_out.at[pl.ds(0, TM)], sem.at[k]).wait()

    @pl.when(pl.program_id(0) == pl.num_programs(0) - 1)
    def _():
        def drain(j, c):
            pltpu.make_async_copy(zeros, sorted_out.at[pl.ds(0, MOE_ROWS)], sem.at[TOP_K]).wait()
            return c
        lax.fori_loop(0, n_pad_rows // MOE_ROWS, drain, 0)


def _dispatch(hn, dest, pad_start, pad_count, n_sorted):
    n_tok = hn.shape[0]
    n_pad_rows = n_sorted - TOP_K * n_tok
    assert n_pad_rows % MOE_ROWS == 0
    grid_spec = pltpu.PrefetchScalarGridSpec(
        num_scalar_prefetch=2,
        grid=(n_tok // TM,),
        in_specs=[pl.BlockSpec((1, TOP_K, TM), lambda i, ps, pc: (i, 0, 0), memory_space=pltpu.SMEM),
                  pl.BlockSpec((TM,) + ROW_TILE, lambda i, ps, pc: (i, 0, 0))],
        out_specs=pl.BlockSpec(memory_space=pl.ANY),
        scratch_shapes=[pltpu.VMEM((MOE_ROWS,) + ROW_TILE, F32), pltpu.SemaphoreType.DMA((TOP_K + 1,))],
    )
    return pl.pallas_call(
        functools.partial(_dispatch_body, n_pad_rows=n_pad_rows),
        grid_spec=grid_spec,
        out_shape=jax.ShapeDtypeStruct((n_sorted,) + ROW_TILE, F32),
        compiler_params=_params("arbitrary"),
    )(pad_start, pad_count, dest, hn)


def _expert_body(blk_e_ref, x_ref, wg_ref, wu_ref, wd_ref, o_ref):
    del blk_e_ref
    x = _tiles_to_rows(x_ref[...]).astype(BF16)
    hg = jnp.dot(x, wg_ref[0].astype(BF16), preferred_element_type=F32)
    hu = jnp.dot(x, wu_ref[0].astype(BF16), preferred_element_type=F32)
    act = (hg * jax.nn.sigmoid(hg) * hu).astype(BF16)
    o_ref[...] = _rows_to_tiles(jnp.dot(act, wd_ref[0].astype(BF16), preferred_element_type=F32))


def _experts(x_sorted, blk_e, w_gate, w_up, w_down):
    n_blocks = blk_e.shape[0]
    row_spec = pl.BlockSpec((MOE_ROWS,) + ROW_TILE, lambda i, be: (i, 0, 0))
    grid_spec = pltpu.PrefetchScalarGridSpec(
        num_scalar_prefetch=1,
        grid=(n_blocks,),
        in_specs=[
            row_spec,
            pl.BlockSpec((1, D_MODEL, D_EXPERT), lambda i, be: (be[i], 0, 0)),
            pl.BlockSpec((1, D_MODEL, D_EXPERT), lambda i, be: (be[i], 0, 0)),
            pl.BlockSpec((1, D_EXPERT, D_MODEL), lambda i, be: (be[i], 0, 0)),
        ],
        out_specs=row_spec,
    )
    return pl.pallas_call(
        _expert_body,
        grid_spec=grid_spec,
        out_shape=jax.ShapeDtypeStruct(x_sorted.shape, F32),
        compiler_params=_params("arbitrary"),
    )(blk_e, x_sorted, w_gate, w_up, w_down)


def _route_plan(ids):
    n_tok = ids.shape[0]
    n_pairs = ids.size
    n_blocks = -(-(n_pairs + N_EXPERTS * (MOE_ROWS - 1)) // MOE_ROWS)
    flat_e = ids.reshape(n_pairs)
    onehot = (flat_e[:, None] == jnp.arange(N_EXPERTS, dtype=jnp.int32)[None, :]).astype(jnp.int32)
    csum = jnp.cumsum(onehot, axis=0)
    rank = jnp.sum(onehot * csum, axis=1) - 1
    counts = csum[-1]
    pcounts = (counts + MOE_ROWS - 1) // MOE_ROWS * MOE_ROWS
    pends = jnp.cumsum(pcounts)
    pstarts = pends - pcounts
    dest = jnp.sum(onehot * pstarts[None, :], axis=1) + rank
    block_start = jnp.arange(n_blocks, dtype=jnp.int32) * MOE_ROWS
    blk_e = jnp.minimum(jnp.sum((block_start[:, None] >= pends[None, :]).astype(jnp.int32), axis=1),
                        N_EXPERTS - 1).astype(jnp.int32)
    dest = dest.astype(jnp.int32).reshape(n_tok // TM, TM, TOP_K).transpose(0, 2, 1)
    n_sorted = n_blocks * MOE_ROWS
    pad_start = jnp.concatenate([pstarts + counts, pends[-1:]]).astype(jnp.int32)
    pad_count = jnp.concatenate([pcounts - counts, n_sorted - pends[-1:]]).astype(jnp.int32)
    return blk_e, dest, pad_start, pad_count, n_sorted


def _final_body(dest_ref, dest_next_ref, h_ref, comb_ref, g_ref, y_hbm, o_first, o_second, ybuf, sem, *, n_first):
    i = pl.program_id(0)
    n = pl.num_programs(0)
    slot = i % 2

    def start_gather(ref, sl):
        def issue(r, c):
            for k in range(TOP_K):
                pltpu.make_async_copy(y_hbm.at[ref[0, k, r]], ybuf.at[sl, k, r], sem.at[sl]).start(priority=k)
            return c
        lax.fori_loop(0, TM, issue, 0, unroll=8)

    @pl.when(i == 0)
    def _():
        start_gather(dest_ref, 0)

    @pl.when(i + 1 < n)
    def _():
        start_gather(dest_next_ref, 1 - slot)

    for k in range(TOP_K):
        pltpu.make_async_copy(y_hbm.at[pl.ds(0, TM)], ybuf.at[slot, k], sem.at[slot]).wait()

    first, second = _tiles_to_rows(ybuf[slot, 0]), _tiles_to_rows(ybuf[slot, 1])
    x = h_ref[...] + (first * comb_ref[:, 0:1] + second * comb_ref[:, 1:2])
    out = x * lax.rsqrt(jnp.mean(x * x, axis=-1, keepdims=True) + EPS) * g_ref[...]

    @pl.when(i < n_first)
    def _():
        o_first[...] = out

    @pl.when(i >= n_first)
    def _():
        o_second[...] = out


def _final(h, y_sorted, dest, comb, g_final, *, n_first_rows):
    n_rows = h.shape[0]
    assert n_rows % TM == 0 and n_first_rows % TM == 0 and 0 < n_first_rows < n_rows
    n_steps, n_first = n_rows // TM, n_first_rows // TM
    row = lambda n: pl.BlockSpec((TM, n), lambda i: (i, 0))
    dest_spec = lambda f: pl.BlockSpec((1, TOP_K, TM), f, memory_space=pltpu.SMEM)
    return pl.pallas_call(
        functools.partial(_final_body, n_first=n_first),
        grid=(n_steps,),
        in_specs=[dest_spec(lambda i: (i, 0, 0)),
                  dest_spec(lambda i: (jnp.minimum(i + 1, n_steps - 1), 0, 0)),
                  row(D_MODEL), row(LANES), pl.BlockSpec((1, D_MODEL), lambda i: (0, 0)),
                  pl.BlockSpec(memory_space=pl.ANY)],
        out_specs=[pl.BlockSpec((TM, D_MODEL), lambda i: (jnp.minimum(i, n_first - 1), 0)),
                   pl.BlockSpec((TM, D_MODEL), lambda i: (jnp.maximum(i - n_first, 0), 0))],
        out_shape=[jax.ShapeDtypeStruct((n_first_rows, D_MODEL), F32),
                   jax.ShapeDtypeStruct((n_rows - n_first_rows, D_MODEL), F32)],
        scratch_shapes=[pltpu.VMEM((2, TOP_K, TM) + ROW_TILE, F32), pltpu.SemaphoreType.DMA((2,))],
        compiler_params=_params("arbitrary"),
    )(dest, dest, h, comb, g_final.reshape(1, D_MODEL), y_sorted)


def _reorder_last(x, shape, order):
    lead = x.shape[:-1]
    n = len(lead)
    y = x.reshape(lead + shape).transpose(tuple(range(n)) + tuple(n + o for o in order))
    return y.reshape(lead + (x.shape[-1],))


HALF_N = RWKV_N // 2


def _key_major(x):
    return _reorder_last(x, (RWKV_HEADS, RWKV_N), (1, 0))


def _key_major_inv(x):
    return _reorder_last(x, (RWKV_N, RWKV_HEADS), (1, 0))


def _value_major(x):
    return _reorder_last(x, (RWKV_HEADS, 2, HALF_N), (2, 1, 0))


def _value_major_inv(x):
    return _reorder_last(x, (HALF_N, 2, RWKV_HEADS), (2, 1, 0))


def _rwkv_cols(x, key_fn, value_fn):
    return jnp.concatenate([key_fn(x[..., :RWKV_W]), key_fn(x[..., RWKV_W:2 * RWKV_W]),
                            value_fn(x[..., 2 * RWKV_W:3 * RWKV_W]), x[..., 3 * RWKV_W:]], axis=-1)


def kernel(x_prompt, x_sample, mem_prompt, state_ret, state_rwkv, state_shift, cache_mem_k, cache_mem_v,
           g_mix, w_in, ret_gn, rwkv_mu, rwkv_w0, rwkv_w2, rwkv_a0, rwkv_a2, rwkv_g2, rwkv_k_k, rwkv_k_a,
           rwkv_r_k, rwkv_lnx_w, rwkv_lnx_b, w_out, g_mem_q, g_mem_kv, w_mq, w_mk, w_mv, w_mo, g_ffn,
           w_group_router, b_group_router, w_expert_router, b_expert_router, w_e_gate, w_e_up, w_e_down,
           g_final):
    assert w_in.shape[0] == 1, "single-layer decoder"
    bp, tp, d = x_prompt.shape
    bs, ts, _ = x_sample.shape
    np_tok, ns_tok = bp * tp, bs * ts
    assert d == D_MODEL and bp * RWKV_HEADS * 2 == LANES and bs == LANES
    l = 0
    x_parts = [x_prompt.reshape(np_tok, d), x_sample.reshape(ns_tok, d)]

    w_in_l = jnp.concatenate([w_in[l][:, :N_RET_COLS], _rwkv_cols(w_in[l][:, N_RET_COLS:], _key_major, _value_major)],
                             axis=1)
    proj = _input_projection(x_parts, g_mix[l], w_in_l)

    pos_p = np.arange(tp)
    pos_s = PAST_LEN + np.arange(ts)
    zero_ret = jnp.zeros((bp, RET_HEADS, RET_DK, RET_DV), F32)
    oret_p, sret_p = _retention(proj, zero_ret, ret_gn[l], pos_p, row0=0, n_batch=bp, t=tp,
                                n_blk=RET_SEQS_LONG, per_blk=1)
    oret_s, sret_s = _retention(proj, state_ret[l], ret_gn[l], pos_s, row0=np_tok, n_batch=bs, t=ts,
                                n_blk=1, per_blk=RET_SEQS_SHORT)
    oret_p, oret_s = oret_p.reshape(np_tok, RET_W), oret_s.reshape(ns_tok, RET_W)

    pre_w = (_rwkv_cols(rwkv_mu[l], _key_major, _value_major), _key_major(rwkv_w0[l]), _key_major(rwkv_w2[l]),
             _key_major(rwkv_a0[l]), _key_major(rwkv_a2[l]), _value_major(rwkv_g2[l]))
    zero_shift = jnp.zeros((bp, N_RWKV_COLS), F32)
    shift_in = _rwkv_cols(state_shift[l], _key_major, _value_major)
    r_p, k_p, v_p, w_p, a_p, gate_p, shift_p = _rwkv_pre(proj, zero_shift, *pre_w, row0=0, n_batch=bp, t=tp,
                                                          c=PRE_CHUNK)
    r_s, k_s, v_s, w_s, a_s, gate_s, shift_s = _rwkv_pre_short(proj, shift_in, *pre_w, row0=np_tok, n_batch=bs,
                                                                t=ts)

    kvec = lambda v: v.reshape(RWKV_HEADS, RWKV_N)
    key_par = lambda v: jnp.broadcast_to(kvec(v).T[:, None, None, :], (RWKV_N, 2, bp, RWKV_HEADS)).reshape(
        RWKV_N, LANES)

    val_par = lambda v: jnp.broadcast_to(
        v.reshape(RWKV_HEADS, 2, HALF_N).transpose(2, 1, 0)[:, :, None, :],
        (HALF_N, 2, bp, RWKV_HEADS)).reshape(HALF_N, LANES)
    mem_shape = (N_MEM, MEM_HEADS, MEM_DH)
    y_p, srw_p, cache_k_heads, cache_v_heads = _rwkv_scan_prompt(
        r_p, k_p, w_p, a_p, v_p, key_par(rwkv_k_k[l]), key_par(rwkv_k_a[l]), key_par(rwkv_r_k[l]),
        val_par(rwkv_lnx_w[l]), val_par(rwkv_lnx_b[l]), jnp.zeros((HALF_N, RWKV_N, LANES), F32),
        cache_mem_k.reshape(bs, *mem_shape), cache_mem_v.reshape(bs, *mem_shape))
    srw_p = srw_p.reshape(HALF_N, RWKV_N, 2, bp, RWKV_HEADS).transpose(3, 4, 2, 0, 1).reshape(
        bp, RWKV_HEADS, RWKV_N, RWKV_N)
    head_par = lambda v: jnp.broadcast_to(kvec(v)[:, :, None], (RWKV_HEADS, RWKV_N, LANES))
    val_rows = lambda v: v.reshape(RWKV_HEADS, 2, HALF_N).transpose(0, 2, 1).reshape(RWKV_HEADS * RWKV_N)
    state_s = state_rwkv[l].astype(F32).reshape(bs, RWKV_HEADS, 2, HALF_N, RWKV_N).transpose(1, 3, 2, 4, 0)
    y_s, srw_s = _rwkv_scan(
        r_s, k_s, w_s, a_s, v_s, head_par(rwkv_k_k[l]), head_par(rwkv_k_a[l]), head_par(rwkv_r_k[l]),
        head_par(val_rows(rwkv_lnx_w[l])), head_par(val_rows(rwkv_lnx_b[l])),
        state_s.reshape(RWKV_HEADS, RWKV_N, RWKV_N, bs), tc=ts)
    y_s = y_s.transpose(3, 1, 2, 0).reshape(ns_tok, RWKV_W)
    srw_s = srw_s.reshape(RWKV_HEADS, HALF_N, 2, RWKV_N, bs).transpose(4, 0, 2, 1, 3).reshape(
        bs, RWKV_HEADS, RWKV_N, RWKV_N)

    w_rwkv_out = _value_major(w_out[l][RET_W:].T).T
    h, q = _merge(x_parts, [oret_p, oret_s], [gate_p, gate_s], y_p, y_s, w_out[l][:RET_W], w_rwkv_out,
                  g_mem_q[l], w_mq[l])

    mk, mv, mk_heads, mv_heads = _mem_kv(mem_prompt, g_mem_kv[l], w_mk[l], w_mv[l])
    att_p = _attention(q, mk_heads, mv_heads, row0=0, n_batch=bp, t=tp, n_seq=1, tq=tp)
    att_s = _attention(q, cache_k_heads, cache_v_heads, row0=np_tok, n_batch=bs, t=ts,
                       n_seq=ATTN_SEQS_SHORT, tq=ts)

    h = _attn_out([att_p, att_s], w_mo[l], h)

    hn, ids, comb = _router(h, g_ffn[l], w_group_router[l], b_group_router[l], w_expert_router[l],
                            b_expert_router[l])
    blk_e, dest, pad_start, pad_count, n_sorted = _route_plan(ids[:, :TOP_K])
    x_sorted = _dispatch(hn, dest, pad_start, pad_count, n_sorted)
    y_sorted = _experts(x_sorted, blk_e, w_e_gate[l], w_e_up[l], w_e_down[l])
    y_prompt, y_sample = _final(h, y_sorted, dest, comb, g_final, n_first_rows=np_tok)
    y_prompt = y_prompt.reshape(bp, tp, d)
    y_sample = y_sample.reshape(bs, ts, d)

    shift_p = _rwkv_cols(shift_p.reshape(bp, N_RWKV_COLS), _key_major_inv, _value_major_inv)
    shift_s = _rwkv_cols(shift_s.reshape(bs, N_RWKV_COLS), _key_major_inv, _value_major_inv)
    return (y_prompt, y_sample, sret_p[None], srw_p[None], shift_p[None],
            mk.reshape(1, bp, *mem_shape), mv.reshape(1, bp, *mem_shape),
            sret_s[None], srw_s[None], shift_s[None])
```
